```python
import math
import jax, jax.numpy as jnp
from jax import lax
import numpy as np

D_MODEL = 2048
BATCH = 8
SEQ = 4096
DEPTH = 2

NORM_EPS = 1e-6
CONV_WIDTH = 4

D_FF = ((8 * D_MODEL + 3 * 256 - 1) // (3 * 256)) * 256

GLA_DV = D_MODEL // 2
GLA_HEADS = 4
GLA_HEAD_V = GLA_DV // GLA_HEADS
GLA_DK = GLA_DV // 2
GLA_HEAD_K = GLA_DK // GLA_HEADS
GLA_GATE_RANK = 16
GLA_GATE_NORM = 16.0
GLA_CHUNK = 64

LRU_WIDTH = D_MODEL // 2
LRU_BLOCKS = 8
LRU_BLOCK = LRU_WIDTH // LRU_BLOCKS
LRU_C = 8.0

EVEN_IN = 2 * GLA_DK + 2 * GLA_DV + GLA_GATE_RANK + 2 * LRU_WIDTH
EVEN_MIX = GLA_DV + LRU_WIDTH

SSD_D_INNER = 2 * D_MODEL
SSD_HEAD_DIM = 64
SSD_HEADS = SSD_D_INNER // SSD_HEAD_DIM
SSD_GROUPS = 8
SSD_STATE = 128
SSD_CHUNK = 64
SSD_CONV_DIM = SSD_D_INNER + 2 * SSD_GROUPS * SSD_STATE
ODD_IN = SSD_D_INNER + SSD_CONV_DIM + SSD_HEADS

kernel_name = "hybrid_gla_rglru_mamba2_swiglu"


def rmsnorm(x, w):
    xf = x.astype(jnp.float32)
    y = xf * lax.rsqrt(jnp.mean(xf * xf, axis=-1, keepdims=True) + NORM_EPS)
    return (y * w.astype(jnp.float32)).astype(x.dtype)


def causal_depthwise_conv(x, w, b):
    K, C = w.shape
    y = lax.conv_general_dilated(
        x, w[:, None, :].astype(x.dtype), window_strides=(1,),
        padding=[(K - 1, 0)], dimension_numbers=("NWC", "WIO", "NWC"),
        feature_group_count=C)
    return y + b.astype(x.dtype)


def _scan_chunk_states(dS, decay):
    def step(s_prev, inp):
        ds_c, dec_c = inp
        return dec_c * s_prev + ds_c, s_prev
    s0 = jnp.zeros(dS.shape[:1] + dS.shape[2:], dS.dtype)
    _, prev = lax.scan(step, s0, (jnp.moveaxis(dS, 1, 0), jnp.moveaxis(decay, 1, 0)))
    return jnp.moveaxis(prev, 0, 1)


def gla_mixer(q, k, v, g, g_lr, w_gate_up, b_gate, w_onorm):
    B, S, _ = q.shape
    H, dk, dv, L = GLA_HEADS, GLA_HEAD_K, GLA_HEAD_V, GLA_CHUNK
    nc = S // L
    f32 = jnp.float32
    q = q.astype(f32).reshape(B, nc, L, H, dk) * (dk ** -0.5)
    k = k.astype(f32).reshape(B, nc, L, H, dk)
    v = v.astype(f32).reshape(B, nc, L, H, dv)
    log_a = jax.nn.log_sigmoid(g_lr.astype(f32) @ w_gate_up.astype(f32)
                               + b_gate.astype(f32)) / GLA_GATE_NORM
    bcum = jnp.cumsum(log_a.reshape(B, nc, L, H, dk), axis=2)
    b_last = bcum[:, :, -1]
    b_mid = bcum[:, :, L // 2:L // 2 + 1]
    q_in = q * jnp.exp(bcum - b_mid)
    k_in = k * jnp.exp(b_mid - bcum)
    causal = jnp.tril(jnp.ones((L, L), bool))
    scores = jnp.where(causal, jnp.einsum("bclhd,bcshd->bchls", q_in, k_in), 0.0)
    o_intra = jnp.einsum("bchls,bcshv->bclhv", scores, v)
    k_st = k * jnp.exp(b_last[:, :, None] - bcum)
    dS = jnp.einsum("bclhd,bclhv->bchdv", k_st, v)
    s_prev = _scan_chunk_states(dS, jnp.exp(b_last)[..., None])
    o_inter = jnp.einsum("bclhd,bchdv->bclhv", q * jnp.exp(bcum), s_prev)
    o = (o_intra + o_inter).reshape(B, S, H, dv)
    o = rmsnorm(o, w_onorm) * jax.nn.silu(g.astype(f32).reshape(B, S, H, dv))
    return o.reshape(B, S, H * dv)


def rglru_mixer(x_br, gate_br, conv_w, conv_b, w_a, b_a, w_i, b_i, lam):
    B, S, W = x_br.shape
    f32 = jnp.float32
    xc = causal_depthwise_conv(x_br, conv_w, conv_b).astype(f32)
    xblk = xc.reshape(B, S, LRU_BLOCKS, LRU_BLOCK)
    r = jax.nn.sigmoid(jnp.einsum("bsnc,ncd->bsnd", xblk, w_a.astype(f32)).reshape(B, S, W)
                       + b_a.astype(f32))
    i = jax.nn.sigmoid(jnp.einsum("bsnc,ncd->bsnd", xblk, w_i.astype(f32)).reshape(B, S, W)
                       + b_i.astype(f32))
    log_a = LRU_C * r * jax.nn.log_sigmoid(lam.astype(f32))
    a = jnp.exp(log_a)
    u = jnp.sqrt(-jnp.expm1(2.0 * log_a)) * (i * xc)

    def combine(c1, c2):
        a1, b1 = c1
        a2, b2 = c2
        return a1 * a2, a2 * b1 + b2

    _, h = lax.associative_scan(combine, (a, u), axis=1)
    return h * jax.nn.gelu(gate_br.astype(f32), approximate=True)


def even_mixer(h, w_in, gla_w_gate, gla_b_gate, gla_w_onorm, lru_conv_w, lru_conv_b,
               lru_w_a, lru_b_a, lru_w_i, lru_b_i, lru_lam, w_out):
    proj = h @ w_in
    cuts = [GLA_DK, 2 * GLA_DK, 2 * GLA_DK + GLA_DV, 2 * GLA_DK + 2 * GLA_DV,
            2 * GLA_DK + 2 * GLA_DV + GLA_GATE_RANK,
            2 * GLA_DK + 2 * GLA_DV + GLA_GATE_RANK + LRU_WIDTH]
    q, k, v, g, g_lr, x_br, gate_br = jnp.split(proj, cuts, axis=-1)
    o_gla = gla_mixer(q, k, v, g, g_lr, gla_w_gate, gla_b_gate, gla_w_onorm)
    o_lru = rglru_mixer(x_br, gate_br, lru_conv_w, lru_conv_b, lru_w_a, lru_b_a,
                        lru_w_i, lru_b_i, lru_lam)
    mix = jnp.concatenate([o_gla, o_lru], axis=-1).astype(h.dtype)
    return mix @ w_out


def ssd_scan(x, dt, A, Bm, Cm):
    Bsz, S, H, P = x.shape
    G, N, L = SSD_GROUPS, SSD_STATE, SSD_CHUNK
    Hg, nc = H // G, S // L
    x = x.reshape(Bsz, nc, L, G, Hg, P)
    dt = dt.reshape(Bsz, nc, L, G, Hg)
    Bm = Bm.reshape(Bsz, nc, L, G, N)
    Cm = Cm.reshape(Bsz, nc, L, G, N)
    acs = jnp.cumsum(dt * A.reshape(G, Hg), axis=2)
    seg = acs[:, :, :, None] - acs[:, :, None, :]
    causal = jnp.tril(jnp.ones((L, L), bool))[:, :, None, None]
    decay = jnp.where(causal, jnp.exp(jnp.minimum(seg, 0.0)), 0.0)
    cb = jnp.einsum("bclgn,bcsgn->bclsg", Cm, Bm)
    wts = cb[..., None] * decay * dt[:, :, None]
    y_diag = jnp.einsum("bclsgh,bcsghp->bclghp", wts, x)
    xw = x * (jnp.exp(acs[:, :, -1:] - acs) * dt)[..., None]
    states = jnp.einsum("bclgn,bclghp->bcghpn", Bm, xw)
    s_prev = _scan_chunk_states(states, jnp.exp(acs[:, :, -1])[..., None, None])
    y_off = jnp.einsum("bclgn,bcghpn->bclghp", Cm, s_prev) * jnp.exp(acs)[..., None]
    return (y_diag + y_off).reshape(Bsz, S, H, P)


def odd_mixer(h, w_in, conv_w, conv_b, dt_bias, a_log, d_skip, gnorm, w_out):
    Bsz, S, _ = h.shape
    f32 = jnp.float32
    proj = h @ w_in
    z, xbc, dt = jnp.split(proj, [SSD_D_INNER, SSD_D_INNER + SSD_CONV_DIM], axis=-1)
    xbc = jax.nn.silu(causal_depthwise_conv(xbc, conv_w, conv_b).astype(f32))
    xs, Bm, Cm = jnp.split(xbc, [SSD_D_INNER, SSD_D_INNER + SSD_GROUPS * SSD_STATE], axis=-1)
    dt = jax.nn.softplus(dt.astype(f32) + dt_bias.astype(f32))
    A = -jnp.exp(a_log.astype(f32))
    xh = xs.reshape(Bsz, S, SSD_HEADS, SSD_HEAD_DIM)
    y = ssd_scan(xh, dt, A,
                 Bm.reshape(Bsz, S, SSD_GROUPS, SSD_STATE),
                 Cm.reshape(Bsz, S, SSD_GROUPS, SSD_STATE))
    y = y + d_skip.astype(f32)[:, None] * xh
    y = y.reshape(Bsz, S, SSD_D_INNER) * jax.nn.silu(z.astype(f32))
    y = rmsnorm(y.reshape(Bsz, S, SSD_GROUPS, SSD_D_INNER // SSD_GROUPS),
                gnorm.reshape(SSD_GROUPS, SSD_D_INNER // SSD_GROUPS))
    return y.reshape(Bsz, S, SSD_D_INNER).astype(h.dtype) @ w_out


def swiglu(h, w_gate, w_up, w_down):
    return (jax.nn.silu(h @ w_gate) * (h @ w_up)) @ w_down


def _fwd_setup_inputs(seed: int = 0) -> dict:
    key = jax.random.key(seed)
    ks = iter(jax.random.split(key, 48))
    ne = (DEPTH + 1) // 2
    no = DEPTH // 2
    f32 = jnp.float32

    def nrm(shape, scale):
        return jax.random.normal(next(ks), shape, f32) * scale

    def gain(shape):
        return 1.0 + nrm(shape, 0.02)

    x = nrm((BATCH, SEQ, D_MODEL), 1.0)
    ev_norm = gain((ne, D_MODEL))
    ev_w_in = nrm((ne, D_MODEL, EVEN_IN), D_MODEL ** -0.5)
    ev_gla_w_gate = nrm((ne, GLA_GATE_RANK, GLA_DK), GLA_GATE_RANK ** -0.5)
    ev_gla_b_gate = nrm((ne, GLA_DK), 0.02)
    ev_gla_w_onorm = gain((ne, GLA_HEAD_V))
    ev_lru_conv_w = nrm((ne, CONV_WIDTH, LRU_WIDTH), CONV_WIDTH ** -0.5)
    ev_lru_conv_b = nrm((ne, LRU_WIDTH), 0.02)
    ev_lru_w_a = nrm((ne, LRU_BLOCKS, LRU_BLOCK, LRU_BLOCK), LRU_BLOCK ** -0.5)
    ev_lru_b_a = nrm((ne, LRU_WIDTH), 0.02)
    ev_lru_w_i = nrm((ne, LRU_BLOCKS, LRU_BLOCK, LRU_BLOCK), LRU_BLOCK ** -0.5)
    ev_lru_b_i = nrm((ne, LRU_WIDTH), 0.02)
    u = jax.random.uniform(next(ks), (ne, LRU_WIDTH), f32, minval=0.9, maxval=0.999)
    a0 = u ** (1.0 / LRU_C)
    ev_lru_lam = jnp.log(a0) - jnp.log1p(-a0)
    ev_w_out = nrm((ne, EVEN_MIX, D_MODEL), EVEN_MIX ** -0.5)
    od_norm = gain((no, D_MODEL))
    od_w_in = nrm((no, D_MODEL, ODD_IN), D_MODEL ** -0.5)
    od_conv_w = nrm((no, CONV_WIDTH, SSD_CONV_DIM), CONV_WIDTH ** -0.5)
    od_conv_b = nrm((no, SSD_CONV_DIM), 0.02)
    dt0 = jnp.exp(jax.random.uniform(next(ks), (no, SSD_HEADS), f32,
                                     minval=math.log(1e-3), maxval=math.log(0.1)))
    od_dt_bias = dt0 + jnp.log(-jnp.expm1(-dt0))
    od_a_log = jnp.log(jax.random.uniform(next(ks), (no, SSD_HEADS), f32, minval=1.0, maxval=16.0))
    od_d_skip = gain((no, SSD_HEADS))
    od_gnorm = gain((no, SSD_D_INNER))
    od_w_out = nrm((no, SSD_D_INNER, D_MODEL), SSD_D_INNER ** -0.5)
    ffn_norm = gain((DEPTH, D_MODEL))
    ffn_w_gate = nrm((DEPTH, D_MODEL, D_FF), D_MODEL ** -0.5)
    ffn_w_up = nrm((DEPTH, D_MODEL, D_FF), D_MODEL ** -0.5)
    ffn_w_down = nrm((DEPTH, D_FF, D_MODEL), D_FF ** -0.5)
    final_norm = gain((D_MODEL,))
    return {
        "x": x,
        "ev_norm": ev_norm, "ev_w_in": ev_w_in,
        "ev_gla_w_gate": ev_gla_w_gate, "ev_gla_b_gate": ev_gla_b_gate,
        "ev_gla_w_onorm": ev_gla_w_onorm,
        "ev_lru_conv_w": ev_lru_conv_w, "ev_lru_conv_b": ev_lru_conv_b,
        "ev_lru_w_a": ev_lru_w_a, "ev_lru_b_a": ev_lru_b_a,
        "ev_lru_w_i": ev_lru_w_i, "ev_lru_b_i": ev_lru_b_i,
        "ev_lru_lam": ev_lru_lam, "ev_w_out": ev_w_out,
        "od_norm": od_norm, "od_w_in": od_w_in,
        "od_conv_w": od_conv_w, "od_conv_b": od_conv_b,
        "od_dt_bias": od_dt_bias, "od_a_log": od_a_log, "od_d_skip": od_d_skip,
        "od_gnorm": od_gnorm, "od_w_out": od_w_out,
        "ffn_norm": ffn_norm, "ffn_w_gate": ffn_w_gate, "ffn_w_up": ffn_w_up,
        "ffn_w_down": ffn_w_down, "final_norm": final_norm,
    }


def _fwd_reference(x, ev_norm, ev_w_in, ev_gla_w_gate, ev_gla_b_gate, ev_gla_w_onorm,
              ev_lru_conv_w, ev_lru_conv_b, ev_lru_w_a, ev_lru_b_a, ev_lru_w_i,
              ev_lru_b_i, ev_lru_lam, ev_w_out,
              od_norm, od_w_in, od_conv_w, od_conv_b, od_dt_bias, od_a_log,
              od_d_skip, od_gnorm, od_w_out,
              ffn_norm, ffn_w_gate, ffn_w_up, ffn_w_down, final_norm):
    for layer in range(DEPTH):
        j = layer // 2
        if layer % 2 == 0:
            h = rmsnorm(x, ev_norm[j])
            x = x + even_mixer(h, ev_w_in[j], ev_gla_w_gate[j], ev_gla_b_gate[j],
                               ev_gla_w_onorm[j], ev_lru_conv_w[j], ev_lru_conv_b[j],
                               ev_lru_w_a[j], ev_lru_b_a[j], ev_lru_w_i[j], ev_lru_b_i[j],
                               ev_lru_lam[j], ev_w_out[j]).astype(x.dtype)
        else:
            h = rmsnorm(x, od_norm[j])
            x = x + odd_mixer(h, od_w_in[j], od_conv_w[j], od_conv_b[j], od_dt_bias[j],
                              od_a_log[j], od_d_skip[j], od_gnorm[j], od_w_out[j]).astype(x.dtype)
        h = rmsnorm(x, ffn_norm[layer])
        x = x + swiglu(h, ffn_w_gate[layer], ffn_w_up[layer], ffn_w_down[layer]).astype(x.dtype)
    return rmsnorm(x, final_norm)


import jax as _jax
import jax.numpy as _jnp

TWIN_FORMAT = 'train_step'
FWD_PARAMS = ['x', 'ev_norm', 'ev_w_in', 'ev_gla_w_gate', 'ev_gla_b_gate', 'ev_gla_w_onorm', 'ev_lru_conv_w', 'ev_lru_conv_b', 'ev_lru_w_a', 'ev_lru_b_a', 'ev_lru_w_i', 'ev_lru_b_i', 'ev_lru_lam', 'ev_w_out', 'od_norm', 'od_w_in', 'od_conv_w', 'od_conv_b', 'od_dt_bias', 'od_a_log', 'od_d_skip', 'od_gnorm', 'od_w_out', 'ffn_norm', 'ffn_w_gate', 'ffn_w_up', 'ffn_w_down', 'final_norm']
TWIN_WEIGHTS = ['ev_norm', 'ev_w_in', 'ev_gla_w_gate', 'ev_gla_b_gate', 'ev_gla_w_onorm', 'ev_lru_conv_w', 'ev_lru_conv_b', 'ev_lru_w_a', 'ev_lru_b_a', 'ev_lru_w_i', 'ev_lru_b_i', 'ev_lru_lam', 'ev_w_out', 'od_norm', 'od_w_in', 'od_conv_w', 'od_conv_b', 'od_dt_bias', 'od_a_log', 'od_d_skip', 'od_gnorm', 'od_w_out', 'ffn_norm', 'ffn_w_gate', 'ffn_w_up', 'ffn_w_down', 'final_norm']
TWIN_DIFF_INPUT = 'x'
TWIN_INPUTS = ['x', 'ev_norm', 'ev_w_in', 'ev_gla_w_gate', 'ev_gla_b_gate', 'ev_gla_w_onorm', 'ev_lru_conv_w', 'ev_lru_conv_b', 'ev_lru_w_a', 'ev_lru_b_a', 'ev_lru_w_i', 'ev_lru_b_i', 'ev_lru_lam', 'ev_w_out', 'od_norm', 'od_w_in', 'od_conv_w', 'od_conv_b', 'od_dt_bias', 'od_a_log', 'od_d_skip', 'od_gnorm', 'od_w_out', 'ffn_norm', 'ffn_w_gate', 'ffn_w_up', 'ffn_w_down', 'final_norm', 'loss_target', 'm_ev_norm', 'm_ev_w_in', 'm_ev_gla_w_gate', 'm_ev_gla_b_gate', 'm_ev_gla_w_onorm', 'm_ev_lru_conv_w', 'm_ev_lru_conv_b', 'm_ev_lru_w_a', 'm_ev_lru_b_a', 'm_ev_lru_w_i', 'm_ev_lru_b_i', 'm_ev_lru_lam', 'm_ev_w_out', 'm_od_norm', 'm_od_w_in', 'm_od_conv_w', 'm_od_conv_b', 'm_od_dt_bias', 'm_od_a_log', 'm_od_d_skip', 'm_od_gnorm', 'm_od_w_out', 'm_ffn_norm', 'm_ffn_w_gate', 'm_ffn_w_up', 'm_ffn_w_down', 'm_final_norm', 'v_ev_norm', 'v_ev_w_in', 'v_ev_gla_w_gate', 'v_ev_gla_b_gate', 'v_ev_gla_w_onorm', 'v_ev_lru_conv_w', 'v_ev_lru_conv_b', 'v_ev_lru_w_a', 'v_ev_lru_b_a', 'v_ev_lru_w_i', 'v_ev_lru_b_i', 'v_ev_lru_lam', 'v_ev_w_out', 'v_od_norm', 'v_od_w_in', 'v_od_conv_w', 'v_od_conv_b', 'v_od_dt_bias', 'v_od_a_log', 'v_od_d_skip', 'v_od_gnorm', 'v_od_w_out', 'v_ffn_norm', 'v_ffn_w_gate', 'v_ffn_w_up', 'v_ffn_w_down', 'v_final_norm']
TWIN_OUTPUTS = ['loss', 'grad_x', 'grad_ev_norm', 'grad_ev_w_in', 'grad_ev_gla_w_gate', 'grad_ev_gla_b_gate', 'grad_ev_gla_w_onorm', 'grad_ev_lru_conv_w', 'grad_ev_lru_conv_b', 'grad_ev_lru_w_a', 'grad_ev_lru_b_a', 'grad_ev_lru_w_i', 'grad_ev_lru_b_i', 'grad_ev_lru_lam', 'grad_ev_w_out', 'grad_od_norm', 'grad_od_w_in', 'grad_od_conv_w', 'grad_od_conv_b', 'grad_od_dt_bias', 'grad_od_a_log', 'grad_od_d_skip', 'grad_od_gnorm', 'grad_od_w_out', 'grad_ffn_norm', 'grad_ffn_w_gate', 'grad_ffn_w_up', 'grad_ffn_w_down', 'grad_final_norm', 'delta_ev_norm', 'delta_ev_w_in', 'delta_ev_gla_w_gate', 'delta_ev_gla_b_gate', 'delta_ev_gla_w_onorm', 'delta_ev_lru_conv_w', 'delta_ev_lru_conv_b', 'delta_ev_lru_w_a', 'delta_ev_lru_b_a', 'delta_ev_lru_w_i', 'delta_ev_lru_b_i', 'delta_ev_lru_lam', 'delta_ev_w_out', 'delta_od_norm', 'delta_od_w_in', 'delta_od_conv_w', 'delta_od_conv_b', 'delta_od_dt_bias', 'delta_od_a_log', 'delta_od_d_skip', 'delta_od_gnorm', 'delta_od_w_out', 'delta_ffn_norm', 'delta_ffn_w_gate', 'delta_ffn_w_up', 'delta_ffn_w_down', 'delta_final_norm', 'new_m_ev_norm', 'new_m_ev_w_in', 'new_m_ev_gla_w_gate', 'new_m_ev_gla_b_gate', 'new_m_ev_gla_w_onorm', 'new_m_ev_lru_conv_w', 'new_m_ev_lru_conv_b', 'new_m_ev_lru_w_a', 'new_m_ev_lru_b_a', 'new_m_ev_lru_w_i', 'new_m_ev_lru_b_i', 'new_m_ev_lru_lam', 'new_m_ev_w_out', 'new_m_od_norm', 'new_m_od_w_in', 'new_m_od_conv_w', 'new_m_od_conv_b', 'new_m_od_dt_bias', 'new_m_od_a_log', 'new_m_od_d_skip', 'new_m_od_gnorm', 'new_m_od_w_out', 'new_m_ffn_norm', 'new_m_ffn_w_gate', 'new_m_ffn_w_up', 'new_m_ffn_w_down', 'new_m_final_norm', 'new_v_ev_norm', 'new_v_ev_w_in', 'new_v_ev_gla_w_gate', 'new_v_ev_gla_b_gate', 'new_v_ev_gla_w_onorm', 'new_v_ev_lru_conv_w', 'new_v_ev_lru_conv_b', 'new_v_ev_lru_w_a', 'new_v_ev_lru_b_a', 'new_v_ev_lru_w_i', 'new_v_ev_lru_b_i', 'new_v_ev_lru_lam', 'new_v_ev_w_out', 'new_v_od_norm', 'new_v_od_w_in', 'new_v_od_conv_w', 'new_v_od_conv_b', 'new_v_od_dt_bias', 'new_v_od_a_log', 'new_v_od_d_skip', 'new_v_od_gnorm', 'new_v_od_w_out', 'new_v_ffn_norm', 'new_v_ffn_w_gate', 'new_v_ffn_w_up', 'new_v_ffn_w_down', 'new_v_final_norm']
TWIN_LEAF_KINDS = {'loss': 'loss', 'grad_x': 'grad_x', 'grad_ev_norm': 'grad_w', 'grad_ev_w_in': 'grad_w', 'grad_ev_gla_w_gate': 'grad_w', 'grad_ev_gla_b_gate': 'grad_w', 'grad_ev_gla_w_onorm': 'grad_w', 'grad_ev_lru_conv_w': 'grad_w', 'grad_ev_lru_conv_b': 'grad_w', 'grad_ev_lru_w_a': 'grad_w', 'grad_ev_lru_b_a': 'grad_w', 'grad_ev_lru_w_i': 'grad_w', 'grad_ev_lru_b_i': 'grad_w', 'grad_ev_lru_lam': 'grad_w', 'grad_ev_w_out': 'grad_w', 'grad_od_norm': 'grad_w', 'grad_od_w_in': 'grad_w', 'grad_od_conv_w': 'grad_w', 'grad_od_conv_b': 'grad_w', 'grad_od_dt_bias': 'grad_w', 'grad_od_a_log': 'grad_w', 'grad_od_d_skip': 'grad_w', 'grad_od_gnorm': 'grad_w', 'grad_od_w_out': 'grad_w', 'grad_ffn_norm': 'grad_w', 'grad_ffn_w_gate': 'grad_w', 'grad_ffn_w_up': 'grad_w', 'grad_ffn_w_down': 'grad_w', 'grad_final_norm': 'grad_w', 'delta_ev_norm': 'delta_w', 'delta_ev_w_in': 'delta_w', 'delta_ev_gla_w_gate': 'delta_w', 'delta_ev_gla_b_gate': 'delta_w', 'delta_ev_gla_w_onorm': 'delta_w', 'delta_ev_lru_conv_w': 'delta_w', 'delta_ev_lru_conv_b': 'delta_w', 'delta_ev_lru_w_a': 'delta_w', 'delta_ev_lru_b_a': 'delta_w', 'delta_ev_lru_w_i': 'delta_w', 'delta_ev_lru_b_i': 'delta_w', 'delta_ev_lru_lam': 'delta_w', 'delta_ev_w_out': 'delta_w', 'delta_od_norm': 'delta_w', 'delta_od_w_in': 'delta_w', 'delta_od_conv_w': 'delta_w', 'delta_od_conv_b': 'delta_w', 'delta_od_dt_bias': 'delta_w', 'delta_od_a_log': 'delta_w', 'delta_od_d_skip': 'delta_w', 'delta_od_gnorm': 'delta_w', 'delta_od_w_out': 'delta_w', 'delta_ffn_norm': 'delta_w', 'delta_ffn_w_gate': 'delta_w', 'delta_ffn_w_up': 'delta_w', 'delta_ffn_w_down': 'delta_w', 'delta_final_norm': 'delta_w', 'new_m_ev_norm': 'new_m', 'new_m_ev_w_in': 'new_m', 'new_m_ev_gla_w_gate': 'new_m', 'new_m_ev_gla_b_gate': 'new_m', 'new_m_ev_gla_w_onorm': 'new_m', 'new_m_ev_lru_conv_w': 'new_m', 'new_m_ev_lru_conv_b': 'new_m', 'new_m_ev_lru_w_a': 'new_m', 'new_m_ev_lru_b_a': 'new_m', 'new_m_ev_lru_w_i': 'new_m', 'new_m_ev_lru_b_i': 'new_m', 'new_m_ev_lru_lam': 'new_m', 'new_m_ev_w_out': 'new_m', 'new_m_od_norm': 'new_m', 'new_m_od_w_in': 'new_m', 'new_m_od_conv_w': 'new_m', 'new_m_od_conv_b': 'new_m', 'new_m_od_dt_bias': 'new_m', 'new_m_od_a_log': 'new_m', 'new_m_od_d_skip': 'new_m', 'new_m_od_gnorm': 'new_m', 'new_m_od_w_out': 'new_m', 'new_m_ffn_norm': 'new_m', 'new_m_ffn_w_gate': 'new_m', 'new_m_ffn_w_up': 'new_m', 'new_m_ffn_w_down': 'new_m', 'new_m_final_norm': 'new_m', 'new_v_ev_norm': 'new_v', 'new_v_ev_w_in': 'new_v', 'new_v_ev_gla_w_gate': 'new_v', 'new_v_ev_gla_b_gate': 'new_v', 'new_v_ev_gla_w_onorm': 'new_v', 'new_v_ev_lru_conv_w': 'new_v', 'new_v_ev_lru_conv_b': 'new_v', 'new_v_ev_lru_w_a': 'new_v', 'new_v_ev_lru_b_a': 'new_v', 'new_v_ev_lru_w_i': 'new_v', 'new_v_ev_lru_b_i': 'new_v', 'new_v_ev_lru_lam': 'new_v', 'new_v_ev_w_out': 'new_v', 'new_v_od_norm': 'new_v', 'new_v_od_w_in': 'new_v', 'new_v_od_conv_w': 'new_v', 'new_v_od_conv_b': 'new_v', 'new_v_od_dt_bias': 'new_v', 'new_v_od_a_log': 'new_v', 'new_v_od_d_skip': 'new_v', 'new_v_od_gnorm': 'new_v', 'new_v_od_w_out': 'new_v', 'new_v_ffn_norm': 'new_v', 'new_v_ffn_w_gate': 'new_v', 'new_v_ffn_w_up': 'new_v', 'new_v_ffn_w_down': 'new_v', 'new_v_final_norm': 'new_v'}


def _forward(args):
    return _fwd_reference(*[args[k] for k in FWD_PARAMS])


def _output_shape():
    def fwd():
        inp = _fwd_setup_inputs(0)
        return _fwd_reference(*[inp[k] for k in FWD_PARAMS])
    out = _jax.eval_shape(fwd)
    return out.shape, out.dtype

N_MICROBATCH = 1
ADAM_LR = 0.001
ADAM_B1 = 0.9
ADAM_B2 = 0.999
ADAM_EPS = 1e-08
ADAM_WD = 0.01
ADAM_STEP = 10
PER_EXAMPLE_BATCH_AXIS = {'x': 0, 'loss_target': 0}
SHARED_INPUTS = []
_WEIGHT_DTYPES = {'ev_norm': _jnp.float32, 'ev_w_in': _jnp.float32, 'ev_gla_w_gate': _jnp.float32, 'ev_gla_b_gate': _jnp.float32, 'ev_gla_w_onorm': _jnp.float32, 'ev_lru_conv_w': _jnp.float32, 'ev_lru_conv_b': _jnp.float32, 'ev_lru_w_a': _jnp.float32, 'ev_lru_b_a': _jnp.float32, 'ev_lru_w_i': _jnp.float32, 'ev_lru_b_i': _jnp.float32, 'ev_lru_lam': _jnp.float32, 'ev_w_out': _jnp.float32, 'od_norm': _jnp.float32, 'od_w_in': _jnp.float32, 'od_conv_w': _jnp.float32, 'od_conv_b': _jnp.float32, 'od_dt_bias': _jnp.float32, 'od_a_log': _jnp.float32, 'od_d_skip': _jnp.float32, 'od_gnorm': _jnp.float32, 'od_w_out': _jnp.float32, 'ffn_norm': _jnp.float32, 'ffn_w_gate': _jnp.float32, 'ffn_w_up': _jnp.float32, 'ffn_w_down': _jnp.float32, 'final_norm': _jnp.float32}
MOMENT_SCALE = {'ev_norm': 1.092980e-01, 'ev_w_in': 6.929627e-02, 'ev_gla_w_gate': 1.103399e-02, 'ev_gla_b_gate': 4.348453e-02, 'ev_gla_w_onorm': 1.388562e-01, 'ev_lru_conv_w': 5.352080e-02, 'ev_lru_conv_b': 5.346712e-01, 'ev_lru_w_a': 1.478286e-02, 'ev_lru_b_a': 1.245964e-02, 'ev_lru_w_i': 2.667858e-02, 'ev_lru_b_i': 1.952999e-02, 'ev_lru_lam': 2.378821e-02, 'ev_w_out': 5.887356e-02, 'od_norm': 9.036117e-02, 'od_w_in': 3.974007e-02, 'od_conv_w': 3.655330e-02, 'od_conv_b': 4.757157e-02, 'od_dt_bias': 1.193479e-01, 'od_a_log': 1.078678e-01, 'od_d_skip': 1.939162e-01, 'od_gnorm': 4.197825e-02, 'od_w_out': 5.839100e-02, 'ffn_norm': 6.434744e-02, 'ffn_w_gate': 2.760596e-02, 'ffn_w_up': 2.674918e-02, 'ffn_w_down': 4.429884e-02, 'final_norm': 1.599072e+01}


def _to_microbatches(a, axis):
    t = _jnp.moveaxis(a, axis, 0)
    t = t.reshape((N_MICROBATCH, t.shape[0] // N_MICROBATCH) + t.shape[1:])
    return _jnp.moveaxis(t, 1, axis + 1)


def setup_inputs(seed: int = 0) -> dict:
    inp = _fwd_setup_inputs(seed)
    key = _jax.random.fold_in(_jax.random.key(seed), 7919)
    shape, _ = _output_shape()
    out = dict(inp)
    out["loss_target"] = _jax.random.normal(_jax.random.fold_in(key, 0), shape, _jnp.float32)
    for i, name in enumerate(TWIN_WEIGHTS):
        w = inp[name].astype(_jnp.float32)
        if MOMENT_SCALE is None:
            s = _jnp.sqrt(_jnp.mean(_jnp.square(w)) + 1e-30)
        else:
            s = MOMENT_SCALE[name]
        km, kv = _jax.random.split(_jax.random.fold_in(key, i + 1))
        out[name] = w
        out["m_" + name] = s * _jax.random.normal(km, w.shape, _jnp.float32)
        out["v_" + name] = (s * s) * _jax.random.uniform(kv, w.shape, _jnp.float32, 0.5, 1.5)
    if N_MICROBATCH > 1:
        for name, axis in PER_EXAMPLE_BATCH_AXIS.items():
            out[name] = _to_microbatches(out[name], axis)
    return {'x': out['x'], 'ev_norm': out['ev_norm'], 'ev_w_in': out['ev_w_in'], 'ev_gla_w_gate': out['ev_gla_w_gate'], 'ev_gla_b_gate': out['ev_gla_b_gate'], 'ev_gla_w_onorm': out['ev_gla_w_onorm'], 'ev_lru_conv_w': out['ev_lru_conv_w'], 'ev_lru_conv_b': out['ev_lru_conv_b'], 'ev_lru_w_a': out['ev_lru_w_a'], 'ev_lru_b_a': out['ev_lru_b_a'], 'ev_lru_w_i': out['ev_lru_w_i'], 'ev_lru_b_i': out['ev_lru_b_i'], 'ev_lru_lam': out['ev_lru_lam'], 'ev_w_out': out['ev_w_out'], 'od_norm': out['od_norm'], 'od_w_in': out['od_w_in'], 'od_conv_w': out['od_conv_w'], 'od_conv_b': out['od_conv_b'], 'od_dt_bias': out['od_dt_bias'], 'od_a_log': out['od_a_log'], 'od_d_skip': out['od_d_skip'], 'od_gnorm': out['od_gnorm'], 'od_w_out': out['od_w_out'], 'ffn_norm': out['ffn_norm'], 'ffn_w_gate': out['ffn_w_gate'], 'ffn_w_up': out['ffn_w_up'], 'ffn_w_down': out['ffn_w_down'], 'final_norm': out['final_norm'], 'loss_target': out['loss_target'], 'm_ev_norm': out['m_ev_norm'], 'm_ev_w_in': out['m_ev_w_in'], 'm_ev_gla_w_gate': out['m_ev_gla_w_gate'], 'm_ev_gla_b_gate': out['m_ev_gla_b_gate'], 'm_ev_gla_w_onorm': out['m_ev_gla_w_onorm'], 'm_ev_lru_conv_w': out['m_ev_lru_conv_w'], 'm_ev_lru_conv_b': out['m_ev_lru_conv_b'], 'm_ev_lru_w_a': out['m_ev_lru_w_a'], 'm_ev_lru_b_a': out['m_ev_lru_b_a'], 'm_ev_lru_w_i': out['m_ev_lru_w_i'], 'm_ev_lru_b_i': out['m_ev_lru_b_i'], 'm_ev_lru_lam': out['m_ev_lru_lam'], 'm_ev_w_out': out['m_ev_w_out'], 'm_od_norm': out['m_od_norm'], 'm_od_w_in': out['m_od_w_in'], 'm_od_conv_w': out['m_od_conv_w'], 'm_od_conv_b': out['m_od_conv_b'], 'm_od_dt_bias': out['m_od_dt_bias'], 'm_od_a_log': out['m_od_a_log'], 'm_od_d_skip': out['m_od_d_skip'], 'm_od_gnorm': out['m_od_gnorm'], 'm_od_w_out': out['m_od_w_out'], 'm_ffn_norm': out['m_ffn_norm'], 'm_ffn_w_gate': out['m_ffn_w_gate'], 'm_ffn_w_up': out['m_ffn_w_up'], 'm_ffn_w_down': out['m_ffn_w_down'], 'm_final_norm': out['m_final_norm'], 'v_ev_norm': out['v_ev_norm'], 'v_ev_w_in': out['v_ev_w_in'], 'v_ev_gla_w_gate': out['v_ev_gla_w_gate'], 'v_ev_gla_b_gate': out['v_ev_gla_b_gate'], 'v_ev_gla_w_onorm': out['v_ev_gla_w_onorm'], 'v_ev_lru_conv_w': out['v_ev_lru_conv_w'], 'v_ev_lru_conv_b': out['v_ev_lru_conv_b'], 'v_ev_lru_w_a': out['v_ev_lru_w_a'], 'v_ev_lru_b_a': out['v_ev_lru_b_a'], 'v_ev_lru_w_i': out['v_ev_lru_w_i'], 'v_ev_lru_b_i': out['v_ev_lru_b_i'], 'v_ev_lru_lam': out['v_ev_lru_lam'], 'v_ev_w_out': out['v_ev_w_out'], 'v_od_norm': out['v_od_norm'], 'v_od_w_in': out['v_od_w_in'], 'v_od_conv_w': out['v_od_conv_w'], 'v_od_conv_b': out['v_od_conv_b'], 'v_od_dt_bias': out['v_od_dt_bias'], 'v_od_a_log': out['v_od_a_log'], 'v_od_d_skip': out['v_od_d_skip'], 'v_od_gnorm': out['v_od_gnorm'], 'v_od_w_out': out['v_od_w_out'], 'v_ffn_norm': out['v_ffn_norm'], 'v_ffn_w_gate': out['v_ffn_w_gate'], 'v_ffn_w_up': out['v_ffn_w_up'], 'v_ffn_w_down': out['v_ffn_w_down'], 'v_final_norm': out['v_final_norm']}


def _loss(weights, diff, rest, loss_target):
    with _jax.named_scope("forward"):
        args = {**rest, TWIN_DIFF_INPUT: diff, **{k: w.astype(_WEIGHT_DTYPES[k]) for k, w in weights.items()}}
        y = _forward(args)
    with _jax.named_scope("loss_head"):
        err = _jnp.square(y.astype(_jnp.float32) - loss_target)
        return 0.5 * _jnp.sum(_jnp.mean(err, axis=-1)) if err.ndim else 0.5 * err


def _adamw(w, g, m, v):
    m = ADAM_B1 * m + (1.0 - ADAM_B1) * g
    v = ADAM_B2 * v + (1.0 - ADAM_B2) * _jnp.square(g)
    m_hat = m / (1.0 - ADAM_B1 ** ADAM_STEP)
    v_hat = v / (1.0 - ADAM_B2 ** ADAM_STEP)
    delta = -ADAM_LR * (m_hat / (_jnp.sqrt(v_hat) + ADAM_EPS) + ADAM_WD * w)
    return delta, m, v


def reference(x, ev_norm, ev_w_in, ev_gla_w_gate, ev_gla_b_gate, ev_gla_w_onorm, ev_lru_conv_w, ev_lru_conv_b, ev_lru_w_a, ev_lru_b_a, ev_lru_w_i, ev_lru_b_i, ev_lru_lam, ev_w_out, od_norm, od_w_in, od_conv_w, od_conv_b, od_dt_bias, od_a_log, od_d_skip, od_gnorm, od_w_out, ffn_norm, ffn_w_gate, ffn_w_up, ffn_w_down, final_norm, loss_target, m_ev_norm, m_ev_w_in, m_ev_gla_w_gate, m_ev_gla_b_gate, m_ev_gla_w_onorm, m_ev_lru_conv_w, m_ev_lru_conv_b, m_ev_lru_w_a, m_ev_lru_b_a, m_ev_lru_w_i, m_ev_lru_b_i, m_ev_lru_lam, m_ev_w_out, m_od_norm, m_od_w_in, m_od_conv_w, m_od_conv_b, m_od_dt_bias, m_od_a_log, m_od_d_skip, m_od_gnorm, m_od_w_out, m_ffn_norm, m_ffn_w_gate, m_ffn_w_up, m_ffn_w_down, m_final_norm, v_ev_norm, v_ev_w_in, v_ev_gla_w_gate, v_ev_gla_b_gate, v_ev_gla_w_onorm, v_ev_lru_conv_w, v_ev_lru_conv_b, v_ev_lru_w_a, v_ev_lru_b_a, v_ev_lru_w_i, v_ev_lru_b_i, v_ev_lru_lam, v_ev_w_out, v_od_norm, v_od_w_in, v_od_conv_w, v_od_conv_b, v_od_dt_bias, v_od_a_log, v_od_d_skip, v_od_gnorm, v_od_w_out, v_ffn_norm, v_ffn_w_gate, v_ffn_w_up, v_ffn_w_down, v_final_norm):
    given = dict(x=x, ev_norm=ev_norm, ev_w_in=ev_w_in, ev_gla_w_gate=ev_gla_w_gate, ev_gla_b_gate=ev_gla_b_gate, ev_gla_w_onorm=ev_gla_w_onorm, ev_lru_conv_w=ev_lru_conv_w, ev_lru_conv_b=ev_lru_conv_b, ev_lru_w_a=ev_lru_w_a, ev_lru_b_a=ev_lru_b_a, ev_lru_w_i=ev_lru_w_i, ev_lru_b_i=ev_lru_b_i, ev_lru_lam=ev_lru_lam, ev_w_out=ev_w_out, od_norm=od_norm, od_w_in=od_w_in, od_conv_w=od_conv_w, od_conv_b=od_conv_b, od_dt_bias=od_dt_bias, od_a_log=od_a_log, od_d_skip=od_d_skip, od_gnorm=od_gnorm, od_w_out=od_w_out, ffn_norm=ffn_norm, ffn_w_gate=ffn_w_gate, ffn_w_up=ffn_w_up, ffn_w_down=ffn_w_down, final_norm=final_norm, loss_target=loss_target, m_ev_norm=m_ev_norm, m_ev_w_in=m_ev_w_in, m_ev_gla_w_gate=m_ev_gla_w_gate, m_ev_gla_b_gate=m_ev_gla_b_gate, m_ev_gla_w_onorm=m_ev_gla_w_onorm, m_ev_lru_conv_w=m_ev_lru_conv_w, m_ev_lru_conv_b=m_ev_lru_conv_b, m_ev_lru_w_a=m_ev_lru_w_a, m_ev_lru_b_a=m_ev_lru_b_a, m_ev_lru_w_i=m_ev_lru_w_i, m_ev_lru_b_i=m_ev_lru_b_i, m_ev_lru_lam=m_ev_lru_lam, m_ev_w_out=m_ev_w_out, m_od_norm=m_od_norm, m_od_w_in=m_od_w_in, m_od_conv_w=m_od_conv_w, m_od_conv_b=m_od_conv_b, m_od_dt_bias=m_od_dt_bias, m_od_a_log=m_od_a_log, m_od_d_skip=m_od_d_skip, m_od_gnorm=m_od_gnorm, m_od_w_out=m_od_w_out, m_ffn_norm=m_ffn_norm, m_ffn_w_gate=m_ffn_w_gate, m_ffn_w_up=m_ffn_w_up, m_ffn_w_down=m_ffn_w_down, m_final_norm=m_final_norm, v_ev_norm=v_ev_norm, v_ev_w_in=v_ev_w_in, v_ev_gla_w_gate=v_ev_gla_w_gate, v_ev_gla_b_gate=v_ev_gla_b_gate, v_ev_gla_w_onorm=v_ev_gla_w_onorm, v_ev_lru_conv_w=v_ev_lru_conv_w, v_ev_lru_conv_b=v_ev_lru_conv_b, v_ev_lru_w_a=v_ev_lru_w_a, v_ev_lru_b_a=v_ev_lru_b_a, v_ev_lru_w_i=v_ev_lru_w_i, v_ev_lru_b_i=v_ev_lru_b_i, v_ev_lru_lam=v_ev_lru_lam, v_ev_w_out=v_ev_w_out, v_od_norm=v_od_norm, v_od_w_in=v_od_w_in, v_od_conv_w=v_od_conv_w, v_od_conv_b=v_od_conv_b, v_od_dt_bias=v_od_dt_bias, v_od_a_log=v_od_a_log, v_od_d_skip=v_od_d_skip, v_od_gnorm=v_od_gnorm, v_od_w_out=v_od_w_out, v_ffn_norm=v_ffn_norm, v_ffn_w_gate=v_ffn_w_gate, v_ffn_w_up=v_ffn_w_up, v_ffn_w_down=v_ffn_w_down, v_final_norm=v_final_norm)
    weights = {n: given[n] for n in TWIN_WEIGHTS}
    shared = {n: given[n] for n in SHARED_INPUTS}
    per_example = {n: given[n] for n in ['x']}
    grad_fn = _jax.value_and_grad(_loss, argnums=(0, 1))

    def one_microbatch(ex, loss_target):
        ex = dict(ex)
        diff = ex.pop(TWIN_DIFF_INPUT)
        return grad_fn(weights, diff, {**shared, **ex}, loss_target)

    if N_MICROBATCH == 1:
        loss, (grad_w, grad_x) = one_microbatch(per_example, given["loss_target"])
    else:
        def body(carry, xs):
            loss_sum, grad_sum = carry
            l_k, (gw_k, gx_k) = one_microbatch(xs[0], xs[1])
            with _jax.named_scope("update"):
                return (loss_sum + l_k, _jax.tree.map(_jnp.add, grad_sum, gw_k)), gx_k

        init = (_jnp.zeros((), _jnp.float32), _jax.tree.map(_jnp.zeros_like, weights))
        (loss, grad_w), grad_x = _jax.lax.scan(body, init, (per_example, given["loss_target"]))
    with _jax.named_scope("update"):
        delta_w, new_m, new_v = {}, {}, {}
        for n in TWIN_WEIGHTS:
            delta_w[n], new_m[n], new_v[n] = _adamw(weights[n], grad_w[n], given["m_" + n], given["v_" + n])
    return (loss, grad_x, *[grad_w[n] for n in TWIN_WEIGHTS], *[delta_w[n] for n in TWIN_WEIGHTS],
            *[new_m[n] for n in TWIN_WEIGHTS], *[new_v[n] for n in TWIN_WEIGHTS])
```

```python
import functools
import math

import jax
import jax.numpy as jnp
from jax import lax
from jax.experimental import pallas as pl
from jax.experimental.pallas import tpu as pltpu

F32 = jnp.float32
BF16 = jnp.bfloat16
MXU_DTYPE = jnp.bfloat16

NORM_EPS = 1e-6
CONV_WIDTH = 4
GLA_HEADS = 4
GLA_GATE_RANK = 16
GLA_GATE_NORM = 16.0
CHUNK = 64
LRU_BLOCK = 128
LRU_C = 8.0
SSD_HEAD_DIM = 64
SSD_GROUPS = 8
SSD_STATE = 128
ADAM_LR, ADAM_B1, ADAM_B2, ADAM_EPS, ADAM_WD, ADAM_STEP = 0.001, 0.9, 0.999, 1e-08, 0.01, 10

LANE = 128
SUBLANE = 8
VMEM_LIMIT = 48 * 1024 * 1024
MESH = pl.DeviceIdType.MESH

WEIGHTS = ['ev_norm', 'ev_w_in', 'ev_gla_w_gate', 'ev_gla_b_gate', 'ev_gla_w_onorm', 'ev_lru_conv_w', 'ev_lru_conv_b',
           'ev_lru_w_a', 'ev_lru_b_a', 'ev_lru_w_i', 'ev_lru_b_i', 'ev_lru_lam', 'ev_w_out', 'od_norm', 'od_w_in',
           'od_conv_w', 'od_conv_b', 'od_dt_bias', 'od_a_log', 'od_d_skip', 'od_gnorm', 'od_w_out', 'ffn_norm',
           'ffn_w_gate', 'ffn_w_up', 'ffn_w_down', 'final_norm']
BIG = ['ev_w_in', 'ev_w_out', 'od_w_in', 'od_w_out', 'ffn_w_gate', 'ffn_w_up', 'ffn_w_down']
SMALL_SHARDED = {'ev_gla_w_gate': 2, 'ev_lru_conv_w': 2, 'od_norm': 1, 'od_conv_w': 2, 'od_conv_b': 1, 'od_gnorm': 1}
SMALL = [n for n in WEIGHTS if n not in BIG]


def _cparams(sem=None, **kw):
    return pltpu.CompilerParams(dimension_semantics=sem, vmem_limit_bytes=VMEM_LIMIT, **kw)


def _full(shape):
    n = len(shape)
    return pl.BlockSpec(shape, lambda *_: (0,) * n)


def _pick(dim, cands):
    for c in cands:
        if dim % c == 0:
            return c
    return dim


def _dot(a, b, ca, cb):
    return lax.dot_general(a.astype(MXU_DTYPE), b.astype(MXU_DTYPE), (((ca,), (cb,)), ((), ())),
                           preferred_element_type=F32)


@jax.custom_vjp
def mm(a, b):
    return _dot(a, b, 1, 0)


def _mm_f(a, b):
    return mm(a, b), (a, b)


def _mm_b(res, g):
    a, b = res
    return mm_nt(g, b).astype(a.dtype), mm_tn(a, g).astype(b.dtype)


@jax.custom_vjp
def mm_nt(a, b):
    return _dot(a, b, 1, 1)


def _mm_nt_f(a, b):
    return mm_nt(a, b), (a, b)


def _mm_nt_b(res, g):
    a, b = res
    return mm(g, b).astype(a.dtype), mm_tn(g, a).astype(b.dtype)


@jax.custom_vjp
def mm_tn(a, b):
    return _dot(a, b, 0, 0)


def _mm_tn_f(a, b):
    return mm_tn(a, b), (a, b)


def _mm_tn_b(res, g):
    a, b = res
    return mm_nt(b, g).astype(a.dtype), mm(a, g).astype(b.dtype)


mm.defvjp(_mm_f, _mm_b)
mm_nt.defvjp(_mm_nt_f, _mm_nt_b)
mm_tn.defvjp(_mm_tn_f, _mm_tn_b)


def _split3(a):
    h = a.astype(BF16)
    r = a - h.astype(F32)
    m = r.astype(BF16)
    l = (r - m.astype(F32)).astype(BF16)
    return h, m, l


def _exact_dot(t, a, ca, cb):
    out = None
    for p in _split3(a):
        d = lax.dot_general(t, p, (((ca,), (cb,)), ((), ())), preferred_element_type=F32)
        out = d if out is None else out + d
    return out


@jax.custom_vjp
def sel_l(t, a):
    return _exact_dot(t, a, 1, 0)


def _sel_l_f(t, a):
    return sel_l(t, a), t


def _sel_l_b(t, g):
    return jnp.zeros_like(t), _exact_dot(t, g, 0, 0)


sel_l.defvjp(_sel_l_f, _sel_l_b)


@jax.custom_vjp
def sel_r(a, t):
    out = None
    for p in _split3(a):
        d = lax.dot_general(p, t, (((1,), (0,)), ((), ())), preferred_element_type=F32)
        out = d if out is None else out + d
    return out


def _sel_r_f(a, t):
    return sel_r(a, t), t


def _sel_r_b(t, g):
    out = None
    for p in _split3(g):
        d = lax.dot_general(p, t, (((1,), (1,)), ((), ())), preferred_element_type=F32)
        out = d if out is None else out + d
    return out, jnp.zeros_like(t)


sel_r.defvjp(_sel_r_f, _sel_r_b)


def _sigmoid(x):
    return 1.0 / (1.0 + jnp.exp(-x))


def _silu(x):
    return x * _sigmoid(x)


def _softplus(x):
    return jnp.maximum(x, 0.0) + jnp.log(1.0 + jnp.exp(-jnp.abs(x)))


def _log_sigmoid(x):
    return -_softplus(-x)


def _gelu_tanh(x):
    c = math.sqrt(2.0 / math.pi)
    return 0.5 * x * (1.0 + jnp.tanh(c * (x + 0.044715 * (x * x * x))))


def _rms(x, w):
    return x * lax.rsqrt(jnp.mean(x * x, axis=-1, keepdims=True) + NORM_EPS) * w


def _tri(n, dtype=BF16):
    r = lax.broadcasted_iota(jnp.int32, (n, n), 0)
    c = lax.broadcasted_iota(jnp.int32, (n, n), 1)
    return (c <= r).astype(dtype)


def matmul(a, b, *, ta=False, tb=False, add=None, out_dtype=F32, name):
    m, k = (a.shape[1], a.shape[0]) if ta else a.shape
    k2, n = (b.shape[1], b.shape[0]) if tb else b.shape
    assert k == k2, (a.shape, b.shape, ta, tb)
    tm = _pick(m, (1024, 512, 256, 128))
    tn = _pick(n, (1024, 512, 256, 128))
    tk = _pick(k, (1024, 512, 256, 128))
    nk = k // tk

    def body(*refs):
        if add is None:
            a_ref, b_ref, o_ref, acc = refs
        else:
            a_ref, b_ref, add_ref, o_ref, acc = refs
        kk = pl.program_id(2)

        @pl.when(kk == 0)
        def _():
            acc[...] = jnp.zeros_like(acc)

        acc[...] += _dot(a_ref[...], b_ref[...], 0 if ta else 1, 1 if tb else 0)

        @pl.when(kk == nk - 1)
        def _():
            r = acc[...]
            if add is not None:
                r = r + add_ref[...].astype(F32)
            o_ref[...] = r.astype(out_dtype)

    a_spec = pl.BlockSpec((tk, tm), lambda i, j, kk: (kk, i)) if ta else pl.BlockSpec((tm, tk), lambda i, j, kk: (i, kk))
    b_spec = pl.BlockSpec((tn, tk), lambda i, j, kk: (j, kk)) if tb else pl.BlockSpec((tk, tn), lambda i, j, kk: (kk, j))
    in_specs, args = [a_spec, b_spec], [a, b]
    if add is not None:
        in_specs.append(pl.BlockSpec((tm, tn), lambda i, j, kk: (i, j)))
        args.append(add)
    return pl.pallas_call(
        body, name=name, grid=(m // tm, n // tn, nk), in_specs=in_specs,
        out_specs=pl.BlockSpec((tm, tn), lambda i, j, kk: (i, j)),
        out_shape=jax.ShapeDtypeStruct((m, n), out_dtype),
        scratch_shapes=[pltpu.VMEM((tm, tn), F32)],
        compiler_params=_cparams(("parallel", "parallel", "arbitrary")),
    )(*args)


def rms_fwd(x, w, *, name):
    t, d = x.shape
    tb = _pick(t, (256, 128, 64))

    def body(x_ref, w_ref, o_ref):
        o_ref[...] = _rms(x_ref[...], w_ref[...]).astype(o_ref.dtype)

    return pl.pallas_call(
        body, name=name, grid=(t // tb,),
        in_specs=[pl.BlockSpec((tb, d), lambda i: (i, 0)), _full((1, d))],
        out_specs=pl.BlockSpec((tb, d), lambda i: (i, 0)),
        out_shape=jax.ShapeDtypeStruct((t, d), BF16),
        compiler_params=_cparams(("parallel",)),
    )(x, w)


def rms_bwd(x, w, dh, dres, *, name):
    t, d = x.shape
    tb = _pick(t, (256, 128, 64))

    def body(x_ref, w_ref, dh_ref, dres_ref, dx_ref, dw_ref):
        @pl.when(pl.program_id(0) == 0)
        def _():
            dw_ref[...] = jnp.zeros_like(dw_ref)

        _, vjp = jax.vjp(_rms, x_ref[...], w_ref[...])
        dx, dw = vjp(dh_ref[...].astype(F32))
        dx_ref[...] = dx + dres_ref[...]
        dw_ref[...] += dw

    row = pl.BlockSpec((tb, d), lambda i: (i, 0))
    return pl.pallas_call(
        body, name=name, grid=(t // tb,),
        in_specs=[row, _full((1, d)), row, row],
        out_specs=[row, _full((1, d))],
        out_shape=[jax.ShapeDtypeStruct((t, d), F32), jax.ShapeDtypeStruct((1, d), F32)],
        compiler_params=_cparams(("arbitrary",)),
    )(x, w, dh, dres)


def _swi(g, u):
    return _silu(g) * u


def swiglu_fwd(gate, up, *, name):
    t, f = gate.shape
    tb = _pick(t, (256, 128, 64))
    row = pl.BlockSpec((tb, f), lambda i: (i, 0))

    def body(g_ref, u_ref, o_ref):
        o_ref[...] = _swi(g_ref[...], u_ref[...]).astype(o_ref.dtype)

    return pl.pallas_call(
        body, name=name, grid=(t // tb,), in_specs=[row, row], out_specs=row,
        out_shape=jax.ShapeDtypeStruct((t, f), BF16), compiler_params=_cparams(("parallel",)),
    )(gate, up)


def swiglu_bwd(gate, up, dact, *, name):
    t, f = gate.shape
    tb = _pick(t, (128, 64))
    row = pl.BlockSpec((tb, f), lambda i: (i, 0))

    def body(g_ref, u_ref, d_ref, dg_ref, du_ref):
        _, vjp = jax.vjp(_swi, g_ref[...], u_ref[...])
        dg, du = vjp(d_ref[...])
        dg_ref[...] = dg.astype(dg_ref.dtype)
        du_ref[...] = du.astype(du_ref.dtype)

    return pl.pallas_call(
        body, name=name, grid=(t // tb,), in_specs=[row, row, row], out_specs=[row, row],
        out_shape=[jax.ShapeDtypeStruct((t, f), BF16)] * 2, compiler_params=_cparams(("parallel",)),
    )(gate, up, dact)


def loss_head(x, w, target, *, name):
    t, d = x.shape
    tb = _pick(t, (256, 128, 64))

    def f(xv, wv, tv):
        y = _rms(xv, wv)
        e = y - tv
        return 0.5 * jnp.sum(jnp.mean(e * e, axis=-1, keepdims=True), axis=0, keepdims=True)

    def body(x_ref, w_ref, t_ref, l_ref, dx_ref, dw_ref):
        @pl.when(pl.program_id(0) == 0)
        def _():
            l_ref[...] = jnp.zeros_like(l_ref)
            dw_ref[...] = jnp.zeros_like(dw_ref)

        val, vjp = jax.vjp(lambda a, b: f(a, b, t_ref[...]), x_ref[...], w_ref[...])
        dx, dw = vjp(jnp.ones((1, 1), F32))
        l_ref[...] += jnp.broadcast_to(val, l_ref.shape)
        dx_ref[...] = dx
        dw_ref[...] += dw

    row = pl.BlockSpec((tb, d), lambda i: (i, 0))
    return pl.pallas_call(
        body, name=name, grid=(t // tb,),
        in_specs=[row, _full((1, d)), row],
        out_specs=[_full((SUBLANE, LANE)), row, _full((1, d))],
        out_shape=[jax.ShapeDtypeStruct((SUBLANE, LANE), F32), jax.ShapeDtypeStruct((t, d), F32),
                   jax.ShapeDtypeStruct((1, d), F32)],
        compiler_params=_cparams(("arbitrary",)),
    )(x, w, target)


def _gla_chunk(q, k, v, g, glr, st, wg, bg, wn, tri):
    L, hk = q.shape
    la = _log_sigmoid(mm(glr, wg) + bg) / GLA_GATE_NORM
    bcum = sel_l(tri, la)
    b_last = jnp.sum(la, axis=0, keepdims=True)
    rows = lax.broadcasted_iota(jnp.int32, (L, 1), 0)
    b_mid = jnp.sum(jnp.where(rows <= L // 2, la, 0.0), axis=0, keepdims=True)
    qs = q * (hk ** -0.5)
    q_in = qs * jnp.exp(bcum - b_mid)
    k_in = k * jnp.exp(b_mid - bcum)
    scores = mm_nt(q_in, k_in) * tri.astype(F32)
    o_intra = mm(scores, v)
    k_st = k * jnp.exp(b_last - bcum)
    d_st = mm_tn(v, k_st)
    o_inter = mm_nt(qs * jnp.exp(bcum), st)
    st_new = jnp.exp(b_last) * st + d_st
    o = _rms(o_intra + o_inter, wn) * _silu(g)
    return o, st_new


def _gla_dims(d):
    dv = d // 2
    dk = dv // 2
    return dk, dv, dk // GLA_HEADS, dv // GLA_HEADS


def gla_fwd(proj, glr_col, wg, bg, wn, dv):
    t = proj.shape[0]
    dk, dv, hk, hv = _gla_dims(2 * dv)
    L, H = CHUNK, GLA_HEADS
    nc = t // L
    wq = 2 * dk + 2 * dv

    def body(p_ref, glr_ref, wg_ref, bg_ref, wn_ref, o_ref, sp_ref, st):
        @pl.when(pl.program_id(0) == 0)
        def _():
            st[...] = jnp.zeros_like(st)

        tri = _tri(L)
        glr = glr_ref[...]
        for h in range(H):
            q = p_ref[:, h * hk:(h + 1) * hk]
            k = p_ref[:, dk + h * hk:dk + (h + 1) * hk]
            v = p_ref[:, 2 * dk + h * hv:2 * dk + (h + 1) * hv]
            g = p_ref[:, 2 * dk + dv + h * hv:2 * dk + dv + (h + 1) * hv]
            s_prev = st[h]
            sp_ref[0, h] = s_prev
            o, s_new = _gla_chunk(q, k, v, g, glr, s_prev, wg_ref[:, h * hk:(h + 1) * hk],
                                  bg_ref[:, h * hk:(h + 1) * hk], wn_ref[...], tri)
            o_ref[:, h * hv:(h + 1) * hv] = o.astype(o_ref.dtype)
            st[h] = s_new

    return pl.pallas_call(
        body, name="gla_fwd", grid=(nc,),
        in_specs=[pl.BlockSpec((L, wq), lambda c: (c, 0)), pl.BlockSpec((L, LANE), lambda c: (c, glr_col)),
                  _full(wg.shape), _full(bg.shape), _full(wn.shape)],
        out_specs=[pl.BlockSpec((L, dv), lambda c: (c, 0)), pl.BlockSpec((1, H, hv, hk), lambda c: (c, 0, 0, 0))],
        out_shape=[jax.ShapeDtypeStruct((t, dv), BF16), jax.ShapeDtypeStruct((nc, H, hv, hk), F32)],
        scratch_shapes=[pltpu.VMEM((H, hv, hk), F32)],
        compiler_params=_cparams(("arbitrary",)),
    )(proj, proj, wg, bg, wn)


def gla_bwd(proj, glr_col, wg, bg, wn, sprev, do, dv):
    t = proj.shape[0]
    dk, _, hk, hv = _gla_dims(2 * dv)
    L, H = CHUNK, GLA_HEADS
    nc = t // L
    wq = 2 * dk + 2 * dv

    def body(p_ref, glr_ref, wg_ref, bg_ref, wn_ref, sp_ref, do_ref, dp_ref, dglr_ref, dwg_ref, dbg_ref, dwn_ref, dst):
        @pl.when(pl.program_id(0) == 0)
        def _():
            dst[...] = jnp.zeros_like(dst)
            dwg_ref[...] = jnp.zeros_like(dwg_ref)
            dbg_ref[...] = jnp.zeros_like(dbg_ref)
            dwn_ref[...] = jnp.zeros_like(dwn_ref)

        tri = _tri(L)
        glr = glr_ref[...]
        dglr = jnp.zeros_like(glr)
        for h in range(H):
            ks = slice(h * hk, (h + 1) * hk)
            q = p_ref[:, ks]
            k = p_ref[:, dk + h * hk:dk + (h + 1) * hk]
            v = p_ref[:, 2 * dk + h * hv:2 * dk + (h + 1) * hv]
            g = p_ref[:, 2 * dk + dv + h * hv:2 * dk + dv + (h + 1) * hv]
            f = functools.partial(_gla_chunk, tri=tri)
            _, vjp = jax.vjp(f, q, k, v, g, glr, sp_ref[0, h], wg_ref[:, ks], bg_ref[:, ks], wn_ref[...])
            dq, dkk, dvv, dg, dgl, ds, dwg, dbg, dwn = vjp((do_ref[:, h * hv:(h + 1) * hv], dst[h]))
            dp_ref[:, ks] = dq.astype(dp_ref.dtype)
            dp_ref[:, dk + h * hk:dk + (h + 1) * hk] = dkk.astype(dp_ref.dtype)
            dp_ref[:, 2 * dk + h * hv:2 * dk + (h + 1) * hv] = dvv.astype(dp_ref.dtype)
            dp_ref[:, 2 * dk + dv + h * hv:2 * dk + dv + (h + 1) * hv] = dg.astype(dp_ref.dtype)
            dglr = dglr + dgl
            dst[h] = ds
            dwg_ref[:, ks] += dwg
            dbg_ref[:, ks] += dbg
            dwn_ref[...] += dwn
        dglr_ref[...] = dglr.astype(dglr_ref.dtype)

    rev = lambda c: nc - 1 - c
    return pl.pallas_call(
        body, name="gla_bwd", grid=(nc,),
        in_specs=[pl.BlockSpec((L, wq), lambda c: (rev(c), 0)), pl.BlockSpec((L, LANE), lambda c: (rev(c), glr_col)),
                  _full(wg.shape), _full(bg.shape), _full(wn.shape),
                  pl.BlockSpec((1, H, hv, hk), lambda c: (rev(c), 0, 0, 0)),
                  pl.BlockSpec((L, dv), lambda c: (rev(c), 0))],
        out_specs=[pl.BlockSpec((L, wq), lambda c: (rev(c), 0)), pl.BlockSpec((L, LANE), lambda c: (rev(c), 0)),
                   _full(wg.shape), _full(bg.shape), _full(wn.shape)],
        out_shape=[jax.ShapeDtypeStruct((t, wq), BF16), jax.ShapeDtypeStruct((t, LANE), BF16),
                   jax.ShapeDtypeStruct(wg.shape, F32), jax.ShapeDtypeStruct(bg.shape, F32),
                   jax.ShapeDtypeStruct(wn.shape, F32)],
        scratch_shapes=[pltpu.VMEM((H, hv, hk), F32)],
        compiler_params=_cparams(("arbitrary",)),
    )(proj, proj, wg, bg, wn, sprev, do)


def _shift_down(x, prev, s):
    if s == 0:
        return x
    rows = lax.broadcasted_iota(jnp.int32, x.shape, 0)
    return jnp.where(rows < s, pltpu.roll(prev, s, 0), pltpu.roll(x, s, 0))


def _shift_up(x, nxt, s):
    if s == 0:
        return x
    n = x.shape[0]
    rows = lax.broadcasted_iota(jnp.int32, x.shape, 0)
    return jnp.where(rows >= n - s, pltpu.roll(nxt, n - s, 0), pltpu.roll(x, n - s, 0))


def _conv(x, prev, w, b):
    y = b
    for k in range(CONV_WIDTH):
        y = y + w[k:k + 1, :] * _shift_down(x, prev, CONV_WIDTH - 1 - k)
    return y


def _conv_bwd(dy, nxt, x, prev, w):
    dx = None
    dws = []
    for k in range(CONV_WIDTH):
        s = CONV_WIDTH - 1 - k
        term = w[k:k + 1, :] * _shift_up(dy, nxt, s)
        dx = term if dx is None else dx + term
        dws.append(jnp.sum(dy * _shift_down(x, prev, s), axis=0, keepdims=True))
    return dx, jnp.concatenate(dws, axis=0), jnp.sum(dy, axis=0, keepdims=True)


def _scan_fwd(a, u):
    n = a.shape[0]
    rows = lax.broadcasted_iota(jnp.int32, a.shape, 0)
    s = 1
    while s < n:
        a_sh = jnp.where(rows < s, 1.0, pltpu.roll(a, s, 0))
        u_sh = jnp.where(rows < s, 0.0, pltpu.roll(u, s, 0))
        u = a * u_sh + u
        a = a * a_sh
        s *= 2
    return a, u


def _scan_rev(c, d):
    n = c.shape[0]
    rows = lax.broadcasted_iota(jnp.int32, c.shape, 0)
    s = 1
    while s < n:
        c_sh = jnp.where(rows >= n - s, 0.0, pltpu.roll(c, n - s, 0))
        d_sh = jnp.where(rows >= n - s, 0.0, pltpu.roll(d, n - s, 0))
        d = d + c * d_sh
        c = c * c_sh
        s *= 2
    return d


def _expm1(x):
    small = x * (1.0 + x * (0.5 + x * (1.0 / 6.0 + x * (1.0 / 24.0))))
    return jnp.where(jnp.abs(x) < 1e-2, small, jnp.exp(x) - 1.0)


def _lru_gates(xc, pa, pi, lam):
    r = _sigmoid(pa)
    i = _sigmoid(pi)
    log_a = LRU_C * r * _log_sigmoid(lam)
    a = jnp.exp(log_a)
    u = jnp.sqrt(-_expm1(2.0 * log_a)) * (i * xc)
    return a, u


def _lru_out(h, gate):
    return h * _gelu_tanh(gate)


def _blockdiag(xc, w_ref, b):
    nb = w_ref.shape[0]
    outs = [mm(xc[:, n * LRU_BLOCK:(n + 1) * LRU_BLOCK], w_ref[n]) for n in range(nb)]
    return jnp.concatenate(outs, axis=1) + b


def lru_fwd(proj, xcol, lw, cw, cb, wa, ba, wi, bi, lam):
    t = proj.shape[0]
    tb = _pick(t, (256, 128, 64))
    nb = t // tb

    def body(x_ref, xp_ref, g_ref, cw_ref, cb_ref, wa_ref, ba_ref, wi_ref, bi_ref, lam_ref, o_ref, hin_ref, hc):
        i = pl.program_id(0)

        @pl.when(i == 0)
        def _():
            hc[...] = jnp.zeros_like(hc)

        prev = jnp.where(i == 0, 0.0, xp_ref[...])
        xc = _conv(x_ref[...], prev, cw_ref[...], cb_ref[...])
        a, u = _lru_gates(xc, _blockdiag(xc, wa_ref, ba_ref[...]), _blockdiag(xc, wi_ref, bi_ref[...]), lam_ref[...])
        acum, h0 = _scan_fwd(a, u)
        h = h0 + acum * hc[...]
        hin_ref[0] = hc[...]
        hc[...] = h[tb - 1:tb, :]
        o_ref[...] = _lru_out(h, g_ref[...]).astype(o_ref.dtype)

    row = lambda col: pl.BlockSpec((tb, lw), lambda i: (i, col))
    return pl.pallas_call(
        body, name="lru_fwd", grid=(nb,),
        in_specs=[row(xcol), pl.BlockSpec((tb, lw), lambda i: (jnp.maximum(i - 1, 0), xcol)), row(xcol + 1),
                  _full(cw.shape), _full(cb.shape), _full(wa.shape), _full(ba.shape), _full(wi.shape), _full(bi.shape),
                  _full(lam.shape)],
        out_specs=[pl.BlockSpec((tb, lw), lambda i: (i, 0)), pl.BlockSpec((1, 1, lw), lambda i: (i, 0, 0))],
        out_shape=[jax.ShapeDtypeStruct((t, lw), BF16), jax.ShapeDtypeStruct((nb, 1, lw), F32)],
        scratch_shapes=[pltpu.VMEM((1, lw), F32)],
        compiler_params=_cparams(("arbitrary",)),
    )(proj, proj, proj, cw, cb, wa, ba, wi, bi, lam)


def lru_bwd(proj, xcol, lw, cw, cb, wa, ba, wi, bi, lam, hin, dmix, docol):
    t = proj.shape[0]
    tb = _pick(t, (256, 128, 64))
    nb = t // tb
    nblk = wa.shape[0]

    def body(x_ref, xp_ref, g_ref, cw_ref, cb_ref, wa_ref, ba_ref, wi_ref, bi_ref, lam_ref, hin_ref, do_ref,
             dxg_ref, dcw_ref, dcb_ref, dwa_ref, dba_ref, dwi_ref, dbi_ref, dlam_ref, gc, dxcn):
        pid = pl.program_id(0)
        i = nb - 1 - pid

        @pl.when(pid == 0)
        def _():
            gc[...] = jnp.zeros_like(gc)
            dxcn[...] = jnp.zeros_like(dxcn)
            for r in (dcw_ref, dcb_ref, dwa_ref, dba_ref, dwi_ref, dbi_ref, dlam_ref):
                r[...] = jnp.zeros_like(r)

        x = x_ref[...]
        prev = jnp.where(i == 0, 0.0, xp_ref[...])
        cw_v = cw_ref[...]
        xc = _conv(x, prev, cw_v, cb_ref[...])
        pa = _blockdiag(xc, wa_ref, ba_ref[...])
        pi = _blockdiag(xc, wi_ref, bi_ref[...])
        (a, u), vjp_g = jax.vjp(_lru_gates, xc, pa, pi, lam_ref[...])
        acum, h0 = _scan_fwd(a, u)
        hi = hin_ref[0]
        h = h0 + acum * hi
        rows = lax.broadcasted_iota(jnp.int32, h.shape, 0)
        hprev = jnp.where(rows < 1, hi, pltpu.roll(h, 1, 0))
        _, vjp_o = jax.vjp(_lru_out, h, g_ref[...])
        dh, dgate = vjp_o(do_ref[...].astype(F32))
        c = jnp.where(rows >= tb - 1, 0.0, pltpu.roll(a, tb - 1, 0))
        g = _scan_rev(c, dh + jnp.where(rows == tb - 1, gc[...], 0.0))
        gc[...] = a[0:1, :] * g[0:1, :]
        dxc, dpa, dpi, dlam = vjp_g((g * hprev, g))
        dlam_ref[...] += dlam
        dba_ref[...] += jnp.sum(dpa, axis=0, keepdims=True)
        dbi_ref[...] += jnp.sum(dpi, axis=0, keepdims=True)
        parts = []
        for n in range(nblk):
            sl = slice(n * LRU_BLOCK, (n + 1) * LRU_BLOCK)
            dwa_ref[n] += mm_tn(xc[:, sl], dpa[:, sl])
            dwi_ref[n] += mm_tn(xc[:, sl], dpi[:, sl])
            parts.append(mm_nt(dpa[:, sl], wa_ref[n]) + mm_nt(dpi[:, sl], wi_ref[n]))
        dxc = dxc + jnp.concatenate(parts, axis=1)
        dx, dcw, dcb = _conv_bwd(dxc, dxcn[...], x, prev, cw_v)
        dxcn[...] = dxc
        dcw_ref[...] += dcw
        dcb_ref[...] += dcb
        dxg_ref[:, :lw] = dx.astype(dxg_ref.dtype)
        dxg_ref[:, lw:] = dgate.astype(dxg_ref.dtype)

    row = lambda col: pl.BlockSpec((tb, lw), lambda p: (nb - 1 - p, col))
    params = [cw, cb, wa, ba, wi, bi, lam]
    return pl.pallas_call(
        body, name="lru_bwd", grid=(nb,),
        in_specs=[row(xcol), pl.BlockSpec((tb, lw), lambda p: (jnp.maximum(nb - 2 - p, 0), xcol)), row(xcol + 1)]
        + [_full(p.shape) for p in params]
        + [pl.BlockSpec((1, 1, lw), lambda p: (nb - 1 - p, 0, 0)), row(docol)],
        out_specs=[pl.BlockSpec((tb, 2 * lw), lambda p: (nb - 1 - p, 0))] + [_full(p.shape) for p in params],
        out_shape=[jax.ShapeDtypeStruct((t, 2 * lw), BF16)] + [jax.ShapeDtypeStruct(p.shape, F32) for p in params],
        scratch_shapes=[pltpu.VMEM((1, lw), F32), pltpu.VMEM((tb, lw), F32)],
        compiler_params=_cparams(("arbitrary",)),
    )(proj, proj, proj, *params, hin, dmix)


def conv_silu_fwd(proj, col0, width, cw, cb):
    t = proj.shape[0]
    tb = _pick(t, (256, 128, 64))
    cbw = _pick(width, (512, 256, 128))
    off = col0 // cbw
    assert col0 % cbw == 0

    def body(x_ref, xp_ref, w_ref, b_ref, o_ref):
        prev = jnp.where(pl.program_id(1) == 0, 0.0, xp_ref[...])
        o_ref[...] = _silu(_conv(x_ref[...], prev, w_ref[...], b_ref[...]))

    return pl.pallas_call(
        body, name="conv_silu_fwd", grid=(width // cbw, t // tb),
        in_specs=[pl.BlockSpec((tb, cbw), lambda j, i: (i, off + j)),
                  pl.BlockSpec((tb, cbw), lambda j, i: (jnp.maximum(i - 1, 0), off + j)),
                  pl.BlockSpec((CONV_WIDTH, cbw), lambda j, i: (0, j)), pl.BlockSpec((1, cbw), lambda j, i: (0, j))],
        out_specs=pl.BlockSpec((tb, cbw), lambda j, i: (i, j)),
        out_shape=jax.ShapeDtypeStruct((t, width), F32),
        compiler_params=_cparams(("parallel", "arbitrary")),
    )(proj, proj, cw, cb)


def conv_silu_bwd(proj, col0, width, cw, cb, dact):
    t = proj.shape[0]
    tb = _pick(t, (256, 128, 64))
    nb = t // tb
    cbw = _pick(width, (512, 256, 128))
    off = col0 // cbw

    def body(x_ref, xp_ref, w_ref, b_ref, d_ref, dx_ref, dw_ref, db_ref, nxt):
        pid = pl.program_id(1)
        i = nb - 1 - pid

        @pl.when(pid == 0)
        def _():
            nxt[...] = jnp.zeros_like(nxt)
            dw_ref[...] = jnp.zeros_like(dw_ref)
            db_ref[...] = jnp.zeros_like(db_ref)

        x = x_ref[...]
        prev = jnp.where(i == 0, 0.0, xp_ref[...])
        w = w_ref[...]
        _, vjp = jax.vjp(_silu, _conv(x, prev, w, b_ref[...]))
        (dcv,) = vjp(d_ref[...])
        dx, dw, db = _conv_bwd(dcv, nxt[...], x, prev, w)
        nxt[...] = dcv
        dx_ref[...] = dx.astype(dx_ref.dtype)
        dw_ref[...] += dw
        db_ref[...] += db

    return pl.pallas_call(
        body, name="conv_silu_bwd", grid=(width // cbw, nb),
        in_specs=[pl.BlockSpec((tb, cbw), lambda j, p: (nb - 1 - p, off + j)),
                  pl.BlockSpec((tb, cbw), lambda j, p: (jnp.maximum(nb - 2 - p, 0), off + j)),
                  pl.BlockSpec((CONV_WIDTH, cbw), lambda j, p: (0, j)), pl.BlockSpec((1, cbw), lambda j, p: (0, j)),
                  pl.BlockSpec((tb, cbw), lambda j, p: (nb - 1 - p, j))],
        out_specs=[pl.BlockSpec((tb, cbw), lambda j, p: (nb - 1 - p, j)),
                   pl.BlockSpec((CONV_WIDTH, cbw), lambda j, p: (0, j)), pl.BlockSpec((1, cbw), lambda j, p: (0, j))],
        out_shape=[jax.ShapeDtypeStruct((t, width), BF16), jax.ShapeDtypeStruct(cw.shape, F32),
                   jax.ShapeDtypeStruct(cb.shape, F32)],
        scratch_shapes=[pltpu.VMEM((tb, cbw), F32)],
        compiler_params=_cparams(("parallel", "arbitrary")),
    )(proj, proj, cw, cb, dact)


def _dt_expand(raw, bias, e):
    return sel_r(_softplus(raw + bias), e)


def dt_fwd(proj, dtcol, bias, e):
    t = proj.shape[0]
    di = e.shape[1]
    tb = _pick(t, (512, 256, 128, 64))

    def body(r_ref, b_ref, e_ref, o_ref):
        o_ref[...] = _dt_expand(r_ref[...], b_ref[...], e_ref[...])

    return pl.pallas_call(
        body, name="dt_fwd", grid=(t // tb,),
        in_specs=[pl.BlockSpec((tb, LANE), lambda i: (i, dtcol)), _full(bias.shape), _full(e.shape)],
        out_specs=pl.BlockSpec((tb, di), lambda i: (i, 0)),
        out_shape=jax.ShapeDtypeStruct((t, di), F32),
        compiler_params=_cparams(("parallel",)),
    )(proj, bias, e)


def dt_bwd(proj, dtcol, bias, e, ddte):
    t = proj.shape[0]
    di = e.shape[1]
    tb = _pick(t, (512, 256, 128, 64))

    def body(r_ref, b_ref, e_ref, d_ref, dr_ref, db_ref):
        @pl.when(pl.program_id(0) == 0)
        def _():
            db_ref[...] = jnp.zeros_like(db_ref)

        e_v = e_ref[...]
        _, vjp = jax.vjp(lambda r, b: _dt_expand(r, b, e_v), r_ref[...], b_ref[...])
        dr, db = vjp(d_ref[...])
        dr_ref[...] = dr.astype(dr_ref.dtype)
        db_ref[...] += db

    return pl.pallas_call(
        body, name="dt_bwd", grid=(t // tb,),
        in_specs=[pl.BlockSpec((tb, LANE), lambda i: (i, dtcol)), _full(bias.shape), _full(e.shape),
                  pl.BlockSpec((tb, di), lambda i: (i, 0))],
        out_specs=[pl.BlockSpec((tb, LANE), lambda i: (i, 0)), _full(bias.shape)],
        out_shape=[jax.ShapeDtypeStruct((t, LANE), BF16), jax.ShapeDtypeStruct(bias.shape, F32)],
        compiler_params=_cparams(("arbitrary",)),
    )(proj, bias, e, ddte)


def head_expand(p, e, *, transpose=False, name):
    di = e.shape[1]

    def body(p_ref, e_ref, o_ref):
        if transpose:
            o_ref[...] = _sel_r_b(e_ref[...], p_ref[...])[0]
        else:
            o_ref[...] = sel_r(p_ref[...], e_ref[...])

    oshape = (SUBLANE, LANE) if transpose else (SUBLANE, di)
    return pl.pallas_call(
        body, name=name, in_specs=[_full(p.shape), _full(e.shape)], out_specs=_full(oshape),
        out_shape=jax.ShapeDtypeStruct(oshape, F32), compiler_params=_cparams(None), grid=(1,),
    )(p, e)


def _ssd_chunk(x, z, bm, cm, dte, st, alog, dskip, gn, tri, cmask, dmask, bd):
    L, gw = x.shape
    reps = gw // L
    a = dte * (-jnp.exp(alog))
    acs = sel_l(tri, a)
    acs_last = jnp.sum(a, axis=0, keepdims=True)
    arow = jnp.sum(acs * dmask, axis=0, keepdims=True)
    dtrow = jnp.sum(dte * dmask, axis=0, keepdims=True)
    cb = mm_nt(cm, jnp.concatenate([bm] * reps, axis=0))
    wts = cb * (jnp.exp(jnp.minimum(acs - arow, 0.0)) * cmask) * dtrow
    xbd = jnp.concatenate([x] * reps, axis=0) * bd
    xw = x * (jnp.exp(acs_last - acs) * dte)
    y = mm(wts, xbd) + mm(cm, st) * jnp.exp(acs) + dskip * x
    st_new = jnp.exp(acs_last) * st + mm_tn(bm, xw)
    return _rms(y * _silu(z), gn), st_new


def _ssd_dims(di):
    gw = di // SSD_GROUPS
    assert CHUNK == SSD_HEAD_DIM and gw % LANE == 0
    return gw, SSD_STATE


def _ssd_masks(gw):
    L = CHUNK
    r = jnp.arange(L)[:, None]
    c = jnp.arange(gw)[None, :]
    cmask = ((c % L) <= r).astype(F32)
    dmask = ((c % L) == r).astype(F32)
    rr = jnp.arange(gw)
    bd = ((rr[:, None] // L) == (rr[None, :] // L)).astype(F32)
    tri = (jnp.arange(L)[None, :] <= jnp.arange(L)[:, None]).astype(BF16)
    return tri, cmask, dmask, bd


def ssd_fwd(xs, proj, dte, alog_e, dskip_e, gn):
    t, di = dte.shape
    gw, n = _ssd_dims(di)
    L, G = CHUNK, SSD_GROUPS
    nc = t // L
    masks = _ssd_masks(gw)
    boff, coff = di // n, di // n + G

    def body(x_ref, z_ref, b_ref, c_ref, dt_ref, al_ref, ds_ref, gn_ref, tri_ref, cm_ref, dm_ref, bd_ref,
             y_ref, sp_ref, st):
        g = pl.program_id(1)

        @pl.when(pl.program_id(0) == 0)
        def _():
            st[g] = jnp.zeros((n, gw), F32)

        s_prev = st[g]
        sp_ref[0, 0] = s_prev
        y, s_new = _ssd_chunk(x_ref[...], z_ref[...], b_ref[...], c_ref[...], dt_ref[...], s_prev, al_ref[0:1, :],
                              ds_ref[0:1, :], gn_ref[...], tri_ref[...], cm_ref[...], dm_ref[...], bd_ref[...])
        y_ref[...] = y.astype(y_ref.dtype)
        st[g] = s_new

    blk = lambda w, off: pl.BlockSpec((L, w), lambda c, g: (c, off + g))
    par = lambda rows: pl.BlockSpec((rows, gw), lambda c, g: (0, g))
    return pl.pallas_call(
        body, name="ssd_fwd", grid=(nc, G),
        in_specs=[blk(gw, 0), blk(gw, 0), blk(n, boff), blk(n, coff), blk(gw, 0), par(SUBLANE), par(SUBLANE), par(1)]
        + [_full(m.shape) for m in masks],
        out_specs=[blk(gw, 0), pl.BlockSpec((1, 1, n, gw), lambda c, g: (c, g, 0, 0))],
        out_shape=[jax.ShapeDtypeStruct((t, di), BF16), jax.ShapeDtypeStruct((nc, G, n, gw), F32)],
        scratch_shapes=[pltpu.VMEM((G, n, gw), F32)],
        compiler_params=_cparams(("arbitrary", "arbitrary")),
    )(xs, proj, xs, xs, dte, alog_e, dskip_e, gn, *masks)


def ssd_bwd(xs, proj, dte, alog_e, dskip_e, gn, sprev, dy):
    t, di = dte.shape
    gw, n = _ssd_dims(di)
    L, G = CHUNK, SSD_GROUPS
    nc = t // L
    masks = _ssd_masks(gw)
    boff, coff = di // n, di // n + G

    def body(x_ref, z_ref, b_ref, c_ref, dt_ref, al_ref, ds_ref, gn_ref, tri_ref, cm_ref, dm_ref, bd_ref, sp_ref,
             dy_ref, dx_ref, dz_ref, db_ref, dc_ref, ddt_ref, dal_ref, dds_ref, dgn_ref, dst):
        g = pl.program_id(1)

        @pl.when(pl.program_id(0) == 0)
        def _():
            dst[g] = jnp.zeros((n, gw), F32)
            dal_ref[g] = jnp.zeros((1, gw), F32)
            dds_ref[g] = jnp.zeros((1, gw), F32)
            dgn_ref[g] = jnp.zeros((1, gw), F32)

        f = functools.partial(_ssd_chunk, tri=tri_ref[...], cmask=cm_ref[...], dmask=dm_ref[...], bd=bd_ref[...])
        _, vjp = jax.vjp(f, x_ref[...], z_ref[...], b_ref[...], c_ref[...], dt_ref[...], sp_ref[0, 0], al_ref[0:1, :],
                         ds_ref[0:1, :], gn_ref[...])
        dx, dz, db, dc, ddt, ds, dal, dds, dgn = vjp((dy_ref[...], dst[g]))
        dx_ref[...] = dx
        dz_ref[...] = dz.astype(dz_ref.dtype)
        db_ref[...] = db
        dc_ref[...] = dc
        ddt_ref[...] = ddt
        dst[g] = ds
        dal_ref[g] += dal
        dds_ref[g] += dds
        dgn_ref[g] += dgn

    blk = lambda w, off: pl.BlockSpec((L, w), lambda c, g: (nc - 1 - c, off + g))
    par = lambda rows: pl.BlockSpec((rows, gw), lambda c, g: (0, g))
    acc = _full((G, 1, gw))
    acc_shape = jax.ShapeDtypeStruct((G, 1, gw), F32)
    return pl.pallas_call(
        body, name="ssd_bwd", grid=(nc, G),
        in_specs=[blk(gw, 0), blk(gw, 0), blk(n, boff), blk(n, coff), blk(gw, 0), par(SUBLANE), par(SUBLANE), par(1)]
        + [_full(m.shape) for m in masks]
        + [pl.BlockSpec((1, 1, n, gw), lambda c, g: (nc - 1 - c, g, 0, 0)), blk(gw, 0)],
        out_specs=[blk(gw, 0), blk(gw, 0), blk(n, 0), blk(n, 0), blk(gw, 0), acc, acc, acc],
        out_shape=[jax.ShapeDtypeStruct((t, di), F32), jax.ShapeDtypeStruct((t, di), BF16),
                   jax.ShapeDtypeStruct((t, G * n), F32), jax.ShapeDtypeStruct((t, G * n), F32),
                   jax.ShapeDtypeStruct((t, di), F32), acc_shape, acc_shape, acc_shape],
        scratch_shapes=[pltpu.VMEM((G, n, gw), F32)],
        compiler_params=_cparams(("arbitrary", "arbitrary")),
    )(xs, proj, xs, xs, dte, alog_e, dskip_e, gn, *masks, sprev, dy)


def _rows2d(a):
    return a.reshape(-1, a.shape[-1])


def _row_tile(rows, cols):
    cap = max(SUBLANE, (1 << 19) // max(cols, 1))
    for c in (2048, 1024, 512, 256, 128, 64, 32, 16, 8):
        if c <= cap and rows % c == 0:
            return c
    return rows


def add_n(xs, *, name):
    shape = xs[0].shape
    xs2 = [_rows2d(a) for a in xs]
    r, c = xs2[0].shape
    tr = _row_tile(r, c)
    row = pl.BlockSpec((tr, c), lambda i: (i, 0))

    def body(*refs):
        acc = refs[0][...]
        for rf in refs[1:-1]:
            acc = acc + rf[...]
        refs[-1][...] = acc

    out = pl.pallas_call(
        body, name=name, grid=(r // tr,), in_specs=[row] * len(xs2), out_specs=row,
        out_shape=jax.ShapeDtypeStruct((r, c), F32), compiler_params=_cparams(("parallel",)),
    )(*xs2)
    return out.reshape(shape)


def sum_leading(x, *, name):
    k, r, c = x.shape
    tr = _row_tile(r, c * k)
    def body(x_ref, o_ref):
        acc = x_ref[0]
        for i in range(1, k):
            acc = acc + x_ref[i]
        o_ref[...] = acc

    return pl.pallas_call(
        body, name=name, grid=(r // tr,), in_specs=[pl.BlockSpec((k, tr, c), lambda i: (0, i, 0))],
        out_specs=pl.BlockSpec((tr, c), lambda i: (i, 0)),
        out_shape=jax.ShapeDtypeStruct((r, c), F32), compiler_params=_cparams(("parallel",)),
    )(x)


def adamw(w, g, m, v, *, name):
    shape = w.shape
    w2, g2, m2, v2 = (_rows2d(a) for a in (w, g, m, v))
    r, c = w2.shape
    tr = _row_tile(r, 2 * c)
    row = pl.BlockSpec((tr, c), lambda i: (i, 0))
    c1 = 1.0 - ADAM_B1 ** ADAM_STEP
    c2 = 1.0 - ADAM_B2 ** ADAM_STEP

    def body(w_ref, g_ref, m_ref, v_ref, d_ref, mo_ref, vo_ref):
        gv = g_ref[...]
        mn = ADAM_B1 * m_ref[...] + (1.0 - ADAM_B1) * gv
        vn = ADAM_B2 * v_ref[...] + (1.0 - ADAM_B2) * (gv * gv)
        d_ref[...] = -ADAM_LR * ((mn / c1) / (jnp.sqrt(vn / c2) + ADAM_EPS) + ADAM_WD * w_ref[...])
        mo_ref[...] = mn
        vo_ref[...] = vn

    outs = pl.pallas_call(
        body, name=name, grid=(r // tr,), in_specs=[row] * 4, out_specs=[row] * 3,
        out_shape=[jax.ShapeDtypeStruct((r, c), F32)] * 3, compiler_params=_cparams(("parallel",)),
    )(w2, g2, m2, v2)
    return tuple(o.reshape(shape) for o in outs)


ANY = pl.BlockSpec(memory_space=pl.ANY)


def _place():
    x, y, c = lax.axis_index("x"), lax.axis_index("y"), lax.axis_index("c")
    chips = [(1 - x, y), (x, 1 - y), (1 - x, 1 - y)]
    return x, y, c, chips


def gather8(block):
    m, n = block.shape

    def body(x_ref, out_ref, send_sems, recv_sems, local_sem):
        x, y, c, chips = _place()
        me, sibling = (x, y, c), (x, y, 1 - c)

        def rows(px, py, pc):
            return out_ref.at[4 * px + 2 * py + pc]

        def copy(k, blk, to, src=None):
            return pltpu.make_async_remote_copy(
                src_ref=rows(*blk) if src is None else src, dst_ref=rows(*blk), send_sem=send_sems.at[k],
                recv_sem=recv_sems.at[k], device_id=to, device_id_type=MESH)

        mine = pltpu.make_async_copy(x_ref, rows(*me), local_sem)
        mine.start()
        first = [copy(0, me, sibling, src=x_ref)]
        first += [copy(1 + j, me, (*chip, c), src=x_ref) for j, chip in enumerate(chips)]
        for cp in first:
            cp.start()
        passed = [copy(4 + j, (*chip, c), sibling) for j, chip in enumerate(chips)]
        for j, chip in enumerate(chips):
            copy(1 + j, (*chip, c), me).wait_recv()
            passed[j].start()
        copy(0, sibling, me).wait_recv()
        for j, chip in enumerate(chips):
            copy(4 + j, (*chip, 1 - c), me).wait_recv()
        for cp in first + passed:
            cp.wait_send()
        mine.wait()

    return pl.pallas_call(
        body, name="gather8",
        out_shape=jax.ShapeDtypeStruct((8, m, n), block.dtype),
        in_specs=[pl.BlockSpec(memory_space=pltpu.VMEM)],
        out_specs=pl.BlockSpec(memory_space=pltpu.VMEM),
        scratch_shapes=[pltpu.SemaphoreType.DMA((7,)), pltpu.SemaphoreType.DMA((7,)), pltpu.SemaphoreType.DMA],
        compiler_params=pltpu.CompilerParams(vmem_limit_bytes=VMEM_LIMIT),
    )(block)


def gather_weights(shards):
    n = len(shards)

    def body(*refs):
        ins, outs = refs[:n], refs[n:2 * n]
        send_sems, recv_sems, local_sems = refs[2 * n:]
        x, y, c, chips = _place()
        me = 2 * x + y
        sibling = (x, y, 1 - c)
        local = [pltpu.make_async_copy(ins[a], outs[a].at[me], local_sems.at[a]) for a in range(n)]
        for cp in local:
            cp.start()

        def copy(a, k, chip_idx, half, to, src=None):
            dst = outs[a].at[chip_idx, half]
            return pltpu.make_async_remote_copy(
                src_ref=dst if src is None else src, dst_ref=dst, send_sem=send_sems.at[a, k],
                recv_sem=recv_sems.at[a, k], device_id=to, device_id_type=MESH)

        first = [copy(a, j, me, c, (*chip, c), src=ins[a].at[c]) for a in range(n) for j, chip in enumerate(chips)]
        for cp in first:
            cp.start()
        passed = []
        for a in range(n):
            for j, (cx, cy) in enumerate(chips):
                copy(a, j, 2 * cx + cy, c, (cx, cy, c)).wait_recv()
                fw = copy(a, 3 + j, 2 * cx + cy, c, sibling)
                fw.start()
                passed.append(fw)
        for a in range(n):
            for j, (cx, cy) in enumerate(chips):
                copy(a, 3 + j, 2 * cx + cy, 1 - c, sibling).wait_recv()
        for cp in first + passed:
            cp.wait_send()
        for cp in local:
            cp.wait()

    return pl.pallas_call(
        body, name="gather_weights",
        out_shape=[jax.ShapeDtypeStruct((4,) + s.shape, s.dtype) for s in shards],
        in_specs=[ANY] * n, out_specs=[ANY] * n,
        scratch_shapes=[pltpu.SemaphoreType.DMA((n, 6)), pltpu.SemaphoreType.DMA((n, 6)), pltpu.SemaphoreType.DMA((n,))],
    )(*shards)


def exchange_halves(grads):
    n = len(grads)

    def body(*refs):
        ins, outs = refs[:n], refs[n:2 * n]
        send_sems, recv_sems = refs[2 * n:]
        x, y, c, _ = _place()
        cps = [pltpu.make_async_remote_copy(
            src_ref=ins[a].at[j, 1 - c], dst_ref=outs[a].at[j], send_sem=send_sems.at[a, j],
            recv_sem=recv_sems.at[a, j], device_id=(x, y, 1 - c), device_id_type=MESH)
            for a in range(n) for j in range(4)]
        for cp in cps:
            cp.start()
        for cp in cps:
            cp.wait()

    return pl.pallas_call(
        body, name="exchange_halves",
        out_shape=[jax.ShapeDtypeStruct((4,) + g.shape[2:], g.dtype) for g in grads],
        in_specs=[ANY] * n, out_specs=[ANY] * n,
        scratch_shapes=[pltpu.SemaphoreType.DMA((n, 4)), pltpu.SemaphoreType.DMA((n, 4))],
    )(*grads)


def scatter_chips(parts):
    n = len(parts)

    def body(*refs):
        ins, outs = refs[:n], refs[n:2 * n]
        send_sems, recv_sems = refs[2 * n:]
        x, y, c, chips = _place()
        cps = [pltpu.make_async_remote_copy(
            src_ref=ins[a].at[2 * cx + cy], dst_ref=outs[a].at[j], send_sem=send_sems.at[a, j],
            recv_sem=recv_sems.at[a, j], device_id=(cx, cy, c), device_id_type=MESH)
            for a in range(n) for j, (cx, cy) in enumerate(chips)]
        for cp in cps:
            cp.start()
        for cp in cps:
            cp.wait()

    return pl.pallas_call(
        body, name="scatter_chips",
        out_shape=[jax.ShapeDtypeStruct((3,) + p.shape[1:], p.dtype) for p in parts],
        in_specs=[ANY] * n, out_specs=[ANY] * n,
        scratch_shapes=[pltpu.SemaphoreType.DMA((n, 3)), pltpu.SemaphoreType.DMA((n, 3))],
    )(*parts)


def join_halves(halves):
    n = len(halves)

    def body(*refs):
        ins, outs = refs[:n], refs[n:2 * n]
        send_sems, recv_sems, local_sems = refs[2 * n:]
        x, y, c, _ = _place()
        local = [pltpu.make_async_copy(ins[a], outs[a].at[c], local_sems.at[a]) for a in range(n)]
        cps = [pltpu.make_async_remote_copy(
            src_ref=ins[a], dst_ref=outs[a].at[c], send_sem=send_sems.at[a], recv_sem=recv_sems.at[a],
            device_id=(x, y, 1 - c), device_id_type=MESH) for a in range(n)]
        for cp in local + cps:
            cp.start()
        for a in range(n):
            pltpu.make_async_remote_copy(
                src_ref=ins[a], dst_ref=outs[a].at[1 - c], send_sem=send_sems.at[a], recv_sem=recv_sems.at[a],
                device_id=(x, y, 1 - c), device_id_type=MESH).wait_recv()
        for cp in cps:
            cp.wait_send()
        for cp in local:
            cp.wait()

    return pl.pallas_call(
        body, name="join_halves",
        out_shape=[jax.ShapeDtypeStruct((2,) + h.shape, h.dtype) for h in halves],
        in_specs=[ANY] * n, out_specs=[ANY] * n,
        scratch_shapes=[pltpu.SemaphoreType.DMA((n,)), pltpu.SemaphoreType.DMA((n,)), pltpu.SemaphoreType.DMA((n,))],
    )(*halves)


INPUTS = ['x'] + WEIGHTS + ['loss_target'] + ['m_' + n for n in WEIGHTS] + ['v_' + n for n in WEIGHTS]


def _round_up(n, m):
    return -(-n // m) * m


def _pack(arrs):
    flat = jnp.concatenate([a.reshape(-1) for a in arrs])
    n = _round_up(flat.shape[0], SUBLANE * LANE)
    return jnp.pad(flat, (0, n - flat.shape[0])).reshape(-1, LANE)


def _unpack(block, shapes):
    flat = block.reshape(-1)
    out, o = [], 0
    for s in shapes:
        n = math.prod(s)
        out.append(flat[o:o + n].reshape(s))
        o += n
    return out


def _cols(g):
    return g.transpose(1, 0, 2).reshape(g.shape[1], -1)


def _uncols(w):
    return w.reshape(w.shape[0], 4, -1).transpose(1, 0, 2)


def _pad_cols(w, total):
    return jnp.pad(w, ((0, 0), (0, total - w.shape[1])))


def kernel(*args):
    a = dict(zip(INPUTS, args))
    x, tgt = a['x'][0], a['loss_target'][0]
    t, d = x.shape
    xi, yi, ci = lax.axis_index("x"), lax.axis_index("y"), lax.axis_index("c")
    chip = 2 * xi + yi
    dk, dv, hk, hv = _gla_dims(d)
    lw = d // 2
    di = 2 * d
    nh = di // SSD_HEAD_DIM
    gn_w = SSD_GROUPS * SSD_STATE
    conv_dim = di + 2 * gn_w
    rank = GLA_GATE_RANK
    wq = 2 * dk + 2 * dv
    ev_used = wq + 2 * lw + rank
    ev_tot = _round_up(wq + 2 * lw + LANE, 512)
    od_used = di + conv_dim + nh
    od_tot = _round_up(di + conv_dim + _round_up(nh, LANE), 512)
    glr_col, xcol, dtcol = (wq + 2 * lw) // LANE, wq // lw, (di + conv_dim) // LANE
    assert wq % lw == 0 and nh <= LANE

    def halves(w):
        w = w.astype(BF16)
        if w.shape[0] == 2:
            return w
        w = w[0]
        return w.reshape((2, w.shape[0] // 2) + w.shape[1:])

    gw = dict(zip(BIG, gather_weights([halves(a[n]) for n in BIG])))
    w_ev_in = _cols(gw['ev_w_in'].reshape((4, d, -1)))
    cuts = [dk, 2 * dk, 2 * dk + dv, wq, wq + rank, wq + rank + lw]
    sq, sk, sv, sg, sglr, sxb, sgb = jnp.split(w_ev_in, cuts, axis=1)
    w_ev_in_p = _pad_cols(jnp.concatenate([sq, sk, sv, sg, sxb, sgb, sglr], axis=1), ev_tot)
    w_ev_out = gw['ev_w_out'].reshape(-1, d)
    w_od_in_p = _pad_cols(_cols(gw['od_w_in'].reshape((4, d, -1))), od_tot)
    w_od_out = gw['od_w_out'].reshape(-1, d)
    w_gate = [_cols(gw['ffn_w_gate'][:, l]) for l in range(2)]
    w_up = [_cols(gw['ffn_w_up'][:, l]) for l in range(2)]
    w_down = [gw['ffn_w_down'][:, l].reshape(-1, d) for l in range(2)]

    sh_names = list(SMALL_SHARDED)
    sh_shapes = [a[n].shape for n in sh_names]
    g8 = gather8(_pack([a[n] for n in sh_names]))
    per_chip = [_unpack(g8[2 * j], sh_shapes) for j in range(4)]
    full = {n: jnp.concatenate([per_chip[j][i] for j in range(4)], axis=SMALL_SHARDED[n])
            for i, n in enumerate(sh_names)}

    wg_p = jnp.zeros((LANE, dk), F32).at[:rank].set(full['ev_gla_w_gate'][0])
    bg, wn = a['ev_gla_b_gate'], a['ev_gla_w_onorm']
    lru_p = [full['ev_lru_conv_w'][0], a['ev_lru_conv_b'], a['ev_lru_w_a'][0], a['ev_lru_b_a'], a['ev_lru_w_i'][0],
             a['ev_lru_b_i'], a['ev_lru_lam']]
    od_cw, od_cb, od_gn = full['od_conv_w'][0], full['od_conv_b'], full['od_gnorm']
    heads = jnp.arange(LANE)[:, None]
    e_mat = ((jnp.arange(di)[None, :] // SSD_HEAD_DIM == heads) & (heads < nh)).astype(BF16)
    row8 = lambda p: jnp.zeros((SUBLANE, LANE), F32).at[0, :nh].set(p[0])
    dt_bias_p = jnp.zeros((1, LANE), F32).at[0, :nh].set(a['od_dt_bias'][0])
    alog_e = head_expand(row8(a['od_a_log']), e_mat, name="expand_a_log")
    dskip_e = head_expand(row8(a['od_d_skip']), e_mat, name="expand_d_skip")

    h0 = rms_fwd(x, a['ev_norm'], name="rms_ev")
    proj = matmul(h0, w_ev_in_p, name="ev_in")
    o_gla, sp_gla = gla_fwd(proj, glr_col, wg_p, bg, wn, dv)
    o_lru, hin = lru_fwd(proj, xcol, lw, *lru_p)
    x1 = matmul(o_gla, w_ev_out[:dv], add=x, name="ev_out_a")
    x1 = matmul(o_lru, w_ev_out[dv:], add=x1, name="ev_out_b")

    def ffn_fwd(xin, l):
        h = rms_fwd(xin, a['ffn_norm'][l:l + 1], name=f"rms_ffn{l}")
        gate = matmul(h, w_gate[l], name=f"ffn{l}_gate")
        up = matmul(h, w_up[l], name=f"ffn{l}_up")
        act = swiglu_fwd(gate, up, name=f"ffn{l}_act")
        return h, gate, up, act, matmul(act, w_down[l], add=xin, name=f"ffn{l}_down")

    h1, gate0, up0, act0, x2 = ffn_fwd(x1, 0)
    h2 = rms_fwd(x2, full['od_norm'], name="rms_od")
    proj2 = matmul(h2, w_od_in_p, name="od_in")
    xs = conv_silu_fwd(proj2, di, conv_dim, od_cw, od_cb)
    dte = dt_fwd(proj2, dtcol, dt_bias_p, e_mat)
    y_ssd, sp_ssd = ssd_fwd(xs, proj2, dte, alog_e, dskip_e, od_gn)
    x3 = matmul(y_ssd, w_od_out, add=x2, name="od_out")
    h3, gate1, up1, act1, x4 = ffn_fwd(x3, 1)
    loss_p, dx4, d_final = loss_head(x4, a['final_norm'][None], tgt, name="loss_head")

    def ffn_bwd(dxo, xin, h, gate, up, act, l):
        dact = matmul(dxo, w_down[l], tb=True, name=f"ffn{l}_d_act")
        d_down = matmul(act, dxo, ta=True, name=f"ffn{l}_dw_down")
        dg, du = swiglu_bwd(gate, up, dact, name=f"ffn{l}_act_bwd")
        dh = matmul(dg, w_gate[l], tb=True, name=f"ffn{l}_dh_gate")
        dh = matmul(du, w_up[l], tb=True, add=dh, name=f"ffn{l}_dh_up")
        d_gate = matmul(h, dg, ta=True, name=f"ffn{l}_dw_gate")
        d_up = matmul(h, du, ta=True, name=f"ffn{l}_dw_up")
        dxi, d_norm = rms_bwd(xin, a['ffn_norm'][l:l + 1], dh, dxo, name=f"rms_ffn{l}_bwd")
        return dxi, d_norm, d_gate, d_up, d_down

    dx3, d_fn1, d_gate1, d_up1, d_down1 = ffn_bwd(dx4, x3, h3, gate1, up1, act1, 1)
    dy = matmul(dx3, w_od_out, tb=True, name="od_out_dy")
    d_od_out = matmul(y_ssd, dx3, ta=True, name="od_out_dw")
    dxs_x, dz, dxs_b, dxs_c, ddte, dal, dds, dgn = ssd_bwd(xs, proj2, dte, alog_e, dskip_e, od_gn, sp_ssd, dy)
    dxs = jnp.concatenate([dxs_x, dxs_b, dxs_c], axis=1)
    dxbc, d_od_cw, d_od_cb = conv_silu_bwd(proj2, di, conv_dim, od_cw, od_cb, dxs)
    ddt_raw, d_dt_bias = dt_bwd(proj2, dtcol, dt_bias_p, e_mat, ddte)
    dproj2 = jnp.concatenate([dz, dxbc, ddt_raw, jnp.zeros((t, od_tot - di - conv_dim - LANE), BF16)], axis=1)
    dh2 = matmul(dproj2, w_od_in_p, tb=True, name="od_in_dh")
    d_od_in = matmul(h2, dproj2, ta=True, name="od_in_dw")[:, :od_used]
    dx2, d_od_norm = rms_bwd(x2, full['od_norm'], dh2, dx3, name="rms_od_bwd")
    to8 = lambda acc: jnp.zeros((SUBLANE, di), F32).at[0].set(acc.reshape(-1))
    d_a_log = head_expand(to8(dal), e_mat, transpose=True, name="reduce_a_log")[0:1, :nh]
    d_d_skip = head_expand(to8(dds), e_mat, transpose=True, name="reduce_d_skip")[0:1, :nh]

    dx1, d_fn0, d_gate0, d_up0, d_down0 = ffn_bwd(dx2, x1, h1, gate0, up0, act0, 0)
    dmix = matmul(dx1, w_ev_out, tb=True, name="ev_out_dmix")
    d_ev_out = jnp.concatenate([matmul(o_gla, dx1, ta=True, name="ev_out_dw_a"),
                                matmul(o_lru, dx1, ta=True, name="ev_out_dw_b")], axis=0)
    dqkvg, dglr, d_wg, d_bg, d_wn = gla_bwd(proj, glr_col, wg_p, bg, wn, sp_gla, dmix, dv)
    dxg, *d_lru = lru_bwd(proj, xcol, lw, *lru_p, hin, dmix, 1)
    dproj = jnp.concatenate([dqkvg, dxg, dglr, jnp.zeros((t, ev_tot - wq - 2 * lw - LANE), BF16)], axis=1)
    dh0 = matmul(dproj, w_ev_in_p, tb=True, name="ev_in_dh")
    d_ev_in_p = matmul(h0, dproj, ta=True, name="ev_in_dw")
    d_ev_in = jnp.concatenate([d_ev_in_p[:, :wq], d_ev_in_p[:, wq + 2 * lw:wq + 2 * lw + rank],
                               d_ev_in_p[:, wq:wq + 2 * lw]], axis=1)
    dx0, d_ev_norm = rms_bwd(x, a['ev_norm'], dh0, dx1, name="rms_ev_bwd")

    def shard_layout(dw, col):
        p = _uncols(dw) if col else dw.reshape((4, dw.shape[0] // 4) + dw.shape[1:])
        return p.reshape((4, 2, p.shape[1] // 2) + p.shape[2:])

    layers = lambda ws, col: jnp.stack([_uncols(w) if col else w.reshape((4, w.shape[0] // 4) + w.shape[1:])
                                        for w in ws], axis=1)
    big_g = {
        'ev_w_in': shard_layout(d_ev_in, True), 'ev_w_out': shard_layout(d_ev_out, False),
        'od_w_in': shard_layout(d_od_in, True), 'od_w_out': shard_layout(d_od_out, False),
        'ffn_w_gate': layers([d_gate0, d_gate1], True), 'ffn_w_up': layers([d_up0, d_up1], True),
        'ffn_w_down': layers([d_down0, d_down1], False),
    }
    g_list = [big_g[n] for n in BIG]
    from_sibling = exchange_halves(g_list)
    chip_sum = [add_n([lax.dynamic_index_in_dim(g, ci, axis=1, keepdims=False), r], name=f"chip_sum_{n}")
                for n, g, r in zip(BIG, g_list, from_sibling)]
    from_chips = scatter_chips(chip_sum)
    half = [add_n([lax.dynamic_index_in_dim(p, chip, axis=0, keepdims=False), r[0], r[1], r[2]], name=f"mesh_sum_{n}")
            for n, p, r in zip(BIG, chip_sum, from_chips)]
    grad = {n: g.reshape(a[n].shape) for n, g in zip(BIG, join_halves(half))}

    small_g = {
        'ev_norm': d_ev_norm, 'ev_gla_w_gate': d_wg[:rank][None], 'ev_gla_b_gate': d_bg, 'ev_gla_w_onorm': d_wn,
        'ev_lru_conv_w': d_lru[0][None], 'ev_lru_conv_b': d_lru[1], 'ev_lru_w_a': d_lru[2][None],
        'ev_lru_b_a': d_lru[3], 'ev_lru_w_i': d_lru[4][None], 'ev_lru_b_i': d_lru[5], 'ev_lru_lam': d_lru[6],
        'od_norm': d_od_norm, 'od_conv_w': d_od_cw[None], 'od_conv_b': d_od_cb, 'od_dt_bias': d_dt_bias[:, :nh],
        'od_a_log': d_a_log, 'od_d_skip': d_d_skip, 'od_gnorm': dgn.reshape(1, di),
        'ffn_norm': jnp.concatenate([d_fn0, d_fn1], axis=0), 'final_norm': d_final[0],
    }
    full_shapes = [small_g[n].shape for n in SMALL]
    summed = sum_leading(gather8(_pack([small_g[n] for n in SMALL])), name="sum_devices")
    for n, g in zip(SMALL, _unpack(summed, full_shapes)):
        if n in SMALL_SHARDED:
            ax = SMALL_SHARDED[n]
            sz = a[n].shape[ax]
            g = lax.dynamic_slice_in_dim(g, chip * sz, sz, axis=ax)
        grad[n] = g

    delta, new_m, new_v = {}, {}, {}
    for n in BIG:
        delta[n], new_m[n], new_v[n] = adamw(a[n], grad[n], a['m_' + n], a['v_' + n], name=f"adamw_{n}")
    shapes = [a[n].shape for n in SMALL]
    packed = [_pack([src[n] if pre is None else a[pre + n] for n in SMALL])
              for src, pre in ((a, None), (grad, None), (None, 'm_'), (None, 'v_'))]
    for outd, blk in zip((delta, new_m, new_v), adamw(*packed, name="adamw_small")):
        outd.update(zip(SMALL, _unpack(blk, shapes)))

    loss = lax.psum(loss_p[0, 0], ("x", "y", "c"))
    return (loss, dx0[None], *[grad[n] for n in WEIGHTS], *[delta[n] for n in WEIGHTS],
            *[new_m[n] for n in WEIGHTS], *[new_v[n] for n in WEIGHTS])
```

```python
import functools
import math

import jax
import jax.numpy as jnp
from jax import lax
from jax.experimental import pallas as pl
from jax.experimental.pallas import tpu as pltpu

F32 = jnp.float32
BF16 = jnp.bfloat16
MXU_DTYPE = jnp.bfloat16

NORM_EPS = 1e-6
CONV_WIDTH = 4
GLA_HEADS = 4
GLA_GATE_RANK = 16
GLA_GATE_NORM = 16.0
CHUNK = 64
LRU_BLOCK = 128
LRU_C = 8.0
SSD_HEAD_DIM = 64
SSD_GROUPS = 8
SSD_STATE = 128
ADAM_LR, ADAM_B1, ADAM_B2, ADAM_EPS, ADAM_WD, ADAM_STEP = 0.001, 0.9, 0.999, 1e-08, 0.01, 10

LANE = 128
SUBLANE = 8
VMEM_LIMIT = 48 * 1024 * 1024
MAX_TK = 2816
MESH = pl.DeviceIdType.MESH

WEIGHTS = ['ev_norm', 'ev_w_in', 'ev_gla_w_gate', 'ev_gla_b_gate', 'ev_gla_w_onorm', 'ev_lru_conv_w', 'ev_lru_conv_b',
           'ev_lru_w_a', 'ev_lru_b_a', 'ev_lru_w_i', 'ev_lru_b_i', 'ev_lru_lam', 'ev_w_out', 'od_norm', 'od_w_in',
           'od_conv_w', 'od_conv_b', 'od_dt_bias', 'od_a_log', 'od_d_skip', 'od_gnorm', 'od_w_out', 'ffn_norm',
           'ffn_w_gate', 'ffn_w_up', 'ffn_w_down', 'final_norm']
BIG = ['ev_w_in', 'ev_w_out', 'od_w_in', 'od_w_out', 'ffn_w_gate', 'ffn_w_up', 'ffn_w_down']
SMALL_SHARDED = {'ev_gla_w_gate': 2, 'ev_lru_conv_w': 2, 'od_norm': 1, 'od_conv_w': 2, 'od_conv_b': 1, 'od_gnorm': 1}
SMALL = [n for n in WEIGHTS if n not in BIG]


def _cparams(sem=None, **kw):
    return pltpu.CompilerParams(dimension_semantics=sem, vmem_limit_bytes=VMEM_LIMIT, **kw)


def _full(shape):
    n = len(shape)
    return pl.BlockSpec(shape, lambda *_: (0,) * n)


def _pick(dim, cands):
    for c in cands:
        if dim % c == 0:
            return c
    return dim


def _dot(a, b, ca, cb):
    return lax.dot_general(a.astype(MXU_DTYPE), b.astype(MXU_DTYPE), (((ca,), (cb,)), ((), ())),
                           preferred_element_type=F32)


@jax.custom_vjp
def mm(a, b):
    return _dot(a, b, 1, 0)


def _mm_f(a, b):
    return mm(a, b), (a, b)


def _mm_b(res, g):
    a, b = res
    return mm_nt(g, b).astype(a.dtype), mm_tn(a, g).astype(b.dtype)


@jax.custom_vjp
def mm_nt(a, b):
    return _dot(a, b, 1, 1)


def _mm_nt_f(a, b):
    return mm_nt(a, b), (a, b)


def _mm_nt_b(res, g):
    a, b = res
    return mm(g, b).astype(a.dtype), mm_tn(g, a).astype(b.dtype)


@jax.custom_vjp
def mm_tn(a, b):
    return _dot(a, b, 0, 0)


def _mm_tn_f(a, b):
    return mm_tn(a, b), (a, b)


def _mm_tn_b(res, g):
    a, b = res
    return mm_nt(b, g).astype(a.dtype), mm(a, g).astype(b.dtype)


mm.defvjp(_mm_f, _mm_b)
mm_nt.defvjp(_mm_nt_f, _mm_nt_b)
mm_tn.defvjp(_mm_tn_f, _mm_tn_b)


def _split3(a):
    h = a.astype(BF16)
    r = a - h.astype(F32)
    m = r.astype(BF16)
    l = (r - m.astype(F32)).astype(BF16)
    return h, m, l


def _exact_dot(t, a, ca, cb):
    out = None
    for p in _split3(a):
        d = lax.dot_general(t, p, (((ca,), (cb,)), ((), ())), preferred_element_type=F32)
        out = d if out is None else out + d
    return out


@jax.custom_vjp
def sel_l(t, a):
    return _exact_dot(t, a, 1, 0)


def _sel_l_f(t, a):
    return sel_l(t, a), t


def _sel_l_b(t, g):
    return jnp.zeros_like(t), _exact_dot(t, g, 0, 0)


sel_l.defvjp(_sel_l_f, _sel_l_b)


@jax.custom_vjp
def sel_r(a, t):
    out = None
    for p in _split3(a):
        d = lax.dot_general(p, t, (((1,), (0,)), ((), ())), preferred_element_type=F32)
        out = d if out is None else out + d
    return out


def _sel_r_f(a, t):
    return sel_r(a, t), t


def _sel_r_b(t, g):
    out = None
    for p in _split3(g):
        d = lax.dot_general(p, t, (((1,), (1,)), ((), ())), preferred_element_type=F32)
        out = d if out is None else out + d
    return out, jnp.zeros_like(t)


sel_r.defvjp(_sel_r_f, _sel_r_b)


def _sigmoid(x):
    return 1.0 / (1.0 + jnp.exp(-x))


def _silu(x):
    return x * _sigmoid(x)


def _softplus(x):
    return jnp.maximum(x, 0.0) + jnp.log(1.0 + jnp.exp(-jnp.abs(x)))


def _log_sigmoid(x):
    return -_softplus(-x)


def _gelu_tanh(x):
    c = math.sqrt(2.0 / math.pi)
    return 0.5 * x * (1.0 + jnp.tanh(c * (x + 0.044715 * (x * x * x))))


def _rms(x, w):
    return x * lax.rsqrt(jnp.mean(x * x, axis=-1, keepdims=True) + NORM_EPS) * w


def _tri(n, dtype=BF16):
    r = lax.broadcasted_iota(jnp.int32, (n, n), 0)
    c = lax.broadcasted_iota(jnp.int32, (n, n), 1)
    return (c <= r).astype(dtype)


def matmul(a, b, *, ta=False, tb=False, add=None, out_dtype=F32, name):
    m, k = (a.shape[1], a.shape[0]) if ta else a.shape
    k2, n = (b.shape[1], b.shape[0]) if tb else b.shape
    assert k == k2, (a.shape, b.shape, ta, tb)
    tm = _pick(m, (1024, 512, 256, 128))
    tn = _pick(n, (1024, 512, 256, 128))
    tk = k if k <= MAX_TK else max(c for c in range(LANE, MAX_TK + 1, LANE) if k % c == 0)
    nk = k // tk

    def body(*refs):
        if add is None:
            a_ref, b_ref, o_ref, acc = refs
        else:
            a_ref, b_ref, add_ref, o_ref, acc = refs
        kk = pl.program_id(2)

        @pl.when(kk == 0)
        def _():
            acc[...] = jnp.zeros_like(acc)

        acc[...] += _dot(a_ref[...], b_ref[...], 0 if ta else 1, 1 if tb else 0)

        @pl.when(kk == nk - 1)
        def _():
            r = acc[...]
            if add is not None:
                r = r + add_ref[...].astype(F32)
            o_ref[...] = r.astype(out_dtype)

    a_spec = pl.BlockSpec((tk, tm), lambda i, j, kk: (kk, i)) if ta else pl.BlockSpec((tm, tk), lambda i, j, kk: (i, kk))
    b_spec = pl.BlockSpec((tn, tk), lambda i, j, kk: (j, kk)) if tb else pl.BlockSpec((tk, tn), lambda i, j, kk: (kk, j))
    in_specs, args = [a_spec, b_spec], [a, b]
    if add is not None:
        in_specs.append(pl.BlockSpec((tm, tn), lambda i, j, kk: (i, j)))
        args.append(add)
    return pl.pallas_call(
        body, name=name, grid=(m // tm, n // tn, nk), in_specs=in_specs,
        out_specs=pl.BlockSpec((tm, tn), lambda i, j, kk: (i, j)),
        out_shape=jax.ShapeDtypeStruct((m, n), out_dtype),
        scratch_shapes=[pltpu.VMEM((tm, tn), F32)],
        compiler_params=_cparams(("parallel", "parallel", "arbitrary")),
    )(*args)


def rms_fwd(x, w, *, name):
    t, d = x.shape
    tb = _pick(t, (256, 128, 64))

    def body(x_ref, w_ref, o_ref):
        o_ref[...] = _rms(x_ref[...], w_ref[...]).astype(o_ref.dtype)

    return pl.pallas_call(
        body, name=name, grid=(t // tb,),
        in_specs=[pl.BlockSpec((tb, d), lambda i: (i, 0)), _full((1, d))],
        out_specs=pl.BlockSpec((tb, d), lambda i: (i, 0)),
        out_shape=jax.ShapeDtypeStruct((t, d), BF16),
        compiler_params=_cparams(("parallel",)),
    )(x, w)


def rms_bwd(x, w, dh, dres, *, name):
    t, d = x.shape
    tb = _pick(t, (256, 128, 64))

    def body(x_ref, w_ref, dh_ref, dres_ref, dx_ref, dw_ref):
        @pl.when(pl.program_id(0) == 0)
        def _():
            dw_ref[...] = jnp.zeros_like(dw_ref)

        _, vjp = jax.vjp(_rms, x_ref[...], w_ref[...])
        dx, dw = vjp(dh_ref[...].astype(F32))
        dx_ref[...] = dx + dres_ref[...]
        dw_ref[...] += dw

    row = pl.BlockSpec((tb, d), lambda i: (i, 0))
    return pl.pallas_call(
        body, name=name, grid=(t // tb,),
        in_specs=[row, _full((1, d)), row, row],
        out_specs=[row, _full((1, d))],
        out_shape=[jax.ShapeDtypeStruct((t, d), F32), jax.ShapeDtypeStruct((1, d), F32)],
        compiler_params=_cparams(("arbitrary",)),
    )(x, w, dh, dres)


def _swi(g, u):
    return _silu(g) * u


def swiglu_fwd(gate, up, *, name):
    t, f = gate.shape
    tb = _pick(t, (256, 128, 64))
    row = pl.BlockSpec((tb, f), lambda i: (i, 0))

    def body(g_ref, u_ref, o_ref):
        o_ref[...] = _swi(g_ref[...], u_ref[...]).astype(o_ref.dtype)

    return pl.pallas_call(
        body, name=name, grid=(t // tb,), in_specs=[row, row], out_specs=row,
        out_shape=jax.ShapeDtypeStruct((t, f), BF16), compiler_params=_cparams(("parallel",)),
    )(gate, up)


def swiglu_bwd(gate, up, dact, *, name):
    t, f = gate.shape
    tb = _pick(t, (128, 64))
    row = pl.BlockSpec((tb, f), lambda i: (i, 0))

    def body(g_ref, u_ref, d_ref, dg_ref, du_ref):
        _, vjp = jax.vjp(_swi, g_ref[...], u_ref[...])
        dg, du = vjp(d_ref[...])
        dg_ref[...] = dg.astype(dg_ref.dtype)
        du_ref[...] = du.astype(du_ref.dtype)

    return pl.pallas_call(
        body, name=name, grid=(t // tb,), in_specs=[row, row, row], out_specs=[row, row],
        out_shape=[jax.ShapeDtypeStruct((t, f), BF16)] * 2, compiler_params=_cparams(("parallel",)),
    )(gate, up, dact)


def loss_head(x, w, target, *, name):
    t, d = x.shape
    tb = _pick(t, (256, 128, 64))

    def f(xv, wv, tv):
        y = _rms(xv, wv)
        e = y - tv
        return 0.5 * jnp.sum(jnp.mean(e * e, axis=-1, keepdims=True), axis=0, keepdims=True)

    def body(x_ref, w_ref, t_ref, l_ref, dx_ref, dw_ref):
        @pl.when(pl.program_id(0) == 0)
        def _():
            l_ref[...] = jnp.zeros_like(l_ref)
            dw_ref[...] = jnp.zeros_like(dw_ref)

        val, vjp = jax.vjp(lambda a, b: f(a, b, t_ref[...]), x_ref[...], w_ref[...])
        dx, dw = vjp(jnp.ones((1, 1), F32))
        l_ref[...] += jnp.broadcast_to(val, l_ref.shape)
        dx_ref[...] = dx
        dw_ref[...] += dw

    row = pl.BlockSpec((tb, d), lambda i: (i, 0))
    return pl.pallas_call(
        body, name=name, grid=(t // tb,),
        in_specs=[row, _full((1, d)), row],
        out_specs=[_full((SUBLANE, LANE)), row, _full((1, d))],
        out_shape=[jax.ShapeDtypeStruct((SUBLANE, LANE), F32), jax.ShapeDtypeStruct((t, d), F32),
                   jax.ShapeDtypeStruct((1, d), F32)],
        compiler_params=_cparams(("arbitrary",)),
    )(x, w, target)


def _gla_chunk(q, k, v, g, glr, st, wg, bg, wn, tri):
    L, hk = q.shape
    la = _log_sigmoid(mm(glr, wg) + bg) / GLA_GATE_NORM
    bcum = sel_l(tri, la)
    b_last = jnp.sum(la, axis=0, keepdims=True)
    rows = lax.broadcasted_iota(jnp.int32, (L, 1), 0)
    b_mid = jnp.sum(jnp.where(rows <= L // 2, la, 0.0), axis=0, keepdims=True)
    qs = q * (hk ** -0.5)
    q_in = qs * jnp.exp(bcum - b_mid)
    k_in = k * jnp.exp(b_mid - bcum)
    scores = mm_nt(q_in, k_in) * tri.astype(F32)
    o_intra = mm(scores, v)
    k_st = k * jnp.exp(b_last - bcum)
    d_st = mm_tn(v, k_st)
    o_inter = mm_nt(qs * jnp.exp(bcum), st)
    st_new = jnp.exp(b_last) * st + d_st
    o = _rms(o_intra + o_inter, wn) * _silu(g)
    return o, st_new


def _gla_dims(d):
    dv = d // 2
    dk = dv // 2
    return dk, dv, dk // GLA_HEADS, dv // GLA_HEADS


def gla_fwd(proj, glr_col, wg, bg, wn, dv):
    t = proj.shape[0]
    dk, dv, hk, hv = _gla_dims(2 * dv)
    L, H = CHUNK, GLA_HEADS
    nc = t // L
    wq = 2 * dk + 2 * dv

    def body(p_ref, glr_ref, wg_ref, bg_ref, wn_ref, o_ref, sp_ref, st):
        @pl.when(pl.program_id(0) == 0)
        def _():
            st[...] = jnp.zeros_like(st)

        tri = _tri(L)
        glr = glr_ref[...]
        for h in range(H):
            q = p_ref[:, h * hk:(h + 1) * hk]
            k = p_ref[:, dk + h * hk:dk + (h + 1) * hk]
            v = p_ref[:, 2 * dk + h * hv:2 * dk + (h + 1) * hv]
            g = p_ref[:, 2 * dk + dv + h * hv:2 * dk + dv + (h + 1) * hv]
            s_prev = st[h]
            sp_ref[0, h] = s_prev
            o, s_new = _gla_chunk(q, k, v, g, glr, s_prev, wg_ref[:, h * hk:(h + 1) * hk],
                                  bg_ref[:, h * hk:(h + 1) * hk], wn_ref[...], tri)
            o_ref[:, h * hv:(h + 1) * hv] = o.astype(o_ref.dtype)
            st[h] = s_new

    return pl.pallas_call(
        body, name="gla_fwd", grid=(nc,),
        in_specs=[pl.BlockSpec((L, wq), lambda c: (c, 0)), pl.BlockSpec((L, LANE), lambda c: (c, glr_col)),
                  _full(wg.shape), _full(bg.shape), _full(wn.shape)],
        out_specs=[pl.BlockSpec((L, dv), lambda c: (c, 0)), pl.BlockSpec((1, H, hv, hk), lambda c: (c, 0, 0, 0))],
        out_shape=[jax.ShapeDtypeStruct((t, dv), BF16), jax.ShapeDtypeStruct((nc, H, hv, hk), F32)],
        scratch_shapes=[pltpu.VMEM((H, hv, hk), F32)],
        compiler_params=_cparams(("arbitrary",)),
    )(proj, proj, wg, bg, wn)


def gla_bwd(proj, glr_col, wg, bg, wn, sprev, do, dv):
    t = proj.shape[0]
    dk, _, hk, hv = _gla_dims(2 * dv)
    L, H = CHUNK, GLA_HEADS
    nc = t // L
    wq = 2 * dk + 2 * dv

    def body(p_ref, glr_ref, wg_ref, bg_ref, wn_ref, sp_ref, do_ref, dp_ref, dglr_ref, dwg_ref, dbg_ref, dwn_ref, dst):
        @pl.when(pl.program_id(0) == 0)
        def _():
            dst[...] = jnp.zeros_like(dst)
            dwg_ref[...] = jnp.zeros_like(dwg_ref)
            dbg_ref[...] = jnp.zeros_like(dbg_ref)
            dwn_ref[...] = jnp.zeros_like(dwn_ref)

        tri = _tri(L)
        glr = glr_ref[...]
        dglr = jnp.zeros_like(glr)
        for h in range(H):
            ks = slice(h * hk, (h + 1) * hk)
            q = p_ref[:, ks]
            k = p_ref[:, dk + h * hk:dk + (h + 1) * hk]
            v = p_ref[:, 2 * dk + h * hv:2 * dk + (h + 1) * hv]
            g = p_ref[:, 2 * dk + dv + h * hv:2 * dk + dv + (h + 1) * hv]
            f = functools.partial(_gla_chunk, tri=tri)
            _, vjp = jax.vjp(f, q, k, v, g, glr, sp_ref[0, h], wg_ref[:, ks], bg_ref[:, ks], wn_ref[...])
            dq, dkk, dvv, dg, dgl, ds, dwg, dbg, dwn = vjp((do_ref[:, h * hv:(h + 1) * hv], dst[h]))
            dp_ref[:, ks] = dq.astype(dp_ref.dtype)
            dp_ref[:, dk + h * hk:dk + (h + 1) * hk] = dkk.astype(dp_ref.dtype)
            dp_ref[:, 2 * dk + h * hv:2 * dk + (h + 1) * hv] = dvv.astype(dp_ref.dtype)
            dp_ref[:, 2 * dk + dv + h * hv:2 * dk + dv + (h + 1) * hv] = dg.astype(dp_ref.dtype)
            dglr = dglr + dgl
            dst[h] = ds
            dwg_ref[:, ks] += dwg
            dbg_ref[:, ks] += dbg
            dwn_ref[...] += dwn
        dglr_ref[...] = dglr.astype(dglr_ref.dtype)

    rev = lambda c: nc - 1 - c
    return pl.pallas_call(
        body, name="gla_bwd", grid=(nc,),
        in_specs=[pl.BlockSpec((L, wq), lambda c: (rev(c), 0)), pl.BlockSpec((L, LANE), lambda c: (rev(c), glr_col)),
                  _full(wg.shape), _full(bg.shape), _full(wn.shape),
                  pl.BlockSpec((1, H, hv, hk), lambda c: (rev(c), 0, 0, 0)),
                  pl.BlockSpec((L, dv), lambda c: (rev(c), 0))],
        out_specs=[pl.BlockSpec((L, wq), lambda c: (rev(c), 0)), pl.BlockSpec((L, LANE), lambda c: (rev(c), 0)),
                   _full(wg.shape), _full(bg.shape), _full(wn.shape)],
        out_shape=[jax.ShapeDtypeStruct((t, wq), BF16), jax.ShapeDtypeStruct((t, LANE), BF16),
                   jax.ShapeDtypeStruct(wg.shape, F32), jax.ShapeDtypeStruct(bg.shape, F32),
                   jax.ShapeDtypeStruct(wn.shape, F32)],
        scratch_shapes=[pltpu.VMEM((H, hv, hk), F32)],
        compiler_params=_cparams(("arbitrary",)),
    )(proj, proj, wg, bg, wn, sprev, do)


def _shift_down(x, prev, s):
    if s == 0:
        return x
    rows = lax.broadcasted_iota(jnp.int32, x.shape, 0)
    return jnp.where(rows < s, pltpu.roll(prev, s, 0), pltpu.roll(x, s, 0))


def _shift_up(x, nxt, s):
    if s == 0:
        return x
    n = x.shape[0]
    rows = lax.broadcasted_iota(jnp.int32, x.shape, 0)
    return jnp.where(rows >= n - s, pltpu.roll(nxt, n - s, 0), pltpu.roll(x, n - s, 0))


def _conv(x, prev, w, b):
    y = b
    for k in range(CONV_WIDTH):
        y = y + w[k:k + 1, :] * _shift_down(x, prev, CONV_WIDTH - 1 - k)
    return y


def _conv_bwd(dy, nxt, x, prev, w):
    dx = None
    dws = []
    for k in range(CONV_WIDTH):
        s = CONV_WIDTH - 1 - k
        term = w[k:k + 1, :] * _shift_up(dy, nxt, s)
        dx = term if dx is None else dx + term
        dws.append(jnp.sum(dy * _shift_down(x, prev, s), axis=0, keepdims=True))
    return dx, jnp.concatenate(dws, axis=0), jnp.sum(dy, axis=0, keepdims=True)


def _scan_fwd(a, u):
    n = a.shape[0]
    rows = lax.broadcasted_iota(jnp.int32, a.shape, 0)
    s = 1
    while s < n:
        a_sh = jnp.where(rows < s, 1.0, pltpu.roll(a, s, 0))
        u_sh = jnp.where(rows < s, 0.0, pltpu.roll(u, s, 0))
        u = a * u_sh + u
        a = a * a_sh
        s *= 2
    return a, u


def _scan_rev(c, d):
    n = c.shape[0]
    rows = lax.broadcasted_iota(jnp.int32, c.shape, 0)
    s = 1
    while s < n:
        c_sh = jnp.where(rows >= n - s, 0.0, pltpu.roll(c, n - s, 0))
        d_sh = jnp.where(rows >= n - s, 0.0, pltpu.roll(d, n - s, 0))
        d = d + c * d_sh
        c = c * c_sh
        s *= 2
    return d


def _expm1(x):
    small = x * (1.0 + x * (0.5 + x * (1.0 / 6.0 + x * (1.0 / 24.0))))
    return jnp.where(jnp.abs(x) < 1e-2, small, jnp.exp(x) - 1.0)


def _lru_gates(xc, pa, pi, lam):
    r = _sigmoid(pa)
    i = _sigmoid(pi)
    log_a = LRU_C * r * _log_sigmoid(lam)
    a = jnp.exp(log_a)
    u = jnp.sqrt(-_expm1(2.0 * log_a)) * (i * xc)
    return a, u


def _lru_out(h, gate):
    return h * _gelu_tanh(gate)


def _blockdiag(xc, w_ref, b):
    nb = w_ref.shape[0]
    outs = [mm(xc[:, n * LRU_BLOCK:(n + 1) * LRU_BLOCK], w_ref[n]) for n in range(nb)]
    return jnp.concatenate(outs, axis=1) + b


def lru_fwd(proj, xcol, lw, cw, cb, wa, ba, wi, bi, lam):
    t = proj.shape[0]
    tb = _pick(t, (256, 128, 64))
    nb = t // tb

    def body(x_ref, xp_ref, g_ref, cw_ref, cb_ref, wa_ref, ba_ref, wi_ref, bi_ref, lam_ref, o_ref, hin_ref, hc):
        i = pl.program_id(0)

        @pl.when(i == 0)
        def _():
            hc[...] = jnp.zeros_like(hc)

        prev = jnp.where(i == 0, 0.0, xp_ref[...])
        xc = _conv(x_ref[...], prev, cw_ref[...], cb_ref[...])
        a, u = _lru_gates(xc, _blockdiag(xc, wa_ref, ba_ref[...]), _blockdiag(xc, wi_ref, bi_ref[...]), lam_ref[...])
        acum, h0 = _scan_fwd(a, u)
        h = h0 + acum * hc[...]
        hin_ref[0] = hc[...]
        hc[...] = h[tb - 1:tb, :]
        o_ref[...] = _lru_out(h, g_ref[...]).astype(o_ref.dtype)

    row = lambda col: pl.BlockSpec((tb, lw), lambda i: (i, col))
    return pl.pallas_call(
        body, name="lru_fwd", grid=(nb,),
        in_specs=[row(xcol), pl.BlockSpec((tb, lw), lambda i: (jnp.maximum(i - 1, 0), xcol)), row(xcol + 1),
                  _full(cw.shape), _full(cb.shape), _full(wa.shape), _full(ba.shape), _full(wi.shape), _full(bi.shape),
                  _full(lam.shape)],
        out_specs=[pl.BlockSpec((tb, lw), lambda i: (i, 0)), pl.BlockSpec((1, 1, lw), lambda i: (i, 0, 0))],
        out_shape=[jax.ShapeDtypeStruct((t, lw), BF16), jax.ShapeDtypeStruct((nb, 1, lw), F32)],
        scratch_shapes=[pltpu.VMEM((1, lw), F32)],
        compiler_params=_cparams(("arbitrary",)),
    )(proj, proj, proj, cw, cb, wa, ba, wi, bi, lam)


def lru_bwd(proj, xcol, lw, cw, cb, wa, ba, wi, bi, lam, hin, dmix, docol):
    t = proj.shape[0]
    tb = _pick(t, (256, 128, 64))
    nb = t // tb
    nblk = wa.shape[0]

    def body(x_ref, xp_ref, g_ref, cw_ref, cb_ref, wa_ref, ba_ref, wi_ref, bi_ref, lam_ref, hin_ref, do_ref,
             dxg_ref, dcw_ref, dcb_ref, dwa_ref, dba_ref, dwi_ref, dbi_ref, dlam_ref, gc, dxcn):
        pid = pl.program_id(0)
        i = nb - 1 - pid

        @pl.when(pid == 0)
        def _():
            gc[...] = jnp.zeros_like(gc)
            dxcn[...] = jnp.zeros_like(dxcn)
            for r in (dcw_ref, dcb_ref, dwa_ref, dba_ref, dwi_ref, dbi_ref, dlam_ref):
                r[...] = jnp.zeros_like(r)

        x = x_ref[...]
        prev = jnp.where(i == 0, 0.0, xp_ref[...])
        cw_v = cw_ref[...]
        xc = _conv(x, prev, cw_v, cb_ref[...])
        pa = _blockdiag(xc, wa_ref, ba_ref[...])
        pi = _blockdiag(xc, wi_ref, bi_ref[...])
        (a, u), vjp_g = jax.vjp(_lru_gates, xc, pa, pi, lam_ref[...])
        acum, h0 = _scan_fwd(a, u)
        hi = hin_ref[0]
        h = h0 + acum * hi
        rows = lax.broadcasted_iota(jnp.int32, h.shape, 0)
        hprev = jnp.where(rows < 1, hi, pltpu.roll(h, 1, 0))
        _, vjp_o = jax.vjp(_lru_out, h, g_ref[...])
        dh, dgate = vjp_o(do_ref[...].astype(F32))
        c = jnp.where(rows >= tb - 1, 0.0, pltpu.roll(a, tb - 1, 0))
        g = _scan_rev(c, dh + jnp.where(rows == tb - 1, gc[...], 0.0))
        gc[...] = a[0:1, :] * g[0:1, :]
        dxc, dpa, dpi, dlam = vjp_g((g * hprev, g))
        dlam_ref[...] += dlam
        dba_ref[...] += jnp.sum(dpa, axis=0, keepdims=True)
        dbi_ref[...] += jnp.sum(dpi, axis=0, keepdims=True)
        parts = []
        for n in range(nblk):
            sl = slice(n * LRU_BLOCK, (n + 1) * LRU_BLOCK)
            dwa_ref[n] += mm_tn(xc[:, sl], dpa[:, sl])
            dwi_ref[n] += mm_tn(xc[:, sl], dpi[:, sl])
            parts.append(mm_nt(dpa[:, sl], wa_ref[n]) + mm_nt(dpi[:, sl], wi_ref[n]))
        dxc = dxc + jnp.concatenate(parts, axis=1)
        dx, dcw, dcb = _conv_bwd(dxc, dxcn[...], x, prev, cw_v)
        dxcn[...] = dxc
        dcw_ref[...] += dcw
        dcb_ref[...] += dcb
        dxg_ref[:, :lw] = dx.astype(dxg_ref.dtype)
        dxg_ref[:, lw:] = dgate.astype(dxg_ref.dtype)

    row = lambda col: pl.BlockSpec((tb, lw), lambda p: (nb - 1 - p, col))
    params = [cw, cb, wa, ba, wi, bi, lam]
    return pl.pallas_call(
        body, name="lru_bwd", grid=(nb,),
        in_specs=[row(xcol), pl.BlockSpec((tb, lw), lambda p: (jnp.maximum(nb - 2 - p, 0), xcol)), row(xcol + 1)]
        + [_full(p.shape) for p in params]
        + [pl.BlockSpec((1, 1, lw), lambda p: (nb - 1 - p, 0, 0)), row(docol)],
        out_specs=[pl.BlockSpec((tb, 2 * lw), lambda p: (nb - 1 - p, 0))] + [_full(p.shape) for p in params],
        out_shape=[jax.ShapeDtypeStruct((t, 2 * lw), BF16)] + [jax.ShapeDtypeStruct(p.shape, F32) for p in params],
        scratch_shapes=[pltpu.VMEM((1, lw), F32), pltpu.VMEM((tb, lw), F32)],
        compiler_params=_cparams(("arbitrary",)),
    )(proj, proj, proj, *params, hin, dmix)


def conv_silu_fwd(proj, col0, width, cw, cb):
    t = proj.shape[0]
    tb = _pick(t, (256, 128, 64))
    cbw = _pick(width, (512, 256, 128))
    off = col0 // cbw
    assert col0 % cbw == 0

    def body(x_ref, xp_ref, w_ref, b_ref, o_ref):
        prev = jnp.where(pl.program_id(1) == 0, 0.0, xp_ref[...])
        o_ref[...] = _silu(_conv(x_ref[...], prev, w_ref[...], b_ref[...]))

    return pl.pallas_call(
        body, name="conv_silu_fwd", grid=(width // cbw, t // tb),
        in_specs=[pl.BlockSpec((tb, cbw), lambda j, i: (i, off + j)),
                  pl.BlockSpec((tb, cbw), lambda j, i: (jnp.maximum(i - 1, 0), off + j)),
                  pl.BlockSpec((CONV_WIDTH, cbw), lambda j, i: (0, j)), pl.BlockSpec((1, cbw), lambda j, i: (0, j))],
        out_specs=pl.BlockSpec((tb, cbw), lambda j, i: (i, j)),
        out_shape=jax.ShapeDtypeStruct((t, width), F32),
        compiler_params=_cparams(("parallel", "arbitrary")),
    )(proj, proj, cw, cb)


def conv_silu_bwd(proj, col0, width, cw, cb, dact):
    t = proj.shape[0]
    tb = _pick(t, (256, 128, 64))
    nb = t // tb
    cbw = _pick(width, (512, 256, 128))
    off = col0 // cbw

    def body(x_ref, xp_ref, w_ref, b_ref, d_ref, dx_ref, dw_ref, db_ref, nxt):
        pid = pl.program_id(1)
        i = nb - 1 - pid

        @pl.when(pid == 0)
        def _():
            nxt[...] = jnp.zeros_like(nxt)
            dw_ref[...] = jnp.zeros_like(dw_ref)
            db_ref[...] = jnp.zeros_like(db_ref)

        x = x_ref[...]
        prev = jnp.where(i == 0, 0.0, xp_ref[...])
        w = w_ref[...]
        _, vjp = jax.vjp(_silu, _conv(x, prev, w, b_ref[...]))
        (dcv,) = vjp(d_ref[...])
        dx, dw, db = _conv_bwd(dcv, nxt[...], x, prev, w)
        nxt[...] = dcv
        dx_ref[...] = dx.astype(dx_ref.dtype)
        dw_ref[...] += dw
        db_ref[...] += db

    return pl.pallas_call(
        body, name="conv_silu_bwd", grid=(width // cbw, nb),
        in_specs=[pl.BlockSpec((tb, cbw), lambda j, p: (nb - 1 - p, off + j)),
                  pl.BlockSpec((tb, cbw), lambda j, p: (jnp.maximum(nb - 2 - p, 0), off + j)),
                  pl.BlockSpec((CONV_WIDTH, cbw), lambda j, p: (0, j)), pl.BlockSpec((1, cbw), lambda j, p: (0, j)),
                  pl.BlockSpec((tb, cbw), lambda j, p: (nb - 1 - p, j))],
        out_specs=[pl.BlockSpec((tb, cbw), lambda j, p: (nb - 1 - p, j)),
                   pl.BlockSpec((CONV_WIDTH, cbw), lambda j, p: (0, j)), pl.BlockSpec((1, cbw), lambda j, p: (0, j))],
        out_shape=[jax.ShapeDtypeStruct((t, width), BF16), jax.ShapeDtypeStruct(cw.shape, F32),
                   jax.ShapeDtypeStruct(cb.shape, F32)],
        scratch_shapes=[pltpu.VMEM((tb, cbw), F32)],
        compiler_params=_cparams(("parallel", "arbitrary")),
    )(proj, proj, cw, cb, dact)


def _dt_expand(raw, bias, e):
    return sel_r(_softplus(raw + bias), e)


def dt_fwd(proj, dtcol, bias, e):
    t = proj.shape[0]
    di = e.shape[1]
    tb = _pick(t, (512, 256, 128, 64))

    def body(r_ref, b_ref, e_ref, o_ref):
        o_ref[...] = _dt_expand(r_ref[...], b_ref[...], e_ref[...])

    return pl.pallas_call(
        body, name="dt_fwd", grid=(t // tb,),
        in_specs=[pl.BlockSpec((tb, LANE), lambda i: (i, dtcol)), _full(bias.shape), _full(e.shape)],
        out_specs=pl.BlockSpec((tb, di), lambda i: (i, 0)),
        out_shape=jax.ShapeDtypeStruct((t, di), F32),
        compiler_params=_cparams(("parallel",)),
    )(proj, bias, e)


def dt_bwd(proj, dtcol, bias, e, ddte):
    t = proj.shape[0]
    di = e.shape[1]
    tb = _pick(t, (512, 256, 128, 64))

    def body(r_ref, b_ref, e_ref, d_ref, dr_ref, db_ref):
        @pl.when(pl.program_id(0) == 0)
        def _():
            db_ref[...] = jnp.zeros_like(db_ref)

        e_v = e_ref[...]
        _, vjp = jax.vjp(lambda r, b: _dt_expand(r, b, e_v), r_ref[...], b_ref[...])
        dr, db = vjp(d_ref[...])
        dr_ref[...] = dr.astype(dr_ref.dtype)
        db_ref[...] += db

    return pl.pallas_call(
        body, name="dt_bwd", grid=(t // tb,),
        in_specs=[pl.BlockSpec((tb, LANE), lambda i: (i, dtcol)), _full(bias.shape), _full(e.shape),
                  pl.BlockSpec((tb, di), lambda i: (i, 0))],
        out_specs=[pl.BlockSpec((tb, LANE), lambda i: (i, 0)), _full(bias.shape)],
        out_shape=[jax.ShapeDtypeStruct((t, LANE), BF16), jax.ShapeDtypeStruct(bias.shape, F32)],
        compiler_params=_cparams(("arbitrary",)),
    )(proj, bias, e, ddte)


def head_expand(p, e, *, transpose=False, name):
    di = e.shape[1]

    def body(p_ref, e_ref, o_ref):
        if transpose:
            o_ref[...] = _sel_r_b(e_ref[...], p_ref[...])[0]
        else:
            o_ref[...] = sel_r(p_ref[...], e_ref[...])

    oshape = (SUBLANE, LANE) if transpose else (SUBLANE, di)
    return pl.pallas_call(
        body, name=name, in_specs=[_full(p.shape), _full(e.shape)], out_specs=_full(oshape),
        out_shape=jax.ShapeDtypeStruct(oshape, F32), compiler_params=_cparams(None), grid=(1,),
    )(p, e)


def _ssd_chunk(x, z, bm, cm, dte, st, alog, dskip, gn, tri, cmask, dmask, bd):
    L, gw = x.shape
    reps = gw // L
    a = dte * (-jnp.exp(alog))
    acs = sel_l(tri, a)
    acs_last = jnp.sum(a, axis=0, keepdims=True)
    arow = jnp.sum(acs * dmask, axis=0, keepdims=True)
    dtrow = jnp.sum(dte * dmask, axis=0, keepdims=True)
    cb = mm_nt(cm, jnp.concatenate([bm] * reps, axis=0))
    wts = cb * (jnp.exp(jnp.minimum(acs - arow, 0.0)) * cmask) * dtrow
    xbd = jnp.concatenate([x] * reps, axis=0) * bd
    xw = x * (jnp.exp(acs_last - acs) * dte)
    y = mm(wts, xbd) + mm(cm, st) * jnp.exp(acs) + dskip * x
    st_new = jnp.exp(acs_last) * st + mm_tn(bm, xw)
    return _rms(y * _silu(z), gn), st_new


def _ssd_dims(di):
    gw = di // SSD_GROUPS
    assert CHUNK == SSD_HEAD_DIM and gw % LANE == 0
    return gw, SSD_STATE


def _ssd_masks(gw):
    L = CHUNK
    r = jnp.arange(L)[:, None]
    c = jnp.arange(gw)[None, :]
    cmask = ((c % L) <= r).astype(F32)
    dmask = ((c % L) == r).astype(F32)
    rr = jnp.arange(gw)
    bd = ((rr[:, None] // L) == (rr[None, :] // L)).astype(F32)
    tri = (jnp.arange(L)[None, :] <= jnp.arange(L)[:, None]).astype(BF16)
    return tri, cmask, dmask, bd


def ssd_fwd(xs, proj, dte, alog_e, dskip_e, gn):
    t, di = dte.shape
    gw, n = _ssd_dims(di)
    L, G = CHUNK, SSD_GROUPS
    nc = t // L
    masks = _ssd_masks(gw)
    boff, coff = di // n, di // n + G

    def body(x_ref, z_ref, b_ref, c_ref, dt_ref, al_ref, ds_ref, gn_ref, tri_ref, cm_ref, dm_ref, bd_ref,
             y_ref, sp_ref, st):
        g = pl.program_id(1)

        @pl.when(pl.program_id(0) == 0)
        def _():
            st[g] = jnp.zeros((n, gw), F32)

        s_prev = st[g]
        sp_ref[0, 0] = s_prev
        y, s_new = _ssd_chunk(x_ref[...], z_ref[...], b_ref[...], c_ref[...], dt_ref[...], s_prev, al_ref[0:1, :],
                              ds_ref[0:1, :], gn_ref[...], tri_ref[...], cm_ref[...], dm_ref[...], bd_ref[...])
        y_ref[...] = y.astype(y_ref.dtype)
        st[g] = s_new

    blk = lambda w, off: pl.BlockSpec((L, w), lambda c, g: (c, off + g))
    par = lambda rows: pl.BlockSpec((rows, gw), lambda c, g: (0, g))
    return pl.pallas_call(
        body, name="ssd_fwd", grid=(nc, G),
        in_specs=[blk(gw, 0), blk(gw, 0), blk(n, boff), blk(n, coff), blk(gw, 0), par(SUBLANE), par(SUBLANE), par(1)]
        + [_full(m.shape) for m in masks],
        out_specs=[blk(gw, 0), pl.BlockSpec((1, 1, n, gw), lambda c, g: (c, g, 0, 0))],
        out_shape=[jax.ShapeDtypeStruct((t, di), BF16), jax.ShapeDtypeStruct((nc, G, n, gw), F32)],
        scratch_shapes=[pltpu.VMEM((G, n, gw), F32)],
        compiler_params=_cparams(("arbitrary", "arbitrary")),
    )(xs, proj, xs, xs, dte, alog_e, dskip_e, gn, *masks)


def ssd_bwd(xs, proj, dte, alog_e, dskip_e, gn, sprev, dy):
    t, di = dte.shape
    gw, n = _ssd_dims(di)
    L, G = CHUNK, SSD_GROUPS
    nc = t // L
    masks = _ssd_masks(gw)
    boff, coff = di // n, di // n + G

    def body(x_ref, z_ref, b_ref, c_ref, dt_ref, al_ref, ds_ref, gn_ref, tri_ref, cm_ref, dm_ref, bd_ref, sp_ref,
             dy_ref, dx_ref, dz_ref, db_ref, dc_ref, ddt_ref, dal_ref, dds_ref, dgn_ref, dst):
        g = pl.program_id(1)

        @pl.when(pl.program_id(0) == 0)
        def _():
            dst[g] = jnp.zeros((n, gw), F32)
            dal_ref[g] = jnp.zeros((1, gw), F32)
            dds_ref[g] = jnp.zeros((1, gw), F32)
            dgn_ref[g] = jnp.zeros((1, gw), F32)

        f = functools.partial(_ssd_chunk, tri=tri_ref[...], cmask=cm_ref[...], dmask=dm_ref[...], bd=bd_ref[...])
        _, vjp = jax.vjp(f, x_ref[...], z_ref[...], b_ref[...], c_ref[...], dt_ref[...], sp_ref[0, 0], al_ref[0:1, :],
                         ds_ref[0:1, :], gn_ref[...])
        dx, dz, db, dc, ddt, ds, dal, dds, dgn = vjp((dy_ref[...], dst[g]))
        dx_ref[...] = dx
        dz_ref[...] = dz.astype(dz_ref.dtype)
        db_ref[...] = db
        dc_ref[...] = dc
        ddt_ref[...] = ddt
        dst[g] = ds
        dal_ref[g] += dal
        dds_ref[g] += dds
        dgn_ref[g] += dgn

    blk = lambda w, off: pl.BlockSpec((L, w), lambda c, g: (nc - 1 - c, off + g))
    par = lambda rows: pl.BlockSpec((rows, gw), lambda c, g: (0, g))
    acc = _full((G, 1, gw))
    acc_shape = jax.ShapeDtypeStruct((G, 1, gw), F32)
    return pl.pallas_call(
        body, name="ssd_bwd", grid=(nc, G),
        in_specs=[blk(gw, 0), blk(gw, 0), blk(n, boff), blk(n, coff), blk(gw, 0), par(SUBLANE), par(SUBLANE), par(1)]
        + [_full(m.shape) for m in masks]
        + [pl.BlockSpec((1, 1, n, gw), lambda c, g: (nc - 1 - c, g, 0, 0)), blk(gw, 0)],
        out_specs=[blk(gw, 0), blk(gw, 0), blk(n, 0), blk(n, 0), blk(gw, 0), acc, acc, acc],
        out_shape=[jax.ShapeDtypeStruct((t, di), F32), jax.ShapeDtypeStruct((t, di), BF16),
                   jax.ShapeDtypeStruct((t, G * n), F32), jax.ShapeDtypeStruct((t, G * n), F32),
                   jax.ShapeDtypeStruct((t, di), F32), acc_shape, acc_shape, acc_shape],
        scratch_shapes=[pltpu.VMEM((G, n, gw), F32)],
        compiler_params=_cparams(("arbitrary", "arbitrary")),
    )(xs, proj, xs, xs, dte, alog_e, dskip_e, gn, *masks, sprev, dy)


def _rows2d(a):
    return a.reshape(-1, a.shape[-1])


def _row_tile(rows, cols):
    cap = max(SUBLANE, (1 << 19) // max(cols, 1))
    for c in (2048, 1024, 512, 256, 128, 64, 32, 16, 8):
        if c <= cap and rows % c == 0:
            return c
    return rows


def chip_sum(g, r, core, *, name):
    shape = r.shape
    cols = shape[-1]
    g4 = g.reshape(4, 2, -1, cols)
    r3 = r.reshape(4, -1, cols)
    rows = r3.shape[1]
    tr = _row_tile(rows, cols)

    def body(c_ref, g_ref, r_ref, o_ref):
        o_ref[...] = (g_ref[...].astype(F32) + r_ref[...].astype(F32)).astype(o_ref.dtype)

    out = pl.pallas_call(
        body, name=name,
        grid_spec=pltpu.PrefetchScalarGridSpec(
            num_scalar_prefetch=1, grid=(4, rows // tr),
            in_specs=[pl.BlockSpec((None, None, tr, cols), lambda j, i, c: (j, c[0], i, 0)),
                      pl.BlockSpec((None, tr, cols), lambda j, i, c: (j, i, 0))],
            out_specs=pl.BlockSpec((None, tr, cols), lambda j, i, c: (j, i, 0))),
        out_shape=jax.ShapeDtypeStruct(r3.shape, BF16), compiler_params=_cparams(("parallel", "parallel")),
    )(core.reshape(1).astype(jnp.int32), g4, r3)
    return out.reshape(shape)


def mesh_sum(p, r, chip, core, *, name):
    shape = p.shape[1:]
    cols = shape[-1]
    p3 = p.reshape(4, -1, cols)
    r3 = r.reshape(3, -1, cols)
    rows = p3.shape[1]
    tr = _row_tile(rows, 2 * cols)

    def body(c_ref, p_ref, r_ref, o_ref):
        o_ref[...] = ((p_ref[...].astype(F32) + r_ref[0].astype(F32)) + r_ref[1].astype(F32)) + r_ref[2].astype(F32)

    out = pl.pallas_call(
        body, name=name,
        grid_spec=pltpu.PrefetchScalarGridSpec(
            num_scalar_prefetch=1, grid=(rows // tr,),
            in_specs=[pl.BlockSpec((None, tr, cols), lambda i, c: (c[0], i, 0)),
                      pl.BlockSpec((3, tr, cols), lambda i, c: (0, i, 0))],
            out_specs=pl.BlockSpec((None, tr, cols), lambda i, c: (c[1], i, 0))),
        out_shape=jax.ShapeDtypeStruct((2, rows, cols), F32), compiler_params=_cparams(("parallel",)),
    )(jnp.stack([chip, core]).astype(jnp.int32), p3, r3)
    return out.reshape((2,) + shape)


def sum_leading(x, *, name):
    k, r, c = x.shape
    tr = _row_tile(r, c * k)
    def body(x_ref, o_ref):
        acc = x_ref[0]
        for i in range(1, k):
            acc = acc + x_ref[i]
        o_ref[...] = acc

    return pl.pallas_call(
        body, name=name, grid=(r // tr,), in_specs=[pl.BlockSpec((k, tr, c), lambda i: (0, i, 0))],
        out_specs=pl.BlockSpec((tr, c), lambda i: (i, 0)),
        out_shape=jax.ShapeDtypeStruct((r, c), F32), compiler_params=_cparams(("parallel",)),
    )(x)


def adamw(w, g, m, v, *, name):
    shape = w.shape
    w2, g2, m2, v2 = (_rows2d(a) for a in (w, g, m, v))
    r, c = w2.shape
    tr = _row_tile(r, 2 * c)
    row = pl.BlockSpec((tr, c), lambda i: (i, 0))
    c1 = 1.0 - ADAM_B1 ** ADAM_STEP
    c2 = 1.0 - ADAM_B2 ** ADAM_STEP

    def body(w_ref, g_ref, m_ref, v_ref, d_ref, mo_ref, vo_ref):
        gv = g_ref[...]
        mn = ADAM_B1 * m_ref[...] + (1.0 - ADAM_B1) * gv
        vn = ADAM_B2 * v_ref[...] + (1.0 - ADAM_B2) * (gv * gv)
        d_ref[...] = -ADAM_LR * ((mn / c1) / (jnp.sqrt(vn / c2) + ADAM_EPS) + ADAM_WD * w_ref[...])
        mo_ref[...] = mn
        vo_ref[...] = vn

    outs = pl.pallas_call(
        body, name=name, grid=(r // tr,), in_specs=[row] * 4, out_specs=[row] * 3,
        out_shape=[jax.ShapeDtypeStruct((r, c), F32)] * 3, compiler_params=_cparams(("parallel",)),
    )(w2, g2, m2, v2)
    return tuple(o.reshape(shape) for o in outs)


ANY = pl.BlockSpec(memory_space=pl.ANY)


def _place():
    x, y, c = lax.axis_index("x"), lax.axis_index("y"), lax.axis_index("c")
    chips = [(1 - x, y), (x, 1 - y), (1 - x, 1 - y)]
    return x, y, c, chips


def gather8(block):
    m, n = block.shape

    def body(x_ref, out_ref, send_sems, recv_sems, local_sem):
        x, y, c, chips = _place()
        me, sibling = (x, y, c), (x, y, 1 - c)

        def rows(px, py, pc):
            return out_ref.at[4 * px + 2 * py + pc]

        def copy(k, blk, to, src=None):
            return pltpu.make_async_remote_copy(
                src_ref=rows(*blk) if src is None else src, dst_ref=rows(*blk), send_sem=send_sems.at[k],
                recv_sem=recv_sems.at[k], device_id=to, device_id_type=MESH)

        mine = pltpu.make_async_copy(x_ref, rows(*me), local_sem)
        mine.start()
        first = [copy(0, me, sibling, src=x_ref)]
        first += [copy(1 + j, me, (*chip, c), src=x_ref) for j, chip in enumerate(chips)]
        for cp in first:
            cp.start()
        passed = [copy(4 + j, (*chip, c), sibling) for j, chip in enumerate(chips)]
        for j, chip in enumerate(chips):
            copy(1 + j, (*chip, c), me).wait_recv()
            passed[j].start()
        copy(0, sibling, me).wait_recv()
        for j, chip in enumerate(chips):
            copy(4 + j, (*chip, 1 - c), me).wait_recv()
        for cp in first + passed:
            cp.wait_send()
        mine.wait()

    return pl.pallas_call(
        body, name="gather8",
        out_shape=jax.ShapeDtypeStruct((8, m, n), block.dtype),
        in_specs=[pl.BlockSpec(memory_space=pltpu.VMEM)],
        out_specs=pl.BlockSpec(memory_space=pltpu.VMEM),
        scratch_shapes=[pltpu.SemaphoreType.DMA((7,)), pltpu.SemaphoreType.DMA((7,)), pltpu.SemaphoreType.DMA],
        compiler_params=pltpu.CompilerParams(vmem_limit_bytes=VMEM_LIMIT),
    )(block)


def gather_weights(shards):
    n = len(shards)

    def body(*refs):
        ins, outs = refs[:n], refs[n:2 * n]
        send_sems, recv_sems = refs[2 * n:]
        x, y, c, chips = _place()
        me = 2 * x + y
        sibling = (x, y, 1 - c)

        def copy(a, k, chip_idx, half, to, src=None):
            dst = outs[a].at[chip_idx, half]
            return pltpu.make_async_remote_copy(
                src_ref=dst if src is None else src, dst_ref=dst, send_sem=send_sems.at[a, k],
                recv_sem=recv_sems.at[a, k], device_id=to, device_id_type=MESH)

        first = [copy(a, j, me, c, (*chip, c), src=ins[a].at[c]) for a in range(n) for j, chip in enumerate(chips)]
        for cp in first:
            cp.start()
        passed = []
        for a in range(n):
            for j, (cx, cy) in enumerate(chips):
                copy(a, j, 2 * cx + cy, c, (cx, cy, c)).wait_recv()
                fw = copy(a, 3 + j, 2 * cx + cy, c, sibling)
                fw.start()
                passed.append(fw)
        for a in range(n):
            for j, (cx, cy) in enumerate(chips):
                copy(a, 3 + j, 2 * cx + cy, 1 - c, sibling).wait_recv()
        for cp in first + passed:
            cp.wait_send()

    return pl.pallas_call(
        body, name="gather_weights",
        out_shape=[jax.ShapeDtypeStruct((4,) + s.shape, s.dtype) for s in shards],
        in_specs=[ANY] * n, out_specs=[ANY] * n,
        scratch_shapes=[pltpu.SemaphoreType.DMA((n, 6)), pltpu.SemaphoreType.DMA((n, 6))],
    )(*shards)


def exchange_halves(grads):
    n = len(grads)

    def body(*refs):
        ins, outs = refs[:n], refs[n:2 * n]
        send_sems, recv_sems = refs[2 * n:]
        x, y, c, _ = _place()
        cps = [pltpu.make_async_remote_copy(
            src_ref=ins[a].at[j, 1 - c], dst_ref=outs[a].at[j], send_sem=send_sems.at[a, j],
            recv_sem=recv_sems.at[a, j], device_id=(x, y, 1 - c), device_id_type=MESH)
            for a in range(n) for j in range(4)]
        for cp in cps:
            cp.start()
        for cp in cps:
            cp.wait()

    return pl.pallas_call(
        body, name="exchange_halves",
        out_shape=[jax.ShapeDtypeStruct((4,) + g.shape[2:], g.dtype) for g in grads],
        in_specs=[ANY] * n, out_specs=[ANY] * n,
        scratch_shapes=[pltpu.SemaphoreType.DMA((n, 4)), pltpu.SemaphoreType.DMA((n, 4))],
    )(*grads)


def scatter_chips(parts):
    n = len(parts)

    def body(*refs):
        ins, outs = refs[:n], refs[n:2 * n]
        send_sems, recv_sems = refs[2 * n:]
        x, y, c, chips = _place()
        cps = [pltpu.make_async_remote_copy(
            src_ref=ins[a].at[2 * cx + cy], dst_ref=outs[a].at[j], send_sem=send_sems.at[a, j],
            recv_sem=recv_sems.at[a, j], device_id=(cx, cy, c), device_id_type=MESH)
            for a in range(n) for j, (cx, cy) in enumerate(chips)]
        for cp in cps:
            cp.start()
        for cp in cps:
            cp.wait()

    return pl.pallas_call(
        body, name="scatter_chips",
        out_shape=[jax.ShapeDtypeStruct((3,) + p.shape[1:], p.dtype) for p in parts],
        in_specs=[ANY] * n, out_specs=[ANY] * n,
        scratch_shapes=[pltpu.SemaphoreType.DMA((n, 3)), pltpu.SemaphoreType.DMA((n, 3))],
    )(*parts)


def join_halves(bufs):
    n = len(bufs)

    def body(*refs):
        outs = refs[n:2 * n]
        send_sems, recv_sems = refs[2 * n:]
        x, y, c, _ = _place()
        cps = [pltpu.make_async_remote_copy(
            src_ref=outs[a].at[c], dst_ref=outs[a].at[c], send_sem=send_sems.at[a], recv_sem=recv_sems.at[a],
            device_id=(x, y, 1 - c), device_id_type=MESH) for a in range(n)]
        for cp in cps:
            cp.start()
        for a in range(n):
            pltpu.make_async_remote_copy(
                src_ref=outs[a].at[c], dst_ref=outs[a].at[1 - c], send_sem=send_sems.at[a], recv_sem=recv_sems.at[a],
                device_id=(x, y, 1 - c), device_id_type=MESH).wait_recv()
        for cp in cps:
            cp.wait_send()

    return pl.pallas_call(
        body, name="join_halves",
        out_shape=[jax.ShapeDtypeStruct(h.shape, h.dtype) for h in bufs],
        in_specs=[ANY] * n, out_specs=[ANY] * n, input_output_aliases={a: a for a in range(n)},
        scratch_shapes=[pltpu.SemaphoreType.DMA((n,)), pltpu.SemaphoreType.DMA((n,))],
    )(*bufs)


INPUTS = ['x'] + WEIGHTS + ['loss_target'] + ['m_' + n for n in WEIGHTS] + ['v_' + n for n in WEIGHTS]


def _round_up(n, m):
    return -(-n // m) * m


def _pack(arrs):
    flat = jnp.concatenate([a.reshape(-1) for a in arrs])
    n = _round_up(flat.shape[0], 512 * LANE)
    return jnp.pad(flat, (0, n - flat.shape[0])).reshape(-1, LANE)


def _unpack(block, shapes):
    flat = block.reshape(-1)
    out, o = [], 0
    for s in shapes:
        n = math.prod(s)
        out.append(flat[o:o + n].reshape(s))
        o += n
    return out


def _cols(g):
    return g.transpose(1, 0, 2).reshape(g.shape[1], -1)


def _uncols(w):
    return w.reshape(w.shape[0], 4, -1).transpose(1, 0, 2)


def _pad_cols(w, total):
    return jnp.pad(w, ((0, 0), (0, total - w.shape[1])))


def kernel(*args):
    a = dict(zip(INPUTS, args))
    x, tgt = a['x'][0], a['loss_target'][0]
    t, d = x.shape
    xi, yi, ci = lax.axis_index("x"), lax.axis_index("y"), lax.axis_index("c")
    chip = 2 * xi + yi
    dk, dv, hk, hv = _gla_dims(d)
    lw = d // 2
    di = 2 * d
    nh = di // SSD_HEAD_DIM
    gn_w = SSD_GROUPS * SSD_STATE
    conv_dim = di + 2 * gn_w
    rank = GLA_GATE_RANK
    wq = 2 * dk + 2 * dv
    ev_used = wq + 2 * lw + rank
    ev_tot = _round_up(wq + 2 * lw + LANE, 512)
    od_used = di + conv_dim + nh
    od_tot = _round_up(di + conv_dim + _round_up(nh, LANE), 512)
    glr_col, xcol, dtcol = (wq + 2 * lw) // LANE, wq // lw, (di + conv_dim) // LANE
    assert wq % lw == 0 and nh <= LANE

    def halves(w):
        w = w.astype(BF16)
        if w.shape[0] == 2:
            return w
        w = w[0]
        return w.reshape((2, w.shape[0] // 2) + w.shape[1:])

    own = [halves(a[n]) for n in BIG]
    gw = {n: lax.dynamic_update_index_in_dim(g, o, chip, 0) for n, g, o in zip(BIG, gather_weights(own), own)}
    w_ev_in = _cols(gw['ev_w_in'].reshape((4, d, -1)))
    cuts = [dk, 2 * dk, 2 * dk + dv, wq, wq + rank, wq + rank + lw]
    sq, sk, sv, sg, sglr, sxb, sgb = jnp.split(w_ev_in, cuts, axis=1)
    w_ev_in_p = _pad_cols(jnp.concatenate([sq, sk, sv, sg, sxb, sgb, sglr], axis=1), ev_tot)
    w_ev_out = gw['ev_w_out'].reshape(-1, d)
    w_od_in_p = _pad_cols(_cols(gw['od_w_in'].reshape((4, d, -1))), od_tot)
    w_od_out = gw['od_w_out'].reshape(-1, d)
    w_gate = [_cols(gw['ffn_w_gate'][:, l]) for l in range(2)]
    w_up = [_cols(gw['ffn_w_up'][:, l]) for l in range(2)]
    w_down = [gw['ffn_w_down'][:, l].reshape(-1, d) for l in range(2)]

    sh_names = list(SMALL_SHARDED)
    sh_shapes = [a[n].shape for n in sh_names]
    g8 = gather8(_pack([a[n] for n in sh_names]))
    per_chip = [_unpack(g8[2 * j], sh_shapes) for j in range(4)]
    full = {n: jnp.concatenate([per_chip[j][i] for j in range(4)], axis=SMALL_SHARDED[n])
            for i, n in enumerate(sh_names)}

    wg_p = jnp.zeros((LANE, dk), F32).at[:rank].set(full['ev_gla_w_gate'][0])
    bg, wn = a['ev_gla_b_gate'], a['ev_gla_w_onorm']
    lru_p = [full['ev_lru_conv_w'][0], a['ev_lru_conv_b'], a['ev_lru_w_a'][0], a['ev_lru_b_a'], a['ev_lru_w_i'][0],
             a['ev_lru_b_i'], a['ev_lru_lam']]
    od_cw, od_cb, od_gn = full['od_conv_w'][0], full['od_conv_b'], full['od_gnorm']
    heads = jnp.arange(LANE)[:, None]
    e_mat = ((jnp.arange(di)[None, :] // SSD_HEAD_DIM == heads) & (heads < nh)).astype(BF16)
    row8 = lambda p: jnp.zeros((SUBLANE, LANE), F32).at[0, :nh].set(p[0])
    dt_bias_p = jnp.zeros((1, LANE), F32).at[0, :nh].set(a['od_dt_bias'][0])
    alog_e = head_expand(row8(a['od_a_log']), e_mat, name="expand_a_log")
    dskip_e = head_expand(row8(a['od_d_skip']), e_mat, name="expand_d_skip")

    h0 = rms_fwd(x, a['ev_norm'], name="rms_ev")
    proj = matmul(h0, w_ev_in_p, name="ev_in")
    o_gla, sp_gla = gla_fwd(proj, glr_col, wg_p, bg, wn, dv)
    o_lru, hin = lru_fwd(proj, xcol, lw, *lru_p)
    x1 = matmul(o_gla, w_ev_out[:dv], add=x, name="ev_out_a")
    x1 = matmul(o_lru, w_ev_out[dv:], add=x1, name="ev_out_b")

    def ffn_fwd(xin, l):
        h = rms_fwd(xin, a['ffn_norm'][l:l + 1], name=f"rms_ffn{l}")
        gate = matmul(h, w_gate[l], name=f"ffn{l}_gate")
        up = matmul(h, w_up[l], name=f"ffn{l}_up")
        act = swiglu_fwd(gate, up, name=f"ffn{l}_act")
        return h, gate, up, act, matmul(act, w_down[l], add=xin, name=f"ffn{l}_down")

    h1, gate0, up0, act0, x2 = ffn_fwd(x1, 0)
    h2 = rms_fwd(x2, full['od_norm'], name="rms_od")
    proj2 = matmul(h2, w_od_in_p, name="od_in")
    xs = conv_silu_fwd(proj2, di, conv_dim, od_cw, od_cb)
    dte = dt_fwd(proj2, dtcol, dt_bias_p, e_mat)
    y_ssd, sp_ssd = ssd_fwd(xs, proj2, dte, alog_e, dskip_e, od_gn)
    x3 = matmul(y_ssd, w_od_out, add=x2, name="od_out")
    h3, gate1, up1, act1, x4 = ffn_fwd(x3, 1)
    loss_p, dx4, d_final = loss_head(x4, a['final_norm'][None], tgt, name="loss_head")

    def ffn_bwd(dxo, xin, h, gate, up, act, l):
        dact = matmul(dxo, w_down[l], tb=True, name=f"ffn{l}_d_act")
        d_down = matmul(act, dxo, ta=True, out_dtype=BF16, name=f"ffn{l}_dw_down")
        dg, du = swiglu_bwd(gate, up, dact, name=f"ffn{l}_act_bwd")
        dh = matmul(dg, w_gate[l], tb=True, name=f"ffn{l}_dh_gate")
        dh = matmul(du, w_up[l], tb=True, add=dh, name=f"ffn{l}_dh_up")
        d_gate = matmul(h, dg, ta=True, out_dtype=BF16, name=f"ffn{l}_dw_gate")
        d_up = matmul(h, du, ta=True, out_dtype=BF16, name=f"ffn{l}_dw_up")
        dxi, d_norm = rms_bwd(xin, a['ffn_norm'][l:l + 1], dh, dxo, name=f"rms_ffn{l}_bwd")
        return dxi, d_norm, d_gate, d_up, d_down

    dx3, d_fn1, d_gate1, d_up1, d_down1 = ffn_bwd(dx4, x3, h3, gate1, up1, act1, 1)
    dy = matmul(dx3, w_od_out, tb=True, name="od_out_dy")
    d_od_out = matmul(y_ssd, dx3, ta=True, out_dtype=BF16, name="od_out_dw")
    dxs_x, dz, dxs_b, dxs_c, ddte, dal, dds, dgn = ssd_bwd(xs, proj2, dte, alog_e, dskip_e, od_gn, sp_ssd, dy)
    dxs = jnp.concatenate([dxs_x, dxs_b, dxs_c], axis=1)
    dxbc, d_od_cw, d_od_cb = conv_silu_bwd(proj2, di, conv_dim, od_cw, od_cb, dxs)
    ddt_raw, d_dt_bias = dt_bwd(proj2, dtcol, dt_bias_p, e_mat, ddte)
    dproj2 = jnp.concatenate([dz, dxbc, ddt_raw, jnp.zeros((t, od_tot - di - conv_dim - LANE), BF16)], axis=1)
    dh2 = matmul(dproj2, w_od_in_p, tb=True, name="od_in_dh")
    d_od_in = matmul(h2, dproj2, ta=True, out_dtype=BF16, name="od_in_dw")[:, :od_used]
    dx2, d_od_norm = rms_bwd(x2, full['od_norm'], dh2, dx3, name="rms_od_bwd")
    to8 = lambda acc: jnp.zeros((SUBLANE, di), F32).at[0].set(acc.reshape(-1))
    d_a_log = head_expand(to8(dal), e_mat, transpose=True, name="reduce_a_log")[0:1, :nh]
    d_d_skip = head_expand(to8(dds), e_mat, transpose=True, name="reduce_d_skip")[0:1, :nh]

    dx1, d_fn0, d_gate0, d_up0, d_down0 = ffn_bwd(dx2, x1, h1, gate0, up0, act0, 0)
    dmix = matmul(dx1, w_ev_out, tb=True, name="ev_out_dmix")
    d_ev_out = jnp.concatenate([matmul(o_gla, dx1, ta=True, out_dtype=BF16, name="ev_out_dw_a"),
                                matmul(o_lru, dx1, ta=True, out_dtype=BF16, name="ev_out_dw_b")], axis=0)
    dqkvg, dglr, d_wg, d_bg, d_wn = gla_bwd(proj, glr_col, wg_p, bg, wn, sp_gla, dmix, dv)
    dxg, *d_lru = lru_bwd(proj, xcol, lw, *lru_p, hin, dmix, 1)
    dproj = jnp.concatenate([dqkvg, dxg, dglr, jnp.zeros((t, ev_tot - wq - 2 * lw - LANE), BF16)], axis=1)
    dh0 = matmul(dproj, w_ev_in_p, tb=True, name="ev_in_dh")
    d_ev_in_p = matmul(h0, dproj, ta=True, out_dtype=BF16, name="ev_in_dw")
    d_ev_in = jnp.concatenate([d_ev_in_p[:, :wq], d_ev_in_p[:, wq + 2 * lw:wq + 2 * lw + rank],
                               d_ev_in_p[:, wq:wq + 2 * lw]], axis=1)
    dx0, d_ev_norm = rms_bwd(x, a['ev_norm'], dh0, dx1, name="rms_ev_bwd")

    def shard_layout(dw, col):
        p = _uncols(dw) if col else dw.reshape((4, dw.shape[0] // 4) + dw.shape[1:])
        return p.reshape((4, 2, p.shape[1] // 2) + p.shape[2:])

    layers = lambda ws, col: jnp.stack([_uncols(w) if col else w.reshape((4, w.shape[0] // 4) + w.shape[1:])
                                        for w in ws], axis=1)
    big_g = {
        'ev_w_in': shard_layout(d_ev_in, True), 'ev_w_out': shard_layout(d_ev_out, False),
        'od_w_in': shard_layout(d_od_in, True), 'od_w_out': shard_layout(d_od_out, False),
        'ffn_w_gate': layers([d_gate0, d_gate1], True), 'ffn_w_up': layers([d_up0, d_up1], True),
        'ffn_w_down': layers([d_down0, d_down1], False),
    }
    g_list = [big_g[n] for n in BIG]
    from_sibling = exchange_halves(g_list)
    chip_sums = [chip_sum(g, r, ci, name=f"chip_sum_{n}") for n, g, r in zip(BIG, g_list, from_sibling)]
    from_chips = scatter_chips(chip_sums)
    half = [mesh_sum(p, r, chip, ci, name=f"mesh_sum_{n}") for n, p, r in zip(BIG, chip_sums, from_chips)]
    grad = {n: g.reshape(a[n].shape) for n, g in zip(BIG, join_halves(half))}

    small_g = {
        'ev_norm': d_ev_norm, 'ev_gla_w_gate': d_wg[:rank][None], 'ev_gla_b_gate': d_bg, 'ev_gla_w_onorm': d_wn,
        'ev_lru_conv_w': d_lru[0][None], 'ev_lru_conv_b': d_lru[1], 'ev_lru_w_a': d_lru[2][None],
        'ev_lru_b_a': d_lru[3], 'ev_lru_w_i': d_lru[4][None], 'ev_lru_b_i': d_lru[5], 'ev_lru_lam': d_lru[6],
        'od_norm': d_od_norm, 'od_conv_w': d_od_cw[None], 'od_conv_b': d_od_cb, 'od_dt_bias': d_dt_bias[:, :nh],
        'od_a_log': d_a_log, 'od_d_skip': d_d_skip, 'od_gnorm': dgn.reshape(1, di),
        'ffn_norm': jnp.concatenate([d_fn0, d_fn1], axis=0), 'final_norm': d_final[0],
    }
    full_shapes = [small_g[n].shape for n in SMALL]
    summed = sum_leading(gather8(_pack([small_g[n] for n in SMALL])), name="sum_devices")
    for n, g in zip(SMALL, _unpack(summed, full_shapes)):
        if n in SMALL_SHARDED:
            ax = SMALL_SHARDED[n]
            sz = a[n].shape[ax]
            g = lax.dynamic_slice_in_dim(g, chip * sz, sz, axis=ax)
        grad[n] = g

    delta, new_m, new_v = {}, {}, {}
    for n in BIG:
        delta[n], new_m[n], new_v[n] = adamw(a[n], grad[n], a['m_' + n], a['v_' + n], name=f"adamw_{n}")
    shapes = [a[n].shape for n in SMALL]
    packed = [_pack([src[n] if pre is None else a[pre + n] for n in SMALL])
              for src, pre in ((a, None), (grad, None), (None, 'm_'), (None, 'v_'))]
    for outd, blk in zip((delta, new_m, new_v), adamw(*packed, name="adamw_small")):
        outd.update(zip(SMALL, _unpack(blk, shapes)))

    loss = lax.psum(loss_p[0, 0], ("x", "y", "c"))
    return (loss, dx0[None], *[grad[n] for n in WEIGHTS], *[delta[n] for n in WEIGHTS],
            *[new_m[n] for n in WEIGHTS], *[new_v[n] for n in WEIGHTS])
```

```python
import functools
import math

import jax
import jax.numpy as jnp
from jax import lax
from jax.experimental import pallas as pl
from jax.experimental.pallas import tpu as pltpu

F32 = jnp.float32
BF16 = jnp.bfloat16
MXU_DTYPE = jnp.bfloat16

NORM_EPS = 1e-6
CONV_WIDTH = 4
GLA_HEADS = 4
GLA_GATE_RANK = 16
GLA_GATE_NORM = 16.0
CHUNK = 64
LRU_BLOCK = 128
LRU_C = 8.0
SSD_HEAD_DIM = 64
SSD_GROUPS = 8
SSD_STATE = 128
ADAM_LR, ADAM_B1, ADAM_B2, ADAM_EPS, ADAM_WD, ADAM_STEP = 0.001, 0.9, 0.999, 1e-08, 0.01, 10

LANE = 128
SUBLANE = 8
VMEM_LIMIT = 48 * 1024 * 1024
MAX_TK = 2816
MESH = pl.DeviceIdType.MESH

WEIGHTS = ['ev_norm', 'ev_w_in', 'ev_gla_w_gate', 'ev_gla_b_gate', 'ev_gla_w_onorm', 'ev_lru_conv_w', 'ev_lru_conv_b',
           'ev_lru_w_a', 'ev_lru_b_a', 'ev_lru_w_i', 'ev_lru_b_i', 'ev_lru_lam', 'ev_w_out', 'od_norm', 'od_w_in',
           'od_conv_w', 'od_conv_b', 'od_dt_bias', 'od_a_log', 'od_d_skip', 'od_gnorm', 'od_w_out', 'ffn_norm',
           'ffn_w_gate', 'ffn_w_up', 'ffn_w_down', 'final_norm']
BIG = ['ev_w_in', 'ev_w_out', 'od_w_in', 'od_w_out', 'ffn_w_gate', 'ffn_w_up', 'ffn_w_down']
SMALL_SHARDED = {'ev_gla_w_gate': 2, 'ev_lru_conv_w': 2, 'od_norm': 1, 'od_conv_w': 2, 'od_conv_b': 1, 'od_gnorm': 1}
SMALL = [n for n in WEIGHTS if n not in BIG]


def _cparams(sem=None, **kw):
    return pltpu.CompilerParams(dimension_semantics=sem, vmem_limit_bytes=VMEM_LIMIT, **kw)


def _full(shape):
    n = len(shape)
    return pl.BlockSpec(shape, lambda *_: (0,) * n)


def _pick(dim, cands):
    for c in cands:
        if dim % c == 0:
            return c
    return dim


def _dot(a, b, ca, cb):
    return lax.dot_general(a.astype(MXU_DTYPE), b.astype(MXU_DTYPE), (((ca,), (cb,)), ((), ())),
                           preferred_element_type=F32)


@jax.custom_vjp
def mm(a, b):
    return _dot(a, b, 1, 0)


def _mm_f(a, b):
    return mm(a, b), (a, b)


def _mm_b(res, g):
    a, b = res
    return mm_nt(g, b).astype(a.dtype), mm_tn(a, g).astype(b.dtype)


@jax.custom_vjp
def mm_nt(a, b):
    return _dot(a, b, 1, 1)


def _mm_nt_f(a, b):
    return mm_nt(a, b), (a, b)


def _mm_nt_b(res, g):
    a, b = res
    return mm(g, b).astype(a.dtype), mm_tn(g, a).astype(b.dtype)


@jax.custom_vjp
def mm_tn(a, b):
    return _dot(a, b, 0, 0)


def _mm_tn_f(a, b):
    return mm_tn(a, b), (a, b)


def _mm_tn_b(res, g):
    a, b = res
    return mm_nt(b, g).astype(a.dtype), mm(a, g).astype(b.dtype)


mm.defvjp(_mm_f, _mm_b)
mm_nt.defvjp(_mm_nt_f, _mm_nt_b)
mm_tn.defvjp(_mm_tn_f, _mm_tn_b)


def _split3(a):
    h = a.astype(BF16)
    r = a - h.astype(F32)
    m = r.astype(BF16)
    l = (r - m.astype(F32)).astype(BF16)
    return h, m, l


def _exact_dot(t, a, ca, cb):
    out = None
    for p in _split3(a):
        d = lax.dot_general(t, p, (((ca,), (cb,)), ((), ())), preferred_element_type=F32)
        out = d if out is None else out + d
    return out


@jax.custom_vjp
def sel_l(t, a):
    return _exact_dot(t, a, 1, 0)


def _sel_l_f(t, a):
    return sel_l(t, a), t


def _sel_l_b(t, g):
    return jnp.zeros_like(t), _exact_dot(t, g, 0, 0)


sel_l.defvjp(_sel_l_f, _sel_l_b)


@jax.custom_vjp
def sel_r(a, t):
    out = None
    for p in _split3(a):
        d = lax.dot_general(p, t, (((1,), (0,)), ((), ())), preferred_element_type=F32)
        out = d if out is None else out + d
    return out


def _sel_r_f(a, t):
    return sel_r(a, t), t


def _sel_r_b(t, g):
    out = None
    for p in _split3(g):
        d = lax.dot_general(p, t, (((1,), (1,)), ((), ())), preferred_element_type=F32)
        out = d if out is None else out + d
    return out, jnp.zeros_like(t)


sel_r.defvjp(_sel_r_f, _sel_r_b)


def _sigmoid(x):
    return 1.0 / (1.0 + jnp.exp(-x))


def _silu(x):
    return x * _sigmoid(x)


def _softplus(x):
    return jnp.maximum(x, 0.0) + jnp.log(1.0 + jnp.exp(-jnp.abs(x)))


def _log_sigmoid(x):
    return -_softplus(-x)


def _gelu_tanh(x):
    c = math.sqrt(2.0 / math.pi)
    return 0.5 * x * (1.0 + jnp.tanh(c * (x + 0.044715 * (x * x * x))))


def _rms(x, w):
    return x * lax.rsqrt(jnp.mean(x * x, axis=-1, keepdims=True) + NORM_EPS) * w


def _tri(n, dtype=BF16):
    r = lax.broadcasted_iota(jnp.int32, (n, n), 0)
    c = lax.broadcasted_iota(jnp.int32, (n, n), 1)
    return (c <= r).astype(dtype)


def matmul(a, b, *, ta=False, tb=False, add=None, out_dtype=F32, name):
    m, k = (a.shape[1], a.shape[0]) if ta else a.shape
    k2, n = (b.shape[1], b.shape[0]) if tb else b.shape
    assert k == k2, (a.shape, b.shape, ta, tb)
    tm = _pick(m, (1024, 512, 256, 128))
    tn = _pick(n, (1024, 512, 256, 128))
    tk = k if k <= MAX_TK else max(c for c in range(LANE, MAX_TK + 1, LANE) if k % c == 0)
    nk = k // tk

    def body(*refs):
        if add is None:
            a_ref, b_ref, o_ref, acc = refs
        else:
            a_ref, b_ref, add_ref, o_ref, acc = refs
        kk = pl.program_id(2)

        @pl.when(kk == 0)
        def _():
            acc[...] = jnp.zeros_like(acc)

        acc[...] += _dot(a_ref[...], b_ref[...], 0 if ta else 1, 1 if tb else 0)

        @pl.when(kk == nk - 1)
        def _():
            r = acc[...]
            if add is not None:
                r = r + add_ref[...].astype(F32)
            o_ref[...] = r.astype(out_dtype)

    a_spec = pl.BlockSpec((tk, tm), lambda i, j, kk: (kk, i)) if ta else pl.BlockSpec((tm, tk), lambda i, j, kk: (i, kk))
    b_spec = pl.BlockSpec((tn, tk), lambda i, j, kk: (j, kk)) if tb else pl.BlockSpec((tk, tn), lambda i, j, kk: (kk, j))
    in_specs, args = [a_spec, b_spec], [a, b]
    if add is not None:
        in_specs.append(pl.BlockSpec((tm, tn), lambda i, j, kk: (i, j)))
        args.append(add)
    return pl.pallas_call(
        body, name=name, grid=(m // tm, n // tn, nk), in_specs=in_specs,
        out_specs=pl.BlockSpec((tm, tn), lambda i, j, kk: (i, j)),
        out_shape=jax.ShapeDtypeStruct((m, n), out_dtype),
        scratch_shapes=[pltpu.VMEM((tm, tn), F32)],
        compiler_params=_cparams(("parallel", "parallel", "arbitrary")),
    )(*args)


def rms_fwd(x, w, *, name):
    t, d = x.shape
    tb = _pick(t, (256, 128, 64))

    def body(x_ref, w_ref, o_ref):
        o_ref[...] = _rms(x_ref[...], w_ref[...]).astype(o_ref.dtype)

    return pl.pallas_call(
        body, name=name, grid=(t // tb,),
        in_specs=[pl.BlockSpec((tb, d), lambda i: (i, 0)), _full((1, d))],
        out_specs=pl.BlockSpec((tb, d), lambda i: (i, 0)),
        out_shape=jax.ShapeDtypeStruct((t, d), BF16),
        compiler_params=_cparams(("parallel",)),
    )(x, w)


def rms_bwd(x, w, dh, dres, *, name):
    t, d = x.shape
    tb = _pick(t, (256, 128, 64))

    def body(x_ref, w_ref, dh_ref, dres_ref, dx_ref, dw_ref):
        @pl.when(pl.program_id(0) == 0)
        def _():
            dw_ref[...] = jnp.zeros_like(dw_ref)

        _, vjp = jax.vjp(_rms, x_ref[...], w_ref[...])
        dx, dw = vjp(dh_ref[...].astype(F32))
        dx_ref[...] = dx + dres_ref[...]
        dw_ref[...] += dw

    row = pl.BlockSpec((tb, d), lambda i: (i, 0))
    return pl.pallas_call(
        body, name=name, grid=(t // tb,),
        in_specs=[row, _full((1, d)), row, row],
        out_specs=[row, _full((1, d))],
        out_shape=[jax.ShapeDtypeStruct((t, d), F32), jax.ShapeDtypeStruct((1, d), F32)],
        compiler_params=_cparams(("arbitrary",)),
    )(x, w, dh, dres)


def _swi(g, u):
    return _silu(g) * u


def swiglu_fwd(gate, up, *, name):
    t, f = gate.shape
    tb = _pick(t, (256, 128, 64))
    row = pl.BlockSpec((tb, f), lambda i: (i, 0))

    def body(g_ref, u_ref, o_ref):
        o_ref[...] = _swi(g_ref[...], u_ref[...]).astype(o_ref.dtype)

    return pl.pallas_call(
        body, name=name, grid=(t // tb,), in_specs=[row, row], out_specs=row,
        out_shape=jax.ShapeDtypeStruct((t, f), BF16), compiler_params=_cparams(("parallel",)),
    )(gate, up)


def swiglu_bwd(gate, up, dact, *, name):
    t, f = gate.shape
    tb = _pick(t, (128, 64))
    row = pl.BlockSpec((tb, f), lambda i: (i, 0))

    def body(g_ref, u_ref, d_ref, dg_ref, du_ref):
        _, vjp = jax.vjp(_swi, g_ref[...], u_ref[...])
        dg, du = vjp(d_ref[...])
        dg_ref[...] = dg.astype(dg_ref.dtype)
        du_ref[...] = du.astype(du_ref.dtype)

    return pl.pallas_call(
        body, name=name, grid=(t // tb,), in_specs=[row, row, row], out_specs=[row, row],
        out_shape=[jax.ShapeDtypeStruct((t, f), BF16)] * 2, compiler_params=_cparams(("parallel",)),
    )(gate, up, dact)


def loss_head(x, w, target, *, name):
    t, d = x.shape
    tb = _pick(t, (256, 128, 64))

    def f(xv, wv, tv):
        y = _rms(xv, wv)
        e = y - tv
        return 0.5 * jnp.sum(jnp.mean(e * e, axis=-1, keepdims=True), axis=0, keepdims=True)

    def body(x_ref, w_ref, t_ref, l_ref, dx_ref, dw_ref):
        @pl.when(pl.program_id(0) == 0)
        def _():
            l_ref[...] = jnp.zeros_like(l_ref)
            dw_ref[...] = jnp.zeros_like(dw_ref)

        val, vjp = jax.vjp(lambda a, b: f(a, b, t_ref[...]), x_ref[...], w_ref[...])
        dx, dw = vjp(jnp.ones((1, 1), F32))
        l_ref[...] += jnp.broadcast_to(val, l_ref.shape)
        dx_ref[...] = dx
        dw_ref[...] += dw

    row = pl.BlockSpec((tb, d), lambda i: (i, 0))
    return pl.pallas_call(
        body, name=name, grid=(t // tb,),
        in_specs=[row, _full((1, d)), row],
        out_specs=[_full((SUBLANE, LANE)), row, _full((1, d))],
        out_shape=[jax.ShapeDtypeStruct((SUBLANE, LANE), F32), jax.ShapeDtypeStruct((t, d), F32),
                   jax.ShapeDtypeStruct((1, d), F32)],
        compiler_params=_cparams(("arbitrary",)),
    )(x, w, target)


def _gla_chunk(q, k, v, g, glr, st, wg, bg, wn, tri):
    L, hk = q.shape
    la = _log_sigmoid(mm(glr, wg) + bg) / GLA_GATE_NORM
    bcum = sel_l(tri, la)
    b_last = jnp.sum(la, axis=0, keepdims=True)
    rows = lax.broadcasted_iota(jnp.int32, (L, 1), 0)
    b_mid = jnp.sum(jnp.where(rows <= L // 2, la, 0.0), axis=0, keepdims=True)
    qs = q * (hk ** -0.5)
    q_in = qs * jnp.exp(bcum - b_mid)
    k_in = k * jnp.exp(b_mid - bcum)
    scores = mm_nt(q_in, k_in) * tri.astype(F32)
    o_intra = mm(scores, v)
    k_st = k * jnp.exp(b_last - bcum)
    d_st = mm_tn(v, k_st)
    o_inter = mm_nt(qs * jnp.exp(bcum), st)
    st_new = jnp.exp(b_last) * st + d_st
    o = _rms(o_intra + o_inter, wn) * _silu(g)
    return o, st_new


def _gla_dims(d):
    dv = d // 2
    dk = dv // 2
    return dk, dv, dk // GLA_HEADS, dv // GLA_HEADS


def gla_fwd(proj, glr_col, wg, bg, wn, dv):
    t = proj.shape[0]
    dk, dv, hk, hv = _gla_dims(2 * dv)
    L, H = CHUNK, GLA_HEADS
    nc = t // L
    wq = 2 * dk + 2 * dv

    def body(p_ref, glr_ref, wg_ref, bg_ref, wn_ref, o_ref, sp_ref, st):
        @pl.when(pl.program_id(0) == 0)
        def _():
            st[...] = jnp.zeros_like(st)

        tri = _tri(L)
        glr = glr_ref[...]
        for h in range(H):
            q = p_ref[:, h * hk:(h + 1) * hk]
            k = p_ref[:, dk + h * hk:dk + (h + 1) * hk]
            v = p_ref[:, 2 * dk + h * hv:2 * dk + (h + 1) * hv]
            g = p_ref[:, 2 * dk + dv + h * hv:2 * dk + dv + (h + 1) * hv]
            s_prev = st[h]
            sp_ref[0, h] = s_prev
            o, s_new = _gla_chunk(q, k, v, g, glr, s_prev, wg_ref[:, h * hk:(h + 1) * hk],
                                  bg_ref[:, h * hk:(h + 1) * hk], wn_ref[...], tri)
            o_ref[:, h * hv:(h + 1) * hv] = o.astype(o_ref.dtype)
            st[h] = s_new

    return pl.pallas_call(
        body, name="gla_fwd", grid=(nc,),
        in_specs=[pl.BlockSpec((L, wq), lambda c: (c, 0)), pl.BlockSpec((L, LANE), lambda c: (c, glr_col)),
                  _full(wg.shape), _full(bg.shape), _full(wn.shape)],
        out_specs=[pl.BlockSpec((L, dv), lambda c: (c, 0)), pl.BlockSpec((1, H, hv, hk), lambda c: (c, 0, 0, 0))],
        out_shape=[jax.ShapeDtypeStruct((t, dv), BF16), jax.ShapeDtypeStruct((nc, H, hv, hk), F32)],
        scratch_shapes=[pltpu.VMEM((H, hv, hk), F32)],
        compiler_params=_cparams(("arbitrary",)),
    )(proj, proj, wg, bg, wn)


def gla_bwd(proj, glr_col, wg, bg, wn, sprev, do, dv, gla_w):
    t = proj.shape[0]
    dk, _, hk, hv = _gla_dims(2 * dv)
    L, H = CHUNK, GLA_HEADS
    nc = t // L
    wq = 2 * dk + 2 * dv

    def body(p_ref, glr_ref, wg_ref, bg_ref, wn_ref, sp_ref, do_ref, dp_ref, dwg_ref, dbg_ref, dwn_ref, dst):
        @pl.when(pl.program_id(0) == 0)
        def _():
            dst[...] = jnp.zeros_like(dst)
            dwg_ref[...] = jnp.zeros_like(dwg_ref)
            dbg_ref[...] = jnp.zeros_like(dbg_ref)
            dwn_ref[...] = jnp.zeros_like(dwn_ref)

        tri = _tri(L)
        glr = glr_ref[...]
        dglr = jnp.zeros_like(glr)
        for h in range(H):
            ks = slice(h * hk, (h + 1) * hk)
            q = p_ref[:, ks]
            k = p_ref[:, dk + h * hk:dk + (h + 1) * hk]
            v = p_ref[:, 2 * dk + h * hv:2 * dk + (h + 1) * hv]
            g = p_ref[:, 2 * dk + dv + h * hv:2 * dk + dv + (h + 1) * hv]
            f = functools.partial(_gla_chunk, tri=tri)
            _, vjp = jax.vjp(f, q, k, v, g, glr, sp_ref[0, h], wg_ref[:, ks], bg_ref[:, ks], wn_ref[...])
            dq, dkk, dvv, dg, dgl, ds, dwg, dbg, dwn = vjp((do_ref[:, h * hv:(h + 1) * hv], dst[h]))
            dp_ref[:, ks] = dq.astype(dp_ref.dtype)
            dp_ref[:, dk + h * hk:dk + (h + 1) * hk] = dkk.astype(dp_ref.dtype)
            dp_ref[:, 2 * dk + h * hv:2 * dk + (h + 1) * hv] = dvv.astype(dp_ref.dtype)
            dp_ref[:, 2 * dk + dv + h * hv:2 * dk + dv + (h + 1) * hv] = dg.astype(dp_ref.dtype)
            dglr = dglr + dgl
            dst[h] = ds
            dwg_ref[:, ks] += dwg
            dbg_ref[:, ks] += dbg
            dwn_ref[...] += dwn
        dp_ref[:, wq:wq + LANE] = dglr.astype(dp_ref.dtype)
        if gla_w > wq + LANE:
            dp_ref[:, wq + LANE:] = jnp.zeros((L, gla_w - wq - LANE), dp_ref.dtype)

    rev = lambda c: nc - 1 - c
    return pl.pallas_call(
        body, name="gla_bwd", grid=(nc,),
        in_specs=[pl.BlockSpec((L, wq), lambda c: (rev(c), 0)), pl.BlockSpec((L, LANE), lambda c: (rev(c), glr_col)),
                  _full(wg.shape), _full(bg.shape), _full(wn.shape),
                  pl.BlockSpec((1, H, hv, hk), lambda c: (rev(c), 0, 0, 0)),
                  pl.BlockSpec((L, dv), lambda c: (rev(c), 0))],
        out_specs=[pl.BlockSpec((L, gla_w), lambda c: (rev(c), 0)),
                   _full(wg.shape), _full(bg.shape), _full(wn.shape)],
        out_shape=[jax.ShapeDtypeStruct(proj.shape, BF16),
                   jax.ShapeDtypeStruct(wg.shape, F32), jax.ShapeDtypeStruct(bg.shape, F32),
                   jax.ShapeDtypeStruct(wn.shape, F32)],
        scratch_shapes=[pltpu.VMEM((H, hv, hk), F32)],
        compiler_params=_cparams(("arbitrary",)),
    )(proj, proj, wg, bg, wn, sprev, do)


def _shift_down(x, tail, s):
    if s == 0:
        return x
    r = pltpu.roll(x, s, 0)
    rows = lax.broadcasted_iota(jnp.int32, tail.shape, 0)
    top = jnp.where(rows < s, pltpu.roll(tail, s, 0), r[:SUBLANE])
    return jnp.concatenate([top, r[SUBLANE:]], axis=0)


def _shift_up(x, head, s):
    if s == 0:
        return x
    n = x.shape[0]
    r = pltpu.roll(x, n - s, 0)
    rows = lax.broadcasted_iota(jnp.int32, head.shape, 0)
    bottom = jnp.where(rows >= SUBLANE - s, pltpu.roll(head, SUBLANE - s, 0), r[n - SUBLANE:])
    return jnp.concatenate([r[:n - SUBLANE], bottom], axis=0)


def _conv(x, prev, w, b):
    y = b
    for k in range(CONV_WIDTH):
        y = y + w[k:k + 1, :] * _shift_down(x, prev, CONV_WIDTH - 1 - k)
    return y


def _conv_bwd(dy, nxt, x, prev, w):
    dx = None
    dws = []
    for k in range(CONV_WIDTH):
        s = CONV_WIDTH - 1 - k
        term = w[k:k + 1, :] * _shift_up(dy, nxt, s)
        dx = term if dx is None else dx + term
        dws.append(jnp.sum(dy * _shift_down(x, prev, s), axis=0, keepdims=True))
    return dx, jnp.concatenate(dws, axis=0), jnp.sum(dy, axis=0, keepdims=True)


def _scan_fwd(a, u):
    n = a.shape[0]
    rows = lax.broadcasted_iota(jnp.int32, a.shape, 0)
    s = 1
    while s < n:
        a_sh = jnp.where(rows < s, 1.0, pltpu.roll(a, s, 0))
        u_sh = jnp.where(rows < s, 0.0, pltpu.roll(u, s, 0))
        u = a * u_sh + u
        a = a * a_sh
        s *= 2
    return a, u


def _scan_rev(c, d):
    n = c.shape[0]
    rows = lax.broadcasted_iota(jnp.int32, c.shape, 0)
    s = 1
    while s < n:
        c_sh = jnp.where(rows >= n - s, 0.0, pltpu.roll(c, n - s, 0))
        d_sh = jnp.where(rows >= n - s, 0.0, pltpu.roll(d, n - s, 0))
        d = d + c * d_sh
        c = c * c_sh
        s *= 2
    return d


def _expm1(x):
    small = x * (1.0 + x * (0.5 + x * (1.0 / 6.0 + x * (1.0 / 24.0))))
    return jnp.where(jnp.abs(x) < 1e-2, small, jnp.exp(x) - 1.0)


def _lru_gates(xc, pa, pi, lam):
    r = _sigmoid(pa)
    i = _sigmoid(pi)
    log_a = LRU_C * r * _log_sigmoid(lam)
    a = jnp.exp(log_a)
    u = jnp.sqrt(-_expm1(2.0 * log_a)) * (i * xc)
    return a, u


def _lru_out(h, gate):
    return h * _gelu_tanh(gate)


def _blockdiag(xc, w_ref, b):
    nb = w_ref.shape[0]
    outs = [mm(xc[:, n * LRU_BLOCK:(n + 1) * LRU_BLOCK], w_ref[n]) for n in range(nb)]
    return jnp.concatenate(outs, axis=1) + b


def lru_fwd(proj, xcol, lw, cw, cb, wa, ba, wi, bi, lam):
    t = proj.shape[0]
    tb = _pick(t, (256, 128, 64))
    nb = t // tb

    def body(x_ref, xp_ref, g_ref, cw_ref, cb_ref, wa_ref, ba_ref, wi_ref, bi_ref, lam_ref, o_ref, hin_ref, hc):
        i = pl.program_id(0)

        @pl.when(i == 0)
        def _():
            hc[...] = jnp.zeros_like(hc)

        prev = jnp.where(i == 0, 0.0, xp_ref[...])
        xc = _conv(x_ref[...], prev, cw_ref[...], cb_ref[...])
        a, u = _lru_gates(xc, _blockdiag(xc, wa_ref, ba_ref[...]), _blockdiag(xc, wi_ref, bi_ref[...]), lam_ref[...])
        acum, h0 = _scan_fwd(a, u)
        h = h0 + acum * hc[...]
        hin_ref[0] = hc[...]
        hc[...] = h[tb - 1:tb, :]
        o_ref[...] = _lru_out(h, g_ref[...]).astype(o_ref.dtype)

    row = lambda col: pl.BlockSpec((tb, lw), lambda i: (i, col))
    return pl.pallas_call(
        body, name="lru_fwd", grid=(nb,),
        in_specs=[row(xcol), pl.BlockSpec((SUBLANE, lw), lambda i: (jnp.maximum(i * (tb // SUBLANE) - 1, 0), xcol)),
                  row(xcol + 1),
                  _full(cw.shape), _full(cb.shape), _full(wa.shape), _full(ba.shape), _full(wi.shape), _full(bi.shape),
                  _full(lam.shape)],
        out_specs=[pl.BlockSpec((tb, lw), lambda i: (i, 0)), pl.BlockSpec((1, 1, lw), lambda i: (i, 0, 0))],
        out_shape=[jax.ShapeDtypeStruct((t, lw), BF16), jax.ShapeDtypeStruct((nb, 1, lw), F32)],
        scratch_shapes=[pltpu.VMEM((1, lw), F32)],
        compiler_params=_cparams(("arbitrary",)),
    )(proj, proj, proj, cw, cb, wa, ba, wi, bi, lam)


def lru_bwd(proj, xcol, lw, cw, cb, wa, ba, wi, bi, lam, hin, dmix, docol, dproj):
    t = proj.shape[0]
    tb = _pick(t, (256, 128, 64))
    nb = t // tb
    nblk = wa.shape[0]
    assert xcol % 2 == 0

    def body(x_ref, xp_ref, g_ref, cw_ref, cb_ref, wa_ref, ba_ref, wi_ref, bi_ref, lam_ref, hin_ref, do_ref, _,
             dxg_ref, dcw_ref, dcb_ref, dwa_ref, dba_ref, dwi_ref, dbi_ref, dlam_ref, gc, dxcn):
        pid = pl.program_id(0)
        i = nb - 1 - pid

        @pl.when(pid == 0)
        def _():
            gc[...] = jnp.zeros_like(gc)
            dxcn[...] = jnp.zeros_like(dxcn)
            for r in (dcw_ref, dcb_ref, dwa_ref, dba_ref, dwi_ref, dbi_ref, dlam_ref):
                r[...] = jnp.zeros_like(r)

        x = x_ref[...]
        prev = jnp.where(i == 0, 0.0, xp_ref[...])
        cw_v = cw_ref[...]
        xc = _conv(x, prev, cw_v, cb_ref[...])
        pa = _blockdiag(xc, wa_ref, ba_ref[...])
        pi = _blockdiag(xc, wi_ref, bi_ref[...])
        (a, u), vjp_g = jax.vjp(_lru_gates, xc, pa, pi, lam_ref[...])
        acum, h0 = _scan_fwd(a, u)
        hi = hin_ref[0]
        h = h0 + acum * hi
        rows = lax.broadcasted_iota(jnp.int32, h.shape, 0)
        hprev = jnp.where(rows < 1, hi, pltpu.roll(h, 1, 0))
        _, vjp_o = jax.vjp(_lru_out, h, g_ref[...])
        dh, dgate = vjp_o(do_ref[...].astype(F32))
        c = jnp.where(rows >= tb - 1, 0.0, pltpu.roll(a, tb - 1, 0))
        g = _scan_rev(c, dh + jnp.where(rows == tb - 1, gc[...], 0.0))
        gc[...] = a[0:1, :] * g[0:1, :]
        dxc, dpa, dpi, dlam = vjp_g((g * hprev, g))
        dlam_ref[...] += dlam
        dba_ref[...] += jnp.sum(dpa, axis=0, keepdims=True)
        dbi_ref[...] += jnp.sum(dpi, axis=0, keepdims=True)
        parts = []
        for n in range(nblk):
            sl = slice(n * LRU_BLOCK, (n + 1) * LRU_BLOCK)
            dwa_ref[n] += mm_tn(xc[:, sl], dpa[:, sl])
            dwi_ref[n] += mm_tn(xc[:, sl], dpi[:, sl])
            parts.append(mm_nt(dpa[:, sl], wa_ref[n]) + mm_nt(dpi[:, sl], wi_ref[n]))
        dxc = dxc + jnp.concatenate(parts, axis=1)
        dx, dcw, dcb = _conv_bwd(dxc, dxcn[...], x, prev, cw_v)
        dxcn[...] = dxc[:SUBLANE]
        dcw_ref[...] += dcw
        dcb_ref[...] += dcb
        dxg_ref[:, :lw] = dx.astype(dxg_ref.dtype)
        dxg_ref[:, lw:] = dgate.astype(dxg_ref.dtype)

    row = lambda col: pl.BlockSpec((tb, lw), lambda p: (nb - 1 - p, col))
    params = [cw, cb, wa, ba, wi, bi, lam]
    return pl.pallas_call(
        body, name="lru_bwd", grid=(nb,),
        in_specs=[row(xcol),
                  pl.BlockSpec((SUBLANE, lw), lambda p: (jnp.maximum((nb - 1 - p) * (tb // SUBLANE) - 1, 0), xcol)),
                  row(xcol + 1)]
        + [_full(p.shape) for p in params]
        + [pl.BlockSpec((1, 1, lw), lambda p: (nb - 1 - p, 0, 0)), row(docol), ANY],
        out_specs=[pl.BlockSpec((tb, 2 * lw), lambda p: (nb - 1 - p, xcol // 2))] + [_full(p.shape) for p in params],
        out_shape=[jax.ShapeDtypeStruct(dproj.shape, dproj.dtype)]
        + [jax.ShapeDtypeStruct(p.shape, F32) for p in params],
        input_output_aliases={12: 0},
        scratch_shapes=[pltpu.VMEM((1, lw), F32), pltpu.VMEM((SUBLANE, lw), F32)],
        compiler_params=_cparams(("arbitrary",)),
    )(proj, proj, proj, *params, hin, dmix, dproj)


def conv_silu_fwd(proj, col0, width, cw, cb):
    t = proj.shape[0]
    tb = _pick(t, (512, 256, 128, 64))
    cbw = _pick(width, (512, 256, 128))
    off = col0 // cbw
    assert col0 % cbw == 0

    def body(x_ref, xp_ref, w_ref, b_ref, o_ref):
        prev = jnp.where(pl.program_id(1) == 0, 0.0, xp_ref[...])
        o_ref[...] = _silu(_conv(x_ref[...], prev, w_ref[...], b_ref[...]))

    return pl.pallas_call(
        body, name="conv_silu_fwd", grid=(width // cbw, t // tb),
        in_specs=[pl.BlockSpec((tb, cbw), lambda j, i: (i, off + j)),
                  pl.BlockSpec((SUBLANE, cbw), lambda j, i: (jnp.maximum(i * (tb // SUBLANE) - 1, 0), off + j)),
                  pl.BlockSpec((CONV_WIDTH, cbw), lambda j, i: (0, j)), pl.BlockSpec((1, cbw), lambda j, i: (0, j))],
        out_specs=pl.BlockSpec((tb, cbw), lambda j, i: (i, j)),
        out_shape=jax.ShapeDtypeStruct((t, width), F32),
        compiler_params=_cparams(("parallel", "arbitrary")),
    )(proj, proj, cw, cb)


def conv_silu_bwd(proj, col0, width, cw, cb, dact, dproj):
    t = proj.shape[0]
    tb = _pick(t, (512, 256, 128, 64))
    nb = t // tb
    cbw = _pick(width, (512, 256, 128))
    off = col0 // cbw

    def body(x_ref, xp_ref, w_ref, b_ref, d_ref, _, dx_ref, dw_ref, db_ref, nxt):
        pid = pl.program_id(1)
        i = nb - 1 - pid

        @pl.when(pid == 0)
        def _():
            nxt[...] = jnp.zeros_like(nxt)
            dw_ref[...] = jnp.zeros_like(dw_ref)
            db_ref[...] = jnp.zeros_like(db_ref)

        x = x_ref[...]
        prev = jnp.where(i == 0, 0.0, xp_ref[...])
        w = w_ref[...]
        _, vjp = jax.vjp(_silu, _conv(x, prev, w, b_ref[...]))
        (dcv,) = vjp(d_ref[...])
        dx, dw, db = _conv_bwd(dcv, nxt[...], x, prev, w)
        nxt[...] = dcv[:SUBLANE]
        dx_ref[...] = dx.astype(dx_ref.dtype)
        dw_ref[...] += dw
        db_ref[...] += db

    return pl.pallas_call(
        body, name="conv_silu_bwd", grid=(width // cbw, nb),
        in_specs=[pl.BlockSpec((tb, cbw), lambda j, p: (nb - 1 - p, off + j)),
                  pl.BlockSpec((SUBLANE, cbw),
                               lambda j, p: (jnp.maximum((nb - 1 - p) * (tb // SUBLANE) - 1, 0), off + j)),
                  pl.BlockSpec((CONV_WIDTH, cbw), lambda j, p: (0, j)), pl.BlockSpec((1, cbw), lambda j, p: (0, j)),
                  pl.BlockSpec((tb, cbw), lambda j, p: (nb - 1 - p, j)), ANY],
        out_specs=[pl.BlockSpec((tb, cbw), lambda j, p: (nb - 1 - p, off + j)),
                   pl.BlockSpec((CONV_WIDTH, cbw), lambda j, p: (0, j)), pl.BlockSpec((1, cbw), lambda j, p: (0, j))],
        out_shape=[jax.ShapeDtypeStruct(dproj.shape, dproj.dtype), jax.ShapeDtypeStruct(cw.shape, F32),
                   jax.ShapeDtypeStruct(cb.shape, F32)],
        scratch_shapes=[pltpu.VMEM((SUBLANE, cbw), F32)],
        input_output_aliases={5: 0},
        compiler_params=_cparams(("parallel", "arbitrary")),
    )(proj, proj, cw, cb, dact, dproj)


def _dt_expand(raw, bias, e):
    return sel_r(_softplus(raw + bias), e)


def dt_fwd(proj, dtcol, bias, e):
    t = proj.shape[0]
    di = e.shape[1]
    tb = _pick(t, (512, 256, 128, 64))

    def body(r_ref, b_ref, e_ref, o_ref):
        o_ref[...] = _dt_expand(r_ref[...], b_ref[...], e_ref[...])

    return pl.pallas_call(
        body, name="dt_fwd", grid=(t // tb,),
        in_specs=[pl.BlockSpec((tb, LANE), lambda i: (i, dtcol)), _full(bias.shape), _full(e.shape)],
        out_specs=pl.BlockSpec((tb, di), lambda i: (i, 0)),
        out_shape=jax.ShapeDtypeStruct((t, di), F32),
        compiler_params=_cparams(("parallel",)),
    )(proj, bias, e)


def dt_bwd(proj, dtcol, bias, e, ddte, dproj):
    t = proj.shape[0]
    di = e.shape[1]
    tb = _pick(t, (512, 256, 128, 64))
    tail = dproj.shape[1] - dtcol * LANE
    assert (dtcol * LANE) % tail == 0

    def body(r_ref, b_ref, e_ref, d_ref, _, dr_ref, db_ref):
        @pl.when(pl.program_id(0) == 0)
        def _():
            db_ref[...] = jnp.zeros_like(db_ref)

        e_v = e_ref[...]
        _, vjp = jax.vjp(lambda r, b: _dt_expand(r, b, e_v), r_ref[...], b_ref[...])
        dr, db = vjp(d_ref[...])
        dr_ref[:, :LANE] = dr.astype(dr_ref.dtype)
        if tail > LANE:
            dr_ref[:, LANE:] = jnp.zeros((tb, tail - LANE), dr_ref.dtype)
        db_ref[...] += db

    return pl.pallas_call(
        body, name="dt_bwd", grid=(t // tb,),
        in_specs=[pl.BlockSpec((tb, LANE), lambda i: (i, dtcol)), _full(bias.shape), _full(e.shape),
                  pl.BlockSpec((tb, di), lambda i: (i, 0)), ANY],
        out_specs=[pl.BlockSpec((tb, tail), lambda i: (i, dtcol * LANE // tail)), _full(bias.shape)],
        out_shape=[jax.ShapeDtypeStruct(dproj.shape, dproj.dtype), jax.ShapeDtypeStruct(bias.shape, F32)],
        input_output_aliases={4: 0},
        compiler_params=_cparams(("arbitrary",)),
    )(proj, bias, e, ddte, dproj)


def head_expand(p, e, *, transpose=False, name):
    di = e.shape[1]

    def body(p_ref, e_ref, o_ref):
        if transpose:
            o_ref[...] = _sel_r_b(e_ref[...], p_ref[...])[0]
        else:
            o_ref[...] = sel_r(p_ref[...], e_ref[...])

    oshape = (SUBLANE, LANE) if transpose else (SUBLANE, di)
    return pl.pallas_call(
        body, name=name, in_specs=[_full(p.shape), _full(e.shape)], out_specs=_full(oshape),
        out_shape=jax.ShapeDtypeStruct(oshape, F32), compiler_params=_cparams(None), grid=(1,),
    )(p, e)


def _ssd_chunk(x, z, bm, cm, dte, st, alog, dskip, gn, tri, cmask, dmask, bd):
    L, gw = x.shape
    reps = gw // L
    a = dte * (-jnp.exp(alog))
    acs = sel_l(tri, a)
    acs_last = jnp.sum(a, axis=0, keepdims=True)
    arow = jnp.sum(acs * dmask, axis=0, keepdims=True)
    dtrow = jnp.sum(dte * dmask, axis=0, keepdims=True)
    cb = mm_nt(cm, jnp.concatenate([bm] * reps, axis=0))
    wts = cb * (jnp.exp(jnp.minimum(acs - arow, 0.0)) * cmask) * dtrow
    xbd = jnp.concatenate([x] * reps, axis=0) * bd
    xw = x * (jnp.exp(acs_last - acs) * dte)
    y = mm(wts, xbd) + mm(cm, st) * jnp.exp(acs) + dskip * x
    st_new = jnp.exp(acs_last) * st + mm_tn(bm, xw)
    return _rms(y * _silu(z), gn), st_new


def _ssd_dims(di):
    gw = di // SSD_GROUPS
    assert CHUNK == SSD_HEAD_DIM and gw % LANE == 0
    return gw, SSD_STATE


def _ssd_masks(gw):
    L = CHUNK
    r = jnp.arange(L)[:, None]
    c = jnp.arange(gw)[None, :]
    cmask = ((c % L) <= r).astype(F32)
    dmask = ((c % L) == r).astype(F32)
    rr = jnp.arange(gw)
    bd = ((rr[:, None] // L) == (rr[None, :] // L)).astype(F32)
    tri = (jnp.arange(L)[None, :] <= jnp.arange(L)[:, None]).astype(BF16)
    return tri, cmask, dmask, bd


def ssd_fwd(xs, proj, dte, alog_e, dskip_e, gn):
    t, di = dte.shape
    gw, n = _ssd_dims(di)
    L, G = CHUNK, SSD_GROUPS
    nc = t // L
    masks = _ssd_masks(gw)
    cdim = xs.shape[1]

    def body(x_ref, z_ref, dt_ref, al_ref, ds_ref, gn_ref, tri_ref, cm_ref, dm_ref, bd_ref, y_ref, sp_ref, st):
        @pl.when(pl.program_id(0) == 0)
        def _():
            st[...] = jnp.zeros_like(st)

        for g in range(G):
            ch = slice(g * gw, (g + 1) * gw)
            s_prev = st[g]
            sp_ref[0, g] = s_prev
            y, s_new = _ssd_chunk(x_ref[:, ch], z_ref[:, ch], x_ref[:, di + g * n:di + (g + 1) * n],
                                  x_ref[:, di + (G + g) * n:di + (G + g + 1) * n], dt_ref[:, ch], s_prev,
                                  al_ref[0:1, ch], ds_ref[0:1, ch], gn_ref[:, ch], tri_ref[...], cm_ref[...],
                                  dm_ref[...], bd_ref[...])
            y_ref[:, ch] = y.astype(y_ref.dtype)
            st[g] = s_new

    row = lambda w: pl.BlockSpec((L, w), lambda c: (c, 0))
    return pl.pallas_call(
        body, name="ssd_fwd", grid=(nc,),
        in_specs=[row(cdim), row(di), row(di), _full(alog_e.shape), _full(dskip_e.shape), _full(gn.shape)]
        + [_full(m.shape) for m in masks],
        out_specs=[row(di), pl.BlockSpec((1, G, n, gw), lambda c: (c, 0, 0, 0))],
        out_shape=[jax.ShapeDtypeStruct((t, di), BF16), jax.ShapeDtypeStruct((nc, G, n, gw), F32)],
        scratch_shapes=[pltpu.VMEM((G, n, gw), F32)],
        compiler_params=_cparams(("arbitrary",)),
    )(xs, proj, dte, alog_e, dskip_e, gn, *masks)


def ssd_bwd(xs, proj, dte, alog_e, dskip_e, gn, sprev, dy, dproj_shape):
    t, di = dte.shape
    gw, n = _ssd_dims(di)
    L, G = CHUNK, SSD_GROUPS
    nc = t // L
    masks = _ssd_masks(gw)
    cdim = xs.shape[1]

    def body(x_ref, z_ref, dt_ref, al_ref, ds_ref, gn_ref, tri_ref, cm_ref, dm_ref, bd_ref, sp_ref, dy_ref,
             dxs_ref, dz_ref, ddt_ref, dal_ref, dds_ref, dgn_ref, dst):
        @pl.when(pl.program_id(0) == 0)
        def _():
            dst[...] = jnp.zeros_like(dst)
            dal_ref[...] = jnp.zeros_like(dal_ref)
            dds_ref[...] = jnp.zeros_like(dds_ref)
            dgn_ref[...] = jnp.zeros_like(dgn_ref)

        f = functools.partial(_ssd_chunk, tri=tri_ref[...], cmask=cm_ref[...], dmask=dm_ref[...], bd=bd_ref[...])
        for g in range(G):
            ch = slice(g * gw, (g + 1) * gw)
            bs = slice(di + g * n, di + (g + 1) * n)
            cs = slice(di + (G + g) * n, di + (G + g + 1) * n)
            _, vjp = jax.vjp(f, x_ref[:, ch], z_ref[:, ch], x_ref[:, bs], x_ref[:, cs], dt_ref[:, ch], sp_ref[0, g],
                             al_ref[0:1, ch], ds_ref[0:1, ch], gn_ref[:, ch])
            dx, dz, db, dc, ddt, ds, dal, dds, dgn = vjp((dy_ref[:, ch], dst[g]))
            dxs_ref[:, ch] = dx
            dxs_ref[:, bs] = db
            dxs_ref[:, cs] = dc
            dz_ref[:, ch] = dz.astype(dz_ref.dtype)
            ddt_ref[:, ch] = ddt
            dst[g] = ds
            dal_ref[:, ch] += dal
            dds_ref[:, ch] += dds
            dgn_ref[:, ch] += dgn

    row = lambda w: pl.BlockSpec((L, w), lambda c: (nc - 1 - c, 0))
    acc = _full((1, di))
    acc_shape = jax.ShapeDtypeStruct((1, di), F32)
    return pl.pallas_call(
        body, name="ssd_bwd", grid=(nc,),
        in_specs=[row(cdim), row(di), row(di), _full(alog_e.shape), _full(dskip_e.shape), _full(gn.shape)]
        + [_full(m.shape) for m in masks]
        + [pl.BlockSpec((1, G, n, gw), lambda c: (nc - 1 - c, 0, 0, 0)), row(di)],
        out_specs=[row(cdim), row(di), row(di), acc, acc, acc],
        out_shape=[jax.ShapeDtypeStruct((t, cdim), F32), jax.ShapeDtypeStruct(dproj_shape, BF16),
                   jax.ShapeDtypeStruct((t, di), F32), acc_shape, acc_shape, acc_shape],
        scratch_shapes=[pltpu.VMEM((G, n, gw), F32)],
        compiler_params=_cparams(("arbitrary",)),
    )(xs, proj, dte, alog_e, dskip_e, gn, *masks, sprev, dy)


def _rows2d(a):
    return a.reshape(-1, a.shape[-1])


def _row_tile(rows, cols):
    cap = max(SUBLANE, (1 << 19) // max(cols, 1))
    for c in (2048, 1024, 512, 256, 128, 64, 32, 16, 8):
        if c <= cap and rows % c == 0:
            return c
    return rows


def chip_sum(g, r, core, *, name):
    shape = r.shape
    cols = shape[-1]
    g4 = g.reshape(4, 2, -1, cols)
    r3 = r.reshape(4, -1, cols)
    rows = r3.shape[1]
    tr = _row_tile(rows, cols)

    def body(c_ref, g_ref, r_ref, o_ref):
        o_ref[...] = (g_ref[...].astype(F32) + r_ref[...].astype(F32)).astype(o_ref.dtype)

    out = pl.pallas_call(
        body, name=name,
        grid_spec=pltpu.PrefetchScalarGridSpec(
            num_scalar_prefetch=1, grid=(4, rows // tr),
            in_specs=[pl.BlockSpec((None, None, tr, cols), lambda j, i, c: (j, c[0], i, 0)),
                      pl.BlockSpec((None, tr, cols), lambda j, i, c: (j, i, 0))],
            out_specs=pl.BlockSpec((None, tr, cols), lambda j, i, c: (j, i, 0))),
        out_shape=jax.ShapeDtypeStruct(r3.shape, BF16), compiler_params=_cparams(("parallel", "parallel")),
    )(core.reshape(1).astype(jnp.int32), g4, r3)
    return out.reshape(shape)


def mesh_sum(p, r, chip, core, *, name):
    shape = p.shape[1:]
    cols = shape[-1]
    p3 = p.reshape(4, -1, cols)
    r3 = r.reshape(3, -1, cols)
    rows = p3.shape[1]
    tr = _row_tile(rows, 2 * cols)

    def body(c_ref, p_ref, r_ref, o_ref):
        o_ref[...] = ((p_ref[...].astype(F32) + r_ref[0].astype(F32)) + r_ref[1].astype(F32)) + r_ref[2].astype(F32)

    out = pl.pallas_call(
        body, name=name,
        grid_spec=pltpu.PrefetchScalarGridSpec(
            num_scalar_prefetch=1, grid=(rows // tr,),
            in_specs=[pl.BlockSpec((None, tr, cols), lambda i, c: (c[0], i, 0)),
                      pl.BlockSpec((3, tr, cols), lambda i, c: (0, i, 0))],
            out_specs=pl.BlockSpec((None, tr, cols), lambda i, c: (c[1], i, 0))),
        out_shape=jax.ShapeDtypeStruct((2, rows, cols), F32), compiler_params=_cparams(("parallel",)),
    )(jnp.stack([chip, core]).astype(jnp.int32), p3, r3)
    return out.reshape((2,) + shape)


def sum_leading(x, *, name):
    k, r, c = x.shape
    tr = _row_tile(r, c * k)
    def body(x_ref, o_ref):
        acc = x_ref[0]
        for i in range(1, k):
            acc = acc + x_ref[i]
        o_ref[...] = acc

    return pl.pallas_call(
        body, name=name, grid=(r // tr,), in_specs=[pl.BlockSpec((k, tr, c), lambda i: (0, i, 0))],
        out_specs=pl.BlockSpec((tr, c), lambda i: (i, 0)),
        out_shape=jax.ShapeDtypeStruct((r, c), F32), compiler_params=_cparams(("parallel",)),
    )(x)


def adamw(w, g, m, v, *, name):
    shape = w.shape
    w2, g2, m2, v2 = (_rows2d(a) for a in (w, g, m, v))
    r, c = w2.shape
    tr = _row_tile(r, 2 * c)
    row = pl.BlockSpec((tr, c), lambda i: (i, 0))
    c1 = 1.0 - ADAM_B1 ** ADAM_STEP
    c2 = 1.0 - ADAM_B2 ** ADAM_STEP

    def body(w_ref, g_ref, m_ref, v_ref, d_ref, mo_ref, vo_ref):
        gv = g_ref[...]
        mn = ADAM_B1 * m_ref[...] + (1.0 - ADAM_B1) * gv
        vn = ADAM_B2 * v_ref[...] + (1.0 - ADAM_B2) * (gv * gv)
        d_ref[...] = -ADAM_LR * ((mn / c1) / (jnp.sqrt(vn / c2) + ADAM_EPS) + ADAM_WD * w_ref[...])
        mo_ref[...] = mn
        vo_ref[...] = vn

    outs = pl.pallas_call(
        body, name=name, grid=(r // tr,), in_specs=[row] * 4, out_specs=[row] * 3,
        out_shape=[jax.ShapeDtypeStruct((r, c), F32)] * 3, compiler_params=_cparams(("parallel",)),
    )(w2, g2, m2, v2)
    return tuple(o.reshape(shape) for o in outs)


ANY = pl.BlockSpec(memory_space=pl.ANY)


def _place():
    x, y, c = lax.axis_index("x"), lax.axis_index("y"), lax.axis_index("c")
    chips = [(1 - x, y), (x, 1 - y), (1 - x, 1 - y)]
    return x, y, c, chips


def gather8(block):
    m, n = block.shape

    def body(x_ref, out_ref, send_sems, recv_sems, local_sem):
        x, y, c, chips = _place()
        me, sibling = (x, y, c), (x, y, 1 - c)

        def rows(px, py, pc):
            return out_ref.at[4 * px + 2 * py + pc]

        def copy(k, blk, to, src=None):
            return pltpu.make_async_remote_copy(
                src_ref=rows(*blk) if src is None else src, dst_ref=rows(*blk), send_sem=send_sems.at[k],
                recv_sem=recv_sems.at[k], device_id=to, device_id_type=MESH)

        mine = pltpu.make_async_copy(x_ref, rows(*me), local_sem)
        mine.start()
        first = [copy(0, me, sibling, src=x_ref)]
        first += [copy(1 + j, me, (*chip, c), src=x_ref) for j, chip in enumerate(chips)]
        for cp in first:
            cp.start()
        passed = [copy(4 + j, (*chip, c), sibling) for j, chip in enumerate(chips)]
        for j, chip in enumerate(chips):
            copy(1 + j, (*chip, c), me).wait_recv()
            passed[j].start()
        copy(0, sibling, me).wait_recv()
        for j, chip in enumerate(chips):
            copy(4 + j, (*chip, 1 - c), me).wait_recv()
        for cp in first + passed:
            cp.wait_send()
        mine.wait()

    return pl.pallas_call(
        body, name="gather8",
        out_shape=jax.ShapeDtypeStruct((8, m, n), block.dtype),
        in_specs=[pl.BlockSpec(memory_space=pltpu.VMEM)],
        out_specs=pl.BlockSpec(memory_space=pltpu.VMEM),
        scratch_shapes=[pltpu.SemaphoreType.DMA((7,)), pltpu.SemaphoreType.DMA((7,)), pltpu.SemaphoreType.DMA],
        compiler_params=pltpu.CompilerParams(vmem_limit_bytes=VMEM_LIMIT),
    )(block)


def gather_weights(shards):
    n = len(shards)

    def body(*refs):
        ins, outs = refs[:n], refs[n:2 * n]
        send_sems, recv_sems = refs[2 * n:]
        x, y, c, chips = _place()
        me = 2 * x + y
        sibling = (x, y, 1 - c)

        def copy(a, k, chip_idx, half, to, src=None):
            dst = outs[a].at[chip_idx, half]
            return pltpu.make_async_remote_copy(
                src_ref=dst if src is None else src, dst_ref=dst, send_sem=send_sems.at[a, k],
                recv_sem=recv_sems.at[a, k], device_id=to, device_id_type=MESH)

        first = [copy(a, j, me, c, (*chip, c), src=ins[a].at[c]) for a in range(n) for j, chip in enumerate(chips)]
        for cp in first:
            cp.start()
        passed = []
        for a in range(n):
            for j, (cx, cy) in enumerate(chips):
                copy(a, j, 2 * cx + cy, c, (cx, cy, c)).wait_recv()
                fw = copy(a, 3 + j, 2 * cx + cy, c, sibling)
                fw.start()
                passed.append(fw)
        for a in range(n):
            for j, (cx, cy) in enumerate(chips):
                copy(a, 3 + j, 2 * cx + cy, 1 - c, sibling).wait_recv()
        for cp in first + passed:
            cp.wait_send()

    return pl.pallas_call(
        body, name="gather_weights",
        out_shape=[jax.ShapeDtypeStruct((4,) + s.shape, s.dtype) for s in shards],
        in_specs=[ANY] * n, out_specs=[ANY] * n,
        scratch_shapes=[pltpu.SemaphoreType.DMA((n, 6)), pltpu.SemaphoreType.DMA((n, 6))],
    )(*shards)


def exchange_halves(grads):
    n = len(grads)

    def body(*refs):
        ins, outs = refs[:n], refs[n:2 * n]
        send_sems, recv_sems = refs[2 * n:]
        x, y, c, _ = _place()
        cps = [pltpu.make_async_remote_copy(
            src_ref=ins[a].at[j, 1 - c], dst_ref=outs[a].at[j], send_sem=send_sems.at[a, j],
            recv_sem=recv_sems.at[a, j], device_id=(x, y, 1 - c), device_id_type=MESH)
            for a in range(n) for j in range(4)]
        for cp in cps:
            cp.start()
        for cp in cps:
            cp.wait()

    return pl.pallas_call(
        body, name="exchange_halves",
        out_shape=[jax.ShapeDtypeStruct((4,) + g.shape[2:], g.dtype) for g in grads],
        in_specs=[ANY] * n, out_specs=[ANY] * n,
        scratch_shapes=[pltpu.SemaphoreType.DMA((n, 4)), pltpu.SemaphoreType.DMA((n, 4))],
    )(*grads)


def scatter_chips(parts):
    n = len(parts)

    def body(*refs):
        ins, outs = refs[:n], refs[n:2 * n]
        send_sems, recv_sems = refs[2 * n:]
        x, y, c, chips = _place()
        cps = [pltpu.make_async_remote_copy(
            src_ref=ins[a].at[2 * cx + cy], dst_ref=outs[a].at[j], send_sem=send_sems.at[a, j],
            recv_sem=recv_sems.at[a, j], device_id=(cx, cy, c), device_id_type=MESH)
            for a in range(n) for j, (cx, cy) in enumerate(chips)]
        for cp in cps:
            cp.start()
        for cp in cps:
            cp.wait()

    return pl.pallas_call(
        body, name="scatter_chips",
        out_shape=[jax.ShapeDtypeStruct((3,) + p.shape[1:], p.dtype) for p in parts],
        in_specs=[ANY] * n, out_specs=[ANY] * n,
        scratch_shapes=[pltpu.SemaphoreType.DMA((n, 3)), pltpu.SemaphoreType.DMA((n, 3))],
    )(*parts)


def join_halves(bufs):
    n = len(bufs)

    def body(*refs):
        outs = refs[n:2 * n]
        send_sems, recv_sems = refs[2 * n:]
        x, y, c, _ = _place()
        cps = [pltpu.make_async_remote_copy(
            src_ref=outs[a].at[c], dst_ref=outs[a].at[c], send_sem=send_sems.at[a], recv_sem=recv_sems.at[a],
            device_id=(x, y, 1 - c), device_id_type=MESH) for a in range(n)]
        for cp in cps:
            cp.start()
        for a in range(n):
            pltpu.make_async_remote_copy(
                src_ref=outs[a].at[c], dst_ref=outs[a].at[1 - c], send_sem=send_sems.at[a], recv_sem=recv_sems.at[a],
                device_id=(x, y, 1 - c), device_id_type=MESH).wait_recv()
        for cp in cps:
            cp.wait_send()

    return pl.pallas_call(
        body, name="join_halves",
        out_shape=[jax.ShapeDtypeStruct(h.shape, h.dtype) for h in bufs],
        in_specs=[ANY] * n, out_specs=[ANY] * n, input_output_aliases={a: a for a in range(n)},
        scratch_shapes=[pltpu.SemaphoreType.DMA((n,)), pltpu.SemaphoreType.DMA((n,))],
    )(*bufs)


INPUTS = ['x'] + WEIGHTS + ['loss_target'] + ['m_' + n for n in WEIGHTS] + ['v_' + n for n in WEIGHTS]


def _round_up(n, m):
    return -(-n // m) * m


def _pack(arrs):
    flat = jnp.concatenate([a.reshape(-1) for a in arrs])
    n = _round_up(flat.shape[0], 512 * LANE)
    return jnp.pad(flat, (0, n - flat.shape[0])).reshape(-1, LANE)


def _unpack(block, shapes):
    flat = block.reshape(-1)
    out, o = [], 0
    for s in shapes:
        n = math.prod(s)
        out.append(flat[o:o + n].reshape(s))
        o += n
    return out


def _cols(g):
    return g.transpose(1, 0, 2).reshape(g.shape[1], -1)


def _uncols(w):
    return w.reshape(w.shape[0], 4, -1).transpose(1, 0, 2)


def _pad_cols(w, total):
    return jnp.pad(w, ((0, 0), (0, total - w.shape[1])))


def kernel(*args):
    a = dict(zip(INPUTS, args))
    x, tgt = a['x'][0], a['loss_target'][0]
    t, d = x.shape
    xi, yi, ci = lax.axis_index("x"), lax.axis_index("y"), lax.axis_index("c")
    chip = 2 * xi + yi
    dk, dv, hk, hv = _gla_dims(d)
    lw = d // 2
    di = 2 * d
    nh = di // SSD_HEAD_DIM
    gn_w = SSD_GROUPS * SSD_STATE
    conv_dim = di + 2 * gn_w
    rank = GLA_GATE_RANK
    wq = 2 * dk + 2 * dv
    gla_w = _round_up(wq + LANE, 2 * lw)
    ev_tot = gla_w + 2 * lw
    od_used = di + conv_dim + nh
    od_tot = _round_up(di + conv_dim + _round_up(nh, LANE), 512)
    glr_col, xcol, dtcol = wq // LANE, gla_w // lw, (di + conv_dim) // LANE
    assert ev_tot % 512 == 0 and nh <= LANE

    def halves(w):
        w = w.astype(BF16)
        if w.shape[0] == 2:
            return w
        w = w[0]
        return w.reshape((2, w.shape[0] // 2) + w.shape[1:])

    own = [halves(a[n]) for n in BIG]
    gw = {n: lax.dynamic_update_index_in_dim(g, o, chip, 0) for n, g, o in zip(BIG, gather_weights(own), own)}
    w_ev_in = _cols(gw['ev_w_in'].reshape((4, d, -1)))
    cuts = [dk, 2 * dk, 2 * dk + dv, wq, wq + rank, wq + rank + lw]
    sq, sk, sv, sg, sglr, sxb, sgb = jnp.split(w_ev_in, cuts, axis=1)
    w_ev_in_p = jnp.concatenate([_pad_cols(jnp.concatenate([sq, sk, sv, sg, sglr], axis=1), gla_w), sxb, sgb], axis=1)
    w_ev_out = gw['ev_w_out'].reshape(-1, d)
    w_od_in_p = _pad_cols(_cols(gw['od_w_in'].reshape((4, d, -1))), od_tot)
    w_od_out = gw['od_w_out'].reshape(-1, d)
    w_gate = [_cols(gw['ffn_w_gate'][:, l]) for l in range(2)]
    w_up = [_cols(gw['ffn_w_up'][:, l]) for l in range(2)]
    w_down = [gw['ffn_w_down'][:, l].reshape(-1, d) for l in range(2)]

    sh_names = list(SMALL_SHARDED)
    sh_shapes = [a[n].shape for n in sh_names]
    g8 = gather8(_pack([a[n] for n in sh_names]))
    per_chip = [_unpack(g8[2 * j], sh_shapes) for j in range(4)]
    full = {n: jnp.concatenate([per_chip[j][i] for j in range(4)], axis=SMALL_SHARDED[n])
            for i, n in enumerate(sh_names)}

    wg_p = jnp.zeros((LANE, dk), F32).at[:rank].set(full['ev_gla_w_gate'][0])
    bg, wn = a['ev_gla_b_gate'], a['ev_gla_w_onorm']
    lru_p = [full['ev_lru_conv_w'][0], a['ev_lru_conv_b'], a['ev_lru_w_a'][0], a['ev_lru_b_a'], a['ev_lru_w_i'][0],
             a['ev_lru_b_i'], a['ev_lru_lam']]
    od_cw, od_cb, od_gn = full['od_conv_w'][0], full['od_conv_b'], full['od_gnorm']
    heads = jnp.arange(LANE)[:, None]
    e_mat = ((jnp.arange(di)[None, :] // SSD_HEAD_DIM == heads) & (heads < nh)).astype(BF16)
    row8 = lambda p: jnp.zeros((SUBLANE, LANE), F32).at[0, :nh].set(p[0])
    dt_bias_p = jnp.zeros((1, LANE), F32).at[0, :nh].set(a['od_dt_bias'][0])
    alog_e = head_expand(row8(a['od_a_log']), e_mat, name="expand_a_log")
    dskip_e = head_expand(row8(a['od_d_skip']), e_mat, name="expand_d_skip")

    h0 = rms_fwd(x, a['ev_norm'], name="rms_ev")
    proj = matmul(h0, w_ev_in_p, name="ev_in")
    o_gla, sp_gla = gla_fwd(proj, glr_col, wg_p, bg, wn, dv)
    o_lru, hin = lru_fwd(proj, xcol, lw, *lru_p)
    x1 = matmul(o_gla, w_ev_out[:dv], add=x, name="ev_out_a")
    x1 = matmul(o_lru, w_ev_out[dv:], add=x1, name="ev_out_b")

    def ffn_fwd(xin, l):
        h = rms_fwd(xin, a['ffn_norm'][l:l + 1], name=f"rms_ffn{l}")
        gate = matmul(h, w_gate[l], name=f"ffn{l}_gate")
        up = matmul(h, w_up[l], name=f"ffn{l}_up")
        act = swiglu_fwd(gate, up, name=f"ffn{l}_act")
        return h, gate, up, act, matmul(act, w_down[l], add=xin, name=f"ffn{l}_down")

    h1, gate0, up0, act0, x2 = ffn_fwd(x1, 0)
    h2 = rms_fwd(x2, full['od_norm'], name="rms_od")
    proj2 = matmul(h2, w_od_in_p, name="od_in")
    xs = conv_silu_fwd(proj2, di, conv_dim, od_cw, od_cb)
    dte = dt_fwd(proj2, dtcol, dt_bias_p, e_mat)
    y_ssd, sp_ssd = ssd_fwd(xs, proj2, dte, alog_e, dskip_e, od_gn)
    x3 = matmul(y_ssd, w_od_out, add=x2, name="od_out")
    h3, gate1, up1, act1, x4 = ffn_fwd(x3, 1)
    loss_p, dx4, d_final = loss_head(x4, a['final_norm'][None], tgt, name="loss_head")

    def ffn_bwd(dxo, xin, h, gate, up, act, l):
        dact = matmul(dxo, w_down[l], tb=True, name=f"ffn{l}_d_act")
        d_down = matmul(act, dxo, ta=True, out_dtype=BF16, name=f"ffn{l}_dw_down")
        dg, du = swiglu_bwd(gate, up, dact, name=f"ffn{l}_act_bwd")
        dh = matmul(dg, w_gate[l], tb=True, name=f"ffn{l}_dh_gate")
        dh = matmul(du, w_up[l], tb=True, add=dh, name=f"ffn{l}_dh_up")
        d_gate = matmul(h, dg, ta=True, out_dtype=BF16, name=f"ffn{l}_dw_gate")
        d_up = matmul(h, du, ta=True, out_dtype=BF16, name=f"ffn{l}_dw_up")
        dxi, d_norm = rms_bwd(xin, a['ffn_norm'][l:l + 1], dh, dxo, name=f"rms_ffn{l}_bwd")
        return dxi, d_norm, d_gate, d_up, d_down

    dx3, d_fn1, d_gate1, d_up1, d_down1 = ffn_bwd(dx4, x3, h3, gate1, up1, act1, 1)
    dy = matmul(dx3, w_od_out, tb=True, name="od_out_dy")
    d_od_out = matmul(y_ssd, dx3, ta=True, out_dtype=BF16, name="od_out_dw")
    dxs, dproj2, ddte, dal, dds, dgn = ssd_bwd(xs, proj2, dte, alog_e, dskip_e, od_gn, sp_ssd, dy, proj2.shape)
    dproj2, d_od_cw, d_od_cb = conv_silu_bwd(proj2, di, conv_dim, od_cw, od_cb, dxs, dproj2)
    dproj2, d_dt_bias = dt_bwd(proj2, dtcol, dt_bias_p, e_mat, ddte, dproj2)
    dh2 = matmul(dproj2, w_od_in_p, tb=True, name="od_in_dh")
    d_od_in = matmul(h2, dproj2, ta=True, out_dtype=BF16, name="od_in_dw")[:, :od_used]
    dx2, d_od_norm = rms_bwd(x2, full['od_norm'], dh2, dx3, name="rms_od_bwd")
    to8 = lambda acc: jnp.zeros((SUBLANE, di), F32).at[0].set(acc.reshape(-1))
    d_a_log = head_expand(to8(dal), e_mat, transpose=True, name="reduce_a_log")[0:1, :nh]
    d_d_skip = head_expand(to8(dds), e_mat, transpose=True, name="reduce_d_skip")[0:1, :nh]

    dx1, d_fn0, d_gate0, d_up0, d_down0 = ffn_bwd(dx2, x1, h1, gate0, up0, act0, 0)
    dmix = matmul(dx1, w_ev_out, tb=True, name="ev_out_dmix")
    d_ev_out = jnp.concatenate([matmul(o_gla, dx1, ta=True, out_dtype=BF16, name="ev_out_dw_a"),
                                matmul(o_lru, dx1, ta=True, out_dtype=BF16, name="ev_out_dw_b")], axis=0)
    dproj, d_wg, d_bg, d_wn = gla_bwd(proj, glr_col, wg_p, bg, wn, sp_gla, dmix, dv, gla_w)
    dproj, *d_lru = lru_bwd(proj, xcol, lw, *lru_p, hin, dmix, 1, dproj)
    dh0 = matmul(dproj, w_ev_in_p, tb=True, name="ev_in_dh")
    d_ev_in_p = matmul(h0, dproj, ta=True, out_dtype=BF16, name="ev_in_dw")
    d_ev_in = jnp.concatenate([d_ev_in_p[:, :wq + rank], d_ev_in_p[:, gla_w:]], axis=1)
    dx0, d_ev_norm = rms_bwd(x, a['ev_norm'], dh0, dx1, name="rms_ev_bwd")

    def shard_layout(dw, col):
        p = _uncols(dw) if col else dw.reshape((4, dw.shape[0] // 4) + dw.shape[1:])
        return p.reshape((4, 2, p.shape[1] // 2) + p.shape[2:])

    layers = lambda ws, col: jnp.stack([_uncols(w) if col else w.reshape((4, w.shape[0] // 4) + w.shape[1:])
                                        for w in ws], axis=1)
    big_g = {
        'ev_w_in': shard_layout(d_ev_in, True), 'ev_w_out': shard_layout(d_ev_out, False),
        'od_w_in': shard_layout(d_od_in, True), 'od_w_out': shard_layout(d_od_out, False),
        'ffn_w_gate': layers([d_gate0, d_gate1], True), 'ffn_w_up': layers([d_up0, d_up1], True),
        'ffn_w_down': layers([d_down0, d_down1], False),
    }
    g_list = [big_g[n] for n in BIG]
    from_sibling = exchange_halves(g_list)
    chip_sums = [chip_sum(g, r, ci, name=f"chip_sum_{n}") for n, g, r in zip(BIG, g_list, from_sibling)]
    from_chips = scatter_chips(chip_sums)
    half = [mesh_sum(p, r, chip, ci, name=f"mesh_sum_{n}") for n, p, r in zip(BIG, chip_sums, from_chips)]
    grad = {n: g.reshape(a[n].shape) for n, g in zip(BIG, join_halves(half))}

    small_g = {
        'ev_norm': d_ev_norm, 'ev_gla_w_gate': d_wg[:rank][None], 'ev_gla_b_gate': d_bg, 'ev_gla_w_onorm': d_wn,
        'ev_lru_conv_w': d_lru[0][None], 'ev_lru_conv_b': d_lru[1], 'ev_lru_w_a': d_lru[2][None],
        'ev_lru_b_a': d_lru[3], 'ev_lru_w_i': d_lru[4][None], 'ev_lru_b_i': d_lru[5], 'ev_lru_lam': d_lru[6],
        'od_norm': d_od_norm, 'od_conv_w': d_od_cw[None], 'od_conv_b': d_od_cb, 'od_dt_bias': d_dt_bias[:, :nh],
        'od_a_log': d_a_log, 'od_d_skip': d_d_skip, 'od_gnorm': dgn.reshape(1, di),
        'ffn_norm': jnp.concatenate([d_fn0, d_fn1], axis=0), 'final_norm': d_final[0],
    }
    full_shapes = [small_g[n].shape for n in SMALL]
    summed = sum_leading(gather8(_pack([small_g[n] for n in SMALL])), name="sum_devices")
    for n, g in zip(SMALL, _unpack(summed, full_shapes)):
        if n in SMALL_SHARDED:
            ax = SMALL_SHARDED[n]
            sz = a[n].shape[ax]
            g = lax.dynamic_slice_in_dim(g, chip * sz, sz, axis=ax)
        grad[n] = g

    delta, new_m, new_v = {}, {}, {}
    for n in BIG:
        delta[n], new_m[n], new_v[n] = adamw(a[n], grad[n], a['m_' + n], a['v_' + n], name=f"adamw_{n}")
    shapes = [a[n].shape for n in SMALL]
    packed = [_pack([src[n] if pre is None else a[pre + n] for n in SMALL])
              for src, pre in ((a, None), (grad, None), (None, 'm_'), (None, 'v_'))]
    for outd, blk in zip((delta, new_m, new_v), adamw(*packed, name="adamw_small")):
        outd.update(zip(SMALL, _unpack(blk, shapes)))

    loss = lax.psum(loss_p[0, 0], ("x", "y", "c"))
    return (loss, dx0[None], *[grad[n] for n in WEIGHTS], *[delta[n] for n in WEIGHTS],
            *[new_m[n] for n in WEIGHTS], *[new_v[n] for n in WEIGHTS])
```

```python
import functools
import math

import jax
import jax.numpy as jnp
from jax import lax
from jax.experimental import pallas as pl
from jax.experimental.pallas import tpu as pltpu

F32 = jnp.float32
BF16 = jnp.bfloat16
MXU_DTYPE = jnp.bfloat16

NORM_EPS = 1e-6
CONV_WIDTH = 4
GLA_HEADS = 4
GLA_GATE_RANK = 16
GLA_GATE_NORM = 16.0
CHUNK = 64
LRU_BLOCK = 128
LRU_C = 8.0
SSD_HEAD_DIM = 64
SSD_GROUPS = 8
SSD_STATE = 128
ADAM_LR, ADAM_B1, ADAM_B2, ADAM_EPS, ADAM_WD, ADAM_STEP = 0.001, 0.9, 0.999, 1e-08, 0.01, 10

LANE = 128
SUBLANE = 8
VMEM_LIMIT = 48 * 1024 * 1024
MAX_TK = 2816
MESH = pl.DeviceIdType.MESH

WEIGHTS = ['ev_norm', 'ev_w_in', 'ev_gla_w_gate', 'ev_gla_b_gate', 'ev_gla_w_onorm', 'ev_lru_conv_w', 'ev_lru_conv_b',
           'ev_lru_w_a', 'ev_lru_b_a', 'ev_lru_w_i', 'ev_lru_b_i', 'ev_lru_lam', 'ev_w_out', 'od_norm', 'od_w_in',
           'od_conv_w', 'od_conv_b', 'od_dt_bias', 'od_a_log', 'od_d_skip', 'od_gnorm', 'od_w_out', 'ffn_norm',
           'ffn_w_gate', 'ffn_w_up', 'ffn_w_down', 'final_norm']
BIG = ['ev_w_in', 'ev_w_out', 'od_w_in', 'od_w_out', 'ffn_w_gate', 'ffn_w_up', 'ffn_w_down']
SMALL_SHARDED = {'ev_gla_w_gate': 2, 'ev_lru_conv_w': 2, 'od_norm': 1, 'od_conv_w': 2, 'od_conv_b': 1, 'od_gnorm': 1}
SMALL = [n for n in WEIGHTS if n not in BIG]


def _cparams(sem=None, **kw):
    return pltpu.CompilerParams(dimension_semantics=sem, vmem_limit_bytes=VMEM_LIMIT, **kw)


def _full(shape):
    n = len(shape)
    return pl.BlockSpec(shape, lambda *_: (0,) * n)


ANY = pl.BlockSpec(memory_space=pl.ANY)


class Carry:
    def __init__(self, arrays, out_shape, n_sems, start, finish, aliases=None):
        self.arrays, self.out_shape, self.n_sems = list(arrays), list(out_shape), n_sems
        self.start, self.finish, self.aliases = start, finish, dict(aliases or {})


def merge_carries(*cs):
    cs = [c for c in cs if c is not None]
    if not cs:
        return None
    arrays = [a for c in cs for a in c.arrays]
    out_shape = [s for c in cs for s in c.out_shape]
    offs, i0, o0, s0 = [], 0, 0, 0
    aliases = {}
    for c in cs:
        offs.append((i0, o0, s0))
        aliases.update({i0 + i: o0 + o for i, o in c.aliases.items()})
        i0, o0, s0 = i0 + len(c.arrays), o0 + len(c.out_shape), s0 + c.n_sems

    def both(which):
        def run(ins, outs, send, recv, base):
            for c, (i, o, s) in zip(cs, offs):
                getattr(c, which)(ins[i:i + len(c.arrays)], outs[o:o + len(c.out_shape)], send, recv, base + s)
        return run

    return Carry(arrays, out_shape, s0, both("start"), both("finish"), aliases)


def _pcall(body, *, name, grid, in_specs, out_specs, out_shape, scratch_shapes=(), compiler_params, carry=None,
           input_output_aliases=None):
    aliases = dict(input_output_aliases or {})
    if carry is None:
        return pl.pallas_call(body, name=name, grid=grid, in_specs=in_specs, out_specs=out_specs, out_shape=out_shape,
                              scratch_shapes=list(scratch_shapes), compiler_params=compiler_params,
                              input_output_aliases=aliases)
    single = not isinstance(out_specs, (list, tuple))
    specs_o = [out_specs] if single else list(out_specs)
    shapes_o = [out_shape] if single else list(out_shape)
    n_in, n_out, k_in, k_out, n_scr = len(in_specs), len(specs_o), len(carry.arrays), len(carry.out_shape), len(scratch_shapes)

    def wrapped(*refs):
        ins, cins = refs[:n_in], refs[n_in:n_in + k_in]
        o0 = n_in + k_in
        outs, couts = refs[o0:o0 + n_out], refs[o0 + n_out:o0 + n_out + k_out]
        scr = refs[o0 + n_out + k_out:o0 + n_out + k_out + n_scr]
        send, recv = refs[-2:]
        ids = [pl.program_id(ax) for ax in range(len(grid))]
        first = functools.reduce(jnp.logical_and, [i == 0 for i in ids])
        last = functools.reduce(jnp.logical_and, [i == g - 1 for i, g in zip(ids, grid)])

        @pl.when(first)
        def _():
            carry.start(cins, couts, send, recv, 0)

        body(*ins, *outs, *scr)

        @pl.when(last)
        def _():
            carry.finish(cins, couts, send, recv, 0)

    aliases.update({n_in + i: n_out + o for i, o in carry.aliases.items()})
    call = pl.pallas_call(
        wrapped, name=name, grid=grid, in_specs=list(in_specs) + [ANY] * k_in, out_specs=specs_o + [ANY] * k_out,
        out_shape=shapes_o + carry.out_shape,
        scratch_shapes=list(scratch_shapes) + [pltpu.SemaphoreType.DMA((carry.n_sems,))] * 2,
        compiler_params=_cparams(("arbitrary",) * len(grid)), input_output_aliases=aliases)

    def run(*args):
        res = call(*args, *carry.arrays)
        main = res[:n_out]
        return (main[0] if single else list(main)), list(res[n_out:])

    return run


def _pick(dim, cands):
    for c in cands:
        if dim % c == 0:
            return c
    return dim


def _dot(a, b, ca, cb):
    return lax.dot_general(a.astype(MXU_DTYPE), b.astype(MXU_DTYPE), (((ca,), (cb,)), ((), ())),
                           preferred_element_type=F32)


@jax.custom_vjp
def mm(a, b):
    return _dot(a, b, 1, 0)


def _mm_f(a, b):
    return mm(a, b), (a, b)


def _mm_b(res, g):
    a, b = res
    return mm_nt(g, b).astype(a.dtype), mm_tn(a, g).astype(b.dtype)


@jax.custom_vjp
def mm_nt(a, b):
    return _dot(a, b, 1, 1)


def _mm_nt_f(a, b):
    return mm_nt(a, b), (a, b)


def _mm_nt_b(res, g):
    a, b = res
    return mm(g, b).astype(a.dtype), mm_tn(g, a).astype(b.dtype)


@jax.custom_vjp
def mm_tn(a, b):
    return _dot(a, b, 0, 0)


def _mm_tn_f(a, b):
    return mm_tn(a, b), (a, b)


def _mm_tn_b(res, g):
    a, b = res
    return mm_nt(b, g).astype(a.dtype), mm(a, g).astype(b.dtype)


mm.defvjp(_mm_f, _mm_b)
mm_nt.defvjp(_mm_nt_f, _mm_nt_b)
mm_tn.defvjp(_mm_tn_f, _mm_tn_b)


def _split3(a):
    h = a.astype(BF16)
    r = a - h.astype(F32)
    m = r.astype(BF16)
    l = (r - m.astype(F32)).astype(BF16)
    return h, m, l


def _exact_dot(t, a, ca, cb):
    out = None
    for p in _split3(a):
        d = lax.dot_general(t, p, (((ca,), (cb,)), ((), ())), preferred_element_type=F32)
        out = d if out is None else out + d
    return out


@jax.custom_vjp
def sel_l(t, a):
    return _exact_dot(t, a, 1, 0)


def _sel_l_f(t, a):
    return sel_l(t, a), t


def _sel_l_b(t, g):
    return jnp.zeros_like(t), _exact_dot(t, g, 0, 0)


sel_l.defvjp(_sel_l_f, _sel_l_b)


@jax.custom_vjp
def sel_r(a, t):
    out = None
    for p in _split3(a):
        d = lax.dot_general(p, t, (((1,), (0,)), ((), ())), preferred_element_type=F32)
        out = d if out is None else out + d
    return out


def _sel_r_f(a, t):
    return sel_r(a, t), t


def _sel_r_b(t, g):
    out = None
    for p in _split3(g):
        d = lax.dot_general(p, t, (((1,), (1,)), ((), ())), preferred_element_type=F32)
        out = d if out is None else out + d
    return out, jnp.zeros_like(t)


sel_r.defvjp(_sel_r_f, _sel_r_b)


def _sigmoid(x):
    return 1.0 / (1.0 + jnp.exp(-x))


def _silu(x):
    return x * _sigmoid(x)


def _softplus(x):
    return jnp.maximum(x, 0.0) + jnp.log(1.0 + jnp.exp(-jnp.abs(x)))


def _log_sigmoid(x):
    return -_softplus(-x)


def _gelu_tanh(x):
    c = math.sqrt(2.0 / math.pi)
    return 0.5 * x * (1.0 + jnp.tanh(c * (x + 0.044715 * (x * x * x))))


def _rms(x, w):
    return x * lax.rsqrt(jnp.mean(x * x, axis=-1, keepdims=True) + NORM_EPS) * w


def _tri(n, dtype=BF16):
    r = lax.broadcasted_iota(jnp.int32, (n, n), 0)
    c = lax.broadcasted_iota(jnp.int32, (n, n), 1)
    return (c <= r).astype(dtype)


def matmul(a, b, *, ta=False, tb=False, add=None, out_dtype=F32, out_shards=False, carry=None, name):
    m, k = (a.shape[1], a.shape[0]) if ta else a.shape
    b_sh = b.ndim == 3
    if b_sh:
        s, br, bc = b.shape
        k2, n = (s * bc, br) if tb else (br, s * bc)
    else:
        k2, n = (b.shape[1], b.shape[0]) if tb else b.shape
    assert k == k2, (a.shape, b.shape, ta, tb)
    tm = _pick(m, (1024, 512, 256, 128))
    tn = _pick(n, (1024, 512, 256, 128))
    tk = k if k <= MAX_TK else max(c for c in range(LANE, MAX_TK + 1, LANE) if k % c == 0)
    if b_sh and tb:
        tk = bc
    elif b_sh:
        tn = bc
    if out_shards:
        tn = n // 4
    nk = k // tk

    def body(*refs):
        if add is None:
            a_ref, b_ref, o_ref, acc = refs
        else:
            a_ref, b_ref, add_ref, o_ref, acc = refs
        kk = pl.program_id(2)

        @pl.when(kk == 0)
        def _():
            acc[...] = jnp.zeros_like(acc)

        acc[...] += _dot(a_ref[...], b_ref[...], 0 if ta else 1, 1 if tb else 0)

        @pl.when(kk == nk - 1)
        def _():
            r = acc[...]
            if add is not None:
                r = r + add_ref[...].astype(F32)
            o_ref[...] = r.astype(out_dtype)

    a_spec = pl.BlockSpec((tk, tm), lambda i, j, kk: (kk, i)) if ta else pl.BlockSpec((tm, tk), lambda i, j, kk: (i, kk))
    if b_sh and tb:
        b_spec = pl.BlockSpec((None, tn, tk), lambda i, j, kk: (kk, j, 0))
    elif b_sh:
        b_spec = pl.BlockSpec((None, tk, tn), lambda i, j, kk: (j, kk, 0))
    elif tb:
        b_spec = pl.BlockSpec((tn, tk), lambda i, j, kk: (j, kk))
    else:
        b_spec = pl.BlockSpec((tk, tn), lambda i, j, kk: (kk, j))
    in_specs, args = [a_spec, b_spec], [a, b]
    if add is not None:
        in_specs.append(pl.BlockSpec((tm, tn), lambda i, j, kk: (i, j)))
        args.append(add)
    if out_shards:
        out_spec = pl.BlockSpec((None, tm, tn), lambda i, j, kk: (j, i, 0))
        out_shape = jax.ShapeDtypeStruct((4, m, tn), out_dtype)
    else:
        out_spec = pl.BlockSpec((tm, tn), lambda i, j, kk: (i, j))
        out_shape = jax.ShapeDtypeStruct((m, n), out_dtype)
    return _pcall(
        body, name=name, grid=(m // tm, n // tn, nk), in_specs=in_specs, out_specs=out_spec, out_shape=out_shape,
        scratch_shapes=[pltpu.VMEM((tm, tn), F32)],
        compiler_params=_cparams(("parallel", "parallel", "arbitrary")), carry=carry,
    )(*args)


def rms_fwd(x, w, *, name):
    t, d = x.shape
    tb = _pick(t, (256, 128, 64))

    def body(x_ref, w_ref, o_ref):
        o_ref[...] = _rms(x_ref[...], w_ref[...]).astype(o_ref.dtype)

    return pl.pallas_call(
        body, name=name, grid=(t // tb,),
        in_specs=[pl.BlockSpec((tb, d), lambda i: (i, 0)), _full((1, d))],
        out_specs=pl.BlockSpec((tb, d), lambda i: (i, 0)),
        out_shape=jax.ShapeDtypeStruct((t, d), BF16),
        compiler_params=_cparams(("parallel",)),
    )(x, w)


def rms_bwd(x, w, dh, dres, *, name):
    t, d = x.shape
    tb = _pick(t, (256, 128, 64))

    def body(x_ref, w_ref, dh_ref, dres_ref, dx_ref, dw_ref):
        @pl.when(pl.program_id(0) == 0)
        def _():
            dw_ref[...] = jnp.zeros_like(dw_ref)

        _, vjp = jax.vjp(_rms, x_ref[...], w_ref[...])
        dx, dw = vjp(dh_ref[...].astype(F32))
        dx_ref[...] = dx + dres_ref[...]
        dw_ref[...] += dw

    row = pl.BlockSpec((tb, d), lambda i: (i, 0))
    return pl.pallas_call(
        body, name=name, grid=(t // tb,),
        in_specs=[row, _full((1, d)), row, row],
        out_specs=[row, _full((1, d))],
        out_shape=[jax.ShapeDtypeStruct((t, d), F32), jax.ShapeDtypeStruct((1, d), F32)],
        compiler_params=_cparams(("arbitrary",)),
    )(x, w, dh, dres)


def _swi(g, u):
    return _silu(g) * u


def swiglu_fwd(gate, up, *, name):
    t, f = gate.shape
    tb = _pick(t, (256, 128, 64))
    row = pl.BlockSpec((tb, f), lambda i: (i, 0))

    def body(g_ref, u_ref, o_ref):
        o_ref[...] = _swi(g_ref[...], u_ref[...]).astype(o_ref.dtype)

    return pl.pallas_call(
        body, name=name, grid=(t // tb,), in_specs=[row, row], out_specs=row,
        out_shape=jax.ShapeDtypeStruct((t, f), BF16), compiler_params=_cparams(("parallel",)),
    )(gate, up)


def swiglu_bwd(gate, up, dact, *, name):
    t, f = gate.shape
    tb = _pick(t, (128, 64))
    row = pl.BlockSpec((tb, f), lambda i: (i, 0))

    def body(g_ref, u_ref, d_ref, dg_ref, du_ref):
        _, vjp = jax.vjp(_swi, g_ref[...], u_ref[...])
        dg, du = vjp(d_ref[...])
        dg_ref[...] = dg.astype(dg_ref.dtype)
        du_ref[...] = du.astype(du_ref.dtype)

    return pl.pallas_call(
        body, name=name, grid=(t // tb,), in_specs=[row, row, row], out_specs=[row, row],
        out_shape=[jax.ShapeDtypeStruct((t, f), BF16)] * 2, compiler_params=_cparams(("parallel",)),
    )(gate, up, dact)


def loss_head(x, w, target, *, name):
    t, d = x.shape
    tb = _pick(t, (256, 128, 64))

    def f(xv, wv, tv):
        y = _rms(xv, wv)
        e = y - tv
        return 0.5 * jnp.sum(jnp.mean(e * e, axis=-1, keepdims=True), axis=0, keepdims=True)

    def body(x_ref, w_ref, t_ref, l_ref, dx_ref, dw_ref):
        @pl.when(pl.program_id(0) == 0)
        def _():
            l_ref[...] = jnp.zeros_like(l_ref)
            dw_ref[...] = jnp.zeros_like(dw_ref)

        val, vjp = jax.vjp(lambda a, b: f(a, b, t_ref[...]), x_ref[...], w_ref[...])
        dx, dw = vjp(jnp.ones((1, 1), F32))
        l_ref[...] += jnp.broadcast_to(val, l_ref.shape)
        dx_ref[...] = dx
        dw_ref[...] += dw

    row = pl.BlockSpec((tb, d), lambda i: (i, 0))
    return pl.pallas_call(
        body, name=name, grid=(t // tb,),
        in_specs=[row, _full((1, d)), row],
        out_specs=[_full((SUBLANE, LANE)), row, _full((1, d))],
        out_shape=[jax.ShapeDtypeStruct((SUBLANE, LANE), F32), jax.ShapeDtypeStruct((t, d), F32),
                   jax.ShapeDtypeStruct((1, d), F32)],
        compiler_params=_cparams(("arbitrary",)),
    )(x, w, target)


def _gla_chunk(q, k, v, g, glr, st, wg, bg, wn, tri):
    L, hk = q.shape
    la = _log_sigmoid(mm(glr, wg) + bg) / GLA_GATE_NORM
    bcum = sel_l(tri, la)
    b_last = jnp.sum(la, axis=0, keepdims=True)
    rows = lax.broadcasted_iota(jnp.int32, (L, 1), 0)
    b_mid = jnp.sum(jnp.where(rows <= L // 2, la, 0.0), axis=0, keepdims=True)
    qs = q * (hk ** -0.5)
    q_in = qs * jnp.exp(bcum - b_mid)
    k_in = k * jnp.exp(b_mid - bcum)
    scores = mm_nt(q_in, k_in) * tri.astype(F32)
    o_intra = mm(scores, v)
    k_st = k * jnp.exp(b_last - bcum)
    d_st = mm_tn(v, k_st)
    o_inter = mm_nt(qs * jnp.exp(bcum), st)
    st_new = jnp.exp(b_last) * st + d_st
    o = _rms(o_intra + o_inter, wn) * _silu(g)
    return o, st_new


def _gla_dims(d):
    dv = d // 2
    dk = dv // 2
    return dk, dv, dk // GLA_HEADS, dv // GLA_HEADS


def gla_fwd(proj, glr_col, wg, bg, wn, dv, carry=None):
    t = proj.shape[0]
    dk, dv, hk, hv = _gla_dims(2 * dv)
    L, H = CHUNK, GLA_HEADS
    nc = t // L
    wq = 2 * dk + 2 * dv

    def body(p_ref, glr_ref, wg_ref, bg_ref, wn_ref, o_ref, sp_ref, st):
        @pl.when(pl.program_id(0) == 0)
        def _():
            st[...] = jnp.zeros_like(st)

        tri = _tri(L)
        glr = glr_ref[...]
        for h in range(H):
            q = p_ref[:, h * hk:(h + 1) * hk]
            k = p_ref[:, dk + h * hk:dk + (h + 1) * hk]
            v = p_ref[:, 2 * dk + h * hv:2 * dk + (h + 1) * hv]
            g = p_ref[:, 2 * dk + dv + h * hv:2 * dk + dv + (h + 1) * hv]
            s_prev = st[h]
            sp_ref[0, h] = s_prev
            o, s_new = _gla_chunk(q, k, v, g, glr, s_prev, wg_ref[:, h * hk:(h + 1) * hk],
                                  bg_ref[:, h * hk:(h + 1) * hk], wn_ref[...], tri)
            o_ref[:, h * hv:(h + 1) * hv] = o.astype(o_ref.dtype)
            st[h] = s_new

    return _pcall(
        body, carry=carry, name="gla_fwd", grid=(nc,),
        in_specs=[pl.BlockSpec((L, wq), lambda c: (c, 0)), pl.BlockSpec((L, LANE), lambda c: (c, glr_col)),
                  _full(wg.shape), _full(bg.shape), _full(wn.shape)],
        out_specs=[pl.BlockSpec((L, dv), lambda c: (c, 0)), pl.BlockSpec((1, H, hv, hk), lambda c: (c, 0, 0, 0))],
        out_shape=[jax.ShapeDtypeStruct((t, dv), BF16), jax.ShapeDtypeStruct((nc, H, hv, hk), F32)],
        scratch_shapes=[pltpu.VMEM((H, hv, hk), F32)],
        compiler_params=_cparams(("arbitrary",)),
    )(proj, proj, wg, bg, wn)


def gla_bwd(proj, glr_col, wg, bg, wn, sprev, do, dv, gla_w, carry=None):
    t = proj.shape[0]
    dk, _, hk, hv = _gla_dims(2 * dv)
    L, H = CHUNK, GLA_HEADS
    nc = t // L
    wq = 2 * dk + 2 * dv

    def body(p_ref, glr_ref, wg_ref, bg_ref, wn_ref, sp_ref, do_ref, dp_ref, dwg_ref, dbg_ref, dwn_ref, dst):
        @pl.when(pl.program_id(0) == 0)
        def _():
            dst[...] = jnp.zeros_like(dst)
            dwg_ref[...] = jnp.zeros_like(dwg_ref)
            dbg_ref[...] = jnp.zeros_like(dbg_ref)
            dwn_ref[...] = jnp.zeros_like(dwn_ref)

        tri = _tri(L)
        glr = glr_ref[...]
        dglr = jnp.zeros_like(glr)
        for h in range(H):
            ks = slice(h * hk, (h + 1) * hk)
            q = p_ref[:, ks]
            k = p_ref[:, dk + h * hk:dk + (h + 1) * hk]
            v = p_ref[:, 2 * dk + h * hv:2 * dk + (h + 1) * hv]
            g = p_ref[:, 2 * dk + dv + h * hv:2 * dk + dv + (h + 1) * hv]
            f = functools.partial(_gla_chunk, tri=tri)
            _, vjp = jax.vjp(f, q, k, v, g, glr, sp_ref[0, h], wg_ref[:, ks], bg_ref[:, ks], wn_ref[...])
            dq, dkk, dvv, dg, dgl, ds, dwg, dbg, dwn = vjp((do_ref[:, h * hv:(h + 1) * hv], dst[h]))
            dp_ref[:, ks] = dq.astype(dp_ref.dtype)
            dp_ref[:, dk + h * hk:dk + (h + 1) * hk] = dkk.astype(dp_ref.dtype)
            dp_ref[:, 2 * dk + h * hv:2 * dk + (h + 1) * hv] = dvv.astype(dp_ref.dtype)
            dp_ref[:, 2 * dk + dv + h * hv:2 * dk + dv + (h + 1) * hv] = dg.astype(dp_ref.dtype)
            dglr = dglr + dgl
            dst[h] = ds
            dwg_ref[:, ks] += dwg
            dbg_ref[:, ks] += dbg
            dwn_ref[...] += dwn
        dp_ref[:, wq:wq + LANE] = dglr.astype(dp_ref.dtype)
        if gla_w > wq + LANE:
            dp_ref[:, wq + LANE:] = jnp.zeros((L, gla_w - wq - LANE), dp_ref.dtype)

    rev = lambda c: nc - 1 - c
    return _pcall(
        body, carry=carry, name="gla_bwd", grid=(nc,),
        in_specs=[pl.BlockSpec((L, wq), lambda c: (rev(c), 0)), pl.BlockSpec((L, LANE), lambda c: (rev(c), glr_col)),
                  _full(wg.shape), _full(bg.shape), _full(wn.shape),
                  pl.BlockSpec((1, H, hv, hk), lambda c: (rev(c), 0, 0, 0)),
                  pl.BlockSpec((L, dv), lambda c: (rev(c), 0))],
        out_specs=[pl.BlockSpec((L, gla_w), lambda c: (rev(c), 0)),
                   _full(wg.shape), _full(bg.shape), _full(wn.shape)],
        out_shape=[jax.ShapeDtypeStruct(proj.shape, BF16),
                   jax.ShapeDtypeStruct(wg.shape, F32), jax.ShapeDtypeStruct(bg.shape, F32),
                   jax.ShapeDtypeStruct(wn.shape, F32)],
        scratch_shapes=[pltpu.VMEM((H, hv, hk), F32)],
        compiler_params=_cparams(("arbitrary",)),
    )(proj, proj, wg, bg, wn, sprev, do)


def _shift_down(x, tail, s):
    if s == 0:
        return x
    r = pltpu.roll(x, s, 0)
    rows = lax.broadcasted_iota(jnp.int32, tail.shape, 0)
    top = jnp.where(rows < s, pltpu.roll(tail, s, 0), r[:SUBLANE])
    return jnp.concatenate([top, r[SUBLANE:]], axis=0)


def _shift_up(x, head, s):
    if s == 0:
        return x
    n = x.shape[0]
    r = pltpu.roll(x, n - s, 0)
    rows = lax.broadcasted_iota(jnp.int32, head.shape, 0)
    bottom = jnp.where(rows >= SUBLANE - s, pltpu.roll(head, SUBLANE - s, 0), r[n - SUBLANE:])
    return jnp.concatenate([r[:n - SUBLANE], bottom], axis=0)


def _conv(x, prev, w, b):
    y = b
    for k in range(CONV_WIDTH):
        y = y + w[k:k + 1, :] * _shift_down(x, prev, CONV_WIDTH - 1 - k)
    return y


def _conv_bwd(dy, nxt, x, prev, w):
    dx = None
    dws = []
    for k in range(CONV_WIDTH):
        s = CONV_WIDTH - 1 - k
        term = w[k:k + 1, :] * _shift_up(dy, nxt, s)
        dx = term if dx is None else dx + term
        dws.append(jnp.sum(dy * _shift_down(x, prev, s), axis=0, keepdims=True))
    return dx, jnp.concatenate(dws, axis=0), jnp.sum(dy, axis=0, keepdims=True)


def _scan_fwd(a, u):
    n = a.shape[0]
    rows = lax.broadcasted_iota(jnp.int32, a.shape, 0)
    s = 1
    while s < n:
        a_sh = jnp.where(rows < s, 1.0, pltpu.roll(a, s, 0))
        u_sh = jnp.where(rows < s, 0.0, pltpu.roll(u, s, 0))
        u = a * u_sh + u
        a = a * a_sh
        s *= 2
    return a, u


def _scan_rev(c, d):
    n = c.shape[0]
    rows = lax.broadcasted_iota(jnp.int32, c.shape, 0)
    s = 1
    while s < n:
        c_sh = jnp.where(rows >= n - s, 0.0, pltpu.roll(c, n - s, 0))
        d_sh = jnp.where(rows >= n - s, 0.0, pltpu.roll(d, n - s, 0))
        d = d + c * d_sh
        c = c * c_sh
        s *= 2
    return d


def _expm1(x):
    small = x * (1.0 + x * (0.5 + x * (1.0 / 6.0 + x * (1.0 / 24.0))))
    return jnp.where(jnp.abs(x) < 1e-2, small, jnp.exp(x) - 1.0)


def _lru_gates(xc, pa, pi, lam):
    r = _sigmoid(pa)
    i = _sigmoid(pi)
    log_a = LRU_C * r * _log_sigmoid(lam)
    a = jnp.exp(log_a)
    u = jnp.sqrt(-_expm1(2.0 * log_a)) * (i * xc)
    return a, u


def _lru_out(h, gate):
    return h * _gelu_tanh(gate)


def _blockdiag(xc, w_ref, b):
    nb = w_ref.shape[0]
    outs = [mm(xc[:, n * LRU_BLOCK:(n + 1) * LRU_BLOCK], w_ref[n]) for n in range(nb)]
    return jnp.concatenate(outs, axis=1) + b


def lru_fwd(proj, xcol, lw, cw, cb, wa, ba, wi, bi, lam):
    t = proj.shape[0]
    tb = _pick(t, (256, 128, 64))
    nb = t // tb

    def body(x_ref, xp_ref, g_ref, cw_ref, cb_ref, wa_ref, ba_ref, wi_ref, bi_ref, lam_ref, o_ref, hin_ref, hc):
        i = pl.program_id(0)

        @pl.when(i == 0)
        def _():
            hc[...] = jnp.zeros_like(hc)

        prev = jnp.where(i == 0, 0.0, xp_ref[...])
        xc = _conv(x_ref[...], prev, cw_ref[...], cb_ref[...])
        a, u = _lru_gates(xc, _blockdiag(xc, wa_ref, ba_ref[...]), _blockdiag(xc, wi_ref, bi_ref[...]), lam_ref[...])
        acum, h0 = _scan_fwd(a, u)
        h = h0 + acum * hc[...]
        hin_ref[0] = hc[...]
        hc[...] = h[tb - 1:tb, :]
        o_ref[...] = _lru_out(h, g_ref[...]).astype(o_ref.dtype)

    row = lambda col: pl.BlockSpec((tb, lw), lambda i: (i, col))
    return pl.pallas_call(
        body, name="lru_fwd", grid=(nb,),
        in_specs=[row(xcol), pl.BlockSpec((SUBLANE, lw), lambda i: (jnp.maximum(i * (tb // SUBLANE) - 1, 0), xcol)),
                  row(xcol + 1),
                  _full(cw.shape), _full(cb.shape), _full(wa.shape), _full(ba.shape), _full(wi.shape), _full(bi.shape),
                  _full(lam.shape)],
        out_specs=[pl.BlockSpec((tb, lw), lambda i: (i, 0)), pl.BlockSpec((1, 1, lw), lambda i: (i, 0, 0))],
        out_shape=[jax.ShapeDtypeStruct((t, lw), BF16), jax.ShapeDtypeStruct((nb, 1, lw), F32)],
        scratch_shapes=[pltpu.VMEM((1, lw), F32)],
        compiler_params=_cparams(("arbitrary",)),
    )(proj, proj, proj, cw, cb, wa, ba, wi, bi, lam)


def lru_bwd(proj, xcol, lw, cw, cb, wa, ba, wi, bi, lam, hin, dmix, docol, dproj, carry=None):
    t = proj.shape[0]
    tb = _pick(t, (256, 128, 64))
    nb = t // tb
    nblk = wa.shape[0]
    assert xcol % 2 == 0

    def body(x_ref, xp_ref, g_ref, cw_ref, cb_ref, wa_ref, ba_ref, wi_ref, bi_ref, lam_ref, hin_ref, do_ref, _,
             dxg_ref, dcw_ref, dcb_ref, dwa_ref, dba_ref, dwi_ref, dbi_ref, dlam_ref, gc, dxcn):
        pid = pl.program_id(0)
        i = nb - 1 - pid

        @pl.when(pid == 0)
        def _():
            gc[...] = jnp.zeros_like(gc)
            dxcn[...] = jnp.zeros_like(dxcn)
            for r in (dcw_ref, dcb_ref, dwa_ref, dba_ref, dwi_ref, dbi_ref, dlam_ref):
                r[...] = jnp.zeros_like(r)

        x = x_ref[...]
        prev = jnp.where(i == 0, 0.0, xp_ref[...])
        cw_v = cw_ref[...]
        xc = _conv(x, prev, cw_v, cb_ref[...])
        pa = _blockdiag(xc, wa_ref, ba_ref[...])
        pi = _blockdiag(xc, wi_ref, bi_ref[...])
        (a, u), vjp_g = jax.vjp(_lru_gates, xc, pa, pi, lam_ref[...])
        acum, h0 = _scan_fwd(a, u)
        hi = hin_ref[0]
        h = h0 + acum * hi
        rows = lax.broadcasted_iota(jnp.int32, h.shape, 0)
        hprev = jnp.where(rows < 1, hi, pltpu.roll(h, 1, 0))
        _, vjp_o = jax.vjp(_lru_out, h, g_ref[...])
        dh, dgate = vjp_o(do_ref[...].astype(F32))
        c = jnp.where(rows >= tb - 1, 0.0, pltpu.roll(a, tb - 1, 0))
        g = _scan_rev(c, dh + jnp.where(rows == tb - 1, gc[...], 0.0))
        gc[...] = a[0:1, :] * g[0:1, :]
        dxc, dpa, dpi, dlam = vjp_g((g * hprev, g))
        dlam_ref[...] += dlam
        dba_ref[...] += jnp.sum(dpa, axis=0, keepdims=True)
        dbi_ref[...] += jnp.sum(dpi, axis=0, keepdims=True)
        parts = []
        for n in range(nblk):
            sl = slice(n * LRU_BLOCK, (n + 1) * LRU_BLOCK)
            dwa_ref[n] += mm_tn(xc[:, sl], dpa[:, sl])
            dwi_ref[n] += mm_tn(xc[:, sl], dpi[:, sl])
            parts.append(mm_nt(dpa[:, sl], wa_ref[n]) + mm_nt(dpi[:, sl], wi_ref[n]))
        dxc = dxc + jnp.concatenate(parts, axis=1)
        dx, dcw, dcb = _conv_bwd(dxc, dxcn[...], x, prev, cw_v)
        dxcn[...] = dxc[:SUBLANE]
        dcw_ref[...] += dcw
        dcb_ref[...] += dcb
        dxg_ref[:, :lw] = dx.astype(dxg_ref.dtype)
        dxg_ref[:, lw:] = dgate.astype(dxg_ref.dtype)

    row = lambda col: pl.BlockSpec((tb, lw), lambda p: (nb - 1 - p, col))
    params = [cw, cb, wa, ba, wi, bi, lam]
    return _pcall(
        body, carry=carry, name="lru_bwd", grid=(nb,),
        in_specs=[row(xcol),
                  pl.BlockSpec((SUBLANE, lw), lambda p: (jnp.maximum((nb - 1 - p) * (tb // SUBLANE) - 1, 0), xcol)),
                  row(xcol + 1)]
        + [_full(p.shape) for p in params]
        + [pl.BlockSpec((1, 1, lw), lambda p: (nb - 1 - p, 0, 0)), row(docol), ANY],
        out_specs=[pl.BlockSpec((tb, 2 * lw), lambda p: (nb - 1 - p, xcol // 2))] + [_full(p.shape) for p in params],
        out_shape=[jax.ShapeDtypeStruct(dproj.shape, dproj.dtype)]
        + [jax.ShapeDtypeStruct(p.shape, F32) for p in params],
        input_output_aliases={12: 0},
        scratch_shapes=[pltpu.VMEM((1, lw), F32), pltpu.VMEM((SUBLANE, lw), F32)],
        compiler_params=_cparams(("arbitrary",)),
    )(proj, proj, proj, *params, hin, dmix, dproj)


def conv_silu_fwd(proj, col0, width, cw, cb, carry=None):
    t = proj.shape[0]
    tb = _pick(t, (512, 256, 128, 64))
    cbw = _pick(width, (512, 256, 128))
    off = col0 // cbw
    assert col0 % cbw == 0

    def body(x_ref, xp_ref, w_ref, b_ref, o_ref):
        prev = jnp.where(pl.program_id(1) == 0, 0.0, xp_ref[...])
        o_ref[...] = _silu(_conv(x_ref[...], prev, w_ref[...], b_ref[...]))

    return _pcall(
        body, carry=carry, name="conv_silu_fwd", grid=(width // cbw, t // tb),
        in_specs=[pl.BlockSpec((tb, cbw), lambda j, i: (i, off + j)),
                  pl.BlockSpec((SUBLANE, cbw), lambda j, i: (jnp.maximum(i * (tb // SUBLANE) - 1, 0), off + j)),
                  pl.BlockSpec((CONV_WIDTH, cbw), lambda j, i: (0, j)), pl.BlockSpec((1, cbw), lambda j, i: (0, j))],
        out_specs=pl.BlockSpec((tb, cbw), lambda j, i: (i, j)),
        out_shape=jax.ShapeDtypeStruct((t, width), F32),
        compiler_params=_cparams(("parallel", "arbitrary")),
    )(proj, proj, cw, cb)


def conv_silu_bwd(proj, col0, width, cw, cb, dact, dproj, carry=None):
    t = proj.shape[0]
    tb = _pick(t, (512, 256, 128, 64))
    nb = t // tb
    cbw = _pick(width, (512, 256, 128))
    off = col0 // cbw

    def body(x_ref, xp_ref, w_ref, b_ref, d_ref, _, dx_ref, dw_ref, db_ref, nxt):
        pid = pl.program_id(1)
        i = nb - 1 - pid

        @pl.when(pid == 0)
        def _():
            nxt[...] = jnp.zeros_like(nxt)
            dw_ref[...] = jnp.zeros_like(dw_ref)
            db_ref[...] = jnp.zeros_like(db_ref)

        x = x_ref[...]
        prev = jnp.where(i == 0, 0.0, xp_ref[...])
        w = w_ref[...]
        _, vjp = jax.vjp(_silu, _conv(x, prev, w, b_ref[...]))
        (dcv,) = vjp(d_ref[...])
        dx, dw, db = _conv_bwd(dcv, nxt[...], x, prev, w)
        nxt[...] = dcv[:SUBLANE]
        dx_ref[...] = dx.astype(dx_ref.dtype)
        dw_ref[...] += dw
        db_ref[...] += db

    return _pcall(
        body, carry=carry, name="conv_silu_bwd", grid=(width // cbw, nb),
        in_specs=[pl.BlockSpec((tb, cbw), lambda j, p: (nb - 1 - p, off + j)),
                  pl.BlockSpec((SUBLANE, cbw),
                               lambda j, p: (jnp.maximum((nb - 1 - p) * (tb // SUBLANE) - 1, 0), off + j)),
                  pl.BlockSpec((CONV_WIDTH, cbw), lambda j, p: (0, j)), pl.BlockSpec((1, cbw), lambda j, p: (0, j)),
                  pl.BlockSpec((tb, cbw), lambda j, p: (nb - 1 - p, j)), ANY],
        out_specs=[pl.BlockSpec((tb, cbw), lambda j, p: (nb - 1 - p, off + j)),
                   pl.BlockSpec((CONV_WIDTH, cbw), lambda j, p: (0, j)), pl.BlockSpec((1, cbw), lambda j, p: (0, j))],
        out_shape=[jax.ShapeDtypeStruct(dproj.shape, dproj.dtype), jax.ShapeDtypeStruct(cw.shape, F32),
                   jax.ShapeDtypeStruct(cb.shape, F32)],
        scratch_shapes=[pltpu.VMEM((SUBLANE, cbw), F32)],
        input_output_aliases={5: 0},
        compiler_params=_cparams(("parallel", "arbitrary")),
    )(proj, proj, cw, cb, dact, dproj)


def _dt_expand(raw, bias, e):
    return sel_r(_softplus(raw + bias), e)


def dt_fwd(proj, dtcol, bias, e):
    t = proj.shape[0]
    di = e.shape[1]
    tb = _pick(t, (512, 256, 128, 64))

    def body(r_ref, b_ref, e_ref, o_ref):
        o_ref[...] = _dt_expand(r_ref[...], b_ref[...], e_ref[...])

    return pl.pallas_call(
        body, name="dt_fwd", grid=(t // tb,),
        in_specs=[pl.BlockSpec((tb, LANE), lambda i: (i, dtcol)), _full(bias.shape), _full(e.shape)],
        out_specs=pl.BlockSpec((tb, di), lambda i: (i, 0)),
        out_shape=jax.ShapeDtypeStruct((t, di), F32),
        compiler_params=_cparams(("parallel",)),
    )(proj, bias, e)


def dt_bwd(proj, dtcol, bias, e, ddte, dproj):
    t = proj.shape[0]
    di = e.shape[1]
    tb = _pick(t, (512, 256, 128, 64))
    tail = dproj.shape[1] - dtcol * LANE
    assert (dtcol * LANE) % tail == 0

    def body(r_ref, b_ref, e_ref, d_ref, _, dr_ref, db_ref):
        @pl.when(pl.program_id(0) == 0)
        def _():
            db_ref[...] = jnp.zeros_like(db_ref)

        e_v = e_ref[...]
        _, vjp = jax.vjp(lambda r, b: _dt_expand(r, b, e_v), r_ref[...], b_ref[...])
        dr, db = vjp(d_ref[...])
        dr_ref[:, :LANE] = dr.astype(dr_ref.dtype)
        if tail > LANE:
            dr_ref[:, LANE:] = jnp.zeros((tb, tail - LANE), dr_ref.dtype)
        db_ref[...] += db

    return pl.pallas_call(
        body, name="dt_bwd", grid=(t // tb,),
        in_specs=[pl.BlockSpec((tb, LANE), lambda i: (i, dtcol)), _full(bias.shape), _full(e.shape),
                  pl.BlockSpec((tb, di), lambda i: (i, 0)), ANY],
        out_specs=[pl.BlockSpec((tb, tail), lambda i: (i, dtcol * LANE // tail)), _full(bias.shape)],
        out_shape=[jax.ShapeDtypeStruct(dproj.shape, dproj.dtype), jax.ShapeDtypeStruct(bias.shape, F32)],
        input_output_aliases={4: 0},
        compiler_params=_cparams(("arbitrary",)),
    )(proj, bias, e, ddte, dproj)


def head_expand(p, e, *, transpose=False, name):
    di = e.shape[1]

    def body(p_ref, e_ref, o_ref):
        if transpose:
            o_ref[...] = _sel_r_b(e_ref[...], p_ref[...])[0]
        else:
            o_ref[...] = sel_r(p_ref[...], e_ref[...])

    oshape = (SUBLANE, LANE) if transpose else (SUBLANE, di)
    return pl.pallas_call(
        body, name=name, in_specs=[_full(p.shape), _full(e.shape)], out_specs=_full(oshape),
        out_shape=jax.ShapeDtypeStruct(oshape, F32), compiler_params=_cparams(None), grid=(1,),
    )(p, e)


def _ssd_chunk(x, z, bm, cm, dte, st, alog, dskip, gn, tri, cmask, dmask, bd):
    L, gw = x.shape
    reps = gw // L
    a = dte * (-jnp.exp(alog))
    acs = sel_l(tri, a)
    acs_last = jnp.sum(a, axis=0, keepdims=True)
    arow = jnp.sum(acs * dmask, axis=0, keepdims=True)
    dtrow = jnp.sum(dte * dmask, axis=0, keepdims=True)
    cb = mm_nt(cm, jnp.concatenate([bm] * reps, axis=0))
    wts = cb * (jnp.exp(jnp.minimum(acs - arow, 0.0)) * cmask) * dtrow
    xbd = jnp.concatenate([x] * reps, axis=0) * bd
    xw = x * (jnp.exp(acs_last - acs) * dte)
    y = mm(wts, xbd) + mm(cm, st) * jnp.exp(acs) + dskip * x
    st_new = jnp.exp(acs_last) * st + mm_tn(bm, xw)
    return _rms(y * _silu(z), gn), st_new


def _ssd_dims(di):
    gw = di // SSD_GROUPS
    assert CHUNK == SSD_HEAD_DIM and gw % LANE == 0
    return gw, SSD_STATE


def _ssd_masks(gw):
    L = CHUNK
    r = jnp.arange(L)[:, None]
    c = jnp.arange(gw)[None, :]
    cmask = ((c % L) <= r).astype(F32)
    dmask = ((c % L) == r).astype(F32)
    rr = jnp.arange(gw)
    bd = ((rr[:, None] // L) == (rr[None, :] // L)).astype(F32)
    tri = (jnp.arange(L)[None, :] <= jnp.arange(L)[:, None]).astype(BF16)
    return tri, cmask, dmask, bd


def ssd_fwd(xs, proj, dte, alog_e, dskip_e, gn, carry=None):
    t, di = dte.shape
    gw, n = _ssd_dims(di)
    L, G = CHUNK, SSD_GROUPS
    nc = t // L
    masks = _ssd_masks(gw)
    cdim = xs.shape[1]

    def body(x_ref, z_ref, dt_ref, al_ref, ds_ref, gn_ref, tri_ref, cm_ref, dm_ref, bd_ref, y_ref, sp_ref, st):
        @pl.when(pl.program_id(0) == 0)
        def _():
            st[...] = jnp.zeros_like(st)

        for g in range(G):
            ch = slice(g * gw, (g + 1) * gw)
            s_prev = st[g]
            sp_ref[0, g] = s_prev
            y, s_new = _ssd_chunk(x_ref[:, ch], z_ref[:, ch], x_ref[:, di + g * n:di + (g + 1) * n],
                                  x_ref[:, di + (G + g) * n:di + (G + g + 1) * n], dt_ref[:, ch], s_prev,
                                  al_ref[0:1, ch], ds_ref[0:1, ch], gn_ref[:, ch], tri_ref[...], cm_ref[...],
                                  dm_ref[...], bd_ref[...])
            y_ref[:, ch] = y.astype(y_ref.dtype)
            st[g] = s_new

    row = lambda w: pl.BlockSpec((L, w), lambda c: (c, 0))
    return _pcall(
        body, carry=carry, name="ssd_fwd", grid=(nc,),
        in_specs=[row(cdim), row(di), row(di), _full(alog_e.shape), _full(dskip_e.shape), _full(gn.shape)]
        + [_full(m.shape) for m in masks],
        out_specs=[row(di), pl.BlockSpec((1, G, n, gw), lambda c: (c, 0, 0, 0))],
        out_shape=[jax.ShapeDtypeStruct((t, di), BF16), jax.ShapeDtypeStruct((nc, G, n, gw), F32)],
        scratch_shapes=[pltpu.VMEM((G, n, gw), F32)],
        compiler_params=_cparams(("arbitrary",)),
    )(xs, proj, dte, alog_e, dskip_e, gn, *masks)


def ssd_bwd(xs, proj, dte, alog_e, dskip_e, gn, sprev, dy, dproj_shape, carry=None):
    t, di = dte.shape
    gw, n = _ssd_dims(di)
    L, G = CHUNK, SSD_GROUPS
    nc = t // L
    masks = _ssd_masks(gw)
    cdim = xs.shape[1]

    def body(x_ref, z_ref, dt_ref, al_ref, ds_ref, gn_ref, tri_ref, cm_ref, dm_ref, bd_ref, sp_ref, dy_ref,
             dxs_ref, dz_ref, ddt_ref, dal_ref, dds_ref, dgn_ref, dst):
        @pl.when(pl.program_id(0) == 0)
        def _():
            dst[...] = jnp.zeros_like(dst)
            dal_ref[...] = jnp.zeros_like(dal_ref)
            dds_ref[...] = jnp.zeros_like(dds_ref)
            dgn_ref[...] = jnp.zeros_like(dgn_ref)

        f = functools.partial(_ssd_chunk, tri=tri_ref[...], cmask=cm_ref[...], dmask=dm_ref[...], bd=bd_ref[...])
        for g in range(G):
            ch = slice(g * gw, (g + 1) * gw)
            bs = slice(di + g * n, di + (g + 1) * n)
            cs = slice(di + (G + g) * n, di + (G + g + 1) * n)
            _, vjp = jax.vjp(f, x_ref[:, ch], z_ref[:, ch], x_ref[:, bs], x_ref[:, cs], dt_ref[:, ch], sp_ref[0, g],
                             al_ref[0:1, ch], ds_ref[0:1, ch], gn_ref[:, ch])
            dx, dz, db, dc, ddt, ds, dal, dds, dgn = vjp((dy_ref[:, ch], dst[g]))
            dxs_ref[:, ch] = dx
            dxs_ref[:, bs] = db
            dxs_ref[:, cs] = dc
            dz_ref[:, ch] = dz.astype(dz_ref.dtype)
            ddt_ref[:, ch] = ddt
            dst[g] = ds
            dal_ref[:, ch] += dal
            dds_ref[:, ch] += dds
            dgn_ref[:, ch] += dgn

    row = lambda w: pl.BlockSpec((L, w), lambda c: (nc - 1 - c, 0))
    acc = _full((1, di))
    acc_shape = jax.ShapeDtypeStruct((1, di), F32)
    return _pcall(
        body, carry=carry, name="ssd_bwd", grid=(nc,),
        in_specs=[row(cdim), row(di), row(di), _full(alog_e.shape), _full(dskip_e.shape), _full(gn.shape)]
        + [_full(m.shape) for m in masks]
        + [pl.BlockSpec((1, G, n, gw), lambda c: (nc - 1 - c, 0, 0, 0)), row(di)],
        out_specs=[row(cdim), row(di), row(di), acc, acc, acc],
        out_shape=[jax.ShapeDtypeStruct((t, cdim), F32), jax.ShapeDtypeStruct(dproj_shape, BF16),
                   jax.ShapeDtypeStruct((t, di), F32), acc_shape, acc_shape, acc_shape],
        scratch_shapes=[pltpu.VMEM((G, n, gw), F32)],
        compiler_params=_cparams(("arbitrary",)),
    )(xs, proj, dte, alog_e, dskip_e, gn, *masks, sprev, dy)


def _rows2d(a):
    return a.reshape(-1, a.shape[-1])


def _row_tile(rows, cols):
    cap = max(SUBLANE, (1 << 19) // max(cols, 1))
    for c in (2048, 1024, 512, 256, 128, 64, 32, 16, 8):
        if c <= cap and rows % c == 0:
            return c
    return rows


def chip_sum(g, r, core, *, name):
    shape = r.shape
    cols = shape[-1]
    g4 = g.reshape(4, 2, -1, cols)
    r3 = r.reshape(4, -1, cols)
    rows = r3.shape[1]
    tr = _row_tile(rows, cols)

    def body(c_ref, g_ref, r_ref, o_ref):
        o_ref[...] = (g_ref[...].astype(F32) + r_ref[...].astype(F32)).astype(o_ref.dtype)

    out = pl.pallas_call(
        body, name=name,
        grid_spec=pltpu.PrefetchScalarGridSpec(
            num_scalar_prefetch=1, grid=(4, rows // tr),
            in_specs=[pl.BlockSpec((None, None, tr, cols), lambda j, i, c: (j, c[0], i, 0)),
                      pl.BlockSpec((None, tr, cols), lambda j, i, c: (j, i, 0))],
            out_specs=pl.BlockSpec((None, tr, cols), lambda j, i, c: (j, i, 0))),
        out_shape=jax.ShapeDtypeStruct(r3.shape, BF16), compiler_params=_cparams(("parallel", "parallel")),
    )(core.reshape(1).astype(jnp.int32), g4, r3)
    return out.reshape(shape)


def mesh_sum(p, r, chip, core, *, name):
    shape = p.shape[1:]
    cols = shape[-1]
    p3 = p.reshape(4, -1, cols)
    r3 = r.reshape(3, -1, cols)
    rows = p3.shape[1]
    tr = _row_tile(rows, 2 * cols)

    def body(c_ref, p_ref, r_ref, o_ref):
        o_ref[...] = ((p_ref[...].astype(F32) + r_ref[0].astype(F32)) + r_ref[1].astype(F32)) + r_ref[2].astype(F32)

    out = pl.pallas_call(
        body, name=name,
        grid_spec=pltpu.PrefetchScalarGridSpec(
            num_scalar_prefetch=1, grid=(rows // tr,),
            in_specs=[pl.BlockSpec((None, tr, cols), lambda i, c: (c[0], i, 0)),
                      pl.BlockSpec((3, tr, cols), lambda i, c: (0, i, 0))],
            out_specs=pl.BlockSpec((None, tr, cols), lambda i, c: (c[1], i, 0))),
        out_shape=jax.ShapeDtypeStruct((2, rows, cols), F32), compiler_params=_cparams(("parallel",)),
    )(jnp.stack([chip, core]).astype(jnp.int32), p3, r3)
    return out.reshape((2,) + shape)


def sum_leading(x, *, name):
    k, r, c = x.shape
    tr = _row_tile(r, c * k)
    def body(x_ref, o_ref):
        acc = x_ref[0]
        for i in range(1, k):
            acc = acc + x_ref[i]
        o_ref[...] = acc

    return pl.pallas_call(
        body, name=name, grid=(r // tr,), in_specs=[pl.BlockSpec((k, tr, c), lambda i: (0, i, 0))],
        out_specs=pl.BlockSpec((tr, c), lambda i: (i, 0)),
        out_shape=jax.ShapeDtypeStruct((r, c), F32), compiler_params=_cparams(("parallel",)),
    )(x)


def adamw(w, g, m, v, *, name):
    shape = w.shape
    w2, g2, m2, v2 = (_rows2d(a) for a in (w, g, m, v))
    r, c = w2.shape
    tr = _row_tile(r, 2 * c)
    row = pl.BlockSpec((tr, c), lambda i: (i, 0))
    c1 = 1.0 - ADAM_B1 ** ADAM_STEP
    c2 = 1.0 - ADAM_B2 ** ADAM_STEP

    def body(w_ref, g_ref, m_ref, v_ref, d_ref, mo_ref, vo_ref):
        gv = g_ref[...]
        mn = ADAM_B1 * m_ref[...] + (1.0 - ADAM_B1) * gv
        vn = ADAM_B2 * v_ref[...] + (1.0 - ADAM_B2) * (gv * gv)
        d_ref[...] = -ADAM_LR * ((mn / c1) / (jnp.sqrt(vn / c2) + ADAM_EPS) + ADAM_WD * w_ref[...])
        mo_ref[...] = mn
        vo_ref[...] = vn

    outs = pl.pallas_call(
        body, name=name, grid=(r // tr,), in_specs=[row] * 4, out_specs=[row] * 3,
        out_shape=[jax.ShapeDtypeStruct((r, c), F32)] * 3, compiler_params=_cparams(("parallel",)),
    )(w2, g2, m2, v2)
    return tuple(o.reshape(shape) for o in outs)


ANY = pl.BlockSpec(memory_space=pl.ANY)


def _place():
    x, y, c = lax.axis_index("x"), lax.axis_index("y"), lax.axis_index("c")
    chips = [(1 - x, y), (x, 1 - y), (1 - x, 1 - y)]
    return x, y, c, chips


def gather8(block):
    m, n = block.shape

    def body(x_ref, out_ref, send_sems, recv_sems, local_sem):
        x, y, c, chips = _place()
        me, sibling = (x, y, c), (x, y, 1 - c)

        def rows(px, py, pc):
            return out_ref.at[4 * px + 2 * py + pc]

        def copy(k, blk, to, src=None):
            return pltpu.make_async_remote_copy(
                src_ref=rows(*blk) if src is None else src, dst_ref=rows(*blk), send_sem=send_sems.at[k],
                recv_sem=recv_sems.at[k], device_id=to, device_id_type=MESH)

        mine = pltpu.make_async_copy(x_ref, rows(*me), local_sem)
        mine.start()
        first = [copy(0, me, sibling, src=x_ref)]
        first += [copy(1 + j, me, (*chip, c), src=x_ref) for j, chip in enumerate(chips)]
        for cp in first:
            cp.start()
        passed = [copy(4 + j, (*chip, c), sibling) for j, chip in enumerate(chips)]
        for j, chip in enumerate(chips):
            copy(1 + j, (*chip, c), me).wait_recv()
            passed[j].start()
        copy(0, sibling, me).wait_recv()
        for j, chip in enumerate(chips):
            copy(4 + j, (*chip, 1 - c), me).wait_recv()
        for cp in first + passed:
            cp.wait_send()
        mine.wait()

    return pl.pallas_call(
        body, name="gather8",
        out_shape=jax.ShapeDtypeStruct((8, m, n), block.dtype),
        in_specs=[pl.BlockSpec(memory_space=pltpu.VMEM)],
        out_specs=pl.BlockSpec(memory_space=pltpu.VMEM),
        scratch_shapes=[pltpu.SemaphoreType.DMA((7,)), pltpu.SemaphoreType.DMA((7,)), pltpu.SemaphoreType.DMA],
        compiler_params=pltpu.CompilerParams(vmem_limit_bytes=VMEM_LIMIT),
    )(block)


def gather_weights(shards):
    n = len(shards)

    def copy(ins, outs, send, recv, base, a, k, chip_idx, half, to, src=None):
        dst = outs[a].at[chip_idx, half]
        return pltpu.make_async_remote_copy(
            src_ref=dst if src is None else src, dst_ref=dst, send_sem=send.at[base + 6 * a + k],
            recv_sem=recv.at[base + 6 * a + k], device_id=to, device_id_type=MESH)

    def first(ins, outs, send, recv, base):
        x, y, c, chips = _place()
        return [copy(ins, outs, send, recv, base, a, j, 2 * x + y, c, (*chip, c), src=ins[a].at[c])
                for a in range(n) for j, chip in enumerate(chips)]

    def start(ins, outs, send, recv, base):
        for cp in first(ins, outs, send, recv, base):
            cp.start()

    def finish(ins, outs, send, recv, base):
        x, y, c, chips = _place()
        sibling = (x, y, 1 - c)
        passed = []
        for a in range(n):
            for j, (cx, cy) in enumerate(chips):
                copy(ins, outs, send, recv, base, a, j, 2 * cx + cy, c, (cx, cy, c)).wait_recv()
                fw = copy(ins, outs, send, recv, base, a, 3 + j, 2 * cx + cy, c, sibling)
                fw.start()
                passed.append(fw)
        for a in range(n):
            for j, (cx, cy) in enumerate(chips):
                copy(ins, outs, send, recv, base, a, 3 + j, 2 * cx + cy, 1 - c, sibling).wait_recv()
        for cp in first(ins, outs, send, recv, base) + passed:
            cp.wait_send()

    return Carry(shards, [jax.ShapeDtypeStruct((4,) + s.shape, s.dtype) for s in shards], 6 * n, start, finish)


def exchange_halves(grads):
    n = len(grads)

    def copies(ins, outs, send, recv, base):
        x, y, c, _ = _place()
        return [pltpu.make_async_remote_copy(
            src_ref=ins[a].at[j, 1 - c], dst_ref=outs[a].at[j], send_sem=send.at[base + 4 * a + j],
            recv_sem=recv.at[base + 4 * a + j], device_id=(x, y, 1 - c), device_id_type=MESH)
            for a in range(n) for j in range(4)]

    def start(*args):
        for cp in copies(*args):
            cp.start()

    def finish(*args):
        for cp in copies(*args):
            cp.wait()

    return Carry(grads, [jax.ShapeDtypeStruct((4,) + g.shape[2:], g.dtype) for g in grads], 4 * n, start, finish)


def scatter_chips(parts):
    n = len(parts)

    def copies(ins, outs, send, recv, base):
        x, y, c, chips = _place()
        return [pltpu.make_async_remote_copy(
            src_ref=ins[a].at[2 * cx + cy], dst_ref=outs[a].at[j], send_sem=send.at[base + 3 * a + j],
            recv_sem=recv.at[base + 3 * a + j], device_id=(cx, cy, c), device_id_type=MESH)
            for a in range(n) for j, (cx, cy) in enumerate(chips)]

    def start(*args):
        for cp in copies(*args):
            cp.start()

    def finish(*args):
        for cp in copies(*args):
            cp.wait()

    return Carry(parts, [jax.ShapeDtypeStruct((3,) + p.shape[1:], p.dtype) for p in parts], 3 * n, start, finish)


def join_halves(bufs):
    n = len(bufs)

    def copy(outs, send, recv, base, a, half):
        x, y, c, _ = _place()
        return pltpu.make_async_remote_copy(
            src_ref=outs[a].at[c], dst_ref=outs[a].at[c if half is None else half], send_sem=send.at[base + a],
            recv_sem=recv.at[base + a], device_id=(x, y, 1 - c), device_id_type=MESH)

    def start(ins, outs, send, recv, base):
        for a in range(n):
            copy(outs, send, recv, base, a, None).start()

    def finish(ins, outs, send, recv, base):
        c = lax.axis_index("c")
        for a in range(n):
            copy(outs, send, recv, base, a, 1 - c).wait_recv()
        for a in range(n):
            copy(outs, send, recv, base, a, None).wait_send()

    return Carry(bufs, [jax.ShapeDtypeStruct(h.shape, h.dtype) for h in bufs], n, start, finish,
                 aliases={a: a for a in range(n)})


def run_comm(carry, *, name):
    k_in, k_out = len(carry.arrays), len(carry.out_shape)

    def body(*refs):
        ins, outs = refs[:k_in], refs[k_in:k_in + k_out]
        send, recv = refs[-2:]
        carry.start(ins, outs, send, recv, 0)
        carry.finish(ins, outs, send, recv, 0)

    return pl.pallas_call(
        body, name=name, out_shape=carry.out_shape, in_specs=[ANY] * k_in, out_specs=[ANY] * k_out,
        scratch_shapes=[pltpu.SemaphoreType.DMA((carry.n_sems,))] * 2, input_output_aliases=carry.aliases,
    )(*carry.arrays)


INPUTS = ['x'] + WEIGHTS + ['loss_target'] + ['m_' + n for n in WEIGHTS] + ['v_' + n for n in WEIGHTS]


def _round_up(n, m):
    return -(-n // m) * m


def _pack(arrs):
    flat = jnp.concatenate([a.reshape(-1) for a in arrs])
    n = _round_up(flat.shape[0], 512 * LANE)
    return jnp.pad(flat, (0, n - flat.shape[0])).reshape(-1, LANE)


def _unpack(block, shapes):
    flat = block.reshape(-1)
    out, o = [], 0
    for s in shapes:
        n = math.prod(s)
        out.append(flat[o:o + n].reshape(s))
        o += n
    return out


def _cols(g):
    return g.transpose(1, 0, 2).reshape(g.shape[1], -1)


def _uncols(w):
    return w.reshape(w.shape[0], 4, -1).transpose(1, 0, 2)


def _pad_cols(w, total):
    return jnp.pad(w, ((0, 0), (0, total - w.shape[1])))


def kernel(*args):
    a = dict(zip(INPUTS, args))
    x, tgt = a['x'][0], a['loss_target'][0]
    t, d = x.shape
    xi, yi, ci = lax.axis_index("x"), lax.axis_index("y"), lax.axis_index("c")
    chip = 2 * xi + yi
    dk, dv, hk, hv = _gla_dims(d)
    lw = d // 2
    di = 2 * d
    nh = di // SSD_HEAD_DIM
    gn_w = SSD_GROUPS * SSD_STATE
    conv_dim = di + 2 * gn_w
    rank = GLA_GATE_RANK
    wq = 2 * dk + 2 * dv
    gla_w = _round_up(wq + LANE, 2 * lw)
    ev_tot = gla_w + 2 * lw
    od_used = di + conv_dim + nh
    od_tot = _round_up(di + conv_dim + _round_up(nh, LANE), 512)
    glr_col, xcol, dtcol = wq // LANE, gla_w // lw, (di + conv_dim) // LANE
    assert ev_tot % 512 == 0 and nh <= LANE

    def halves(w):
        w = w.astype(BF16)
        return w.reshape((2, w.shape[0] // 2) + w.shape[1:])

    own = {'ev_w_in': halves(a['ev_w_in'][0]), 'ev_w_out': halves(a['ev_w_out'][0]),
           'od_w_in_a': halves(a['od_w_in'][0][:d // 2]), 'od_w_in_b': halves(a['od_w_in'][0][d // 2:]),
           'od_w_out': halves(a['od_w_out'][0])}
    for l in range(2):
        own[f'gate{l}'], own[f'up{l}'] = halves(a['ffn_w_gate'][l]), halves(a['ffn_w_up'][l])
        own[f'down{l}'] = halves(a['ffn_w_down'][l])

    def gather(*units):
        return gather_weights([own[u] for u in units])

    def filled(unit, g):
        g = lax.dynamic_update_index_in_dim(g, own[unit], chip, 0)
        return g.reshape((4, 2 * g.shape[2]) + g.shape[3:])

    g_ev_in, g_ev_out = run_comm(gather('ev_w_in', 'ev_w_out'), name="gather_ev")
    w_ev_in = _cols(filled('ev_w_in', g_ev_in))
    cuts = [dk, 2 * dk, 2 * dk + dv, wq, wq + rank, wq + rank + lw]
    sq, sk, sv, sg, sglr, sxb, sgb = jnp.split(w_ev_in, cuts, axis=1)
    w_ev_in_p = jnp.concatenate([_pad_cols(jnp.concatenate([sq, sk, sv, sg, sglr], axis=1), gla_w), sxb, sgb], axis=1)
    w_ev_out = filled('ev_w_out', g_ev_out).reshape(-1, d)
    w_gate, w_up, w_down = [None, None], [None, None], [None, None]

    sh_names = list(SMALL_SHARDED)
    sh_shapes = [a[n].shape for n in sh_names]
    g8 = gather8(_pack([a[n] for n in sh_names]))
    per_chip = [_unpack(g8[2 * j], sh_shapes) for j in range(4)]
    full = {n: jnp.concatenate([per_chip[j][i] for j in range(4)], axis=SMALL_SHARDED[n])
            for i, n in enumerate(sh_names)}

    wg_p = jnp.zeros((LANE, dk), F32).at[:rank].set(full['ev_gla_w_gate'][0])
    bg, wn = a['ev_gla_b_gate'], a['ev_gla_w_onorm']
    lru_p = [full['ev_lru_conv_w'][0], a['ev_lru_conv_b'], a['ev_lru_w_a'][0], a['ev_lru_b_a'], a['ev_lru_w_i'][0],
             a['ev_lru_b_i'], a['ev_lru_lam']]
    od_cw, od_cb, od_gn = full['od_conv_w'][0], full['od_conv_b'], full['od_gnorm']
    heads = jnp.arange(LANE)[:, None]
    e_mat = ((jnp.arange(di)[None, :] // SSD_HEAD_DIM == heads) & (heads < nh)).astype(BF16)
    row8 = lambda p: jnp.zeros((SUBLANE, LANE), F32).at[0, :nh].set(p[0])
    dt_bias_p = jnp.zeros((1, LANE), F32).at[0, :nh].set(a['od_dt_bias'][0])
    alog_e = head_expand(row8(a['od_a_log']), e_mat, name="expand_a_log")
    dskip_e = head_expand(row8(a['od_d_skip']), e_mat, name="expand_d_skip")

    h0 = rms_fwd(x, a['ev_norm'], name="rms_ev")
    proj, (g,) = matmul(h0, w_ev_in_p, name="ev_in", carry=gather('gate0'))
    w_gate[0] = filled('gate0', g)
    (o_gla, sp_gla), (g,) = gla_fwd(proj, glr_col, wg_p, bg, wn, dv, carry=gather('up0'))
    w_up[0] = filled('up0', g)
    o_lru, hin = lru_fwd(proj, xcol, lw, *lru_p)
    x1 = matmul(o_gla, w_ev_out[:dv], add=x, name="ev_out_a")
    x1 = matmul(o_lru, w_ev_out[dv:], add=x1, name="ev_out_b")

    h1 = rms_fwd(x1, a['ffn_norm'][0:1], name="rms_ffn0")
    gate0, (g,) = matmul(h1, w_gate[0], name="ffn0_gate", carry=gather('down0'))
    w_down[0] = filled('down0', g).reshape(-1, d)
    up0, (g_a,) = matmul(h1, w_up[0], name="ffn0_up", carry=gather('od_w_in_a'))
    act0 = swiglu_fwd(gate0, up0, name="ffn0_act")
    x2, (g_b,) = matmul(act0, w_down[0], add=x1, name="ffn0_down", carry=gather('od_w_in_b'))
    w_od_in = jnp.concatenate([filled('od_w_in_a', g_a), filled('od_w_in_b', g_b)], axis=1)
    w_od_in_p = _pad_cols(_cols(w_od_in), od_tot)

    h2 = rms_fwd(x2, full['od_norm'], name="rms_od")
    proj2, (g, g1) = matmul(h2, w_od_in_p, name="od_in", carry=gather('od_w_out', 'gate1'))
    w_od_out, w_gate[1] = filled('od_w_out', g).reshape(-1, d), filled('gate1', g1)
    xs, (g,) = conv_silu_fwd(proj2, di, conv_dim, od_cw, od_cb, carry=gather('up1'))
    w_up[1] = filled('up1', g)
    dte = dt_fwd(proj2, dtcol, dt_bias_p, e_mat)
    (y_ssd, sp_ssd), (g,) = ssd_fwd(xs, proj2, dte, alog_e, dskip_e, od_gn, carry=gather('down1'))
    w_down[1] = filled('down1', g).reshape(-1, d)
    x3 = matmul(y_ssd, w_od_out, add=x2, name="od_out")
    h3 = rms_fwd(x3, a['ffn_norm'][1:2], name="rms_ffn1")
    gate1 = matmul(h3, w_gate[1], name="ffn1_gate")
    up1 = matmul(h3, w_up[1], name="ffn1_up")
    act1 = swiglu_fwd(gate1, up1, name="ffn1_act")
    x4 = matmul(act1, w_down[1], add=x3, name="ffn1_down")
    loss_p, dx4, d_final = loss_head(x4, a['final_norm'][None], tgt, name="loss_head")

    grads, from_sib, part, from_chips = {}, {}, {}, {}

    def rows4(dw):
        return dw.reshape((4, 2, dw.shape[0] // 8) + dw.shape[1:])

    def cols4(dw):
        return dw.reshape((4, 2, dw.shape[1] // 2) + dw.shape[2:])

    def exchange(*units):
        return exchange_halves([grads[u] for u in units])

    def scatter(*units):
        return scatter_chips([part[u] for u in units])

    def sum_chip(u):
        part[u] = chip_sum(grads[u], from_sib[u], ci, name=f"chip_sum_{u}")

    def ffn_bwd(dxo, xin, h, gate, up, act, l, first_carry, first_units):
        dn, gt, up_ = f'down{l}', f'gate{l}', f'up{l}'
        dact = matmul(dxo, w_down[l], tb=True, name=f"ffn{l}_d_act", carry=first_carry)
        dact, got = dact if first_units else (dact, ())
        for u, r in zip(first_units, got):
            from_sib[u] = r
            sum_chip(u)
        grads[dn] = rows4(matmul(act, dxo, ta=True, out_dtype=BF16, name=f"ffn{l}_dw_down"))
        dg, du = swiglu_bwd(gate, up, dact, name=f"ffn{l}_act_bwd")
        dh, (from_sib[dn],) = matmul(dg, w_gate[l], tb=True, name=f"ffn{l}_dh_gate", carry=exchange(dn))
        sum_chip(dn)
        dh = matmul(du, w_up[l], tb=True, add=dh, name=f"ffn{l}_dh_up")
        d_gate, (from_chips[dn],) = matmul(h, dg, ta=True, out_dtype=BF16, out_shards=True, name=f"ffn{l}_dw_gate",
                                           carry=scatter(dn))
        grads[gt] = cols4(d_gate)
        d_up, (from_sib[gt],) = matmul(h, du, ta=True, out_dtype=BF16, out_shards=True, name=f"ffn{l}_dw_up",
                                       carry=exchange(gt))
        grads[up_] = cols4(d_up)
        dxi, d_norm = rms_bwd(xin, a['ffn_norm'][l:l + 1], dh, dxo, name=f"rms_ffn{l}_bwd")
        return dxi, d_norm

    dx3, d_fn1 = ffn_bwd(dx4, x3, h3, gate1, up1, act1, 1, None, ())
    dy, (from_sib['up1'],) = matmul(dx3, w_od_out, tb=True, name="od_out_dy", carry=exchange('up1'))
    sum_chip('gate1')
    sum_chip('up1')
    grads['od_w_out'] = rows4(matmul(y_ssd, dx3, ta=True, out_dtype=BF16, name="od_out_dw"))
    (dxs, dproj2, ddte, dal, dds, dgn), (from_chips['gate1'], from_chips['up1'], from_sib['od_w_out']) = ssd_bwd(
        xs, proj2, dte, alog_e, dskip_e, od_gn, sp_ssd, dy, proj2.shape,
        carry=merge_carries(scatter('gate1', 'up1'), exchange('od_w_out')))
    sum_chip('od_w_out')
    (dproj2, d_od_cw, d_od_cb), (from_chips['od_w_out'],) = conv_silu_bwd(
        proj2, di, conv_dim, od_cw, od_cb, dxs, dproj2, carry=scatter('od_w_out'))
    dproj2, d_dt_bias = dt_bwd(proj2, dtcol, dt_bias_p, e_mat, ddte, dproj2)
    dh2 = matmul(dproj2, w_od_in_p, tb=True, name="od_in_dh")
    d_od_in = matmul(h2, dproj2, ta=True, out_dtype=BF16, name="od_in_dw")[:, :od_used]
    grads['od_w_in'] = cols4(_uncols(d_od_in))
    dx2, d_od_norm = rms_bwd(x2, full['od_norm'], dh2, dx3, name="rms_od_bwd")
    to8 = lambda acc: jnp.zeros((SUBLANE, di), F32).at[0].set(acc.reshape(-1))
    d_a_log = head_expand(to8(dal), e_mat, transpose=True, name="reduce_a_log")[0:1, :nh]
    d_d_skip = head_expand(to8(dds), e_mat, transpose=True, name="reduce_d_skip")[0:1, :nh]

    dx1, d_fn0 = ffn_bwd(dx2, x1, h1, gate0, up0, act0, 0, exchange('od_w_in'), ('od_w_in',))
    dmix, (from_sib['up0'],) = matmul(dx1, w_ev_out, tb=True, name="ev_out_dmix", carry=exchange('up0'))
    sum_chip('gate0')
    sum_chip('up0')
    grads['ev_w_out'] = rows4(jnp.concatenate([matmul(o_gla, dx1, ta=True, out_dtype=BF16, name="ev_out_dw_a"),
                                               matmul(o_lru, dx1, ta=True, out_dtype=BF16, name="ev_out_dw_b")], axis=0))
    (dproj, d_wg, d_bg, d_wn), (from_chips['od_w_in'], from_sib['ev_w_out']) = gla_bwd(
        proj, glr_col, wg_p, bg, wn, sp_gla, dmix, dv, gla_w,
        carry=merge_carries(scatter('od_w_in'), exchange('ev_w_out')))
    sum_chip('ev_w_out')
    (dproj, *d_lru), (from_chips['ev_w_out'],) = lru_bwd(proj, xcol, lw, *lru_p, hin, dmix, 1, dproj,
                                                         carry=scatter('ev_w_out'))
    dh0, (from_chips['gate0'],) = matmul(dproj, w_ev_in_p, tb=True, name="ev_in_dh", carry=scatter('gate0'))
    d_ev_in_p, (from_chips['up0'],) = matmul(h0, dproj, ta=True, out_dtype=BF16, name="ev_in_dw", carry=scatter('up0'))
    d_ev_in = jnp.concatenate([d_ev_in_p[:, :wq + rank], d_ev_in_p[:, gla_w:]], axis=1)
    grads['ev_w_in'] = cols4(_uncols(d_ev_in))
    dx0, d_ev_norm = rms_bwd(x, a['ev_norm'], dh0, dx1, name="rms_ev_bwd")
    (from_sib['ev_w_in'],) = run_comm(exchange('ev_w_in'), name="exchange_ev_in")
    sum_chip('ev_w_in')
    (from_chips['ev_w_in'],) = run_comm(scatter('ev_w_in'), name="scatter_ev_in")

    units = list(grads)
    half = [mesh_sum(part[u], from_chips[u], chip, ci, name=f"mesh_sum_{u}") for u in units]
    done = dict(zip(units, run_comm(join_halves(half), name="join_halves")))
    grad = {n: done[n].reshape(a[n].shape) for n in ('ev_w_in', 'ev_w_out', 'od_w_in', 'od_w_out')}
    for n, u in (('ffn_w_gate', 'gate'), ('ffn_w_up', 'up'), ('ffn_w_down', 'down')):
        grad[n] = jnp.stack([done[f'{u}{l}'].reshape(a[n].shape[1:]) for l in range(2)])

    small_g = {
        'ev_norm': d_ev_norm, 'ev_gla_w_gate': d_wg[:rank][None], 'ev_gla_b_gate': d_bg, 'ev_gla_w_onorm': d_wn,
        'ev_lru_conv_w': d_lru[0][None], 'ev_lru_conv_b': d_lru[1], 'ev_lru_w_a': d_lru[2][None],
        'ev_lru_b_a': d_lru[3], 'ev_lru_w_i': d_lru[4][None], 'ev_lru_b_i': d_lru[5], 'ev_lru_lam': d_lru[6],
        'od_norm': d_od_norm, 'od_conv_w': d_od_cw[None], 'od_conv_b': d_od_cb, 'od_dt_bias': d_dt_bias[:, :nh],
        'od_a_log': d_a_log, 'od_d_skip': d_d_skip, 'od_gnorm': dgn.reshape(1, di),
        'ffn_norm': jnp.concatenate([d_fn0, d_fn1], axis=0), 'final_norm': d_final[0],
    }
    full_shapes = [small_g[n].shape for n in SMALL]
    summed = sum_leading(gather8(_pack([small_g[n] for n in SMALL])), name="sum_devices")
    for n, g in zip(SMALL, _unpack(summed, full_shapes)):
        if n in SMALL_SHARDED:
            ax = SMALL_SHARDED[n]
            sz = a[n].shape[ax]
            g = lax.dynamic_slice_in_dim(g, chip * sz, sz, axis=ax)
        grad[n] = g

    delta, new_m, new_v = {}, {}, {}
    for n in BIG:
        delta[n], new_m[n], new_v[n] = adamw(a[n], grad[n], a['m_' + n], a['v_' + n], name=f"adamw_{n}")
    shapes = [a[n].shape for n in SMALL]
    packed = [_pack([src[n] if pre is None else a[pre + n] for n in SMALL])
              for src, pre in ((a, None), (grad, None), (None, 'm_'), (None, 'v_'))]
    for outd, blk in zip((delta, new_m, new_v), adamw(*packed, name="adamw_small")):
        outd.update(zip(SMALL, _unpack(blk, shapes)))

    loss = lax.psum(loss_p[0, 0], ("x", "y", "c"))
    return (loss, dx0[None], *[grad[n] for n in WEIGHTS], *[delta[n] for n in WEIGHTS],
            *[new_m[n] for n in WEIGHTS], *[new_v[n] for n in WEIGHTS])
```

```python
import functools
import math

import jax
import jax.numpy as jnp
from jax import lax
from jax.experimental import pallas as pl
from jax.experimental.pallas import tpu as pltpu

F32 = jnp.float32
BF16 = jnp.bfloat16
MXU_DTYPE = jnp.bfloat16

NORM_EPS = 1e-6
CONV_WIDTH = 4
GLA_HEADS = 4
GLA_GATE_RANK = 16
GLA_GATE_NORM = 16.0
CHUNK = 64
LRU_BLOCK = 128
LRU_C = 8.0
SSD_HEAD_DIM = 64
SSD_GROUPS = 8
SSD_STATE = 128
ADAM_LR, ADAM_B1, ADAM_B2, ADAM_EPS, ADAM_WD, ADAM_STEP = 0.001, 0.9, 0.999, 1e-08, 0.01, 10

LANE = 128
SUBLANE = 8
VMEM_LIMIT = 48 * 1024 * 1024
MAX_TK = 2816
MESH = pl.DeviceIdType.MESH

WEIGHTS = ['ev_norm', 'ev_w_in', 'ev_gla_w_gate', 'ev_gla_b_gate', 'ev_gla_w_onorm', 'ev_lru_conv_w', 'ev_lru_conv_b',
           'ev_lru_w_a', 'ev_lru_b_a', 'ev_lru_w_i', 'ev_lru_b_i', 'ev_lru_lam', 'ev_w_out', 'od_norm', 'od_w_in',
           'od_conv_w', 'od_conv_b', 'od_dt_bias', 'od_a_log', 'od_d_skip', 'od_gnorm', 'od_w_out', 'ffn_norm',
           'ffn_w_gate', 'ffn_w_up', 'ffn_w_down', 'final_norm']
BIG = ['ev_w_in', 'ev_w_out', 'od_w_in', 'od_w_out', 'ffn_w_gate', 'ffn_w_up', 'ffn_w_down']
SMALL_SHARDED = {'ev_gla_w_gate': 2, 'ev_lru_conv_w': 2, 'od_norm': 1, 'od_conv_w': 2, 'od_conv_b': 1, 'od_gnorm': 1}
SMALL = [n for n in WEIGHTS if n not in BIG]


def _cparams(sem=None, **kw):
    return pltpu.CompilerParams(dimension_semantics=sem, vmem_limit_bytes=VMEM_LIMIT, **kw)


def _full(shape):
    n = len(shape)
    return pl.BlockSpec(shape, lambda *_: (0,) * n)


ANY = pl.BlockSpec(memory_space=pl.ANY)


class Carry:
    def __init__(self, arrays, out_shape, n_sems, start, finish, aliases=None):
        self.arrays, self.out_shape, self.n_sems = list(arrays), list(out_shape), n_sems
        self.start, self.finish, self.aliases = start, finish, dict(aliases or {})


def merge_carries(*cs):
    cs = [c for c in cs if c is not None]
    if not cs:
        return None
    arrays = [a for c in cs for a in c.arrays]
    out_shape = [s for c in cs for s in c.out_shape]
    offs, i0, o0, s0 = [], 0, 0, 0
    aliases = {}
    for c in cs:
        offs.append((i0, o0, s0))
        aliases.update({i0 + i: o0 + o for i, o in c.aliases.items()})
        i0, o0, s0 = i0 + len(c.arrays), o0 + len(c.out_shape), s0 + c.n_sems

    def both(which):
        def run(ins, outs, send, recv, base):
            for c, (i, o, s) in zip(cs, offs):
                getattr(c, which)(ins[i:i + len(c.arrays)], outs[o:o + len(c.out_shape)], send, recv, base + s)
        return run

    return Carry(arrays, out_shape, s0, both("start"), both("finish"), aliases)


def _pcall(body, *, name, grid, in_specs, out_specs, out_shape, scratch_shapes=(), compiler_params, carry=None,
           input_output_aliases=None):
    aliases = dict(input_output_aliases or {})
    if carry is None:
        return pl.pallas_call(body, name=name, grid=grid, in_specs=in_specs, out_specs=out_specs, out_shape=out_shape,
                              scratch_shapes=list(scratch_shapes), compiler_params=compiler_params,
                              input_output_aliases=aliases)
    single = not isinstance(out_specs, (list, tuple))
    specs_o = [out_specs] if single else list(out_specs)
    shapes_o = [out_shape] if single else list(out_shape)
    n_in, n_out, k_in, k_out, n_scr = len(in_specs), len(specs_o), len(carry.arrays), len(carry.out_shape), len(scratch_shapes)

    def wrapped(*refs):
        ins, cins = refs[:n_in], refs[n_in:n_in + k_in]
        o0 = n_in + k_in
        outs, couts = refs[o0:o0 + n_out], refs[o0 + n_out:o0 + n_out + k_out]
        scr = refs[o0 + n_out + k_out:o0 + n_out + k_out + n_scr]
        send, recv = refs[-2:]
        ids = [pl.program_id(ax) for ax in range(len(grid))]
        first = functools.reduce(jnp.logical_and, [i == 0 for i in ids])
        last = functools.reduce(jnp.logical_and, [i == g - 1 for i, g in zip(ids, grid)])

        @pl.when(first)
        def _():
            carry.start(cins, couts, send, recv, 0)

        body(*ins, *outs, *scr)

        @pl.when(last)
        def _():
            carry.finish(cins, couts, send, recv, 0)

    aliases.update({n_in + i: n_out + o for i, o in carry.aliases.items()})
    call = pl.pallas_call(
        wrapped, name=name, grid=grid, in_specs=list(in_specs) + [ANY] * k_in, out_specs=specs_o + [ANY] * k_out,
        out_shape=shapes_o + carry.out_shape,
        scratch_shapes=list(scratch_shapes) + [pltpu.SemaphoreType.DMA((carry.n_sems,))] * 2,
        compiler_params=_cparams(("arbitrary",) * len(grid)), input_output_aliases=aliases)

    def run(*args):
        res = call(*args, *carry.arrays)
        main = res[:n_out]
        return (main[0] if single else list(main)), list(res[n_out:])

    return run


def _pick(dim, cands):
    for c in cands:
        if dim % c == 0:
            return c
    return dim


def _dot(a, b, ca, cb):
    return lax.dot_general(a.astype(MXU_DTYPE), b.astype(MXU_DTYPE), (((ca,), (cb,)), ((), ())),
                           preferred_element_type=F32)


@jax.custom_vjp
def mm(a, b):
    return _dot(a, b, 1, 0)


def _mm_f(a, b):
    return mm(a, b), (a, b)


def _mm_b(res, g):
    a, b = res
    return mm_nt(g, b).astype(a.dtype), mm_tn(a, g).astype(b.dtype)


@jax.custom_vjp
def mm_nt(a, b):
    return _dot(a, b, 1, 1)


def _mm_nt_f(a, b):
    return mm_nt(a, b), (a, b)


def _mm_nt_b(res, g):
    a, b = res
    return mm(g, b).astype(a.dtype), mm_tn(g, a).astype(b.dtype)


@jax.custom_vjp
def mm_tn(a, b):
    return _dot(a, b, 0, 0)


def _mm_tn_f(a, b):
    return mm_tn(a, b), (a, b)


def _mm_tn_b(res, g):
    a, b = res
    return mm_nt(b, g).astype(a.dtype), mm(a, g).astype(b.dtype)


mm.defvjp(_mm_f, _mm_b)
mm_nt.defvjp(_mm_nt_f, _mm_nt_b)
mm_tn.defvjp(_mm_tn_f, _mm_tn_b)


def _split3(a):
    h = a.astype(BF16)
    r = a - h.astype(F32)
    m = r.astype(BF16)
    l = (r - m.astype(F32)).astype(BF16)
    return h, m, l


def _exact_dot(t, a, ca, cb):
    out = None
    for p in _split3(a):
        d = lax.dot_general(t, p, (((ca,), (cb,)), ((), ())), preferred_element_type=F32)
        out = d if out is None else out + d
    return out


@jax.custom_vjp
def sel_l(t, a):
    return _exact_dot(t, a, 1, 0)


def _sel_l_f(t, a):
    return sel_l(t, a), t


def _sel_l_b(t, g):
    return jnp.zeros_like(t), _exact_dot(t, g, 0, 0)


sel_l.defvjp(_sel_l_f, _sel_l_b)


@jax.custom_vjp
def sel_r(a, t):
    out = None
    for p in _split3(a):
        d = lax.dot_general(p, t, (((1,), (0,)), ((), ())), preferred_element_type=F32)
        out = d if out is None else out + d
    return out


def _sel_r_f(a, t):
    return sel_r(a, t), t


def _sel_r_b(t, g):
    out = None
    for p in _split3(g):
        d = lax.dot_general(p, t, (((1,), (1,)), ((), ())), preferred_element_type=F32)
        out = d if out is None else out + d
    return out, jnp.zeros_like(t)


sel_r.defvjp(_sel_r_f, _sel_r_b)


def _sigmoid(x):
    return 1.0 / (1.0 + jnp.exp(-x))


def _silu(x):
    return x * _sigmoid(x)


def _softplus(x):
    return jnp.maximum(x, 0.0) + jnp.log(1.0 + jnp.exp(-jnp.abs(x)))


def _log_sigmoid(x):
    return -_softplus(-x)


def _gelu_tanh(x):
    c = math.sqrt(2.0 / math.pi)
    return 0.5 * x * (1.0 + jnp.tanh(c * (x + 0.044715 * (x * x * x))))


def _rms(x, w):
    return x * lax.rsqrt(jnp.mean(x * x, axis=-1, keepdims=True) + NORM_EPS) * w


def _tri(n, dtype=BF16):
    r = lax.broadcasted_iota(jnp.int32, (n, n), 0)
    c = lax.broadcasted_iota(jnp.int32, (n, n), 1)
    return (c <= r).astype(dtype)


def matmul(a, b, *, ta=False, tb=False, add=None, out_dtype=F32, out_shards=False, carry=None, name,
           epi=None, epi_in=(), epi_out=(), tm_cap=1024):
    m, k = (a.shape[1], a.shape[0]) if ta else a.shape
    b_sh = b.ndim == 3
    if b_sh:
        s, br, bc = b.shape
        k2, n = (s * bc, br) if tb else (br, s * bc)
    else:
        k2, n = (b.shape[1], b.shape[0]) if tb else b.shape
    assert k == k2, (a.shape, b.shape, ta, tb)
    tm = _pick(m, tuple(c for c in (1024, 512, 256, 128) if c <= tm_cap))
    tn = _pick(n, (1024, 512, 256, 128))
    tk = k if k <= MAX_TK else max(c for c in range(LANE, MAX_TK + 1, LANE) if k % c == 0)
    if b_sh and tb:
        tk = bc
    elif b_sh:
        tn = bc
    if out_shards:
        tn = n // 4
    nk = k // tk
    n_add, n_x = int(add is not None), len(epi_in)
    out_dtypes = list(epi_out) if epi is not None else [out_dtype]
    n_o = len(out_dtypes)

    def body(*refs):
        a_ref, b_ref = refs[:2]
        x_refs = refs[2 + n_add:2 + n_add + n_x]
        o_refs = refs[2 + n_add + n_x:2 + n_add + n_x + n_o]
        acc = refs[-1]
        kk = pl.program_id(2)

        @pl.when(kk == 0)
        def _():
            acc[...] = jnp.zeros_like(acc)

        acc[...] += _dot(a_ref[...], b_ref[...], 0 if ta else 1, 1 if tb else 0)

        @pl.when(kk == nk - 1)
        def _():
            r = acc[...]
            if add is not None:
                r = r + refs[2][...].astype(F32)
            vals = (r,) if epi is None else epi(r, *[x[...] for x in x_refs])
            for o_ref, v in zip(o_refs, vals):
                o_ref[...] = v.astype(o_ref.dtype)

    a_spec = pl.BlockSpec((tk, tm), lambda i, j, kk: (kk, i)) if ta else pl.BlockSpec((tm, tk), lambda i, j, kk: (i, kk))
    if b_sh and tb:
        b_spec = pl.BlockSpec((None, tn, tk), lambda i, j, kk: (kk, j, 0))
    elif b_sh:
        b_spec = pl.BlockSpec((None, tk, tn), lambda i, j, kk: (j, kk, 0))
    elif tb:
        b_spec = pl.BlockSpec((tn, tk), lambda i, j, kk: (j, kk))
    else:
        b_spec = pl.BlockSpec((tk, tn), lambda i, j, kk: (kk, j))
    in_specs, args = [a_spec, b_spec], [a, b]
    tile = pl.BlockSpec((tm, tn), lambda i, j, kk: (i, j))
    for extra in ([add] if add is not None else []) + list(epi_in):
        in_specs.append(tile)
        args.append(extra)
    if out_shards:
        out_spec = pl.BlockSpec((None, tm, tn), lambda i, j, kk: (j, i, 0))
        out_shape = jax.ShapeDtypeStruct((4, m, tn), out_dtype)
    elif epi is not None:
        out_spec = [tile] * n_o
        out_shape = [jax.ShapeDtypeStruct((m, n), dt) for dt in out_dtypes]
    else:
        out_spec = tile
        out_shape = jax.ShapeDtypeStruct((m, n), out_dtype)
    return _pcall(
        body, name=name, grid=(m // tm, n // tn, nk), in_specs=in_specs, out_specs=out_spec, out_shape=out_shape,
        scratch_shapes=[pltpu.VMEM((tm, tn), F32)],
        compiler_params=_cparams(("parallel", "parallel", "arbitrary")), carry=carry,
    )(*args)


def rms_fwd(x, w, *, name):
    t, d = x.shape
    tb = _pick(t, (256, 128, 64))

    def body(x_ref, w_ref, o_ref):
        o_ref[...] = _rms(x_ref[...], w_ref[...]).astype(o_ref.dtype)

    return pl.pallas_call(
        body, name=name, grid=(t // tb,),
        in_specs=[pl.BlockSpec((tb, d), lambda i: (i, 0)), _full((1, d))],
        out_specs=pl.BlockSpec((tb, d), lambda i: (i, 0)),
        out_shape=jax.ShapeDtypeStruct((t, d), BF16),
        compiler_params=_cparams(("parallel",)),
    )(x, w)


def rms_bwd(x, w, dh, dres, *, name, carry=None):
    t, d = x.shape
    tb = _pick(t, (256, 128, 64))

    def body(x_ref, w_ref, dh_ref, dres_ref, dx_ref, dw_ref):
        @pl.when(pl.program_id(0) == 0)
        def _():
            dw_ref[...] = jnp.zeros_like(dw_ref)

        _, vjp = jax.vjp(_rms, x_ref[...], w_ref[...])
        dx, dw = vjp(dh_ref[...].astype(F32))
        dx_ref[...] = dx + dres_ref[...]
        dw_ref[...] += dw

    row = pl.BlockSpec((tb, d), lambda i: (i, 0))
    return _pcall(
        body, name=name, grid=(t // tb,), carry=carry,
        in_specs=[row, _full((1, d)), row, row],
        out_specs=[row, _full((1, d))],
        out_shape=[jax.ShapeDtypeStruct((t, d), F32), jax.ShapeDtypeStruct((1, d), F32)],
        compiler_params=_cparams(("arbitrary",)),
    )(x, w, dh, dres)


def _swi(g, u):
    return _silu(g) * u


def _swi_fwd_epi(u, g):
    return u, _swi(g, u)


def _swi_bwd_epi(d, g, u):
    return jax.vjp(_swi, g, u)[1](d)


def loss_head(x, w, target, *, name):
    t, d = x.shape
    tb = _pick(t, (256, 128, 64))

    def f(xv, wv, tv):
        y = _rms(xv, wv)
        e = y - tv
        return 0.5 * jnp.sum(jnp.mean(e * e, axis=-1, keepdims=True), axis=0, keepdims=True)

    def body(x_ref, w_ref, t_ref, l_ref, dx_ref, dw_ref):
        @pl.when(pl.program_id(0) == 0)
        def _():
            l_ref[...] = jnp.zeros_like(l_ref)
            dw_ref[...] = jnp.zeros_like(dw_ref)

        val, vjp = jax.vjp(lambda a, b: f(a, b, t_ref[...]), x_ref[...], w_ref[...])
        dx, dw = vjp(jnp.ones((1, 1), F32))
        l_ref[...] += jnp.broadcast_to(val, l_ref.shape)
        dx_ref[...] = dx
        dw_ref[...] += dw

    row = pl.BlockSpec((tb, d), lambda i: (i, 0))
    return pl.pallas_call(
        body, name=name, grid=(t // tb,),
        in_specs=[row, _full((1, d)), row],
        out_specs=[_full((SUBLANE, LANE)), row, _full((1, d))],
        out_shape=[jax.ShapeDtypeStruct((SUBLANE, LANE), F32), jax.ShapeDtypeStruct((t, d), F32),
                   jax.ShapeDtypeStruct((1, d), F32)],
        compiler_params=_cparams(("arbitrary",)),
    )(x, w, target)


def _gla_chunk(q, k, v, g, glr, st, wg, bg, wn, tri):
    L, hk = q.shape
    la = _log_sigmoid(mm(glr, wg) + bg) / GLA_GATE_NORM
    bcum = sel_l(tri, la)
    b_last = jnp.sum(la, axis=0, keepdims=True)
    rows = lax.broadcasted_iota(jnp.int32, (L, 1), 0)
    b_mid = jnp.sum(jnp.where(rows <= L // 2, la, 0.0), axis=0, keepdims=True)
    qs = q * (hk ** -0.5)
    q_in = qs * jnp.exp(bcum - b_mid)
    k_in = k * jnp.exp(b_mid - bcum)
    scores = mm_nt(q_in, k_in) * tri.astype(F32)
    o_intra = mm(scores, v)
    k_st = k * jnp.exp(b_last - bcum)
    d_st = mm_tn(v, k_st)
    o_inter = mm_nt(qs * jnp.exp(bcum), st)
    st_new = jnp.exp(b_last) * st + d_st
    o = _rms(o_intra + o_inter, wn) * _silu(g)
    return o, st_new


def _gla_dims(d):
    dv = d // 2
    dk = dv // 2
    return dk, dv, dk // GLA_HEADS, dv // GLA_HEADS


def gla_fwd(proj, glr_col, wg, bg, wn, dv, carry=None):
    t = proj.shape[0]
    dk, dv, hk, hv = _gla_dims(2 * dv)
    L, H = CHUNK, GLA_HEADS
    nc = t // L
    wq = 2 * dk + 2 * dv

    def body(p_ref, glr_ref, wg_ref, bg_ref, wn_ref, o_ref, sp_ref, st):
        @pl.when(pl.program_id(0) == 0)
        def _():
            st[...] = jnp.zeros_like(st)

        tri = _tri(L)
        glr = glr_ref[...]
        for h in range(H):
            q = p_ref[:, h * hk:(h + 1) * hk]
            k = p_ref[:, dk + h * hk:dk + (h + 1) * hk]
            v = p_ref[:, 2 * dk + h * hv:2 * dk + (h + 1) * hv]
            g = p_ref[:, 2 * dk + dv + h * hv:2 * dk + dv + (h + 1) * hv]
            s_prev = st[h]
            sp_ref[0, h] = s_prev
            o, s_new = _gla_chunk(q, k, v, g, glr, s_prev, wg_ref[:, h * hk:(h + 1) * hk],
                                  bg_ref[:, h * hk:(h + 1) * hk], wn_ref[...], tri)
            o_ref[:, h * hv:(h + 1) * hv] = o.astype(o_ref.dtype)
            st[h] = s_new

    return _pcall(
        body, carry=carry, name="gla_fwd", grid=(nc,),
        in_specs=[pl.BlockSpec((L, wq), lambda c: (c, 0)), pl.BlockSpec((L, LANE), lambda c: (c, glr_col)),
                  _full(wg.shape), _full(bg.shape), _full(wn.shape)],
        out_specs=[pl.BlockSpec((L, dv), lambda c: (c, 0)), pl.BlockSpec((1, H, hv, hk), lambda c: (c, 0, 0, 0))],
        out_shape=[jax.ShapeDtypeStruct((t, dv), BF16), jax.ShapeDtypeStruct((nc, H, hv, hk), F32)],
        scratch_shapes=[pltpu.VMEM((H, hv, hk), F32)],
        compiler_params=_cparams(("arbitrary",)),
    )(proj, proj, wg, bg, wn)


def gla_bwd(proj, glr_col, wg, bg, wn, sprev, do, dv, gla_w, carry=None):
    t = proj.shape[0]
    dk, _, hk, hv = _gla_dims(2 * dv)
    L, H = CHUNK, GLA_HEADS
    nc = t // L
    wq = 2 * dk + 2 * dv

    def body(p_ref, glr_ref, wg_ref, bg_ref, wn_ref, sp_ref, do_ref, dp_ref, dwg_ref, dbg_ref, dwn_ref, dst):
        @pl.when(pl.program_id(0) == 0)
        def _():
            dst[...] = jnp.zeros_like(dst)
            dwg_ref[...] = jnp.zeros_like(dwg_ref)
            dbg_ref[...] = jnp.zeros_like(dbg_ref)
            dwn_ref[...] = jnp.zeros_like(dwn_ref)

        tri = _tri(L)
        glr = glr_ref[...]
        dglr = jnp.zeros_like(glr)
        for h in range(H):
            ks = slice(h * hk, (h + 1) * hk)
            q = p_ref[:, ks]
            k = p_ref[:, dk + h * hk:dk + (h + 1) * hk]
            v = p_ref[:, 2 * dk + h * hv:2 * dk + (h + 1) * hv]
            g = p_ref[:, 2 * dk + dv + h * hv:2 * dk + dv + (h + 1) * hv]
            f = functools.partial(_gla_chunk, tri=tri)
            _, vjp = jax.vjp(f, q, k, v, g, glr, sp_ref[0, h], wg_ref[:, ks], bg_ref[:, ks], wn_ref[...])
            dq, dkk, dvv, dg, dgl, ds, dwg, dbg, dwn = vjp((do_ref[:, h * hv:(h + 1) * hv], dst[h]))
            dp_ref[:, ks] = dq.astype(dp_ref.dtype)
            dp_ref[:, dk + h * hk:dk + (h + 1) * hk] = dkk.astype(dp_ref.dtype)
            dp_ref[:, 2 * dk + h * hv:2 * dk + (h + 1) * hv] = dvv.astype(dp_ref.dtype)
            dp_ref[:, 2 * dk + dv + h * hv:2 * dk + dv + (h + 1) * hv] = dg.astype(dp_ref.dtype)
            dglr = dglr + dgl
            dst[h] = ds
            dwg_ref[:, ks] += dwg
            dbg_ref[:, ks] += dbg
            dwn_ref[...] += dwn
        dp_ref[:, wq:wq + LANE] = dglr.astype(dp_ref.dtype)
        if gla_w > wq + LANE:
            dp_ref[:, wq + LANE:] = jnp.zeros((L, gla_w - wq - LANE), dp_ref.dtype)

    rev = lambda c: nc - 1 - c
    return _pcall(
        body, carry=carry, name="gla_bwd", grid=(nc,),
        in_specs=[pl.BlockSpec((L, wq), lambda c: (rev(c), 0)), pl.BlockSpec((L, LANE), lambda c: (rev(c), glr_col)),
                  _full(wg.shape), _full(bg.shape), _full(wn.shape),
                  pl.BlockSpec((1, H, hv, hk), lambda c: (rev(c), 0, 0, 0)),
                  pl.BlockSpec((L, dv), lambda c: (rev(c), 0))],
        out_specs=[pl.BlockSpec((L, gla_w), lambda c: (rev(c), 0)),
                   _full(wg.shape), _full(bg.shape), _full(wn.shape)],
        out_shape=[jax.ShapeDtypeStruct(proj.shape, BF16),
                   jax.ShapeDtypeStruct(wg.shape, F32), jax.ShapeDtypeStruct(bg.shape, F32),
                   jax.ShapeDtypeStruct(wn.shape, F32)],
        scratch_shapes=[pltpu.VMEM((H, hv, hk), F32)],
        compiler_params=_cparams(("arbitrary",)),
    )(proj, proj, wg, bg, wn, sprev, do)


def _shift_down(x, tail, s):
    if s == 0:
        return x
    r = pltpu.roll(x, s, 0)
    rows = lax.broadcasted_iota(jnp.int32, tail.shape, 0)
    top = jnp.where(rows < s, pltpu.roll(tail, s, 0), r[:SUBLANE])
    return jnp.concatenate([top, r[SUBLANE:]], axis=0)


def _shift_up(x, head, s):
    if s == 0:
        return x
    n = x.shape[0]
    r = pltpu.roll(x, n - s, 0)
    rows = lax.broadcasted_iota(jnp.int32, head.shape, 0)
    bottom = jnp.where(rows >= SUBLANE - s, pltpu.roll(head, SUBLANE - s, 0), r[n - SUBLANE:])
    return jnp.concatenate([r[:n - SUBLANE], bottom], axis=0)


def _conv(x, prev, w, b):
    y = b
    for k in range(CONV_WIDTH):
        y = y + w[k:k + 1, :] * _shift_down(x, prev, CONV_WIDTH - 1 - k)
    return y


def _conv_bwd(dy, nxt, x, prev, w):
    dx = None
    dws = []
    for k in range(CONV_WIDTH):
        s = CONV_WIDTH - 1 - k
        term = w[k:k + 1, :] * _shift_up(dy, nxt, s)
        dx = term if dx is None else dx + term
        dws.append(jnp.sum(dy * _shift_down(x, prev, s), axis=0, keepdims=True))
    return dx, jnp.concatenate(dws, axis=0), jnp.sum(dy, axis=0, keepdims=True)


def _scan_fwd(a, u):
    n = a.shape[0]
    rows = lax.broadcasted_iota(jnp.int32, a.shape, 0)
    s = 1
    while s < n:
        a_sh = jnp.where(rows < s, 1.0, pltpu.roll(a, s, 0))
        u_sh = jnp.where(rows < s, 0.0, pltpu.roll(u, s, 0))
        u = a * u_sh + u
        a = a * a_sh
        s *= 2
    return a, u


def _scan_rev(c, d):
    n = c.shape[0]
    rows = lax.broadcasted_iota(jnp.int32, c.shape, 0)
    s = 1
    while s < n:
        c_sh = jnp.where(rows >= n - s, 0.0, pltpu.roll(c, n - s, 0))
        d_sh = jnp.where(rows >= n - s, 0.0, pltpu.roll(d, n - s, 0))
        d = d + c * d_sh
        c = c * c_sh
        s *= 2
    return d


def _expm1(x):
    small = x * (1.0 + x * (0.5 + x * (1.0 / 6.0 + x * (1.0 / 24.0))))
    return jnp.where(jnp.abs(x) < 1e-2, small, jnp.exp(x) - 1.0)


def _lru_gates(xc, pa, pi, lam):
    r = _sigmoid(pa)
    i = _sigmoid(pi)
    log_a = LRU_C * r * _log_sigmoid(lam)
    a = jnp.exp(log_a)
    u = jnp.sqrt(-_expm1(2.0 * log_a)) * (i * xc)
    return a, u


def _lru_out(h, gate):
    return h * _gelu_tanh(gate)


def _blockdiag(xc, w_ref, b):
    nb = w_ref.shape[0]
    outs = [mm(xc[:, n * LRU_BLOCK:(n + 1) * LRU_BLOCK], w_ref[n]) for n in range(nb)]
    return jnp.concatenate(outs, axis=1) + b


def lru_fwd(proj, xcol, lw, cw, cb, wa, ba, wi, bi, lam):
    t = proj.shape[0]
    tb = _pick(t, (256, 128, 64))
    nb = t // tb

    def body(x_ref, xp_ref, g_ref, cw_ref, cb_ref, wa_ref, ba_ref, wi_ref, bi_ref, lam_ref, o_ref, hin_ref, hc):
        i = pl.program_id(0)

        @pl.when(i == 0)
        def _():
            hc[...] = jnp.zeros_like(hc)

        prev = jnp.where(i == 0, 0.0, xp_ref[...])
        xc = _conv(x_ref[...], prev, cw_ref[...], cb_ref[...])
        a, u = _lru_gates(xc, _blockdiag(xc, wa_ref, ba_ref[...]), _blockdiag(xc, wi_ref, bi_ref[...]), lam_ref[...])
        acum, h0 = _scan_fwd(a, u)
        h = h0 + acum * hc[...]
        hin_ref[0] = hc[...]
        hc[...] = h[tb - 1:tb, :]
        o_ref[...] = _lru_out(h, g_ref[...]).astype(o_ref.dtype)

    row = lambda col: pl.BlockSpec((tb, lw), lambda i: (i, col))
    return pl.pallas_call(
        body, name="lru_fwd", grid=(nb,),
        in_specs=[row(xcol), pl.BlockSpec((SUBLANE, lw), lambda i: (jnp.maximum(i * (tb // SUBLANE) - 1, 0), xcol)),
                  row(xcol + 1),
                  _full(cw.shape), _full(cb.shape), _full(wa.shape), _full(ba.shape), _full(wi.shape), _full(bi.shape),
                  _full(lam.shape)],
        out_specs=[pl.BlockSpec((tb, lw), lambda i: (i, 0)), pl.BlockSpec((1, 1, lw), lambda i: (i, 0, 0))],
        out_shape=[jax.ShapeDtypeStruct((t, lw), BF16), jax.ShapeDtypeStruct((nb, 1, lw), F32)],
        scratch_shapes=[pltpu.VMEM((1, lw), F32)],
        compiler_params=_cparams(("arbitrary",)),
    )(proj, proj, proj, cw, cb, wa, ba, wi, bi, lam)


def lru_bwd(proj, xcol, lw, cw, cb, wa, ba, wi, bi, lam, hin, dmix, docol, dproj, carry=None):
    t = proj.shape[0]
    tb = _pick(t, (256, 128, 64))
    nb = t // tb
    nblk = wa.shape[0]
    assert xcol % 2 == 0

    def body(x_ref, xp_ref, g_ref, cw_ref, cb_ref, wa_ref, ba_ref, wi_ref, bi_ref, lam_ref, hin_ref, do_ref, _,
             dxg_ref, dcw_ref, dcb_ref, dwa_ref, dba_ref, dwi_ref, dbi_ref, dlam_ref, gc, dxcn):
        pid = pl.program_id(0)
        i = nb - 1 - pid

        @pl.when(pid == 0)
        def _():
            gc[...] = jnp.zeros_like(gc)
            dxcn[...] = jnp.zeros_like(dxcn)
            for r in (dcw_ref, dcb_ref, dwa_ref, dba_ref, dwi_ref, dbi_ref, dlam_ref):
                r[...] = jnp.zeros_like(r)

        x = x_ref[...]
        prev = jnp.where(i == 0, 0.0, xp_ref[...])
        cw_v = cw_ref[...]
        xc = _conv(x, prev, cw_v, cb_ref[...])
        pa = _blockdiag(xc, wa_ref, ba_ref[...])
        pi = _blockdiag(xc, wi_ref, bi_ref[...])
        (a, u), vjp_g = jax.vjp(_lru_gates, xc, pa, pi, lam_ref[...])
        acum, h0 = _scan_fwd(a, u)
        hi = hin_ref[0]
        h = h0 + acum * hi
        rows = lax.broadcasted_iota(jnp.int32, h.shape, 0)
        hprev = jnp.where(rows < 1, hi, pltpu.roll(h, 1, 0))
        _, vjp_o = jax.vjp(_lru_out, h, g_ref[...])
        dh, dgate = vjp_o(do_ref[...].astype(F32))
        c = jnp.where(rows >= tb - 1, 0.0, pltpu.roll(a, tb - 1, 0))
        g = _scan_rev(c, dh + jnp.where(rows == tb - 1, gc[...], 0.0))
        gc[...] = a[0:1, :] * g[0:1, :]
        dxc, dpa, dpi, dlam = vjp_g((g * hprev, g))
        dlam_ref[...] += dlam
        dba_ref[...] += jnp.sum(dpa, axis=0, keepdims=True)
        dbi_ref[...] += jnp.sum(dpi, axis=0, keepdims=True)
        parts = []
        for n in range(nblk):
            sl = slice(n * LRU_BLOCK, (n + 1) * LRU_BLOCK)
            dwa_ref[n] += mm_tn(xc[:, sl], dpa[:, sl])
            dwi_ref[n] += mm_tn(xc[:, sl], dpi[:, sl])
            parts.append(mm_nt(dpa[:, sl], wa_ref[n]) + mm_nt(dpi[:, sl], wi_ref[n]))
        dxc = dxc + jnp.concatenate(parts, axis=1)
        dx, dcw, dcb = _conv_bwd(dxc, dxcn[...], x, prev, cw_v)
        dxcn[...] = dxc[:SUBLANE]
        dcw_ref[...] += dcw
        dcb_ref[...] += dcb
        dxg_ref[:, :lw] = dx.astype(dxg_ref.dtype)
        dxg_ref[:, lw:] = dgate.astype(dxg_ref.dtype)

    row = lambda col: pl.BlockSpec((tb, lw), lambda p: (nb - 1 - p, col))
    params = [cw, cb, wa, ba, wi, bi, lam]
    return _pcall(
        body, carry=carry, name="lru_bwd", grid=(nb,),
        in_specs=[row(xcol),
                  pl.BlockSpec((SUBLANE, lw), lambda p: (jnp.maximum((nb - 1 - p) * (tb // SUBLANE) - 1, 0), xcol)),
                  row(xcol + 1)]
        + [_full(p.shape) for p in params]
        + [pl.BlockSpec((1, 1, lw), lambda p: (nb - 1 - p, 0, 0)), row(docol), ANY],
        out_specs=[pl.BlockSpec((tb, 2 * lw), lambda p: (nb - 1 - p, xcol // 2))] + [_full(p.shape) for p in params],
        out_shape=[jax.ShapeDtypeStruct(dproj.shape, dproj.dtype)]
        + [jax.ShapeDtypeStruct(p.shape, F32) for p in params],
        input_output_aliases={12: 0},
        scratch_shapes=[pltpu.VMEM((1, lw), F32), pltpu.VMEM((SUBLANE, lw), F32)],
        compiler_params=_cparams(("arbitrary",)),
    )(proj, proj, proj, *params, hin, dmix, dproj)


def conv_silu_fwd(proj, col0, width, cw, cb, carry=None):
    t = proj.shape[0]
    tb = _pick(t, (512, 256, 128, 64))
    cbw = _pick(width, (512, 256, 128))
    off = col0 // cbw
    assert col0 % cbw == 0

    def body(x_ref, xp_ref, w_ref, b_ref, o_ref):
        prev = jnp.where(pl.program_id(1) == 0, 0.0, xp_ref[...])
        o_ref[...] = _silu(_conv(x_ref[...], prev, w_ref[...], b_ref[...]))

    return _pcall(
        body, carry=carry, name="conv_silu_fwd", grid=(width // cbw, t // tb),
        in_specs=[pl.BlockSpec((tb, cbw), lambda j, i: (i, off + j)),
                  pl.BlockSpec((SUBLANE, cbw), lambda j, i: (jnp.maximum(i * (tb // SUBLANE) - 1, 0), off + j)),
                  pl.BlockSpec((CONV_WIDTH, cbw), lambda j, i: (0, j)), pl.BlockSpec((1, cbw), lambda j, i: (0, j))],
        out_specs=pl.BlockSpec((tb, cbw), lambda j, i: (i, j)),
        out_shape=jax.ShapeDtypeStruct((t, width), F32),
        compiler_params=_cparams(("parallel", "arbitrary")),
    )(proj, proj, cw, cb)


def conv_silu_bwd(proj, col0, width, cw, cb, dact, dproj, carry=None):
    t = proj.shape[0]
    tb = _pick(t, (512, 256, 128, 64))
    nb = t // tb
    cbw = _pick(width, (512, 256, 128))
    off = col0 // cbw

    def body(x_ref, xp_ref, w_ref, b_ref, d_ref, _, dx_ref, dw_ref, db_ref, nxt):
        pid = pl.program_id(1)
        i = nb - 1 - pid

        @pl.when(pid == 0)
        def _():
            nxt[...] = jnp.zeros_like(nxt)
            dw_ref[...] = jnp.zeros_like(dw_ref)
            db_ref[...] = jnp.zeros_like(db_ref)

        x = x_ref[...]
        prev = jnp.where(i == 0, 0.0, xp_ref[...])
        w = w_ref[...]
        _, vjp = jax.vjp(_silu, _conv(x, prev, w, b_ref[...]))
        (dcv,) = vjp(d_ref[...])
        dx, dw, db = _conv_bwd(dcv, nxt[...], x, prev, w)
        nxt[...] = dcv[:SUBLANE]
        dx_ref[...] = dx.astype(dx_ref.dtype)
        dw_ref[...] += dw
        db_ref[...] += db

    return _pcall(
        body, carry=carry, name="conv_silu_bwd", grid=(width // cbw, nb),
        in_specs=[pl.BlockSpec((tb, cbw), lambda j, p: (nb - 1 - p, off + j)),
                  pl.BlockSpec((SUBLANE, cbw),
                               lambda j, p: (jnp.maximum((nb - 1 - p) * (tb // SUBLANE) - 1, 0), off + j)),
                  pl.BlockSpec((CONV_WIDTH, cbw), lambda j, p: (0, j)), pl.BlockSpec((1, cbw), lambda j, p: (0, j)),
                  pl.BlockSpec((tb, cbw), lambda j, p: (nb - 1 - p, j)), ANY],
        out_specs=[pl.BlockSpec((tb, cbw), lambda j, p: (nb - 1 - p, off + j)),
                   pl.BlockSpec((CONV_WIDTH, cbw), lambda j, p: (0, j)), pl.BlockSpec((1, cbw), lambda j, p: (0, j))],
        out_shape=[jax.ShapeDtypeStruct(dproj.shape, dproj.dtype), jax.ShapeDtypeStruct(cw.shape, F32),
                   jax.ShapeDtypeStruct(cb.shape, F32)],
        scratch_shapes=[pltpu.VMEM((SUBLANE, cbw), F32)],
        input_output_aliases={5: 0},
        compiler_params=_cparams(("parallel", "arbitrary")),
    )(proj, proj, cw, cb, dact, dproj)


def _dt_expand(raw, bias, e):
    return sel_r(_softplus(raw + bias), e)


def dt_fwd(proj, dtcol, bias, e):
    t = proj.shape[0]
    di = e.shape[1]
    tb = _pick(t, (512, 256, 128, 64))

    def body(r_ref, b_ref, e_ref, o_ref):
        o_ref[...] = _dt_expand(r_ref[...], b_ref[...], e_ref[...])

    return pl.pallas_call(
        body, name="dt_fwd", grid=(t // tb,),
        in_specs=[pl.BlockSpec((tb, LANE), lambda i: (i, dtcol)), _full(bias.shape), _full(e.shape)],
        out_specs=pl.BlockSpec((tb, di), lambda i: (i, 0)),
        out_shape=jax.ShapeDtypeStruct((t, di), F32),
        compiler_params=_cparams(("parallel",)),
    )(proj, bias, e)


def dt_bwd(proj, dtcol, bias, e, ddte, dproj):
    t = proj.shape[0]
    di = e.shape[1]
    tb = _pick(t, (512, 256, 128, 64))
    tail = dproj.shape[1] - dtcol * LANE
    assert (dtcol * LANE) % tail == 0

    def body(r_ref, b_ref, e_ref, d_ref, _, dr_ref, db_ref):
        @pl.when(pl.program_id(0) == 0)
        def _():
            db_ref[...] = jnp.zeros_like(db_ref)

        e_v = e_ref[...]
        _, vjp = jax.vjp(lambda r, b: _dt_expand(r, b, e_v), r_ref[...], b_ref[...])
        dr, db = vjp(d_ref[...])
        dr_ref[:, :LANE] = dr.astype(dr_ref.dtype)
        if tail > LANE:
            dr_ref[:, LANE:] = jnp.zeros((tb, tail - LANE), dr_ref.dtype)
        db_ref[...] += db

    return pl.pallas_call(
        body, name="dt_bwd", grid=(t // tb,),
        in_specs=[pl.BlockSpec((tb, LANE), lambda i: (i, dtcol)), _full(bias.shape), _full(e.shape),
                  pl.BlockSpec((tb, di), lambda i: (i, 0)), ANY],
        out_specs=[pl.BlockSpec((tb, tail), lambda i: (i, dtcol * LANE // tail)), _full(bias.shape)],
        out_shape=[jax.ShapeDtypeStruct(dproj.shape, dproj.dtype), jax.ShapeDtypeStruct(bias.shape, F32)],
        input_output_aliases={4: 0},
        compiler_params=_cparams(("arbitrary",)),
    )(proj, bias, e, ddte, dproj)


def head_expand(p, e, *, transpose=False, name):
    di = e.shape[1]

    def body(p_ref, e_ref, o_ref):
        if transpose:
            o_ref[...] = _sel_r_b(e_ref[...], p_ref[...])[0]
        else:
            o_ref[...] = sel_r(p_ref[...], e_ref[...])

    oshape = (SUBLANE, LANE) if transpose else (SUBLANE, di)
    return pl.pallas_call(
        body, name=name, in_specs=[_full(p.shape), _full(e.shape)], out_specs=_full(oshape),
        out_shape=jax.ShapeDtypeStruct(oshape, F32), compiler_params=_cparams(None), grid=(1,),
    )(p, e)


def _ssd_chunk(x, z, bm, cm, dte, st, alog, dskip, gn, tri, cmask, dmask, bd):
    L, gw = x.shape
    reps = gw // L
    a = dte * (-jnp.exp(alog))
    acs = sel_l(tri, a)
    acs_last = jnp.sum(a, axis=0, keepdims=True)
    arow = jnp.sum(acs * dmask, axis=0, keepdims=True)
    dtrow = jnp.sum(dte * dmask, axis=0, keepdims=True)
    cb = mm_nt(cm, jnp.concatenate([bm] * reps, axis=0))
    wts = cb * (jnp.exp(jnp.minimum(acs - arow, 0.0)) * cmask) * dtrow
    xbd = jnp.concatenate([x] * reps, axis=0) * bd
    xw = x * (jnp.exp(acs_last - acs) * dte)
    y = mm(wts, xbd) + mm(cm, st) * jnp.exp(acs) + dskip * x
    st_new = jnp.exp(acs_last) * st + mm_tn(bm, xw)
    return _rms(y * _silu(z), gn), st_new


def _ssd_dims(di):
    gw = di // SSD_GROUPS
    assert CHUNK == SSD_HEAD_DIM and gw % LANE == 0
    return gw, SSD_STATE


def _ssd_masks(gw):
    L = CHUNK
    r = jnp.arange(L)[:, None]
    c = jnp.arange(gw)[None, :]
    cmask = ((c % L) <= r).astype(F32)
    dmask = ((c % L) == r).astype(F32)
    rr = jnp.arange(gw)
    bd = ((rr[:, None] // L) == (rr[None, :] // L)).astype(F32)
    tri = (jnp.arange(L)[None, :] <= jnp.arange(L)[:, None]).astype(BF16)
    return tri, cmask, dmask, bd


def ssd_fwd(xs, proj, dte, alog_e, dskip_e, gn, carry=None):
    t, di = dte.shape
    gw, n = _ssd_dims(di)
    L, G = CHUNK, SSD_GROUPS
    nc = t // L
    masks = _ssd_masks(gw)
    cdim = xs.shape[1]

    def body(x_ref, z_ref, dt_ref, al_ref, ds_ref, gn_ref, tri_ref, cm_ref, dm_ref, bd_ref, y_ref, sp_ref, st):
        @pl.when(pl.program_id(0) == 0)
        def _():
            st[...] = jnp.zeros_like(st)

        for g in range(G):
            ch = slice(g * gw, (g + 1) * gw)
            s_prev = st[g]
            sp_ref[0, g] = s_prev
            y, s_new = _ssd_chunk(x_ref[:, ch], z_ref[:, ch], x_ref[:, di + g * n:di + (g + 1) * n],
                                  x_ref[:, di + (G + g) * n:di + (G + g + 1) * n], dt_ref[:, ch], s_prev,
                                  al_ref[0:1, ch], ds_ref[0:1, ch], gn_ref[:, ch], tri_ref[...], cm_ref[...],
                                  dm_ref[...], bd_ref[...])
            y_ref[:, ch] = y.astype(y_ref.dtype)
            st[g] = s_new

    row = lambda w: pl.BlockSpec((L, w), lambda c: (c, 0))
    return _pcall(
        body, carry=carry, name="ssd_fwd", grid=(nc,),
        in_specs=[row(cdim), row(di), row(di), _full(alog_e.shape), _full(dskip_e.shape), _full(gn.shape)]
        + [_full(m.shape) for m in masks],
        out_specs=[row(di), pl.BlockSpec((1, G, n, gw), lambda c: (c, 0, 0, 0))],
        out_shape=[jax.ShapeDtypeStruct((t, di), BF16), jax.ShapeDtypeStruct((nc, G, n, gw), F32)],
        scratch_shapes=[pltpu.VMEM((G, n, gw), F32)],
        compiler_params=_cparams(("arbitrary",)),
    )(xs, proj, dte, alog_e, dskip_e, gn, *masks)


def ssd_bwd(xs, proj, dte, alog_e, dskip_e, gn, sprev, dy, dproj_shape, carry=None):
    t, di = dte.shape
    gw, n = _ssd_dims(di)
    L, G = CHUNK, SSD_GROUPS
    nc = t // L
    masks = _ssd_masks(gw)
    cdim = xs.shape[1]

    def body(x_ref, z_ref, dt_ref, al_ref, ds_ref, gn_ref, tri_ref, cm_ref, dm_ref, bd_ref, sp_ref, dy_ref,
             dxs_ref, dz_ref, ddt_ref, dal_ref, dds_ref, dgn_ref, dst):
        @pl.when(pl.program_id(0) == 0)
        def _():
            dst[...] = jnp.zeros_like(dst)
            dal_ref[...] = jnp.zeros_like(dal_ref)
            dds_ref[...] = jnp.zeros_like(dds_ref)
            dgn_ref[...] = jnp.zeros_like(dgn_ref)

        f = functools.partial(_ssd_chunk, tri=tri_ref[...], cmask=cm_ref[...], dmask=dm_ref[...], bd=bd_ref[...])
        for g in range(G):
            ch = slice(g * gw, (g + 1) * gw)
            bs = slice(di + g * n, di + (g + 1) * n)
            cs = slice(di + (G + g) * n, di + (G + g + 1) * n)
            _, vjp = jax.vjp(f, x_ref[:, ch], z_ref[:, ch], x_ref[:, bs], x_ref[:, cs], dt_ref[:, ch], sp_ref[0, g],
                             al_ref[0:1, ch], ds_ref[0:1, ch], gn_ref[:, ch])
            dx, dz, db, dc, ddt, ds, dal, dds, dgn = vjp((dy_ref[:, ch], dst[g]))
            dxs_ref[:, ch] = dx
            dxs_ref[:, bs] = db
            dxs_ref[:, cs] = dc
            dz_ref[:, ch] = dz.astype(dz_ref.dtype)
            ddt_ref[:, ch] = ddt
            dst[g] = ds
            dal_ref[:, ch] += dal
            dds_ref[:, ch] += dds
            dgn_ref[:, ch] += dgn

    row = lambda w: pl.BlockSpec((L, w), lambda c: (nc - 1 - c, 0))
    acc = _full((1, di))
    acc_shape = jax.ShapeDtypeStruct((1, di), F32)
    return _pcall(
        body, carry=carry, name="ssd_bwd", grid=(nc,),
        in_specs=[row(cdim), row(di), row(di), _full(alog_e.shape), _full(dskip_e.shape), _full(gn.shape)]
        + [_full(m.shape) for m in masks]
        + [pl.BlockSpec((1, G, n, gw), lambda c: (nc - 1 - c, 0, 0, 0)), row(di)],
        out_specs=[row(cdim), row(di), row(di), acc, acc, acc],
        out_shape=[jax.ShapeDtypeStruct((t, cdim), F32), jax.ShapeDtypeStruct(dproj_shape, BF16),
                   jax.ShapeDtypeStruct((t, di), F32), acc_shape, acc_shape, acc_shape],
        scratch_shapes=[pltpu.VMEM((G, n, gw), F32)],
        compiler_params=_cparams(("arbitrary",)),
    )(xs, proj, dte, alog_e, dskip_e, gn, *masks, sprev, dy)


def _rows2d(a):
    return a.reshape(-1, a.shape[-1])


def _row_tile(rows, cols):
    cap = max(SUBLANE, (1 << 19) // max(cols, 1))
    for c in (2048, 1024, 512, 256, 128, 64, 32, 16, 8):
        if c <= cap and rows % c == 0:
            return c
    return rows


def chip_sum(g, r, core, *, name):
    shape = r.shape
    cols = shape[-1]
    g4 = g.reshape(4, 2, -1, cols)
    r3 = r.reshape(4, -1, cols)
    rows = r3.shape[1]
    tr = _row_tile(rows, cols)

    def body(c_ref, g_ref, r_ref, o_ref):
        o_ref[...] = (g_ref[...].astype(F32) + r_ref[...].astype(F32)).astype(o_ref.dtype)

    out = pl.pallas_call(
        body, name=name,
        grid_spec=pltpu.PrefetchScalarGridSpec(
            num_scalar_prefetch=1, grid=(4, rows // tr),
            in_specs=[pl.BlockSpec((None, None, tr, cols), lambda j, i, c: (j, c[0], i, 0)),
                      pl.BlockSpec((None, tr, cols), lambda j, i, c: (j, i, 0))],
            out_specs=pl.BlockSpec((None, tr, cols), lambda j, i, c: (j, i, 0))),
        out_shape=jax.ShapeDtypeStruct(r3.shape, BF16), compiler_params=_cparams(("parallel", "parallel")),
    )(core.reshape(1).astype(jnp.int32), g4, r3)
    return out.reshape(shape)


def mesh_sum(p, r, chip, core, *, name):
    shape = p.shape[1:]
    cols = shape[-1]
    p3 = p.reshape(4, -1, cols)
    r3 = r.reshape(3, -1, cols)
    rows = p3.shape[1]
    tr = _row_tile(rows, 2 * cols)

    def body(c_ref, p_ref, r_ref, o_ref):
        o_ref[...] = ((p_ref[...].astype(F32) + r_ref[0].astype(F32)) + r_ref[1].astype(F32)) + r_ref[2].astype(F32)

    out = pl.pallas_call(
        body, name=name,
        grid_spec=pltpu.PrefetchScalarGridSpec(
            num_scalar_prefetch=1, grid=(rows // tr,),
            in_specs=[pl.BlockSpec((None, tr, cols), lambda i, c: (c[0], i, 0)),
                      pl.BlockSpec((3, tr, cols), lambda i, c: (0, i, 0))],
            out_specs=pl.BlockSpec((None, tr, cols), lambda i, c: (c[1], i, 0))),
        out_shape=jax.ShapeDtypeStruct((2, rows, cols), F32), compiler_params=_cparams(("parallel",)),
    )(jnp.stack([chip, core]).astype(jnp.int32), p3, r3)
    return out.reshape((2,) + shape)


def sum_leading(x, *, name):
    k, r, c = x.shape
    tr = _row_tile(r, c * k)
    def body(x_ref, o_ref):
        acc = x_ref[0]
        for i in range(1, k):
            acc = acc + x_ref[i]
        o_ref[...] = acc

    return pl.pallas_call(
        body, name=name, grid=(r // tr,), in_specs=[pl.BlockSpec((k, tr, c), lambda i: (0, i, 0))],
        out_specs=pl.BlockSpec((tr, c), lambda i: (i, 0)),
        out_shape=jax.ShapeDtypeStruct((r, c), F32), compiler_params=_cparams(("parallel",)),
    )(x)


def adamw(w, g, m, v, *, name, carry=None):
    shape = w.shape
    w2, g2, m2, v2 = (_rows2d(a) for a in (w, g, m, v))
    r, c = w2.shape
    tr = _row_tile(r, 2 * c)
    row = pl.BlockSpec((tr, c), lambda i: (i, 0))
    c1 = 1.0 - ADAM_B1 ** ADAM_STEP
    c2 = 1.0 - ADAM_B2 ** ADAM_STEP

    def body(w_ref, g_ref, m_ref, v_ref, d_ref, mo_ref, vo_ref):
        gv = g_ref[...]
        mn = ADAM_B1 * m_ref[...] + (1.0 - ADAM_B1) * gv
        vn = ADAM_B2 * v_ref[...] + (1.0 - ADAM_B2) * (gv * gv)
        d_ref[...] = -ADAM_LR * ((mn / c1) / (jnp.sqrt(vn / c2) + ADAM_EPS) + ADAM_WD * w_ref[...])
        mo_ref[...] = mn
        vo_ref[...] = vn

    outs = _pcall(
        body, name=name, grid=(r // tr,), in_specs=[row] * 4, out_specs=[row] * 3, carry=carry,
        out_shape=[jax.ShapeDtypeStruct((r, c), F32)] * 3, compiler_params=_cparams(("parallel",)),
    )(w2, g2, m2, v2)
    outs, carried = outs if carry is not None else (outs, None)
    outs = tuple(o.reshape(shape) for o in outs)
    return outs if carry is None else (outs, carried)


ANY = pl.BlockSpec(memory_space=pl.ANY)


def _place():
    x, y, c = lax.axis_index("x"), lax.axis_index("y"), lax.axis_index("c")
    chips = [(1 - x, y), (x, 1 - y), (1 - x, 1 - y)]
    return x, y, c, chips


def gather8(block):
    m, n = block.shape

    def body(x_ref, out_ref, send_sems, recv_sems, local_sem):
        x, y, c, chips = _place()
        me, sibling = (x, y, c), (x, y, 1 - c)

        def rows(px, py, pc):
            return out_ref.at[4 * px + 2 * py + pc]

        def copy(k, blk, to, src=None):
            return pltpu.make_async_remote_copy(
                src_ref=rows(*blk) if src is None else src, dst_ref=rows(*blk), send_sem=send_sems.at[k],
                recv_sem=recv_sems.at[k], device_id=to, device_id_type=MESH)

        mine = pltpu.make_async_copy(x_ref, rows(*me), local_sem)
        mine.start()
        first = [copy(0, me, sibling, src=x_ref)]
        first += [copy(1 + j, me, (*chip, c), src=x_ref) for j, chip in enumerate(chips)]
        for cp in first:
            cp.start()
        passed = [copy(4 + j, (*chip, c), sibling) for j, chip in enumerate(chips)]
        for j, chip in enumerate(chips):
            copy(1 + j, (*chip, c), me).wait_recv()
            passed[j].start()
        copy(0, sibling, me).wait_recv()
        for j, chip in enumerate(chips):
            copy(4 + j, (*chip, 1 - c), me).wait_recv()
        for cp in first + passed:
            cp.wait_send()
        mine.wait()

    return pl.pallas_call(
        body, name="gather8",
        out_shape=jax.ShapeDtypeStruct((8, m, n), block.dtype),
        in_specs=[pl.BlockSpec(memory_space=pltpu.VMEM)],
        out_specs=pl.BlockSpec(memory_space=pltpu.VMEM),
        scratch_shapes=[pltpu.SemaphoreType.DMA((7,)), pltpu.SemaphoreType.DMA((7,)), pltpu.SemaphoreType.DMA],
        compiler_params=pltpu.CompilerParams(vmem_limit_bytes=VMEM_LIMIT),
    )(block)


def gather_weights(shards):
    n = len(shards)

    def copy(ins, outs, send, recv, base, a, k, chip_idx, half, to, src=None):
        dst = outs[a].at[chip_idx, half]
        return pltpu.make_async_remote_copy(
            src_ref=dst if src is None else src, dst_ref=dst, send_sem=send.at[base + 6 * a + k],
            recv_sem=recv.at[base + 6 * a + k], device_id=to, device_id_type=MESH)

    def first(ins, outs, send, recv, base):
        x, y, c, chips = _place()
        return [copy(ins, outs, send, recv, base, a, j, 2 * x + y, c, (*chip, c), src=ins[a].at[c])
                for a in range(n) for j, chip in enumerate(chips)]

    def start(ins, outs, send, recv, base):
        for cp in first(ins, outs, send, recv, base):
            cp.start()

    def finish(ins, outs, send, recv, base):
        x, y, c, chips = _place()
        sibling = (x, y, 1 - c)
        passed = []
        for a in range(n):
            for j, (cx, cy) in enumerate(chips):
                copy(ins, outs, send, recv, base, a, j, 2 * cx + cy, c, (cx, cy, c)).wait_recv()
                fw = copy(ins, outs, send, recv, base, a, 3 + j, 2 * cx + cy, c, sibling)
                fw.start()
                passed.append(fw)
        for a in range(n):
            for j, (cx, cy) in enumerate(chips):
                copy(ins, outs, send, recv, base, a, 3 + j, 2 * cx + cy, 1 - c, sibling).wait_recv()
        for cp in first(ins, outs, send, recv, base) + passed:
            cp.wait_send()

    return Carry(shards, [jax.ShapeDtypeStruct((4,) + s.shape, s.dtype) for s in shards], 6 * n, start, finish)


def exchange_halves(grads):
    n = len(grads)

    def copies(ins, outs, send, recv, base):
        x, y, c, _ = _place()
        return [pltpu.make_async_remote_copy(
            src_ref=ins[a].at[j, 1 - c], dst_ref=outs[a].at[j], send_sem=send.at[base + 4 * a + j],
            recv_sem=recv.at[base + 4 * a + j], device_id=(x, y, 1 - c), device_id_type=MESH)
            for a in range(n) for j in range(4)]

    def start(*args):
        for cp in copies(*args):
            cp.start()

    def finish(*args):
        for cp in copies(*args):
            cp.wait()

    return Carry(grads, [jax.ShapeDtypeStruct((4,) + g.shape[2:], g.dtype) for g in grads], 4 * n, start, finish)


def scatter_chips(parts):
    n = len(parts)

    def copies(ins, outs, send, recv, base):
        x, y, c, chips = _place()
        return [pltpu.make_async_remote_copy(
            src_ref=ins[a].at[2 * cx + cy], dst_ref=outs[a].at[j], send_sem=send.at[base + 3 * a + j],
            recv_sem=recv.at[base + 3 * a + j], device_id=(cx, cy, c), device_id_type=MESH)
            for a in range(n) for j, (cx, cy) in enumerate(chips)]

    def start(*args):
        for cp in copies(*args):
            cp.start()

    def finish(*args):
        for cp in copies(*args):
            cp.wait()

    return Carry(parts, [jax.ShapeDtypeStruct((3,) + p.shape[1:], p.dtype) for p in parts], 3 * n, start, finish)


def join_halves(bufs):
    n = len(bufs)

    def copy(outs, send, recv, base, a, half):
        x, y, c, _ = _place()
        return pltpu.make_async_remote_copy(
            src_ref=outs[a].at[c], dst_ref=outs[a].at[c if half is None else half], send_sem=send.at[base + a],
            recv_sem=recv.at[base + a], device_id=(x, y, 1 - c), device_id_type=MESH)

    def start(ins, outs, send, recv, base):
        for a in range(n):
            copy(outs, send, recv, base, a, None).start()

    def finish(ins, outs, send, recv, base):
        c = lax.axis_index("c")
        for a in range(n):
            copy(outs, send, recv, base, a, 1 - c).wait_recv()
        for a in range(n):
            copy(outs, send, recv, base, a, None).wait_send()

    return Carry(bufs, [jax.ShapeDtypeStruct(h.shape, h.dtype) for h in bufs], n, start, finish,
                 aliases={a: a for a in range(n)})


def run_comm(carry, *, name):
    k_in, k_out = len(carry.arrays), len(carry.out_shape)

    def body(*refs):
        ins, outs = refs[:k_in], refs[k_in:k_in + k_out]
        send, recv = refs[-2:]
        carry.start(ins, outs, send, recv, 0)
        carry.finish(ins, outs, send, recv, 0)

    return pl.pallas_call(
        body, name=name, out_shape=carry.out_shape, in_specs=[ANY] * k_in, out_specs=[ANY] * k_out,
        scratch_shapes=[pltpu.SemaphoreType.DMA((carry.n_sems,))] * 2, input_output_aliases=carry.aliases,
    )(*carry.arrays)


INPUTS = ['x'] + WEIGHTS + ['loss_target'] + ['m_' + n for n in WEIGHTS] + ['v_' + n for n in WEIGHTS]


def _round_up(n, m):
    return -(-n // m) * m


def _pack(arrs):
    flat = jnp.concatenate([a.reshape(-1) for a in arrs])
    n = _round_up(flat.shape[0], 512 * LANE)
    return jnp.pad(flat, (0, n - flat.shape[0])).reshape(-1, LANE)


def _unpack(block, shapes):
    flat = block.reshape(-1)
    out, o = [], 0
    for s in shapes:
        n = math.prod(s)
        out.append(flat[o:o + n].reshape(s))
        o += n
    return out


def _cols(g):
    return g.transpose(1, 0, 2).reshape(g.shape[1], -1)


def _uncols(w):
    return w.reshape(w.shape[0], 4, -1).transpose(1, 0, 2)


def _pad_cols(w, total):
    return jnp.pad(w, ((0, 0), (0, total - w.shape[1])))


def kernel(*args):
    a = dict(zip(INPUTS, args))
    x, tgt = a['x'][0], a['loss_target'][0]
    t, d = x.shape
    xi, yi, ci = lax.axis_index("x"), lax.axis_index("y"), lax.axis_index("c")
    chip = 2 * xi + yi
    dk, dv, hk, hv = _gla_dims(d)
    lw = d // 2
    di = 2 * d
    nh = di // SSD_HEAD_DIM
    gn_w = SSD_GROUPS * SSD_STATE
    conv_dim = di + 2 * gn_w
    rank = GLA_GATE_RANK
    wq = 2 * dk + 2 * dv
    gla_w = _round_up(wq + LANE, 2 * lw)
    ev_tot = gla_w + 2 * lw
    od_used = di + conv_dim + nh
    od_tot = _round_up(di + conv_dim + _round_up(nh, LANE), 512)
    glr_col, xcol, dtcol = wq // LANE, gla_w // lw, (di + conv_dim) // LANE
    assert ev_tot % 512 == 0 and nh <= LANE

    def halves(w):
        w = w.astype(BF16)
        return w.reshape((2, w.shape[0] // 2) + w.shape[1:])

    own = {'ev_w_in': halves(a['ev_w_in'][0]), 'ev_w_out': halves(a['ev_w_out'][0]),
           'od_w_in_a': halves(a['od_w_in'][0][:d // 2]), 'od_w_in_b': halves(a['od_w_in'][0][d // 2:]),
           'od_w_out': halves(a['od_w_out'][0])}
    for l in range(2):
        own[f'gate{l}'], own[f'up{l}'] = halves(a['ffn_w_gate'][l]), halves(a['ffn_w_up'][l])
        own[f'down{l}'] = halves(a['ffn_w_down'][l])

    def gather(*units):
        return gather_weights([own[u] for u in units])

    def filled(unit, g):
        g = lax.dynamic_update_index_in_dim(g, own[unit], chip, 0)
        return g.reshape((4, 2 * g.shape[2]) + g.shape[3:])

    g_ev_in, g_ev_out = run_comm(gather('ev_w_in', 'ev_w_out'), name="gather_ev")
    w_ev_in = _cols(filled('ev_w_in', g_ev_in))
    cuts = [dk, 2 * dk, 2 * dk + dv, wq, wq + rank, wq + rank + lw]
    sq, sk, sv, sg, sglr, sxb, sgb = jnp.split(w_ev_in, cuts, axis=1)
    w_ev_in_p = jnp.concatenate([_pad_cols(jnp.concatenate([sq, sk, sv, sg, sglr], axis=1), gla_w), sxb, sgb], axis=1)
    w_ev_out = filled('ev_w_out', g_ev_out).reshape(-1, d)
    w_gate, w_up, w_down = [None, None], [None, None], [None, None]

    sh_names = list(SMALL_SHARDED)
    sh_shapes = [a[n].shape for n in sh_names]
    g8 = gather8(_pack([a[n] for n in sh_names]))
    per_chip = [_unpack(g8[2 * j], sh_shapes) for j in range(4)]
    full = {n: jnp.concatenate([per_chip[j][i] for j in range(4)], axis=SMALL_SHARDED[n])
            for i, n in enumerate(sh_names)}

    wg_p = jnp.zeros((LANE, dk), F32).at[:rank].set(full['ev_gla_w_gate'][0])
    bg, wn = a['ev_gla_b_gate'], a['ev_gla_w_onorm']
    lru_p = [full['ev_lru_conv_w'][0], a['ev_lru_conv_b'], a['ev_lru_w_a'][0], a['ev_lru_b_a'], a['ev_lru_w_i'][0],
             a['ev_lru_b_i'], a['ev_lru_lam']]
    od_cw, od_cb, od_gn = full['od_conv_w'][0], full['od_conv_b'], full['od_gnorm']
    heads = jnp.arange(LANE)[:, None]
    e_mat = ((jnp.arange(di)[None, :] // SSD_HEAD_DIM == heads) & (heads < nh)).astype(BF16)
    row8 = lambda p: jnp.zeros((SUBLANE, LANE), F32).at[0, :nh].set(p[0])
    dt_bias_p = jnp.zeros((1, LANE), F32).at[0, :nh].set(a['od_dt_bias'][0])
    alog_e = head_expand(row8(a['od_a_log']), e_mat, name="expand_a_log")
    dskip_e = head_expand(row8(a['od_d_skip']), e_mat, name="expand_d_skip")

    h0 = rms_fwd(x, a['ev_norm'], name="rms_ev")
    proj, (g,) = matmul(h0, w_ev_in_p, name="ev_in", carry=gather('gate0'))
    w_gate[0] = filled('gate0', g)
    (o_gla, sp_gla), (g,) = gla_fwd(proj, glr_col, wg_p, bg, wn, dv, carry=gather('up0'))
    w_up[0] = filled('up0', g)
    o_lru, hin = lru_fwd(proj, xcol, lw, *lru_p)
    x1 = matmul(o_gla, w_ev_out[:dv], add=x, name="ev_out_a")
    x1 = matmul(o_lru, w_ev_out[dv:], add=x1, name="ev_out_b")

    h1 = rms_fwd(x1, a['ffn_norm'][0:1], name="rms_ffn0")
    gate0, (g,) = matmul(h1, w_gate[0], name="ffn0_gate", carry=gather('down0'))
    w_down[0] = filled('down0', g).reshape(-1, d)
    swi = dict(epi=_swi_fwd_epi, epi_out=(F32, BF16), tm_cap=512)
    (up0, act0), (g_a,) = matmul(h1, w_up[0], name="ffn0_up", epi_in=(gate0,), carry=gather('od_w_in_a'), **swi)
    x2, (g_b,) = matmul(act0, w_down[0], add=x1, name="ffn0_down", carry=gather('od_w_in_b'))
    w_od_in = jnp.concatenate([filled('od_w_in_a', g_a), filled('od_w_in_b', g_b)], axis=1)
    w_od_in_p = _pad_cols(_cols(w_od_in), od_tot)

    h2 = rms_fwd(x2, full['od_norm'], name="rms_od")
    proj2, (g, g1) = matmul(h2, w_od_in_p, name="od_in", carry=gather('od_w_out', 'gate1'))
    w_od_out, w_gate[1] = filled('od_w_out', g).reshape(-1, d), filled('gate1', g1)
    xs, (g,) = conv_silu_fwd(proj2, di, conv_dim, od_cw, od_cb, carry=gather('up1'))
    w_up[1] = filled('up1', g)
    dte = dt_fwd(proj2, dtcol, dt_bias_p, e_mat)
    (y_ssd, sp_ssd), (g,) = ssd_fwd(xs, proj2, dte, alog_e, dskip_e, od_gn, carry=gather('down1'))
    w_down[1] = filled('down1', g).reshape(-1, d)
    x3 = matmul(y_ssd, w_od_out, add=x2, name="od_out")
    h3 = rms_fwd(x3, a['ffn_norm'][1:2], name="rms_ffn1")
    gate1 = matmul(h3, w_gate[1], name="ffn1_gate")
    up1, act1 = matmul(h3, w_up[1], name="ffn1_up", epi_in=(gate1,), **swi)
    x4 = matmul(act1, w_down[1], add=x3, name="ffn1_down")
    loss_p, dx4, d_final = loss_head(x4, a['final_norm'][None], tgt, name="loss_head")

    grads, from_sib, part, from_chips = {}, {}, {}, {}

    def rows4(dw):
        return dw.reshape((4, 2, dw.shape[0] // 8) + dw.shape[1:])

    def cols4(dw):
        return dw.reshape((4, 2, dw.shape[1] // 2) + dw.shape[2:])

    def exchange(*units):
        return exchange_halves([grads[u] for u in units])

    def scatter(*units):
        return scatter_chips([part[u] for u in units])

    def sum_chip(u):
        part[u] = chip_sum(grads[u], from_sib[u], ci, name=f"chip_sum_{u}")

    def ffn_bwd(dxo, xin, h, gate, up, act, l, first_carry, first_units):
        dn, gt, up_ = f'down{l}', f'gate{l}', f'up{l}'
        dgu = matmul(dxo, w_down[l], tb=True, name=f"ffn{l}_d_act", carry=first_carry, epi=_swi_bwd_epi,
                     epi_in=(gate, up), epi_out=(BF16, BF16))
        (dg, du), got = dgu if first_units else (dgu, ())
        for u, r in zip(first_units, got):
            from_sib[u] = r
            sum_chip(u)
        grads[dn] = rows4(matmul(act, dxo, ta=True, out_dtype=BF16, name=f"ffn{l}_dw_down"))
        dh, (from_sib[dn],) = matmul(dg, w_gate[l], tb=True, name=f"ffn{l}_dh_gate", carry=exchange(dn))
        sum_chip(dn)
        dh = matmul(du, w_up[l], tb=True, add=dh, name=f"ffn{l}_dh_up")
        d_gate, (from_chips[dn],) = matmul(h, dg, ta=True, out_dtype=BF16, out_shards=True, name=f"ffn{l}_dw_gate",
                                           carry=scatter(dn))
        grads[gt] = cols4(d_gate)
        d_up, (from_sib[gt],) = matmul(h, du, ta=True, out_dtype=BF16, out_shards=True, name=f"ffn{l}_dw_up",
                                       carry=exchange(gt))
        grads[up_] = cols4(d_up)
        dxi, d_norm = rms_bwd(xin, a['ffn_norm'][l:l + 1], dh, dxo, name=f"rms_ffn{l}_bwd")
        return dxi, d_norm

    dx3, d_fn1 = ffn_bwd(dx4, x3, h3, gate1, up1, act1, 1, None, ())
    dy, (from_sib['up1'],) = matmul(dx3, w_od_out, tb=True, name="od_out_dy", carry=exchange('up1'))
    sum_chip('gate1')
    sum_chip('up1')
    grads['od_w_out'] = rows4(matmul(y_ssd, dx3, ta=True, out_dtype=BF16, name="od_out_dw"))
    (dxs, dproj2, ddte, dal, dds, dgn), (from_chips['gate1'], from_chips['up1'], from_sib['od_w_out']) = ssd_bwd(
        xs, proj2, dte, alog_e, dskip_e, od_gn, sp_ssd, dy, proj2.shape,
        carry=merge_carries(scatter('gate1', 'up1'), exchange('od_w_out')))
    sum_chip('od_w_out')
    (dproj2, d_od_cw, d_od_cb), (from_chips['od_w_out'],) = conv_silu_bwd(
        proj2, di, conv_dim, od_cw, od_cb, dxs, dproj2, carry=scatter('od_w_out'))
    dproj2, d_dt_bias = dt_bwd(proj2, dtcol, dt_bias_p, e_mat, ddte, dproj2)
    dh2 = matmul(dproj2, w_od_in_p, tb=True, name="od_in_dh")
    d_od_in = matmul(h2, dproj2, ta=True, out_dtype=BF16, name="od_in_dw")[:, :od_used]
    grads['od_w_in'] = cols4(_uncols(d_od_in))
    dx2, d_od_norm = rms_bwd(x2, full['od_norm'], dh2, dx3, name="rms_od_bwd")
    to8 = lambda acc: jnp.zeros((SUBLANE, di), F32).at[0].set(acc.reshape(-1))
    d_a_log = head_expand(to8(dal), e_mat, transpose=True, name="reduce_a_log")[0:1, :nh]
    d_d_skip = head_expand(to8(dds), e_mat, transpose=True, name="reduce_d_skip")[0:1, :nh]

    dx1, d_fn0 = ffn_bwd(dx2, x1, h1, gate0, up0, act0, 0, exchange('od_w_in'), ('od_w_in',))
    dmix, (from_sib['up0'],) = matmul(dx1, w_ev_out, tb=True, name="ev_out_dmix", carry=exchange('up0'))
    sum_chip('gate0')
    sum_chip('up0')
    grads['ev_w_out'] = rows4(jnp.concatenate([matmul(o_gla, dx1, ta=True, out_dtype=BF16, name="ev_out_dw_a"),
                                               matmul(o_lru, dx1, ta=True, out_dtype=BF16, name="ev_out_dw_b")], axis=0))
    (dproj, d_wg, d_bg, d_wn), (from_chips['od_w_in'], from_sib['ev_w_out']) = gla_bwd(
        proj, glr_col, wg_p, bg, wn, sp_gla, dmix, dv, gla_w,
        carry=merge_carries(scatter('od_w_in'), exchange('ev_w_out')))
    sum_chip('ev_w_out')
    (dproj, *d_lru), (from_chips['ev_w_out'],) = lru_bwd(proj, xcol, lw, *lru_p, hin, dmix, 1, dproj,
                                                         carry=scatter('ev_w_out'))
    dh0, (from_chips['gate0'],) = matmul(dproj, w_ev_in_p, tb=True, name="ev_in_dh", carry=scatter('gate0'))
    d_ev_in_p, (from_chips['up0'],) = matmul(h0, dproj, ta=True, out_dtype=BF16, name="ev_in_dw", carry=scatter('up0'))
    d_ev_in = jnp.concatenate([d_ev_in_p[:, :wq + rank], d_ev_in_p[:, gla_w:]], axis=1)
    early = list(grads)
    d_ev_in = _uncols(d_ev_in)
    last = ('ev_w_in_a', 'ev_w_in_b')
    grads[last[0]], grads[last[1]] = cols4(d_ev_in[:, :d // 2]), cols4(d_ev_in[:, d // 2:])
    half = {u: mesh_sum(part[u], from_chips[u], chip, ci, name=f"mesh_sum_{u}") for u in early}
    (dx0, d_ev_norm), got = rms_bwd(x, a['ev_norm'], dh0, dx1, name="rms_ev_bwd",
                                    carry=merge_carries(exchange(*last), join_halves([half[u] for u in early])))
    from_sib[last[0]], from_sib[last[1]] = got[:2]
    done = dict(zip(early, got[2:]))
    sum_chip(last[0])
    sum_chip(last[1])
    grad = {n: done[n].reshape(a[n].shape) for n in ('ev_w_out', 'od_w_in', 'od_w_out')}
    for n, u in (('ffn_w_gate', 'gate'), ('ffn_w_up', 'up'), ('ffn_w_down', 'down')):
        grad[n] = jnp.stack([done[f'{u}{l}'].reshape(a[n].shape[1:]) for l in range(2)])
    delta, new_m, new_v = {}, {}, {}

    def adam_big(n, carry=None):
        res = adamw(a[n], grad[n], a['m_' + n], a['v_' + n], name=f"adamw_{n}", carry=carry)
        (delta[n], new_m[n], new_v[n]), got = res if carry is not None else (res, None)
        return got

    (from_chips[last[0]],) = adam_big('ffn_w_down', scatter(last[0]))
    (from_chips[last[1]],) = adam_big('od_w_in', scatter(last[1]))
    for u in last:
        half[u] = mesh_sum(part[u], from_chips[u], chip, ci, name=f"mesh_sum_{u}")
    ga, gb = adam_big('ffn_w_gate', join_halves([half[u] for u in last]))
    grad['ev_w_in'] = jnp.concatenate([ga.reshape(d // 2, -1), gb.reshape(d // 2, -1)], axis=0).reshape(a['ev_w_in'].shape)
    for n in ('ffn_w_up', 'od_w_out', 'ev_w_out', 'ev_w_in'):
        adam_big(n)

    small_g = {
        'ev_norm': d_ev_norm, 'ev_gla_w_gate': d_wg[:rank][None], 'ev_gla_b_gate': d_bg, 'ev_gla_w_onorm': d_wn,
        'ev_lru_conv_w': d_lru[0][None], 'ev_lru_conv_b': d_lru[1], 'ev_lru_w_a': d_lru[2][None],
        'ev_lru_b_a': d_lru[3], 'ev_lru_w_i': d_lru[4][None], 'ev_lru_b_i': d_lru[5], 'ev_lru_lam': d_lru[6],
        'od_norm': d_od_norm, 'od_conv_w': d_od_cw[None], 'od_conv_b': d_od_cb, 'od_dt_bias': d_dt_bias[:, :nh],
        'od_a_log': d_a_log, 'od_d_skip': d_d_skip, 'od_gnorm': dgn.reshape(1, di),
        'ffn_norm': jnp.concatenate([d_fn0, d_fn1], axis=0), 'final_norm': d_final[0],
    }
    full_shapes = [small_g[n].shape for n in SMALL]
    summed = sum_leading(gather8(_pack([small_g[n] for n in SMALL])), name="sum_devices")
    for n, g in zip(SMALL, _unpack(summed, full_shapes)):
        if n in SMALL_SHARDED:
            ax = SMALL_SHARDED[n]
            sz = a[n].shape[ax]
            g = lax.dynamic_slice_in_dim(g, chip * sz, sz, axis=ax)
        grad[n] = g

    shapes = [a[n].shape for n in SMALL]
    packed = [_pack([src[n] if pre is None else a[pre + n] for n in SMALL])
              for src, pre in ((a, None), (grad, None), (None, 'm_'), (None, 'v_'))]
    for outd, blk in zip((delta, new_m, new_v), adamw(*packed, name="adamw_small")):
        outd.update(zip(SMALL, _unpack(blk, shapes)))

    loss = lax.psum(loss_p[0, 0], ("x", "y", "c"))
    return (loss, dx0[None], *[grad[n] for n in WEIGHTS], *[delta[n] for n in WEIGHTS],
            *[new_m[n] for n in WEIGHTS], *[new_v[n] for n in WEIGHTS])
```

```python
import functools
import math

import jax
import jax.numpy as jnp
from jax import lax
from jax.experimental import pallas as pl
from jax.experimental.pallas import tpu as pltpu

F32 = jnp.float32
BF16 = jnp.bfloat16
MXU_DTYPE = jnp.bfloat16

NORM_EPS = 1e-6
CONV_WIDTH = 4
GLA_HEADS = 4
GLA_GATE_RANK = 16
GLA_GATE_NORM = 16.0
CHUNK = 64
LRU_BLOCK = 128
LRU_C = 8.0
SSD_HEAD_DIM = 64
SSD_GROUPS = 8
SSD_STATE = 128
ADAM_LR, ADAM_B1, ADAM_B2, ADAM_EPS, ADAM_WD, ADAM_STEP = 0.001, 0.9, 0.999, 1e-08, 0.01, 10

LANE = 128
SUBLANE = 8
VMEM_LIMIT = 48 * 1024 * 1024
MAX_TK = 2816
MESH = pl.DeviceIdType.MESH

WEIGHTS = ['ev_norm', 'ev_w_in', 'ev_gla_w_gate', 'ev_gla_b_gate', 'ev_gla_w_onorm', 'ev_lru_conv_w', 'ev_lru_conv_b',
           'ev_lru_w_a', 'ev_lru_b_a', 'ev_lru_w_i', 'ev_lru_b_i', 'ev_lru_lam', 'ev_w_out', 'od_norm', 'od_w_in',
           'od_conv_w', 'od_conv_b', 'od_dt_bias', 'od_a_log', 'od_d_skip', 'od_gnorm', 'od_w_out', 'ffn_norm',
           'ffn_w_gate', 'ffn_w_up', 'ffn_w_down', 'final_norm']
BIG = ['ev_w_in', 'ev_w_out', 'od_w_in', 'od_w_out', 'ffn_w_gate', 'ffn_w_up', 'ffn_w_down']
SMALL_SHARDED = {'ev_gla_w_gate': 2, 'ev_lru_conv_w': 2, 'od_norm': 1, 'od_conv_w': 2, 'od_conv_b': 1, 'od_gnorm': 1}
SMALL = [n for n in WEIGHTS if n not in BIG]


def _cparams(sem=None, **kw):
    return pltpu.CompilerParams(dimension_semantics=sem, vmem_limit_bytes=VMEM_LIMIT, **kw)


def _full(shape):
    n = len(shape)
    return pl.BlockSpec(shape, lambda *_: (0,) * n)


ANY = pl.BlockSpec(memory_space=pl.ANY)


class Carry:
    def __init__(self, arrays, out_shape, n_sems, start, finish, aliases=None):
        self.arrays, self.out_shape, self.n_sems = list(arrays), list(out_shape), n_sems
        self.start, self.finish, self.aliases = start, finish, dict(aliases or {})


def merge_carries(*cs):
    cs = [c for c in cs if c is not None]
    if not cs:
        return None
    arrays = [a for c in cs for a in c.arrays]
    out_shape = [s for c in cs for s in c.out_shape]
    offs, i0, o0, s0 = [], 0, 0, 0
    aliases = {}
    for c in cs:
        offs.append((i0, o0, s0))
        aliases.update({i0 + i: o0 + o for i, o in c.aliases.items()})
        i0, o0, s0 = i0 + len(c.arrays), o0 + len(c.out_shape), s0 + c.n_sems

    def both(which):
        def run(ins, outs, send, recv, base):
            for c, (i, o, s) in zip(cs, offs):
                getattr(c, which)(ins[i:i + len(c.arrays)], outs[o:o + len(c.out_shape)], send, recv, base + s)
        return run

    return Carry(arrays, out_shape, s0, both("start"), both("finish"), aliases)


def _pcall(body, *, name, grid, in_specs, out_specs, out_shape, scratch_shapes=(), compiler_params, carry=None,
           input_output_aliases=None):
    aliases = dict(input_output_aliases or {})
    if carry is None:
        return pl.pallas_call(body, name=name, grid=grid, in_specs=in_specs, out_specs=out_specs, out_shape=out_shape,
                              scratch_shapes=list(scratch_shapes), compiler_params=compiler_params,
                              input_output_aliases=aliases)
    single = not isinstance(out_specs, (list, tuple))
    specs_o = [out_specs] if single else list(out_specs)
    shapes_o = [out_shape] if single else list(out_shape)
    n_in, n_out, k_in, k_out, n_scr = len(in_specs), len(specs_o), len(carry.arrays), len(carry.out_shape), len(scratch_shapes)

    def wrapped(*refs):
        ins, cins = refs[:n_in], refs[n_in:n_in + k_in]
        o0 = n_in + k_in
        outs, couts = refs[o0:o0 + n_out], refs[o0 + n_out:o0 + n_out + k_out]
        scr = refs[o0 + n_out + k_out:o0 + n_out + k_out + n_scr]
        send, recv = refs[-2:]
        ids = [pl.program_id(ax) for ax in range(len(grid))]
        first = functools.reduce(jnp.logical_and, [i == 0 for i in ids])
        last = functools.reduce(jnp.logical_and, [i == g - 1 for i, g in zip(ids, grid)])

        @pl.when(first)
        def _():
            carry.start(cins, couts, send, recv, 0)

        body(*ins, *outs, *scr)

        @pl.when(last)
        def _():
            carry.finish(cins, couts, send, recv, 0)

    aliases.update({n_in + i: n_out + o for i, o in carry.aliases.items()})
    call = pl.pallas_call(
        wrapped, name=name, grid=grid, in_specs=list(in_specs) + [ANY] * k_in, out_specs=specs_o + [ANY] * k_out,
        out_shape=shapes_o + carry.out_shape,
        scratch_shapes=list(scratch_shapes) + [pltpu.SemaphoreType.DMA((carry.n_sems,))] * 2,
        compiler_params=_cparams(("arbitrary",) * len(grid)), input_output_aliases=aliases)

    def run(*args):
        res = call(*args, *carry.arrays)
        main = res[:n_out]
        return (main[0] if single else list(main)), list(res[n_out:])

    return run


def _pick(dim, cands):
    for c in cands:
        if dim % c == 0:
            return c
    return dim


def _dot(a, b, ca, cb):
    return lax.dot_general(a.astype(MXU_DTYPE), b.astype(MXU_DTYPE), (((ca,), (cb,)), ((), ())),
                           preferred_element_type=F32)


@jax.custom_vjp
def mm(a, b):
    return _dot(a, b, 1, 0)


def _mm_f(a, b):
    return mm(a, b), (a, b)


def _mm_b(res, g):
    a, b = res
    return mm_nt(g, b).astype(a.dtype), mm_tn(a, g).astype(b.dtype)


@jax.custom_vjp
def mm_nt(a, b):
    return _dot(a, b, 1, 1)


def _mm_nt_f(a, b):
    return mm_nt(a, b), (a, b)


def _mm_nt_b(res, g):
    a, b = res
    return mm(g, b).astype(a.dtype), mm_tn(g, a).astype(b.dtype)


@jax.custom_vjp
def mm_tn(a, b):
    return _dot(a, b, 0, 0)


def _mm_tn_f(a, b):
    return mm_tn(a, b), (a, b)


def _mm_tn_b(res, g):
    a, b = res
    return mm_nt(b, g).astype(a.dtype), mm(a, g).astype(b.dtype)


mm.defvjp(_mm_f, _mm_b)
mm_nt.defvjp(_mm_nt_f, _mm_nt_b)
mm_tn.defvjp(_mm_tn_f, _mm_tn_b)


def _split3(a):
    h = a.astype(BF16)
    r = a - h.astype(F32)
    m = r.astype(BF16)
    l = (r - m.astype(F32)).astype(BF16)
    return h, m, l


def _exact_dot(t, a, ca, cb):
    out = None
    for p in _split3(a):
        d = lax.dot_general(t, p, (((ca,), (cb,)), ((), ())), preferred_element_type=F32)
        out = d if out is None else out + d
    return out


@jax.custom_vjp
def sel_l(t, a):
    return _exact_dot(t, a, 1, 0)


def _sel_l_f(t, a):
    return sel_l(t, a), t


def _sel_l_b(t, g):
    return jnp.zeros_like(t), _exact_dot(t, g, 0, 0)


sel_l.defvjp(_sel_l_f, _sel_l_b)


@jax.custom_vjp
def sel_r(a, t):
    out = None
    for p in _split3(a):
        d = lax.dot_general(p, t, (((1,), (0,)), ((), ())), preferred_element_type=F32)
        out = d if out is None else out + d
    return out


def _sel_r_f(a, t):
    return sel_r(a, t), t


def _sel_r_b(t, g):
    out = None
    for p in _split3(g):
        d = lax.dot_general(p, t, (((1,), (1,)), ((), ())), preferred_element_type=F32)
        out = d if out is None else out + d
    return out, jnp.zeros_like(t)


sel_r.defvjp(_sel_r_f, _sel_r_b)


def _sigmoid(x):
    return 1.0 / (1.0 + jnp.exp(-x))


def _silu(x):
    return x * _sigmoid(x)


def _softplus(x):
    return jnp.maximum(x, 0.0) + jnp.log(1.0 + jnp.exp(-jnp.abs(x)))


def _log_sigmoid(x):
    return -_softplus(-x)


def _gelu_tanh(x):
    c = math.sqrt(2.0 / math.pi)
    return 0.5 * x * (1.0 + jnp.tanh(c * (x + 0.044715 * (x * x * x))))


def _rms(x, w):
    return x * lax.rsqrt(jnp.mean(x * x, axis=-1, keepdims=True) + NORM_EPS) * w


def _tri(n, dtype=BF16):
    r = lax.broadcasted_iota(jnp.int32, (n, n), 0)
    c = lax.broadcasted_iota(jnp.int32, (n, n), 1)
    return (c <= r).astype(dtype)


def matmul(a, b, *, ta=False, tb=False, add=None, out_dtype=F32, out_shards=False, carry=None, name,
           epi=None, epi_in=(), epi_out=(), tm_cap=1024):
    m, k = (a.shape[1], a.shape[0]) if ta else a.shape
    b_sh = b.ndim == 3
    if b_sh:
        s, br, bc = b.shape
        k2, n = (s * bc, br) if tb else (br, s * bc)
    else:
        k2, n = (b.shape[1], b.shape[0]) if tb else b.shape
    assert k == k2, (a.shape, b.shape, ta, tb)
    tm = _pick(m, tuple(c for c in (1024, 512, 256, 128) if c <= tm_cap))
    tn = _pick(n, (1024, 512, 256, 128))
    tk = k if k <= MAX_TK else max(c for c in range(LANE, MAX_TK + 1, LANE) if k % c == 0)
    if b_sh and tb:
        tk = bc
    elif b_sh:
        tn = bc
    if out_shards:
        tn = n // 4
    nk = k // tk
    n_add, n_x = int(add is not None), len(epi_in)
    out_dtypes = list(epi_out) if epi is not None else [out_dtype]
    n_o = len(out_dtypes)

    def body(*refs):
        a_ref, b_ref = refs[:2]
        x_refs = refs[2 + n_add:2 + n_add + n_x]
        o_refs = refs[2 + n_add + n_x:2 + n_add + n_x + n_o]
        acc = refs[-1]
        kk = pl.program_id(2)

        @pl.when(kk == 0)
        def _():
            acc[...] = jnp.zeros_like(acc)

        acc[...] += _dot(a_ref[...], b_ref[...], 0 if ta else 1, 1 if tb else 0)

        @pl.when(kk == nk - 1)
        def _():
            r = acc[...]
            if add is not None:
                r = r + refs[2][...].astype(F32)
            vals = (r,) if epi is None else epi(r, *[x[...] for x in x_refs])
            for o_ref, v in zip(o_refs, vals):
                o_ref[...] = v.astype(o_ref.dtype)

    a_spec = pl.BlockSpec((tk, tm), lambda i, j, kk: (kk, i)) if ta else pl.BlockSpec((tm, tk), lambda i, j, kk: (i, kk))
    if b_sh and tb:
        b_spec = pl.BlockSpec((None, tn, tk), lambda i, j, kk: (kk, j, 0))
    elif b_sh:
        b_spec = pl.BlockSpec((None, tk, tn), lambda i, j, kk: (j, kk, 0))
    elif tb:
        b_spec = pl.BlockSpec((tn, tk), lambda i, j, kk: (j, kk))
    else:
        b_spec = pl.BlockSpec((tk, tn), lambda i, j, kk: (kk, j))
    in_specs, args = [a_spec, b_spec], [a, b]
    tile = pl.BlockSpec((tm, tn), lambda i, j, kk: (i, j))
    for extra in ([add] if add is not None else []) + list(epi_in):
        in_specs.append(tile)
        args.append(extra)
    if out_shards:
        out_spec = pl.BlockSpec((None, tm, tn), lambda i, j, kk: (j, i, 0))
        out_shape = jax.ShapeDtypeStruct((4, m, tn), out_dtype)
    elif epi is not None:
        out_spec = [tile] * n_o
        out_shape = [jax.ShapeDtypeStruct((m, n), dt) for dt in out_dtypes]
    else:
        out_spec = tile
        out_shape = jax.ShapeDtypeStruct((m, n), out_dtype)
    return _pcall(
        body, name=name, grid=(m // tm, n // tn, nk), in_specs=in_specs, out_specs=out_spec, out_shape=out_shape,
        scratch_shapes=[pltpu.VMEM((tm, tn), F32)],
        compiler_params=_cparams(("parallel", "parallel", "arbitrary")), carry=carry,
    )(*args)


def rms_fwd(x, w, *, name):
    t, d = x.shape
    tb = _pick(t, (256, 128, 64))

    def body(x_ref, w_ref, o_ref):
        o_ref[...] = _rms(x_ref[...], w_ref[...]).astype(o_ref.dtype)

    return pl.pallas_call(
        body, name=name, grid=(t // tb,),
        in_specs=[pl.BlockSpec((tb, d), lambda i: (i, 0)), _full((1, d))],
        out_specs=pl.BlockSpec((tb, d), lambda i: (i, 0)),
        out_shape=jax.ShapeDtypeStruct((t, d), BF16),
        compiler_params=_cparams(("parallel",)),
    )(x, w)


def rms_bwd(x, w, dh, dres, *, name, carry=None):
    t, d = x.shape
    tb = _pick(t, (256, 128, 64))

    def body(x_ref, w_ref, dh_ref, dres_ref, dx_ref, dw_ref):
        @pl.when(pl.program_id(0) == 0)
        def _():
            dw_ref[...] = jnp.zeros_like(dw_ref)

        _, vjp = jax.vjp(_rms, x_ref[...], w_ref[...])
        dx, dw = vjp(dh_ref[...].astype(F32))
        dx_ref[...] = dx + dres_ref[...]
        dw_ref[...] += dw

    row = pl.BlockSpec((tb, d), lambda i: (i, 0))
    return _pcall(
        body, name=name, grid=(t // tb,), carry=carry,
        in_specs=[row, _full((1, d)), row, row],
        out_specs=[row, _full((1, d))],
        out_shape=[jax.ShapeDtypeStruct((t, d), F32), jax.ShapeDtypeStruct((1, d), F32)],
        compiler_params=_cparams(("arbitrary",)),
    )(x, w, dh, dres)


def _swi(g, u):
    return _silu(g) * u


def _swi_fwd_epi(u, g):
    return u, _swi(g, u)


def _swi_bwd_epi(d, g, u):
    return jax.vjp(_swi, g, u)[1](d)


def loss_head(x, w, target, *, name):
    t, d = x.shape
    tb = _pick(t, (256, 128, 64))

    def f(xv, wv, tv):
        y = _rms(xv, wv)
        e = y - tv
        return 0.5 * jnp.sum(jnp.mean(e * e, axis=-1, keepdims=True), axis=0, keepdims=True)

    def body(x_ref, w_ref, t_ref, l_ref, dx_ref, dw_ref):
        @pl.when(pl.program_id(0) == 0)
        def _():
            l_ref[...] = jnp.zeros_like(l_ref)
            dw_ref[...] = jnp.zeros_like(dw_ref)

        val, vjp = jax.vjp(lambda a, b: f(a, b, t_ref[...]), x_ref[...], w_ref[...])
        dx, dw = vjp(jnp.ones((1, 1), F32))
        l_ref[...] += jnp.broadcast_to(val, l_ref.shape)
        dx_ref[...] = dx
        dw_ref[...] += dw

    row = pl.BlockSpec((tb, d), lambda i: (i, 0))
    return pl.pallas_call(
        body, name=name, grid=(t // tb,),
        in_specs=[row, _full((1, d)), row],
        out_specs=[_full((SUBLANE, LANE)), row, _full((1, d))],
        out_shape=[jax.ShapeDtypeStruct((SUBLANE, LANE), F32), jax.ShapeDtypeStruct((t, d), F32),
                   jax.ShapeDtypeStruct((1, d), F32)],
        compiler_params=_cparams(("arbitrary",)),
    )(x, w, target)


def _gla_chunk(q, k, v, g, glr, st, wg, bg, wn, tri):
    L, hk = q.shape
    la = _log_sigmoid(mm(glr, wg) + bg) / GLA_GATE_NORM
    bcum = sel_l(tri, la)
    b_last = jnp.sum(la, axis=0, keepdims=True)
    rows = lax.broadcasted_iota(jnp.int32, (L, 1), 0)
    b_mid = jnp.sum(jnp.where(rows <= L // 2, la, 0.0), axis=0, keepdims=True)
    qs = q * (hk ** -0.5)
    q_in = qs * jnp.exp(bcum - b_mid)
    k_in = k * jnp.exp(b_mid - bcum)
    scores = mm_nt(q_in, k_in) * tri.astype(F32)
    o_intra = mm(scores, v)
    k_st = k * jnp.exp(b_last - bcum)
    d_st = mm_tn(v, k_st)
    o_inter = mm_nt(qs * jnp.exp(bcum), st)
    st_new = jnp.exp(b_last) * st + d_st
    o = _rms(o_intra + o_inter, wn) * _silu(g)
    return o, st_new


def _gla_dims(d):
    dv = d // 2
    dk = dv // 2
    return dk, dv, dk // GLA_HEADS, dv // GLA_HEADS


def gla_fwd(proj, glr_col, wg, bg, wn, dv, carry=None):
    t = proj.shape[0]
    dk, dv, hk, hv = _gla_dims(2 * dv)
    L, H = CHUNK, GLA_HEADS
    nc = t // L
    wq = 2 * dk + 2 * dv

    def body(p_ref, glr_ref, wg_ref, bg_ref, wn_ref, o_ref, sp_ref, st):
        @pl.when(pl.program_id(0) == 0)
        def _():
            st[...] = jnp.zeros_like(st)

        tri = _tri(L)
        glr = glr_ref[...]
        for h in range(H):
            q = p_ref[:, h * hk:(h + 1) * hk]
            k = p_ref[:, dk + h * hk:dk + (h + 1) * hk]
            v = p_ref[:, 2 * dk + h * hv:2 * dk + (h + 1) * hv]
            g = p_ref[:, 2 * dk + dv + h * hv:2 * dk + dv + (h + 1) * hv]
            s_prev = st[h]
            sp_ref[0, h] = s_prev
            o, s_new = _gla_chunk(q, k, v, g, glr, s_prev, wg_ref[:, h * hk:(h + 1) * hk],
                                  bg_ref[:, h * hk:(h + 1) * hk], wn_ref[...], tri)
            o_ref[:, h * hv:(h + 1) * hv] = o.astype(o_ref.dtype)
            st[h] = s_new

    return _pcall(
        body, carry=carry, name="gla_fwd", grid=(nc,),
        in_specs=[pl.BlockSpec((L, wq), lambda c: (c, 0)), pl.BlockSpec((L, LANE), lambda c: (c, glr_col)),
                  _full(wg.shape), _full(bg.shape), _full(wn.shape)],
        out_specs=[pl.BlockSpec((L, dv), lambda c: (c, 0)), pl.BlockSpec((1, H, hv, hk), lambda c: (c, 0, 0, 0))],
        out_shape=[jax.ShapeDtypeStruct((t, dv), BF16), jax.ShapeDtypeStruct((nc, H, hv, hk), F32)],
        scratch_shapes=[pltpu.VMEM((H, hv, hk), F32)],
        compiler_params=_cparams(("arbitrary",)),
    )(proj, proj, wg, bg, wn)


def gla_bwd(proj, glr_col, wg, bg, wn, sprev, do, dv, gla_w, carry=None):
    t = proj.shape[0]
    dk, _, hk, hv = _gla_dims(2 * dv)
    L, H = CHUNK, GLA_HEADS
    nc = t // L
    wq = 2 * dk + 2 * dv

    def body(p_ref, glr_ref, wg_ref, bg_ref, wn_ref, sp_ref, do_ref, dp_ref, dwg_ref, dbg_ref, dwn_ref, dst):
        @pl.when(pl.program_id(0) == 0)
        def _():
            dst[...] = jnp.zeros_like(dst)
            dwg_ref[...] = jnp.zeros_like(dwg_ref)
            dbg_ref[...] = jnp.zeros_like(dbg_ref)
            dwn_ref[...] = jnp.zeros_like(dwn_ref)

        tri = _tri(L)
        glr = glr_ref[...]
        dglr = jnp.zeros_like(glr)
        for h in range(H):
            ks = slice(h * hk, (h + 1) * hk)
            q = p_ref[:, ks]
            k = p_ref[:, dk + h * hk:dk + (h + 1) * hk]
            v = p_ref[:, 2 * dk + h * hv:2 * dk + (h + 1) * hv]
            g = p_ref[:, 2 * dk + dv + h * hv:2 * dk + dv + (h + 1) * hv]
            f = functools.partial(_gla_chunk, tri=tri)
            _, vjp = jax.vjp(f, q, k, v, g, glr, sp_ref[0, h], wg_ref[:, ks], bg_ref[:, ks], wn_ref[...])
            dq, dkk, dvv, dg, dgl, ds, dwg, dbg, dwn = vjp((do_ref[:, h * hv:(h + 1) * hv], dst[h]))
            dp_ref[:, ks] = dq.astype(dp_ref.dtype)
            dp_ref[:, dk + h * hk:dk + (h + 1) * hk] = dkk.astype(dp_ref.dtype)
            dp_ref[:, 2 * dk + h * hv:2 * dk + (h + 1) * hv] = dvv.astype(dp_ref.dtype)
            dp_ref[:, 2 * dk + dv + h * hv:2 * dk + dv + (h + 1) * hv] = dg.astype(dp_ref.dtype)
            dglr = dglr + dgl
            dst[h] = ds
            dwg_ref[:, ks] += dwg
            dbg_ref[:, ks] += dbg
            dwn_ref[...] += dwn
        dp_ref[:, wq:wq + LANE] = dglr.astype(dp_ref.dtype)
        if gla_w > wq + LANE:
            dp_ref[:, wq + LANE:] = jnp.zeros((L, gla_w - wq - LANE), dp_ref.dtype)

    rev = lambda c: nc - 1 - c
    return _pcall(
        body, carry=carry, name="gla_bwd", grid=(nc,),
        in_specs=[pl.BlockSpec((L, wq), lambda c: (rev(c), 0)), pl.BlockSpec((L, LANE), lambda c: (rev(c), glr_col)),
                  _full(wg.shape), _full(bg.shape), _full(wn.shape),
                  pl.BlockSpec((1, H, hv, hk), lambda c: (rev(c), 0, 0, 0)),
                  pl.BlockSpec((L, dv), lambda c: (rev(c), 0))],
        out_specs=[pl.BlockSpec((L, gla_w), lambda c: (rev(c), 0)),
                   _full(wg.shape), _full(bg.shape), _full(wn.shape)],
        out_shape=[jax.ShapeDtypeStruct(proj.shape, BF16),
                   jax.ShapeDtypeStruct(wg.shape, F32), jax.ShapeDtypeStruct(bg.shape, F32),
                   jax.ShapeDtypeStruct(wn.shape, F32)],
        scratch_shapes=[pltpu.VMEM((H, hv, hk), F32)],
        compiler_params=_cparams(("arbitrary",)),
    )(proj, proj, wg, bg, wn, sprev, do)


def _shift_down(x, tail, s):
    if s == 0:
        return x
    r = pltpu.roll(x, s, 0)
    rows = lax.broadcasted_iota(jnp.int32, tail.shape, 0)
    top = jnp.where(rows < s, pltpu.roll(tail, s, 0), r[:SUBLANE])
    return jnp.concatenate([top, r[SUBLANE:]], axis=0)


def _shift_up(x, head, s):
    if s == 0:
        return x
    n = x.shape[0]
    r = pltpu.roll(x, n - s, 0)
    rows = lax.broadcasted_iota(jnp.int32, head.shape, 0)
    bottom = jnp.where(rows >= SUBLANE - s, pltpu.roll(head, SUBLANE - s, 0), r[n - SUBLANE:])
    return jnp.concatenate([r[:n - SUBLANE], bottom], axis=0)


def _conv(x, prev, w, b):
    y = b
    for k in range(CONV_WIDTH):
        y = y + w[k:k + 1, :] * _shift_down(x, prev, CONV_WIDTH - 1 - k)
    return y


def _conv_bwd(dy, nxt, x, prev, w):
    dx = None
    dws = []
    for k in range(CONV_WIDTH):
        s = CONV_WIDTH - 1 - k
        term = w[k:k + 1, :] * _shift_up(dy, nxt, s)
        dx = term if dx is None else dx + term
        dws.append(jnp.sum(dy * _shift_down(x, prev, s), axis=0, keepdims=True))
    return dx, jnp.concatenate(dws, axis=0), jnp.sum(dy, axis=0, keepdims=True)


def _scan_fwd(a, u):
    n = a.shape[0]
    rows = lax.broadcasted_iota(jnp.int32, a.shape, 0)
    s = 1
    while s < n:
        a_sh = jnp.where(rows < s, 1.0, pltpu.roll(a, s, 0))
        u_sh = jnp.where(rows < s, 0.0, pltpu.roll(u, s, 0))
        u = a * u_sh + u
        a = a * a_sh
        s *= 2
    return a, u


def _scan_rev(c, d):
    n = c.shape[0]
    rows = lax.broadcasted_iota(jnp.int32, c.shape, 0)
    s = 1
    while s < n:
        c_sh = jnp.where(rows >= n - s, 0.0, pltpu.roll(c, n - s, 0))
        d_sh = jnp.where(rows >= n - s, 0.0, pltpu.roll(d, n - s, 0))
        d = d + c * d_sh
        c = c * c_sh
        s *= 2
    return d


def _expm1(x):
    small = x * (1.0 + x * (0.5 + x * (1.0 / 6.0 + x * (1.0 / 24.0))))
    return jnp.where(jnp.abs(x) < 1e-2, small, jnp.exp(x) - 1.0)


def _lru_gates(xc, pa, pi, lam):
    r = _sigmoid(pa)
    i = _sigmoid(pi)
    log_a = LRU_C * r * _log_sigmoid(lam)
    a = jnp.exp(log_a)
    u = jnp.sqrt(-_expm1(2.0 * log_a)) * (i * xc)
    return a, u


def _lru_out(h, gate):
    return h * _gelu_tanh(gate)


def _blockdiag(xc, w_ref, b):
    nb = w_ref.shape[0]
    outs = [mm(xc[:, n * LRU_BLOCK:(n + 1) * LRU_BLOCK], w_ref[n]) for n in range(nb)]
    return jnp.concatenate(outs, axis=1) + b


def lru_fwd(proj, xcol, lw, cw, cb, wa, ba, wi, bi, lam):
    t = proj.shape[0]
    tb = _pick(t, (256, 128, 64))
    nb = t // tb

    def body(x_ref, xp_ref, g_ref, cw_ref, cb_ref, wa_ref, ba_ref, wi_ref, bi_ref, lam_ref, o_ref, hin_ref, hc):
        i = pl.program_id(0)

        @pl.when(i == 0)
        def _():
            hc[...] = jnp.zeros_like(hc)

        prev = jnp.where(i == 0, 0.0, xp_ref[...])
        xc = _conv(x_ref[...], prev, cw_ref[...], cb_ref[...])
        a, u = _lru_gates(xc, _blockdiag(xc, wa_ref, ba_ref[...]), _blockdiag(xc, wi_ref, bi_ref[...]), lam_ref[...])
        acum, h0 = _scan_fwd(a, u)
        h = h0 + acum * hc[...]
        hin_ref[0] = hc[...]
        hc[...] = h[tb - 1:tb, :]
        o_ref[...] = _lru_out(h, g_ref[...]).astype(o_ref.dtype)

    row = lambda col: pl.BlockSpec((tb, lw), lambda i: (i, col))
    return pl.pallas_call(
        body, name="lru_fwd", grid=(nb,),
        in_specs=[row(xcol), pl.BlockSpec((SUBLANE, lw), lambda i: (jnp.maximum(i * (tb // SUBLANE) - 1, 0), xcol)),
                  row(xcol + 1),
                  _full(cw.shape), _full(cb.shape), _full(wa.shape), _full(ba.shape), _full(wi.shape), _full(bi.shape),
                  _full(lam.shape)],
        out_specs=[pl.BlockSpec((tb, lw), lambda i: (i, 0)), pl.BlockSpec((1, 1, lw), lambda i: (i, 0, 0))],
        out_shape=[jax.ShapeDtypeStruct((t, lw), BF16), jax.ShapeDtypeStruct((nb, 1, lw), F32)],
        scratch_shapes=[pltpu.VMEM((1, lw), F32)],
        compiler_params=_cparams(("arbitrary",)),
    )(proj, proj, proj, cw, cb, wa, ba, wi, bi, lam)


def lru_bwd(proj, xcol, lw, cw, cb, wa, ba, wi, bi, lam, hin, dmix, docol, dproj, carry=None):
    t = proj.shape[0]
    tb = _pick(t, (256, 128, 64))
    nb = t // tb
    nblk = wa.shape[0]
    assert xcol % 2 == 0

    def body(x_ref, xp_ref, g_ref, cw_ref, cb_ref, wa_ref, ba_ref, wi_ref, bi_ref, lam_ref, hin_ref, do_ref, _,
             dxg_ref, dcw_ref, dcb_ref, dwa_ref, dba_ref, dwi_ref, dbi_ref, dlam_ref, gc, dxcn):
        pid = pl.program_id(0)
        i = nb - 1 - pid

        @pl.when(pid == 0)
        def _():
            gc[...] = jnp.zeros_like(gc)
            dxcn[...] = jnp.zeros_like(dxcn)
            for r in (dcw_ref, dcb_ref, dwa_ref, dba_ref, dwi_ref, dbi_ref, dlam_ref):
                r[...] = jnp.zeros_like(r)

        x = x_ref[...]
        prev = jnp.where(i == 0, 0.0, xp_ref[...])
        cw_v = cw_ref[...]
        xc = _conv(x, prev, cw_v, cb_ref[...])
        pa = _blockdiag(xc, wa_ref, ba_ref[...])
        pi = _blockdiag(xc, wi_ref, bi_ref[...])
        (a, u), vjp_g = jax.vjp(_lru_gates, xc, pa, pi, lam_ref[...])
        acum, h0 = _scan_fwd(a, u)
        hi = hin_ref[0]
        h = h0 + acum * hi
        rows = lax.broadcasted_iota(jnp.int32, h.shape, 0)
        hprev = jnp.where(rows < 1, hi, pltpu.roll(h, 1, 0))
        _, vjp_o = jax.vjp(_lru_out, h, g_ref[...])
        dh, dgate = vjp_o(do_ref[...].astype(F32))
        c = jnp.where(rows >= tb - 1, 0.0, pltpu.roll(a, tb - 1, 0))
        g = _scan_rev(c, dh + jnp.where(rows == tb - 1, gc[...], 0.0))
        gc[...] = a[0:1, :] * g[0:1, :]
        dxc, dpa, dpi, dlam = vjp_g((g * hprev, g))
        dlam_ref[...] += dlam
        dba_ref[...] += jnp.sum(dpa, axis=0, keepdims=True)
        dbi_ref[...] += jnp.sum(dpi, axis=0, keepdims=True)
        parts = []
        for n in range(nblk):
            sl = slice(n * LRU_BLOCK, (n + 1) * LRU_BLOCK)
            dwa_ref[n] += mm_tn(xc[:, sl], dpa[:, sl])
            dwi_ref[n] += mm_tn(xc[:, sl], dpi[:, sl])
            parts.append(mm_nt(dpa[:, sl], wa_ref[n]) + mm_nt(dpi[:, sl], wi_ref[n]))
        dxc = dxc + jnp.concatenate(parts, axis=1)
        dx, dcw, dcb = _conv_bwd(dxc, dxcn[...], x, prev, cw_v)
        dxcn[...] = dxc[:SUBLANE]
        dcw_ref[...] += dcw
        dcb_ref[...] += dcb
        dxg_ref[:, :lw] = dx.astype(dxg_ref.dtype)
        dxg_ref[:, lw:] = dgate.astype(dxg_ref.dtype)

    row = lambda col: pl.BlockSpec((tb, lw), lambda p: (nb - 1 - p, col))
    params = [cw, cb, wa, ba, wi, bi, lam]
    return _pcall(
        body, carry=carry, name="lru_bwd", grid=(nb,),
        in_specs=[row(xcol),
                  pl.BlockSpec((SUBLANE, lw), lambda p: (jnp.maximum((nb - 1 - p) * (tb // SUBLANE) - 1, 0), xcol)),
                  row(xcol + 1)]
        + [_full(p.shape) for p in params]
        + [pl.BlockSpec((1, 1, lw), lambda p: (nb - 1 - p, 0, 0)), row(docol), ANY],
        out_specs=[pl.BlockSpec((tb, 2 * lw), lambda p: (nb - 1 - p, xcol // 2))] + [_full(p.shape) for p in params],
        out_shape=[jax.ShapeDtypeStruct(dproj.shape, dproj.dtype)]
        + [jax.ShapeDtypeStruct(p.shape, F32) for p in params],
        input_output_aliases={12: 0},
        scratch_shapes=[pltpu.VMEM((1, lw), F32), pltpu.VMEM((SUBLANE, lw), F32)],
        compiler_params=_cparams(("arbitrary",)),
    )(proj, proj, proj, *params, hin, dmix, dproj)


def conv_silu_fwd(proj, col0, width, cw, cb, carry=None):
    t = proj.shape[0]
    tb = _pick(t, (512, 256, 128, 64))
    cbw = _pick(width, (512, 256, 128))
    off = col0 // cbw
    assert col0 % cbw == 0

    def body(x_ref, xp_ref, w_ref, b_ref, o_ref):
        prev = jnp.where(pl.program_id(1) == 0, 0.0, xp_ref[...])
        o_ref[...] = _silu(_conv(x_ref[...], prev, w_ref[...], b_ref[...]))

    return _pcall(
        body, carry=carry, name="conv_silu_fwd", grid=(width // cbw, t // tb),
        in_specs=[pl.BlockSpec((tb, cbw), lambda j, i: (i, off + j)),
                  pl.BlockSpec((SUBLANE, cbw), lambda j, i: (jnp.maximum(i * (tb // SUBLANE) - 1, 0), off + j)),
                  pl.BlockSpec((CONV_WIDTH, cbw), lambda j, i: (0, j)), pl.BlockSpec((1, cbw), lambda j, i: (0, j))],
        out_specs=pl.BlockSpec((tb, cbw), lambda j, i: (i, j)),
        out_shape=jax.ShapeDtypeStruct((t, width), F32),
        compiler_params=_cparams(("parallel", "arbitrary")),
    )(proj, proj, cw, cb)


def conv_silu_bwd(proj, col0, width, cw, cb, dact, dproj, carry=None):
    t = proj.shape[0]
    tb = _pick(t, (512, 256, 128, 64))
    nb = t // tb
    cbw = _pick(width, (512, 256, 128))
    off = col0 // cbw

    def body(x_ref, xp_ref, w_ref, b_ref, d_ref, _, dx_ref, dw_ref, db_ref, nxt):
        pid = pl.program_id(1)
        i = nb - 1 - pid

        @pl.when(pid == 0)
        def _():
            nxt[...] = jnp.zeros_like(nxt)
            dw_ref[...] = jnp.zeros_like(dw_ref)
            db_ref[...] = jnp.zeros_like(db_ref)

        x = x_ref[...]
        prev = jnp.where(i == 0, 0.0, xp_ref[...])
        w = w_ref[...]
        _, vjp = jax.vjp(_silu, _conv(x, prev, w, b_ref[...]))
        (dcv,) = vjp(d_ref[...])
        dx, dw, db = _conv_bwd(dcv, nxt[...], x, prev, w)
        nxt[...] = dcv[:SUBLANE]
        dx_ref[...] = dx.astype(dx_ref.dtype)
        dw_ref[...] += dw
        db_ref[...] += db

    return _pcall(
        body, carry=carry, name="conv_silu_bwd", grid=(width // cbw, nb),
        in_specs=[pl.BlockSpec((tb, cbw), lambda j, p: (nb - 1 - p, off + j)),
                  pl.BlockSpec((SUBLANE, cbw),
                               lambda j, p: (jnp.maximum((nb - 1 - p) * (tb // SUBLANE) - 1, 0), off + j)),
                  pl.BlockSpec((CONV_WIDTH, cbw), lambda j, p: (0, j)), pl.BlockSpec((1, cbw), lambda j, p: (0, j)),
                  pl.BlockSpec((tb, cbw), lambda j, p: (nb - 1 - p, j)), ANY],
        out_specs=[pl.BlockSpec((tb, cbw), lambda j, p: (nb - 1 - p, off + j)),
                   pl.BlockSpec((CONV_WIDTH, cbw), lambda j, p: (0, j)), pl.BlockSpec((1, cbw), lambda j, p: (0, j))],
        out_shape=[jax.ShapeDtypeStruct(dproj.shape, dproj.dtype), jax.ShapeDtypeStruct(cw.shape, F32),
                   jax.ShapeDtypeStruct(cb.shape, F32)],
        scratch_shapes=[pltpu.VMEM((SUBLANE, cbw), F32)],
        input_output_aliases={5: 0},
        compiler_params=_cparams(("parallel", "arbitrary")),
    )(proj, proj, cw, cb, dact, dproj)


def _dt_expand(raw, bias, e):
    return sel_r(_softplus(raw + bias), e)


def dt_fwd(proj, dtcol, bias, e):
    t = proj.shape[0]
    di = e.shape[1]
    tb = _pick(t, (512, 256, 128, 64))

    def body(r_ref, b_ref, e_ref, o_ref):
        o_ref[...] = _dt_expand(r_ref[...], b_ref[...], e_ref[...])

    return pl.pallas_call(
        body, name="dt_fwd", grid=(t // tb,),
        in_specs=[pl.BlockSpec((tb, LANE), lambda i: (i, dtcol)), _full(bias.shape), _full(e.shape)],
        out_specs=pl.BlockSpec((tb, di), lambda i: (i, 0)),
        out_shape=jax.ShapeDtypeStruct((t, di), F32),
        compiler_params=_cparams(("parallel",)),
    )(proj, bias, e)


def dt_bwd(proj, dtcol, bias, e, ddte, dproj):
    t = proj.shape[0]
    di = e.shape[1]
    tb = _pick(t, (512, 256, 128, 64))
    tail = dproj.shape[1] - dtcol * LANE
    assert (dtcol * LANE) % tail == 0

    def body(r_ref, b_ref, e_ref, d_ref, _, dr_ref, db_ref):
        @pl.when(pl.program_id(0) == 0)
        def _():
            db_ref[...] = jnp.zeros_like(db_ref)

        e_v = e_ref[...]
        _, vjp = jax.vjp(lambda r, b: _dt_expand(r, b, e_v), r_ref[...], b_ref[...])
        dr, db = vjp(d_ref[...])
        dr_ref[:, :LANE] = dr.astype(dr_ref.dtype)
        if tail > LANE:
            dr_ref[:, LANE:] = jnp.zeros((tb, tail - LANE), dr_ref.dtype)
        db_ref[...] += db

    return pl.pallas_call(
        body, name="dt_bwd", grid=(t // tb,),
        in_specs=[pl.BlockSpec((tb, LANE), lambda i: (i, dtcol)), _full(bias.shape), _full(e.shape),
                  pl.BlockSpec((tb, di), lambda i: (i, 0)), ANY],
        out_specs=[pl.BlockSpec((tb, tail), lambda i: (i, dtcol * LANE // tail)), _full(bias.shape)],
        out_shape=[jax.ShapeDtypeStruct(dproj.shape, dproj.dtype), jax.ShapeDtypeStruct(bias.shape, F32)],
        input_output_aliases={4: 0},
        compiler_params=_cparams(("arbitrary",)),
    )(proj, bias, e, ddte, dproj)


def head_expand(p, e, *, transpose=False, name):
    di = e.shape[1]

    def body(p_ref, e_ref, o_ref):
        if transpose:
            o_ref[...] = _sel_r_b(e_ref[...], p_ref[...])[0]
        else:
            o_ref[...] = sel_r(p_ref[...], e_ref[...])

    oshape = (SUBLANE, LANE) if transpose else (SUBLANE, di)
    return pl.pallas_call(
        body, name=name, in_specs=[_full(p.shape), _full(e.shape)], out_specs=_full(oshape),
        out_shape=jax.ShapeDtypeStruct(oshape, F32), compiler_params=_cparams(None), grid=(1,),
    )(p, e)


def _ssd_chunk(x, z, bm, cm, dte, st, alog, dskip, gn, tri, cmask, dmask, bd):
    L, gw = x.shape
    reps = gw // L
    a = dte * (-jnp.exp(alog))
    acs = sel_l(tri, a)
    acs_last = jnp.sum(a, axis=0, keepdims=True)
    arow = jnp.sum(acs * dmask, axis=0, keepdims=True)
    dtrow = jnp.sum(dte * dmask, axis=0, keepdims=True)
    cb = mm_nt(cm, jnp.concatenate([bm] * reps, axis=0))
    wts = cb * (jnp.exp(jnp.minimum(acs - arow, 0.0)) * cmask) * dtrow
    xbd = jnp.concatenate([x] * reps, axis=0) * bd
    xw = x * (jnp.exp(acs_last - acs) * dte)
    y = mm(wts, xbd) + mm(cm, st) * jnp.exp(acs) + dskip * x
    st_new = jnp.exp(acs_last) * st + mm_tn(bm, xw)
    return _rms(y * _silu(z), gn), st_new


def _ssd_dims(di):
    gw = di // SSD_GROUPS
    assert CHUNK == SSD_HEAD_DIM and gw % LANE == 0
    return gw, SSD_STATE


def _ssd_masks(gw):
    L = CHUNK
    r = jnp.arange(L)[:, None]
    c = jnp.arange(gw)[None, :]
    cmask = ((c % L) <= r).astype(F32)
    dmask = ((c % L) == r).astype(F32)
    rr = jnp.arange(gw)
    bd = ((rr[:, None] // L) == (rr[None, :] // L)).astype(F32)
    tri = (jnp.arange(L)[None, :] <= jnp.arange(L)[:, None]).astype(BF16)
    return tri, cmask, dmask, bd


def ssd_fwd(xs, proj, dte, alog_e, dskip_e, gn, carry=None):
    t, di = dte.shape
    gw, n = _ssd_dims(di)
    L, G = CHUNK, SSD_GROUPS
    nc = t // L
    masks = _ssd_masks(gw)
    cdim = xs.shape[1]

    def body(x_ref, z_ref, dt_ref, al_ref, ds_ref, gn_ref, tri_ref, cm_ref, dm_ref, bd_ref, y_ref, sp_ref, st):
        @pl.when(pl.program_id(0) == 0)
        def _():
            st[...] = jnp.zeros_like(st)

        for g in range(G):
            ch = slice(g * gw, (g + 1) * gw)
            s_prev = st[g]
            sp_ref[0, g] = s_prev
            y, s_new = _ssd_chunk(x_ref[:, ch], z_ref[:, ch], x_ref[:, di + g * n:di + (g + 1) * n],
                                  x_ref[:, di + (G + g) * n:di + (G + g + 1) * n], dt_ref[:, ch], s_prev,
                                  al_ref[0:1, ch], ds_ref[0:1, ch], gn_ref[:, ch], tri_ref[...], cm_ref[...],
                                  dm_ref[...], bd_ref[...])
            y_ref[:, ch] = y.astype(y_ref.dtype)
            st[g] = s_new

    row = lambda w: pl.BlockSpec((L, w), lambda c: (c, 0))
    return _pcall(
        body, carry=carry, name="ssd_fwd", grid=(nc,),
        in_specs=[row(cdim), row(di), row(di), _full(alog_e.shape), _full(dskip_e.shape), _full(gn.shape)]
        + [_full(m.shape) for m in masks],
        out_specs=[row(di), pl.BlockSpec((1, G, n, gw), lambda c: (c, 0, 0, 0))],
        out_shape=[jax.ShapeDtypeStruct((t, di), BF16), jax.ShapeDtypeStruct((nc, G, n, gw), F32)],
        scratch_shapes=[pltpu.VMEM((G, n, gw), F32)],
        compiler_params=_cparams(("arbitrary",)),
    )(xs, proj, dte, alog_e, dskip_e, gn, *masks)


def ssd_bwd(xs, proj, dte, alog_e, dskip_e, gn, sprev, dy, dproj_shape, carry=None):
    t, di = dte.shape
    gw, n = _ssd_dims(di)
    L, G = CHUNK, SSD_GROUPS
    nc = t // L
    masks = _ssd_masks(gw)
    cdim = xs.shape[1]

    def body(x_ref, z_ref, dt_ref, al_ref, ds_ref, gn_ref, tri_ref, cm_ref, dm_ref, bd_ref, sp_ref, dy_ref,
             dxs_ref, dz_ref, ddt_ref, dal_ref, dds_ref, dgn_ref, dst):
        @pl.when(pl.program_id(0) == 0)
        def _():
            dst[...] = jnp.zeros_like(dst)
            dal_ref[...] = jnp.zeros_like(dal_ref)
            dds_ref[...] = jnp.zeros_like(dds_ref)
            dgn_ref[...] = jnp.zeros_like(dgn_ref)

        f = functools.partial(_ssd_chunk, tri=tri_ref[...], cmask=cm_ref[...], dmask=dm_ref[...], bd=bd_ref[...])
        for g in range(G):
            ch = slice(g * gw, (g + 1) * gw)
            bs = slice(di + g * n, di + (g + 1) * n)
            cs = slice(di + (G + g) * n, di + (G + g + 1) * n)
            _, vjp = jax.vjp(f, x_ref[:, ch], z_ref[:, ch], x_ref[:, bs], x_ref[:, cs], dt_ref[:, ch], sp_ref[0, g],
                             al_ref[0:1, ch], ds_ref[0:1, ch], gn_ref[:, ch])
            dx, dz, db, dc, ddt, ds, dal, dds, dgn = vjp((dy_ref[:, ch], dst[g]))
            dxs_ref[:, ch] = dx
            dxs_ref[:, bs] = db
            dxs_ref[:, cs] = dc
            dz_ref[:, ch] = dz.astype(dz_ref.dtype)
            ddt_ref[:, ch] = ddt
            dst[g] = ds
            dal_ref[:, ch] += dal
            dds_ref[:, ch] += dds
            dgn_ref[:, ch] += dgn

    row = lambda w: pl.BlockSpec((L, w), lambda c: (nc - 1 - c, 0))
    acc = _full((1, di))
    acc_shape = jax.ShapeDtypeStruct((1, di), F32)
    return _pcall(
        body, carry=carry, name="ssd_bwd", grid=(nc,),
        in_specs=[row(cdim), row(di), row(di), _full(alog_e.shape), _full(dskip_e.shape), _full(gn.shape)]
        + [_full(m.shape) for m in masks]
        + [pl.BlockSpec((1, G, n, gw), lambda c: (nc - 1 - c, 0, 0, 0)), row(di)],
        out_specs=[row(cdim), row(di), row(di), acc, acc, acc],
        out_shape=[jax.ShapeDtypeStruct((t, cdim), F32), jax.ShapeDtypeStruct(dproj_shape, BF16),
                   jax.ShapeDtypeStruct((t, di), F32), acc_shape, acc_shape, acc_shape],
        scratch_shapes=[pltpu.VMEM((G, n, gw), F32)],
        compiler_params=_cparams(("arbitrary",)),
    )(xs, proj, dte, alog_e, dskip_e, gn, *masks, sprev, dy)


def _rows2d(a):
    return a.reshape(-1, a.shape[-1])


def _row_tile(rows, cols):
    cap = max(SUBLANE, (1 << 19) // max(cols, 1))
    step = 2 * SUBLANE
    for c in range(min(cap, rows) // step * step, 0, -step):
        if rows % c == 0:
            return c
    return rows


def chip_sum(g, r, core, *, name):
    shape = r.shape
    cols = shape[-1]
    g4 = g.reshape(4, 2, -1, cols)
    r3 = r.reshape(4, -1, cols)
    rows = r3.shape[1]
    tr = _row_tile(rows, cols)

    def body(c_ref, g_ref, r_ref, o_ref):
        o_ref[...] = (g_ref[...].astype(F32) + r_ref[...].astype(F32)).astype(o_ref.dtype)

    out = pl.pallas_call(
        body, name=name,
        grid_spec=pltpu.PrefetchScalarGridSpec(
            num_scalar_prefetch=1, grid=(4, rows // tr),
            in_specs=[pl.BlockSpec((None, None, tr, cols), lambda j, i, c: (j, c[0], i, 0)),
                      pl.BlockSpec((None, tr, cols), lambda j, i, c: (j, i, 0))],
            out_specs=pl.BlockSpec((None, tr, cols), lambda j, i, c: (j, i, 0))),
        out_shape=jax.ShapeDtypeStruct(r3.shape, BF16), compiler_params=_cparams(("parallel", "parallel")),
    )(core.reshape(1).astype(jnp.int32), g4, r3)
    return out.reshape(shape)


def mesh_sum(p, r, chip, core, *, name):
    shape = p.shape[1:]
    cols = shape[-1]
    p3 = p.reshape(4, -1, cols)
    r3 = r.reshape(3, -1, cols)
    rows = p3.shape[1]
    tr = _row_tile(rows, 2 * cols)

    def body(c_ref, p_ref, r_ref, o_ref):
        o_ref[...] = ((p_ref[...].astype(F32) + r_ref[0].astype(F32)) + r_ref[1].astype(F32)) + r_ref[2].astype(F32)

    out = pl.pallas_call(
        body, name=name,
        grid_spec=pltpu.PrefetchScalarGridSpec(
            num_scalar_prefetch=1, grid=(rows // tr,),
            in_specs=[pl.BlockSpec((None, tr, cols), lambda i, c: (c[0], i, 0)),
                      pl.BlockSpec((3, tr, cols), lambda i, c: (0, i, 0))],
            out_specs=pl.BlockSpec((None, tr, cols), lambda i, c: (c[1], i, 0))),
        out_shape=jax.ShapeDtypeStruct((2, rows, cols), F32), compiler_params=_cparams(("parallel",)),
    )(jnp.stack([chip, core]).astype(jnp.int32), p3, r3)
    return out.reshape((2,) + shape)


def sum_leading(x, *, name):
    k, r, c = x.shape
    tr = _row_tile(r, c * k)
    def body(x_ref, o_ref):
        acc = x_ref[0]
        for i in range(1, k):
            acc = acc + x_ref[i]
        o_ref[...] = acc

    return pl.pallas_call(
        body, name=name, grid=(r // tr,), in_specs=[pl.BlockSpec((k, tr, c), lambda i: (0, i, 0))],
        out_specs=pl.BlockSpec((tr, c), lambda i: (i, 0)),
        out_shape=jax.ShapeDtypeStruct((r, c), F32), compiler_params=_cparams(("parallel",)),
    )(x)


def adamw(w, gs, m, v, *, name):
    shape = w.shape
    w3, m3, v3 = (a.reshape((-1,) + a.shape[-2:]) for a in (w, m, v))
    nl, r, c = w3.shape
    assert len(gs) == nl
    tr = _row_tile(r, 2 * c)
    tc = c
    if tr == r and r * c > (1 << 19):
        tc = next(t for t in (1024, 512, 256, 128) if c % t == 0 and r * t <= (1 << 19))
    c1 = 1.0 - ADAM_B1 ** ADAM_STEP
    c2 = 1.0 - ADAM_B2 ** ADAM_STEP

    def body(w_ref, g_ref, m_ref, v_ref, *rest):
        go_ref, d_ref, mo_ref, vo_ref = rest[-4:]
        gv = g_ref[...]
        mn = ADAM_B1 * m_ref[...] + (1.0 - ADAM_B1) * gv
        vn = ADAM_B2 * v_ref[...] + (1.0 - ADAM_B2) * (gv * gv)
        go_ref[...] = gv
        d_ref[...] = -ADAM_LR * ((mn / c1) / (jnp.sqrt(vn / c2) + ADAM_EPS) + ADAM_WD * w_ref[...])
        mo_ref[...] = mn
        vo_ref[...] = vn

    outs = None
    for l, g in enumerate(gs):
        layer = pl.BlockSpec((None, tr, tc), lambda i, j, l=l: (l, i, j))
        prev = [] if outs is None else list(outs)
        outs = pl.pallas_call(
            functools.partial(body), name=f"{name}_{l}", grid=(r // tr, c // tc),
            in_specs=[layer, pl.BlockSpec((tr, tc), lambda i, j: (i, j)), layer, layer] + [ANY] * len(prev),
            out_specs=[layer] * 4, out_shape=[jax.ShapeDtypeStruct((nl, r, c), F32)] * 4,
            input_output_aliases={4 + k: k for k in range(len(prev))},
            compiler_params=_cparams(("parallel", "parallel")),
        )(w3, g.reshape(r, c), m3, v3, *prev)
    return tuple(o.reshape(shape) for o in outs)


ANY = pl.BlockSpec(memory_space=pl.ANY)


def _place():
    x, y, c = lax.axis_index("x"), lax.axis_index("y"), lax.axis_index("c")
    chips = [(1 - x, y), (x, 1 - y), (1 - x, 1 - y)]
    return x, y, c, chips


def gather8(block):
    m, n = block.shape

    def body(x_ref, out_ref, send_sems, recv_sems, local_sem):
        x, y, c, chips = _place()
        me, sibling = (x, y, c), (x, y, 1 - c)

        def rows(px, py, pc):
            return out_ref.at[4 * px + 2 * py + pc]

        def copy(k, blk, to, src=None):
            return pltpu.make_async_remote_copy(
                src_ref=rows(*blk) if src is None else src, dst_ref=rows(*blk), send_sem=send_sems.at[k],
                recv_sem=recv_sems.at[k], device_id=to, device_id_type=MESH)

        mine = pltpu.make_async_copy(x_ref, rows(*me), local_sem)
        mine.start()
        first = [copy(0, me, sibling, src=x_ref)]
        first += [copy(1 + j, me, (*chip, c), src=x_ref) for j, chip in enumerate(chips)]
        for cp in first:
            cp.start()
        passed = [copy(4 + j, (*chip, c), sibling) for j, chip in enumerate(chips)]
        for j, chip in enumerate(chips):
            copy(1 + j, (*chip, c), me).wait_recv()
            passed[j].start()
        copy(0, sibling, me).wait_recv()
        for j, chip in enumerate(chips):
            copy(4 + j, (*chip, 1 - c), me).wait_recv()
        for cp in first + passed:
            cp.wait_send()
        mine.wait()

    return pl.pallas_call(
        body, name="gather8",
        out_shape=jax.ShapeDtypeStruct((8, m, n), block.dtype),
        in_specs=[pl.BlockSpec(memory_space=pltpu.VMEM)],
        out_specs=pl.BlockSpec(memory_space=pltpu.VMEM),
        scratch_shapes=[pltpu.SemaphoreType.DMA((7,)), pltpu.SemaphoreType.DMA((7,)), pltpu.SemaphoreType.DMA],
        compiler_params=pltpu.CompilerParams(vmem_limit_bytes=VMEM_LIMIT),
    )(block)


def gather_weights(shards):
    n = len(shards)

    def copy(ins, outs, send, recv, base, a, k, chip_idx, half, to, src=None):
        dst = outs[a].at[chip_idx, half]
        return pltpu.make_async_remote_copy(
            src_ref=dst if src is None else src, dst_ref=dst, send_sem=send.at[base + 6 * a + k],
            recv_sem=recv.at[base + 6 * a + k], device_id=to, device_id_type=MESH)

    def first(ins, outs, send, recv, base):
        x, y, c, chips = _place()
        return [copy(ins, outs, send, recv, base, a, j, 2 * x + y, c, (*chip, c), src=ins[a].at[c])
                for a in range(n) for j, chip in enumerate(chips)]

    def start(ins, outs, send, recv, base):
        for cp in first(ins, outs, send, recv, base):
            cp.start()

    def finish(ins, outs, send, recv, base):
        x, y, c, chips = _place()
        sibling = (x, y, 1 - c)
        passed = []
        for a in range(n):
            for j, (cx, cy) in enumerate(chips):
                copy(ins, outs, send, recv, base, a, j, 2 * cx + cy, c, (cx, cy, c)).wait_recv()
                fw = copy(ins, outs, send, recv, base, a, 3 + j, 2 * cx + cy, c, sibling)
                fw.start()
                passed.append(fw)
        for a in range(n):
            for j, (cx, cy) in enumerate(chips):
                copy(ins, outs, send, recv, base, a, 3 + j, 2 * cx + cy, 1 - c, sibling).wait_recv()
        for cp in first(ins, outs, send, recv, base) + passed:
            cp.wait_send()

    return Carry(shards, [jax.ShapeDtypeStruct((4,) + s.shape, s.dtype) for s in shards], 6 * n, start, finish)


def exchange_halves(grads):
    n = len(grads)

    def copies(ins, outs, send, recv, base):
        x, y, c, _ = _place()
        return [pltpu.make_async_remote_copy(
            src_ref=ins[a].at[j, 1 - c], dst_ref=outs[a].at[j], send_sem=send.at[base + 4 * a + j],
            recv_sem=recv.at[base + 4 * a + j], device_id=(x, y, 1 - c), device_id_type=MESH)
            for a in range(n) for j in range(4)]

    def start(*args):
        for cp in copies(*args):
            cp.start()

    def finish(*args):
        for cp in copies(*args):
            cp.wait()

    return Carry(grads, [jax.ShapeDtypeStruct((4,) + g.shape[2:], g.dtype) for g in grads], 4 * n, start, finish)


def scatter_chips(parts):
    n = len(parts)

    def copies(ins, outs, send, recv, base):
        x, y, c, chips = _place()
        return [pltpu.make_async_remote_copy(
            src_ref=ins[a].at[2 * cx + cy], dst_ref=outs[a].at[j], send_sem=send.at[base + 3 * a + j],
            recv_sem=recv.at[base + 3 * a + j], device_id=(cx, cy, c), device_id_type=MESH)
            for a in range(n) for j, (cx, cy) in enumerate(chips)]

    def start(*args):
        for cp in copies(*args):
            cp.start()

    def finish(*args):
        for cp in copies(*args):
            cp.wait()

    return Carry(parts, [jax.ShapeDtypeStruct((3,) + p.shape[1:], p.dtype) for p in parts], 3 * n, start, finish)


def join_halves(bufs):
    n = len(bufs)

    def copy(outs, send, recv, base, a, half):
        x, y, c, _ = _place()
        return pltpu.make_async_remote_copy(
            src_ref=outs[a].at[c], dst_ref=outs[a].at[c if half is None else half], send_sem=send.at[base + a],
            recv_sem=recv.at[base + a], device_id=(x, y, 1 - c), device_id_type=MESH)

    def start(ins, outs, send, recv, base):
        for a in range(n):
            copy(outs, send, recv, base, a, None).start()

    def finish(ins, outs, send, recv, base):
        c = lax.axis_index("c")
        for a in range(n):
            copy(outs, send, recv, base, a, 1 - c).wait_recv()
        for a in range(n):
            copy(outs, send, recv, base, a, None).wait_send()

    return Carry(bufs, [jax.ShapeDtypeStruct(h.shape, h.dtype) for h in bufs], n, start, finish,
                 aliases={a: a for a in range(n)})


def run_comm(carry, *, name):
    k_in, k_out = len(carry.arrays), len(carry.out_shape)

    def body(*refs):
        ins, outs = refs[:k_in], refs[k_in:k_in + k_out]
        send, recv = refs[-2:]
        carry.start(ins, outs, send, recv, 0)
        carry.finish(ins, outs, send, recv, 0)

    return pl.pallas_call(
        body, name=name, out_shape=carry.out_shape, in_specs=[ANY] * k_in, out_specs=[ANY] * k_out,
        scratch_shapes=[pltpu.SemaphoreType.DMA((carry.n_sems,))] * 2, input_output_aliases=carry.aliases,
    )(*carry.arrays)


INPUTS = ['x'] + WEIGHTS + ['loss_target'] + ['m_' + n for n in WEIGHTS] + ['v_' + n for n in WEIGHTS]


def _round_up(n, m):
    return -(-n // m) * m


def _pack(arrs):
    flat = jnp.concatenate([a.reshape(-1) for a in arrs])
    n = _round_up(flat.shape[0], 512 * LANE)
    return jnp.pad(flat, (0, n - flat.shape[0])).reshape(-1, LANE)


def _unpack(block, shapes):
    flat = block.reshape(-1)
    out, o = [], 0
    for s in shapes:
        n = math.prod(s)
        out.append(flat[o:o + n].reshape(s))
        o += n
    return out


def _cols(g):
    return g.transpose(1, 0, 2).reshape(g.shape[1], -1)


def _uncols(w):
    return w.reshape(w.shape[0], 4, -1).transpose(1, 0, 2)


def _pad_cols(w, total):
    return jnp.pad(w, ((0, 0), (0, total - w.shape[1])))


def kernel(*args):
    a = dict(zip(INPUTS, args))
    x, tgt = a['x'][0], a['loss_target'][0]
    t, d = x.shape
    xi, yi, ci = lax.axis_index("x"), lax.axis_index("y"), lax.axis_index("c")
    chip = 2 * xi + yi
    dk, dv, hk, hv = _gla_dims(d)
    lw = d // 2
    di = 2 * d
    nh = di // SSD_HEAD_DIM
    gn_w = SSD_GROUPS * SSD_STATE
    conv_dim = di + 2 * gn_w
    rank = GLA_GATE_RANK
    wq = 2 * dk + 2 * dv
    gla_w = _round_up(wq + LANE, 2 * lw)
    ev_tot = gla_w + 2 * lw
    od_used = di + conv_dim + nh
    od_tot = _round_up(di + conv_dim + _round_up(nh, LANE), 512)
    glr_col, xcol, dtcol = wq // LANE, gla_w // lw, (di + conv_dim) // LANE
    assert ev_tot % 512 == 0 and nh <= LANE

    def halves(w):
        w = w.astype(BF16)
        return w.reshape((2, w.shape[0] // 2) + w.shape[1:])

    own = {'ev_w_in': halves(a['ev_w_in'][0]), 'ev_w_out': halves(a['ev_w_out'][0]),
           'od_w_in_a': halves(a['od_w_in'][0][:d // 2]), 'od_w_in_b': halves(a['od_w_in'][0][d // 2:]),
           'od_w_out': halves(a['od_w_out'][0])}
    for l in range(2):
        own[f'gate{l}'], own[f'up{l}'] = halves(a['ffn_w_gate'][l]), halves(a['ffn_w_up'][l])
        own[f'down{l}'] = halves(a['ffn_w_down'][l])

    def gather(*units):
        return gather_weights([own[u] for u in units])

    def filled(unit, g):
        g = lax.dynamic_update_index_in_dim(g, own[unit], chip, 0)
        return g.reshape((4, 2 * g.shape[2]) + g.shape[3:])

    g_ev_in, g_ev_out = run_comm(gather('ev_w_in', 'ev_w_out'), name="gather_ev")
    w_ev_in = _cols(filled('ev_w_in', g_ev_in))
    cuts = [dk, 2 * dk, 2 * dk + dv, wq, wq + rank, wq + rank + lw]
    sq, sk, sv, sg, sglr, sxb, sgb = jnp.split(w_ev_in, cuts, axis=1)
    w_ev_in_p = jnp.concatenate([_pad_cols(jnp.concatenate([sq, sk, sv, sg, sglr], axis=1), gla_w), sxb, sgb], axis=1)
    w_ev_out = filled('ev_w_out', g_ev_out).reshape(-1, d)
    w_gate, w_up, w_down = [None, None], [None, None], [None, None]

    sh_names = list(SMALL_SHARDED)
    sh_shapes = [a[n].shape for n in sh_names]
    g8 = gather8(_pack([a[n] for n in sh_names]))
    per_chip = [_unpack(g8[2 * j], sh_shapes) for j in range(4)]
    full = {n: jnp.concatenate([per_chip[j][i] for j in range(4)], axis=SMALL_SHARDED[n])
            for i, n in enumerate(sh_names)}

    wg_p = jnp.zeros((LANE, dk), F32).at[:rank].set(full['ev_gla_w_gate'][0])
    bg, wn = a['ev_gla_b_gate'], a['ev_gla_w_onorm']
    lru_p = [full['ev_lru_conv_w'][0], a['ev_lru_conv_b'], a['ev_lru_w_a'][0], a['ev_lru_b_a'], a['ev_lru_w_i'][0],
             a['ev_lru_b_i'], a['ev_lru_lam']]
    od_cw, od_cb, od_gn = full['od_conv_w'][0], full['od_conv_b'], full['od_gnorm']
    heads = jnp.arange(LANE)[:, None]
    e_mat = ((jnp.arange(di)[None, :] // SSD_HEAD_DIM == heads) & (heads < nh)).astype(BF16)
    row8 = lambda p: jnp.zeros((SUBLANE, LANE), F32).at[0, :nh].set(p[0])
    dt_bias_p = jnp.zeros((1, LANE), F32).at[0, :nh].set(a['od_dt_bias'][0])
    alog_e = head_expand(row8(a['od_a_log']), e_mat, name="expand_a_log")
    dskip_e = head_expand(row8(a['od_d_skip']), e_mat, name="expand_d_skip")

    h0 = rms_fwd(x, a['ev_norm'], name="rms_ev")
    proj, (g,) = matmul(h0, w_ev_in_p, name="ev_in", carry=gather('gate0'))
    w_gate[0] = filled('gate0', g)
    (o_gla, sp_gla), (g,) = gla_fwd(proj, glr_col, wg_p, bg, wn, dv, carry=gather('up0'))
    w_up[0] = filled('up0', g)
    o_lru, hin = lru_fwd(proj, xcol, lw, *lru_p)
    x1 = matmul(o_gla, w_ev_out[:dv], add=x, name="ev_out_a")
    x1 = matmul(o_lru, w_ev_out[dv:], add=x1, name="ev_out_b")

    h1 = rms_fwd(x1, a['ffn_norm'][0:1], name="rms_ffn0")
    gate0, (g,) = matmul(h1, w_gate[0], name="ffn0_gate", carry=gather('down0'))
    w_down[0] = filled('down0', g).reshape(-1, d)
    swi = dict(epi=_swi_fwd_epi, epi_out=(F32, BF16), tm_cap=512)
    (up0, act0), (g_a,) = matmul(h1, w_up[0], name="ffn0_up", epi_in=(gate0,), carry=gather('od_w_in_a'), **swi)
    x2, (g_b,) = matmul(act0, w_down[0], add=x1, name="ffn0_down", carry=gather('od_w_in_b'))
    w_od_in = jnp.concatenate([filled('od_w_in_a', g_a), filled('od_w_in_b', g_b)], axis=1)
    w_od_in_p = _pad_cols(_cols(w_od_in), od_tot)

    h2 = rms_fwd(x2, full['od_norm'], name="rms_od")
    proj2, (g, g1) = matmul(h2, w_od_in_p, name="od_in", carry=gather('od_w_out', 'gate1'))
    w_od_out, w_gate[1] = filled('od_w_out', g).reshape(-1, d), filled('gate1', g1)
    xs, (g,) = conv_silu_fwd(proj2, di, conv_dim, od_cw, od_cb, carry=gather('up1'))
    w_up[1] = filled('up1', g)
    dte = dt_fwd(proj2, dtcol, dt_bias_p, e_mat)
    (y_ssd, sp_ssd), (g,) = ssd_fwd(xs, proj2, dte, alog_e, dskip_e, od_gn, carry=gather('down1'))
    w_down[1] = filled('down1', g).reshape(-1, d)
    x3 = matmul(y_ssd, w_od_out, add=x2, name="od_out")
    h3 = rms_fwd(x3, a['ffn_norm'][1:2], name="rms_ffn1")
    gate1 = matmul(h3, w_gate[1], name="ffn1_gate")
    up1, act1 = matmul(h3, w_up[1], name="ffn1_up", epi_in=(gate1,), **swi)
    x4 = matmul(act1, w_down[1], add=x3, name="ffn1_down")
    loss_p, dx4, d_final = loss_head(x4, a['final_norm'][None], tgt, name="loss_head")

    grads, from_sib, part, from_chips = {}, {}, {}, {}

    def rows4(dw):
        return dw.reshape((4, 2, dw.shape[0] // 8) + dw.shape[1:])

    def cols4(dw):
        return dw.reshape((4, 2, dw.shape[1] // 2) + dw.shape[2:])

    def exchange(*units):
        return exchange_halves([grads[u] for u in units])

    def scatter(*units):
        return scatter_chips([part[u] for u in units])

    def sum_chip(u):
        part[u] = chip_sum(grads[u], from_sib[u], ci, name=f"chip_sum_{u}")

    def ffn_bwd(dxo, xin, h, gate, up, act, l, first_carry, first_units):
        dn, gt, up_ = f'down{l}', f'gate{l}', f'up{l}'
        dgu = matmul(dxo, w_down[l], tb=True, name=f"ffn{l}_d_act", carry=first_carry, epi=_swi_bwd_epi,
                     epi_in=(gate, up), epi_out=(BF16, BF16))
        (dg, du), got = dgu if first_units else (dgu, ())
        for u, r in zip(first_units, got):
            from_sib[u] = r
            sum_chip(u)
        grads[dn] = rows4(matmul(act, dxo, ta=True, out_dtype=BF16, name=f"ffn{l}_dw_down"))
        dh, (from_sib[dn],) = matmul(dg, w_gate[l], tb=True, name=f"ffn{l}_dh_gate", carry=exchange(dn))
        sum_chip(dn)
        dh = matmul(du, w_up[l], tb=True, add=dh, name=f"ffn{l}_dh_up")
        d_gate, (from_chips[dn],) = matmul(h, dg, ta=True, out_dtype=BF16, out_shards=True, name=f"ffn{l}_dw_gate",
                                           carry=scatter(dn))
        grads[gt] = cols4(d_gate)
        d_up, (from_sib[gt],) = matmul(h, du, ta=True, out_dtype=BF16, out_shards=True, name=f"ffn{l}_dw_up",
                                       carry=exchange(gt))
        grads[up_] = cols4(d_up)
        dxi, d_norm = rms_bwd(xin, a['ffn_norm'][l:l + 1], dh, dxo, name=f"rms_ffn{l}_bwd")
        return dxi, d_norm

    dx3, d_fn1 = ffn_bwd(dx4, x3, h3, gate1, up1, act1, 1, None, ())
    dy, (from_sib['up1'],) = matmul(dx3, w_od_out, tb=True, name="od_out_dy", carry=exchange('up1'))
    sum_chip('gate1')
    sum_chip('up1')
    grads['od_w_out'] = rows4(matmul(y_ssd, dx3, ta=True, out_dtype=BF16, name="od_out_dw"))
    (dxs, dproj2, ddte, dal, dds, dgn), (from_chips['gate1'], from_chips['up1'], from_sib['od_w_out']) = ssd_bwd(
        xs, proj2, dte, alog_e, dskip_e, od_gn, sp_ssd, dy, proj2.shape,
        carry=merge_carries(scatter('gate1', 'up1'), exchange('od_w_out')))
    sum_chip('od_w_out')
    (dproj2, d_od_cw, d_od_cb), (from_chips['od_w_out'],) = conv_silu_bwd(
        proj2, di, conv_dim, od_cw, od_cb, dxs, dproj2, carry=scatter('od_w_out'))
    dproj2, d_dt_bias = dt_bwd(proj2, dtcol, dt_bias_p, e_mat, ddte, dproj2)
    dh2 = matmul(dproj2, w_od_in_p, tb=True, name="od_in_dh")
    d_od_in = matmul(h2, dproj2, ta=True, out_dtype=BF16, name="od_in_dw")[:, :od_used]
    grads['od_w_in'] = cols4(_uncols(d_od_in))
    dx2, d_od_norm = rms_bwd(x2, full['od_norm'], dh2, dx3, name="rms_od_bwd")
    to8 = lambda acc: jnp.zeros((SUBLANE, di), F32).at[0].set(acc.reshape(-1))
    d_a_log = head_expand(to8(dal), e_mat, transpose=True, name="reduce_a_log")[0:1, :nh]
    d_d_skip = head_expand(to8(dds), e_mat, transpose=True, name="reduce_d_skip")[0:1, :nh]

    dx1, d_fn0 = ffn_bwd(dx2, x1, h1, gate0, up0, act0, 0, exchange('od_w_in'), ('od_w_in',))
    dmix, (from_sib['up0'],) = matmul(dx1, w_ev_out, tb=True, name="ev_out_dmix", carry=exchange('up0'))
    sum_chip('gate0')
    sum_chip('up0')
    grads['ev_w_out'] = rows4(jnp.concatenate([matmul(o_gla, dx1, ta=True, out_dtype=BF16, name="ev_out_dw_a"),
                                               matmul(o_lru, dx1, ta=True, out_dtype=BF16, name="ev_out_dw_b")], axis=0))
    (dproj, d_wg, d_bg, d_wn), (from_chips['od_w_in'], from_sib['ev_w_out']) = gla_bwd(
        proj, glr_col, wg_p, bg, wn, sp_gla, dmix, dv, gla_w,
        carry=merge_carries(scatter('od_w_in'), exchange('ev_w_out')))
    sum_chip('ev_w_out')
    (dproj, *d_lru), (from_chips['ev_w_out'], from_chips['gate0']) = lru_bwd(
        proj, xcol, lw, *lru_p, hin, dmix, 1, dproj, carry=scatter('ev_w_out', 'gate0'))
    d_ev_in_p, (from_chips['up0'],) = matmul(h0, dproj, ta=True, out_dtype=BF16, name="ev_in_dw", carry=scatter('up0'))
    d_ev_in = jnp.concatenate([d_ev_in_p[:, :wq + rank], d_ev_in_p[:, gla_w:]], axis=1)
    grads['ev_w_in'] = cols4(_uncols(d_ev_in))
    (from_sib['ev_w_in'],) = run_comm(exchange('ev_w_in'), name="exchange_ev_in")
    sum_chip('ev_w_in')
    dh0, (from_chips['ev_w_in'],) = matmul(dproj, w_ev_in_p, tb=True, name="ev_in_dh", carry=scatter('ev_w_in'))
    dx0, d_ev_norm = rms_bwd(x, a['ev_norm'], dh0, dx1, name="rms_ev_bwd")

    units = list(grads)
    half = [mesh_sum(part[u], from_chips[u], chip, ci, name=f"mesh_sum_{u}") for u in units]
    done = dict(zip(units, run_comm(join_halves(half), name="join_halves")))
    layers = {n: [done[n]] for n in ('ev_w_in', 'ev_w_out', 'od_w_in', 'od_w_out')}
    layers.update({f'ffn_w_{u}': [done[f'{u}0'], done[f'{u}1']] for u in ('gate', 'up', 'down')})
    grad, delta, new_m, new_v = {}, {}, {}, {}
    for n in BIG:
        flip = a[n].shape[-1] % LANE != 0
        tr_ = (lambda t: jnp.swapaxes(t, -1, -2)) if flip else (lambda t: t)
        gs = [tr_(g.reshape(a[n].shape[-2:])) for g in layers[n]]
        outs = adamw(tr_(a[n]), gs, tr_(a['m_' + n]), tr_(a['v_' + n]), name=f"adamw_{n}")
        grad[n], delta[n], new_m[n], new_v[n] = (tr_(o) for o in outs)

    small_g = {
        'ev_norm': d_ev_norm, 'ev_gla_w_gate': d_wg[:rank][None], 'ev_gla_b_gate': d_bg, 'ev_gla_w_onorm': d_wn,
        'ev_lru_conv_w': d_lru[0][None], 'ev_lru_conv_b': d_lru[1], 'ev_lru_w_a': d_lru[2][None],
        'ev_lru_b_a': d_lru[3], 'ev_lru_w_i': d_lru[4][None], 'ev_lru_b_i': d_lru[5], 'ev_lru_lam': d_lru[6],
        'od_norm': d_od_norm, 'od_conv_w': d_od_cw[None], 'od_conv_b': d_od_cb, 'od_dt_bias': d_dt_bias[:, :nh],
        'od_a_log': d_a_log, 'od_d_skip': d_d_skip, 'od_gnorm': dgn.reshape(1, di),
        'ffn_norm': jnp.concatenate([d_fn0, d_fn1], axis=0), 'final_norm': d_final[0],
    }
    full_shapes = [small_g[n].shape for n in SMALL]
    summed = sum_leading(gather8(_pack([small_g[n] for n in SMALL])), name="sum_devices")
    for n, g in zip(SMALL, _unpack(summed, full_shapes)):
        if n in SMALL_SHARDED:
            ax = SMALL_SHARDED[n]
            sz = a[n].shape[ax]
            g = lax.dynamic_slice_in_dim(g, chip * sz, sz, axis=ax)
        grad[n] = g

    shapes = [a[n].shape for n in SMALL]
    packed = [_pack([src[n] if pre is None else a[pre + n] for n in SMALL])
              for src, pre in ((a, None), (grad, None), (None, 'm_'), (None, 'v_'))]
    small_out = adamw(packed[0], [packed[1]], packed[2], packed[3], name="adamw_small")
    for outd, blk in zip((delta, new_m, new_v), small_out[1:]):
        outd.update(zip(SMALL, _unpack(blk, shapes)))

    loss = lax.psum(loss_p[0, 0], ("x", "y", "c"))
    return (loss, dx0[None], *[grad[n] for n in WEIGHTS], *[delta[n] for n in WEIGHTS],
            *[new_m[n] for n in WEIGHTS], *[new_v[n] for n in WEIGHTS])
```

```python
import functools
import math

import jax
import jax.numpy as jnp
from jax import lax
from jax.experimental import pallas as pl
from jax.experimental.pallas import tpu as pltpu

F32 = jnp.float32
BF16 = jnp.bfloat16
MXU_DTYPE = jnp.bfloat16

NORM_EPS = 1e-6
CONV_WIDTH = 4
GLA_HEADS = 4
GLA_GATE_RANK = 16
GLA_GATE_NORM = 16.0
CHUNK = 64
LRU_BLOCK = 128
LRU_C = 8.0
SSD_HEAD_DIM = 64
SSD_GROUPS = 8
SSD_STATE = 128
ADAM_LR, ADAM_B1, ADAM_B2, ADAM_EPS, ADAM_WD, ADAM_STEP = 0.001, 0.9, 0.999, 1e-08, 0.01, 10

LANE = 128
SUBLANE = 8
VMEM_LIMIT = 48 * 1024 * 1024
MAX_TK = 2816
MATMUL_VMEM_BUDGET = 45 * 1024 * 1024
MESH = pl.DeviceIdType.MESH

WEIGHTS = ['ev_norm', 'ev_w_in', 'ev_gla_w_gate', 'ev_gla_b_gate', 'ev_gla_w_onorm', 'ev_lru_conv_w', 'ev_lru_conv_b',
           'ev_lru_w_a', 'ev_lru_b_a', 'ev_lru_w_i', 'ev_lru_b_i', 'ev_lru_lam', 'ev_w_out', 'od_norm', 'od_w_in',
           'od_conv_w', 'od_conv_b', 'od_dt_bias', 'od_a_log', 'od_d_skip', 'od_gnorm', 'od_w_out', 'ffn_norm',
           'ffn_w_gate', 'ffn_w_up', 'ffn_w_down', 'final_norm']
BIG = ['ev_w_in', 'ev_w_out', 'od_w_in', 'od_w_out', 'ffn_w_gate', 'ffn_w_up', 'ffn_w_down']
SMALL_SHARDED = {'ev_gla_w_gate': 2, 'ev_lru_conv_w': 2, 'od_norm': 1, 'od_conv_w': 2, 'od_conv_b': 1, 'od_gnorm': 1}
SMALL = [n for n in WEIGHTS if n not in BIG]


def _cparams(sem=None, **kw):
    return pltpu.CompilerParams(dimension_semantics=sem, vmem_limit_bytes=VMEM_LIMIT, **kw)


def _full(shape):
    n = len(shape)
    return pl.BlockSpec(shape, lambda *_: (0,) * n)


ANY = pl.BlockSpec(memory_space=pl.ANY)


class Carry:
    def __init__(self, arrays, out_shape, n_sems, start, finish, aliases=None):
        self.arrays, self.out_shape, self.n_sems = list(arrays), list(out_shape), n_sems
        self.start, self.finish, self.aliases = start, finish, dict(aliases or {})


def merge_carries(*cs):
    cs = [c for c in cs if c is not None]
    if not cs:
        return None
    arrays = [a for c in cs for a in c.arrays]
    out_shape = [s for c in cs for s in c.out_shape]
    offs, i0, o0, s0 = [], 0, 0, 0
    aliases = {}
    for c in cs:
        offs.append((i0, o0, s0))
        aliases.update({i0 + i: o0 + o for i, o in c.aliases.items()})
        i0, o0, s0 = i0 + len(c.arrays), o0 + len(c.out_shape), s0 + c.n_sems

    def both(which):
        def run(ins, outs, send, recv, base):
            for c, (i, o, s) in zip(cs, offs):
                getattr(c, which)(ins[i:i + len(c.arrays)], outs[o:o + len(c.out_shape)], send, recv, base + s)
        return run

    return Carry(arrays, out_shape, s0, both("start"), both("finish"), aliases)


def _pcall(body, *, name, grid, in_specs, out_specs, out_shape, scratch_shapes=(), compiler_params, carry=None,
           input_output_aliases=None):
    aliases = dict(input_output_aliases or {})
    if carry is None:
        return pl.pallas_call(body, name=name, grid=grid, in_specs=in_specs, out_specs=out_specs, out_shape=out_shape,
                              scratch_shapes=list(scratch_shapes), compiler_params=compiler_params,
                              input_output_aliases=aliases)
    single = not isinstance(out_specs, (list, tuple))
    specs_o = [out_specs] if single else list(out_specs)
    shapes_o = [out_shape] if single else list(out_shape)
    n_in, n_out, k_in, k_out, n_scr = len(in_specs), len(specs_o), len(carry.arrays), len(carry.out_shape), len(scratch_shapes)

    def wrapped(*refs):
        ins, cins = refs[:n_in], refs[n_in:n_in + k_in]
        o0 = n_in + k_in
        outs, couts = refs[o0:o0 + n_out], refs[o0 + n_out:o0 + n_out + k_out]
        scr = refs[o0 + n_out + k_out:o0 + n_out + k_out + n_scr]
        send, recv = refs[-2:]
        ids = [pl.program_id(ax) for ax in range(len(grid))]
        first = functools.reduce(jnp.logical_and, [i == 0 for i in ids])
        last = functools.reduce(jnp.logical_and, [i == g - 1 for i, g in zip(ids, grid)])

        @pl.when(first)
        def _():
            carry.start(cins, couts, send, recv, 0)

        body(*ins, *outs, *scr)

        @pl.when(last)
        def _():
            carry.finish(cins, couts, send, recv, 0)

    aliases.update({n_in + i: n_out + o for i, o in carry.aliases.items()})
    call = pl.pallas_call(
        wrapped, name=name, grid=grid, in_specs=list(in_specs) + [ANY] * k_in, out_specs=specs_o + [ANY] * k_out,
        out_shape=shapes_o + carry.out_shape,
        scratch_shapes=list(scratch_shapes) + [pltpu.SemaphoreType.DMA((carry.n_sems,))] * 2,
        compiler_params=_cparams(("arbitrary",) * len(grid)), input_output_aliases=aliases)

    def run(*args):
        res = call(*args, *carry.arrays)
        main = res[:n_out]
        return (main[0] if single else list(main)), list(res[n_out:])

    return run


def _pick(dim, cands):
    for c in cands:
        if dim % c == 0:
            return c
    return dim


def _dot(a, b, ca, cb):
    return lax.dot_general(a.astype(MXU_DTYPE), b.astype(MXU_DTYPE), (((ca,), (cb,)), ((), ())),
                           preferred_element_type=F32)


@jax.custom_vjp
def mm(a, b):
    return _dot(a, b, 1, 0)


def _mm_f(a, b):
    return mm(a, b), (a, b)


def _mm_b(res, g):
    a, b = res
    return mm_nt(g, b).astype(a.dtype), mm_tn(a, g).astype(b.dtype)


@jax.custom_vjp
def mm_nt(a, b):
    return _dot(a, b, 1, 1)


def _mm_nt_f(a, b):
    return mm_nt(a, b), (a, b)


def _mm_nt_b(res, g):
    a, b = res
    return mm(g, b).astype(a.dtype), mm_tn(g, a).astype(b.dtype)


@jax.custom_vjp
def mm_tn(a, b):
    return _dot(a, b, 0, 0)


def _mm_tn_f(a, b):
    return mm_tn(a, b), (a, b)


def _mm_tn_b(res, g):
    a, b = res
    return mm_nt(b, g).astype(a.dtype), mm(a, g).astype(b.dtype)


mm.defvjp(_mm_f, _mm_b)
mm_nt.defvjp(_mm_nt_f, _mm_nt_b)
mm_tn.defvjp(_mm_tn_f, _mm_tn_b)


def _split3(a):
    h = a.astype(BF16)
    r = a - h.astype(F32)
    m = r.astype(BF16)
    l = (r - m.astype(F32)).astype(BF16)
    return h, m, l


def _exact_dot(t, a, ca, cb):
    out = None
    for p in _split3(a):
        d = lax.dot_general(t, p, (((ca,), (cb,)), ((), ())), preferred_element_type=F32)
        out = d if out is None else out + d
    return out


@jax.custom_vjp
def sel_l(t, a):
    return _exact_dot(t, a, 1, 0)


def _sel_l_f(t, a):
    return sel_l(t, a), t


def _sel_l_b(t, g):
    return jnp.zeros_like(t), _exact_dot(t, g, 0, 0)


sel_l.defvjp(_sel_l_f, _sel_l_b)


@jax.custom_vjp
def sel_r(a, t):
    out = None
    for p in _split3(a):
        d = lax.dot_general(p, t, (((1,), (0,)), ((), ())), preferred_element_type=F32)
        out = d if out is None else out + d
    return out


def _sel_r_f(a, t):
    return sel_r(a, t), t


def _sel_r_b(t, g):
    out = None
    for p in _split3(g):
        d = lax.dot_general(p, t, (((1,), (1,)), ((), ())), preferred_element_type=F32)
        out = d if out is None else out + d
    return out, jnp.zeros_like(t)


sel_r.defvjp(_sel_r_f, _sel_r_b)


def _sigmoid(x):
    return 1.0 / (1.0 + jnp.exp(-x))


def _silu(x):
    return x * _sigmoid(x)


def _softplus(x):
    return jnp.maximum(x, 0.0) + jnp.log(1.0 + jnp.exp(-jnp.abs(x)))


def _log_sigmoid(x):
    return -_softplus(-x)


def _gelu_tanh(x):
    c = math.sqrt(2.0 / math.pi)
    return 0.5 * x * (1.0 + jnp.tanh(c * (x + 0.044715 * (x * x * x))))


def _rms(x, w):
    return x * lax.rsqrt(jnp.mean(x * x, axis=-1, keepdims=True) + NORM_EPS) * w


def _tri(n, dtype=BF16):
    r = lax.broadcasted_iota(jnp.int32, (n, n), 0)
    c = lax.broadcasted_iota(jnp.int32, (n, n), 1)
    return (c <= r).astype(dtype)


def matmul(a, b, *, ta=False, tb=False, add=None, out_dtype=F32, out_shards=False, carry=None, name,
           epi=None, epi_in=(), epi_out=()):
    m, k = (a.shape[1], a.shape[0]) if ta else a.shape
    b_sh = b.ndim == 3
    if b_sh:
        s, br, bc = b.shape
        k2, n = (s * bc, br) if tb else (br, s * bc)
    else:
        k2, n = (b.shape[1], b.shape[0]) if tb else b.shape
    assert k == k2, (a.shape, b.shape, ta, tb)
    tk = k if k <= MAX_TK else max(c for c in range(LANE, MAX_TK + 1, LANE) if k % c == 0)
    if b_sh and tb:
        tk = bc
    nk = k // tk
    n_add, n_x = int(add is not None), len(epi_in)
    out_dtypes = list(epi_out) if epi is not None else [out_dtype]
    n_o = len(out_dtypes)
    tn_opts = [bc] if b_sh and not tb else [n // 4] if out_shards else \
        [c for c in range(2048, LANE - 1, -LANE) if n % c == 0] or [n]
    tm_opts = [c for c in range(2048, LANE - 1, -LANE) if m % c == 0] or [m]
    sa, sb = a.dtype.itemsize, b.dtype.itemsize
    per_elem = sum(jnp.dtype(dt).itemsize for dt in out_dtypes) + 4 * n_add + sum(x.dtype.itemsize for x in epi_in)
    best = None
    for tm_ in tm_opts:
        for tn_ in tn_opts:
            vmem = 2 * (tm_ * tk * sa + tk * tn_ * sb) + tm_ * tn_ * (4 + 2 * per_elem)
            vmem += tm_ * tn_ * 4
            if vmem > MATMUL_VMEM_BUDGET and (tm_, tn_) != (tm_opts[-1], tn_opts[-1]):
                continue
            moved = m * k * sa * (1 if nk == 1 else n // tn_) + k * n * sb * (m // tm_)
            if best is None or (moved, -tm_ * tn_) < best[0]:
                best = ((moved, -tm_ * tn_), tm_, tn_)
    _, tm, tn = best

    def body(*refs):
        a_ref, b_ref = refs[:2]
        x_refs = refs[2 + n_add:2 + n_add + n_x]
        o_refs = refs[2 + n_add + n_x:2 + n_add + n_x + n_o]
        acc = refs[-1]
        kk = pl.program_id(2)

        @pl.when(kk == 0)
        def _():
            acc[...] = jnp.zeros_like(acc)

        acc[...] += _dot(a_ref[...], b_ref[...], 0 if ta else 1, 1 if tb else 0)

        @pl.when(kk == nk - 1)
        def _():
            r = acc[...]
            if add is not None:
                r = r + refs[2][...].astype(F32)
            vals = (r,) if epi is None else epi(r, *[x[...] for x in x_refs])
            for o_ref, v in zip(o_refs, vals):
                o_ref[...] = v.astype(o_ref.dtype)

    a_spec = pl.BlockSpec((tk, tm), lambda i, j, kk: (kk, i)) if ta else pl.BlockSpec((tm, tk), lambda i, j, kk: (i, kk))
    if b_sh and tb:
        b_spec = pl.BlockSpec((None, tn, tk), lambda i, j, kk: (kk, j, 0))
    elif b_sh:
        b_spec = pl.BlockSpec((None, tk, tn), lambda i, j, kk: (j, kk, 0))
    elif tb:
        b_spec = pl.BlockSpec((tn, tk), lambda i, j, kk: (j, kk))
    else:
        b_spec = pl.BlockSpec((tk, tn), lambda i, j, kk: (kk, j))
    in_specs, args = [a_spec, b_spec], [a, b]
    tile = pl.BlockSpec((tm, tn), lambda i, j, kk: (i, j))
    for extra in ([add] if add is not None else []) + list(epi_in):
        in_specs.append(tile)
        args.append(extra)
    if out_shards:
        out_spec = pl.BlockSpec((None, tm, tn), lambda i, j, kk: (j, i, 0))
        out_shape = jax.ShapeDtypeStruct((4, m, tn), out_dtype)
    elif epi is not None:
        out_spec = [tile] * n_o
        out_shape = [jax.ShapeDtypeStruct((m, n), dt) for dt in out_dtypes]
    else:
        out_spec = tile
        out_shape = jax.ShapeDtypeStruct((m, n), out_dtype)
    return _pcall(
        body, name=name, grid=(m // tm, n // tn, nk), in_specs=in_specs, out_specs=out_spec, out_shape=out_shape,
        scratch_shapes=[pltpu.VMEM((tm, tn), F32)],
        compiler_params=_cparams(("parallel", "parallel", "arbitrary")), carry=carry,
    )(*args)


def rms_fwd(x, w, *, name):
    t, d = x.shape
    tb = _pick(t, (256, 128, 64))

    def body(x_ref, w_ref, o_ref):
        o_ref[...] = _rms(x_ref[...], w_ref[...]).astype(o_ref.dtype)

    return pl.pallas_call(
        body, name=name, grid=(t // tb,),
        in_specs=[pl.BlockSpec((tb, d), lambda i: (i, 0)), _full((1, d))],
        out_specs=pl.BlockSpec((tb, d), lambda i: (i, 0)),
        out_shape=jax.ShapeDtypeStruct((t, d), BF16),
        compiler_params=_cparams(("parallel",)),
    )(x, w)


def rms_bwd(x, w, dh, dres, *, name, carry=None):
    t, d = x.shape
    tb = _pick(t, (256, 128, 64))

    def body(x_ref, w_ref, dh_ref, dres_ref, dx_ref, dxb_ref, dw_ref):
        @pl.when(pl.program_id(0) == 0)
        def _():
            dw_ref[...] = jnp.zeros_like(dw_ref)

        _, vjp = jax.vjp(_rms, x_ref[...], w_ref[...])
        dx, dw = vjp(dh_ref[...].astype(F32))
        dx = dx + dres_ref[...]
        dx_ref[...] = dx
        dxb_ref[...] = dx.astype(dxb_ref.dtype)
        dw_ref[...] += dw

    row = pl.BlockSpec((tb, d), lambda i: (i, 0))
    return _pcall(
        body, name=name, grid=(t // tb,), carry=carry,
        in_specs=[row, _full((1, d)), row, row],
        out_specs=[row, row, _full((1, d))],
        out_shape=[jax.ShapeDtypeStruct((t, d), F32), jax.ShapeDtypeStruct((t, d), BF16),
                   jax.ShapeDtypeStruct((1, d), F32)],
        compiler_params=_cparams(("arbitrary",)),
    )(x, w, dh, dres)


def _swi(g, u):
    return _silu(g) * u


def _swi_fwd_epi(u, g):
    return u, _swi(g, u)


def _swi_bwd_epi(d, g, u):
    return jax.vjp(_swi, g, u)[1](d)


def loss_head(x, w, target, *, name):
    t, d = x.shape
    tb = _pick(t, (256, 128, 64))

    def f(xv, wv, tv):
        y = _rms(xv, wv)
        e = y - tv
        return 0.5 * jnp.sum(jnp.mean(e * e, axis=-1, keepdims=True), axis=0, keepdims=True)

    def body(x_ref, w_ref, t_ref, l_ref, dx_ref, dxb_ref, dw_ref):
        @pl.when(pl.program_id(0) == 0)
        def _():
            l_ref[...] = jnp.zeros_like(l_ref)
            dw_ref[...] = jnp.zeros_like(dw_ref)

        val, vjp = jax.vjp(lambda a, b: f(a, b, t_ref[...]), x_ref[...], w_ref[...])
        dx, dw = vjp(jnp.ones((1, 1), F32))
        l_ref[...] += jnp.broadcast_to(val, l_ref.shape)
        dx_ref[...] = dx
        dxb_ref[...] = dx.astype(dxb_ref.dtype)
        dw_ref[...] += dw

    row = pl.BlockSpec((tb, d), lambda i: (i, 0))
    return pl.pallas_call(
        body, name=name, grid=(t // tb,),
        in_specs=[row, _full((1, d)), row],
        out_specs=[_full((SUBLANE, LANE)), row, row, _full((1, d))],
        out_shape=[jax.ShapeDtypeStruct((SUBLANE, LANE), F32), jax.ShapeDtypeStruct((t, d), F32),
                   jax.ShapeDtypeStruct((t, d), BF16),
                   jax.ShapeDtypeStruct((1, d), F32)],
        compiler_params=_cparams(("arbitrary",)),
    )(x, w, target)


def _gla_chunk(q, k, v, g, glr, st, wg, bg, wn, tri):
    L, hk = q.shape
    la = _log_sigmoid(mm(glr, wg) + bg) / GLA_GATE_NORM
    bcum = sel_l(tri, la)
    b_last = jnp.sum(la, axis=0, keepdims=True)
    rows = lax.broadcasted_iota(jnp.int32, (L, 1), 0)
    b_mid = jnp.sum(jnp.where(rows <= L // 2, la, 0.0), axis=0, keepdims=True)
    qs = q * (hk ** -0.5)
    q_in = qs * jnp.exp(bcum - b_mid)
    k_in = k * jnp.exp(b_mid - bcum)
    scores = mm_nt(q_in, k_in) * tri.astype(F32)
    o_intra = mm(scores, v)
    k_st = k * jnp.exp(b_last - bcum)
    d_st = mm_tn(v, k_st)
    o_inter = mm_nt(qs * jnp.exp(bcum), st)
    st_new = jnp.exp(b_last) * st + d_st
    o = _rms(o_intra + o_inter, wn) * _silu(g)
    return o, st_new


def _gla_dims(d):
    dv = d // 2
    dk = dv // 2
    return dk, dv, dk // GLA_HEADS, dv // GLA_HEADS


def gla_fwd(proj, glr_col, wg, bg, wn, dv, carry=None):
    t = proj.shape[0]
    dk, dv, hk, hv = _gla_dims(2 * dv)
    L, H = CHUNK, GLA_HEADS
    nc = t // L
    wq = 2 * dk + 2 * dv

    def body(p_ref, glr_ref, wg_ref, bg_ref, wn_ref, o_ref, sp_ref, st):
        @pl.when(pl.program_id(0) == 0)
        def _():
            st[...] = jnp.zeros_like(st)

        tri = _tri(L)
        glr = glr_ref[...]
        for h in range(H):
            q = p_ref[:, h * hk:(h + 1) * hk]
            k = p_ref[:, dk + h * hk:dk + (h + 1) * hk]
            v = p_ref[:, 2 * dk + h * hv:2 * dk + (h + 1) * hv]
            g = p_ref[:, 2 * dk + dv + h * hv:2 * dk + dv + (h + 1) * hv]
            s_prev = st[h]
            sp_ref[0, h] = s_prev
            o, s_new = _gla_chunk(q, k, v, g, glr, s_prev, wg_ref[:, h * hk:(h + 1) * hk],
                                  bg_ref[:, h * hk:(h + 1) * hk], wn_ref[...], tri)
            o_ref[:, h * hv:(h + 1) * hv] = o.astype(o_ref.dtype)
            st[h] = s_new

    return _pcall(
        body, carry=carry, name="gla_fwd", grid=(nc,),
        in_specs=[pl.BlockSpec((L, wq), lambda c: (c, 0)), pl.BlockSpec((L, LANE), lambda c: (c, glr_col)),
                  _full(wg.shape), _full(bg.shape), _full(wn.shape)],
        out_specs=[pl.BlockSpec((L, dv), lambda c: (c, 0)), pl.BlockSpec((1, H, hv, hk), lambda c: (c, 0, 0, 0))],
        out_shape=[jax.ShapeDtypeStruct((t, dv), BF16), jax.ShapeDtypeStruct((nc, H, hv, hk), F32)],
        scratch_shapes=[pltpu.VMEM((H, hv, hk), F32)],
        compiler_params=_cparams(("arbitrary",)),
    )(proj, proj, wg, bg, wn)


def gla_bwd(proj, glr_col, wg, bg, wn, sprev, do, dv, gla_w, carry=None):
    t = proj.shape[0]
    dk, _, hk, hv = _gla_dims(2 * dv)
    L, H = CHUNK, GLA_HEADS
    nc = t // L
    wq = 2 * dk + 2 * dv

    def body(p_ref, glr_ref, wg_ref, bg_ref, wn_ref, sp_ref, do_ref, dp_ref, dwg_ref, dbg_ref, dwn_ref, dst):
        @pl.when(pl.program_id(0) == 0)
        def _():
            dst[...] = jnp.zeros_like(dst)
            dwg_ref[...] = jnp.zeros_like(dwg_ref)
            dbg_ref[...] = jnp.zeros_like(dbg_ref)
            dwn_ref[...] = jnp.zeros_like(dwn_ref)

        tri = _tri(L)
        glr = glr_ref[...]
        dglr = jnp.zeros_like(glr)
        for h in range(H):
            ks = slice(h * hk, (h + 1) * hk)
            q = p_ref[:, ks]
            k = p_ref[:, dk + h * hk:dk + (h + 1) * hk]
            v = p_ref[:, 2 * dk + h * hv:2 * dk + (h + 1) * hv]
            g = p_ref[:, 2 * dk + dv + h * hv:2 * dk + dv + (h + 1) * hv]
            f = functools.partial(_gla_chunk, tri=tri)
            _, vjp = jax.vjp(f, q, k, v, g, glr, sp_ref[0, h], wg_ref[:, ks], bg_ref[:, ks], wn_ref[...])
            dq, dkk, dvv, dg, dgl, ds, dwg, dbg, dwn = vjp((do_ref[:, h * hv:(h + 1) * hv], dst[h]))
            dp_ref[:, ks] = dq.astype(dp_ref.dtype)
            dp_ref[:, dk + h * hk:dk + (h + 1) * hk] = dkk.astype(dp_ref.dtype)
            dp_ref[:, 2 * dk + h * hv:2 * dk + (h + 1) * hv] = dvv.astype(dp_ref.dtype)
            dp_ref[:, 2 * dk + dv + h * hv:2 * dk + dv + (h + 1) * hv] = dg.astype(dp_ref.dtype)
            dglr = dglr + dgl
            dst[h] = ds
            dwg_ref[:, ks] += dwg
            dbg_ref[:, ks] += dbg
            dwn_ref[...] += dwn
        dp_ref[:, wq:wq + LANE] = dglr.astype(dp_ref.dtype)
        if gla_w > wq + LANE:
            dp_ref[:, wq + LANE:] = jnp.zeros((L, gla_w - wq - LANE), dp_ref.dtype)

    rev = lambda c: nc - 1 - c
    return _pcall(
        body, carry=carry, name="gla_bwd", grid=(nc,),
        in_specs=[pl.BlockSpec((L, wq), lambda c: (rev(c), 0)), pl.BlockSpec((L, LANE), lambda c: (rev(c), glr_col)),
                  _full(wg.shape), _full(bg.shape), _full(wn.shape),
                  pl.BlockSpec((1, H, hv, hk), lambda c: (rev(c), 0, 0, 0)),
                  pl.BlockSpec((L, dv), lambda c: (rev(c), 0))],
        out_specs=[pl.BlockSpec((L, gla_w), lambda c: (rev(c), 0)),
                   _full(wg.shape), _full(bg.shape), _full(wn.shape)],
        out_shape=[jax.ShapeDtypeStruct(proj.shape, BF16),
                   jax.ShapeDtypeStruct(wg.shape, F32), jax.ShapeDtypeStruct(bg.shape, F32),
                   jax.ShapeDtypeStruct(wn.shape, F32)],
        scratch_shapes=[pltpu.VMEM((H, hv, hk), F32)],
        compiler_params=_cparams(("arbitrary",)),
    )(proj, proj, wg, bg, wn, sprev, do)


def _shift_down(x, tail, s):
    if s == 0:
        return x
    r = pltpu.roll(x, s, 0)
    rows = lax.broadcasted_iota(jnp.int32, tail.shape, 0)
    top = jnp.where(rows < s, pltpu.roll(tail, s, 0), r[:SUBLANE])
    return jnp.concatenate([top, r[SUBLANE:]], axis=0)


def _shift_up(x, head, s):
    if s == 0:
        return x
    n = x.shape[0]
    r = pltpu.roll(x, n - s, 0)
    rows = lax.broadcasted_iota(jnp.int32, head.shape, 0)
    bottom = jnp.where(rows >= SUBLANE - s, pltpu.roll(head, SUBLANE - s, 0), r[n - SUBLANE:])
    return jnp.concatenate([r[:n - SUBLANE], bottom], axis=0)


def _conv(x, prev, w, b):
    y = b
    for k in range(CONV_WIDTH):
        y = y + w[k:k + 1, :] * _shift_down(x, prev, CONV_WIDTH - 1 - k)
    return y


def _conv_bwd(dy, nxt, x, prev, w):
    dx = None
    dws = []
    for k in range(CONV_WIDTH):
        s = CONV_WIDTH - 1 - k
        term = w[k:k + 1, :] * _shift_up(dy, nxt, s)
        dx = term if dx is None else dx + term
        dws.append(jnp.sum(dy * _shift_down(x, prev, s), axis=0, keepdims=True))
    return dx, jnp.concatenate(dws, axis=0), jnp.sum(dy, axis=0, keepdims=True)


def _scan_fwd(a, u):
    n = a.shape[0]
    rows = lax.broadcasted_iota(jnp.int32, a.shape, 0)
    s = 1
    while s < n:
        a_sh = jnp.where(rows < s, 1.0, pltpu.roll(a, s, 0))
        u_sh = jnp.where(rows < s, 0.0, pltpu.roll(u, s, 0))
        u = a * u_sh + u
        a = a * a_sh
        s *= 2
    return a, u


def _scan_rev(c, d):
    n = c.shape[0]
    rows = lax.broadcasted_iota(jnp.int32, c.shape, 0)
    s = 1
    while s < n:
        c_sh = jnp.where(rows >= n - s, 0.0, pltpu.roll(c, n - s, 0))
        d_sh = jnp.where(rows >= n - s, 0.0, pltpu.roll(d, n - s, 0))
        d = d + c * d_sh
        c = c * c_sh
        s *= 2
    return d


def _expm1(x):
    small = x * (1.0 + x * (0.5 + x * (1.0 / 6.0 + x * (1.0 / 24.0))))
    return jnp.where(jnp.abs(x) < 1e-2, small, jnp.exp(x) - 1.0)


def _lru_gates(xc, pa, pi, lam):
    r = _sigmoid(pa)
    i = _sigmoid(pi)
    log_a = LRU_C * r * _log_sigmoid(lam)
    a = jnp.exp(log_a)
    u = jnp.sqrt(-_expm1(2.0 * log_a)) * (i * xc)
    return a, u


def _lru_out(h, gate):
    return h * _gelu_tanh(gate)


def _blockdiag(xc, w_ref, b):
    nb = w_ref.shape[0]
    outs = [mm(xc[:, n * LRU_BLOCK:(n + 1) * LRU_BLOCK], w_ref[n]) for n in range(nb)]
    return jnp.concatenate(outs, axis=1) + b


def lru_fwd(proj, xcol, lw, cw, cb, wa, ba, wi, bi, lam):
    t = proj.shape[0]
    tb = _pick(t, (256, 128, 64))
    nb = t // tb

    def body(x_ref, xp_ref, g_ref, cw_ref, cb_ref, wa_ref, ba_ref, wi_ref, bi_ref, lam_ref, o_ref, hin_ref, hc):
        i = pl.program_id(0)

        @pl.when(i == 0)
        def _():
            hc[...] = jnp.zeros_like(hc)

        prev = jnp.where(i == 0, 0.0, xp_ref[...])
        xc = _conv(x_ref[...], prev, cw_ref[...], cb_ref[...])
        a, u = _lru_gates(xc, _blockdiag(xc, wa_ref, ba_ref[...]), _blockdiag(xc, wi_ref, bi_ref[...]), lam_ref[...])
        acum, h0 = _scan_fwd(a, u)
        h = h0 + acum * hc[...]
        hin_ref[0] = hc[...]
        hc[...] = h[tb - 1:tb, :]
        o_ref[...] = _lru_out(h, g_ref[...]).astype(o_ref.dtype)

    row = lambda col: pl.BlockSpec((tb, lw), lambda i: (i, col))
    return pl.pallas_call(
        body, name="lru_fwd", grid=(nb,),
        in_specs=[row(xcol), pl.BlockSpec((SUBLANE, lw), lambda i: (jnp.maximum(i * (tb // SUBLANE) - 1, 0), xcol)),
                  row(xcol + 1),
                  _full(cw.shape), _full(cb.shape), _full(wa.shape), _full(ba.shape), _full(wi.shape), _full(bi.shape),
                  _full(lam.shape)],
        out_specs=[pl.BlockSpec((tb, lw), lambda i: (i, 0)), pl.BlockSpec((1, 1, lw), lambda i: (i, 0, 0))],
        out_shape=[jax.ShapeDtypeStruct((t, lw), BF16), jax.ShapeDtypeStruct((nb, 1, lw), F32)],
        scratch_shapes=[pltpu.VMEM((1, lw), F32)],
        compiler_params=_cparams(("arbitrary",)),
    )(proj, proj, proj, cw, cb, wa, ba, wi, bi, lam)


def lru_bwd(proj, xcol, lw, cw, cb, wa, ba, wi, bi, lam, hin, dmix, docol, dproj, carry=None):
    t = proj.shape[0]
    tb = _pick(t, (256, 128, 64))
    nb = t // tb
    nblk = wa.shape[0]
    assert xcol % 2 == 0

    def body(x_ref, xp_ref, g_ref, cw_ref, cb_ref, wa_ref, ba_ref, wi_ref, bi_ref, lam_ref, hin_ref, do_ref, _,
             dxg_ref, dcw_ref, dcb_ref, dwa_ref, dba_ref, dwi_ref, dbi_ref, dlam_ref, gc, dxcn):
        pid = pl.program_id(0)
        i = nb - 1 - pid

        @pl.when(pid == 0)
        def _():
            gc[...] = jnp.zeros_like(gc)
            dxcn[...] = jnp.zeros_like(dxcn)
            for r in (dcw_ref, dcb_ref, dwa_ref, dba_ref, dwi_ref, dbi_ref, dlam_ref):
                r[...] = jnp.zeros_like(r)

        x = x_ref[...]
        prev = jnp.where(i == 0, 0.0, xp_ref[...])
        cw_v = cw_ref[...]
        xc = _conv(x, prev, cw_v, cb_ref[...])
        pa = _blockdiag(xc, wa_ref, ba_ref[...])
        pi = _blockdiag(xc, wi_ref, bi_ref[...])
        (a, u), vjp_g = jax.vjp(_lru_gates, xc, pa, pi, lam_ref[...])
        acum, h0 = _scan_fwd(a, u)
        hi = hin_ref[0]
        h = h0 + acum * hi
        rows = lax.broadcasted_iota(jnp.int32, h.shape, 0)
        hprev = jnp.where(rows < 1, hi, pltpu.roll(h, 1, 0))
        _, vjp_o = jax.vjp(_lru_out, h, g_ref[...])
        dh, dgate = vjp_o(do_ref[...].astype(F32))
        c = jnp.where(rows >= tb - 1, 0.0, pltpu.roll(a, tb - 1, 0))
        g = _scan_rev(c, dh + jnp.where(rows == tb - 1, gc[...], 0.0))
        gc[...] = a[0:1, :] * g[0:1, :]
        dxc, dpa, dpi, dlam = vjp_g((g * hprev, g))
        dlam_ref[...] += dlam
        dba_ref[...] += jnp.sum(dpa, axis=0, keepdims=True)
        dbi_ref[...] += jnp.sum(dpi, axis=0, keepdims=True)
        parts = []
        for n in range(nblk):
            sl = slice(n * LRU_BLOCK, (n + 1) * LRU_BLOCK)
            dwa_ref[n] += mm_tn(xc[:, sl], dpa[:, sl])
            dwi_ref[n] += mm_tn(xc[:, sl], dpi[:, sl])
            parts.append(mm_nt(dpa[:, sl], wa_ref[n]) + mm_nt(dpi[:, sl], wi_ref[n]))
        dxc = dxc + jnp.concatenate(parts, axis=1)
        dx, dcw, dcb = _conv_bwd(dxc, dxcn[...], x, prev, cw_v)
        dxcn[...] = dxc[:SUBLANE]
        dcw_ref[...] += dcw
        dcb_ref[...] += dcb
        dxg_ref[:, :lw] = dx.astype(dxg_ref.dtype)
        dxg_ref[:, lw:] = dgate.astype(dxg_ref.dtype)

    row = lambda col: pl.BlockSpec((tb, lw), lambda p: (nb - 1 - p, col))
    params = [cw, cb, wa, ba, wi, bi, lam]
    return _pcall(
        body, carry=carry, name="lru_bwd", grid=(nb,),
        in_specs=[row(xcol),
                  pl.BlockSpec((SUBLANE, lw), lambda p: (jnp.maximum((nb - 1 - p) * (tb // SUBLANE) - 1, 0), xcol)),
                  row(xcol + 1)]
        + [_full(p.shape) for p in params]
        + [pl.BlockSpec((1, 1, lw), lambda p: (nb - 1 - p, 0, 0)), row(docol), ANY],
        out_specs=[pl.BlockSpec((tb, 2 * lw), lambda p: (nb - 1 - p, xcol // 2))] + [_full(p.shape) for p in params],
        out_shape=[jax.ShapeDtypeStruct(dproj.shape, dproj.dtype)]
        + [jax.ShapeDtypeStruct(p.shape, F32) for p in params],
        input_output_aliases={12: 0},
        scratch_shapes=[pltpu.VMEM((1, lw), F32), pltpu.VMEM((SUBLANE, lw), F32)],
        compiler_params=_cparams(("arbitrary",)),
    )(proj, proj, proj, *params, hin, dmix, dproj)


def conv_silu_fwd(proj, col0, width, cw, cb, carry=None):
    t = proj.shape[0]
    tb = _pick(t, (512, 256, 128, 64))
    cbw = _pick(width, (512, 256, 128))
    off = col0 // cbw
    assert col0 % cbw == 0

    def body(x_ref, xp_ref, w_ref, b_ref, o_ref):
        prev = jnp.where(pl.program_id(1) == 0, 0.0, xp_ref[...])
        o_ref[...] = _silu(_conv(x_ref[...], prev, w_ref[...], b_ref[...]))

    return _pcall(
        body, carry=carry, name="conv_silu_fwd", grid=(width // cbw, t // tb),
        in_specs=[pl.BlockSpec((tb, cbw), lambda j, i: (i, off + j)),
                  pl.BlockSpec((SUBLANE, cbw), lambda j, i: (jnp.maximum(i * (tb // SUBLANE) - 1, 0), off + j)),
                  pl.BlockSpec((CONV_WIDTH, cbw), lambda j, i: (0, j)), pl.BlockSpec((1, cbw), lambda j, i: (0, j))],
        out_specs=pl.BlockSpec((tb, cbw), lambda j, i: (i, j)),
        out_shape=jax.ShapeDtypeStruct((t, width), F32),
        compiler_params=_cparams(("parallel", "arbitrary")),
    )(proj, proj, cw, cb)


def conv_silu_bwd(proj, col0, width, cw, cb, dact, dproj, carry=None):
    t = proj.shape[0]
    tb = _pick(t, (512, 256, 128, 64))
    nb = t // tb
    cbw = _pick(width, (512, 256, 128))
    off = col0 // cbw

    def body(x_ref, xp_ref, w_ref, b_ref, d_ref, _, dx_ref, dw_ref, db_ref, nxt):
        pid = pl.program_id(1)
        i = nb - 1 - pid

        @pl.when(pid == 0)
        def _():
            nxt[...] = jnp.zeros_like(nxt)
            dw_ref[...] = jnp.zeros_like(dw_ref)
            db_ref[...] = jnp.zeros_like(db_ref)

        x = x_ref[...]
        prev = jnp.where(i == 0, 0.0, xp_ref[...])
        w = w_ref[...]
        _, vjp = jax.vjp(_silu, _conv(x, prev, w, b_ref[...]))
        (dcv,) = vjp(d_ref[...])
        dx, dw, db = _conv_bwd(dcv, nxt[...], x, prev, w)
        nxt[...] = dcv[:SUBLANE]
        dx_ref[...] = dx.astype(dx_ref.dtype)
        dw_ref[...] += dw
        db_ref[...] += db

    return _pcall(
        body, carry=carry, name="conv_silu_bwd", grid=(width // cbw, nb),
        in_specs=[pl.BlockSpec((tb, cbw), lambda j, p: (nb - 1 - p, off + j)),
                  pl.BlockSpec((SUBLANE, cbw),
                               lambda j, p: (jnp.maximum((nb - 1 - p) * (tb // SUBLANE) - 1, 0), off + j)),
                  pl.BlockSpec((CONV_WIDTH, cbw), lambda j, p: (0, j)), pl.BlockSpec((1, cbw), lambda j, p: (0, j)),
                  pl.BlockSpec((tb, cbw), lambda j, p: (nb - 1 - p, j)), ANY],
        out_specs=[pl.BlockSpec((tb, cbw), lambda j, p: (nb - 1 - p, off + j)),
                   pl.BlockSpec((CONV_WIDTH, cbw), lambda j, p: (0, j)), pl.BlockSpec((1, cbw), lambda j, p: (0, j))],
        out_shape=[jax.ShapeDtypeStruct(dproj.shape, dproj.dtype), jax.ShapeDtypeStruct(cw.shape, F32),
                   jax.ShapeDtypeStruct(cb.shape, F32)],
        scratch_shapes=[pltpu.VMEM((SUBLANE, cbw), F32)],
        input_output_aliases={5: 0},
        compiler_params=_cparams(("parallel", "arbitrary")),
    )(proj, proj, cw, cb, dact, dproj)


def _dt_expand(raw, bias, e):
    return sel_r(_softplus(raw + bias), e)


def dt_fwd(proj, dtcol, bias, e):
    t = proj.shape[0]
    di = e.shape[1]
    tb = _pick(t, (512, 256, 128, 64))

    def body(r_ref, b_ref, e_ref, o_ref):
        o_ref[...] = _dt_expand(r_ref[...], b_ref[...], e_ref[...])

    return pl.pallas_call(
        body, name="dt_fwd", grid=(t // tb,),
        in_specs=[pl.BlockSpec((tb, LANE), lambda i: (i, dtcol)), _full(bias.shape), _full(e.shape)],
        out_specs=pl.BlockSpec((tb, di), lambda i: (i, 0)),
        out_shape=jax.ShapeDtypeStruct((t, di), F32),
        compiler_params=_cparams(("parallel",)),
    )(proj, bias, e)


def dt_bwd(proj, dtcol, bias, e, ddte, dproj):
    t = proj.shape[0]
    di = e.shape[1]
    tb = _pick(t, (512, 256, 128, 64))
    tail = dproj.shape[1] - dtcol * LANE
    assert (dtcol * LANE) % tail == 0

    def body(r_ref, b_ref, e_ref, d_ref, _, dr_ref, db_ref):
        @pl.when(pl.program_id(0) == 0)
        def _():
            db_ref[...] = jnp.zeros_like(db_ref)

        e_v = e_ref[...]
        _, vjp = jax.vjp(lambda r, b: _dt_expand(r, b, e_v), r_ref[...], b_ref[...])
        dr, db = vjp(d_ref[...])
        dr_ref[:, :LANE] = dr.astype(dr_ref.dtype)
        if tail > LANE:
            dr_ref[:, LANE:] = jnp.zeros((tb, tail - LANE), dr_ref.dtype)
        db_ref[...] += db

    return pl.pallas_call(
        body, name="dt_bwd", grid=(t // tb,),
        in_specs=[pl.BlockSpec((tb, LANE), lambda i: (i, dtcol)), _full(bias.shape), _full(e.shape),
                  pl.BlockSpec((tb, di), lambda i: (i, 0)), ANY],
        out_specs=[pl.BlockSpec((tb, tail), lambda i: (i, dtcol * LANE // tail)), _full(bias.shape)],
        out_shape=[jax.ShapeDtypeStruct(dproj.shape, dproj.dtype), jax.ShapeDtypeStruct(bias.shape, F32)],
        input_output_aliases={4: 0},
        compiler_params=_cparams(("arbitrary",)),
    )(proj, bias, e, ddte, dproj)


def head_expand(p, e, *, transpose=False, name):
    di = e.shape[1]

    def body(p_ref, e_ref, o_ref):
        if transpose:
            o_ref[...] = _sel_r_b(e_ref[...], p_ref[...])[0]
        else:
            o_ref[...] = sel_r(p_ref[...], e_ref[...])

    oshape = (SUBLANE, LANE) if transpose else (SUBLANE, di)
    return pl.pallas_call(
        body, name=name, in_specs=[_full(p.shape), _full(e.shape)], out_specs=_full(oshape),
        out_shape=jax.ShapeDtypeStruct(oshape, F32), compiler_params=_cparams(None), grid=(1,),
    )(p, e)


def _ssd_chunk(x, z, bm, cm, dte, st, alog, dskip, gn, tri, cmask, dmask, bd):
    L, gw = x.shape
    reps = gw // L
    a = dte * (-jnp.exp(alog))
    acs = sel_l(tri, a)
    acs_last = jnp.sum(a, axis=0, keepdims=True)
    arow = jnp.sum(acs * dmask, axis=0, keepdims=True)
    dtrow = jnp.sum(dte * dmask, axis=0, keepdims=True)
    cb = mm_nt(cm, jnp.concatenate([bm] * reps, axis=0))
    wts = cb * (jnp.exp(jnp.minimum(acs - arow, 0.0)) * cmask) * dtrow
    xbd = jnp.concatenate([x] * reps, axis=0) * bd
    xw = x * (jnp.exp(acs_last - acs) * dte)
    y = mm(wts, xbd) + mm(cm, st) * jnp.exp(acs) + dskip * x
    st_new = jnp.exp(acs_last) * st + mm_tn(bm, xw)
    return _rms(y * _silu(z), gn), st_new


def _ssd_dims(di):
    gw = di // SSD_GROUPS
    assert CHUNK == SSD_HEAD_DIM and gw % LANE == 0
    return gw, SSD_STATE


def _ssd_masks(gw):
    L = CHUNK
    r = jnp.arange(L)[:, None]
    c = jnp.arange(gw)[None, :]
    cmask = ((c % L) <= r).astype(F32)
    dmask = ((c % L) == r).astype(F32)
    rr = jnp.arange(gw)
    bd = ((rr[:, None] // L) == (rr[None, :] // L)).astype(F32)
    tri = (jnp.arange(L)[None, :] <= jnp.arange(L)[:, None]).astype(BF16)
    return tri, cmask, dmask, bd


def ssd_fwd(xs, proj, dte, alog_e, dskip_e, gn, carry=None):
    t, di = dte.shape
    gw, n = _ssd_dims(di)
    L, G = CHUNK, SSD_GROUPS
    nc = t // L
    masks = _ssd_masks(gw)
    cdim = xs.shape[1]

    def body(x_ref, z_ref, dt_ref, al_ref, ds_ref, gn_ref, tri_ref, cm_ref, dm_ref, bd_ref, y_ref, sp_ref, st):
        @pl.when(pl.program_id(0) == 0)
        def _():
            st[...] = jnp.zeros_like(st)

        for g in range(G):
            ch = slice(g * gw, (g + 1) * gw)
            s_prev = st[g]
            sp_ref[0, g] = s_prev
            y, s_new = _ssd_chunk(x_ref[:, ch], z_ref[:, ch], x_ref[:, di + g * n:di + (g + 1) * n],
                                  x_ref[:, di + (G + g) * n:di + (G + g + 1) * n], dt_ref[:, ch], s_prev,
                                  al_ref[0:1, ch], ds_ref[0:1, ch], gn_ref[:, ch], tri_ref[...], cm_ref[...],
                                  dm_ref[...], bd_ref[...])
            y_ref[:, ch] = y.astype(y_ref.dtype)
            st[g] = s_new

    row = lambda w: pl.BlockSpec((L, w), lambda c: (c, 0))
    return _pcall(
        body, carry=carry, name="ssd_fwd", grid=(nc,),
        in_specs=[row(cdim), row(di), row(di), _full(alog_e.shape), _full(dskip_e.shape), _full(gn.shape)]
        + [_full(m.shape) for m in masks],
        out_specs=[row(di), pl.BlockSpec((1, G, n, gw), lambda c: (c, 0, 0, 0))],
        out_shape=[jax.ShapeDtypeStruct((t, di), BF16), jax.ShapeDtypeStruct((nc, G, n, gw), F32)],
        scratch_shapes=[pltpu.VMEM((G, n, gw), F32)],
        compiler_params=_cparams(("arbitrary",)),
    )(xs, proj, dte, alog_e, dskip_e, gn, *masks)


def ssd_bwd(xs, proj, dte, alog_e, dskip_e, gn, sprev, dy, dproj_shape, carry=None):
    t, di = dte.shape
    gw, n = _ssd_dims(di)
    L, G = CHUNK, SSD_GROUPS
    nc = t // L
    masks = _ssd_masks(gw)
    cdim = xs.shape[1]

    def body(x_ref, z_ref, dt_ref, al_ref, ds_ref, gn_ref, tri_ref, cm_ref, dm_ref, bd_ref, sp_ref, dy_ref,
             dxs_ref, dz_ref, ddt_ref, dal_ref, dds_ref, dgn_ref, dst):
        @pl.when(pl.program_id(0) == 0)
        def _():
            dst[...] = jnp.zeros_like(dst)
            dal_ref[...] = jnp.zeros_like(dal_ref)
            dds_ref[...] = jnp.zeros_like(dds_ref)
            dgn_ref[...] = jnp.zeros_like(dgn_ref)

        f = functools.partial(_ssd_chunk, tri=tri_ref[...], cmask=cm_ref[...], dmask=dm_ref[...], bd=bd_ref[...])
        for g in range(G):
            ch = slice(g * gw, (g + 1) * gw)
            bs = slice(di + g * n, di + (g + 1) * n)
            cs = slice(di + (G + g) * n, di + (G + g + 1) * n)
            _, vjp = jax.vjp(f, x_ref[:, ch], z_ref[:, ch], x_ref[:, bs], x_ref[:, cs], dt_ref[:, ch], sp_ref[0, g],
                             al_ref[0:1, ch], ds_ref[0:1, ch], gn_ref[:, ch])
            dx, dz, db, dc, ddt, ds, dal, dds, dgn = vjp((dy_ref[:, ch], dst[g]))
            dxs_ref[:, ch] = dx
            dxs_ref[:, bs] = db
            dxs_ref[:, cs] = dc
            dz_ref[:, ch] = dz.astype(dz_ref.dtype)
            ddt_ref[:, ch] = ddt
            dst[g] = ds
            dal_ref[:, ch] += dal
            dds_ref[:, ch] += dds
            dgn_ref[:, ch] += dgn

    row = lambda w: pl.BlockSpec((L, w), lambda c: (nc - 1 - c, 0))
    acc = _full((1, di))
    acc_shape = jax.ShapeDtypeStruct((1, di), F32)
    return _pcall(
        body, carry=carry, name="ssd_bwd", grid=(nc,),
        in_specs=[row(cdim), row(di), row(di), _full(alog_e.shape), _full(dskip_e.shape), _full(gn.shape)]
        + [_full(m.shape) for m in masks]
        + [pl.BlockSpec((1, G, n, gw), lambda c: (nc - 1 - c, 0, 0, 0)), row(di)],
        out_specs=[row(cdim), row(di), row(di), acc, acc, acc],
        out_shape=[jax.ShapeDtypeStruct((t, cdim), F32), jax.ShapeDtypeStruct(dproj_shape, BF16),
                   jax.ShapeDtypeStruct((t, di), F32), acc_shape, acc_shape, acc_shape],
        scratch_shapes=[pltpu.VMEM((G, n, gw), F32)],
        compiler_params=_cparams(("arbitrary",)),
    )(xs, proj, dte, alog_e, dskip_e, gn, *masks, sprev, dy)


def _rows2d(a):
    return a.reshape(-1, a.shape[-1])


def _row_tile(rows, cols):
    cap = max(SUBLANE, (1 << 19) // max(cols, 1))
    step = 2 * SUBLANE
    for c in range(min(cap, rows) // step * step, 0, -step):
        if rows % c == 0:
            return c
    return rows


def chip_sum(g, r, core, *, name):
    shape = r.shape
    cols = shape[-1]
    g4 = g.reshape(4, 2, -1, cols)
    r3 = r.reshape(4, -1, cols)
    rows = r3.shape[1]
    tr = _row_tile(rows, cols)

    def body(c_ref, g_ref, r_ref, o_ref):
        o_ref[...] = (g_ref[...].astype(F32) + r_ref[...].astype(F32)).astype(o_ref.dtype)

    out = pl.pallas_call(
        body, name=name,
        grid_spec=pltpu.PrefetchScalarGridSpec(
            num_scalar_prefetch=1, grid=(4, rows // tr),
            in_specs=[pl.BlockSpec((None, None, tr, cols), lambda j, i, c: (j, c[0], i, 0)),
                      pl.BlockSpec((None, tr, cols), lambda j, i, c: (j, i, 0))],
            out_specs=pl.BlockSpec((None, tr, cols), lambda j, i, c: (j, i, 0))),
        out_shape=jax.ShapeDtypeStruct(r3.shape, BF16), compiler_params=_cparams(("parallel", "parallel")),
    )(core.reshape(1).astype(jnp.int32), g4, r3)
    return out.reshape(shape)


def mesh_sum(p, r, chip, core, *, name):
    shape = p.shape[1:]
    cols = shape[-1]
    p3 = p.reshape(4, -1, cols)
    r3 = r.reshape(3, -1, cols)
    rows = p3.shape[1]
    tr = _row_tile(rows, 2 * cols)

    def body(c_ref, p_ref, r_ref, o_ref):
        o_ref[...] = ((p_ref[...].astype(F32) + r_ref[0].astype(F32)) + r_ref[1].astype(F32)) + r_ref[2].astype(F32)

    out = pl.pallas_call(
        body, name=name,
        grid_spec=pltpu.PrefetchScalarGridSpec(
            num_scalar_prefetch=1, grid=(rows // tr,),
            in_specs=[pl.BlockSpec((None, tr, cols), lambda i, c: (c[0], i, 0)),
                      pl.BlockSpec((3, tr, cols), lambda i, c: (0, i, 0))],
            out_specs=pl.BlockSpec((None, tr, cols), lambda i, c: (c[1], i, 0))),
        out_shape=jax.ShapeDtypeStruct((2, rows, cols), F32), compiler_params=_cparams(("parallel",)),
    )(jnp.stack([chip, core]).astype(jnp.int32), p3, r3)
    return out.reshape((2,) + shape)


def sum_leading(x, *, name):
    k, r, c = x.shape
    tr = _row_tile(r, c * k)
    def body(x_ref, o_ref):
        acc = x_ref[0]
        for i in range(1, k):
            acc = acc + x_ref[i]
        o_ref[...] = acc

    return pl.pallas_call(
        body, name=name, grid=(r // tr,), in_specs=[pl.BlockSpec((k, tr, c), lambda i: (0, i, 0))],
        out_specs=pl.BlockSpec((tr, c), lambda i: (i, 0)),
        out_shape=jax.ShapeDtypeStruct((r, c), F32), compiler_params=_cparams(("parallel",)),
    )(x)


def adamw(w, gs, m, v, *, name):
    shape = w.shape
    w3, m3, v3 = (a.reshape((-1,) + a.shape[-2:]) for a in (w, m, v))
    nl, r, c = w3.shape
    assert len(gs) == nl
    tr = _row_tile(r, 2 * c)
    tc = c
    if tr == r and r * c > (1 << 19):
        tc = next(t for t in (1024, 512, 256, 128) if c % t == 0 and r * t <= (1 << 19))
    c1 = 1.0 - ADAM_B1 ** ADAM_STEP
    c2 = 1.0 - ADAM_B2 ** ADAM_STEP

    def body(w_ref, g_ref, m_ref, v_ref, *rest):
        go_ref, d_ref, mo_ref, vo_ref = rest[-4:]
        gv = g_ref[...]
        mn = ADAM_B1 * m_ref[...] + (1.0 - ADAM_B1) * gv
        vn = ADAM_B2 * v_ref[...] + (1.0 - ADAM_B2) * (gv * gv)
        go_ref[...] = gv
        d_ref[...] = -ADAM_LR * ((mn / c1) / (jnp.sqrt(vn / c2) + ADAM_EPS) + ADAM_WD * w_ref[...])
        mo_ref[...] = mn
        vo_ref[...] = vn

    outs = None
    for l, g in enumerate(gs):
        layer = pl.BlockSpec((None, tr, tc), lambda i, j, l=l: (l, i, j))
        prev = [] if outs is None else list(outs)
        outs = pl.pallas_call(
            functools.partial(body), name=f"{name}_{l}", grid=(r // tr, c // tc),
            in_specs=[layer, pl.BlockSpec((tr, tc), lambda i, j: (i, j)), layer, layer] + [ANY] * len(prev),
            out_specs=[layer] * 4, out_shape=[jax.ShapeDtypeStruct((nl, r, c), F32)] * 4,
            input_output_aliases={4 + k: k for k in range(len(prev))},
            compiler_params=_cparams(("parallel", "parallel")),
        )(w3, g.reshape(r, c), m3, v3, *prev)
    return tuple(o.reshape(shape) for o in outs)


ANY = pl.BlockSpec(memory_space=pl.ANY)


def _place():
    x, y, c = lax.axis_index("x"), lax.axis_index("y"), lax.axis_index("c")
    chips = [(1 - x, y), (x, 1 - y), (1 - x, 1 - y)]
    return x, y, c, chips


def gather8(block):
    m, n = block.shape

    def body(x_ref, out_ref, send_sems, recv_sems, local_sem):
        x, y, c, chips = _place()
        me, sibling = (x, y, c), (x, y, 1 - c)

        def rows(px, py, pc):
            return out_ref.at[4 * px + 2 * py + pc]

        def copy(k, blk, to, src=None):
            return pltpu.make_async_remote_copy(
                src_ref=rows(*blk) if src is None else src, dst_ref=rows(*blk), send_sem=send_sems.at[k],
                recv_sem=recv_sems.at[k], device_id=to, device_id_type=MESH)

        mine = pltpu.make_async_copy(x_ref, rows(*me), local_sem)
        mine.start()
        first = [copy(0, me, sibling, src=x_ref)]
        first += [copy(1 + j, me, (*chip, c), src=x_ref) for j, chip in enumerate(chips)]
        for cp in first:
            cp.start()
        passed = [copy(4 + j, (*chip, c), sibling) for j, chip in enumerate(chips)]
        for j, chip in enumerate(chips):
            copy(1 + j, (*chip, c), me).wait_recv()
            passed[j].start()
        copy(0, sibling, me).wait_recv()
        for j, chip in enumerate(chips):
            copy(4 + j, (*chip, 1 - c), me).wait_recv()
        for cp in first + passed:
            cp.wait_send()
        mine.wait()

    return pl.pallas_call(
        body, name="gather8",
        out_shape=jax.ShapeDtypeStruct((8, m, n), block.dtype),
        in_specs=[pl.BlockSpec(memory_space=pltpu.VMEM)],
        out_specs=pl.BlockSpec(memory_space=pltpu.VMEM),
        scratch_shapes=[pltpu.SemaphoreType.DMA((7,)), pltpu.SemaphoreType.DMA((7,)), pltpu.SemaphoreType.DMA],
        compiler_params=pltpu.CompilerParams(vmem_limit_bytes=VMEM_LIMIT),
    )(block)


def gather_weights(shards):
    n = len(shards)

    def copy(ins, outs, send, recv, base, a, k, chip_idx, half, to, src=None):
        dst = outs[a].at[chip_idx, half]
        return pltpu.make_async_remote_copy(
            src_ref=dst if src is None else src, dst_ref=dst, send_sem=send.at[base + 6 * a + k],
            recv_sem=recv.at[base + 6 * a + k], device_id=to, device_id_type=MESH)

    def first(ins, outs, send, recv, base):
        x, y, c, chips = _place()
        return [copy(ins, outs, send, recv, base, a, j, 2 * x + y, c, (*chip, c), src=ins[a].at[c])
                for a in range(n) for j, chip in enumerate(chips)]

    def start(ins, outs, send, recv, base):
        for cp in first(ins, outs, send, recv, base):
            cp.start()

    def finish(ins, outs, send, recv, base):
        x, y, c, chips = _place()
        sibling = (x, y, 1 - c)
        passed = []
        for a in range(n):
            for j, (cx, cy) in enumerate(chips):
                copy(ins, outs, send, recv, base, a, j, 2 * cx + cy, c, (cx, cy, c)).wait_recv()
                fw = copy(ins, outs, send, recv, base, a, 3 + j, 2 * cx + cy, c, sibling)
                fw.start()
                passed.append(fw)
        for a in range(n):
            for j, (cx, cy) in enumerate(chips):
                copy(ins, outs, send, recv, base, a, 3 + j, 2 * cx + cy, 1 - c, sibling).wait_recv()
        for cp in first(ins, outs, send, recv, base) + passed:
            cp.wait_send()

    return Carry(shards, [jax.ShapeDtypeStruct((4,) + s.shape, s.dtype) for s in shards], 6 * n, start, finish)


def exchange_halves(grads):
    n = len(grads)

    def copies(ins, outs, send, recv, base):
        x, y, c, _ = _place()
        return [pltpu.make_async_remote_copy(
            src_ref=ins[a].at[j, 1 - c], dst_ref=outs[a].at[j], send_sem=send.at[base + 4 * a + j],
            recv_sem=recv.at[base + 4 * a + j], device_id=(x, y, 1 - c), device_id_type=MESH)
            for a in range(n) for j in range(4)]

    def start(*args):
        for cp in copies(*args):
            cp.start()

    def finish(*args):
        for cp in copies(*args):
            cp.wait()

    return Carry(grads, [jax.ShapeDtypeStruct((4,) + g.shape[2:], g.dtype) for g in grads], 4 * n, start, finish)


def scatter_chips(parts):
    n = len(parts)

    def copies(ins, outs, send, recv, base):
        x, y, c, chips = _place()
        return [pltpu.make_async_remote_copy(
            src_ref=ins[a].at[2 * cx + cy], dst_ref=outs[a].at[j], send_sem=send.at[base + 3 * a + j],
            recv_sem=recv.at[base + 3 * a + j], device_id=(cx, cy, c), device_id_type=MESH)
            for a in range(n) for j, (cx, cy) in enumerate(chips)]

    def start(*args):
        for cp in copies(*args):
            cp.start()

    def finish(*args):
        for cp in copies(*args):
            cp.wait()

    return Carry(parts, [jax.ShapeDtypeStruct((3,) + p.shape[1:], p.dtype) for p in parts], 3 * n, start, finish)


def join_halves(bufs):
    n = len(bufs)

    def copy(outs, send, recv, base, a, half):
        x, y, c, _ = _place()
        return pltpu.make_async_remote_copy(
            src_ref=outs[a].at[c], dst_ref=outs[a].at[c if half is None else half], send_sem=send.at[base + a],
            recv_sem=recv.at[base + a], device_id=(x, y, 1 - c), device_id_type=MESH)

    def start(ins, outs, send, recv, base):
        for a in range(n):
            copy(outs, send, recv, base, a, None).start()

    def finish(ins, outs, send, recv, base):
        c = lax.axis_index("c")
        for a in range(n):
            copy(outs, send, recv, base, a, 1 - c).wait_recv()
        for a in range(n):
            copy(outs, send, recv, base, a, None).wait_send()

    return Carry(bufs, [jax.ShapeDtypeStruct(h.shape, h.dtype) for h in bufs], n, start, finish,
                 aliases={a: a for a in range(n)})


def run_comm(carry, *, name):
    k_in, k_out = len(carry.arrays), len(carry.out_shape)

    def body(*refs):
        ins, outs = refs[:k_in], refs[k_in:k_in + k_out]
        send, recv = refs[-2:]
        carry.start(ins, outs, send, recv, 0)
        carry.finish(ins, outs, send, recv, 0)

    return pl.pallas_call(
        body, name=name, out_shape=carry.out_shape, in_specs=[ANY] * k_in, out_specs=[ANY] * k_out,
        scratch_shapes=[pltpu.SemaphoreType.DMA((carry.n_sems,))] * 2, input_output_aliases=carry.aliases,
    )(*carry.arrays)


INPUTS = ['x'] + WEIGHTS + ['loss_target'] + ['m_' + n for n in WEIGHTS] + ['v_' + n for n in WEIGHTS]


def _round_up(n, m):
    return -(-n // m) * m


def _pack(arrs):
    flat = jnp.concatenate([a.reshape(-1) for a in arrs])
    n = _round_up(flat.shape[0], 512 * LANE)
    return jnp.pad(flat, (0, n - flat.shape[0])).reshape(-1, LANE)


def _unpack(block, shapes):
    flat = block.reshape(-1)
    out, o = [], 0
    for s in shapes:
        n = math.prod(s)
        out.append(flat[o:o + n].reshape(s))
        o += n
    return out


def _cols(g):
    return g.transpose(1, 0, 2).reshape(g.shape[1], -1)


def _uncols(w):
    return w.reshape(w.shape[0], 4, -1).transpose(1, 0, 2)


def _pad_cols(w, total):
    return jnp.pad(w, ((0, 0), (0, total - w.shape[1])))


def kernel(*args):
    a = dict(zip(INPUTS, args))
    x, tgt = a['x'][0], a['loss_target'][0]
    t, d = x.shape
    xi, yi, ci = lax.axis_index("x"), lax.axis_index("y"), lax.axis_index("c")
    chip = 2 * xi + yi
    dk, dv, hk, hv = _gla_dims(d)
    lw = d // 2
    di = 2 * d
    nh = di // SSD_HEAD_DIM
    gn_w = SSD_GROUPS * SSD_STATE
    conv_dim = di + 2 * gn_w
    rank = GLA_GATE_RANK
    wq = 2 * dk + 2 * dv
    gla_w = _round_up(wq + LANE, 2 * lw)
    ev_tot = gla_w + 2 * lw
    od_used = di + conv_dim + nh
    od_tot = _round_up(di + conv_dim + _round_up(nh, LANE), 512)
    glr_col, xcol, dtcol = wq // LANE, gla_w // lw, (di + conv_dim) // LANE
    assert ev_tot % 512 == 0 and nh <= LANE

    def halves(w):
        w = w.astype(BF16)
        return w.reshape((2, w.shape[0] // 2) + w.shape[1:])

    own = {'ev_w_in': halves(a['ev_w_in'][0]), 'ev_w_out': halves(a['ev_w_out'][0]),
           'od_w_in_a': halves(a['od_w_in'][0][:d // 2]), 'od_w_in_b': halves(a['od_w_in'][0][d // 2:]),
           'od_w_out': halves(a['od_w_out'][0])}
    for l in range(2):
        own[f'gate{l}'], own[f'up{l}'] = halves(a['ffn_w_gate'][l]), halves(a['ffn_w_up'][l])
        own[f'down{l}'] = halves(a['ffn_w_down'][l])

    def gather(*units):
        return gather_weights([own[u] for u in units])

    def filled(unit, g):
        g = lax.dynamic_update_index_in_dim(g, own[unit], chip, 0)
        return g.reshape((4, 2 * g.shape[2]) + g.shape[3:])

    g_ev_in, g_ev_out = run_comm(gather('ev_w_in', 'ev_w_out'), name="gather_ev")
    w_ev_in = _cols(filled('ev_w_in', g_ev_in))
    cuts = [dk, 2 * dk, 2 * dk + dv, wq, wq + rank, wq + rank + lw]
    sq, sk, sv, sg, sglr, sxb, sgb = jnp.split(w_ev_in, cuts, axis=1)
    w_ev_in_p = jnp.concatenate([_pad_cols(jnp.concatenate([sq, sk, sv, sg, sglr], axis=1), gla_w), sxb, sgb], axis=1)
    w_ev_out = filled('ev_w_out', g_ev_out).reshape(-1, d)
    w_gate, w_up, w_down = [None, None], [None, None], [None, None]

    sh_names = list(SMALL_SHARDED)
    sh_shapes = [a[n].shape for n in sh_names]
    g8 = gather8(_pack([a[n] for n in sh_names]))
    per_chip = [_unpack(g8[2 * j], sh_shapes) for j in range(4)]
    full = {n: jnp.concatenate([per_chip[j][i] for j in range(4)], axis=SMALL_SHARDED[n])
            for i, n in enumerate(sh_names)}

    wg_p = jnp.zeros((LANE, dk), F32).at[:rank].set(full['ev_gla_w_gate'][0])
    bg, wn = a['ev_gla_b_gate'], a['ev_gla_w_onorm']
    lru_p = [full['ev_lru_conv_w'][0], a['ev_lru_conv_b'], a['ev_lru_w_a'][0], a['ev_lru_b_a'], a['ev_lru_w_i'][0],
             a['ev_lru_b_i'], a['ev_lru_lam']]
    od_cw, od_cb, od_gn = full['od_conv_w'][0], full['od_conv_b'], full['od_gnorm']
    heads = jnp.arange(LANE)[:, None]
    e_mat = ((jnp.arange(di)[None, :] // SSD_HEAD_DIM == heads) & (heads < nh)).astype(BF16)
    row8 = lambda p: jnp.zeros((SUBLANE, LANE), F32).at[0, :nh].set(p[0])
    dt_bias_p = jnp.zeros((1, LANE), F32).at[0, :nh].set(a['od_dt_bias'][0])
    alog_e = head_expand(row8(a['od_a_log']), e_mat, name="expand_a_log")
    dskip_e = head_expand(row8(a['od_d_skip']), e_mat, name="expand_d_skip")

    h0 = rms_fwd(x, a['ev_norm'], name="rms_ev")
    proj, (g,) = matmul(h0, w_ev_in_p, name="ev_in", carry=gather('gate0'))
    w_gate[0] = filled('gate0', g)
    (o_gla, sp_gla), (g,) = gla_fwd(proj, glr_col, wg_p, bg, wn, dv, carry=gather('up0'))
    w_up[0] = filled('up0', g)
    o_lru, hin = lru_fwd(proj, xcol, lw, *lru_p)
    x1 = matmul(o_gla, w_ev_out[:dv], add=x, name="ev_out_a")
    x1 = matmul(o_lru, w_ev_out[dv:], add=x1, name="ev_out_b")

    h1 = rms_fwd(x1, a['ffn_norm'][0:1], name="rms_ffn0")
    gate0, (g,) = matmul(h1, w_gate[0], name="ffn0_gate", carry=gather('down0'))
    w_down[0] = filled('down0', g).reshape(-1, d)
    swi = dict(epi=_swi_fwd_epi, epi_out=(F32, BF16))
    (up0, act0), (g_a,) = matmul(h1, w_up[0], name="ffn0_up", epi_in=(gate0,), carry=gather('od_w_in_a'), **swi)
    x2, (g_b,) = matmul(act0, w_down[0], add=x1, name="ffn0_down", carry=gather('od_w_in_b'))
    w_od_in = jnp.concatenate([filled('od_w_in_a', g_a), filled('od_w_in_b', g_b)], axis=1)
    w_od_in_p = _pad_cols(_cols(w_od_in), od_tot)

    h2 = rms_fwd(x2, full['od_norm'], name="rms_od")
    proj2, (g, g1) = matmul(h2, w_od_in_p, name="od_in", carry=gather('od_w_out', 'gate1'))
    w_od_out, w_gate[1] = filled('od_w_out', g).reshape(-1, d), filled('gate1', g1)
    xs, (g,) = conv_silu_fwd(proj2, di, conv_dim, od_cw, od_cb, carry=gather('up1'))
    w_up[1] = filled('up1', g)
    dte = dt_fwd(proj2, dtcol, dt_bias_p, e_mat)
    (y_ssd, sp_ssd), (g,) = ssd_fwd(xs, proj2, dte, alog_e, dskip_e, od_gn, carry=gather('down1'))
    w_down[1] = filled('down1', g).reshape(-1, d)
    x3 = matmul(y_ssd, w_od_out, add=x2, name="od_out")
    h3 = rms_fwd(x3, a['ffn_norm'][1:2], name="rms_ffn1")
    gate1 = matmul(h3, w_gate[1], name="ffn1_gate")
    up1, act1 = matmul(h3, w_up[1], name="ffn1_up", epi_in=(gate1,), **swi)
    x4 = matmul(act1, w_down[1], add=x3, name="ffn1_down")
    loss_p, dx4, dx4b, d_final = loss_head(x4, a['final_norm'][None], tgt, name="loss_head")

    grads, from_sib, part, from_chips = {}, {}, {}, {}

    def rows4(dw):
        return dw.reshape((4, 2, dw.shape[0] // 8) + dw.shape[1:])

    def cols4(dw):
        return dw.reshape((4, 2, dw.shape[1] // 2) + dw.shape[2:])

    def exchange(*units):
        return exchange_halves([grads[u] for u in units])

    def scatter(*units):
        return scatter_chips([part[u] for u in units])

    def sum_chip(u):
        part[u] = chip_sum(grads[u], from_sib[u], ci, name=f"chip_sum_{u}")

    def ffn_bwd(dxo, dxob, xin, h, gate, up, act, l, first_carry, first_units):
        dn, gt, up_ = f'down{l}', f'gate{l}', f'up{l}'
        dgu = matmul(dxob, w_down[l], tb=True, name=f"ffn{l}_d_act", carry=first_carry, epi=_swi_bwd_epi,
                     epi_in=(gate, up), epi_out=(BF16, BF16))
        (dg, du), got = dgu if first_units else (dgu, ())
        for u, r in zip(first_units, got):
            from_sib[u] = r
            sum_chip(u)
        grads[dn] = rows4(matmul(act, dxob, ta=True, out_dtype=BF16, name=f"ffn{l}_dw_down"))
        dh, (from_sib[dn],) = matmul(dg, w_gate[l], tb=True, name=f"ffn{l}_dh_gate", carry=exchange(dn))
        sum_chip(dn)
        dh = matmul(du, w_up[l], tb=True, add=dh, name=f"ffn{l}_dh_up")
        d_gate, (from_chips[dn],) = matmul(h, dg, ta=True, out_dtype=BF16, out_shards=True, name=f"ffn{l}_dw_gate",
                                           carry=scatter(dn))
        grads[gt] = cols4(d_gate)
        d_up, (from_sib[gt],) = matmul(h, du, ta=True, out_dtype=BF16, out_shards=True, name=f"ffn{l}_dw_up",
                                       carry=exchange(gt))
        grads[up_] = cols4(d_up)
        return rms_bwd(xin, a['ffn_norm'][l:l + 1], dh, dxo, name=f"rms_ffn{l}_bwd")

    dx3, dx3b, d_fn1 = ffn_bwd(dx4, dx4b, x3, h3, gate1, up1, act1, 1, None, ())
    dy, (from_sib['up1'],) = matmul(dx3b, w_od_out, tb=True, name="od_out_dy", carry=exchange('up1'))
    sum_chip('gate1')
    sum_chip('up1')
    grads['od_w_out'] = rows4(matmul(y_ssd, dx3b, ta=True, out_dtype=BF16, name="od_out_dw"))
    (dxs, dproj2, ddte, dal, dds, dgn), (from_chips['gate1'], from_chips['up1'], from_sib['od_w_out']) = ssd_bwd(
        xs, proj2, dte, alog_e, dskip_e, od_gn, sp_ssd, dy, proj2.shape,
        carry=merge_carries(scatter('gate1', 'up1'), exchange('od_w_out')))
    sum_chip('od_w_out')
    (dproj2, d_od_cw, d_od_cb), (from_chips['od_w_out'],) = conv_silu_bwd(
        proj2, di, conv_dim, od_cw, od_cb, dxs, dproj2, carry=scatter('od_w_out'))
    dproj2, d_dt_bias = dt_bwd(proj2, dtcol, dt_bias_p, e_mat, ddte, dproj2)
    dh2 = matmul(dproj2, w_od_in_p, tb=True, name="od_in_dh")
    d_od_in = matmul(h2, dproj2, ta=True, out_dtype=BF16, name="od_in_dw")[:, :od_used]
    grads['od_w_in'] = cols4(_uncols(d_od_in))
    dx2, dx2b, d_od_norm = rms_bwd(x2, full['od_norm'], dh2, dx3, name="rms_od_bwd")
    to8 = lambda acc: jnp.zeros((SUBLANE, di), F32).at[0].set(acc.reshape(-1))
    d_a_log = head_expand(to8(dal), e_mat, transpose=True, name="reduce_a_log")[0:1, :nh]
    d_d_skip = head_expand(to8(dds), e_mat, transpose=True, name="reduce_d_skip")[0:1, :nh]

    dx1, dx1b, d_fn0 = ffn_bwd(dx2, dx2b, x1, h1, gate0, up0, act0, 0, exchange('od_w_in'), ('od_w_in',))
    dmix, (from_sib['up0'],) = matmul(dx1b, w_ev_out, tb=True, name="ev_out_dmix", carry=exchange('up0'))
    sum_chip('gate0')
    sum_chip('up0')
    grads['ev_w_out'] = rows4(jnp.concatenate([matmul(o_gla, dx1b, ta=True, out_dtype=BF16, name="ev_out_dw_a"),
                                               matmul(o_lru, dx1b, ta=True, out_dtype=BF16, name="ev_out_dw_b")], axis=0))
    (dproj, d_wg, d_bg, d_wn), (from_chips['od_w_in'], from_sib['ev_w_out']) = gla_bwd(
        proj, glr_col, wg_p, bg, wn, sp_gla, dmix, dv, gla_w,
        carry=merge_carries(scatter('od_w_in'), exchange('ev_w_out')))
    sum_chip('ev_w_out')
    (dproj, *d_lru), (from_chips['ev_w_out'], from_chips['gate0']) = lru_bwd(
        proj, xcol, lw, *lru_p, hin, dmix, 1, dproj, carry=scatter('ev_w_out', 'gate0'))
    d_ev_in_p, (from_chips['up0'],) = matmul(h0, dproj, ta=True, out_dtype=BF16, name="ev_in_dw", carry=scatter('up0'))
    d_ev_in = jnp.concatenate([d_ev_in_p[:, :wq + rank], d_ev_in_p[:, gla_w:]], axis=1)
    grads['ev_w_in'] = cols4(_uncols(d_ev_in))
    (from_sib['ev_w_in'],) = run_comm(exchange('ev_w_in'), name="exchange_ev_in")
    sum_chip('ev_w_in')
    dh0, (from_chips['ev_w_in'],) = matmul(dproj, w_ev_in_p, tb=True, name="ev_in_dh", carry=scatter('ev_w_in'))
    dx0, _, d_ev_norm = rms_bwd(x, a['ev_norm'], dh0, dx1, name="rms_ev_bwd")

    units = list(grads)
    half = [mesh_sum(part[u], from_chips[u], chip, ci, name=f"mesh_sum_{u}") for u in units]
    done = dict(zip(units, run_comm(join_halves(half), name="join_halves")))
    layers = {n: [done[n]] for n in ('ev_w_in', 'ev_w_out', 'od_w_in', 'od_w_out')}
    layers.update({f'ffn_w_{u}': [done[f'{u}0'], done[f'{u}1']] for u in ('gate', 'up', 'down')})
    grad, delta, new_m, new_v = {}, {}, {}, {}
    for n in BIG:
        flip = a[n].shape[-1] % LANE != 0
        tr_ = (lambda t: jnp.swapaxes(t, -1, -2)) if flip else (lambda t: t)
        gs = [tr_(g.reshape(a[n].shape[-2:])) for g in layers[n]]
        outs = adamw(tr_(a[n]), gs, tr_(a['m_' + n]), tr_(a['v_' + n]), name=f"adamw_{n}")
        grad[n], delta[n], new_m[n], new_v[n] = (tr_(o) for o in outs)

    small_g = {
        'ev_norm': d_ev_norm, 'ev_gla_w_gate': d_wg[:rank][None], 'ev_gla_b_gate': d_bg, 'ev_gla_w_onorm': d_wn,
        'ev_lru_conv_w': d_lru[0][None], 'ev_lru_conv_b': d_lru[1], 'ev_lru_w_a': d_lru[2][None],
        'ev_lru_b_a': d_lru[3], 'ev_lru_w_i': d_lru[4][None], 'ev_lru_b_i': d_lru[5], 'ev_lru_lam': d_lru[6],
        'od_norm': d_od_norm, 'od_conv_w': d_od_cw[None], 'od_conv_b': d_od_cb, 'od_dt_bias': d_dt_bias[:, :nh],
        'od_a_log': d_a_log, 'od_d_skip': d_d_skip, 'od_gnorm': dgn.reshape(1, di),
        'ffn_norm': jnp.concatenate([d_fn0, d_fn1], axis=0), 'final_norm': d_final[0],
    }
    full_shapes = [small_g[n].shape for n in SMALL]
    summed = sum_leading(gather8(_pack([small_g[n] for n in SMALL])), name="sum_devices")
    for n, g in zip(SMALL, _unpack(summed, full_shapes)):
        if n in SMALL_SHARDED:
            ax = SMALL_SHARDED[n]
            sz = a[n].shape[ax]
            g = lax.dynamic_slice_in_dim(g, chip * sz, sz, axis=ax)
        grad[n] = g

    shapes = [a[n].shape for n in SMALL]
    packed = [_pack([src[n] if pre is None else a[pre + n] for n in SMALL])
              for src, pre in ((a, None), (grad, None), (None, 'm_'), (None, 'v_'))]
    small_out = adamw(packed[0], [packed[1]], packed[2], packed[3], name="adamw_small")
    for outd, blk in zip((delta, new_m, new_v), small_out[1:]):
        outd.update(zip(SMALL, _unpack(blk, shapes)))

    loss = lax.psum(loss_p[0, 0], ("x", "y", "c"))
    return (loss, dx0[None], *[grad[n] for n in WEIGHTS], *[delta[n] for n in WEIGHTS],
            *[new_m[n] for n in WEIGHTS], *[new_v[n] for n in WEIGHTS])
```

```python
import functools
import math

import jax
import jax.numpy as jnp
from jax import lax
from jax.experimental import pallas as pl
from jax.experimental.pallas import tpu as pltpu

F32 = jnp.float32
BF16 = jnp.bfloat16
MXU_DTYPE = jnp.bfloat16

NORM_EPS = 1e-6
CONV_WIDTH = 4
GLA_HEADS = 4
GLA_GATE_RANK = 16
GLA_GATE_NORM = 16.0
CHUNK = 64
LRU_BLOCK = 128
LRU_C = 8.0
SSD_HEAD_DIM = 64
SSD_GROUPS = 8
SSD_STATE = 128
ADAM_LR, ADAM_B1, ADAM_B2, ADAM_EPS, ADAM_WD, ADAM_STEP = 0.001, 0.9, 0.999, 1e-08, 0.01, 10

LANE = 128
SUBLANE = 8
VMEM_LIMIT = 48 * 1024 * 1024
MAX_TK = 2816
MATMUL_VMEM_BUDGET = 45 * 1024 * 1024
MESH = pl.DeviceIdType.MESH

WEIGHTS = ['ev_norm', 'ev_w_in', 'ev_gla_w_gate', 'ev_gla_b_gate', 'ev_gla_w_onorm', 'ev_lru_conv_w', 'ev_lru_conv_b',
           'ev_lru_w_a', 'ev_lru_b_a', 'ev_lru_w_i', 'ev_lru_b_i', 'ev_lru_lam', 'ev_w_out', 'od_norm', 'od_w_in',
           'od_conv_w', 'od_conv_b', 'od_dt_bias', 'od_a_log', 'od_d_skip', 'od_gnorm', 'od_w_out', 'ffn_norm',
           'ffn_w_gate', 'ffn_w_up', 'ffn_w_down', 'final_norm']
BIG = ['ev_w_in', 'ev_w_out', 'od_w_in', 'od_w_out', 'ffn_w_gate', 'ffn_w_up', 'ffn_w_down']
SMALL_SHARDED = {'ev_gla_w_gate': 2, 'ev_lru_conv_w': 2, 'od_norm': 1, 'od_conv_w': 2, 'od_conv_b': 1, 'od_gnorm': 1}
SMALL = [n for n in WEIGHTS if n not in BIG]


def _cparams(sem=None, **kw):
    return pltpu.CompilerParams(dimension_semantics=sem, vmem_limit_bytes=VMEM_LIMIT, **kw)


def _full(shape):
    n = len(shape)
    return pl.BlockSpec(shape, lambda *_: (0,) * n)


ANY = pl.BlockSpec(memory_space=pl.ANY)


class Carry:
    def __init__(self, arrays, out_shape, n_sems, start, finish, aliases=None):
        self.arrays, self.out_shape, self.n_sems = list(arrays), list(out_shape), n_sems
        self.start, self.finish, self.aliases = start, finish, dict(aliases or {})


def merge_carries(*cs):
    cs = [c for c in cs if c is not None]
    if not cs:
        return None
    arrays = [a for c in cs for a in c.arrays]
    out_shape = [s for c in cs for s in c.out_shape]
    offs, i0, o0, s0 = [], 0, 0, 0
    aliases = {}
    for c in cs:
        offs.append((i0, o0, s0))
        aliases.update({i0 + i: o0 + o for i, o in c.aliases.items()})
        i0, o0, s0 = i0 + len(c.arrays), o0 + len(c.out_shape), s0 + c.n_sems

    def both(which):
        def run(ins, outs, send, recv, base):
            for c, (i, o, s) in zip(cs, offs):
                getattr(c, which)(ins[i:i + len(c.arrays)], outs[o:o + len(c.out_shape)], send, recv, base + s)
        return run

    return Carry(arrays, out_shape, s0, both("start"), both("finish"), aliases)


def _pcall(body, *, name, grid, in_specs, out_specs, out_shape, scratch_shapes=(), compiler_params, carry=None,
           input_output_aliases=None):
    aliases = dict(input_output_aliases or {})
    if carry is None:
        return pl.pallas_call(body, name=name, grid=grid, in_specs=in_specs, out_specs=out_specs, out_shape=out_shape,
                              scratch_shapes=list(scratch_shapes), compiler_params=compiler_params,
                              input_output_aliases=aliases)
    single = not isinstance(out_specs, (list, tuple))
    specs_o = [out_specs] if single else list(out_specs)
    shapes_o = [out_shape] if single else list(out_shape)
    n_in, n_out, k_in, k_out, n_scr = len(in_specs), len(specs_o), len(carry.arrays), len(carry.out_shape), len(scratch_shapes)

    def wrapped(*refs):
        ins, cins = refs[:n_in], refs[n_in:n_in + k_in]
        o0 = n_in + k_in
        outs, couts = refs[o0:o0 + n_out], refs[o0 + n_out:o0 + n_out + k_out]
        scr = refs[o0 + n_out + k_out:o0 + n_out + k_out + n_scr]
        send, recv = refs[-2:]
        ids = [pl.program_id(ax) for ax in range(len(grid))]
        first = functools.reduce(jnp.logical_and, [i == 0 for i in ids])
        last = functools.reduce(jnp.logical_and, [i == g - 1 for i, g in zip(ids, grid)])

        @pl.when(first)
        def _():
            carry.start(cins, couts, send, recv, 0)

        body(*ins, *outs, *scr)

        @pl.when(last)
        def _():
            carry.finish(cins, couts, send, recv, 0)

    aliases.update({n_in + i: n_out + o for i, o in carry.aliases.items()})
    call = pl.pallas_call(
        wrapped, name=name, grid=grid, in_specs=list(in_specs) + [ANY] * k_in, out_specs=specs_o + [ANY] * k_out,
        out_shape=shapes_o + carry.out_shape,
        scratch_shapes=list(scratch_shapes) + [pltpu.SemaphoreType.DMA((carry.n_sems,))] * 2,
        compiler_params=_cparams(("arbitrary",) * len(grid)), input_output_aliases=aliases)

    def run(*args):
        res = call(*args, *carry.arrays)
        main = res[:n_out]
        return (main[0] if single else list(main)), list(res[n_out:])

    return run


def _pick(dim, cands):
    for c in cands:
        if dim % c == 0:
            return c
    return dim


def _dims(a, ca, cb):
    nb = a.ndim - 2
    return (((ca + nb,), (cb + nb,)), (tuple(range(nb)), tuple(range(nb))))


def _dot(a, b, ca, cb):
    return lax.dot_general(a.astype(MXU_DTYPE), b.astype(MXU_DTYPE), _dims(a, ca, cb), preferred_element_type=F32)


@jax.custom_vjp
def mm(a, b):
    return _dot(a, b, 1, 0)


def _mm_f(a, b):
    return mm(a, b), (a, b)


def _mm_b(res, g):
    a, b = res
    return mm_nt(g, b).astype(a.dtype), mm_tn(a, g).astype(b.dtype)


@jax.custom_vjp
def mm_nt(a, b):
    return _dot(a, b, 1, 1)


def _mm_nt_f(a, b):
    return mm_nt(a, b), (a, b)


def _mm_nt_b(res, g):
    a, b = res
    return mm(g, b).astype(a.dtype), mm_tn(g, a).astype(b.dtype)


@jax.custom_vjp
def mm_tn(a, b):
    return _dot(a, b, 0, 0)


def _mm_tn_f(a, b):
    return mm_tn(a, b), (a, b)


def _mm_tn_b(res, g):
    a, b = res
    return mm_nt(b, g).astype(a.dtype), mm(a, g).astype(b.dtype)


mm.defvjp(_mm_f, _mm_b)
mm_nt.defvjp(_mm_nt_f, _mm_nt_b)
mm_tn.defvjp(_mm_tn_f, _mm_tn_b)


def _split3(a):
    h = a.astype(BF16)
    r = a - h.astype(F32)
    m = r.astype(BF16)
    l = (r - m.astype(F32)).astype(BF16)
    return h, m, l


def _exact_dot(t, a, ca, cb):
    out = None
    for p in _split3(a):
        d = lax.dot_general(t, p, _dims(a, ca, cb), preferred_element_type=F32)
        out = d if out is None else out + d
    return out


@jax.custom_vjp
def sel_l(t, a):
    return _exact_dot(t, a, 1, 0)


def _sel_l_f(t, a):
    return sel_l(t, a), t


def _sel_l_b(t, g):
    return jnp.zeros_like(t), _exact_dot(t, g, 0, 0)


sel_l.defvjp(_sel_l_f, _sel_l_b)


@jax.custom_vjp
def sel_r(a, t):
    out = None
    for p in _split3(a):
        d = lax.dot_general(p, t, (((1,), (0,)), ((), ())), preferred_element_type=F32)
        out = d if out is None else out + d
    return out


def _sel_r_f(a, t):
    return sel_r(a, t), t


def _sel_r_b(t, g):
    out = None
    for p in _split3(g):
        d = lax.dot_general(p, t, (((1,), (1,)), ((), ())), preferred_element_type=F32)
        out = d if out is None else out + d
    return out, jnp.zeros_like(t)


sel_r.defvjp(_sel_r_f, _sel_r_b)


def _sigmoid(x):
    return 1.0 / (1.0 + jnp.exp(-x))


def _silu(x):
    return x * _sigmoid(x)


def _softplus(x):
    return jnp.maximum(x, 0.0) + jnp.log(1.0 + jnp.exp(-jnp.abs(x)))


def _log_sigmoid(x):
    return -_softplus(-x)


def _gelu_tanh(x):
    c = math.sqrt(2.0 / math.pi)
    return 0.5 * x * (1.0 + jnp.tanh(c * (x + 0.044715 * (x * x * x))))


def _rms(x, w):
    return x * lax.rsqrt(jnp.mean(x * x, axis=-1, keepdims=True) + NORM_EPS) * w


def _tri(n, dtype=BF16):
    r = lax.broadcasted_iota(jnp.int32, (n, n), 0)
    c = lax.broadcasted_iota(jnp.int32, (n, n), 1)
    return (c <= r).astype(dtype)


def matmul(a, b, *, ta=False, tb=False, add=None, out_dtype=F32, out_shards=False, carry=None, name,
           epi=None, epi_in=(), epi_out=()):
    m, k = (a.shape[1], a.shape[0]) if ta else a.shape
    b_sh = b.ndim == 3
    if b_sh:
        s, br, bc = b.shape
        k2, n = (s * bc, br) if tb else (br, s * bc)
    else:
        k2, n = (b.shape[1], b.shape[0]) if tb else b.shape
    assert k == k2, (a.shape, b.shape, ta, tb)
    tk_opts = [bc] if b_sh and tb else [k] if k <= MAX_TK else \
        [c for c in range(MAX_TK, LANE - 1, -LANE) if k % c == 0][:1]
    n_add, n_x = int(add is not None), len(epi_in)
    out_dtypes = list(epi_out) if epi is not None else [out_dtype]
    n_o = len(out_dtypes)
    tn_opts = [bc] if b_sh and not tb else [n // 4] if out_shards else \
        [c for c in range(2048, LANE - 1, -LANE) if n % c == 0] or [n]
    tm_opts = [c for c in range(2048, LANE - 1, -LANE) if m % c == 0] or [m]
    sa, sb = a.dtype.itemsize, b.dtype.itemsize
    per_elem = sum(jnp.dtype(dt).itemsize for dt in out_dtypes) + 4 * n_add + sum(x.dtype.itemsize for x in epi_in)
    best = None
    for tk_ in tk_opts:
        for tm_ in tm_opts:
            for tn_ in tn_opts:
                vmem = 2 * (tm_ * tk_ * sa + tk_ * tn_ * sb) + tm_ * tn_ * (4 + 2 * per_elem)
                vmem += tm_ * tn_ * 4
                if vmem > MATMUL_VMEM_BUDGET and (tk_, tm_, tn_) != (tk_opts[-1], tm_opts[-1], tn_opts[-1]):
                    continue
                moved = m * k * sa * (1 if tk_ == k else n // tn_) + k * n * sb * (m // tm_)
                if best is None or (moved, -tk_, -tm_ * tn_) < best[0]:
                    best = ((moved, -tk_, -tm_ * tn_), tm_, tn_, tk_)
    _, tm, tn, tk = best
    nk = k // tk

    def body(*refs):
        a_ref, b_ref = refs[:2]
        x_refs = refs[2 + n_add:2 + n_add + n_x]
        o_refs = refs[2 + n_add + n_x:2 + n_add + n_x + n_o]
        acc = refs[-1]
        kk = pl.program_id(2)

        @pl.when(kk == 0)
        def _():
            acc[...] = jnp.zeros_like(acc)

        acc[...] += _dot(a_ref[...], b_ref[...], 0 if ta else 1, 1 if tb else 0)

        @pl.when(kk == nk - 1)
        def _():
            r = acc[...]
            if add is not None:
                r = r + refs[2][...].astype(F32)
            vals = (r,) if epi is None else epi(r, *[x[...] for x in x_refs])
            for o_ref, v in zip(o_refs, vals):
                o_ref[...] = v.astype(o_ref.dtype)

    a_spec = pl.BlockSpec((tk, tm), lambda i, j, kk: (kk, i)) if ta else pl.BlockSpec((tm, tk), lambda i, j, kk: (i, kk))
    if b_sh and tb:
        b_spec = pl.BlockSpec((None, tn, tk), lambda i, j, kk: (kk, j, 0))
    elif b_sh:
        b_spec = pl.BlockSpec((None, tk, tn), lambda i, j, kk: (j, kk, 0))
    elif tb:
        b_spec = pl.BlockSpec((tn, tk), lambda i, j, kk: (j, kk))
    else:
        b_spec = pl.BlockSpec((tk, tn), lambda i, j, kk: (kk, j))
    in_specs, args = [a_spec, b_spec], [a, b]
    tile = pl.BlockSpec((tm, tn), lambda i, j, kk: (i, j))
    for extra in ([add] if add is not None else []) + list(epi_in):
        in_specs.append(tile)
        args.append(extra)
    if out_shards:
        out_spec = pl.BlockSpec((None, tm, tn), lambda i, j, kk: (j, i, 0))
        out_shape = jax.ShapeDtypeStruct((4, m, tn), out_dtype)
    elif epi is not None:
        out_spec = [tile] * n_o
        out_shape = [jax.ShapeDtypeStruct((m, n), dt) for dt in out_dtypes]
    else:
        out_spec = tile
        out_shape = jax.ShapeDtypeStruct((m, n), out_dtype)
    return _pcall(
        body, name=name, grid=(m // tm, n // tn, nk), in_specs=in_specs, out_specs=out_spec, out_shape=out_shape,
        scratch_shapes=[pltpu.VMEM((tm, tn), F32)],
        compiler_params=_cparams(("parallel", "parallel", "arbitrary")), carry=carry,
    )(*args)


def rms_fwd(x, w, *, name):
    t, d = x.shape
    tb = _pick(t, (256, 128, 64))

    def body(x_ref, w_ref, o_ref):
        o_ref[...] = _rms(x_ref[...], w_ref[...]).astype(o_ref.dtype)

    return pl.pallas_call(
        body, name=name, grid=(t // tb,),
        in_specs=[pl.BlockSpec((tb, d), lambda i: (i, 0)), _full((1, d))],
        out_specs=pl.BlockSpec((tb, d), lambda i: (i, 0)),
        out_shape=jax.ShapeDtypeStruct((t, d), BF16),
        compiler_params=_cparams(("parallel",)),
    )(x, w)


def rms_bwd(x, w, dh, dres, *, name, carry=None):
    t, d = x.shape
    tb = _pick(t, (256, 128, 64))

    def body(x_ref, w_ref, dh_ref, dres_ref, dx_ref, dxb_ref, dw_ref):
        @pl.when(pl.program_id(0) == 0)
        def _():
            dw_ref[...] = jnp.zeros_like(dw_ref)

        _, vjp = jax.vjp(_rms, x_ref[...], w_ref[...])
        dx, dw = vjp(dh_ref[...].astype(F32))
        dx = dx + dres_ref[...]
        dx_ref[...] = dx
        dxb_ref[...] = dx.astype(dxb_ref.dtype)
        dw_ref[...] += dw

    row = pl.BlockSpec((tb, d), lambda i: (i, 0))
    return _pcall(
        body, name=name, grid=(t // tb,), carry=carry,
        in_specs=[row, _full((1, d)), row, row],
        out_specs=[row, row, _full((1, d))],
        out_shape=[jax.ShapeDtypeStruct((t, d), F32), jax.ShapeDtypeStruct((t, d), BF16),
                   jax.ShapeDtypeStruct((1, d), F32)],
        compiler_params=_cparams(("arbitrary",)),
    )(x, w, dh, dres)


def _swi(g, u):
    return _silu(g) * u


def _swi_fwd_epi(u, g):
    return u, _swi(g, u)


def _swi_bwd_epi(d, g, u):
    return jax.vjp(_swi, g, u)[1](d)


def loss_head(x, w, target, *, name):
    t, d = x.shape
    tb = _pick(t, (256, 128, 64))

    def f(xv, wv, tv):
        y = _rms(xv, wv)
        e = y - tv
        return 0.5 * jnp.sum(jnp.mean(e * e, axis=-1, keepdims=True), axis=0, keepdims=True)

    def body(x_ref, w_ref, t_ref, l_ref, dx_ref, dxb_ref, dw_ref):
        @pl.when(pl.program_id(0) == 0)
        def _():
            l_ref[...] = jnp.zeros_like(l_ref)
            dw_ref[...] = jnp.zeros_like(dw_ref)

        val, vjp = jax.vjp(lambda a, b: f(a, b, t_ref[...]), x_ref[...], w_ref[...])
        dx, dw = vjp(jnp.ones((1, 1), F32))
        l_ref[...] += jnp.broadcast_to(val, l_ref.shape)
        dx_ref[...] = dx
        dxb_ref[...] = dx.astype(dxb_ref.dtype)
        dw_ref[...] += dw

    row = pl.BlockSpec((tb, d), lambda i: (i, 0))
    return pl.pallas_call(
        body, name=name, grid=(t // tb,),
        in_specs=[row, _full((1, d)), row],
        out_specs=[_full((SUBLANE, LANE)), row, row, _full((1, d))],
        out_shape=[jax.ShapeDtypeStruct((SUBLANE, LANE), F32), jax.ShapeDtypeStruct((t, d), F32),
                   jax.ShapeDtypeStruct((t, d), BF16),
                   jax.ShapeDtypeStruct((1, d), F32)],
        compiler_params=_cparams(("arbitrary",)),
    )(x, w, target)


def _gla_chunk(q, k, v, g, glr, st, wg, bg, wn, tri):
    L, hk = q.shape[-2:]
    la = _log_sigmoid(mm(glr, wg) + bg) / GLA_GATE_NORM
    bcum = sel_l(jnp.broadcast_to(tri, la.shape[:-2] + tri.shape), la)
    b_last = jnp.sum(la, axis=-2, keepdims=True)
    rows = lax.broadcasted_iota(jnp.int32, (L, 1), 0)
    b_mid = jnp.sum(jnp.where(rows <= L // 2, la, 0.0), axis=-2, keepdims=True)
    qs = q * (hk ** -0.5)
    q_in = qs * jnp.exp(bcum - b_mid)
    k_in = k * jnp.exp(b_mid - bcum)
    scores = mm_nt(q_in, k_in) * tri.astype(F32)
    o_intra = mm(scores, v)
    k_st = k * jnp.exp(b_last - bcum)
    d_st = mm_tn(v, k_st)
    o_inter = mm_nt(qs * jnp.exp(bcum), st)
    st_new = jnp.exp(b_last) * st + d_st
    o = _rms(o_intra + o_inter, wn) * _silu(g)
    return o, st_new


def _heads(ref, start, width, n):
    return jnp.stack([ref[:, start + h * width:start + (h + 1) * width] for h in range(n)], axis=0)


def _gla_heads(p_ref, dk, dv, n):
    hk, hv = dk // n, dv // n
    return (_heads(p_ref, 0, hk, n), _heads(p_ref, dk, hk, n), _heads(p_ref, 2 * dk, hv, n),
            _heads(p_ref, 2 * dk + dv, hv, n))


def _gla_dims(d):
    dv = d // 2
    dk = dv // 2
    return dk, dv, dk // GLA_HEADS, dv // GLA_HEADS


def gla_fwd(proj, glr_col, wg, bg, wn, dv, carry=None):
    t = proj.shape[0]
    dk, dv, hk, hv = _gla_dims(2 * dv)
    L, H = CHUNK, GLA_HEADS
    nc = t // L
    wq = 2 * dk + 2 * dv

    def body(p_ref, glr_ref, wg_ref, bg_ref, wn_ref, o_ref, sp_ref, st):
        @pl.when(pl.program_id(0) == 0)
        def _():
            st[...] = jnp.zeros_like(st)

        s_prev = st[...]
        sp_ref[0] = s_prev
        o, s_new = _gla_chunk(*_gla_heads(p_ref, dk, dv, H), jnp.broadcast_to(glr_ref[...], (H, L, LANE)), s_prev,
                              _heads(wg_ref, 0, hk, H), _heads(bg_ref, 0, hk, H), wn_ref[...], _tri(L))
        for h in range(H):
            o_ref[:, h * hv:(h + 1) * hv] = o[h].astype(o_ref.dtype)
        st[...] = s_new

    return _pcall(
        body, carry=carry, name="gla_fwd", grid=(nc,),
        in_specs=[pl.BlockSpec((L, wq), lambda c: (c, 0)), pl.BlockSpec((L, LANE), lambda c: (c, glr_col)),
                  _full(wg.shape), _full(bg.shape), _full(wn.shape)],
        out_specs=[pl.BlockSpec((L, dv), lambda c: (c, 0)), pl.BlockSpec((1, H, hv, hk), lambda c: (c, 0, 0, 0))],
        out_shape=[jax.ShapeDtypeStruct((t, dv), BF16), jax.ShapeDtypeStruct((nc, H, hv, hk), F32)],
        scratch_shapes=[pltpu.VMEM((H, hv, hk), F32)],
        compiler_params=_cparams(("arbitrary",)),
    )(proj, proj, wg, bg, wn)


def gla_bwd(proj, glr_col, wg, bg, wn, sprev, do, dv, gla_w, carry=None):
    t = proj.shape[0]
    dk, _, hk, hv = _gla_dims(2 * dv)
    L, H = CHUNK, GLA_HEADS
    nc = t // L
    wq = 2 * dk + 2 * dv

    def body(p_ref, glr_ref, wg_ref, bg_ref, wn_ref, sp_ref, do_ref, dp_ref, dwg_ref, dbg_ref, dwn_ref, dst):
        @pl.when(pl.program_id(0) == 0)
        def _():
            dst[...] = jnp.zeros_like(dst)
            dwg_ref[...] = jnp.zeros_like(dwg_ref)
            dbg_ref[...] = jnp.zeros_like(dbg_ref)
            dwn_ref[...] = jnp.zeros_like(dwn_ref)

        f = functools.partial(_gla_chunk, tri=_tri(L))
        _, vjp = jax.vjp(f, *_gla_heads(p_ref, dk, dv, H), jnp.broadcast_to(glr_ref[...], (H, L, LANE)), sp_ref[0],
                         _heads(wg_ref, 0, hk, H), _heads(bg_ref, 0, hk, H), wn_ref[...])
        dq, dkk, dvv, dg, dgl, ds, dwg, dbg, dwn = vjp((_heads(do_ref, 0, hv, H), dst[...]))
        for h in range(H):
            dp_ref[:, h * hk:(h + 1) * hk] = dq[h].astype(dp_ref.dtype)
            dp_ref[:, dk + h * hk:dk + (h + 1) * hk] = dkk[h].astype(dp_ref.dtype)
            dp_ref[:, 2 * dk + h * hv:2 * dk + (h + 1) * hv] = dvv[h].astype(dp_ref.dtype)
            dp_ref[:, 2 * dk + dv + h * hv:2 * dk + dv + (h + 1) * hv] = dg[h].astype(dp_ref.dtype)
            dwg_ref[:, h * hk:(h + 1) * hk] += dwg[h]
            dbg_ref[:, h * hk:(h + 1) * hk] += dbg[h]
        dst[...] = ds
        dwn_ref[...] += dwn
        dp_ref[:, wq:wq + LANE] = jnp.sum(dgl, axis=0).astype(dp_ref.dtype)
        if gla_w > wq + LANE:
            dp_ref[:, wq + LANE:] = jnp.zeros((L, gla_w - wq - LANE), dp_ref.dtype)

    rev = lambda c: nc - 1 - c
    return _pcall(
        body, carry=carry, name="gla_bwd", grid=(nc,),
        in_specs=[pl.BlockSpec((L, wq), lambda c: (rev(c), 0)), pl.BlockSpec((L, LANE), lambda c: (rev(c), glr_col)),
                  _full(wg.shape), _full(bg.shape), _full(wn.shape),
                  pl.BlockSpec((1, H, hv, hk), lambda c: (rev(c), 0, 0, 0)),
                  pl.BlockSpec((L, dv), lambda c: (rev(c), 0))],
        out_specs=[pl.BlockSpec((L, gla_w), lambda c: (rev(c), 0)),
                   _full(wg.shape), _full(bg.shape), _full(wn.shape)],
        out_shape=[jax.ShapeDtypeStruct(proj.shape, BF16),
                   jax.ShapeDtypeStruct(wg.shape, F32), jax.ShapeDtypeStruct(bg.shape, F32),
                   jax.ShapeDtypeStruct(wn.shape, F32)],
        scratch_shapes=[pltpu.VMEM((H, hv, hk), F32)],
        compiler_params=_cparams(("arbitrary",)),
    )(proj, proj, wg, bg, wn, sprev, do)


def _shift_down(x, tail, s):
    if s == 0:
        return x
    r = pltpu.roll(x, s, 0)
    rows = lax.broadcasted_iota(jnp.int32, tail.shape, 0)
    top = jnp.where(rows < s, pltpu.roll(tail, s, 0), r[:SUBLANE])
    return jnp.concatenate([top, r[SUBLANE:]], axis=0)


def _shift_up(x, head, s):
    if s == 0:
        return x
    n = x.shape[0]
    r = pltpu.roll(x, n - s, 0)
    rows = lax.broadcasted_iota(jnp.int32, head.shape, 0)
    bottom = jnp.where(rows >= SUBLANE - s, pltpu.roll(head, SUBLANE - s, 0), r[n - SUBLANE:])
    return jnp.concatenate([r[:n - SUBLANE], bottom], axis=0)


def _conv(x, prev, w, b):
    y = b
    for k in range(CONV_WIDTH):
        y = y + w[k:k + 1, :] * _shift_down(x, prev, CONV_WIDTH - 1 - k)
    return y


def _conv_bwd(dy, nxt, x, prev, w):
    dx = None
    dws = []
    for k in range(CONV_WIDTH):
        s = CONV_WIDTH - 1 - k
        term = w[k:k + 1, :] * _shift_up(dy, nxt, s)
        dx = term if dx is None else dx + term
        dws.append(jnp.sum(dy * _shift_down(x, prev, s), axis=0, keepdims=True))
    return dx, jnp.concatenate(dws, axis=0), jnp.sum(dy, axis=0, keepdims=True)


def _scan_fwd(a, u):
    n = a.shape[0]
    rows = lax.broadcasted_iota(jnp.int32, a.shape, 0)
    s = 1
    while s < n:
        a_sh = jnp.where(rows < s, 1.0, pltpu.roll(a, s, 0))
        u_sh = jnp.where(rows < s, 0.0, pltpu.roll(u, s, 0))
        u = a * u_sh + u
        a = a * a_sh
        s *= 2
    return a, u


def _scan_rev(c, d):
    n = c.shape[0]
    rows = lax.broadcasted_iota(jnp.int32, c.shape, 0)
    s = 1
    while s < n:
        c_sh = jnp.where(rows >= n - s, 0.0, pltpu.roll(c, n - s, 0))
        d_sh = jnp.where(rows >= n - s, 0.0, pltpu.roll(d, n - s, 0))
        d = d + c * d_sh
        c = c * c_sh
        s *= 2
    return d


def _expm1(x):
    small = x * (1.0 + x * (0.5 + x * (1.0 / 6.0 + x * (1.0 / 24.0))))
    return jnp.where(jnp.abs(x) < 1e-2, small, jnp.exp(x) - 1.0)


def _lru_gates(xc, pa, pi, lam):
    r = _sigmoid(pa)
    i = _sigmoid(pi)
    log_a = LRU_C * r * _log_sigmoid(lam)
    a = jnp.exp(log_a)
    u = jnp.sqrt(-_expm1(2.0 * log_a)) * (i * xc)
    return a, u


def _lru_out(h, gate):
    return h * _gelu_tanh(gate)


def _blockdiag(xc, w_ref, b):
    nb = w_ref.shape[0]
    outs = [mm(xc[:, n * LRU_BLOCK:(n + 1) * LRU_BLOCK], w_ref[n]) for n in range(nb)]
    return jnp.concatenate(outs, axis=1) + b


def lru_fwd(proj, xcol, lw, cw, cb, wa, ba, wi, bi, lam):
    t = proj.shape[0]
    tb = _pick(t, (256, 128, 64))
    nb = t // tb

    def body(x_ref, xp_ref, g_ref, cw_ref, cb_ref, wa_ref, ba_ref, wi_ref, bi_ref, lam_ref, o_ref, hin_ref, hc):
        i = pl.program_id(0)

        @pl.when(i == 0)
        def _():
            hc[...] = jnp.zeros_like(hc)

        prev = jnp.where(i == 0, 0.0, xp_ref[...])
        xc = _conv(x_ref[...], prev, cw_ref[...], cb_ref[...])
        a, u = _lru_gates(xc, _blockdiag(xc, wa_ref, ba_ref[...]), _blockdiag(xc, wi_ref, bi_ref[...]), lam_ref[...])
        acum, h0 = _scan_fwd(a, u)
        h = h0 + acum * hc[...]
        hin_ref[0] = hc[...]
        hc[...] = h[tb - 1:tb, :]
        o_ref[...] = _lru_out(h, g_ref[...]).astype(o_ref.dtype)

    row = lambda col: pl.BlockSpec((tb, lw), lambda i: (i, col))
    return pl.pallas_call(
        body, name="lru_fwd", grid=(nb,),
        in_specs=[row(xcol), pl.BlockSpec((SUBLANE, lw), lambda i: (jnp.maximum(i * (tb // SUBLANE) - 1, 0), xcol)),
                  row(xcol + 1),
                  _full(cw.shape), _full(cb.shape), _full(wa.shape), _full(ba.shape), _full(wi.shape), _full(bi.shape),
                  _full(lam.shape)],
        out_specs=[pl.BlockSpec((tb, lw), lambda i: (i, 0)), pl.BlockSpec((1, 1, lw), lambda i: (i, 0, 0))],
        out_shape=[jax.ShapeDtypeStruct((t, lw), BF16), jax.ShapeDtypeStruct((nb, 1, lw), F32)],
        scratch_shapes=[pltpu.VMEM((1, lw), F32)],
        compiler_params=_cparams(("arbitrary",)),
    )(proj, proj, proj, cw, cb, wa, ba, wi, bi, lam)


def lru_bwd(proj, xcol, lw, cw, cb, wa, ba, wi, bi, lam, hin, dmix, docol, dproj, carry=None):
    t = proj.shape[0]
    tb = _pick(t, (256, 128, 64))
    nb = t // tb
    nblk = wa.shape[0]
    assert xcol % 2 == 0

    def body(x_ref, xp_ref, g_ref, cw_ref, cb_ref, wa_ref, ba_ref, wi_ref, bi_ref, lam_ref, hin_ref, do_ref, _,
             dxg_ref, dcw_ref, dcb_ref, dwa_ref, dba_ref, dwi_ref, dbi_ref, dlam_ref, gc, dxcn):
        pid = pl.program_id(0)
        i = nb - 1 - pid

        @pl.when(pid == 0)
        def _():
            gc[...] = jnp.zeros_like(gc)
            dxcn[...] = jnp.zeros_like(dxcn)
            for r in (dcw_ref, dcb_ref, dwa_ref, dba_ref, dwi_ref, dbi_ref, dlam_ref):
                r[...] = jnp.zeros_like(r)

        x = x_ref[...]
        prev = jnp.where(i == 0, 0.0, xp_ref[...])
        cw_v = cw_ref[...]
        xc = _conv(x, prev, cw_v, cb_ref[...])
        pa = _blockdiag(xc, wa_ref, ba_ref[...])
        pi = _blockdiag(xc, wi_ref, bi_ref[...])
        (a, u), vjp_g = jax.vjp(_lru_gates, xc, pa, pi, lam_ref[...])
        acum, h0 = _scan_fwd(a, u)
        hi = hin_ref[0]
        h = h0 + acum * hi
        rows = lax.broadcasted_iota(jnp.int32, h.shape, 0)
        hprev = jnp.where(rows < 1, hi, pltpu.roll(h, 1, 0))
        _, vjp_o = jax.vjp(_lru_out, h, g_ref[...])
        dh, dgate = vjp_o(do_ref[...].astype(F32))
        c = jnp.where(rows >= tb - 1, 0.0, pltpu.roll(a, tb - 1, 0))
        g = _scan_rev(c, dh + jnp.where(rows == tb - 1, gc[...], 0.0))
        gc[...] = a[0:1, :] * g[0:1, :]
        dxc, dpa, dpi, dlam = vjp_g((g * hprev, g))
        dlam_ref[...] += dlam
        dba_ref[...] += jnp.sum(dpa, axis=0, keepdims=True)
        dbi_ref[...] += jnp.sum(dpi, axis=0, keepdims=True)
        parts = []
        for n in range(nblk):
            sl = slice(n * LRU_BLOCK, (n + 1) * LRU_BLOCK)
            dwa_ref[n] += mm_tn(xc[:, sl], dpa[:, sl])
            dwi_ref[n] += mm_tn(xc[:, sl], dpi[:, sl])
            parts.append(mm_nt(dpa[:, sl], wa_ref[n]) + mm_nt(dpi[:, sl], wi_ref[n]))
        dxc = dxc + jnp.concatenate(parts, axis=1)
        dx, dcw, dcb = _conv_bwd(dxc, dxcn[...], x, prev, cw_v)
        dxcn[...] = dxc[:SUBLANE]
        dcw_ref[...] += dcw
        dcb_ref[...] += dcb
        dxg_ref[:, :lw] = dx.astype(dxg_ref.dtype)
        dxg_ref[:, lw:] = dgate.astype(dxg_ref.dtype)

    row = lambda col: pl.BlockSpec((tb, lw), lambda p: (nb - 1 - p, col))
    params = [cw, cb, wa, ba, wi, bi, lam]
    return _pcall(
        body, carry=carry, name="lru_bwd", grid=(nb,),
        in_specs=[row(xcol),
                  pl.BlockSpec((SUBLANE, lw), lambda p: (jnp.maximum((nb - 1 - p) * (tb // SUBLANE) - 1, 0), xcol)),
                  row(xcol + 1)]
        + [_full(p.shape) for p in params]
        + [pl.BlockSpec((1, 1, lw), lambda p: (nb - 1 - p, 0, 0)), row(docol), ANY],
        out_specs=[pl.BlockSpec((tb, 2 * lw), lambda p: (nb - 1 - p, xcol // 2))] + [_full(p.shape) for p in params],
        out_shape=[jax.ShapeDtypeStruct(dproj.shape, dproj.dtype)]
        + [jax.ShapeDtypeStruct(p.shape, F32) for p in params],
        input_output_aliases={12: 0},
        scratch_shapes=[pltpu.VMEM((1, lw), F32), pltpu.VMEM((SUBLANE, lw), F32)],
        compiler_params=_cparams(("arbitrary",)),
    )(proj, proj, proj, *params, hin, dmix, dproj)


def conv_silu_fwd(proj, col0, width, cw, cb, carry=None):
    t = proj.shape[0]
    tb = _pick(t, (512, 256, 128, 64))
    cbw = _pick(width, (512, 256, 128))
    off = col0 // cbw
    assert col0 % cbw == 0

    def body(x_ref, xp_ref, w_ref, b_ref, o_ref):
        prev = jnp.where(pl.program_id(1) == 0, 0.0, xp_ref[...])
        o_ref[...] = _silu(_conv(x_ref[...], prev, w_ref[...], b_ref[...]))

    return _pcall(
        body, carry=carry, name="conv_silu_fwd", grid=(width // cbw, t // tb),
        in_specs=[pl.BlockSpec((tb, cbw), lambda j, i: (i, off + j)),
                  pl.BlockSpec((SUBLANE, cbw), lambda j, i: (jnp.maximum(i * (tb // SUBLANE) - 1, 0), off + j)),
                  pl.BlockSpec((CONV_WIDTH, cbw), lambda j, i: (0, j)), pl.BlockSpec((1, cbw), lambda j, i: (0, j))],
        out_specs=pl.BlockSpec((tb, cbw), lambda j, i: (i, j)),
        out_shape=jax.ShapeDtypeStruct((t, width), F32),
        compiler_params=_cparams(("parallel", "arbitrary")),
    )(proj, proj, cw, cb)


def conv_silu_bwd(proj, col0, width, cw, cb, dact, dproj, carry=None):
    t = proj.shape[0]
    tb = _pick(t, (512, 256, 128, 64))
    nb = t // tb
    cbw = _pick(width, (512, 256, 128))
    off = col0 // cbw

    def body(x_ref, xp_ref, w_ref, b_ref, d_ref, _, dx_ref, dw_ref, db_ref, nxt):
        pid = pl.program_id(1)
        i = nb - 1 - pid

        @pl.when(pid == 0)
        def _():
            nxt[...] = jnp.zeros_like(nxt)
            dw_ref[...] = jnp.zeros_like(dw_ref)
            db_ref[...] = jnp.zeros_like(db_ref)

        x = x_ref[...]
        prev = jnp.where(i == 0, 0.0, xp_ref[...])
        w = w_ref[...]
        _, vjp = jax.vjp(_silu, _conv(x, prev, w, b_ref[...]))
        (dcv,) = vjp(d_ref[...])
        dx, dw, db = _conv_bwd(dcv, nxt[...], x, prev, w)
        nxt[...] = dcv[:SUBLANE]
        dx_ref[...] = dx.astype(dx_ref.dtype)
        dw_ref[...] += dw
        db_ref[...] += db

    return _pcall(
        body, carry=carry, name="conv_silu_bwd", grid=(width // cbw, nb),
        in_specs=[pl.BlockSpec((tb, cbw), lambda j, p: (nb - 1 - p, off + j)),
                  pl.BlockSpec((SUBLANE, cbw),
                               lambda j, p: (jnp.maximum((nb - 1 - p) * (tb // SUBLANE) - 1, 0), off + j)),
                  pl.BlockSpec((CONV_WIDTH, cbw), lambda j, p: (0, j)), pl.BlockSpec((1, cbw), lambda j, p: (0, j)),
                  pl.BlockSpec((tb, cbw), lambda j, p: (nb - 1 - p, j)), ANY],
        out_specs=[pl.BlockSpec((tb, cbw), lambda j, p: (nb - 1 - p, off + j)),
                   pl.BlockSpec((CONV_WIDTH, cbw), lambda j, p: (0, j)), pl.BlockSpec((1, cbw), lambda j, p: (0, j))],
        out_shape=[jax.ShapeDtypeStruct(dproj.shape, dproj.dtype), jax.ShapeDtypeStruct(cw.shape, F32),
                   jax.ShapeDtypeStruct(cb.shape, F32)],
        scratch_shapes=[pltpu.VMEM((SUBLANE, cbw), F32)],
        input_output_aliases={5: 0},
        compiler_params=_cparams(("parallel", "arbitrary")),
    )(proj, proj, cw, cb, dact, dproj)


def _dt_expand(raw, bias, e):
    return sel_r(_softplus(raw + bias), e)


def dt_fwd(proj, dtcol, bias, e):
    t = proj.shape[0]
    di = e.shape[1]
    tb = _pick(t, (512, 256, 128, 64))

    def body(r_ref, b_ref, e_ref, o_ref):
        o_ref[...] = _dt_expand(r_ref[...], b_ref[...], e_ref[...])

    return pl.pallas_call(
        body, name="dt_fwd", grid=(t // tb,),
        in_specs=[pl.BlockSpec((tb, LANE), lambda i: (i, dtcol)), _full(bias.shape), _full(e.shape)],
        out_specs=pl.BlockSpec((tb, di), lambda i: (i, 0)),
        out_shape=jax.ShapeDtypeStruct((t, di), F32),
        compiler_params=_cparams(("parallel",)),
    )(proj, bias, e)


def dt_bwd(proj, dtcol, bias, e, ddte, dproj):
    t = proj.shape[0]
    di = e.shape[1]
    tb = _pick(t, (512, 256, 128, 64))
    tail = dproj.shape[1] - dtcol * LANE
    assert (dtcol * LANE) % tail == 0

    def body(r_ref, b_ref, e_ref, d_ref, _, dr_ref, db_ref):
        @pl.when(pl.program_id(0) == 0)
        def _():
            db_ref[...] = jnp.zeros_like(db_ref)

        e_v = e_ref[...]
        _, vjp = jax.vjp(lambda r, b: _dt_expand(r, b, e_v), r_ref[...], b_ref[...])
        dr, db = vjp(d_ref[...])
        dr_ref[:, :LANE] = dr.astype(dr_ref.dtype)
        if tail > LANE:
            dr_ref[:, LANE:] = jnp.zeros((tb, tail - LANE), dr_ref.dtype)
        db_ref[...] += db

    return pl.pallas_call(
        body, name="dt_bwd", grid=(t // tb,),
        in_specs=[pl.BlockSpec((tb, LANE), lambda i: (i, dtcol)), _full(bias.shape), _full(e.shape),
                  pl.BlockSpec((tb, di), lambda i: (i, 0)), ANY],
        out_specs=[pl.BlockSpec((tb, tail), lambda i: (i, dtcol * LANE // tail)), _full(bias.shape)],
        out_shape=[jax.ShapeDtypeStruct(dproj.shape, dproj.dtype), jax.ShapeDtypeStruct(bias.shape, F32)],
        input_output_aliases={4: 0},
        compiler_params=_cparams(("arbitrary",)),
    )(proj, bias, e, ddte, dproj)


def head_expand(p, e, *, transpose=False, name):
    di = e.shape[1]

    def body(p_ref, e_ref, o_ref):
        if transpose:
            o_ref[...] = _sel_r_b(e_ref[...], p_ref[...])[0]
        else:
            o_ref[...] = sel_r(p_ref[...], e_ref[...])

    oshape = (SUBLANE, LANE) if transpose else (SUBLANE, di)
    return pl.pallas_call(
        body, name=name, in_specs=[_full(p.shape), _full(e.shape)], out_specs=_full(oshape),
        out_shape=jax.ShapeDtypeStruct(oshape, F32), compiler_params=_cparams(None), grid=(1,),
    )(p, e)


def _ssd_chunk(x, z, bm, cm, dte, st, alog, dskip, gn, tri, cmask, dmask, bd):
    L, gw = x.shape
    reps = gw // L
    a = dte * (-jnp.exp(alog))
    acs = sel_l(tri, a)
    acs_last = jnp.sum(a, axis=0, keepdims=True)
    arow = jnp.sum(acs * dmask, axis=0, keepdims=True)
    dtrow = jnp.sum(dte * dmask, axis=0, keepdims=True)
    cb = mm_nt(cm, jnp.concatenate([bm] * reps, axis=0))
    wts = cb * (jnp.exp(jnp.minimum(acs - arow, 0.0)) * cmask) * dtrow
    xbd = jnp.concatenate([x] * reps, axis=0) * bd
    xw = x * (jnp.exp(acs_last - acs) * dte)
    y = mm(wts, xbd) + mm(cm, st) * jnp.exp(acs) + dskip * x
    st_new = jnp.exp(acs_last) * st + mm_tn(bm, xw)
    return _rms(y * _silu(z), gn), st_new


def _ssd_dims(di):
    gw = di // SSD_GROUPS
    assert CHUNK == SSD_HEAD_DIM and gw % LANE == 0
    return gw, SSD_STATE


def _ssd_masks(gw):
    L = CHUNK
    r = jnp.arange(L)[:, None]
    c = jnp.arange(gw)[None, :]
    cmask = ((c % L) <= r).astype(F32)
    dmask = ((c % L) == r).astype(F32)
    rr = jnp.arange(gw)
    bd = ((rr[:, None] // L) == (rr[None, :] // L)).astype(F32)
    tri = (jnp.arange(L)[None, :] <= jnp.arange(L)[:, None]).astype(BF16)
    return tri, cmask, dmask, bd


def ssd_fwd(xs, proj, dte, alog_e, dskip_e, gn, carry=None):
    t, di = dte.shape
    gw, n = _ssd_dims(di)
    L, G = CHUNK, SSD_GROUPS
    nc = t // L
    masks = _ssd_masks(gw)
    cdim = xs.shape[1]

    def body(x_ref, z_ref, dt_ref, al_ref, ds_ref, gn_ref, tri_ref, cm_ref, dm_ref, bd_ref, y_ref, sp_ref, st):
        @pl.when(pl.program_id(0) == 0)
        def _():
            st[...] = jnp.zeros_like(st)

        for g in range(G):
            ch = slice(g * gw, (g + 1) * gw)
            s_prev = st[g]
            sp_ref[0, g] = s_prev
            y, s_new = _ssd_chunk(x_ref[:, ch], z_ref[:, ch], x_ref[:, di + g * n:di + (g + 1) * n],
                                  x_ref[:, di + (G + g) * n:di + (G + g + 1) * n], dt_ref[:, ch], s_prev,
                                  al_ref[0:1, ch], ds_ref[0:1, ch], gn_ref[:, ch], tri_ref[...], cm_ref[...],
                                  dm_ref[...], bd_ref[...])
            y_ref[:, ch] = y.astype(y_ref.dtype)
            st[g] = s_new

    row = lambda w: pl.BlockSpec((L, w), lambda c: (c, 0))
    return _pcall(
        body, carry=carry, name="ssd_fwd", grid=(nc,),
        in_specs=[row(cdim), row(di), row(di), _full(alog_e.shape), _full(dskip_e.shape), _full(gn.shape)]
        + [_full(m.shape) for m in masks],
        out_specs=[row(di), pl.BlockSpec((1, G, n, gw), lambda c: (c, 0, 0, 0))],
        out_shape=[jax.ShapeDtypeStruct((t, di), BF16), jax.ShapeDtypeStruct((nc, G, n, gw), F32)],
        scratch_shapes=[pltpu.VMEM((G, n, gw), F32)],
        compiler_params=_cparams(("arbitrary",)),
    )(xs, proj, dte, alog_e, dskip_e, gn, *masks)


def ssd_bwd(xs, proj, dte, alog_e, dskip_e, gn, sprev, dy, dproj_shape, carry=None):
    t, di = dte.shape
    gw, n = _ssd_dims(di)
    L, G = CHUNK, SSD_GROUPS
    nc = t // L
    masks = _ssd_masks(gw)
    cdim = xs.shape[1]

    def body(x_ref, z_ref, dt_ref, al_ref, ds_ref, gn_ref, tri_ref, cm_ref, dm_ref, bd_ref, sp_ref, dy_ref,
             dxs_ref, dz_ref, ddt_ref, dal_ref, dds_ref, dgn_ref, dst):
        @pl.when(pl.program_id(0) == 0)
        def _():
            dst[...] = jnp.zeros_like(dst)
            dal_ref[...] = jnp.zeros_like(dal_ref)
            dds_ref[...] = jnp.zeros_like(dds_ref)
            dgn_ref[...] = jnp.zeros_like(dgn_ref)

        f = functools.partial(_ssd_chunk, tri=tri_ref[...], cmask=cm_ref[...], dmask=dm_ref[...], bd=bd_ref[...])
        for g in range(G):
            ch = slice(g * gw, (g + 1) * gw)
            bs = slice(di + g * n, di + (g + 1) * n)
            cs = slice(di + (G + g) * n, di + (G + g + 1) * n)
            _, vjp = jax.vjp(f, x_ref[:, ch], z_ref[:, ch], x_ref[:, bs], x_ref[:, cs], dt_ref[:, ch], sp_ref[0, g],
                             al_ref[0:1, ch], ds_ref[0:1, ch], gn_ref[:, ch])
            dx, dz, db, dc, ddt, ds, dal, dds, dgn = vjp((dy_ref[:, ch], dst[g]))
            dxs_ref[:, ch] = dx
            dxs_ref[:, bs] = db
            dxs_ref[:, cs] = dc
            dz_ref[:, ch] = dz.astype(dz_ref.dtype)
            ddt_ref[:, ch] = ddt
            dst[g] = ds
            dal_ref[:, ch] += dal
            dds_ref[:, ch] += dds
            dgn_ref[:, ch] += dgn

    row = lambda w: pl.BlockSpec((L, w), lambda c: (nc - 1 - c, 0))
    acc = _full((1, di))
    acc_shape = jax.ShapeDtypeStruct((1, di), F32)
    return _pcall(
        body, carry=carry, name="ssd_bwd", grid=(nc,),
        in_specs=[row(cdim), row(di), row(di), _full(alog_e.shape), _full(dskip_e.shape), _full(gn.shape)]
        + [_full(m.shape) for m in masks]
        + [pl.BlockSpec((1, G, n, gw), lambda c: (nc - 1 - c, 0, 0, 0)), row(di)],
        out_specs=[row(cdim), row(di), row(di), acc, acc, acc],
        out_shape=[jax.ShapeDtypeStruct((t, cdim), F32), jax.ShapeDtypeStruct(dproj_shape, BF16),
                   jax.ShapeDtypeStruct((t, di), F32), acc_shape, acc_shape, acc_shape],
        scratch_shapes=[pltpu.VMEM((G, n, gw), F32)],
        compiler_params=_cparams(("arbitrary",)),
    )(xs, proj, dte, alog_e, dskip_e, gn, *masks, sprev, dy)


def _rows2d(a):
    return a.reshape(-1, a.shape[-1])


def _row_tile(rows, cols):
    cap = max(SUBLANE, (1 << 19) // max(cols, 1))
    step = 2 * SUBLANE
    for c in range(min(cap, rows) // step * step, 0, -step):
        if rows % c == 0:
            return c
    return rows


def chip_sum(g, r, core, *, name):
    shape = r.shape
    cols = shape[-1]
    g4 = g.reshape(4, 2, -1, cols)
    r3 = r.reshape(4, -1, cols)
    rows = r3.shape[1]
    tr = _row_tile(rows, cols)

    def body(c_ref, g_ref, r_ref, o_ref):
        o_ref[...] = (g_ref[...].astype(F32) + r_ref[...].astype(F32)).astype(o_ref.dtype)

    out = pl.pallas_call(
        body, name=name,
        grid_spec=pltpu.PrefetchScalarGridSpec(
            num_scalar_prefetch=1, grid=(4, rows // tr),
            in_specs=[pl.BlockSpec((None, None, tr, cols), lambda j, i, c: (j, c[0], i, 0)),
                      pl.BlockSpec((None, tr, cols), lambda j, i, c: (j, i, 0))],
            out_specs=pl.BlockSpec((None, tr, cols), lambda j, i, c: (j, i, 0))),
        out_shape=jax.ShapeDtypeStruct(r3.shape, BF16), compiler_params=_cparams(("parallel", "parallel")),
    )(core.reshape(1).astype(jnp.int32), g4, r3)
    return out.reshape(shape)


def mesh_sum(p, r, chip, core, *, name):
    shape = p.shape[1:]
    cols = shape[-1]
    p3 = p.reshape(4, -1, cols)
    r3 = r.reshape(3, -1, cols)
    rows = p3.shape[1]
    tr = _row_tile(rows, 2 * cols)

    def body(c_ref, p_ref, r_ref, o_ref):
        o_ref[...] = ((p_ref[...].astype(F32) + r_ref[0].astype(F32)) + r_ref[1].astype(F32)) + r_ref[2].astype(F32)

    out = pl.pallas_call(
        body, name=name,
        grid_spec=pltpu.PrefetchScalarGridSpec(
            num_scalar_prefetch=1, grid=(rows // tr,),
            in_specs=[pl.BlockSpec((None, tr, cols), lambda i, c: (c[0], i, 0)),
                      pl.BlockSpec((3, tr, cols), lambda i, c: (0, i, 0))],
            out_specs=pl.BlockSpec((None, tr, cols), lambda i, c: (c[1], i, 0))),
        out_shape=jax.ShapeDtypeStruct((2, rows, cols), F32), compiler_params=_cparams(("parallel",)),
    )(jnp.stack([chip, core]).astype(jnp.int32), p3, r3)
    return out.reshape((2,) + shape)


def sum_leading(x, *, name):
    k, r, c = x.shape
    tr = _row_tile(r, c * k)
    def body(x_ref, o_ref):
        acc = x_ref[0]
        for i in range(1, k):
            acc = acc + x_ref[i]
        o_ref[...] = acc

    return pl.pallas_call(
        body, name=name, grid=(r // tr,), in_specs=[pl.BlockSpec((k, tr, c), lambda i: (0, i, 0))],
        out_specs=pl.BlockSpec((tr, c), lambda i: (i, 0)),
        out_shape=jax.ShapeDtypeStruct((r, c), F32), compiler_params=_cparams(("parallel",)),
    )(x)


def adamw(w, gs, m, v, *, name):
    shape = w.shape
    w3, m3, v3 = (a.reshape((-1,) + a.shape[-2:]) for a in (w, m, v))
    nl, r, c = w3.shape
    assert len(gs) == nl
    tr = _row_tile(r, 2 * c)
    tc = c
    if tr == r and r * c > (1 << 19):
        tc = next(t for t in (1024, 512, 256, 128) if c % t == 0 and r * t <= (1 << 19))
    c1 = 1.0 - ADAM_B1 ** ADAM_STEP
    c2 = 1.0 - ADAM_B2 ** ADAM_STEP

    def body(w_ref, g_ref, m_ref, v_ref, *rest):
        go_ref, d_ref, mo_ref, vo_ref = rest[-4:]
        gv = g_ref[...]
        mn = ADAM_B1 * m_ref[...] + (1.0 - ADAM_B1) * gv
        vn = ADAM_B2 * v_ref[...] + (1.0 - ADAM_B2) * (gv * gv)
        go_ref[...] = gv
        d_ref[...] = -ADAM_LR * ((mn / c1) / (jnp.sqrt(vn / c2) + ADAM_EPS) + ADAM_WD * w_ref[...])
        mo_ref[...] = mn
        vo_ref[...] = vn

    outs = None
    for l, g in enumerate(gs):
        layer = pl.BlockSpec((None, tr, tc), lambda i, j, l=l: (l, i, j))
        prev = [] if outs is None else list(outs)
        outs = pl.pallas_call(
            functools.partial(body), name=f"{name}_{l}", grid=(r // tr, c // tc),
            in_specs=[layer, pl.BlockSpec((tr, tc), lambda i, j: (i, j)), layer, layer] + [ANY] * len(prev),
            out_specs=[layer] * 4, out_shape=[jax.ShapeDtypeStruct((nl, r, c), F32)] * 4,
            input_output_aliases={4 + k: k for k in range(len(prev))},
            compiler_params=_cparams(("parallel", "parallel")),
        )(w3, g.reshape(r, c), m3, v3, *prev)
    return tuple(o.reshape(shape) for o in outs)


ANY = pl.BlockSpec(memory_space=pl.ANY)


def _place():
    x, y, c = lax.axis_index("x"), lax.axis_index("y"), lax.axis_index("c")
    chips = [(1 - x, y), (x, 1 - y), (1 - x, 1 - y)]
    return x, y, c, chips


def gather8(block):
    m, n = block.shape

    def body(x_ref, out_ref, send_sems, recv_sems, local_sem):
        x, y, c, chips = _place()
        me, sibling = (x, y, c), (x, y, 1 - c)

        def rows(px, py, pc):
            return out_ref.at[4 * px + 2 * py + pc]

        def copy(k, blk, to, src=None):
            return pltpu.make_async_remote_copy(
                src_ref=rows(*blk) if src is None else src, dst_ref=rows(*blk), send_sem=send_sems.at[k],
                recv_sem=recv_sems.at[k], device_id=to, device_id_type=MESH)

        mine = pltpu.make_async_copy(x_ref, rows(*me), local_sem)
        mine.start()
        first = [copy(0, me, sibling, src=x_ref)]
        first += [copy(1 + j, me, (*chip, c), src=x_ref) for j, chip in enumerate(chips)]
        for cp in first:
            cp.start()
        passed = [copy(4 + j, (*chip, c), sibling) for j, chip in enumerate(chips)]
        for j, chip in enumerate(chips):
            copy(1 + j, (*chip, c), me).wait_recv()
            passed[j].start()
        copy(0, sibling, me).wait_recv()
        for j, chip in enumerate(chips):
            copy(4 + j, (*chip, 1 - c), me).wait_recv()
        for cp in first + passed:
            cp.wait_send()
        mine.wait()

    return pl.pallas_call(
        body, name="gather8",
        out_shape=jax.ShapeDtypeStruct((8, m, n), block.dtype),
        in_specs=[pl.BlockSpec(memory_space=pltpu.VMEM)],
        out_specs=pl.BlockSpec(memory_space=pltpu.VMEM),
        scratch_shapes=[pltpu.SemaphoreType.DMA((7,)), pltpu.SemaphoreType.DMA((7,)), pltpu.SemaphoreType.DMA],
        compiler_params=pltpu.CompilerParams(vmem_limit_bytes=VMEM_LIMIT),
    )(block)


def gather_weights(shards):
    n = len(shards)

    def copy(ins, outs, send, recv, base, a, k, chip_idx, half, to, src=None):
        dst = outs[a].at[chip_idx, half]
        return pltpu.make_async_remote_copy(
            src_ref=dst if src is None else src, dst_ref=dst, send_sem=send.at[base + 6 * a + k],
            recv_sem=recv.at[base + 6 * a + k], device_id=to, device_id_type=MESH)

    def first(ins, outs, send, recv, base):
        x, y, c, chips = _place()
        return [copy(ins, outs, send, recv, base, a, j, 2 * x + y, c, (*chip, c), src=ins[a].at[c])
                for a in range(n) for j, chip in enumerate(chips)]

    def start(ins, outs, send, recv, base):
        for cp in first(ins, outs, send, recv, base):
            cp.start()

    def finish(ins, outs, send, recv, base):
        x, y, c, chips = _place()
        sibling = (x, y, 1 - c)
        passed = []
        for a in range(n):
            for j, (cx, cy) in enumerate(chips):
                copy(ins, outs, send, recv, base, a, j, 2 * cx + cy, c, (cx, cy, c)).wait_recv()
                fw = copy(ins, outs, send, recv, base, a, 3 + j, 2 * cx + cy, c, sibling)
                fw.start()
                passed.append(fw)
        for a in range(n):
            for j, (cx, cy) in enumerate(chips):
                copy(ins, outs, send, recv, base, a, 3 + j, 2 * cx + cy, 1 - c, sibling).wait_recv()
        for cp in first(ins, outs, send, recv, base) + passed:
            cp.wait_send()

    return Carry(shards, [jax.ShapeDtypeStruct((4,) + s.shape, s.dtype) for s in shards], 6 * n, start, finish)


def exchange_halves(grads):
    n = len(grads)

    def copies(ins, outs, send, recv, base):
        x, y, c, _ = _place()
        return [pltpu.make_async_remote_copy(
            src_ref=ins[a].at[j, 1 - c], dst_ref=outs[a].at[j], send_sem=send.at[base + 4 * a + j],
            recv_sem=recv.at[base + 4 * a + j], device_id=(x, y, 1 - c), device_id_type=MESH)
            for a in range(n) for j in range(4)]

    def start(*args):
        for cp in copies(*args):
            cp.start()

    def finish(*args):
        for cp in copies(*args):
            cp.wait()

    return Carry(grads, [jax.ShapeDtypeStruct((4,) + g.shape[2:], g.dtype) for g in grads], 4 * n, start, finish)


def scatter_chips(parts):
    n = len(parts)

    def copies(ins, outs, send, recv, base):
        x, y, c, chips = _place()
        return [pltpu.make_async_remote_copy(
            src_ref=ins[a].at[2 * cx + cy], dst_ref=outs[a].at[j], send_sem=send.at[base + 3 * a + j],
            recv_sem=recv.at[base + 3 * a + j], device_id=(cx, cy, c), device_id_type=MESH)
            for a in range(n) for j, (cx, cy) in enumerate(chips)]

    def start(*args):
        for cp in copies(*args):
            cp.start()

    def finish(*args):
        for cp in copies(*args):
            cp.wait()

    return Carry(parts, [jax.ShapeDtypeStruct((3,) + p.shape[1:], p.dtype) for p in parts], 3 * n, start, finish)


def join_halves(bufs):
    n = len(bufs)

    def copy(outs, send, recv, base, a, half):
        x, y, c, _ = _place()
        return pltpu.make_async_remote_copy(
            src_ref=outs[a].at[c], dst_ref=outs[a].at[c if half is None else half], send_sem=send.at[base + a],
            recv_sem=recv.at[base + a], device_id=(x, y, 1 - c), device_id_type=MESH)

    def start(ins, outs, send, recv, base):
        for a in range(n):
            copy(outs, send, recv, base, a, None).start()

    def finish(ins, outs, send, recv, base):
        c = lax.axis_index("c")
        for a in range(n):
            copy(outs, send, recv, base, a, 1 - c).wait_recv()
        for a in range(n):
            copy(outs, send, recv, base, a, None).wait_send()

    return Carry(bufs, [jax.ShapeDtypeStruct(h.shape, h.dtype) for h in bufs], n, start, finish,
                 aliases={a: a for a in range(n)})


def run_comm(carry, *, name):
    k_in, k_out = len(carry.arrays), len(carry.out_shape)

    def body(*refs):
        ins, outs = refs[:k_in], refs[k_in:k_in + k_out]
        send, recv = refs[-2:]
        carry.start(ins, outs, send, recv, 0)
        carry.finish(ins, outs, send, recv, 0)

    return pl.pallas_call(
        body, name=name, out_shape=carry.out_shape, in_specs=[ANY] * k_in, out_specs=[ANY] * k_out,
        scratch_shapes=[pltpu.SemaphoreType.DMA((carry.n_sems,))] * 2, input_output_aliases=carry.aliases,
    )(*carry.arrays)


INPUTS = ['x'] + WEIGHTS + ['loss_target'] + ['m_' + n for n in WEIGHTS] + ['v_' + n for n in WEIGHTS]


def _round_up(n, m):
    return -(-n // m) * m


def _pack(arrs):
    flat = jnp.concatenate([a.reshape(-1) for a in arrs])
    n = _round_up(flat.shape[0], 512 * LANE)
    return jnp.pad(flat, (0, n - flat.shape[0])).reshape(-1, LANE)


def _unpack(block, shapes):
    flat = block.reshape(-1)
    out, o = [], 0
    for s in shapes:
        n = math.prod(s)
        out.append(flat[o:o + n].reshape(s))
        o += n
    return out


def _cols(g):
    return g.transpose(1, 0, 2).reshape(g.shape[1], -1)


def _uncols(w):
    return w.reshape(w.shape[0], 4, -1).transpose(1, 0, 2)


def _pad_cols(w, total):
    return jnp.pad(w, ((0, 0), (0, total - w.shape[1])))


def kernel(*args):
    a = dict(zip(INPUTS, args))
    x, tgt = a['x'][0], a['loss_target'][0]
    t, d = x.shape
    xi, yi, ci = lax.axis_index("x"), lax.axis_index("y"), lax.axis_index("c")
    chip = 2 * xi + yi
    dk, dv, hk, hv = _gla_dims(d)
    lw = d // 2
    di = 2 * d
    nh = di // SSD_HEAD_DIM
    gn_w = SSD_GROUPS * SSD_STATE
    conv_dim = di + 2 * gn_w
    rank = GLA_GATE_RANK
    wq = 2 * dk + 2 * dv
    gla_w = _round_up(wq + LANE, 2 * lw)
    ev_tot = gla_w + 2 * lw
    od_used = di + conv_dim + nh
    od_tot = _round_up(di + conv_dim + _round_up(nh, LANE), 512)
    glr_col, xcol, dtcol = wq // LANE, gla_w // lw, (di + conv_dim) // LANE
    assert ev_tot % 512 == 0 and nh <= LANE

    def halves(w):
        w = w.astype(BF16)
        return w.reshape((2, w.shape[0] // 2) + w.shape[1:])

    own = {'ev_w_in': halves(a['ev_w_in'][0]), 'ev_w_out': halves(a['ev_w_out'][0]),
           'od_w_in_a': halves(a['od_w_in'][0][:d // 2]), 'od_w_in_b': halves(a['od_w_in'][0][d // 2:]),
           'od_w_out': halves(a['od_w_out'][0])}
    for l in range(2):
        own[f'gate{l}'], own[f'up{l}'] = halves(a['ffn_w_gate'][l]), halves(a['ffn_w_up'][l])
        own[f'down{l}'] = halves(a['ffn_w_down'][l])

    def gather(*units):
        return gather_weights([own[u] for u in units])

    def filled(unit, g):
        g = lax.dynamic_update_index_in_dim(g, own[unit], chip, 0)
        return g.reshape((4, 2 * g.shape[2]) + g.shape[3:])

    g_ev_in, g_ev_out = run_comm(gather('ev_w_in', 'ev_w_out'), name="gather_ev")
    w_ev_in = _cols(filled('ev_w_in', g_ev_in))
    cuts = [dk, 2 * dk, 2 * dk + dv, wq, wq + rank, wq + rank + lw]
    sq, sk, sv, sg, sglr, sxb, sgb = jnp.split(w_ev_in, cuts, axis=1)
    w_ev_in_p = jnp.concatenate([_pad_cols(jnp.concatenate([sq, sk, sv, sg, sglr], axis=1), gla_w), sxb, sgb], axis=1)
    w_ev_out = filled('ev_w_out', g_ev_out).reshape(-1, d)
    w_gate, w_up, w_down = [None, None], [None, None], [None, None]

    sh_names = list(SMALL_SHARDED)
    sh_shapes = [a[n].shape for n in sh_names]
    g8 = gather8(_pack([a[n] for n in sh_names]))
    per_chip = [_unpack(g8[2 * j], sh_shapes) for j in range(4)]
    full = {n: jnp.concatenate([per_chip[j][i] for j in range(4)], axis=SMALL_SHARDED[n])
            for i, n in enumerate(sh_names)}

    wg_p = jnp.zeros((LANE, dk), F32).at[:rank].set(full['ev_gla_w_gate'][0])
    bg, wn = a['ev_gla_b_gate'], a['ev_gla_w_onorm']
    lru_p = [full['ev_lru_conv_w'][0], a['ev_lru_conv_b'], a['ev_lru_w_a'][0], a['ev_lru_b_a'], a['ev_lru_w_i'][0],
             a['ev_lru_b_i'], a['ev_lru_lam']]
    od_cw, od_cb, od_gn = full['od_conv_w'][0], full['od_conv_b'], full['od_gnorm']
    heads = jnp.arange(LANE)[:, None]
    e_mat = ((jnp.arange(di)[None, :] // SSD_HEAD_DIM == heads) & (heads < nh)).astype(BF16)
    row8 = lambda p: jnp.zeros((SUBLANE, LANE), F32).at[0, :nh].set(p[0])
    dt_bias_p = jnp.zeros((1, LANE), F32).at[0, :nh].set(a['od_dt_bias'][0])
    alog_e = head_expand(row8(a['od_a_log']), e_mat, name="expand_a_log")
    dskip_e = head_expand(row8(a['od_d_skip']), e_mat, name="expand_d_skip")

    h0 = rms_fwd(x, a['ev_norm'], name="rms_ev")
    proj, (g,) = matmul(h0, w_ev_in_p, name="ev_in", carry=gather('gate0'))
    w_gate[0] = filled('gate0', g)
    (o_gla, sp_gla), (g,) = gla_fwd(proj, glr_col, wg_p, bg, wn, dv, carry=gather('up0'))
    w_up[0] = filled('up0', g)
    o_lru, hin = lru_fwd(proj, xcol, lw, *lru_p)
    x1 = matmul(o_gla, w_ev_out[:dv], add=x, name="ev_out_a")
    x1 = matmul(o_lru, w_ev_out[dv:], add=x1, name="ev_out_b")

    h1 = rms_fwd(x1, a['ffn_norm'][0:1], name="rms_ffn0")
    gate0, (g,) = matmul(h1, w_gate[0], name="ffn0_gate", carry=gather('down0'))
    w_down[0] = filled('down0', g).reshape(-1, d)
    swi = dict(epi=_swi_fwd_epi, epi_out=(F32, BF16))
    (up0, act0), (g_a,) = matmul(h1, w_up[0], name="ffn0_up", epi_in=(gate0,), carry=gather('od_w_in_a'), **swi)
    x2, (g_b,) = matmul(act0, w_down[0], add=x1, name="ffn0_down", carry=gather('od_w_in_b'))
    w_od_in = jnp.concatenate([filled('od_w_in_a', g_a), filled('od_w_in_b', g_b)], axis=1)
    w_od_in_p = _pad_cols(_cols(w_od_in), od_tot)

    h2 = rms_fwd(x2, full['od_norm'], name="rms_od")
    proj2, (g, g1) = matmul(h2, w_od_in_p, name="od_in", carry=gather('od_w_out', 'gate1'))
    w_od_out, w_gate[1] = filled('od_w_out', g).reshape(-1, d), filled('gate1', g1)
    xs, (g,) = conv_silu_fwd(proj2, di, conv_dim, od_cw, od_cb, carry=gather('up1'))
    w_up[1] = filled('up1', g)
    dte = dt_fwd(proj2, dtcol, dt_bias_p, e_mat)
    (y_ssd, sp_ssd), (g,) = ssd_fwd(xs, proj2, dte, alog_e, dskip_e, od_gn, carry=gather('down1'))
    w_down[1] = filled('down1', g).reshape(-1, d)
    x3 = matmul(y_ssd, w_od_out, add=x2, name="od_out")
    h3 = rms_fwd(x3, a['ffn_norm'][1:2], name="rms_ffn1")
    gate1 = matmul(h3, w_gate[1], name="ffn1_gate")
    up1, act1 = matmul(h3, w_up[1], name="ffn1_up", epi_in=(gate1,), **swi)
    x4 = matmul(act1, w_down[1], add=x3, name="ffn1_down")
    loss_p, dx4, dx4b, d_final = loss_head(x4, a['final_norm'][None], tgt, name="loss_head")

    grads, from_sib, part, from_chips = {}, {}, {}, {}

    def rows4(dw):
        return dw.reshape((4, 2, dw.shape[0] // 8) + dw.shape[1:])

    def cols4(dw):
        return dw.reshape((4, 2, dw.shape[1] // 2) + dw.shape[2:])

    def exchange(*units):
        return exchange_halves([grads[u] for u in units])

    def scatter(*units):
        return scatter_chips([part[u] for u in units])

    def sum_chip(u):
        part[u] = chip_sum(grads[u], from_sib[u], ci, name=f"chip_sum_{u}")

    def ffn_bwd(dxo, dxob, xin, h, gate, up, act, l, first_carry, first_units):
        dn, gt, up_ = f'down{l}', f'gate{l}', f'up{l}'
        dgu = matmul(dxob, w_down[l], tb=True, name=f"ffn{l}_d_act", carry=first_carry, epi=_swi_bwd_epi,
                     epi_in=(gate, up), epi_out=(BF16, BF16))
        (dg, du), got = dgu if first_units else (dgu, ())
        for u, r in zip(first_units, got):
            from_sib[u] = r
            sum_chip(u)
        d_down = matmul(act, dxob, ta=True, out_dtype=BF16, name=f"ffn{l}_dw_down",
                        carry=scatter(first_units[0]) if first_units else None)
        if first_units:
            d_down, (from_chips[first_units[0]],) = d_down
        grads[dn] = rows4(d_down)
        dh, (from_sib[dn],) = matmul(dg, w_gate[l], tb=True, name=f"ffn{l}_dh_gate", carry=exchange(dn))
        sum_chip(dn)
        dh = matmul(du, w_up[l], tb=True, add=dh, name=f"ffn{l}_dh_up",
                    carry=scatter(first_units[1]) if len(first_units) > 1 else None)
        if len(first_units) > 1:
            dh, (from_chips[first_units[1]],) = dh
        d_gate, (from_chips[dn],) = matmul(h, dg, ta=True, out_dtype=BF16, out_shards=True, name=f"ffn{l}_dw_gate",
                                           carry=scatter(dn))
        grads[gt] = cols4(d_gate)
        d_up, (from_sib[gt],) = matmul(h, du, ta=True, out_dtype=BF16, out_shards=True, name=f"ffn{l}_dw_up",
                                       carry=exchange(gt))
        grads[up_] = cols4(d_up)
        return rms_bwd(xin, a['ffn_norm'][l:l + 1], dh, dxo, name=f"rms_ffn{l}_bwd")

    dx3, dx3b, d_fn1 = ffn_bwd(dx4, dx4b, x3, h3, gate1, up1, act1, 1, None, ())
    dy, (from_sib['up1'],) = matmul(dx3b, w_od_out, tb=True, name="od_out_dy", carry=exchange('up1'))
    sum_chip('gate1')
    sum_chip('up1')
    grads['od_w_out'] = rows4(matmul(y_ssd, dx3b, ta=True, out_dtype=BF16, name="od_out_dw"))
    (dxs, dproj2, ddte, dal, dds, dgn), (from_chips['gate1'], from_chips['up1'], from_sib['od_w_out']) = ssd_bwd(
        xs, proj2, dte, alog_e, dskip_e, od_gn, sp_ssd, dy, proj2.shape,
        carry=merge_carries(scatter('gate1', 'up1'), exchange('od_w_out')))
    sum_chip('od_w_out')
    (dproj2, d_od_cw, d_od_cb), (from_chips['od_w_out'],) = conv_silu_bwd(
        proj2, di, conv_dim, od_cw, od_cb, dxs, dproj2, carry=scatter('od_w_out'))
    dproj2, d_dt_bias = dt_bwd(proj2, dtcol, dt_bias_p, e_mat, ddte, dproj2)
    dh2 = matmul(dproj2, w_od_in_p, tb=True, name="od_in_dh")
    d_od_in = matmul(h2, dproj2, ta=True, out_dtype=BF16, name="od_in_dw")[:, :od_used]
    d_od_in = _uncols(d_od_in)
    grads['od_w_in_a'], grads['od_w_in_b'] = cols4(d_od_in[:, :d // 2]), cols4(d_od_in[:, d // 2:])
    dx2, dx2b, d_od_norm = rms_bwd(x2, full['od_norm'], dh2, dx3, name="rms_od_bwd")
    to8 = lambda acc: jnp.zeros((SUBLANE, di), F32).at[0].set(acc.reshape(-1))
    d_a_log = head_expand(to8(dal), e_mat, transpose=True, name="reduce_a_log")[0:1, :nh]
    d_d_skip = head_expand(to8(dds), e_mat, transpose=True, name="reduce_d_skip")[0:1, :nh]

    od_in_units = ('od_w_in_a', 'od_w_in_b')
    dx1, dx1b, d_fn0 = ffn_bwd(dx2, dx2b, x1, h1, gate0, up0, act0, 0, exchange(*od_in_units), od_in_units)
    dmix, (from_sib['up0'],) = matmul(dx1b, w_ev_out, tb=True, name="ev_out_dmix", carry=exchange('up0'))
    sum_chip('gate0')
    sum_chip('up0')
    grads['ev_w_out'] = rows4(jnp.concatenate([matmul(o_gla, dx1b, ta=True, out_dtype=BF16, name="ev_out_dw_a"),
                                               matmul(o_lru, dx1b, ta=True, out_dtype=BF16, name="ev_out_dw_b")], axis=0))
    (dproj, d_wg, d_bg, d_wn), (from_sib['ev_w_out'],) = gla_bwd(
        proj, glr_col, wg_p, bg, wn, sp_gla, dmix, dv, gla_w, carry=exchange('ev_w_out'))
    sum_chip('ev_w_out')
    (dproj, *d_lru), (from_chips['ev_w_out'], from_chips['gate0']) = lru_bwd(
        proj, xcol, lw, *lru_p, hin, dmix, 1, dproj, carry=scatter('ev_w_out', 'gate0'))
    d_ev_in_p, (from_chips['up0'],) = matmul(h0, dproj, ta=True, out_dtype=BF16, name="ev_in_dw", carry=scatter('up0'))
    d_ev_in = jnp.concatenate([d_ev_in_p[:, :wq + rank], d_ev_in_p[:, gla_w:]], axis=1)
    grads['ev_w_in'] = cols4(_uncols(d_ev_in))
    (from_sib['ev_w_in'],) = run_comm(exchange('ev_w_in'), name="exchange_ev_in")
    sum_chip('ev_w_in')
    dh0, (from_chips['ev_w_in'],) = matmul(dproj, w_ev_in_p, tb=True, name="ev_in_dh", carry=scatter('ev_w_in'))
    dx0, _, d_ev_norm = rms_bwd(x, a['ev_norm'], dh0, dx1, name="rms_ev_bwd")

    units = list(grads)
    half = [mesh_sum(part[u], from_chips[u], chip, ci, name=f"mesh_sum_{u}") for u in units]
    done = dict(zip(units, run_comm(join_halves(half), name="join_halves")))
    layers = {n: [done[n]] for n in ('ev_w_in', 'ev_w_out', 'od_w_out')}
    layers['od_w_in'] = [jnp.concatenate([done[u].reshape(d // 2, -1) for u in od_in_units], axis=0)]
    layers.update({f'ffn_w_{u}': [done[f'{u}0'], done[f'{u}1']] for u in ('gate', 'up', 'down')})
    grad, delta, new_m, new_v = {}, {}, {}, {}
    for n in BIG:
        flip = a[n].shape[-1] % LANE != 0
        tr_ = (lambda t: jnp.swapaxes(t, -1, -2)) if flip else (lambda t: t)
        gs = [tr_(g.reshape(a[n].shape[-2:])) for g in layers[n]]
        outs = adamw(tr_(a[n]), gs, tr_(a['m_' + n]), tr_(a['v_' + n]), name=f"adamw_{n}")
        grad[n], delta[n], new_m[n], new_v[n] = (tr_(o) for o in outs)

    small_g = {
        'ev_norm': d_ev_norm, 'ev_gla_w_gate': d_wg[:rank][None], 'ev_gla_b_gate': d_bg, 'ev_gla_w_onorm': d_wn,
        'ev_lru_conv_w': d_lru[0][None], 'ev_lru_conv_b': d_lru[1], 'ev_lru_w_a': d_lru[2][None],
        'ev_lru_b_a': d_lru[3], 'ev_lru_w_i': d_lru[4][None], 'ev_lru_b_i': d_lru[5], 'ev_lru_lam': d_lru[6],
        'od_norm': d_od_norm, 'od_conv_w': d_od_cw[None], 'od_conv_b': d_od_cb, 'od_dt_bias': d_dt_bias[:, :nh],
        'od_a_log': d_a_log, 'od_d_skip': d_d_skip, 'od_gnorm': dgn.reshape(1, di),
        'ffn_norm': jnp.concatenate([d_fn0, d_fn1], axis=0), 'final_norm': d_final[0],
    }
    full_shapes = [small_g[n].shape for n in SMALL]
    summed = sum_leading(gather8(_pack([small_g[n] for n in SMALL])), name="sum_devices")
    for n, g in zip(SMALL, _unpack(summed, full_shapes)):
        if n in SMALL_SHARDED:
            ax = SMALL_SHARDED[n]
            sz = a[n].shape[ax]
            g = lax.dynamic_slice_in_dim(g, chip * sz, sz, axis=ax)
        grad[n] = g

    shapes = [a[n].shape for n in SMALL]
    packed = [_pack([src[n] if pre is None else a[pre + n] for n in SMALL])
              for src, pre in ((a, None), (grad, None), (None, 'm_'), (None, 'v_'))]
    small_out = adamw(packed[0], [packed[1]], packed[2], packed[3], name="adamw_small")
    for outd, blk in zip((delta, new_m, new_v), small_out[1:]):
        outd.update(zip(SMALL, _unpack(blk, shapes)))

    loss = lax.psum(loss_p[0, 0], ("x", "y", "c"))
    return (loss, dx0[None], *[grad[n] for n in WEIGHTS], *[delta[n] for n in WEIGHTS],
            *[new_m[n] for n in WEIGHTS], *[new_v[n] for n in WEIGHTS])
```

```python
import functools
import math

import jax
import jax.numpy as jnp
from jax import lax
from jax.experimental import pallas as pl
from jax.experimental.pallas import tpu as pltpu

F32 = jnp.float32
BF16 = jnp.bfloat16
MXU_DTYPE = jnp.bfloat16

NORM_EPS = 1e-6
CONV_WIDTH = 4
GLA_HEADS = 4
GLA_GATE_RANK = 16
GLA_GATE_NORM = 16.0
CHUNK = 64
LRU_BLOCK = 128
LRU_C = 8.0
SSD_HEAD_DIM = 64
SSD_GROUPS = 8
SSD_STATE = 128
ADAM_LR, ADAM_B1, ADAM_B2, ADAM_EPS, ADAM_WD, ADAM_STEP = 0.001, 0.9, 0.999, 1e-08, 0.01, 10

LANE = 128
SUBLANE = 8
VMEM_LIMIT = 48 * 1024 * 1024
MAX_TK = 2816
MATMUL_VMEM_BUDGET = 45 * 1024 * 1024
MESH = pl.DeviceIdType.MESH

WEIGHTS = ['ev_norm', 'ev_w_in', 'ev_gla_w_gate', 'ev_gla_b_gate', 'ev_gla_w_onorm', 'ev_lru_conv_w', 'ev_lru_conv_b',
           'ev_lru_w_a', 'ev_lru_b_a', 'ev_lru_w_i', 'ev_lru_b_i', 'ev_lru_lam', 'ev_w_out', 'od_norm', 'od_w_in',
           'od_conv_w', 'od_conv_b', 'od_dt_bias', 'od_a_log', 'od_d_skip', 'od_gnorm', 'od_w_out', 'ffn_norm',
           'ffn_w_gate', 'ffn_w_up', 'ffn_w_down', 'final_norm']
BIG = ['ev_w_in', 'ev_w_out', 'od_w_in', 'od_w_out', 'ffn_w_gate', 'ffn_w_up', 'ffn_w_down']
SMALL_SHARDED = {'ev_gla_w_gate': 2, 'ev_lru_conv_w': 2, 'od_norm': 1, 'od_conv_w': 2, 'od_conv_b': 1, 'od_gnorm': 1}
SMALL = [n for n in WEIGHTS if n not in BIG]


def _cparams(sem=None, **kw):
    return pltpu.CompilerParams(dimension_semantics=sem, vmem_limit_bytes=VMEM_LIMIT, **kw)


def _full(shape):
    n = len(shape)
    return pl.BlockSpec(shape, lambda *_: (0,) * n)


ANY = pl.BlockSpec(memory_space=pl.ANY)


class Carry:
    def __init__(self, arrays, out_shape, n_sems, start, finish, aliases=None):
        self.arrays, self.out_shape, self.n_sems = list(arrays), list(out_shape), n_sems
        self.start, self.finish, self.aliases = start, finish, dict(aliases or {})


def merge_carries(*cs):
    cs = [c for c in cs if c is not None]
    if not cs:
        return None
    arrays = [a for c in cs for a in c.arrays]
    out_shape = [s for c in cs for s in c.out_shape]
    offs, i0, o0, s0 = [], 0, 0, 0
    aliases = {}
    for c in cs:
        offs.append((i0, o0, s0))
        aliases.update({i0 + i: o0 + o for i, o in c.aliases.items()})
        i0, o0, s0 = i0 + len(c.arrays), o0 + len(c.out_shape), s0 + c.n_sems

    def both(which):
        def run(ins, outs, send, recv, base):
            for c, (i, o, s) in zip(cs, offs):
                getattr(c, which)(ins[i:i + len(c.arrays)], outs[o:o + len(c.out_shape)], send, recv, base + s)
        return run

    return Carry(arrays, out_shape, s0, both("start"), both("finish"), aliases)


def _pcall(body, *, name, grid, in_specs, out_specs, out_shape, scratch_shapes=(), compiler_params, carry=None,
           input_output_aliases=None):
    aliases = dict(input_output_aliases or {})
    if carry is None:
        return pl.pallas_call(body, name=name, grid=grid, in_specs=in_specs, out_specs=out_specs, out_shape=out_shape,
                              scratch_shapes=list(scratch_shapes), compiler_params=compiler_params,
                              input_output_aliases=aliases)
    single = not isinstance(out_specs, (list, tuple))
    specs_o = [out_specs] if single else list(out_specs)
    shapes_o = [out_shape] if single else list(out_shape)
    n_in, n_out, k_in, k_out, n_scr = len(in_specs), len(specs_o), len(carry.arrays), len(carry.out_shape), len(scratch_shapes)

    def wrapped(*refs):
        ins, cins = refs[:n_in], refs[n_in:n_in + k_in]
        o0 = n_in + k_in
        outs, couts = refs[o0:o0 + n_out], refs[o0 + n_out:o0 + n_out + k_out]
        scr = refs[o0 + n_out + k_out:o0 + n_out + k_out + n_scr]
        send, recv = refs[-2:]
        ids = [pl.program_id(ax) for ax in range(len(grid))]
        first = functools.reduce(jnp.logical_and, [i == 0 for i in ids])
        last = functools.reduce(jnp.logical_and, [i == g - 1 for i, g in zip(ids, grid)])

        @pl.when(first)
        def _():
            carry.start(cins, couts, send, recv, 0)

        body(*ins, *outs, *scr)

        @pl.when(last)
        def _():
            carry.finish(cins, couts, send, recv, 0)

    aliases.update({n_in + i: n_out + o for i, o in carry.aliases.items()})
    call = pl.pallas_call(
        wrapped, name=name, grid=grid, in_specs=list(in_specs) + [ANY] * k_in, out_specs=specs_o + [ANY] * k_out,
        out_shape=shapes_o + carry.out_shape,
        scratch_shapes=list(scratch_shapes) + [pltpu.SemaphoreType.DMA((carry.n_sems,))] * 2,
        compiler_params=_cparams(("arbitrary",) * len(grid)), input_output_aliases=aliases)

    def run(*args):
        res = call(*args, *carry.arrays)
        main = res[:n_out]
        return (main[0] if single else list(main)), list(res[n_out:])

    return run


def _pick(dim, cands):
    for c in cands:
        if dim % c == 0:
            return c
    return dim


def _dims(a, ca, cb):
    nb = a.ndim - 2
    return (((ca + nb,), (cb + nb,)), (tuple(range(nb)), tuple(range(nb))))


def _dot(a, b, ca, cb):
    return lax.dot_general(a.astype(MXU_DTYPE), b.astype(MXU_DTYPE), _dims(a, ca, cb), preferred_element_type=F32)


@jax.custom_vjp
def mm(a, b):
    return _dot(a, b, 1, 0)


def _mm_f(a, b):
    return mm(a, b), (a, b)


def _mm_b(res, g):
    a, b = res
    return mm_nt(g, b).astype(a.dtype), mm_tn(a, g).astype(b.dtype)


@jax.custom_vjp
def mm_nt(a, b):
    return _dot(a, b, 1, 1)


def _mm_nt_f(a, b):
    return mm_nt(a, b), (a, b)


def _mm_nt_b(res, g):
    a, b = res
    return mm(g, b).astype(a.dtype), mm_tn(g, a).astype(b.dtype)


@jax.custom_vjp
def mm_tn(a, b):
    return _dot(a, b, 0, 0)


def _mm_tn_f(a, b):
    return mm_tn(a, b), (a, b)


def _mm_tn_b(res, g):
    a, b = res
    return mm_nt(b, g).astype(a.dtype), mm(a, g).astype(b.dtype)


mm.defvjp(_mm_f, _mm_b)
mm_nt.defvjp(_mm_nt_f, _mm_nt_b)
mm_tn.defvjp(_mm_tn_f, _mm_tn_b)


def _split3(a):
    h = a.astype(BF16)
    r = a - h.astype(F32)
    m = r.astype(BF16)
    l = (r - m.astype(F32)).astype(BF16)
    return h, m, l


def _exact_dot(t, a, ca, cb):
    out = None
    for p in _split3(a):
        d = lax.dot_general(t, p, _dims(a, ca, cb), preferred_element_type=F32)
        out = d if out is None else out + d
    return out


@jax.custom_vjp
def sel_l(t, a):
    return _exact_dot(t, a, 1, 0)


def _sel_l_f(t, a):
    return sel_l(t, a), t


def _sel_l_b(t, g):
    return jnp.zeros_like(t), _exact_dot(t, g, 0, 0)


sel_l.defvjp(_sel_l_f, _sel_l_b)


@jax.custom_vjp
def sel_r(a, t):
    out = None
    for p in _split3(a):
        d = lax.dot_general(p, t, (((1,), (0,)), ((), ())), preferred_element_type=F32)
        out = d if out is None else out + d
    return out


def _sel_r_f(a, t):
    return sel_r(a, t), t


def _sel_r_b(t, g):
    out = None
    for p in _split3(g):
        d = lax.dot_general(p, t, (((1,), (1,)), ((), ())), preferred_element_type=F32)
        out = d if out is None else out + d
    return out, jnp.zeros_like(t)


sel_r.defvjp(_sel_r_f, _sel_r_b)


def _sigmoid(x):
    return 1.0 / (1.0 + jnp.exp(-x))


def _silu(x):
    return x * _sigmoid(x)


def _softplus(x):
    return jnp.maximum(x, 0.0) + jnp.log(1.0 + jnp.exp(-jnp.abs(x)))


def _log_sigmoid(x):
    return -_softplus(-x)


def _gelu_tanh(x):
    c = math.sqrt(2.0 / math.pi)
    return 0.5 * x * (1.0 + jnp.tanh(c * (x + 0.044715 * (x * x * x))))


def _rms(x, w):
    return x * lax.rsqrt(jnp.mean(x * x, axis=-1, keepdims=True) + NORM_EPS) * w


def _tri(n, dtype=BF16):
    r = lax.broadcasted_iota(jnp.int32, (n, n), 0)
    c = lax.broadcasted_iota(jnp.int32, (n, n), 1)
    return (c <= r).astype(dtype)


def matmul(a, b, *, ta=False, tb=False, add=None, out_dtype=F32, out_shards=False, carry=None, name,
           epi=None, epi_in=(), epi_out=()):
    m, k = (a.shape[1], a.shape[0]) if ta else a.shape
    b_sh = b.ndim == 3
    if b_sh:
        s, br, bc = b.shape
        k2, n = (s * bc, br) if tb else (br, s * bc)
    else:
        k2, n = (b.shape[1], b.shape[0]) if tb else b.shape
    assert k == k2, (a.shape, b.shape, ta, tb)
    tk_opts = [bc] if b_sh and tb else [k] if k <= MAX_TK else \
        [c for c in range(MAX_TK, LANE - 1, -LANE) if k % c == 0][:1]
    n_add, n_x = int(add is not None), len(epi_in)
    out_dtypes = list(epi_out) if epi is not None else [out_dtype]
    n_o = len(out_dtypes)
    tn_opts = [bc] if b_sh and not tb else [n // 4] if out_shards else \
        [c for c in range(2048, LANE - 1, -LANE) if n % c == 0] or [n]
    tm_opts = [c for c in range(2048, LANE - 1, -LANE) if m % c == 0] or [m]
    sa, sb = a.dtype.itemsize, b.dtype.itemsize
    per_elem = sum(jnp.dtype(dt).itemsize for dt in out_dtypes) + 4 * n_add + sum(x.dtype.itemsize for x in epi_in)
    best = None
    for tk_ in tk_opts:
        for tm_ in tm_opts:
            for tn_ in tn_opts:
                vmem = 2 * (tm_ * tk_ * sa + tk_ * tn_ * sb) + tm_ * tn_ * (4 + 2 * per_elem)
                vmem += tm_ * tn_ * 4
                if vmem > MATMUL_VMEM_BUDGET and (tk_, tm_, tn_) != (tk_opts[-1], tm_opts[-1], tn_opts[-1]):
                    continue
                moved = m * k * sa * (1 if tk_ == k else n // tn_) + k * n * sb * (m // tm_)
                if best is None or (moved, -tk_, -tm_ * tn_) < best[0]:
                    best = ((moved, -tk_, -tm_ * tn_), tm_, tn_, tk_)
    _, tm, tn, tk = best
    nk = k // tk

    def body(*refs):
        a_ref, b_ref = refs[:2]
        x_refs = refs[2 + n_add:2 + n_add + n_x]
        o_refs = refs[2 + n_add + n_x:2 + n_add + n_x + n_o]
        acc = refs[-1]
        kk = pl.program_id(2)

        @pl.when(kk == 0)
        def _():
            acc[...] = jnp.zeros_like(acc)

        acc[...] += _dot(a_ref[...], b_ref[...], 0 if ta else 1, 1 if tb else 0)

        @pl.when(kk == nk - 1)
        def _():
            r = acc[...]
            if add is not None:
                r = r + refs[2][...].astype(F32)
            vals = (r,) if epi is None else epi(r, *[x[...] for x in x_refs])
            for o_ref, v in zip(o_refs, vals):
                o_ref[...] = v.astype(o_ref.dtype)

    a_spec = pl.BlockSpec((tk, tm), lambda i, j, kk: (kk, i)) if ta else pl.BlockSpec((tm, tk), lambda i, j, kk: (i, kk))
    if b_sh and tb:
        b_spec = pl.BlockSpec((None, tn, tk), lambda i, j, kk: (kk, j, 0))
    elif b_sh:
        b_spec = pl.BlockSpec((None, tk, tn), lambda i, j, kk: (j, kk, 0))
    elif tb:
        b_spec = pl.BlockSpec((tn, tk), lambda i, j, kk: (j, kk))
    else:
        b_spec = pl.BlockSpec((tk, tn), lambda i, j, kk: (kk, j))
    in_specs, args = [a_spec, b_spec], [a, b]
    tile = pl.BlockSpec((tm, tn), lambda i, j, kk: (i, j))
    for extra in ([add] if add is not None else []) + list(epi_in):
        in_specs.append(tile)
        args.append(extra)
    if out_shards:
        out_spec = pl.BlockSpec((None, tm, tn), lambda i, j, kk: (j, i, 0))
        out_shape = jax.ShapeDtypeStruct((4, m, tn), out_dtype)
    elif epi is not None:
        out_spec = [tile] * n_o
        out_shape = [jax.ShapeDtypeStruct((m, n), dt) for dt in out_dtypes]
    else:
        out_spec = tile
        out_shape = jax.ShapeDtypeStruct((m, n), out_dtype)
    return _pcall(
        body, name=name, grid=(m // tm, n // tn, nk), in_specs=in_specs, out_specs=out_spec, out_shape=out_shape,
        scratch_shapes=[pltpu.VMEM((tm, tn), F32)],
        compiler_params=_cparams(("parallel", "parallel", "arbitrary")), carry=carry,
    )(*args)


def rms_fwd(x, w, *, name):
    t, d = x.shape
    tb = _pick(t, (256, 128, 64))

    def body(x_ref, w_ref, o_ref):
        o_ref[...] = _rms(x_ref[...], w_ref[...]).astype(o_ref.dtype)

    return pl.pallas_call(
        body, name=name, grid=(t // tb,),
        in_specs=[pl.BlockSpec((tb, d), lambda i: (i, 0)), _full((1, d))],
        out_specs=pl.BlockSpec((tb, d), lambda i: (i, 0)),
        out_shape=jax.ShapeDtypeStruct((t, d), BF16),
        compiler_params=_cparams(("parallel",)),
    )(x, w)


def rms_bwd(x, w, dh, dres, *, name, carry=None):
    t, d = x.shape
    tb = _pick(t, (256, 128, 64))

    def body(x_ref, w_ref, dh_ref, dres_ref, dx_ref, dxb_ref, dw_ref):
        @pl.when(pl.program_id(0) == 0)
        def _():
            dw_ref[...] = jnp.zeros_like(dw_ref)

        _, vjp = jax.vjp(_rms, x_ref[...], w_ref[...])
        dx, dw = vjp(dh_ref[...].astype(F32))
        dx = dx + dres_ref[...]
        dx_ref[...] = dx
        dxb_ref[...] = dx.astype(dxb_ref.dtype)
        dw_ref[...] += dw

    row = pl.BlockSpec((tb, d), lambda i: (i, 0))
    return _pcall(
        body, name=name, grid=(t // tb,), carry=carry,
        in_specs=[row, _full((1, d)), row, row],
        out_specs=[row, row, _full((1, d))],
        out_shape=[jax.ShapeDtypeStruct((t, d), F32), jax.ShapeDtypeStruct((t, d), BF16),
                   jax.ShapeDtypeStruct((1, d), F32)],
        compiler_params=_cparams(("arbitrary",)),
    )(x, w, dh, dres)


def _swi(g, u):
    return _silu(g) * u


def _swi_fwd_epi(u, g):
    return u, _swi(g, u)


def _swi_bwd_epi(d, g, u):
    return jax.vjp(_swi, g, u)[1](d)


def loss_head(x, w, target, *, name):
    t, d = x.shape
    tb = _pick(t, (256, 128, 64))

    def f(xv, wv, tv):
        y = _rms(xv, wv)
        e = y - tv
        return 0.5 * jnp.sum(jnp.mean(e * e, axis=-1, keepdims=True), axis=0, keepdims=True)

    def body(x_ref, w_ref, t_ref, l_ref, dx_ref, dxb_ref, dw_ref):
        @pl.when(pl.program_id(0) == 0)
        def _():
            l_ref[...] = jnp.zeros_like(l_ref)
            dw_ref[...] = jnp.zeros_like(dw_ref)

        val, vjp = jax.vjp(lambda a, b: f(a, b, t_ref[...]), x_ref[...], w_ref[...])
        dx, dw = vjp(jnp.ones((1, 1), F32))
        l_ref[...] += jnp.broadcast_to(val, l_ref.shape)
        dx_ref[...] = dx
        dxb_ref[...] = dx.astype(dxb_ref.dtype)
        dw_ref[...] += dw

    row = pl.BlockSpec((tb, d), lambda i: (i, 0))
    return pl.pallas_call(
        body, name=name, grid=(t // tb,),
        in_specs=[row, _full((1, d)), row],
        out_specs=[_full((SUBLANE, LANE)), row, row, _full((1, d))],
        out_shape=[jax.ShapeDtypeStruct((SUBLANE, LANE), F32), jax.ShapeDtypeStruct((t, d), F32),
                   jax.ShapeDtypeStruct((t, d), BF16),
                   jax.ShapeDtypeStruct((1, d), F32)],
        compiler_params=_cparams(("arbitrary",)),
    )(x, w, target)


def _gla_chunk(q, k, v, g, glr, st, wg, bg, wn, tri):
    L, hk = q.shape[-2:]
    la = _log_sigmoid(mm(glr, wg) + bg) / GLA_GATE_NORM
    bcum = sel_l(jnp.broadcast_to(tri, la.shape[:-2] + tri.shape), la)
    b_last = jnp.sum(la, axis=-2, keepdims=True)
    rows = lax.broadcasted_iota(jnp.int32, (L, 1), 0)
    b_mid = jnp.sum(jnp.where(rows <= L // 2, la, 0.0), axis=-2, keepdims=True)
    qs = q * (hk ** -0.5)
    q_in = qs * jnp.exp(bcum - b_mid)
    k_in = k * jnp.exp(b_mid - bcum)
    scores = mm_nt(q_in, k_in) * tri.astype(F32)
    o_intra = mm(scores, v)
    k_st = k * jnp.exp(b_last - bcum)
    d_st = mm_tn(v, k_st)
    o_inter = mm_nt(qs * jnp.exp(bcum), st)
    st_new = jnp.exp(b_last) * st + d_st
    o = _rms(o_intra + o_inter, wn) * _silu(g)
    return o, st_new


def _heads(ref, start, width, n):
    return jnp.stack([ref[:, start + h * width:start + (h + 1) * width] for h in range(n)], axis=0)


def _gla_heads(p_ref, dk, dv, n):
    hk, hv = dk // n, dv // n
    return (_heads(p_ref, 0, hk, n), _heads(p_ref, dk, hk, n), _heads(p_ref, 2 * dk, hv, n),
            _heads(p_ref, 2 * dk + dv, hv, n))


def _gla_dims(d):
    dv = d // 2
    dk = dv // 2
    return dk, dv, dk // GLA_HEADS, dv // GLA_HEADS


def gla_fwd(proj, glr_col, wg, bg, wn, dv, carry=None):
    t = proj.shape[0]
    dk, dv, hk, hv = _gla_dims(2 * dv)
    L, H = CHUNK, GLA_HEADS
    nc = t // L
    wq = 2 * dk + 2 * dv

    def body(p_ref, glr_ref, wg_ref, bg_ref, wn_ref, o_ref, sp_ref, st):
        @pl.when(pl.program_id(0) == 0)
        def _():
            st[...] = jnp.zeros_like(st)

        s_prev = st[...]
        sp_ref[0] = s_prev
        o, s_new = _gla_chunk(*_gla_heads(p_ref, dk, dv, H), jnp.broadcast_to(glr_ref[...], (H, L, LANE)), s_prev,
                              _heads(wg_ref, 0, hk, H), _heads(bg_ref, 0, hk, H), wn_ref[...], _tri(L))
        for h in range(H):
            o_ref[:, h * hv:(h + 1) * hv] = o[h].astype(o_ref.dtype)
        st[...] = s_new

    return _pcall(
        body, carry=carry, name="gla_fwd", grid=(nc,),
        in_specs=[pl.BlockSpec((L, wq), lambda c: (c, 0)), pl.BlockSpec((L, LANE), lambda c: (c, glr_col)),
                  _full(wg.shape), _full(bg.shape), _full(wn.shape)],
        out_specs=[pl.BlockSpec((L, dv), lambda c: (c, 0)), pl.BlockSpec((1, H, hv, hk), lambda c: (c, 0, 0, 0))],
        out_shape=[jax.ShapeDtypeStruct((t, dv), BF16), jax.ShapeDtypeStruct((nc, H, hv, hk), F32)],
        scratch_shapes=[pltpu.VMEM((H, hv, hk), F32)],
        compiler_params=_cparams(("arbitrary",)),
    )(proj, proj, wg, bg, wn)


def gla_bwd(proj, glr_col, wg, bg, wn, sprev, do, dv, gla_w, carry=None):
    t = proj.shape[0]
    dk, _, hk, hv = _gla_dims(2 * dv)
    L, H = CHUNK, GLA_HEADS
    nc = t // L
    wq = 2 * dk + 2 * dv

    def body(p_ref, glr_ref, wg_ref, bg_ref, wn_ref, sp_ref, do_ref, dp_ref, dwg_ref, dbg_ref, dwn_ref, dst):
        @pl.when(pl.program_id(0) == 0)
        def _():
            dst[...] = jnp.zeros_like(dst)
            dwg_ref[...] = jnp.zeros_like(dwg_ref)
            dbg_ref[...] = jnp.zeros_like(dbg_ref)
            dwn_ref[...] = jnp.zeros_like(dwn_ref)

        f = functools.partial(_gla_chunk, tri=_tri(L))
        _, vjp = jax.vjp(f, *_gla_heads(p_ref, dk, dv, H), jnp.broadcast_to(glr_ref[...], (H, L, LANE)), sp_ref[0],
                         _heads(wg_ref, 0, hk, H), _heads(bg_ref, 0, hk, H), wn_ref[...])
        dq, dkk, dvv, dg, dgl, ds, dwg, dbg, dwn = vjp((_heads(do_ref, 0, hv, H), dst[...]))
        for h in range(H):
            dp_ref[:, h * hk:(h + 1) * hk] = dq[h].astype(dp_ref.dtype)
            dp_ref[:, dk + h * hk:dk + (h + 1) * hk] = dkk[h].astype(dp_ref.dtype)
            dp_ref[:, 2 * dk + h * hv:2 * dk + (h + 1) * hv] = dvv[h].astype(dp_ref.dtype)
            dp_ref[:, 2 * dk + dv + h * hv:2 * dk + dv + (h + 1) * hv] = dg[h].astype(dp_ref.dtype)
            dwg_ref[:, h * hk:(h + 1) * hk] += dwg[h]
            dbg_ref[:, h * hk:(h + 1) * hk] += dbg[h]
        dst[...] = ds
        dwn_ref[...] += dwn
        dp_ref[:, wq:wq + LANE] = jnp.sum(dgl, axis=0).astype(dp_ref.dtype)
        if gla_w > wq + LANE:
            dp_ref[:, wq + LANE:] = jnp.zeros((L, gla_w - wq - LANE), dp_ref.dtype)

    rev = lambda c: nc - 1 - c
    return _pcall(
        body, carry=carry, name="gla_bwd", grid=(nc,),
        in_specs=[pl.BlockSpec((L, wq), lambda c: (rev(c), 0)), pl.BlockSpec((L, LANE), lambda c: (rev(c), glr_col)),
                  _full(wg.shape), _full(bg.shape), _full(wn.shape),
                  pl.BlockSpec((1, H, hv, hk), lambda c: (rev(c), 0, 0, 0)),
                  pl.BlockSpec((L, dv), lambda c: (rev(c), 0))],
        out_specs=[pl.BlockSpec((L, gla_w), lambda c: (rev(c), 0)),
                   _full(wg.shape), _full(bg.shape), _full(wn.shape)],
        out_shape=[jax.ShapeDtypeStruct(proj.shape, BF16),
                   jax.ShapeDtypeStruct(wg.shape, F32), jax.ShapeDtypeStruct(bg.shape, F32),
                   jax.ShapeDtypeStruct(wn.shape, F32)],
        scratch_shapes=[pltpu.VMEM((H, hv, hk), F32)],
        compiler_params=_cparams(("arbitrary",)),
    )(proj, proj, wg, bg, wn, sprev, do)


def _shift_down(x, tail, s):
    if s == 0:
        return x
    r = pltpu.roll(x, s, 0)
    rows = lax.broadcasted_iota(jnp.int32, tail.shape, 0)
    top = jnp.where(rows < s, pltpu.roll(tail, s, 0), r[:SUBLANE])
    return jnp.concatenate([top, r[SUBLANE:]], axis=0)


def _shift_up(x, head, s):
    if s == 0:
        return x
    n = x.shape[0]
    r = pltpu.roll(x, n - s, 0)
    rows = lax.broadcasted_iota(jnp.int32, head.shape, 0)
    bottom = jnp.where(rows >= SUBLANE - s, pltpu.roll(head, SUBLANE - s, 0), r[n - SUBLANE:])
    return jnp.concatenate([r[:n - SUBLANE], bottom], axis=0)


def _conv(x, prev, w, b):
    y = b
    for k in range(CONV_WIDTH):
        y = y + w[k:k + 1, :] * _shift_down(x, prev, CONV_WIDTH - 1 - k)
    return y


def _conv_bwd(dy, nxt, x, prev, w):
    dx = None
    dws = []
    for k in range(CONV_WIDTH):
        s = CONV_WIDTH - 1 - k
        term = w[k:k + 1, :] * _shift_up(dy, nxt, s)
        dx = term if dx is None else dx + term
        dws.append(jnp.sum(dy * _shift_down(x, prev, s), axis=0, keepdims=True))
    return dx, jnp.concatenate(dws, axis=0), jnp.sum(dy, axis=0, keepdims=True)


def _scan_fwd(a, u):
    n = a.shape[0]
    rows = lax.broadcasted_iota(jnp.int32, a.shape, 0)
    s = 1
    while s < n:
        a_sh = jnp.where(rows < s, 1.0, pltpu.roll(a, s, 0))
        u_sh = jnp.where(rows < s, 0.0, pltpu.roll(u, s, 0))
        u = a * u_sh + u
        a = a * a_sh
        s *= 2
    return a, u


def _scan_rev(c, d):
    n = c.shape[0]
    rows = lax.broadcasted_iota(jnp.int32, c.shape, 0)
    s = 1
    while s < n:
        c_sh = jnp.where(rows >= n - s, 0.0, pltpu.roll(c, n - s, 0))
        d_sh = jnp.where(rows >= n - s, 0.0, pltpu.roll(d, n - s, 0))
        d = d + c * d_sh
        c = c * c_sh
        s *= 2
    return d


def _expm1(x):
    small = x * (1.0 + x * (0.5 + x * (1.0 / 6.0 + x * (1.0 / 24.0))))
    return jnp.where(jnp.abs(x) < 1e-2, small, jnp.exp(x) - 1.0)


def _lru_gates(xc, pa, pi, lam):
    r = _sigmoid(pa)
    i = _sigmoid(pi)
    log_a = LRU_C * r * _log_sigmoid(lam)
    a = jnp.exp(log_a)
    u = jnp.sqrt(-_expm1(2.0 * log_a)) * (i * xc)
    return a, u


def _lru_out(h, gate):
    return h * _gelu_tanh(gate)


def _blockdiag(xc, w_ref, b):
    nb = w_ref.shape[0]
    outs = [mm(xc[:, n * LRU_BLOCK:(n + 1) * LRU_BLOCK], w_ref[n]) for n in range(nb)]
    return jnp.concatenate(outs, axis=1) + b


def lru_fwd(proj, xcol, lw, cw, cb, wa, ba, wi, bi, lam):
    t = proj.shape[0]
    tb = _pick(t, (256, 128, 64))
    nb = t // tb

    def body(x_ref, xp_ref, g_ref, cw_ref, cb_ref, wa_ref, ba_ref, wi_ref, bi_ref, lam_ref, o_ref, hin_ref, hc):
        i = pl.program_id(0)

        @pl.when(i == 0)
        def _():
            hc[...] = jnp.zeros_like(hc)

        prev = jnp.where(i == 0, 0.0, xp_ref[...])
        xc = _conv(x_ref[...], prev, cw_ref[...], cb_ref[...])
        a, u = _lru_gates(xc, _blockdiag(xc, wa_ref, ba_ref[...]), _blockdiag(xc, wi_ref, bi_ref[...]), lam_ref[...])
        acum, h0 = _scan_fwd(a, u)
        h = h0 + acum * hc[...]
        hin_ref[0] = hc[...]
        hc[...] = h[tb - 1:tb, :]
        o_ref[...] = _lru_out(h, g_ref[...]).astype(o_ref.dtype)

    row = lambda col: pl.BlockSpec((tb, lw), lambda i: (i, col))
    return pl.pallas_call(
        body, name="lru_fwd", grid=(nb,),
        in_specs=[row(xcol), pl.BlockSpec((SUBLANE, lw), lambda i: (jnp.maximum(i * (tb // SUBLANE) - 1, 0), xcol)),
                  row(xcol + 1),
                  _full(cw.shape), _full(cb.shape), _full(wa.shape), _full(ba.shape), _full(wi.shape), _full(bi.shape),
                  _full(lam.shape)],
        out_specs=[pl.BlockSpec((tb, lw), lambda i: (i, 0)), pl.BlockSpec((1, 1, lw), lambda i: (i, 0, 0))],
        out_shape=[jax.ShapeDtypeStruct((t, lw), BF16), jax.ShapeDtypeStruct((nb, 1, lw), F32)],
        scratch_shapes=[pltpu.VMEM((1, lw), F32)],
        compiler_params=_cparams(("arbitrary",)),
    )(proj, proj, proj, cw, cb, wa, ba, wi, bi, lam)


def lru_bwd(proj, xcol, lw, cw, cb, wa, ba, wi, bi, lam, hin, dmix, docol, dproj, carry=None):
    t = proj.shape[0]
    tb = _pick(t, (256, 128, 64))
    nb = t // tb
    nblk = wa.shape[0]
    assert xcol % 2 == 0

    def body(x_ref, xp_ref, g_ref, cw_ref, cb_ref, wa_ref, ba_ref, wi_ref, bi_ref, lam_ref, hin_ref, do_ref, _,
             dxg_ref, dcw_ref, dcb_ref, dwa_ref, dba_ref, dwi_ref, dbi_ref, dlam_ref, gc, dxcn):
        pid = pl.program_id(0)
        i = nb - 1 - pid

        @pl.when(pid == 0)
        def _():
            gc[...] = jnp.zeros_like(gc)
            dxcn[...] = jnp.zeros_like(dxcn)
            for r in (dcw_ref, dcb_ref, dwa_ref, dba_ref, dwi_ref, dbi_ref, dlam_ref):
                r[...] = jnp.zeros_like(r)

        x = x_ref[...]
        prev = jnp.where(i == 0, 0.0, xp_ref[...])
        cw_v = cw_ref[...]
        xc = _conv(x, prev, cw_v, cb_ref[...])
        pa = _blockdiag(xc, wa_ref, ba_ref[...])
        pi = _blockdiag(xc, wi_ref, bi_ref[...])
        (a, u), vjp_g = jax.vjp(_lru_gates, xc, pa, pi, lam_ref[...])
        acum, h0 = _scan_fwd(a, u)
        hi = hin_ref[0]
        h = h0 + acum * hi
        rows = lax.broadcasted_iota(jnp.int32, h.shape, 0)
        hprev = jnp.where(rows < 1, hi, pltpu.roll(h, 1, 0))
        _, vjp_o = jax.vjp(_lru_out, h, g_ref[...])
        dh, dgate = vjp_o(do_ref[...].astype(F32))
        c = jnp.where(rows >= tb - 1, 0.0, pltpu.roll(a, tb - 1, 0))
        g = _scan_rev(c, dh + jnp.where(rows == tb - 1, gc[...], 0.0))
        gc[...] = a[0:1, :] * g[0:1, :]
        dxc, dpa, dpi, dlam = vjp_g((g * hprev, g))
        dlam_ref[...] += dlam
        dba_ref[...] += jnp.sum(dpa, axis=0, keepdims=True)
        dbi_ref[...] += jnp.sum(dpi, axis=0, keepdims=True)
        parts = []
        for n in range(nblk):
            sl = slice(n * LRU_BLOCK, (n + 1) * LRU_BLOCK)
            dwa_ref[n] += mm_tn(xc[:, sl], dpa[:, sl])
            dwi_ref[n] += mm_tn(xc[:, sl], dpi[:, sl])
            parts.append(mm_nt(dpa[:, sl], wa_ref[n]) + mm_nt(dpi[:, sl], wi_ref[n]))
        dxc = dxc + jnp.concatenate(parts, axis=1)
        dx, dcw, dcb = _conv_bwd(dxc, dxcn[...], x, prev, cw_v)
        dxcn[...] = dxc[:SUBLANE]
        dcw_ref[...] += dcw
        dcb_ref[...] += dcb
        dxg_ref[:, :lw] = dx.astype(dxg_ref.dtype)
        dxg_ref[:, lw:] = dgate.astype(dxg_ref.dtype)

    row = lambda col: pl.BlockSpec((tb, lw), lambda p: (nb - 1 - p, col))
    params = [cw, cb, wa, ba, wi, bi, lam]
    return _pcall(
        body, carry=carry, name="lru_bwd", grid=(nb,),
        in_specs=[row(xcol),
                  pl.BlockSpec((SUBLANE, lw), lambda p: (jnp.maximum((nb - 1 - p) * (tb // SUBLANE) - 1, 0), xcol)),
                  row(xcol + 1)]
        + [_full(p.shape) for p in params]
        + [pl.BlockSpec((1, 1, lw), lambda p: (nb - 1 - p, 0, 0)), row(docol), ANY],
        out_specs=[pl.BlockSpec((tb, 2 * lw), lambda p: (nb - 1 - p, xcol // 2))] + [_full(p.shape) for p in params],
        out_shape=[jax.ShapeDtypeStruct(dproj.shape, dproj.dtype)]
        + [jax.ShapeDtypeStruct(p.shape, F32) for p in params],
        input_output_aliases={12: 0},
        scratch_shapes=[pltpu.VMEM((1, lw), F32), pltpu.VMEM((SUBLANE, lw), F32)],
        compiler_params=_cparams(("arbitrary",)),
    )(proj, proj, proj, *params, hin, dmix, dproj)


def conv_silu_fwd(proj, col0, width, cw, cb, carry=None):
    t = proj.shape[0]
    tb = _pick(t, (512, 256, 128, 64))
    cbw = _pick(width, (512, 256, 128))
    off = col0 // cbw
    assert col0 % cbw == 0

    def body(x_ref, xp_ref, w_ref, b_ref, o_ref):
        prev = jnp.where(pl.program_id(1) == 0, 0.0, xp_ref[...])
        o_ref[...] = _silu(_conv(x_ref[...], prev, w_ref[...], b_ref[...]))

    return _pcall(
        body, carry=carry, name="conv_silu_fwd", grid=(width // cbw, t // tb),
        in_specs=[pl.BlockSpec((tb, cbw), lambda j, i: (i, off + j)),
                  pl.BlockSpec((SUBLANE, cbw), lambda j, i: (jnp.maximum(i * (tb // SUBLANE) - 1, 0), off + j)),
                  pl.BlockSpec((CONV_WIDTH, cbw), lambda j, i: (0, j)), pl.BlockSpec((1, cbw), lambda j, i: (0, j))],
        out_specs=pl.BlockSpec((tb, cbw), lambda j, i: (i, j)),
        out_shape=jax.ShapeDtypeStruct((t, width), F32),
        compiler_params=_cparams(("parallel", "arbitrary")),
    )(proj, proj, cw, cb)


def conv_silu_bwd(proj, col0, width, cw, cb, dact, dproj, carry=None):
    t = proj.shape[0]
    tb = _pick(t, (512, 256, 128, 64))
    nb = t // tb
    cbw = _pick(width, (512, 256, 128))
    off = col0 // cbw

    def body(x_ref, xp_ref, w_ref, b_ref, d_ref, _, dx_ref, dw_ref, db_ref, nxt):
        pid = pl.program_id(1)
        i = nb - 1 - pid

        @pl.when(pid == 0)
        def _():
            nxt[...] = jnp.zeros_like(nxt)
            dw_ref[...] = jnp.zeros_like(dw_ref)
            db_ref[...] = jnp.zeros_like(db_ref)

        x = x_ref[...]
        prev = jnp.where(i == 0, 0.0, xp_ref[...])
        w = w_ref[...]
        _, vjp = jax.vjp(_silu, _conv(x, prev, w, b_ref[...]))
        (dcv,) = vjp(d_ref[...])
        dx, dw, db = _conv_bwd(dcv, nxt[...], x, prev, w)
        nxt[...] = dcv[:SUBLANE]
        dx_ref[...] = dx.astype(dx_ref.dtype)
        dw_ref[...] += dw
        db_ref[...] += db

    return _pcall(
        body, carry=carry, name="conv_silu_bwd", grid=(width // cbw, nb),
        in_specs=[pl.BlockSpec((tb, cbw), lambda j, p: (nb - 1 - p, off + j)),
                  pl.BlockSpec((SUBLANE, cbw),
                               lambda j, p: (jnp.maximum((nb - 1 - p) * (tb // SUBLANE) - 1, 0), off + j)),
                  pl.BlockSpec((CONV_WIDTH, cbw), lambda j, p: (0, j)), pl.BlockSpec((1, cbw), lambda j, p: (0, j)),
                  pl.BlockSpec((tb, cbw), lambda j, p: (nb - 1 - p, j)), ANY],
        out_specs=[pl.BlockSpec((tb, cbw), lambda j, p: (nb - 1 - p, off + j)),
                   pl.BlockSpec((CONV_WIDTH, cbw), lambda j, p: (0, j)), pl.BlockSpec((1, cbw), lambda j, p: (0, j))],
        out_shape=[jax.ShapeDtypeStruct(dproj.shape, dproj.dtype), jax.ShapeDtypeStruct(cw.shape, F32),
                   jax.ShapeDtypeStruct(cb.shape, F32)],
        scratch_shapes=[pltpu.VMEM((SUBLANE, cbw), F32)],
        input_output_aliases={5: 0},
        compiler_params=_cparams(("parallel", "arbitrary")),
    )(proj, proj, cw, cb, dact, dproj)


def _dt_expand(raw, bias, e):
    return sel_r(_softplus(raw + bias), e)


def dt_fwd(proj, dtcol, bias, e):
    t = proj.shape[0]
    di = e.shape[1]
    tb = _pick(t, (512, 256, 128, 64))

    def body(r_ref, b_ref, e_ref, o_ref):
        o_ref[...] = _dt_expand(r_ref[...], b_ref[...], e_ref[...])

    return pl.pallas_call(
        body, name="dt_fwd", grid=(t // tb,),
        in_specs=[pl.BlockSpec((tb, LANE), lambda i: (i, dtcol)), _full(bias.shape), _full(e.shape)],
        out_specs=pl.BlockSpec((tb, di), lambda i: (i, 0)),
        out_shape=jax.ShapeDtypeStruct((t, di), F32),
        compiler_params=_cparams(("parallel",)),
    )(proj, bias, e)


def dt_bwd(proj, dtcol, bias, e, ddte, dproj):
    t = proj.shape[0]
    di = e.shape[1]
    tb = _pick(t, (512, 256, 128, 64))
    tail = dproj.shape[1] - dtcol * LANE
    assert (dtcol * LANE) % tail == 0

    def body(r_ref, b_ref, e_ref, d_ref, _, dr_ref, db_ref):
        @pl.when(pl.program_id(0) == 0)
        def _():
            db_ref[...] = jnp.zeros_like(db_ref)

        e_v = e_ref[...]
        _, vjp = jax.vjp(lambda r, b: _dt_expand(r, b, e_v), r_ref[...], b_ref[...])
        dr, db = vjp(d_ref[...])
        dr_ref[:, :LANE] = dr.astype(dr_ref.dtype)
        if tail > LANE:
            dr_ref[:, LANE:] = jnp.zeros((tb, tail - LANE), dr_ref.dtype)
        db_ref[...] += db

    return pl.pallas_call(
        body, name="dt_bwd", grid=(t // tb,),
        in_specs=[pl.BlockSpec((tb, LANE), lambda i: (i, dtcol)), _full(bias.shape), _full(e.shape),
                  pl.BlockSpec((tb, di), lambda i: (i, 0)), ANY],
        out_specs=[pl.BlockSpec((tb, tail), lambda i: (i, dtcol * LANE // tail)), _full(bias.shape)],
        out_shape=[jax.ShapeDtypeStruct(dproj.shape, dproj.dtype), jax.ShapeDtypeStruct(bias.shape, F32)],
        input_output_aliases={4: 0},
        compiler_params=_cparams(("arbitrary",)),
    )(proj, bias, e, ddte, dproj)


def head_expand(p, e, *, transpose=False, name):
    di = e.shape[1]

    def body(p_ref, e_ref, o_ref):
        if transpose:
            o_ref[...] = _sel_r_b(e_ref[...], p_ref[...])[0]
        else:
            o_ref[...] = sel_r(p_ref[...], e_ref[...])

    oshape = (SUBLANE, LANE) if transpose else (SUBLANE, di)
    return pl.pallas_call(
        body, name=name, in_specs=[_full(p.shape), _full(e.shape)], out_specs=_full(oshape),
        out_shape=jax.ShapeDtypeStruct(oshape, F32), compiler_params=_cparams(None), grid=(1,),
    )(p, e)


def _ssd_chunk(x, z, bm, cm, dte, st, alog, dskip, gn, tri, cmask, dmask, bd):
    L, gw = x.shape
    reps = gw // L
    a = dte * (-jnp.exp(alog))
    acs = sel_l(tri, a)
    acs_last = jnp.sum(a, axis=0, keepdims=True)
    arow = jnp.sum(acs * dmask, axis=0, keepdims=True)
    dtrow = jnp.sum(dte * dmask, axis=0, keepdims=True)
    cb = mm_nt(cm, jnp.concatenate([bm] * reps, axis=0))
    wts = cb * (jnp.exp(jnp.minimum(acs - arow, 0.0)) * cmask) * dtrow
    xbd = jnp.concatenate([x] * reps, axis=0) * bd
    xw = x * (jnp.exp(acs_last - acs) * dte)
    y = mm(wts, xbd) + mm(cm, st) * jnp.exp(acs) + dskip * x
    st_new = jnp.exp(acs_last) * st + mm_tn(bm, xw)
    return _rms(y * _silu(z), gn), st_new


def _ssd_dims(di):
    gw = di // SSD_GROUPS
    assert CHUNK == SSD_HEAD_DIM and gw % LANE == 0
    return gw, SSD_STATE


def _ssd_masks(gw):
    L = CHUNK
    r = jnp.arange(L)[:, None]
    c = jnp.arange(gw)[None, :]
    cmask = ((c % L) <= r).astype(F32)
    dmask = ((c % L) == r).astype(F32)
    rr = jnp.arange(gw)
    bd = ((rr[:, None] // L) == (rr[None, :] // L)).astype(F32)
    tri = (jnp.arange(L)[None, :] <= jnp.arange(L)[:, None]).astype(BF16)
    return tri, cmask, dmask, bd


def ssd_fwd(xs, proj, dte, alog_e, dskip_e, gn, carry=None):
    t, di = dte.shape
    gw, n = _ssd_dims(di)
    L, G = CHUNK, SSD_GROUPS
    nc = t // L
    masks = _ssd_masks(gw)
    cdim = xs.shape[1]

    def body(x_ref, z_ref, dt_ref, al_ref, ds_ref, gn_ref, tri_ref, cm_ref, dm_ref, bd_ref, y_ref, sp_ref, st):
        @pl.when(pl.program_id(0) == 0)
        def _():
            st[...] = jnp.zeros_like(st)

        for g in range(G):
            ch = slice(g * gw, (g + 1) * gw)
            s_prev = st[g]
            sp_ref[0, g] = s_prev
            y, s_new = _ssd_chunk(x_ref[:, ch], z_ref[:, ch], x_ref[:, di + g * n:di + (g + 1) * n],
                                  x_ref[:, di + (G + g) * n:di + (G + g + 1) * n], dt_ref[:, ch], s_prev,
                                  al_ref[0:1, ch], ds_ref[0:1, ch], gn_ref[:, ch], tri_ref[...], cm_ref[...],
                                  dm_ref[...], bd_ref[...])
            y_ref[:, ch] = y.astype(y_ref.dtype)
            st[g] = s_new

    row = lambda w: pl.BlockSpec((L, w), lambda c: (c, 0))
    return _pcall(
        body, carry=carry, name="ssd_fwd", grid=(nc,),
        in_specs=[row(cdim), row(di), row(di), _full(alog_e.shape), _full(dskip_e.shape), _full(gn.shape)]
        + [_full(m.shape) for m in masks],
        out_specs=[row(di), pl.BlockSpec((1, G, n, gw), lambda c: (c, 0, 0, 0))],
        out_shape=[jax.ShapeDtypeStruct((t, di), BF16), jax.ShapeDtypeStruct((nc, G, n, gw), F32)],
        scratch_shapes=[pltpu.VMEM((G, n, gw), F32)],
        compiler_params=_cparams(("arbitrary",)),
    )(xs, proj, dte, alog_e, dskip_e, gn, *masks)


def ssd_bwd(xs, proj, dte, alog_e, dskip_e, gn, sprev, dy, dproj_shape, carry=None):
    t, di = dte.shape
    gw, n = _ssd_dims(di)
    L, G = CHUNK, SSD_GROUPS
    nc = t // L
    masks = _ssd_masks(gw)
    cdim = xs.shape[1]

    def body(x_ref, z_ref, dt_ref, al_ref, ds_ref, gn_ref, tri_ref, cm_ref, dm_ref, bd_ref, sp_ref, dy_ref,
             dxs_ref, dz_ref, ddt_ref, dal_ref, dds_ref, dgn_ref, dst):
        @pl.when(pl.program_id(0) == 0)
        def _():
            dst[...] = jnp.zeros_like(dst)
            dal_ref[...] = jnp.zeros_like(dal_ref)
            dds_ref[...] = jnp.zeros_like(dds_ref)
            dgn_ref[...] = jnp.zeros_like(dgn_ref)

        f = functools.partial(_ssd_chunk, tri=tri_ref[...], cmask=cm_ref[...], dmask=dm_ref[...], bd=bd_ref[...])
        for g in range(G):
            ch = slice(g * gw, (g + 1) * gw)
            bs = slice(di + g * n, di + (g + 1) * n)
            cs = slice(di + (G + g) * n, di + (G + g + 1) * n)
            _, vjp = jax.vjp(f, x_ref[:, ch], z_ref[:, ch], x_ref[:, bs], x_ref[:, cs], dt_ref[:, ch], sp_ref[0, g],
                             al_ref[0:1, ch], ds_ref[0:1, ch], gn_ref[:, ch])
            dx, dz, db, dc, ddt, ds, dal, dds, dgn = vjp((dy_ref[:, ch], dst[g]))
            dxs_ref[:, ch] = dx
            dxs_ref[:, bs] = db
            dxs_ref[:, cs] = dc
            dz_ref[:, ch] = dz.astype(dz_ref.dtype)
            ddt_ref[:, ch] = ddt
            dst[g] = ds
            dal_ref[:, ch] += dal
            dds_ref[:, ch] += dds
            dgn_ref[:, ch] += dgn

    row = lambda w: pl.BlockSpec((L, w), lambda c: (nc - 1 - c, 0))
    acc = _full((1, di))
    acc_shape = jax.ShapeDtypeStruct((1, di), F32)
    return _pcall(
        body, carry=carry, name="ssd_bwd", grid=(nc,),
        in_specs=[row(cdim), row(di), row(di), _full(alog_e.shape), _full(dskip_e.shape), _full(gn.shape)]
        + [_full(m.shape) for m in masks]
        + [pl.BlockSpec((1, G, n, gw), lambda c: (nc - 1 - c, 0, 0, 0)), row(di)],
        out_specs=[row(cdim), row(di), row(di), acc, acc, acc],
        out_shape=[jax.ShapeDtypeStruct((t, cdim), F32), jax.ShapeDtypeStruct(dproj_shape, BF16),
                   jax.ShapeDtypeStruct((t, di), F32), acc_shape, acc_shape, acc_shape],
        scratch_shapes=[pltpu.VMEM((G, n, gw), F32)],
        compiler_params=_cparams(("arbitrary",)),
    )(xs, proj, dte, alog_e, dskip_e, gn, *masks, sprev, dy)


def _rows2d(a):
    return a.reshape(-1, a.shape[-1])


def _row_tile(rows, cols):
    cap = max(SUBLANE, (1 << 19) // max(cols, 1))
    step = 2 * SUBLANE
    for c in range(min(cap, rows) // step * step, 0, -step):
        if rows % c == 0:
            return c
    return rows


def chip_sum(g, r, core, *, name):
    shape = r.shape
    cols = shape[-1]
    g4 = g.reshape(4, 2, -1, cols)
    r3 = r.reshape(4, -1, cols)
    rows = r3.shape[1]
    tr = _row_tile(rows, cols)

    def body(c_ref, g_ref, r_ref, o_ref):
        o_ref[...] = (g_ref[...].astype(F32) + r_ref[...].astype(F32)).astype(o_ref.dtype)

    out = pl.pallas_call(
        body, name=name,
        grid_spec=pltpu.PrefetchScalarGridSpec(
            num_scalar_prefetch=1, grid=(4, rows // tr),
            in_specs=[pl.BlockSpec((None, None, tr, cols), lambda j, i, c: (j, c[0], i, 0)),
                      pl.BlockSpec((None, tr, cols), lambda j, i, c: (j, i, 0))],
            out_specs=pl.BlockSpec((None, tr, cols), lambda j, i, c: (j, i, 0))),
        out_shape=jax.ShapeDtypeStruct(r3.shape, BF16), compiler_params=_cparams(("parallel", "parallel")),
    )(core.reshape(1).astype(jnp.int32), g4, r3)
    return out.reshape(shape)


def mesh_sum(p, r, chip, core, *, name):
    shape = p.shape[1:]
    cols = shape[-1]
    p3 = p.reshape(4, -1, cols)
    r3 = r.reshape(3, -1, cols)
    rows = p3.shape[1]
    tr = _row_tile(rows, 2 * cols)

    def body(c_ref, p_ref, r_ref, o_ref):
        o_ref[...] = ((p_ref[...].astype(F32) + r_ref[0].astype(F32)) + r_ref[1].astype(F32)) + r_ref[2].astype(F32)

    out = pl.pallas_call(
        body, name=name,
        grid_spec=pltpu.PrefetchScalarGridSpec(
            num_scalar_prefetch=1, grid=(rows // tr,),
            in_specs=[pl.BlockSpec((None, tr, cols), lambda i, c: (c[0], i, 0)),
                      pl.BlockSpec((3, tr, cols), lambda i, c: (0, i, 0))],
            out_specs=pl.BlockSpec((None, tr, cols), lambda i, c: (c[1], i, 0))),
        out_shape=jax.ShapeDtypeStruct((2, rows, cols), F32), compiler_params=_cparams(("parallel",)),
    )(jnp.stack([chip, core]).astype(jnp.int32), p3, r3)
    return out.reshape((2,) + shape)


def sum_leading(x, *, name):
    k, r, c = x.shape
    tr = _row_tile(r, c * k)
    def body(x_ref, o_ref):
        acc = x_ref[0]
        for i in range(1, k):
            acc = acc + x_ref[i]
        o_ref[...] = acc

    return pl.pallas_call(
        body, name=name, grid=(r // tr,), in_specs=[pl.BlockSpec((k, tr, c), lambda i: (0, i, 0))],
        out_specs=pl.BlockSpec((tr, c), lambda i: (i, 0)),
        out_shape=jax.ShapeDtypeStruct((r, c), F32), compiler_params=_cparams(("parallel",)),
    )(x)


def adamw(w, gs, m, v, *, name):
    shape = w.shape
    w3, m3, v3 = (a.reshape((-1,) + a.shape[-2:]) for a in (w, m, v))
    nl, r, c = w3.shape
    assert len(gs) == nl
    tr = _row_tile(r, 2 * c)
    tc = c
    if tr == r and r * c > (1 << 19):
        tc = next(t for t in (1024, 512, 256, 128) if c % t == 0 and r * t <= (1 << 19))
    c1 = 1.0 - ADAM_B1 ** ADAM_STEP
    c2 = 1.0 - ADAM_B2 ** ADAM_STEP

    def body(w_ref, g_ref, m_ref, v_ref, *rest):
        go_ref, d_ref, mo_ref, vo_ref = rest[-4:]
        gv = g_ref[...]
        mn = ADAM_B1 * m_ref[...] + (1.0 - ADAM_B1) * gv
        vn = ADAM_B2 * v_ref[...] + (1.0 - ADAM_B2) * (gv * gv)
        go_ref[...] = gv
        d_ref[...] = -ADAM_LR * ((mn / c1) / (jnp.sqrt(vn / c2) + ADAM_EPS) + ADAM_WD * w_ref[...])
        mo_ref[...] = mn
        vo_ref[...] = vn

    outs = None
    for l, g in enumerate(gs):
        layer = pl.BlockSpec((None, tr, tc), lambda i, j, l=l: (l, i, j))
        prev = [] if outs is None else list(outs)
        outs = pl.pallas_call(
            functools.partial(body), name=f"{name}_{l}", grid=(r // tr, c // tc),
            in_specs=[layer, pl.BlockSpec((tr, tc), lambda i, j: (i, j)), layer, layer] + [ANY] * len(prev),
            out_specs=[layer] * 4, out_shape=[jax.ShapeDtypeStruct((nl, r, c), F32)] * 4,
            input_output_aliases={4 + k: k for k in range(len(prev))},
            compiler_params=_cparams(("parallel", "parallel")),
        )(w3, g.reshape(r, c), m3, v3, *prev)
    return tuple(o.reshape(shape) for o in outs)


ANY = pl.BlockSpec(memory_space=pl.ANY)


def _place():
    x, y, c = lax.axis_index("x"), lax.axis_index("y"), lax.axis_index("c")
    chips = [(1 - x, y), (x, 1 - y), (1 - x, 1 - y)]
    return x, y, c, chips


def gather8(block):
    m, n = block.shape

    def body(x_ref, out_ref, send_sems, recv_sems, local_sem):
        x, y, c, chips = _place()
        me, sibling = (x, y, c), (x, y, 1 - c)

        def rows(px, py, pc):
            return out_ref.at[4 * px + 2 * py + pc]

        def copy(k, blk, to, src=None):
            return pltpu.make_async_remote_copy(
                src_ref=rows(*blk) if src is None else src, dst_ref=rows(*blk), send_sem=send_sems.at[k],
                recv_sem=recv_sems.at[k], device_id=to, device_id_type=MESH)

        mine = pltpu.make_async_copy(x_ref, rows(*me), local_sem)
        mine.start()
        first = [copy(0, me, sibling, src=x_ref)]
        first += [copy(1 + j, me, (*chip, c), src=x_ref) for j, chip in enumerate(chips)]
        for cp in first:
            cp.start()
        passed = [copy(4 + j, (*chip, c), sibling) for j, chip in enumerate(chips)]
        for j, chip in enumerate(chips):
            copy(1 + j, (*chip, c), me).wait_recv()
            passed[j].start()
        copy(0, sibling, me).wait_recv()
        for j, chip in enumerate(chips):
            copy(4 + j, (*chip, 1 - c), me).wait_recv()
        for cp in first + passed:
            cp.wait_send()
        mine.wait()

    return pl.pallas_call(
        body, name="gather8",
        out_shape=jax.ShapeDtypeStruct((8, m, n), block.dtype),
        in_specs=[pl.BlockSpec(memory_space=pltpu.VMEM)],
        out_specs=pl.BlockSpec(memory_space=pltpu.VMEM),
        scratch_shapes=[pltpu.SemaphoreType.DMA((7,)), pltpu.SemaphoreType.DMA((7,)), pltpu.SemaphoreType.DMA],
        compiler_params=pltpu.CompilerParams(vmem_limit_bytes=VMEM_LIMIT),
    )(block)


def gather_weights(shards):
    n = len(shards)

    def copy(ins, outs, send, recv, base, a, k, chip_idx, half, to, src=None):
        dst = outs[a].at[chip_idx, half]
        return pltpu.make_async_remote_copy(
            src_ref=dst if src is None else src, dst_ref=dst, send_sem=send.at[base + 6 * a + k],
            recv_sem=recv.at[base + 6 * a + k], device_id=to, device_id_type=MESH)

    def first(ins, outs, send, recv, base):
        x, y, c, chips = _place()
        return [copy(ins, outs, send, recv, base, a, j, 2 * x + y, c, (*chip, c), src=ins[a].at[c])
                for a in range(n) for j, chip in enumerate(chips)]

    def start(ins, outs, send, recv, base):
        for cp in first(ins, outs, send, recv, base):
            cp.start()

    def finish(ins, outs, send, recv, base):
        x, y, c, chips = _place()
        sibling = (x, y, 1 - c)
        passed = []
        for a in range(n):
            for j, (cx, cy) in enumerate(chips):
                copy(ins, outs, send, recv, base, a, j, 2 * cx + cy, c, (cx, cy, c)).wait_recv()
                fw = copy(ins, outs, send, recv, base, a, 3 + j, 2 * cx + cy, c, sibling)
                fw.start()
                passed.append(fw)
        for a in range(n):
            for j, (cx, cy) in enumerate(chips):
                copy(ins, outs, send, recv, base, a, 3 + j, 2 * cx + cy, 1 - c, sibling).wait_recv()
        for cp in first(ins, outs, send, recv, base) + passed:
            cp.wait_send()

    return Carry(shards, [jax.ShapeDtypeStruct((4,) + s.shape, s.dtype) for s in shards], 6 * n, start, finish)


def exchange_halves(grads):
    n = len(grads)

    def copies(ins, outs, send, recv, base):
        x, y, c, _ = _place()
        return [pltpu.make_async_remote_copy(
            src_ref=ins[a].at[j, 1 - c], dst_ref=outs[a].at[j], send_sem=send.at[base + 4 * a + j],
            recv_sem=recv.at[base + 4 * a + j], device_id=(x, y, 1 - c), device_id_type=MESH)
            for a in range(n) for j in range(4)]

    def start(*args):
        for cp in copies(*args):
            cp.start()

    def finish(*args):
        for cp in copies(*args):
            cp.wait()

    return Carry(grads, [jax.ShapeDtypeStruct((4,) + g.shape[2:], g.dtype) for g in grads], 4 * n, start, finish)


def scatter_chips(parts, rows=None, into=None):
    n = len(parts)
    sl = (lambda r: r) if rows is None else (lambda r: r.at[pl.ds(rows[0], rows[1])])

    def copies(ins, outs, send, recv, base):
        x, y, c, chips = _place()
        return [pltpu.make_async_remote_copy(
            src_ref=sl(ins[a].at[2 * cx + cy]), dst_ref=sl(outs[a].at[j]), send_sem=send.at[base + 3 * a + j],
            recv_sem=recv.at[base + 3 * a + j], device_id=(cx, cy, c), device_id_type=MESH)
            for a in range(n) for j, (cx, cy) in enumerate(chips)]

    def start(*args):
        for cp in copies(*args):
            cp.start()

    def finish(*args):
        for cp in copies(*args):
            cp.wait()

    shapes = [jax.ShapeDtypeStruct((3,) + p.shape[1:], p.dtype) for p in parts]
    if into is None:
        return Carry(parts, shapes, 3 * n, start, finish)
    return Carry(list(parts) + list(into), shapes, 3 * n, start, finish, aliases={n + a: a for a in range(n)})


def join_halves(bufs):
    n = len(bufs)

    def copy(outs, send, recv, base, a, half):
        x, y, c, _ = _place()
        return pltpu.make_async_remote_copy(
            src_ref=outs[a].at[c], dst_ref=outs[a].at[c if half is None else half], send_sem=send.at[base + a],
            recv_sem=recv.at[base + a], device_id=(x, y, 1 - c), device_id_type=MESH)

    def start(ins, outs, send, recv, base):
        for a in range(n):
            copy(outs, send, recv, base, a, None).start()

    def finish(ins, outs, send, recv, base):
        c = lax.axis_index("c")
        for a in range(n):
            copy(outs, send, recv, base, a, 1 - c).wait_recv()
        for a in range(n):
            copy(outs, send, recv, base, a, None).wait_send()

    return Carry(bufs, [jax.ShapeDtypeStruct(h.shape, h.dtype) for h in bufs], n, start, finish,
                 aliases={a: a for a in range(n)})


def run_comm(carry, *, name):
    k_in, k_out = len(carry.arrays), len(carry.out_shape)

    def body(*refs):
        ins, outs = refs[:k_in], refs[k_in:k_in + k_out]
        send, recv = refs[-2:]
        carry.start(ins, outs, send, recv, 0)
        carry.finish(ins, outs, send, recv, 0)

    return pl.pallas_call(
        body, name=name, out_shape=carry.out_shape, in_specs=[ANY] * k_in, out_specs=[ANY] * k_out,
        scratch_shapes=[pltpu.SemaphoreType.DMA((carry.n_sems,))] * 2, input_output_aliases=carry.aliases,
    )(*carry.arrays)


INPUTS = ['x'] + WEIGHTS + ['loss_target'] + ['m_' + n for n in WEIGHTS] + ['v_' + n for n in WEIGHTS]


def _round_up(n, m):
    return -(-n // m) * m


def _pack(arrs):
    flat = jnp.concatenate([a.reshape(-1) for a in arrs])
    n = _round_up(flat.shape[0], 512 * LANE)
    return jnp.pad(flat, (0, n - flat.shape[0])).reshape(-1, LANE)


def _unpack(block, shapes):
    flat = block.reshape(-1)
    out, o = [], 0
    for s in shapes:
        n = math.prod(s)
        out.append(flat[o:o + n].reshape(s))
        o += n
    return out


def _cols(g):
    return g.transpose(1, 0, 2).reshape(g.shape[1], -1)


def _uncols(w):
    return w.reshape(w.shape[0], 4, -1).transpose(1, 0, 2)


def _pad_cols(w, total):
    return jnp.pad(w, ((0, 0), (0, total - w.shape[1])))


def kernel(*args):
    a = dict(zip(INPUTS, args))
    x, tgt = a['x'][0], a['loss_target'][0]
    t, d = x.shape
    xi, yi, ci = lax.axis_index("x"), lax.axis_index("y"), lax.axis_index("c")
    chip = 2 * xi + yi
    dk, dv, hk, hv = _gla_dims(d)
    lw = d // 2
    di = 2 * d
    nh = di // SSD_HEAD_DIM
    gn_w = SSD_GROUPS * SSD_STATE
    conv_dim = di + 2 * gn_w
    rank = GLA_GATE_RANK
    wq = 2 * dk + 2 * dv
    gla_w = _round_up(wq + LANE, 2 * lw)
    ev_tot = gla_w + 2 * lw
    od_used = di + conv_dim + nh
    od_tot = _round_up(di + conv_dim + _round_up(nh, LANE), 512)
    glr_col, xcol, dtcol = wq // LANE, gla_w // lw, (di + conv_dim) // LANE
    assert ev_tot % 512 == 0 and nh <= LANE

    def halves(w):
        w = w.astype(BF16)
        return w.reshape((2, w.shape[0] // 2) + w.shape[1:])

    own = {'ev_w_in': halves(a['ev_w_in'][0]), 'ev_w_out': halves(a['ev_w_out'][0]),
           'od_w_in_a': halves(a['od_w_in'][0][:d // 2]), 'od_w_in_b': halves(a['od_w_in'][0][d // 2:]),
           'od_w_out': halves(a['od_w_out'][0])}
    for l in range(2):
        own[f'gate{l}'], own[f'up{l}'] = halves(a['ffn_w_gate'][l]), halves(a['ffn_w_up'][l])
        own[f'down{l}'] = halves(a['ffn_w_down'][l])

    def gather(*units):
        return gather_weights([own[u] for u in units])

    def filled(unit, g):
        g = lax.dynamic_update_index_in_dim(g, own[unit], chip, 0)
        return g.reshape((4, 2 * g.shape[2]) + g.shape[3:])

    g_ev_in, g_ev_out = run_comm(gather('ev_w_in', 'ev_w_out'), name="gather_ev")
    w_ev_in = _cols(filled('ev_w_in', g_ev_in))
    cuts = [dk, 2 * dk, 2 * dk + dv, wq, wq + rank, wq + rank + lw]
    sq, sk, sv, sg, sglr, sxb, sgb = jnp.split(w_ev_in, cuts, axis=1)
    w_ev_in_p = jnp.concatenate([_pad_cols(jnp.concatenate([sq, sk, sv, sg, sglr], axis=1), gla_w), sxb, sgb], axis=1)
    w_ev_out = filled('ev_w_out', g_ev_out).reshape(-1, d)
    w_gate, w_up, w_down = [None, None], [None, None], [None, None]

    sh_names = list(SMALL_SHARDED)
    sh_shapes = [a[n].shape for n in sh_names]
    g8 = gather8(_pack([a[n] for n in sh_names]))
    per_chip = [_unpack(g8[2 * j], sh_shapes) for j in range(4)]
    full = {n: jnp.concatenate([per_chip[j][i] for j in range(4)], axis=SMALL_SHARDED[n])
            for i, n in enumerate(sh_names)}

    wg_p = jnp.zeros((LANE, dk), F32).at[:rank].set(full['ev_gla_w_gate'][0])
    bg, wn = a['ev_gla_b_gate'], a['ev_gla_w_onorm']
    lru_p = [full['ev_lru_conv_w'][0], a['ev_lru_conv_b'], a['ev_lru_w_a'][0], a['ev_lru_b_a'], a['ev_lru_w_i'][0],
             a['ev_lru_b_i'], a['ev_lru_lam']]
    od_cw, od_cb, od_gn = full['od_conv_w'][0], full['od_conv_b'], full['od_gnorm']
    heads = jnp.arange(LANE)[:, None]
    e_mat = ((jnp.arange(di)[None, :] // SSD_HEAD_DIM == heads) & (heads < nh)).astype(BF16)
    row8 = lambda p: jnp.zeros((SUBLANE, LANE), F32).at[0, :nh].set(p[0])
    dt_bias_p = jnp.zeros((1, LANE), F32).at[0, :nh].set(a['od_dt_bias'][0])
    alog_e = head_expand(row8(a['od_a_log']), e_mat, name="expand_a_log")
    dskip_e = head_expand(row8(a['od_d_skip']), e_mat, name="expand_d_skip")

    h0 = rms_fwd(x, a['ev_norm'], name="rms_ev")
    proj, (g,) = matmul(h0, w_ev_in_p, name="ev_in", carry=gather('gate0'))
    w_gate[0] = filled('gate0', g)
    (o_gla, sp_gla), (g,) = gla_fwd(proj, glr_col, wg_p, bg, wn, dv, carry=gather('up0'))
    w_up[0] = filled('up0', g)
    o_lru, hin = lru_fwd(proj, xcol, lw, *lru_p)
    x1 = matmul(o_gla, w_ev_out[:dv], add=x, name="ev_out_a")
    x1 = matmul(o_lru, w_ev_out[dv:], add=x1, name="ev_out_b")

    h1 = rms_fwd(x1, a['ffn_norm'][0:1], name="rms_ffn0")
    gate0, (g,) = matmul(h1, w_gate[0], name="ffn0_gate", carry=gather('down0'))
    w_down[0] = filled('down0', g).reshape(-1, d)
    swi = dict(epi=_swi_fwd_epi, epi_out=(F32, BF16))
    (up0, act0), (g_a,) = matmul(h1, w_up[0], name="ffn0_up", epi_in=(gate0,), carry=gather('od_w_in_a'), **swi)
    x2, (g_b,) = matmul(act0, w_down[0], add=x1, name="ffn0_down", carry=gather('od_w_in_b'))
    w_od_in = jnp.concatenate([filled('od_w_in_a', g_a), filled('od_w_in_b', g_b)], axis=1)
    w_od_in_p = _pad_cols(_cols(w_od_in), od_tot)

    h2 = rms_fwd(x2, full['od_norm'], name="rms_od")
    proj2, (g, g1) = matmul(h2, w_od_in_p, name="od_in", carry=gather('od_w_out', 'gate1'))
    w_od_out, w_gate[1] = filled('od_w_out', g).reshape(-1, d), filled('gate1', g1)
    xs, (g,) = conv_silu_fwd(proj2, di, conv_dim, od_cw, od_cb, carry=gather('up1'))
    w_up[1] = filled('up1', g)
    dte = dt_fwd(proj2, dtcol, dt_bias_p, e_mat)
    (y_ssd, sp_ssd), (g,) = ssd_fwd(xs, proj2, dte, alog_e, dskip_e, od_gn, carry=gather('down1'))
    w_down[1] = filled('down1', g).reshape(-1, d)
    x3 = matmul(y_ssd, w_od_out, add=x2, name="od_out")
    h3 = rms_fwd(x3, a['ffn_norm'][1:2], name="rms_ffn1")
    gate1 = matmul(h3, w_gate[1], name="ffn1_gate")
    up1, act1 = matmul(h3, w_up[1], name="ffn1_up", epi_in=(gate1,), **swi)
    x4 = matmul(act1, w_down[1], add=x3, name="ffn1_down")
    loss_p, dx4, dx4b, d_final = loss_head(x4, a['final_norm'][None], tgt, name="loss_head")

    grads, from_sib, part, from_chips = {}, {}, {}, {}

    def rows4(dw):
        return dw.reshape((4, 2, dw.shape[0] // 8) + dw.shape[1:])

    def cols4(dw):
        return dw.reshape((4, 2, dw.shape[1] // 2) + dw.shape[2:])

    def exchange(*units):
        return exchange_halves([grads[u] for u in units])

    def scatter(*units):
        return scatter_chips([part[u] for u in units])

    def sum_chip(u):
        part[u] = chip_sum(grads[u], from_sib[u], ci, name=f"chip_sum_{u}")

    def ffn_bwd(dxo, dxob, xin, h, gate, up, act, l, first_carry, first_units):
        dn, gt, up_ = f'down{l}', f'gate{l}', f'up{l}'
        dgu = matmul(dxob, w_down[l], tb=True, name=f"ffn{l}_d_act", carry=first_carry, epi=_swi_bwd_epi,
                     epi_in=(gate, up), epi_out=(BF16, BF16))
        (dg, du), got = dgu if first_units else (dgu, ())
        for u, r in zip(first_units, got):
            from_sib[u] = r
            sum_chip(u)
        piece = [part[u].shape[1] // 2 for u in first_units]
        d_down = matmul(act, dxob, ta=True, out_dtype=BF16, name=f"ffn{l}_dw_down",
                        carry=scatter_chips([part[u] for u in first_units], rows=(0, piece[0])) if first_units else None)
        if first_units:
            d_down, first_rb = d_down
        grads[dn] = rows4(d_down)
        dh, (from_sib[dn],) = matmul(dg, w_gate[l], tb=True, name=f"ffn{l}_dh_gate", carry=exchange(dn))
        sum_chip(dn)
        dh = matmul(du, w_up[l], tb=True, add=dh, name=f"ffn{l}_dh_up",
                    carry=scatter_chips([part[u] for u in first_units], rows=(piece[0], piece[0]), into=first_rb)
                    if first_units else None)
        if first_units:
            dh, (from_chips[first_units[0]],) = dh
        d_gate, (from_chips[dn],) = matmul(h, dg, ta=True, out_dtype=BF16, out_shards=True, name=f"ffn{l}_dw_gate",
                                           carry=scatter(dn))
        grads[gt] = cols4(d_gate)
        d_up, (from_sib[gt],) = matmul(h, du, ta=True, out_dtype=BF16, out_shards=True, name=f"ffn{l}_dw_up",
                                       carry=exchange(gt))
        grads[up_] = cols4(d_up)
        return rms_bwd(xin, a['ffn_norm'][l:l + 1], dh, dxo, name=f"rms_ffn{l}_bwd")

    dx3, dx3b, d_fn1 = ffn_bwd(dx4, dx4b, x3, h3, gate1, up1, act1, 1, None, ())
    dy, (from_sib['up1'],) = matmul(dx3b, w_od_out, tb=True, name="od_out_dy", carry=exchange('up1'))
    sum_chip('gate1')
    sum_chip('up1')
    grads['od_w_out'] = rows4(matmul(y_ssd, dx3b, ta=True, out_dtype=BF16, name="od_out_dw"))
    (dxs, dproj2, ddte, dal, dds, dgn), (from_chips['gate1'], from_chips['up1'], from_sib['od_w_out']) = ssd_bwd(
        xs, proj2, dte, alog_e, dskip_e, od_gn, sp_ssd, dy, proj2.shape,
        carry=merge_carries(scatter('gate1', 'up1'), exchange('od_w_out')))
    sum_chip('od_w_out')
    (dproj2, d_od_cw, d_od_cb), (from_chips['od_w_out'],) = conv_silu_bwd(
        proj2, di, conv_dim, od_cw, od_cb, dxs, dproj2, carry=scatter('od_w_out'))
    dproj2, d_dt_bias = dt_bwd(proj2, dtcol, dt_bias_p, e_mat, ddte, dproj2)
    dh2 = matmul(dproj2, w_od_in_p, tb=True, name="od_in_dh")
    d_od_in = matmul(h2, dproj2, ta=True, out_dtype=BF16, name="od_in_dw")[:, :od_used]
    grads['od_w_in'] = cols4(_uncols(d_od_in))
    dx2, dx2b, d_od_norm = rms_bwd(x2, full['od_norm'], dh2, dx3, name="rms_od_bwd")
    to8 = lambda acc: jnp.zeros((SUBLANE, di), F32).at[0].set(acc.reshape(-1))
    d_a_log = head_expand(to8(dal), e_mat, transpose=True, name="reduce_a_log")[0:1, :nh]
    d_d_skip = head_expand(to8(dds), e_mat, transpose=True, name="reduce_d_skip")[0:1, :nh]

    dx1, dx1b, d_fn0 = ffn_bwd(dx2, dx2b, x1, h1, gate0, up0, act0, 0, exchange('od_w_in'), ('od_w_in',))
    dmix, (from_sib['up0'],) = matmul(dx1b, w_ev_out, tb=True, name="ev_out_dmix", carry=exchange('up0'))
    sum_chip('gate0')
    sum_chip('up0')
    grads['ev_w_out'] = rows4(jnp.concatenate([matmul(o_gla, dx1b, ta=True, out_dtype=BF16, name="ev_out_dw_a"),
                                               matmul(o_lru, dx1b, ta=True, out_dtype=BF16, name="ev_out_dw_b")], axis=0))
    (dproj, d_wg, d_bg, d_wn), (from_sib['ev_w_out'],) = gla_bwd(
        proj, glr_col, wg_p, bg, wn, sp_gla, dmix, dv, gla_w, carry=exchange('ev_w_out'))
    sum_chip('ev_w_out')
    (dproj, *d_lru), (from_chips['ev_w_out'], from_chips['gate0']) = lru_bwd(
        proj, xcol, lw, *lru_p, hin, dmix, 1, dproj, carry=scatter('ev_w_out', 'gate0'))
    d_ev_in_p, (from_chips['up0'],) = matmul(h0, dproj, ta=True, out_dtype=BF16, name="ev_in_dw", carry=scatter('up0'))
    d_ev_in = jnp.concatenate([d_ev_in_p[:, :wq + rank], d_ev_in_p[:, gla_w:]], axis=1)
    grads['ev_w_in'] = cols4(_uncols(d_ev_in))
    (from_sib['ev_w_in'],) = run_comm(exchange('ev_w_in'), name="exchange_ev_in")
    sum_chip('ev_w_in')
    dh0, (from_chips['ev_w_in'],) = matmul(dproj, w_ev_in_p, tb=True, name="ev_in_dh", carry=scatter('ev_w_in'))
    dx0, _, d_ev_norm = rms_bwd(x, a['ev_norm'], dh0, dx1, name="rms_ev_bwd")

    units = list(grads)
    half = [mesh_sum(part[u], from_chips[u], chip, ci, name=f"mesh_sum_{u}") for u in units]
    done = dict(zip(units, run_comm(join_halves(half), name="join_halves")))
    layers = {n: [done[n]] for n in ('ev_w_in', 'ev_w_out', 'od_w_in', 'od_w_out')}
    layers.update({f'ffn_w_{u}': [done[f'{u}0'], done[f'{u}1']] for u in ('gate', 'up', 'down')})
    grad, delta, new_m, new_v = {}, {}, {}, {}
    for n in BIG:
        flip = a[n].shape[-1] % LANE != 0
        tr_ = (lambda t: jnp.swapaxes(t, -1, -2)) if flip else (lambda t: t)
        gs = [tr_(g.reshape(a[n].shape[-2:])) for g in layers[n]]
        outs = adamw(tr_(a[n]), gs, tr_(a['m_' + n]), tr_(a['v_' + n]), name=f"adamw_{n}")
        grad[n], delta[n], new_m[n], new_v[n] = (tr_(o) for o in outs)

    small_g = {
        'ev_norm': d_ev_norm, 'ev_gla_w_gate': d_wg[:rank][None], 'ev_gla_b_gate': d_bg, 'ev_gla_w_onorm': d_wn,
        'ev_lru_conv_w': d_lru[0][None], 'ev_lru_conv_b': d_lru[1], 'ev_lru_w_a': d_lru[2][None],
        'ev_lru_b_a': d_lru[3], 'ev_lru_w_i': d_lru[4][None], 'ev_lru_b_i': d_lru[5], 'ev_lru_lam': d_lru[6],
        'od_norm': d_od_norm, 'od_conv_w': d_od_cw[None], 'od_conv_b': d_od_cb, 'od_dt_bias': d_dt_bias[:, :nh],
        'od_a_log': d_a_log, 'od_d_skip': d_d_skip, 'od_gnorm': dgn.reshape(1, di),
        'ffn_norm': jnp.concatenate([d_fn0, d_fn1], axis=0), 'final_norm': d_final[0],
    }
    full_shapes = [small_g[n].shape for n in SMALL]
    summed = sum_leading(gather8(_pack([small_g[n] for n in SMALL])), name="sum_devices")
    for n, g in zip(SMALL, _unpack(summed, full_shapes)):
        if n in SMALL_SHARDED:
            ax = SMALL_SHARDED[n]
            sz = a[n].shape[ax]
            g = lax.dynamic_slice_in_dim(g, chip * sz, sz, axis=ax)
        grad[n] = g

    shapes = [a[n].shape for n in SMALL]
    packed = [_pack([src[n] if pre is None else a[pre + n] for n in SMALL])
              for src, pre in ((a, None), (grad, None), (None, 'm_'), (None, 'v_'))]
    small_out = adamw(packed[0], [packed[1]], packed[2], packed[3], name="adamw_small")
    for outd, blk in zip((delta, new_m, new_v), small_out[1:]):
        outd.update(zip(SMALL, _unpack(blk, shapes)))

    loss = lax.psum(loss_p[0, 0], ("x", "y", "c"))
    return (loss, dx0[None], *[grad[n] for n in WEIGHTS], *[delta[n] for n in WEIGHTS],
            *[new_m[n] for n in WEIGHTS], *[new_v[n] for n in WEIGHTS])
```

```python
import functools
import math

import jax
import jax.numpy as jnp
from jax import lax
from jax.experimental import pallas as pl
from jax.experimental.pallas import tpu as pltpu

F32 = jnp.float32
BF16 = jnp.bfloat16
MXU_DTYPE = jnp.bfloat16

NORM_EPS = 1e-6
CONV_WIDTH = 4
GLA_HEADS = 4
GLA_GATE_RANK = 16
GLA_GATE_NORM = 16.0
CHUNK = 64
LRU_BLOCK = 128
LRU_C = 8.0
SSD_HEAD_DIM = 64
SSD_GROUPS = 8
SSD_STATE = 128
ADAM_LR, ADAM_B1, ADAM_B2, ADAM_EPS, ADAM_WD, ADAM_STEP = 0.001, 0.9, 0.999, 1e-08, 0.01, 10

LANE = 128
SUBLANE = 8
VMEM_LIMIT = 48 * 1024 * 1024
MAX_TK = 2816
MATMUL_VMEM_BUDGET = 45 * 1024 * 1024
MESH = pl.DeviceIdType.MESH

WEIGHTS = ['ev_norm', 'ev_w_in', 'ev_gla_w_gate', 'ev_gla_b_gate', 'ev_gla_w_onorm', 'ev_lru_conv_w', 'ev_lru_conv_b',
           'ev_lru_w_a', 'ev_lru_b_a', 'ev_lru_w_i', 'ev_lru_b_i', 'ev_lru_lam', 'ev_w_out', 'od_norm', 'od_w_in',
           'od_conv_w', 'od_conv_b', 'od_dt_bias', 'od_a_log', 'od_d_skip', 'od_gnorm', 'od_w_out', 'ffn_norm',
           'ffn_w_gate', 'ffn_w_up', 'ffn_w_down', 'final_norm']
BIG = ['ev_w_in', 'ev_w_out', 'od_w_in', 'od_w_out', 'ffn_w_gate', 'ffn_w_up', 'ffn_w_down']
SMALL_SHARDED = {'ev_gla_w_gate': 2, 'ev_lru_conv_w': 2, 'od_norm': 1, 'od_conv_w': 2, 'od_conv_b': 1, 'od_gnorm': 1}
SMALL = [n for n in WEIGHTS if n not in BIG]


def _cparams(sem=None, **kw):
    return pltpu.CompilerParams(dimension_semantics=sem, vmem_limit_bytes=VMEM_LIMIT, **kw)


def _full(shape):
    n = len(shape)
    return pl.BlockSpec(shape, lambda *_: (0,) * n)


ANY = pl.BlockSpec(memory_space=pl.ANY)


class Carry:
    def __init__(self, arrays, out_shape, n_sems, start, finish, aliases=None):
        self.arrays, self.out_shape, self.n_sems = list(arrays), list(out_shape), n_sems
        self.start, self.finish, self.aliases = start, finish, dict(aliases or {})


def merge_carries(*cs):
    cs = [c for c in cs if c is not None]
    if not cs:
        return None
    arrays = [a for c in cs for a in c.arrays]
    out_shape = [s for c in cs for s in c.out_shape]
    offs, i0, o0, s0 = [], 0, 0, 0
    aliases = {}
    for c in cs:
        offs.append((i0, o0, s0))
        aliases.update({i0 + i: o0 + o for i, o in c.aliases.items()})
        i0, o0, s0 = i0 + len(c.arrays), o0 + len(c.out_shape), s0 + c.n_sems

    def both(which):
        def run(ins, outs, send, recv, base):
            for c, (i, o, s) in zip(cs, offs):
                getattr(c, which)(ins[i:i + len(c.arrays)], outs[o:o + len(c.out_shape)], send, recv, base + s)
        return run

    return Carry(arrays, out_shape, s0, both("start"), both("finish"), aliases)


def _pcall(body, *, name, grid, in_specs, out_specs, out_shape, scratch_shapes=(), compiler_params, carry=None,
           input_output_aliases=None):
    aliases = dict(input_output_aliases or {})
    if carry is None:
        return pl.pallas_call(body, name=name, grid=grid, in_specs=in_specs, out_specs=out_specs, out_shape=out_shape,
                              scratch_shapes=list(scratch_shapes), compiler_params=compiler_params,
                              input_output_aliases=aliases)
    single = not isinstance(out_specs, (list, tuple))
    specs_o = [out_specs] if single else list(out_specs)
    shapes_o = [out_shape] if single else list(out_shape)
    n_in, n_out, k_in, k_out, n_scr = len(in_specs), len(specs_o), len(carry.arrays), len(carry.out_shape), len(scratch_shapes)

    def wrapped(*refs):
        ins, cins = refs[:n_in], refs[n_in:n_in + k_in]
        o0 = n_in + k_in
        outs, couts = refs[o0:o0 + n_out], refs[o0 + n_out:o0 + n_out + k_out]
        scr = refs[o0 + n_out + k_out:o0 + n_out + k_out + n_scr]
        send, recv = refs[-2:]
        ids = [pl.program_id(ax) for ax in range(len(grid))]
        first = functools.reduce(jnp.logical_and, [i == 0 for i in ids])
        last = functools.reduce(jnp.logical_and, [i == g - 1 for i, g in zip(ids, grid)])

        @pl.when(first)
        def _():
            carry.start(cins, couts, send, recv, 0)

        body(*ins, *outs, *scr)

        @pl.when(last)
        def _():
            carry.finish(cins, couts, send, recv, 0)

    aliases.update({n_in + i: n_out + o for i, o in carry.aliases.items()})
    call = pl.pallas_call(
        wrapped, name=name, grid=grid, in_specs=list(in_specs) + [ANY] * k_in, out_specs=specs_o + [ANY] * k_out,
        out_shape=shapes_o + carry.out_shape,
        scratch_shapes=list(scratch_shapes) + [pltpu.SemaphoreType.DMA((carry.n_sems,))] * 2,
        compiler_params=_cparams(("arbitrary",) * len(grid)), input_output_aliases=aliases)

    def run(*args):
        res = call(*args, *carry.arrays)
        main = res[:n_out]
        return (main[0] if single else list(main)), list(res[n_out:])

    return run


def _pick(dim, cands):
    for c in cands:
        if dim % c == 0:
            return c
    return dim


def _dims(a, ca, cb):
    nb = a.ndim - 2
    return (((ca + nb,), (cb + nb,)), (tuple(range(nb)), tuple(range(nb))))


def _dot(a, b, ca, cb):
    return lax.dot_general(a.astype(MXU_DTYPE), b.astype(MXU_DTYPE), _dims(a, ca, cb), preferred_element_type=F32)


@jax.custom_vjp
def mm(a, b):
    return _dot(a, b, 1, 0)


def _mm_f(a, b):
    return mm(a, b), (a, b)


def _mm_b(res, g):
    a, b = res
    return mm_nt(g, b).astype(a.dtype), mm_tn(a, g).astype(b.dtype)


@jax.custom_vjp
def mm_nt(a, b):
    return _dot(a, b, 1, 1)


def _mm_nt_f(a, b):
    return mm_nt(a, b), (a, b)


def _mm_nt_b(res, g):
    a, b = res
    return mm(g, b).astype(a.dtype), mm_tn(g, a).astype(b.dtype)


@jax.custom_vjp
def mm_tn(a, b):
    return _dot(a, b, 0, 0)


def _mm_tn_f(a, b):
    return mm_tn(a, b), (a, b)


def _mm_tn_b(res, g):
    a, b = res
    return mm_nt(b, g).astype(a.dtype), mm(a, g).astype(b.dtype)


mm.defvjp(_mm_f, _mm_b)
mm_nt.defvjp(_mm_nt_f, _mm_nt_b)
mm_tn.defvjp(_mm_tn_f, _mm_tn_b)


def _split3(a):
    h = a.astype(BF16)
    r = a - h.astype(F32)
    m = r.astype(BF16)
    l = (r - m.astype(F32)).astype(BF16)
    return h, m, l


def _exact_dot(t, a, ca, cb):
    out = None
    for p in _split3(a):
        d = lax.dot_general(t, p, _dims(a, ca, cb), preferred_element_type=F32)
        out = d if out is None else out + d
    return out


@jax.custom_vjp
def sel_l(t, a):
    return _exact_dot(t, a, 1, 0)


def _sel_l_f(t, a):
    return sel_l(t, a), t


def _sel_l_b(t, g):
    return jnp.zeros_like(t), _exact_dot(t, g, 0, 0)


sel_l.defvjp(_sel_l_f, _sel_l_b)


@jax.custom_vjp
def sel_r(a, t):
    out = None
    for p in _split3(a):
        d = lax.dot_general(p, t, (((1,), (0,)), ((), ())), preferred_element_type=F32)
        out = d if out is None else out + d
    return out


def _sel_r_f(a, t):
    return sel_r(a, t), t


def _sel_r_b(t, g):
    out = None
    for p in _split3(g):
        d = lax.dot_general(p, t, (((1,), (1,)), ((), ())), preferred_element_type=F32)
        out = d if out is None else out + d
    return out, jnp.zeros_like(t)


sel_r.defvjp(_sel_r_f, _sel_r_b)


def _sigmoid(x):
    return 1.0 / (1.0 + jnp.exp(-x))


def _silu(x):
    return x * _sigmoid(x)


def _softplus(x):
    return jnp.maximum(x, 0.0) + jnp.log(1.0 + jnp.exp(-jnp.abs(x)))


def _log_sigmoid(x):
    return -_softplus(-x)


def _gelu_tanh(x):
    c = math.sqrt(2.0 / math.pi)
    return 0.5 * x * (1.0 + jnp.tanh(c * (x + 0.044715 * (x * x * x))))


def _rms(x, w):
    return x * lax.rsqrt(jnp.mean(x * x, axis=-1, keepdims=True) + NORM_EPS) * w


def _tri(n, dtype=BF16):
    r = lax.broadcasted_iota(jnp.int32, (n, n), 0)
    c = lax.broadcasted_iota(jnp.int32, (n, n), 1)
    return (c <= r).astype(dtype)


def matmul(a, b, *, ta=False, tb=False, add=None, out_dtype=F32, out_shards=False, carry=None, name,
           epi=None, epi_in=(), epi_out=()):
    m, k = (a.shape[1], a.shape[0]) if ta else a.shape
    b_sh = b.ndim == 3
    if b_sh:
        s, br, bc = b.shape
        k2, n = (s * bc, br) if tb else (br, s * bc)
    else:
        k2, n = (b.shape[1], b.shape[0]) if tb else b.shape
    assert k == k2, (a.shape, b.shape, ta, tb)
    tk_opts = [bc] if b_sh and tb else [k] if k <= MAX_TK else \
        [c for c in range(MAX_TK, LANE - 1, -LANE) if k % c == 0][:1]
    n_add, n_x = int(add is not None), len(epi_in)
    out_dtypes = list(epi_out) if epi is not None else [out_dtype]
    n_o = len(out_dtypes)
    tn_opts = [bc] if b_sh and not tb else [n // 4] if out_shards else \
        [c for c in range(2048, LANE - 1, -LANE) if n % c == 0] or [n]
    tm_opts = [c for c in range(2048, LANE - 1, -LANE) if m % c == 0] or [m]
    sa, sb = a.dtype.itemsize, b.dtype.itemsize
    per_elem = sum(jnp.dtype(dt).itemsize for dt in out_dtypes) + 4 * n_add + sum(x.dtype.itemsize for x in epi_in)
    best = None
    for tk_ in tk_opts:
        for tm_ in tm_opts:
            for tn_ in tn_opts:
                vmem = 2 * (tm_ * tk_ * sa + tk_ * tn_ * sb) + tm_ * tn_ * (4 + 2 * per_elem)
                vmem += tm_ * tn_ * 4
                if vmem > MATMUL_VMEM_BUDGET and (tk_, tm_, tn_) != (tk_opts[-1], tm_opts[-1], tn_opts[-1]):
                    continue
                moved = m * k * sa * (1 if tk_ == k else n // tn_) + k * n * sb * (m // tm_)
                if best is None or (moved, -tk_, -tm_ * tn_) < best[0]:
                    best = ((moved, -tk_, -tm_ * tn_), tm_, tn_, tk_)
    _, tm, tn, tk = best
    nk = k // tk

    def body(*refs):
        a_ref, b_ref = refs[:2]
        x_refs = refs[2 + n_add:2 + n_add + n_x]
        o_refs = refs[2 + n_add + n_x:2 + n_add + n_x + n_o]
        acc = refs[-1]
        kk = pl.program_id(2)

        @pl.when(kk == 0)
        def _():
            acc[...] = jnp.zeros_like(acc)

        acc[...] += _dot(a_ref[...], b_ref[...], 0 if ta else 1, 1 if tb else 0)

        @pl.when(kk == nk - 1)
        def _():
            r = acc[...]
            if add is not None:
                r = r + refs[2][...].astype(F32)
            vals = (r,) if epi is None else epi(r, *[x[...] for x in x_refs])
            for o_ref, v in zip(o_refs, vals):
                o_ref[...] = v.astype(o_ref.dtype)

    a_spec = pl.BlockSpec((tk, tm), lambda i, j, kk: (kk, i)) if ta else pl.BlockSpec((tm, tk), lambda i, j, kk: (i, kk))
    if b_sh and tb:
        b_spec = pl.BlockSpec((None, tn, tk), lambda i, j, kk: (kk, j, 0))
    elif b_sh:
        b_spec = pl.BlockSpec((None, tk, tn), lambda i, j, kk: (j, kk, 0))
    elif tb:
        b_spec = pl.BlockSpec((tn, tk), lambda i, j, kk: (j, kk))
    else:
        b_spec = pl.BlockSpec((tk, tn), lambda i, j, kk: (kk, j))
    in_specs, args = [a_spec, b_spec], [a, b]
    tile = pl.BlockSpec((tm, tn), lambda i, j, kk: (i, j))
    for extra in ([add] if add is not None else []) + list(epi_in):
        in_specs.append(tile)
        args.append(extra)
    if out_shards:
        out_spec = pl.BlockSpec((None, tm, tn), lambda i, j, kk: (j, i, 0))
        out_shape = jax.ShapeDtypeStruct((4, m, tn), out_dtype)
    elif epi is not None:
        out_spec = [tile] * n_o
        out_shape = [jax.ShapeDtypeStruct((m, n), dt) for dt in out_dtypes]
    else:
        out_spec = tile
        out_shape = jax.ShapeDtypeStruct((m, n), out_dtype)
    return _pcall(
        body, name=name, grid=(m // tm, n // tn, nk), in_specs=in_specs, out_specs=out_spec, out_shape=out_shape,
        scratch_shapes=[pltpu.VMEM((tm, tn), F32)],
        compiler_params=_cparams(("parallel", "parallel", "arbitrary")), carry=carry,
    )(*args)


def rms_fwd(x, w, *, name):
    t, d = x.shape
    tb = _pick(t, (256, 128, 64))

    def body(x_ref, w_ref, o_ref):
        o_ref[...] = _rms(x_ref[...], w_ref[...]).astype(o_ref.dtype)

    return pl.pallas_call(
        body, name=name, grid=(t // tb,),
        in_specs=[pl.BlockSpec((tb, d), lambda i: (i, 0)), _full((1, d))],
        out_specs=pl.BlockSpec((tb, d), lambda i: (i, 0)),
        out_shape=jax.ShapeDtypeStruct((t, d), BF16),
        compiler_params=_cparams(("parallel",)),
    )(x, w)


def rms_bwd(x, w, dh, dres, *, name, carry=None):
    t, d = x.shape
    tb = _pick(t, (256, 128, 64))

    def body(x_ref, w_ref, dh_ref, dres_ref, dx_ref, dxb_ref, dw_ref):
        @pl.when(pl.program_id(0) == 0)
        def _():
            dw_ref[...] = jnp.zeros_like(dw_ref)

        _, vjp = jax.vjp(_rms, x_ref[...], w_ref[...])
        dx, dw = vjp(dh_ref[...].astype(F32))
        dx = dx + dres_ref[...]
        dx_ref[...] = dx
        dxb_ref[...] = dx.astype(dxb_ref.dtype)
        dw_ref[...] += dw

    row = pl.BlockSpec((tb, d), lambda i: (i, 0))
    return _pcall(
        body, name=name, grid=(t // tb,), carry=carry,
        in_specs=[row, _full((1, d)), row, row],
        out_specs=[row, row, _full((1, d))],
        out_shape=[jax.ShapeDtypeStruct((t, d), F32), jax.ShapeDtypeStruct((t, d), BF16),
                   jax.ShapeDtypeStruct((1, d), F32)],
        compiler_params=_cparams(("arbitrary",)),
    )(x, w, dh, dres)


def _swi(g, u):
    return _silu(g) * u


def _swi_fwd_epi(u, g):
    return u, _swi(g, u)


def _swi_bwd_epi(d, g, u):
    return jax.vjp(_swi, g, u)[1](d)


def loss_head(x, w, target, *, name):
    t, d = x.shape
    tb = _pick(t, (256, 128, 64))

    def f(xv, wv, tv):
        y = _rms(xv, wv)
        e = y - tv
        return 0.5 * jnp.sum(jnp.mean(e * e, axis=-1, keepdims=True), axis=0, keepdims=True)

    def body(x_ref, w_ref, t_ref, l_ref, dx_ref, dxb_ref, dw_ref):
        @pl.when(pl.program_id(0) == 0)
        def _():
            l_ref[...] = jnp.zeros_like(l_ref)
            dw_ref[...] = jnp.zeros_like(dw_ref)

        val, vjp = jax.vjp(lambda a, b: f(a, b, t_ref[...]), x_ref[...], w_ref[...])
        dx, dw = vjp(jnp.ones((1, 1), F32))
        l_ref[...] += jnp.broadcast_to(val, l_ref.shape)
        dx_ref[...] = dx
        dxb_ref[...] = dx.astype(dxb_ref.dtype)
        dw_ref[...] += dw

    row = pl.BlockSpec((tb, d), lambda i: (i, 0))
    return pl.pallas_call(
        body, name=name, grid=(t // tb,),
        in_specs=[row, _full((1, d)), row],
        out_specs=[_full((SUBLANE, LANE)), row, row, _full((1, d))],
        out_shape=[jax.ShapeDtypeStruct((SUBLANE, LANE), F32), jax.ShapeDtypeStruct((t, d), F32),
                   jax.ShapeDtypeStruct((t, d), BF16),
                   jax.ShapeDtypeStruct((1, d), F32)],
        compiler_params=_cparams(("arbitrary",)),
    )(x, w, target)


def _gla_chunk(q, k, v, g, glr, st, wg, bg, wn, tri):
    L, hk = q.shape[-2:]
    la = _log_sigmoid(mm(glr, wg) + bg) / GLA_GATE_NORM
    bcum = sel_l(jnp.broadcast_to(tri, la.shape[:-2] + tri.shape), la)
    b_last = jnp.sum(la, axis=-2, keepdims=True)
    rows = lax.broadcasted_iota(jnp.int32, (L, 1), 0)
    b_mid = jnp.sum(jnp.where(rows <= L // 2, la, 0.0), axis=-2, keepdims=True)
    qs = q * (hk ** -0.5)
    q_in = qs * jnp.exp(bcum - b_mid)
    k_in = k * jnp.exp(b_mid - bcum)
    scores = mm_nt(q_in, k_in) * tri.astype(F32)
    o_intra = mm(scores, v)
    k_st = k * jnp.exp(b_last - bcum)
    d_st = mm_tn(v, k_st)
    o_inter = mm_nt(qs * jnp.exp(bcum), st)
    st_new = jnp.exp(b_last) * st + d_st
    o = _rms(o_intra + o_inter, wn) * _silu(g)
    return o, st_new


def _heads(ref, start, width, n):
    return jnp.stack([ref[:, start + h * width:start + (h + 1) * width] for h in range(n)], axis=0)


def _gla_heads(p_ref, dk, dv, n):
    hk, hv = dk // n, dv // n
    return (_heads(p_ref, 0, hk, n), _heads(p_ref, dk, hk, n), _heads(p_ref, 2 * dk, hv, n),
            _heads(p_ref, 2 * dk + dv, hv, n))


def _gla_dims(d):
    dv = d // 2
    dk = dv // 2
    return dk, dv, dk // GLA_HEADS, dv // GLA_HEADS


def gla_fwd(proj, glr_col, wg, bg, wn, dv, carry=None):
    t = proj.shape[0]
    dk, dv, hk, hv = _gla_dims(2 * dv)
    L, H = CHUNK, GLA_HEADS
    nc = t // L
    wq = 2 * dk + 2 * dv

    def body(p_ref, glr_ref, wg_ref, bg_ref, wn_ref, o_ref, sp_ref, st):
        @pl.when(pl.program_id(0) == 0)
        def _():
            st[...] = jnp.zeros_like(st)

        s_prev = st[...]
        sp_ref[0] = s_prev
        o, s_new = _gla_chunk(*_gla_heads(p_ref, dk, dv, H), jnp.broadcast_to(glr_ref[...], (H, L, LANE)), s_prev,
                              _heads(wg_ref, 0, hk, H), _heads(bg_ref, 0, hk, H), wn_ref[...], _tri(L))
        for h in range(H):
            o_ref[:, h * hv:(h + 1) * hv] = o[h].astype(o_ref.dtype)
        st[...] = s_new

    return _pcall(
        body, carry=carry, name="gla_fwd", grid=(nc,),
        in_specs=[pl.BlockSpec((L, wq), lambda c: (c, 0)), pl.BlockSpec((L, LANE), lambda c: (c, glr_col)),
                  _full(wg.shape), _full(bg.shape), _full(wn.shape)],
        out_specs=[pl.BlockSpec((L, dv), lambda c: (c, 0)), pl.BlockSpec((1, H, hv, hk), lambda c: (c, 0, 0, 0))],
        out_shape=[jax.ShapeDtypeStruct((t, dv), BF16), jax.ShapeDtypeStruct((nc, H, hv, hk), F32)],
        scratch_shapes=[pltpu.VMEM((H, hv, hk), F32)],
        compiler_params=_cparams(("arbitrary",)),
    )(proj, proj, wg, bg, wn)


def gla_bwd(proj, glr_col, wg, bg, wn, sprev, do, dv, gla_w, carry=None):
    t = proj.shape[0]
    dk, _, hk, hv = _gla_dims(2 * dv)
    L, H = CHUNK, GLA_HEADS
    nc = t // L
    wq = 2 * dk + 2 * dv

    def body(p_ref, glr_ref, wg_ref, bg_ref, wn_ref, sp_ref, do_ref, dp_ref, dwg_ref, dbg_ref, dwn_ref, dst):
        @pl.when(pl.program_id(0) == 0)
        def _():
            dst[...] = jnp.zeros_like(dst)
            dwg_ref[...] = jnp.zeros_like(dwg_ref)
            dbg_ref[...] = jnp.zeros_like(dbg_ref)
            dwn_ref[...] = jnp.zeros_like(dwn_ref)

        f = functools.partial(_gla_chunk, tri=_tri(L))
        _, vjp = jax.vjp(f, *_gla_heads(p_ref, dk, dv, H), jnp.broadcast_to(glr_ref[...], (H, L, LANE)), sp_ref[0],
                         _heads(wg_ref, 0, hk, H), _heads(bg_ref, 0, hk, H), wn_ref[...])
        dq, dkk, dvv, dg, dgl, ds, dwg, dbg, dwn = vjp((_heads(do_ref, 0, hv, H), dst[...]))
        for h in range(H):
            dp_ref[:, h * hk:(h + 1) * hk] = dq[h].astype(dp_ref.dtype)
            dp_ref[:, dk + h * hk:dk + (h + 1) * hk] = dkk[h].astype(dp_ref.dtype)
            dp_ref[:, 2 * dk + h * hv:2 * dk + (h + 1) * hv] = dvv[h].astype(dp_ref.dtype)
            dp_ref[:, 2 * dk + dv + h * hv:2 * dk + dv + (h + 1) * hv] = dg[h].astype(dp_ref.dtype)
            dwg_ref[:, h * hk:(h + 1) * hk] += dwg[h]
            dbg_ref[:, h * hk:(h + 1) * hk] += dbg[h]
        dst[...] = ds
        dwn_ref[...] += dwn
        dp_ref[:, wq:wq + LANE] = jnp.sum(dgl, axis=0).astype(dp_ref.dtype)
        if gla_w > wq + LANE:
            dp_ref[:, wq + LANE:] = jnp.zeros((L, gla_w - wq - LANE), dp_ref.dtype)

    rev = lambda c: nc - 1 - c
    return _pcall(
        body, carry=carry, name="gla_bwd", grid=(nc,),
        in_specs=[pl.BlockSpec((L, wq), lambda c: (rev(c), 0)), pl.BlockSpec((L, LANE), lambda c: (rev(c), glr_col)),
                  _full(wg.shape), _full(bg.shape), _full(wn.shape),
                  pl.BlockSpec((1, H, hv, hk), lambda c: (rev(c), 0, 0, 0)),
                  pl.BlockSpec((L, dv), lambda c: (rev(c), 0))],
        out_specs=[pl.BlockSpec((L, gla_w), lambda c: (rev(c), 0)),
                   _full(wg.shape), _full(bg.shape), _full(wn.shape)],
        out_shape=[jax.ShapeDtypeStruct(proj.shape, BF16),
                   jax.ShapeDtypeStruct(wg.shape, F32), jax.ShapeDtypeStruct(bg.shape, F32),
                   jax.ShapeDtypeStruct(wn.shape, F32)],
        scratch_shapes=[pltpu.VMEM((H, hv, hk), F32)],
        compiler_params=_cparams(("arbitrary",)),
    )(proj, proj, wg, bg, wn, sprev, do)


def _shift_down(x, tail, s):
    if s == 0:
        return x
    r = pltpu.roll(x, s, 0)
    rows = lax.broadcasted_iota(jnp.int32, tail.shape, 0)
    top = jnp.where(rows < s, pltpu.roll(tail, s, 0), r[:SUBLANE])
    return jnp.concatenate([top, r[SUBLANE:]], axis=0)


def _shift_up(x, head, s):
    if s == 0:
        return x
    n = x.shape[0]
    r = pltpu.roll(x, n - s, 0)
    rows = lax.broadcasted_iota(jnp.int32, head.shape, 0)
    bottom = jnp.where(rows >= SUBLANE - s, pltpu.roll(head, SUBLANE - s, 0), r[n - SUBLANE:])
    return jnp.concatenate([r[:n - SUBLANE], bottom], axis=0)


def _conv(x, prev, w, b):
    y = b
    for k in range(CONV_WIDTH):
        y = y + w[k:k + 1, :] * _shift_down(x, prev, CONV_WIDTH - 1 - k)
    return y


def _conv_bwd(dy, nxt, x, prev, w):
    dx = None
    dws = []
    for k in range(CONV_WIDTH):
        s = CONV_WIDTH - 1 - k
        term = w[k:k + 1, :] * _shift_up(dy, nxt, s)
        dx = term if dx is None else dx + term
        dws.append(jnp.sum(dy * _shift_down(x, prev, s), axis=0, keepdims=True))
    return dx, jnp.concatenate(dws, axis=0), jnp.sum(dy, axis=0, keepdims=True)


def _scan_fwd(a, u):
    n = a.shape[0]
    rows = lax.broadcasted_iota(jnp.int32, a.shape, 0)
    s = 1
    while s < n:
        a_sh = jnp.where(rows < s, 1.0, pltpu.roll(a, s, 0))
        u_sh = jnp.where(rows < s, 0.0, pltpu.roll(u, s, 0))
        u = a * u_sh + u
        a = a * a_sh
        s *= 2
    return a, u


def _scan_rev(c, d):
    n = c.shape[0]
    rows = lax.broadcasted_iota(jnp.int32, c.shape, 0)
    s = 1
    while s < n:
        c_sh = jnp.where(rows >= n - s, 0.0, pltpu.roll(c, n - s, 0))
        d_sh = jnp.where(rows >= n - s, 0.0, pltpu.roll(d, n - s, 0))
        d = d + c * d_sh
        c = c * c_sh
        s *= 2
    return d


def _expm1(x):
    small = x * (1.0 + x * (0.5 + x * (1.0 / 6.0 + x * (1.0 / 24.0))))
    return jnp.where(jnp.abs(x) < 1e-2, small, jnp.exp(x) - 1.0)


def _lru_gates(xc, pa, pi, lam):
    r = _sigmoid(pa)
    i = _sigmoid(pi)
    log_a = LRU_C * r * _log_sigmoid(lam)
    a = jnp.exp(log_a)
    u = jnp.sqrt(-_expm1(2.0 * log_a)) * (i * xc)
    return a, u


def _lru_out(h, gate):
    return h * _gelu_tanh(gate)


def _blockdiag(xc, w_ref, b):
    nb = w_ref.shape[0]
    outs = [mm(xc[:, n * LRU_BLOCK:(n + 1) * LRU_BLOCK], w_ref[n]) for n in range(nb)]
    return jnp.concatenate(outs, axis=1) + b


def lru_fwd(proj, xcol, lw, cw, cb, wa, ba, wi, bi, lam, carry=None):
    t = proj.shape[0]
    tb = _pick(t, (256, 128, 64))
    nb = t // tb

    def body(x_ref, xp_ref, g_ref, cw_ref, cb_ref, wa_ref, ba_ref, wi_ref, bi_ref, lam_ref, o_ref, hin_ref, hc):
        i = pl.program_id(0)

        @pl.when(i == 0)
        def _():
            hc[...] = jnp.zeros_like(hc)

        prev = jnp.where(i == 0, 0.0, xp_ref[...])
        xc = _conv(x_ref[...], prev, cw_ref[...], cb_ref[...])
        a, u = _lru_gates(xc, _blockdiag(xc, wa_ref, ba_ref[...]), _blockdiag(xc, wi_ref, bi_ref[...]), lam_ref[...])
        acum, h0 = _scan_fwd(a, u)
        h = h0 + acum * hc[...]
        hin_ref[0] = hc[...]
        hc[...] = h[tb - 1:tb, :]
        o_ref[...] = _lru_out(h, g_ref[...]).astype(o_ref.dtype)

    row = lambda col: pl.BlockSpec((tb, lw), lambda i: (i, col))
    return _pcall(
        body, carry=carry, name="lru_fwd", grid=(nb,),
        in_specs=[row(xcol), pl.BlockSpec((SUBLANE, lw), lambda i: (jnp.maximum(i * (tb // SUBLANE) - 1, 0), xcol)),
                  row(xcol + 1),
                  _full(cw.shape), _full(cb.shape), _full(wa.shape), _full(ba.shape), _full(wi.shape), _full(bi.shape),
                  _full(lam.shape)],
        out_specs=[pl.BlockSpec((tb, lw), lambda i: (i, 0)), pl.BlockSpec((1, 1, lw), lambda i: (i, 0, 0))],
        out_shape=[jax.ShapeDtypeStruct((t, lw), BF16), jax.ShapeDtypeStruct((nb, 1, lw), F32)],
        scratch_shapes=[pltpu.VMEM((1, lw), F32)],
        compiler_params=_cparams(("arbitrary",)),
    )(proj, proj, proj, cw, cb, wa, ba, wi, bi, lam)


def lru_bwd(proj, xcol, lw, cw, cb, wa, ba, wi, bi, lam, hin, dmix, docol, dproj, carry=None):
    t = proj.shape[0]
    tb = _pick(t, (256, 128, 64))
    nb = t // tb
    nblk = wa.shape[0]
    assert xcol % 2 == 0

    def body(x_ref, xp_ref, g_ref, cw_ref, cb_ref, wa_ref, ba_ref, wi_ref, bi_ref, lam_ref, hin_ref, do_ref, _,
             dxg_ref, dcw_ref, dcb_ref, dwa_ref, dba_ref, dwi_ref, dbi_ref, dlam_ref, gc, dxcn):
        pid = pl.program_id(0)
        i = nb - 1 - pid

        @pl.when(pid == 0)
        def _():
            gc[...] = jnp.zeros_like(gc)
            dxcn[...] = jnp.zeros_like(dxcn)
            for r in (dcw_ref, dcb_ref, dwa_ref, dba_ref, dwi_ref, dbi_ref, dlam_ref):
                r[...] = jnp.zeros_like(r)

        x = x_ref[...]
        prev = jnp.where(i == 0, 0.0, xp_ref[...])
        cw_v = cw_ref[...]
        xc = _conv(x, prev, cw_v, cb_ref[...])
        pa = _blockdiag(xc, wa_ref, ba_ref[...])
        pi = _blockdiag(xc, wi_ref, bi_ref[...])
        (a, u), vjp_g = jax.vjp(_lru_gates, xc, pa, pi, lam_ref[...])
        acum, h0 = _scan_fwd(a, u)
        hi = hin_ref[0]
        h = h0 + acum * hi
        rows = lax.broadcasted_iota(jnp.int32, h.shape, 0)
        hprev = jnp.where(rows < 1, hi, pltpu.roll(h, 1, 0))
        _, vjp_o = jax.vjp(_lru_out, h, g_ref[...])
        dh, dgate = vjp_o(do_ref[...].astype(F32))
        c = jnp.where(rows >= tb - 1, 0.0, pltpu.roll(a, tb - 1, 0))
        g = _scan_rev(c, dh + jnp.where(rows == tb - 1, gc[...], 0.0))
        gc[...] = a[0:1, :] * g[0:1, :]
        dxc, dpa, dpi, dlam = vjp_g((g * hprev, g))
        dlam_ref[...] += dlam
        dba_ref[...] += jnp.sum(dpa, axis=0, keepdims=True)
        dbi_ref[...] += jnp.sum(dpi, axis=0, keepdims=True)
        parts = []
        for n in range(nblk):
            sl = slice(n * LRU_BLOCK, (n + 1) * LRU_BLOCK)
            dwa_ref[n] += mm_tn(xc[:, sl], dpa[:, sl])
            dwi_ref[n] += mm_tn(xc[:, sl], dpi[:, sl])
            parts.append(mm_nt(dpa[:, sl], wa_ref[n]) + mm_nt(dpi[:, sl], wi_ref[n]))
        dxc = dxc + jnp.concatenate(parts, axis=1)
        dx, dcw, dcb = _conv_bwd(dxc, dxcn[...], x, prev, cw_v)
        dxcn[...] = dxc[:SUBLANE]
        dcw_ref[...] += dcw
        dcb_ref[...] += dcb
        dxg_ref[:, :lw] = dx.astype(dxg_ref.dtype)
        dxg_ref[:, lw:] = dgate.astype(dxg_ref.dtype)

    row = lambda col: pl.BlockSpec((tb, lw), lambda p: (nb - 1 - p, col))
    params = [cw, cb, wa, ba, wi, bi, lam]
    return _pcall(
        body, carry=carry, name="lru_bwd", grid=(nb,),
        in_specs=[row(xcol),
                  pl.BlockSpec((SUBLANE, lw), lambda p: (jnp.maximum((nb - 1 - p) * (tb // SUBLANE) - 1, 0), xcol)),
                  row(xcol + 1)]
        + [_full(p.shape) for p in params]
        + [pl.BlockSpec((1, 1, lw), lambda p: (nb - 1 - p, 0, 0)), row(docol), ANY],
        out_specs=[pl.BlockSpec((tb, 2 * lw), lambda p: (nb - 1 - p, xcol // 2))] + [_full(p.shape) for p in params],
        out_shape=[jax.ShapeDtypeStruct(dproj.shape, dproj.dtype)]
        + [jax.ShapeDtypeStruct(p.shape, F32) for p in params],
        input_output_aliases={12: 0},
        scratch_shapes=[pltpu.VMEM((1, lw), F32), pltpu.VMEM((SUBLANE, lw), F32)],
        compiler_params=_cparams(("arbitrary",)),
    )(proj, proj, proj, *params, hin, dmix, dproj)


def conv_silu_fwd(proj, col0, width, cw, cb, carry=None):
    t = proj.shape[0]
    tb = _pick(t, (512, 256, 128, 64))
    cbw = _pick(width, (512, 256, 128))
    off = col0 // cbw
    assert col0 % cbw == 0

    def body(x_ref, xp_ref, w_ref, b_ref, o_ref):
        prev = jnp.where(pl.program_id(1) == 0, 0.0, xp_ref[...])
        o_ref[...] = _silu(_conv(x_ref[...], prev, w_ref[...], b_ref[...]))

    return _pcall(
        body, carry=carry, name="conv_silu_fwd", grid=(width // cbw, t // tb),
        in_specs=[pl.BlockSpec((tb, cbw), lambda j, i: (i, off + j)),
                  pl.BlockSpec((SUBLANE, cbw), lambda j, i: (jnp.maximum(i * (tb // SUBLANE) - 1, 0), off + j)),
                  pl.BlockSpec((CONV_WIDTH, cbw), lambda j, i: (0, j)), pl.BlockSpec((1, cbw), lambda j, i: (0, j))],
        out_specs=pl.BlockSpec((tb, cbw), lambda j, i: (i, j)),
        out_shape=jax.ShapeDtypeStruct((t, width), F32),
        compiler_params=_cparams(("parallel", "arbitrary")),
    )(proj, proj, cw, cb)


def conv_silu_bwd(proj, col0, width, cw, cb, dact, dproj, carry=None):
    t = proj.shape[0]
    tb = _pick(t, (512, 256, 128, 64))
    nb = t // tb
    cbw = _pick(width, (512, 256, 128))
    off = col0 // cbw

    def body(x_ref, xp_ref, w_ref, b_ref, d_ref, _, dx_ref, dw_ref, db_ref, nxt):
        pid = pl.program_id(1)
        i = nb - 1 - pid

        @pl.when(pid == 0)
        def _():
            nxt[...] = jnp.zeros_like(nxt)
            dw_ref[...] = jnp.zeros_like(dw_ref)
            db_ref[...] = jnp.zeros_like(db_ref)

        x = x_ref[...]
        prev = jnp.where(i == 0, 0.0, xp_ref[...])
        w = w_ref[...]
        _, vjp = jax.vjp(_silu, _conv(x, prev, w, b_ref[...]))
        (dcv,) = vjp(d_ref[...])
        dx, dw, db = _conv_bwd(dcv, nxt[...], x, prev, w)
        nxt[...] = dcv[:SUBLANE]
        dx_ref[...] = dx.astype(dx_ref.dtype)
        dw_ref[...] += dw
        db_ref[...] += db

    return _pcall(
        body, carry=carry, name="conv_silu_bwd", grid=(width // cbw, nb),
        in_specs=[pl.BlockSpec((tb, cbw), lambda j, p: (nb - 1 - p, off + j)),
                  pl.BlockSpec((SUBLANE, cbw),
                               lambda j, p: (jnp.maximum((nb - 1 - p) * (tb // SUBLANE) - 1, 0), off + j)),
                  pl.BlockSpec((CONV_WIDTH, cbw), lambda j, p: (0, j)), pl.BlockSpec((1, cbw), lambda j, p: (0, j)),
                  pl.BlockSpec((tb, cbw), lambda j, p: (nb - 1 - p, j)), ANY],
        out_specs=[pl.BlockSpec((tb, cbw), lambda j, p: (nb - 1 - p, off + j)),
                   pl.BlockSpec((CONV_WIDTH, cbw), lambda j, p: (0, j)), pl.BlockSpec((1, cbw), lambda j, p: (0, j))],
        out_shape=[jax.ShapeDtypeStruct(dproj.shape, dproj.dtype), jax.ShapeDtypeStruct(cw.shape, F32),
                   jax.ShapeDtypeStruct(cb.shape, F32)],
        scratch_shapes=[pltpu.VMEM((SUBLANE, cbw), F32)],
        input_output_aliases={5: 0},
        compiler_params=_cparams(("parallel", "arbitrary")),
    )(proj, proj, cw, cb, dact, dproj)


def _dt_expand(raw, bias, e):
    return sel_r(_softplus(raw + bias), e)


def dt_fwd(proj, dtcol, bias, e):
    t = proj.shape[0]
    di = e.shape[1]
    tb = _pick(t, (512, 256, 128, 64))

    def body(r_ref, b_ref, e_ref, o_ref):
        o_ref[...] = _dt_expand(r_ref[...], b_ref[...], e_ref[...])

    return pl.pallas_call(
        body, name="dt_fwd", grid=(t // tb,),
        in_specs=[pl.BlockSpec((tb, LANE), lambda i: (i, dtcol)), _full(bias.shape), _full(e.shape)],
        out_specs=pl.BlockSpec((tb, di), lambda i: (i, 0)),
        out_shape=jax.ShapeDtypeStruct((t, di), F32),
        compiler_params=_cparams(("parallel",)),
    )(proj, bias, e)


def dt_bwd(proj, dtcol, bias, e, ddte, dproj):
    t = proj.shape[0]
    di = e.shape[1]
    tb = _pick(t, (512, 256, 128, 64))
    tail = dproj.shape[1] - dtcol * LANE
    assert (dtcol * LANE) % tail == 0

    def body(r_ref, b_ref, e_ref, d_ref, _, dr_ref, db_ref):
        @pl.when(pl.program_id(0) == 0)
        def _():
            db_ref[...] = jnp.zeros_like(db_ref)

        e_v = e_ref[...]
        _, vjp = jax.vjp(lambda r, b: _dt_expand(r, b, e_v), r_ref[...], b_ref[...])
        dr, db = vjp(d_ref[...])
        dr_ref[:, :LANE] = dr.astype(dr_ref.dtype)
        if tail > LANE:
            dr_ref[:, LANE:] = jnp.zeros((tb, tail - LANE), dr_ref.dtype)
        db_ref[...] += db

    return pl.pallas_call(
        body, name="dt_bwd", grid=(t // tb,),
        in_specs=[pl.BlockSpec((tb, LANE), lambda i: (i, dtcol)), _full(bias.shape), _full(e.shape),
                  pl.BlockSpec((tb, di), lambda i: (i, 0)), ANY],
        out_specs=[pl.BlockSpec((tb, tail), lambda i: (i, dtcol * LANE // tail)), _full(bias.shape)],
        out_shape=[jax.ShapeDtypeStruct(dproj.shape, dproj.dtype), jax.ShapeDtypeStruct(bias.shape, F32)],
        input_output_aliases={4: 0},
        compiler_params=_cparams(("arbitrary",)),
    )(proj, bias, e, ddte, dproj)


def head_expand(p, e, *, transpose=False, name):
    di = e.shape[1]

    def body(p_ref, e_ref, o_ref):
        if transpose:
            o_ref[...] = _sel_r_b(e_ref[...], p_ref[...])[0]
        else:
            o_ref[...] = sel_r(p_ref[...], e_ref[...])

    oshape = (SUBLANE, LANE) if transpose else (SUBLANE, di)
    return pl.pallas_call(
        body, name=name, in_specs=[_full(p.shape), _full(e.shape)], out_specs=_full(oshape),
        out_shape=jax.ShapeDtypeStruct(oshape, F32), compiler_params=_cparams(None), grid=(1,),
    )(p, e)


def _ssd_chunk(x, z, bm, cm, dte, st, alog, dskip, gn, tri, cmask, dmask, bd):
    L, gw = x.shape
    reps = gw // L
    a = dte * (-jnp.exp(alog))
    acs = sel_l(tri, a)
    acs_last = jnp.sum(a, axis=0, keepdims=True)
    arow = jnp.sum(acs * dmask, axis=0, keepdims=True)
    dtrow = jnp.sum(dte * dmask, axis=0, keepdims=True)
    cb = mm_nt(cm, jnp.concatenate([bm] * reps, axis=0))
    wts = cb * (jnp.exp(jnp.minimum(acs - arow, 0.0)) * cmask) * dtrow
    xbd = jnp.concatenate([x] * reps, axis=0) * bd
    xw = x * (jnp.exp(acs_last - acs) * dte)
    y = mm(wts, xbd) + mm(cm, st) * jnp.exp(acs) + dskip * x
    st_new = jnp.exp(acs_last) * st + mm_tn(bm, xw)
    return _rms(y * _silu(z), gn), st_new


def _ssd_dims(di):
    gw = di // SSD_GROUPS
    assert CHUNK == SSD_HEAD_DIM and gw % LANE == 0
    return gw, SSD_STATE


def _ssd_masks(gw):
    L = CHUNK
    r = jnp.arange(L)[:, None]
    c = jnp.arange(gw)[None, :]
    cmask = ((c % L) <= r).astype(F32)
    dmask = ((c % L) == r).astype(F32)
    rr = jnp.arange(gw)
    bd = ((rr[:, None] // L) == (rr[None, :] // L)).astype(F32)
    tri = (jnp.arange(L)[None, :] <= jnp.arange(L)[:, None]).astype(BF16)
    return tri, cmask, dmask, bd


def ssd_fwd(xs, proj, dte, alog_e, dskip_e, gn, carry=None):
    t, di = dte.shape
    gw, n = _ssd_dims(di)
    L, G = CHUNK, SSD_GROUPS
    nc = t // L
    masks = _ssd_masks(gw)
    cdim = xs.shape[1]

    def body(x_ref, z_ref, dt_ref, al_ref, ds_ref, gn_ref, tri_ref, cm_ref, dm_ref, bd_ref, y_ref, sp_ref, st):
        @pl.when(pl.program_id(0) == 0)
        def _():
            st[...] = jnp.zeros_like(st)

        for g in range(G):
            ch = slice(g * gw, (g + 1) * gw)
            s_prev = st[g]
            sp_ref[0, g] = s_prev
            y, s_new = _ssd_chunk(x_ref[:, ch], z_ref[:, ch], x_ref[:, di + g * n:di + (g + 1) * n],
                                  x_ref[:, di + (G + g) * n:di + (G + g + 1) * n], dt_ref[:, ch], s_prev,
                                  al_ref[0:1, ch], ds_ref[0:1, ch], gn_ref[:, ch], tri_ref[...], cm_ref[...],
                                  dm_ref[...], bd_ref[...])
            y_ref[:, ch] = y.astype(y_ref.dtype)
            st[g] = s_new

    row = lambda w: pl.BlockSpec((L, w), lambda c: (c, 0))
    return _pcall(
        body, carry=carry, name="ssd_fwd", grid=(nc,),
        in_specs=[row(cdim), row(di), row(di), _full(alog_e.shape), _full(dskip_e.shape), _full(gn.shape)]
        + [_full(m.shape) for m in masks],
        out_specs=[row(di), pl.BlockSpec((1, G, n, gw), lambda c: (c, 0, 0, 0))],
        out_shape=[jax.ShapeDtypeStruct((t, di), BF16), jax.ShapeDtypeStruct((nc, G, n, gw), F32)],
        scratch_shapes=[pltpu.VMEM((G, n, gw), F32)],
        compiler_params=_cparams(("arbitrary",)),
    )(xs, proj, dte, alog_e, dskip_e, gn, *masks)


def ssd_bwd(xs, proj, dte, alog_e, dskip_e, gn, sprev, dy, dproj_shape, carry=None):
    t, di = dte.shape
    gw, n = _ssd_dims(di)
    L, G = CHUNK, SSD_GROUPS
    nc = t // L
    masks = _ssd_masks(gw)
    cdim = xs.shape[1]

    def body(x_ref, z_ref, dt_ref, al_ref, ds_ref, gn_ref, tri_ref, cm_ref, dm_ref, bd_ref, sp_ref, dy_ref,
             dxs_ref, dz_ref, ddt_ref, dal_ref, dds_ref, dgn_ref, dst):
        @pl.when(pl.program_id(0) == 0)
        def _():
            dst[...] = jnp.zeros_like(dst)
            dal_ref[...] = jnp.zeros_like(dal_ref)
            dds_ref[...] = jnp.zeros_like(dds_ref)
            dgn_ref[...] = jnp.zeros_like(dgn_ref)

        f = functools.partial(_ssd_chunk, tri=tri_ref[...], cmask=cm_ref[...], dmask=dm_ref[...], bd=bd_ref[...])
        for g in range(G):
            ch = slice(g * gw, (g + 1) * gw)
            bs = slice(di + g * n, di + (g + 1) * n)
            cs = slice(di + (G + g) * n, di + (G + g + 1) * n)
            _, vjp = jax.vjp(f, x_ref[:, ch], z_ref[:, ch], x_ref[:, bs], x_ref[:, cs], dt_ref[:, ch], sp_ref[0, g],
                             al_ref[0:1, ch], ds_ref[0:1, ch], gn_ref[:, ch])
            dx, dz, db, dc, ddt, ds, dal, dds, dgn = vjp((dy_ref[:, ch], dst[g]))
            dxs_ref[:, ch] = dx
            dxs_ref[:, bs] = db
            dxs_ref[:, cs] = dc
            dz_ref[:, ch] = dz.astype(dz_ref.dtype)
            ddt_ref[:, ch] = ddt
            dst[g] = ds
            dal_ref[:, ch] += dal
            dds_ref[:, ch] += dds
            dgn_ref[:, ch] += dgn

    row = lambda w: pl.BlockSpec((L, w), lambda c: (nc - 1 - c, 0))
    acc = _full((1, di))
    acc_shape = jax.ShapeDtypeStruct((1, di), F32)
    return _pcall(
        body, carry=carry, name="ssd_bwd", grid=(nc,),
        in_specs=[row(cdim), row(di), row(di), _full(alog_e.shape), _full(dskip_e.shape), _full(gn.shape)]
        + [_full(m.shape) for m in masks]
        + [pl.BlockSpec((1, G, n, gw), lambda c: (nc - 1 - c, 0, 0, 0)), row(di)],
        out_specs=[row(cdim), row(di), row(di), acc, acc, acc],
        out_shape=[jax.ShapeDtypeStruct((t, cdim), F32), jax.ShapeDtypeStruct(dproj_shape, BF16),
                   jax.ShapeDtypeStruct((t, di), F32), acc_shape, acc_shape, acc_shape],
        scratch_shapes=[pltpu.VMEM((G, n, gw), F32)],
        compiler_params=_cparams(("arbitrary",)),
    )(xs, proj, dte, alog_e, dskip_e, gn, *masks, sprev, dy)


def _rows2d(a):
    return a.reshape(-1, a.shape[-1])


def _row_tile(rows, cols):
    cap = max(SUBLANE, (1 << 19) // max(cols, 1))
    step = 2 * SUBLANE
    for c in range(min(cap, rows) // step * step, 0, -step):
        if rows % c == 0:
            return c
    return rows


def chip_sum(g, r, core, *, name):
    shape = r.shape
    cols = shape[-1]
    g4 = g.reshape(4, 2, -1, cols)
    r3 = r.reshape(4, -1, cols)
    rows = r3.shape[1]
    tr = _row_tile(rows, cols)

    def body(c_ref, g_ref, r_ref, o_ref):
        o_ref[...] = (g_ref[...].astype(F32) + r_ref[...].astype(F32)).astype(o_ref.dtype)

    out = pl.pallas_call(
        body, name=name,
        grid_spec=pltpu.PrefetchScalarGridSpec(
            num_scalar_prefetch=1, grid=(4, rows // tr),
            in_specs=[pl.BlockSpec((None, None, tr, cols), lambda j, i, c: (j, c[0], i, 0)),
                      pl.BlockSpec((None, tr, cols), lambda j, i, c: (j, i, 0))],
            out_specs=pl.BlockSpec((None, tr, cols), lambda j, i, c: (j, i, 0))),
        out_shape=jax.ShapeDtypeStruct(r3.shape, BF16), compiler_params=_cparams(("parallel", "parallel")),
    )(core.reshape(1).astype(jnp.int32), g4, r3)
    return out.reshape(shape)


def mesh_sum(p, r, chip, core, *, name):
    shape = p.shape[1:]
    cols = shape[-1]
    p3 = p.reshape(4, -1, cols)
    r3 = r.reshape(3, -1, cols)
    rows = p3.shape[1]
    tr = _row_tile(rows, 2 * cols)

    def body(c_ref, p_ref, r_ref, o_ref):
        o_ref[...] = ((p_ref[...].astype(F32) + r_ref[0].astype(F32)) + r_ref[1].astype(F32)) + r_ref[2].astype(F32)

    out = pl.pallas_call(
        body, name=name,
        grid_spec=pltpu.PrefetchScalarGridSpec(
            num_scalar_prefetch=1, grid=(rows // tr,),
            in_specs=[pl.BlockSpec((None, tr, cols), lambda i, c: (c[0], i, 0)),
                      pl.BlockSpec((3, tr, cols), lambda i, c: (0, i, 0))],
            out_specs=pl.BlockSpec((None, tr, cols), lambda i, c: (c[1], i, 0))),
        out_shape=jax.ShapeDtypeStruct((2, rows, cols), F32), compiler_params=_cparams(("parallel",)),
    )(jnp.stack([chip, core]).astype(jnp.int32), p3, r3)
    return out.reshape((2,) + shape)


def sum_leading(x, *, name):
    k, r, c = x.shape
    tr = _row_tile(r, c * k)
    def body(x_ref, o_ref):
        acc = x_ref[0]
        for i in range(1, k):
            acc = acc + x_ref[i]
        o_ref[...] = acc

    return pl.pallas_call(
        body, name=name, grid=(r // tr,), in_specs=[pl.BlockSpec((k, tr, c), lambda i: (0, i, 0))],
        out_specs=pl.BlockSpec((tr, c), lambda i: (i, 0)),
        out_shape=jax.ShapeDtypeStruct((r, c), F32), compiler_params=_cparams(("parallel",)),
    )(x)


def adamw(w, gs, m, v, *, name):
    shape = w.shape
    w3, m3, v3 = (a.reshape((-1,) + a.shape[-2:]) for a in (w, m, v))
    nl, r, c = w3.shape
    assert len(gs) == nl
    tr = _row_tile(r, 2 * c)
    tc = c
    if tr == r and r * c > (1 << 19):
        tc = next(t for t in (1024, 512, 256, 128) if c % t == 0 and r * t <= (1 << 19))
    c1 = 1.0 - ADAM_B1 ** ADAM_STEP
    c2 = 1.0 - ADAM_B2 ** ADAM_STEP

    def body(w_ref, g_ref, m_ref, v_ref, *rest):
        go_ref, d_ref, mo_ref, vo_ref = rest[-4:]
        gv = g_ref[...]
        mn = ADAM_B1 * m_ref[...] + (1.0 - ADAM_B1) * gv
        vn = ADAM_B2 * v_ref[...] + (1.0 - ADAM_B2) * (gv * gv)
        go_ref[...] = gv
        d_ref[...] = -ADAM_LR * ((mn / c1) / (jnp.sqrt(vn / c2) + ADAM_EPS) + ADAM_WD * w_ref[...])
        mo_ref[...] = mn
        vo_ref[...] = vn

    outs = None
    for l, g in enumerate(gs):
        layer = pl.BlockSpec((None, tr, tc), lambda i, j, l=l: (l, i, j))
        prev = [] if outs is None else list(outs)
        outs = pl.pallas_call(
            functools.partial(body), name=f"{name}_{l}", grid=(r // tr, c // tc),
            in_specs=[layer, pl.BlockSpec((tr, tc), lambda i, j: (i, j)), layer, layer] + [ANY] * len(prev),
            out_specs=[layer] * 4, out_shape=[jax.ShapeDtypeStruct((nl, r, c), F32)] * 4,
            input_output_aliases={4 + k: k for k in range(len(prev))},
            compiler_params=_cparams(("parallel", "parallel")),
        )(w3, g.reshape(r, c), m3, v3, *prev)
    return tuple(o.reshape(shape) for o in outs)


ANY = pl.BlockSpec(memory_space=pl.ANY)


def _place():
    x, y, c = lax.axis_index("x"), lax.axis_index("y"), lax.axis_index("c")
    chips = [(1 - x, y), (x, 1 - y), (1 - x, 1 - y)]
    return x, y, c, chips


def gather8(block):
    m, n = block.shape

    def body(x_ref, out_ref, send_sems, recv_sems, local_sem):
        x, y, c, chips = _place()
        me, sibling = (x, y, c), (x, y, 1 - c)

        def rows(px, py, pc):
            return out_ref.at[4 * px + 2 * py + pc]

        def copy(k, blk, to, src=None):
            return pltpu.make_async_remote_copy(
                src_ref=rows(*blk) if src is None else src, dst_ref=rows(*blk), send_sem=send_sems.at[k],
                recv_sem=recv_sems.at[k], device_id=to, device_id_type=MESH)

        mine = pltpu.make_async_copy(x_ref, rows(*me), local_sem)
        mine.start()
        first = [copy(0, me, sibling, src=x_ref)]
        first += [copy(1 + j, me, (*chip, c), src=x_ref) for j, chip in enumerate(chips)]
        for cp in first:
            cp.start()
        passed = [copy(4 + j, (*chip, c), sibling) for j, chip in enumerate(chips)]
        for j, chip in enumerate(chips):
            copy(1 + j, (*chip, c), me).wait_recv()
            passed[j].start()
        copy(0, sibling, me).wait_recv()
        for j, chip in enumerate(chips):
            copy(4 + j, (*chip, 1 - c), me).wait_recv()
        for cp in first + passed:
            cp.wait_send()
        mine.wait()

    return pl.pallas_call(
        body, name="gather8",
        out_shape=jax.ShapeDtypeStruct((8, m, n), block.dtype),
        in_specs=[pl.BlockSpec(memory_space=pltpu.VMEM)],
        out_specs=pl.BlockSpec(memory_space=pltpu.VMEM),
        scratch_shapes=[pltpu.SemaphoreType.DMA((7,)), pltpu.SemaphoreType.DMA((7,)), pltpu.SemaphoreType.DMA],
        compiler_params=pltpu.CompilerParams(vmem_limit_bytes=VMEM_LIMIT),
    )(block)


def gather_weights(shards):
    n = len(shards)

    def copy(ins, outs, send, recv, base, a, k, chip_idx, half, to, src=None):
        dst = outs[a].at[chip_idx, half]
        return pltpu.make_async_remote_copy(
            src_ref=dst if src is None else src, dst_ref=dst, send_sem=send.at[base + 6 * a + k],
            recv_sem=recv.at[base + 6 * a + k], device_id=to, device_id_type=MESH)

    def first(ins, outs, send, recv, base):
        x, y, c, chips = _place()
        return [copy(ins, outs, send, recv, base, a, j, 2 * x + y, c, (*chip, c), src=ins[a].at[c])
                for a in range(n) for j, chip in enumerate(chips)]

    def start(ins, outs, send, recv, base):
        for cp in first(ins, outs, send, recv, base):
            cp.start()

    def finish(ins, outs, send, recv, base):
        x, y, c, chips = _place()
        sibling = (x, y, 1 - c)
        passed = []
        for a in range(n):
            for j, (cx, cy) in enumerate(chips):
                copy(ins, outs, send, recv, base, a, j, 2 * cx + cy, c, (cx, cy, c)).wait_recv()
                fw = copy(ins, outs, send, recv, base, a, 3 + j, 2 * cx + cy, c, sibling)
                fw.start()
                passed.append(fw)
        for a in range(n):
            for j, (cx, cy) in enumerate(chips):
                copy(ins, outs, send, recv, base, a, 3 + j, 2 * cx + cy, 1 - c, sibling).wait_recv()
        for cp in first(ins, outs, send, recv, base) + passed:
            cp.wait_send()

    return Carry(shards, [jax.ShapeDtypeStruct((4,) + s.shape, s.dtype) for s in shards], 6 * n, start, finish)


def exchange_halves(grads):
    n = len(grads)

    def copies(ins, outs, send, recv, base):
        x, y, c, _ = _place()
        return [pltpu.make_async_remote_copy(
            src_ref=ins[a].at[j, 1 - c], dst_ref=outs[a].at[j], send_sem=send.at[base + 4 * a + j],
            recv_sem=recv.at[base + 4 * a + j], device_id=(x, y, 1 - c), device_id_type=MESH)
            for a in range(n) for j in range(4)]

    def start(*args):
        for cp in copies(*args):
            cp.start()

    def finish(*args):
        for cp in copies(*args):
            cp.wait()

    return Carry(grads, [jax.ShapeDtypeStruct((4,) + g.shape[2:], g.dtype) for g in grads], 4 * n, start, finish)


def scatter_chips(parts, rows=None, into=None):
    n = len(parts)
    sl = (lambda r: r) if rows is None else (lambda r: r.at[pl.ds(rows[0], rows[1])])

    def copies(ins, outs, send, recv, base):
        x, y, c, chips = _place()
        return [pltpu.make_async_remote_copy(
            src_ref=sl(ins[a].at[2 * cx + cy]), dst_ref=sl(outs[a].at[j]), send_sem=send.at[base + 3 * a + j],
            recv_sem=recv.at[base + 3 * a + j], device_id=(cx, cy, c), device_id_type=MESH)
            for a in range(n) for j, (cx, cy) in enumerate(chips)]

    def start(*args):
        for cp in copies(*args):
            cp.start()

    def finish(*args):
        for cp in copies(*args):
            cp.wait()

    shapes = [jax.ShapeDtypeStruct((3,) + p.shape[1:], p.dtype) for p in parts]
    if into is None:
        return Carry(parts, shapes, 3 * n, start, finish)
    return Carry(list(parts) + list(into), shapes, 3 * n, start, finish, aliases={n + a: a for a in range(n)})


def join_halves(bufs):
    n = len(bufs)

    def copy(outs, send, recv, base, a, half):
        x, y, c, _ = _place()
        return pltpu.make_async_remote_copy(
            src_ref=outs[a].at[c], dst_ref=outs[a].at[c if half is None else half], send_sem=send.at[base + a],
            recv_sem=recv.at[base + a], device_id=(x, y, 1 - c), device_id_type=MESH)

    def start(ins, outs, send, recv, base):
        for a in range(n):
            copy(outs, send, recv, base, a, None).start()

    def finish(ins, outs, send, recv, base):
        c = lax.axis_index("c")
        for a in range(n):
            copy(outs, send, recv, base, a, 1 - c).wait_recv()
        for a in range(n):
            copy(outs, send, recv, base, a, None).wait_send()

    return Carry(bufs, [jax.ShapeDtypeStruct(h.shape, h.dtype) for h in bufs], n, start, finish,
                 aliases={a: a for a in range(n)})


def run_comm(carry, *, name):
    k_in, k_out = len(carry.arrays), len(carry.out_shape)

    def body(*refs):
        ins, outs = refs[:k_in], refs[k_in:k_in + k_out]
        send, recv = refs[-2:]
        carry.start(ins, outs, send, recv, 0)
        carry.finish(ins, outs, send, recv, 0)

    return pl.pallas_call(
        body, name=name, out_shape=carry.out_shape, in_specs=[ANY] * k_in, out_specs=[ANY] * k_out,
        scratch_shapes=[pltpu.SemaphoreType.DMA((carry.n_sems,))] * 2, input_output_aliases=carry.aliases,
    )(*carry.arrays)


INPUTS = ['x'] + WEIGHTS + ['loss_target'] + ['m_' + n for n in WEIGHTS] + ['v_' + n for n in WEIGHTS]


def _round_up(n, m):
    return -(-n // m) * m


def _pack(arrs):
    flat = jnp.concatenate([a.reshape(-1) for a in arrs])
    n = _round_up(flat.shape[0], 512 * LANE)
    return jnp.pad(flat, (0, n - flat.shape[0])).reshape(-1, LANE)


def _unpack(block, shapes):
    flat = block.reshape(-1)
    out, o = [], 0
    for s in shapes:
        n = math.prod(s)
        out.append(flat[o:o + n].reshape(s))
        o += n
    return out


def _cols(g):
    return g.transpose(1, 0, 2).reshape(g.shape[1], -1)


def _uncols(w):
    return w.reshape(w.shape[0], 4, -1).transpose(1, 0, 2)


def _pad_cols(w, total):
    return jnp.pad(w, ((0, 0), (0, total - w.shape[1])))


def kernel(*args):
    a = dict(zip(INPUTS, args))
    x, tgt = a['x'][0], a['loss_target'][0]
    t, d = x.shape
    xi, yi, ci = lax.axis_index("x"), lax.axis_index("y"), lax.axis_index("c")
    chip = 2 * xi + yi
    dk, dv, hk, hv = _gla_dims(d)
    lw = d // 2
    di = 2 * d
    nh = di // SSD_HEAD_DIM
    gn_w = SSD_GROUPS * SSD_STATE
    conv_dim = di + 2 * gn_w
    rank = GLA_GATE_RANK
    wq = 2 * dk + 2 * dv
    gla_w = _round_up(wq + LANE, 2 * lw)
    ev_tot = gla_w + 2 * lw
    od_used = di + conv_dim + nh
    od_tot = _round_up(di + conv_dim + _round_up(nh, LANE), 512)
    glr_col, xcol, dtcol = wq // LANE, gla_w // lw, (di + conv_dim) // LANE
    assert ev_tot % 512 == 0 and nh <= LANE

    def halves(w):
        w = w.astype(BF16)
        return w.reshape((2, w.shape[0] // 2) + w.shape[1:])

    own = {'ev_w_in': halves(a['ev_w_in'][0]), 'ev_w_out': halves(a['ev_w_out'][0]),
           'od_w_in_a': halves(a['od_w_in'][0][:d // 2]), 'od_w_in_b': halves(a['od_w_in'][0][d // 2:]),
           'od_w_out': halves(a['od_w_out'][0])}
    for l in range(2):
        own[f'gate{l}'], own[f'up{l}'] = halves(a['ffn_w_gate'][l]), halves(a['ffn_w_up'][l])
        own[f'down{l}'] = halves(a['ffn_w_down'][l])

    def gather(*units):
        return gather_weights([own[u] for u in units])

    def filled(unit, g):
        g = lax.dynamic_update_index_in_dim(g, own[unit], chip, 0)
        return g.reshape((4, 2 * g.shape[2]) + g.shape[3:])

    (g_ev_in,) = run_comm(gather('ev_w_in'), name="gather_ev")
    w_ev_in = _cols(filled('ev_w_in', g_ev_in))
    cuts = [dk, 2 * dk, 2 * dk + dv, wq, wq + rank, wq + rank + lw]
    sq, sk, sv, sg, sglr, sxb, sgb = jnp.split(w_ev_in, cuts, axis=1)
    w_ev_in_p = jnp.concatenate([_pad_cols(jnp.concatenate([sq, sk, sv, sg, sglr], axis=1), gla_w), sxb, sgb], axis=1)
    w_gate, w_up, w_down = [None, None], [None, None], [None, None]

    sh_names = list(SMALL_SHARDED)
    sh_shapes = [a[n].shape for n in sh_names]
    g8 = gather8(_pack([a[n] for n in sh_names]))
    per_chip = [_unpack(g8[2 * j], sh_shapes) for j in range(4)]
    full = {n: jnp.concatenate([per_chip[j][i] for j in range(4)], axis=SMALL_SHARDED[n])
            for i, n in enumerate(sh_names)}

    wg_p = jnp.zeros((LANE, dk), F32).at[:rank].set(full['ev_gla_w_gate'][0])
    bg, wn = a['ev_gla_b_gate'], a['ev_gla_w_onorm']
    lru_p = [full['ev_lru_conv_w'][0], a['ev_lru_conv_b'], a['ev_lru_w_a'][0], a['ev_lru_b_a'], a['ev_lru_w_i'][0],
             a['ev_lru_b_i'], a['ev_lru_lam']]
    od_cw, od_cb, od_gn = full['od_conv_w'][0], full['od_conv_b'], full['od_gnorm']
    heads = jnp.arange(LANE)[:, None]
    e_mat = ((jnp.arange(di)[None, :] // SSD_HEAD_DIM == heads) & (heads < nh)).astype(BF16)
    row8 = lambda p: jnp.zeros((SUBLANE, LANE), F32).at[0, :nh].set(p[0])
    dt_bias_p = jnp.zeros((1, LANE), F32).at[0, :nh].set(a['od_dt_bias'][0])
    alog_e = head_expand(row8(a['od_a_log']), e_mat, name="expand_a_log")
    dskip_e = head_expand(row8(a['od_d_skip']), e_mat, name="expand_d_skip")

    h0 = rms_fwd(x, a['ev_norm'], name="rms_ev")
    proj, (g,) = matmul(h0, w_ev_in_p, name="ev_in", carry=gather('gate0'))
    w_gate[0] = filled('gate0', g)
    (o_gla, sp_gla), (g,) = gla_fwd(proj, glr_col, wg_p, bg, wn, dv, carry=gather('up0'))
    w_up[0] = filled('up0', g)
    (o_lru, hin), (g,) = lru_fwd(proj, xcol, lw, *lru_p, carry=gather('ev_w_out'))
    w_ev_out = filled('ev_w_out', g).reshape(-1, d)
    x1 = matmul(o_gla, w_ev_out[:dv], add=x, name="ev_out_a")
    x1 = matmul(o_lru, w_ev_out[dv:], add=x1, name="ev_out_b")

    h1 = rms_fwd(x1, a['ffn_norm'][0:1], name="rms_ffn0")
    gate0, (g,) = matmul(h1, w_gate[0], name="ffn0_gate", carry=gather('down0'))
    w_down[0] = filled('down0', g).reshape(-1, d)
    swi = dict(epi=_swi_fwd_epi, epi_out=(F32, BF16))
    (up0, act0), (g_a,) = matmul(h1, w_up[0], name="ffn0_up", epi_in=(gate0,), carry=gather('od_w_in_a'), **swi)
    x2, (g_b,) = matmul(act0, w_down[0], add=x1, name="ffn0_down", carry=gather('od_w_in_b'))
    w_od_in = jnp.concatenate([filled('od_w_in_a', g_a), filled('od_w_in_b', g_b)], axis=1)
    w_od_in_p = _pad_cols(_cols(w_od_in), od_tot)

    h2 = rms_fwd(x2, full['od_norm'], name="rms_od")
    proj2, (g, g1) = matmul(h2, w_od_in_p, name="od_in", carry=gather('od_w_out', 'gate1'))
    w_od_out, w_gate[1] = filled('od_w_out', g).reshape(-1, d), filled('gate1', g1)
    xs, (g,) = conv_silu_fwd(proj2, di, conv_dim, od_cw, od_cb, carry=gather('up1'))
    w_up[1] = filled('up1', g)
    dte = dt_fwd(proj2, dtcol, dt_bias_p, e_mat)
    (y_ssd, sp_ssd), (g,) = ssd_fwd(xs, proj2, dte, alog_e, dskip_e, od_gn, carry=gather('down1'))
    w_down[1] = filled('down1', g).reshape(-1, d)
    x3 = matmul(y_ssd, w_od_out, add=x2, name="od_out")
    h3 = rms_fwd(x3, a['ffn_norm'][1:2], name="rms_ffn1")
    gate1 = matmul(h3, w_gate[1], name="ffn1_gate")
    up1, act1 = matmul(h3, w_up[1], name="ffn1_up", epi_in=(gate1,), **swi)
    x4 = matmul(act1, w_down[1], add=x3, name="ffn1_down")
    loss_p, dx4, dx4b, d_final = loss_head(x4, a['final_norm'][None], tgt, name="loss_head")

    grads, from_sib, part, from_chips = {}, {}, {}, {}

    def rows4(dw):
        return dw.reshape((4, 2, dw.shape[0] // 8) + dw.shape[1:])

    def cols4(dw):
        return dw.reshape((4, 2, dw.shape[1] // 2) + dw.shape[2:])

    def exchange(*units):
        return exchange_halves([grads[u] for u in units])

    def scatter(*units):
        return scatter_chips([part[u] for u in units])

    def sum_chip(u):
        part[u] = chip_sum(grads[u], from_sib[u], ci, name=f"chip_sum_{u}")

    def ffn_bwd(dxo, dxob, xin, h, gate, up, act, l, first_carry, first_units):
        dn, gt, up_ = f'down{l}', f'gate{l}', f'up{l}'
        dgu = matmul(dxob, w_down[l], tb=True, name=f"ffn{l}_d_act", carry=first_carry, epi=_swi_bwd_epi,
                     epi_in=(gate, up), epi_out=(BF16, BF16))
        (dg, du), got = dgu if first_units else (dgu, ())
        for u, r in zip(first_units, got):
            from_sib[u] = r
            sum_chip(u)
        piece = [part[u].shape[1] // 2 for u in first_units]
        d_down = matmul(act, dxob, ta=True, out_dtype=BF16, name=f"ffn{l}_dw_down",
                        carry=scatter_chips([part[u] for u in first_units], rows=(0, piece[0])) if first_units else None)
        if first_units:
            d_down, first_rb = d_down
        grads[dn] = rows4(d_down)
        dh, (from_sib[dn],) = matmul(dg, w_gate[l], tb=True, name=f"ffn{l}_dh_gate", carry=exchange(dn))
        sum_chip(dn)
        dh = matmul(du, w_up[l], tb=True, add=dh, name=f"ffn{l}_dh_up",
                    carry=scatter_chips([part[u] for u in first_units], rows=(piece[0], piece[0]), into=first_rb)
                    if first_units else None)
        if first_units:
            dh, (from_chips[first_units[0]],) = dh
        d_gate, (from_chips[dn],) = matmul(h, dg, ta=True, out_dtype=BF16, out_shards=True, name=f"ffn{l}_dw_gate",
                                           carry=scatter(dn))
        grads[gt] = cols4(d_gate)
        d_up, (from_sib[gt],) = matmul(h, du, ta=True, out_dtype=BF16, out_shards=True, name=f"ffn{l}_dw_up",
                                       carry=exchange(gt))
        grads[up_] = cols4(d_up)
        return rms_bwd(xin, a['ffn_norm'][l:l + 1], dh, dxo, name=f"rms_ffn{l}_bwd")

    dx3, dx3b, d_fn1 = ffn_bwd(dx4, dx4b, x3, h3, gate1, up1, act1, 1, None, ())
    dy, (from_sib['up1'],) = matmul(dx3b, w_od_out, tb=True, name="od_out_dy", carry=exchange('up1'))
    sum_chip('gate1')
    sum_chip('up1')
    grads['od_w_out'] = rows4(matmul(y_ssd, dx3b, ta=True, out_dtype=BF16, name="od_out_dw"))
    (dxs, dproj2, ddte, dal, dds, dgn), (from_chips['gate1'], from_chips['up1'], from_sib['od_w_out']) = ssd_bwd(
        xs, proj2, dte, alog_e, dskip_e, od_gn, sp_ssd, dy, proj2.shape,
        carry=merge_carries(scatter('gate1', 'up1'), exchange('od_w_out')))
    sum_chip('od_w_out')
    (dproj2, d_od_cw, d_od_cb), (from_chips['od_w_out'],) = conv_silu_bwd(
        proj2, di, conv_dim, od_cw, od_cb, dxs, dproj2, carry=scatter('od_w_out'))
    dproj2, d_dt_bias = dt_bwd(proj2, dtcol, dt_bias_p, e_mat, ddte, dproj2)
    dh2 = matmul(dproj2, w_od_in_p, tb=True, name="od_in_dh")
    d_od_in = matmul(h2, dproj2, ta=True, out_dtype=BF16, name="od_in_dw")[:, :od_used]
    grads['od_w_in'] = cols4(_uncols(d_od_in))
    dx2, dx2b, d_od_norm = rms_bwd(x2, full['od_norm'], dh2, dx3, name="rms_od_bwd")
    to8 = lambda acc: jnp.zeros((SUBLANE, di), F32).at[0].set(acc.reshape(-1))
    d_a_log = head_expand(to8(dal), e_mat, transpose=True, name="reduce_a_log")[0:1, :nh]
    d_d_skip = head_expand(to8(dds), e_mat, transpose=True, name="reduce_d_skip")[0:1, :nh]

    dx1, dx1b, d_fn0 = ffn_bwd(dx2, dx2b, x1, h1, gate0, up0, act0, 0, exchange('od_w_in'), ('od_w_in',))
    dmix, (from_sib['up0'],) = matmul(dx1b, w_ev_out, tb=True, name="ev_out_dmix", carry=exchange('up0'))
    sum_chip('gate0')
    sum_chip('up0')
    grads['ev_w_out'] = rows4(jnp.concatenate([matmul(o_gla, dx1b, ta=True, out_dtype=BF16, name="ev_out_dw_a"),
                                               matmul(o_lru, dx1b, ta=True, out_dtype=BF16, name="ev_out_dw_b")], axis=0))
    (dproj, d_wg, d_bg, d_wn), (from_chips['gate0'], from_sib['ev_w_out']) = gla_bwd(
        proj, glr_col, wg_p, bg, wn, sp_gla, dmix, dv, gla_w,
        carry=merge_carries(scatter('gate0'), exchange('ev_w_out')))
    sum_chip('ev_w_out')
    (dproj, *d_lru), (from_chips['ev_w_out'],) = lru_bwd(
        proj, xcol, lw, *lru_p, hin, dmix, 1, dproj, carry=scatter('ev_w_out'))
    d_ev_in_p, (from_chips['up0'],) = matmul(h0, dproj, ta=True, out_dtype=BF16, name="ev_in_dw", carry=scatter('up0'))
    d_ev_in = jnp.concatenate([d_ev_in_p[:, :wq + rank], d_ev_in_p[:, gla_w:]], axis=1)
    grads['ev_w_in'] = cols4(_uncols(d_ev_in))
    (from_sib['ev_w_in'],) = run_comm(exchange('ev_w_in'), name="exchange_ev_in")
    sum_chip('ev_w_in')
    dh0, (from_chips['ev_w_in'],) = matmul(dproj, w_ev_in_p, tb=True, name="ev_in_dh", carry=scatter('ev_w_in'))
    dx0, _, d_ev_norm = rms_bwd(x, a['ev_norm'], dh0, dx1, name="rms_ev_bwd")

    units = list(grads)
    half = [mesh_sum(part[u], from_chips[u], chip, ci, name=f"mesh_sum_{u}") for u in units]
    done = dict(zip(units, run_comm(join_halves(half), name="join_halves")))
    layers = {n: [done[n]] for n in ('ev_w_in', 'ev_w_out', 'od_w_in', 'od_w_out')}
    layers.update({f'ffn_w_{u}': [done[f'{u}0'], done[f'{u}1']] for u in ('gate', 'up', 'down')})
    grad, delta, new_m, new_v = {}, {}, {}, {}
    for n in BIG:
        flip = a[n].shape[-1] % LANE != 0
        tr_ = (lambda t: jnp.swapaxes(t, -1, -2)) if flip else (lambda t: t)
        gs = [tr_(g.reshape(a[n].shape[-2:])) for g in layers[n]]
        outs = adamw(tr_(a[n]), gs, tr_(a['m_' + n]), tr_(a['v_' + n]), name=f"adamw_{n}")
        grad[n], delta[n], new_m[n], new_v[n] = (tr_(o) for o in outs)

    small_g = {
        'ev_norm': d_ev_norm, 'ev_gla_w_gate': d_wg[:rank][None], 'ev_gla_b_gate': d_bg, 'ev_gla_w_onorm': d_wn,
        'ev_lru_conv_w': d_lru[0][None], 'ev_lru_conv_b': d_lru[1], 'ev_lru_w_a': d_lru[2][None],
        'ev_lru_b_a': d_lru[3], 'ev_lru_w_i': d_lru[4][None], 'ev_lru_b_i': d_lru[5], 'ev_lru_lam': d_lru[6],
        'od_norm': d_od_norm, 'od_conv_w': d_od_cw[None], 'od_conv_b': d_od_cb, 'od_dt_bias': d_dt_bias[:, :nh],
        'od_a_log': d_a_log, 'od_d_skip': d_d_skip, 'od_gnorm': dgn.reshape(1, di),
        'ffn_norm': jnp.concatenate([d_fn0, d_fn1], axis=0), 'final_norm': d_final[0],
    }
    full_shapes = [small_g[n].shape for n in SMALL]
    summed = sum_leading(gather8(_pack([small_g[n] for n in SMALL])), name="sum_devices")
    for n, g in zip(SMALL, _unpack(summed, full_shapes)):
        if n in SMALL_SHARDED:
            ax = SMALL_SHARDED[n]
            sz = a[n].shape[ax]
            g = lax.dynamic_slice_in_dim(g, chip * sz, sz, axis=ax)
        grad[n] = g

    shapes = [a[n].shape for n in SMALL]
    packed = [_pack([src[n] if pre is None else a[pre + n] for n in SMALL])
              for src, pre in ((a, None), (grad, None), (None, 'm_'), (None, 'v_'))]
    small_out = adamw(packed[0], [packed[1]], packed[2], packed[3], name="adamw_small")
    for outd, blk in zip((delta, new_m, new_v), small_out[1:]):
        outd.update(zip(SMALL, _unpack(blk, shapes)))

    loss = lax.psum(loss_p[0, 0], ("x", "y", "c"))
    return (loss, dx0[None], *[grad[n] for n in WEIGHTS], *[delta[n] for n in WEIGHTS],
            *[new_m[n] for n in WEIGHTS], *[new_v[n] for n in WEIGHTS])
```

```python
import functools
import math

import jax
import jax.numpy as jnp
from jax import lax
from jax.experimental import pallas as pl
from jax.experimental.pallas import tpu as pltpu

F32 = jnp.float32
BF16 = jnp.bfloat16
MXU_DTYPE = jnp.bfloat16

NORM_EPS = 1e-6
CONV_WIDTH = 4
GLA_HEADS = 4
GLA_GATE_RANK = 16
GLA_GATE_NORM = 16.0
CHUNK = 64
LRU_BLOCK = 128
LRU_C = 8.0
SSD_HEAD_DIM = 64
SSD_GROUPS = 8
SSD_STATE = 128
ADAM_LR, ADAM_B1, ADAM_B2, ADAM_EPS, ADAM_WD, ADAM_STEP = 0.001, 0.9, 0.999, 1e-08, 0.01, 10

LANE = 128
SUBLANE = 8
VMEM_LIMIT = 56 * 1024 * 1024
MAX_TK = 2816
MATMUL_VMEM_BUDGET = 53 * 1024 * 1024
MESH = pl.DeviceIdType.MESH

WEIGHTS = ['ev_norm', 'ev_w_in', 'ev_gla_w_gate', 'ev_gla_b_gate', 'ev_gla_w_onorm', 'ev_lru_conv_w', 'ev_lru_conv_b',
           'ev_lru_w_a', 'ev_lru_b_a', 'ev_lru_w_i', 'ev_lru_b_i', 'ev_lru_lam', 'ev_w_out', 'od_norm', 'od_w_in',
           'od_conv_w', 'od_conv_b', 'od_dt_bias', 'od_a_log', 'od_d_skip', 'od_gnorm', 'od_w_out', 'ffn_norm',
           'ffn_w_gate', 'ffn_w_up', 'ffn_w_down', 'final_norm']
BIG = ['ev_w_in', 'ev_w_out', 'od_w_in', 'od_w_out', 'ffn_w_gate', 'ffn_w_up', 'ffn_w_down']
SMALL_SHARDED = {'ev_gla_w_gate': 2, 'ev_lru_conv_w': 2, 'od_norm': 1, 'od_conv_w': 2, 'od_conv_b': 1, 'od_gnorm': 1}
SMALL = [n for n in WEIGHTS if n not in BIG]


def _cparams(sem=None, **kw):
    return pltpu.CompilerParams(dimension_semantics=sem, vmem_limit_bytes=VMEM_LIMIT, **kw)


def _full(shape):
    n = len(shape)
    return pl.BlockSpec(shape, lambda *_: (0,) * n)


ANY = pl.BlockSpec(memory_space=pl.ANY)


class Carry:
    def __init__(self, arrays, out_shape, n_sems, start, finish, aliases=None):
        self.arrays, self.out_shape, self.n_sems = list(arrays), list(out_shape), n_sems
        self.start, self.finish, self.aliases = start, finish, dict(aliases or {})


def merge_carries(*cs):
    cs = [c for c in cs if c is not None]
    if not cs:
        return None
    arrays = [a for c in cs for a in c.arrays]
    out_shape = [s for c in cs for s in c.out_shape]
    offs, i0, o0, s0 = [], 0, 0, 0
    aliases = {}
    for c in cs:
        offs.append((i0, o0, s0))
        aliases.update({i0 + i: o0 + o for i, o in c.aliases.items()})
        i0, o0, s0 = i0 + len(c.arrays), o0 + len(c.out_shape), s0 + c.n_sems

    def both(which):
        def run(ins, outs, send, recv, base):
            for c, (i, o, s) in zip(cs, offs):
                getattr(c, which)(ins[i:i + len(c.arrays)], outs[o:o + len(c.out_shape)], send, recv, base + s)
        return run

    return Carry(arrays, out_shape, s0, both("start"), both("finish"), aliases)


def _pcall(body, *, name, grid, in_specs, out_specs, out_shape, scratch_shapes=(), compiler_params, carry=None,
           input_output_aliases=None):
    aliases = dict(input_output_aliases or {})
    if carry is None:
        return pl.pallas_call(body, name=name, grid=grid, in_specs=in_specs, out_specs=out_specs, out_shape=out_shape,
                              scratch_shapes=list(scratch_shapes), compiler_params=compiler_params,
                              input_output_aliases=aliases)
    single = not isinstance(out_specs, (list, tuple))
    specs_o = [out_specs] if single else list(out_specs)
    shapes_o = [out_shape] if single else list(out_shape)
    n_in, n_out, k_in, k_out, n_scr = len(in_specs), len(specs_o), len(carry.arrays), len(carry.out_shape), len(scratch_shapes)

    def wrapped(*refs):
        ins, cins = refs[:n_in], refs[n_in:n_in + k_in]
        o0 = n_in + k_in
        outs, couts = refs[o0:o0 + n_out], refs[o0 + n_out:o0 + n_out + k_out]
        scr = refs[o0 + n_out + k_out:o0 + n_out + k_out + n_scr]
        send, recv = refs[-2:]
        ids = [pl.program_id(ax) for ax in range(len(grid))]
        first = functools.reduce(jnp.logical_and, [i == 0 for i in ids])
        last = functools.reduce(jnp.logical_and, [i == g - 1 for i, g in zip(ids, grid)])

        @pl.when(first)
        def _():
            carry.start(cins, couts, send, recv, 0)

        body(*ins, *outs, *scr)

        @pl.when(last)
        def _():
            carry.finish(cins, couts, send, recv, 0)

    aliases.update({n_in + i: n_out + o for i, o in carry.aliases.items()})
    call = pl.pallas_call(
        wrapped, name=name, grid=grid, in_specs=list(in_specs) + [ANY] * k_in, out_specs=specs_o + [ANY] * k_out,
        out_shape=shapes_o + carry.out_shape,
        scratch_shapes=list(scratch_shapes) + [pltpu.SemaphoreType.DMA((carry.n_sems,))] * 2,
        compiler_params=_cparams(("arbitrary",) * len(grid)), input_output_aliases=aliases)

    def run(*args):
        res = call(*args, *carry.arrays)
        main = res[:n_out]
        return (main[0] if single else list(main)), list(res[n_out:])

    return run


def _pick(dim, cands):
    for c in cands:
        if dim % c == 0:
            return c
    return dim


def _dims(a, ca, cb):
    nb = a.ndim - 2
    return (((ca + nb,), (cb + nb,)), (tuple(range(nb)), tuple(range(nb))))


def _dot(a, b, ca, cb):
    return lax.dot_general(a.astype(MXU_DTYPE), b.astype(MXU_DTYPE), _dims(a, ca, cb), preferred_element_type=F32)


@jax.custom_vjp
def mm(a, b):
    return _dot(a, b, 1, 0)


def _mm_f(a, b):
    return mm(a, b), (a, b)


def _mm_b(res, g):
    a, b = res
    return mm_nt(g, b).astype(a.dtype), mm_tn(a, g).astype(b.dtype)


@jax.custom_vjp
def mm_nt(a, b):
    return _dot(a, b, 1, 1)


def _mm_nt_f(a, b):
    return mm_nt(a, b), (a, b)


def _mm_nt_b(res, g):
    a, b = res
    return mm(g, b).astype(a.dtype), mm_tn(g, a).astype(b.dtype)


@jax.custom_vjp
def mm_tn(a, b):
    return _dot(a, b, 0, 0)


def _mm_tn_f(a, b):
    return mm_tn(a, b), (a, b)


def _mm_tn_b(res, g):
    a, b = res
    return mm_nt(b, g).astype(a.dtype), mm(a, g).astype(b.dtype)


mm.defvjp(_mm_f, _mm_b)
mm_nt.defvjp(_mm_nt_f, _mm_nt_b)
mm_tn.defvjp(_mm_tn_f, _mm_tn_b)


def _split3(a):
    h = a.astype(BF16)
    r = a - h.astype(F32)
    m = r.astype(BF16)
    l = (r - m.astype(F32)).astype(BF16)
    return h, m, l


def _exact_dot(t, a, ca, cb):
    out = None
    for p in _split3(a):
        d = lax.dot_general(t, p, _dims(a, ca, cb), preferred_element_type=F32)
        out = d if out is None else out + d
    return out


@jax.custom_vjp
def sel_l(t, a):
    return _exact_dot(t, a, 1, 0)


def _sel_l_f(t, a):
    return sel_l(t, a), t


def _sel_l_b(t, g):
    return jnp.zeros_like(t), _exact_dot(t, g, 0, 0)


sel_l.defvjp(_sel_l_f, _sel_l_b)


@jax.custom_vjp
def sel_r(a, t):
    out = None
    for p in _split3(a):
        d = lax.dot_general(p, t, (((1,), (0,)), ((), ())), preferred_element_type=F32)
        out = d if out is None else out + d
    return out


def _sel_r_f(a, t):
    return sel_r(a, t), t


def _sel_r_b(t, g):
    out = None
    for p in _split3(g):
        d = lax.dot_general(p, t, (((1,), (1,)), ((), ())), preferred_element_type=F32)
        out = d if out is None else out + d
    return out, jnp.zeros_like(t)


sel_r.defvjp(_sel_r_f, _sel_r_b)


def _sigmoid(x):
    return 1.0 / (1.0 + jnp.exp(-x))


def _silu(x):
    return x * _sigmoid(x)


def _softplus(x):
    return jnp.maximum(x, 0.0) + jnp.log(1.0 + jnp.exp(-jnp.abs(x)))


def _log_sigmoid(x):
    return -_softplus(-x)


def _gelu_tanh(x):
    c = math.sqrt(2.0 / math.pi)
    return 0.5 * x * (1.0 + jnp.tanh(c * (x + 0.044715 * (x * x * x))))


def _rms(x, w):
    return x * lax.rsqrt(jnp.mean(x * x, axis=-1, keepdims=True) + NORM_EPS) * w


def _tri(n, dtype=BF16):
    r = lax.broadcasted_iota(jnp.int32, (n, n), 0)
    c = lax.broadcasted_iota(jnp.int32, (n, n), 1)
    return (c <= r).astype(dtype)


def matmul(a, b, *, ta=False, tb=False, add=None, out_dtype=F32, out_shards=False, carry=None, name,
           epi=None, epi_in=(), epi_out=()):
    m, k = (a.shape[1], a.shape[0]) if ta else a.shape
    b_sh = b.ndim == 3
    if b_sh:
        s, br, bc = b.shape
        k2, n = (s * bc, br) if tb else (br, s * bc)
    else:
        k2, n = (b.shape[1], b.shape[0]) if tb else b.shape
    assert k == k2, (a.shape, b.shape, ta, tb)
    tk_opts = [bc] if b_sh and tb else [k] if k <= MAX_TK else \
        [c for c in range(MAX_TK, LANE - 1, -LANE) if k % c == 0][:1]
    n_add, n_x = int(add is not None), len(epi_in)
    out_dtypes = list(epi_out) if epi is not None else [out_dtype]
    n_o = len(out_dtypes)
    tn_opts = [bc] if b_sh and not tb else [n // 4] if out_shards else \
        [c for c in range(2048, LANE - 1, -LANE) if n % c == 0] or [n]
    tm_opts = [c for c in range(2048, LANE - 1, -LANE) if m % c == 0] or [m]
    sa, sb = a.dtype.itemsize, b.dtype.itemsize
    per_elem = sum(jnp.dtype(dt).itemsize for dt in out_dtypes) + 4 * n_add + sum(x.dtype.itemsize for x in epi_in)
    best = None
    for tk_ in tk_opts:
        for tm_ in tm_opts:
            for tn_ in tn_opts:
                vmem = 2 * (tm_ * tk_ * sa + tk_ * tn_ * sb) + tm_ * tn_ * (4 + 2 * per_elem)
                vmem += tm_ * tn_ * 4
                if vmem > MATMUL_VMEM_BUDGET and (tk_, tm_, tn_) != (tk_opts[-1], tm_opts[-1], tn_opts[-1]):
                    continue
                moved = m * k * sa * (1 if tk_ == k else n // tn_) + k * n * sb * (m // tm_)
                if best is None or (moved, -tk_, -tm_ * tn_) < best[0]:
                    best = ((moved, -tk_, -tm_ * tn_), tm_, tn_, tk_)
    _, tm, tn, tk = best
    nk = k // tk

    def body(*refs):
        a_ref, b_ref = refs[:2]
        x_refs = refs[2 + n_add:2 + n_add + n_x]
        o_refs = refs[2 + n_add + n_x:2 + n_add + n_x + n_o]
        acc = refs[-1]
        kk = pl.program_id(2)

        @pl.when(kk == 0)
        def _():
            acc[...] = jnp.zeros_like(acc)

        acc[...] += _dot(a_ref[...], b_ref[...], 0 if ta else 1, 1 if tb else 0)

        @pl.when(kk == nk - 1)
        def _():
            r = acc[...]
            if add is not None:
                r = r + refs[2][...].astype(F32)
            vals = (r,) if epi is None else epi(r, *[x[...] for x in x_refs])
            for o_ref, v in zip(o_refs, vals):
                o_ref[...] = v.astype(o_ref.dtype)

    a_spec = pl.BlockSpec((tk, tm), lambda i, j, kk: (kk, i)) if ta else pl.BlockSpec((tm, tk), lambda i, j, kk: (i, kk))
    if b_sh and tb:
        b_spec = pl.BlockSpec((None, tn, tk), lambda i, j, kk: (kk, j, 0))
    elif b_sh:
        b_spec = pl.BlockSpec((None, tk, tn), lambda i, j, kk: (j, kk, 0))
    elif tb:
        b_spec = pl.BlockSpec((tn, tk), lambda i, j, kk: (j, kk))
    else:
        b_spec = pl.BlockSpec((tk, tn), lambda i, j, kk: (kk, j))
    in_specs, args = [a_spec, b_spec], [a, b]
    tile = pl.BlockSpec((tm, tn), lambda i, j, kk: (i, j))
    for extra in ([add] if add is not None else []) + list(epi_in):
        in_specs.append(tile)
        args.append(extra)
    if out_shards:
        out_spec = pl.BlockSpec((None, tm, tn), lambda i, j, kk: (j, i, 0))
        out_shape = jax.ShapeDtypeStruct((4, m, tn), out_dtype)
    elif epi is not None:
        out_spec = [tile] * n_o
        out_shape = [jax.ShapeDtypeStruct((m, n), dt) for dt in out_dtypes]
    else:
        out_spec = tile
        out_shape = jax.ShapeDtypeStruct((m, n), out_dtype)
    return _pcall(
        body, name=name, grid=(m // tm, n // tn, nk), in_specs=in_specs, out_specs=out_spec, out_shape=out_shape,
        scratch_shapes=[pltpu.VMEM((tm, tn), F32)],
        compiler_params=_cparams(("parallel", "parallel", "arbitrary")), carry=carry,
    )(*args)


def rms_fwd(x, w, *, name):
    t, d = x.shape
    tb = _pick(t, (256, 128, 64))

    def body(x_ref, w_ref, o_ref):
        o_ref[...] = _rms(x_ref[...], w_ref[...]).astype(o_ref.dtype)

    return pl.pallas_call(
        body, name=name, grid=(t // tb,),
        in_specs=[pl.BlockSpec((tb, d), lambda i: (i, 0)), _full((1, d))],
        out_specs=pl.BlockSpec((tb, d), lambda i: (i, 0)),
        out_shape=jax.ShapeDtypeStruct((t, d), BF16),
        compiler_params=_cparams(("parallel",)),
    )(x, w)


def rms_bwd(x, w, dh, dres, *, name, carry=None):
    t, d = x.shape
    tb = _pick(t, (256, 128, 64))

    def body(x_ref, w_ref, dh_ref, dres_ref, dx_ref, dxb_ref, dw_ref):
        @pl.when(pl.program_id(0) == 0)
        def _():
            dw_ref[...] = jnp.zeros_like(dw_ref)

        _, vjp = jax.vjp(_rms, x_ref[...], w_ref[...])
        dx, dw = vjp(dh_ref[...].astype(F32))
        dx = dx + dres_ref[...]
        dx_ref[...] = dx
        dxb_ref[...] = dx.astype(dxb_ref.dtype)
        dw_ref[...] += dw

    row = pl.BlockSpec((tb, d), lambda i: (i, 0))
    return _pcall(
        body, name=name, grid=(t // tb,), carry=carry,
        in_specs=[row, _full((1, d)), row, row],
        out_specs=[row, row, _full((1, d))],
        out_shape=[jax.ShapeDtypeStruct((t, d), F32), jax.ShapeDtypeStruct((t, d), BF16),
                   jax.ShapeDtypeStruct((1, d), F32)],
        compiler_params=_cparams(("arbitrary",)),
    )(x, w, dh, dres)


def _swi(g, u):
    return _silu(g) * u


def _swi_fwd_epi(u, g):
    return u, _swi(g, u)


def _swi_bwd_epi(d, g, u):
    return jax.vjp(_swi, g, u)[1](d)


def loss_head(x, w, target, *, name):
    t, d = x.shape
    tb = _pick(t, (256, 128, 64))

    def f(xv, wv, tv):
        y = _rms(xv, wv)
        e = y - tv
        return 0.5 * jnp.sum(jnp.mean(e * e, axis=-1, keepdims=True), axis=0, keepdims=True)

    def body(x_ref, w_ref, t_ref, l_ref, dx_ref, dxb_ref, dw_ref):
        @pl.when(pl.program_id(0) == 0)
        def _():
            l_ref[...] = jnp.zeros_like(l_ref)
            dw_ref[...] = jnp.zeros_like(dw_ref)

        val, vjp = jax.vjp(lambda a, b: f(a, b, t_ref[...]), x_ref[...], w_ref[...])
        dx, dw = vjp(jnp.ones((1, 1), F32))
        l_ref[...] += jnp.broadcast_to(val, l_ref.shape)
        dx_ref[...] = dx
        dxb_ref[...] = dx.astype(dxb_ref.dtype)
        dw_ref[...] += dw

    row = pl.BlockSpec((tb, d), lambda i: (i, 0))
    return pl.pallas_call(
        body, name=name, grid=(t // tb,),
        in_specs=[row, _full((1, d)), row],
        out_specs=[_full((SUBLANE, LANE)), row, row, _full((1, d))],
        out_shape=[jax.ShapeDtypeStruct((SUBLANE, LANE), F32), jax.ShapeDtypeStruct((t, d), F32),
                   jax.ShapeDtypeStruct((t, d), BF16),
                   jax.ShapeDtypeStruct((1, d), F32)],
        compiler_params=_cparams(("arbitrary",)),
    )(x, w, target)


def _gla_chunk(q, k, v, g, glr, st, wg, bg, wn, tri):
    L, hk = q.shape[-2:]
    la = _log_sigmoid(mm(glr, wg) + bg) / GLA_GATE_NORM
    bcum = sel_l(jnp.broadcast_to(tri, la.shape[:-2] + tri.shape), la)
    b_last = jnp.sum(la, axis=-2, keepdims=True)
    rows = lax.broadcasted_iota(jnp.int32, (L, 1), 0)
    b_mid = jnp.sum(jnp.where(rows <= L // 2, la, 0.0), axis=-2, keepdims=True)
    qs = q * (hk ** -0.5)
    q_in = qs * jnp.exp(bcum - b_mid)
    k_in = k * jnp.exp(b_mid - bcum)
    scores = mm_nt(q_in, k_in) * tri.astype(F32)
    o_intra = mm(scores, v)
    k_st = k * jnp.exp(b_last - bcum)
    d_st = mm_tn(v, k_st)
    o_inter = mm_nt(qs * jnp.exp(bcum), st)
    st_new = jnp.exp(b_last) * st + d_st
    o = _rms(o_intra + o_inter, wn) * _silu(g)
    return o, st_new


def _heads(ref, start, width, n):
    return jnp.stack([ref[:, start + h * width:start + (h + 1) * width] for h in range(n)], axis=0)


def _gla_heads(p_ref, dk, dv, n):
    hk, hv = dk // n, dv // n
    return (_heads(p_ref, 0, hk, n), _heads(p_ref, dk, hk, n), _heads(p_ref, 2 * dk, hv, n),
            _heads(p_ref, 2 * dk + dv, hv, n))


def _gla_dims(d):
    dv = d // 2
    dk = dv // 2
    return dk, dv, dk // GLA_HEADS, dv // GLA_HEADS


def gla_fwd(proj, glr_col, wg, bg, wn, dv, carry=None):
    t = proj.shape[0]
    dk, dv, hk, hv = _gla_dims(2 * dv)
    L, H = CHUNK, GLA_HEADS
    nc = t // L
    wq = 2 * dk + 2 * dv

    def body(p_ref, glr_ref, wg_ref, bg_ref, wn_ref, o_ref, sp_ref, st):
        @pl.when(pl.program_id(0) == 0)
        def _():
            st[...] = jnp.zeros_like(st)

        s_prev = st[...]
        sp_ref[0] = s_prev
        o, s_new = _gla_chunk(*_gla_heads(p_ref, dk, dv, H), jnp.broadcast_to(glr_ref[...], (H, L, LANE)), s_prev,
                              _heads(wg_ref, 0, hk, H), _heads(bg_ref, 0, hk, H), wn_ref[...], _tri(L))
        for h in range(H):
            o_ref[:, h * hv:(h + 1) * hv] = o[h].astype(o_ref.dtype)
        st[...] = s_new

    return _pcall(
        body, carry=carry, name="gla_fwd", grid=(nc,),
        in_specs=[pl.BlockSpec((L, wq), lambda c: (c, 0)), pl.BlockSpec((L, LANE), lambda c: (c, glr_col)),
                  _full(wg.shape), _full(bg.shape), _full(wn.shape)],
        out_specs=[pl.BlockSpec((L, dv), lambda c: (c, 0)), pl.BlockSpec((1, H, hv, hk), lambda c: (c, 0, 0, 0))],
        out_shape=[jax.ShapeDtypeStruct((t, dv), BF16), jax.ShapeDtypeStruct((nc, H, hv, hk), F32)],
        scratch_shapes=[pltpu.VMEM((H, hv, hk), F32)],
        compiler_params=_cparams(("arbitrary",)),
    )(proj, proj, wg, bg, wn)


def gla_bwd(proj, glr_col, wg, bg, wn, sprev, do, dv, gla_w, carry=None):
    t = proj.shape[0]
    dk, _, hk, hv = _gla_dims(2 * dv)
    L, H = CHUNK, GLA_HEADS
    nc = t // L
    wq = 2 * dk + 2 * dv

    def body(p_ref, glr_ref, wg_ref, bg_ref, wn_ref, sp_ref, do_ref, dp_ref, dwg_ref, dbg_ref, dwn_ref, dst):
        @pl.when(pl.program_id(0) == 0)
        def _():
            dst[...] = jnp.zeros_like(dst)
            dwg_ref[...] = jnp.zeros_like(dwg_ref)
            dbg_ref[...] = jnp.zeros_like(dbg_ref)
            dwn_ref[...] = jnp.zeros_like(dwn_ref)

        f = functools.partial(_gla_chunk, tri=_tri(L))
        _, vjp = jax.vjp(f, *_gla_heads(p_ref, dk, dv, H), jnp.broadcast_to(glr_ref[...], (H, L, LANE)), sp_ref[0],
                         _heads(wg_ref, 0, hk, H), _heads(bg_ref, 0, hk, H), wn_ref[...])
        dq, dkk, dvv, dg, dgl, ds, dwg, dbg, dwn = vjp((_heads(do_ref, 0, hv, H), dst[...]))
        for h in range(H):
            dp_ref[:, h * hk:(h + 1) * hk] = dq[h].astype(dp_ref.dtype)
            dp_ref[:, dk + h * hk:dk + (h + 1) * hk] = dkk[h].astype(dp_ref.dtype)
            dp_ref[:, 2 * dk + h * hv:2 * dk + (h + 1) * hv] = dvv[h].astype(dp_ref.dtype)
            dp_ref[:, 2 * dk + dv + h * hv:2 * dk + dv + (h + 1) * hv] = dg[h].astype(dp_ref.dtype)
            dwg_ref[:, h * hk:(h + 1) * hk] += dwg[h]
            dbg_ref[:, h * hk:(h + 1) * hk] += dbg[h]
        dst[...] = ds
        dwn_ref[...] += dwn
        dp_ref[:, wq:wq + LANE] = jnp.sum(dgl, axis=0).astype(dp_ref.dtype)
        if gla_w > wq + LANE:
            dp_ref[:, wq + LANE:] = jnp.zeros((L, gla_w - wq - LANE), dp_ref.dtype)

    rev = lambda c: nc - 1 - c
    return _pcall(
        body, carry=carry, name="gla_bwd", grid=(nc,),
        in_specs=[pl.BlockSpec((L, wq), lambda c: (rev(c), 0)), pl.BlockSpec((L, LANE), lambda c: (rev(c), glr_col)),
                  _full(wg.shape), _full(bg.shape), _full(wn.shape),
                  pl.BlockSpec((1, H, hv, hk), lambda c: (rev(c), 0, 0, 0)),
                  pl.BlockSpec((L, dv), lambda c: (rev(c), 0))],
        out_specs=[pl.BlockSpec((L, gla_w), lambda c: (rev(c), 0)),
                   _full(wg.shape), _full(bg.shape), _full(wn.shape)],
        out_shape=[jax.ShapeDtypeStruct(proj.shape, BF16),
                   jax.ShapeDtypeStruct(wg.shape, F32), jax.ShapeDtypeStruct(bg.shape, F32),
                   jax.ShapeDtypeStruct(wn.shape, F32)],
        scratch_shapes=[pltpu.VMEM((H, hv, hk), F32)],
        compiler_params=_cparams(("arbitrary",)),
    )(proj, proj, wg, bg, wn, sprev, do)


def _shift_down(x, tail, s):
    if s == 0:
        return x
    r = pltpu.roll(x, s, 0)
    rows = lax.broadcasted_iota(jnp.int32, tail.shape, 0)
    top = jnp.where(rows < s, pltpu.roll(tail, s, 0), r[:SUBLANE])
    return jnp.concatenate([top, r[SUBLANE:]], axis=0)


def _shift_up(x, head, s):
    if s == 0:
        return x
    n = x.shape[0]
    r = pltpu.roll(x, n - s, 0)
    rows = lax.broadcasted_iota(jnp.int32, head.shape, 0)
    bottom = jnp.where(rows >= SUBLANE - s, pltpu.roll(head, SUBLANE - s, 0), r[n - SUBLANE:])
    return jnp.concatenate([r[:n - SUBLANE], bottom], axis=0)


def _conv(x, prev, w, b):
    y = b
    for k in range(CONV_WIDTH):
        y = y + w[k:k + 1, :] * _shift_down(x, prev, CONV_WIDTH - 1 - k)
    return y


def _conv_bwd(dy, nxt, x, prev, w):
    dx = None
    dws = []
    for k in range(CONV_WIDTH):
        s = CONV_WIDTH - 1 - k
        term = w[k:k + 1, :] * _shift_up(dy, nxt, s)
        dx = term if dx is None else dx + term
        dws.append(jnp.sum(dy * _shift_down(x, prev, s), axis=0, keepdims=True))
    return dx, jnp.concatenate(dws, axis=0), jnp.sum(dy, axis=0, keepdims=True)


def _scan_fwd(a, u):
    n = a.shape[0]
    rows = lax.broadcasted_iota(jnp.int32, a.shape, 0)
    s = 1
    while s < n:
        a_sh = jnp.where(rows < s, 1.0, pltpu.roll(a, s, 0))
        u_sh = jnp.where(rows < s, 0.0, pltpu.roll(u, s, 0))
        u = a * u_sh + u
        a = a * a_sh
        s *= 2
    return a, u


def _scan_rev(c, d):
    n = c.shape[0]
    rows = lax.broadcasted_iota(jnp.int32, c.shape, 0)
    s = 1
    while s < n:
        c_sh = jnp.where(rows >= n - s, 0.0, pltpu.roll(c, n - s, 0))
        d_sh = jnp.where(rows >= n - s, 0.0, pltpu.roll(d, n - s, 0))
        d = d + c * d_sh
        c = c * c_sh
        s *= 2
    return d


def _expm1(x):
    small = x * (1.0 + x * (0.5 + x * (1.0 / 6.0 + x * (1.0 / 24.0))))
    return jnp.where(jnp.abs(x) < 1e-2, small, jnp.exp(x) - 1.0)


def _lru_gates(xc, pa, pi, lam):
    r = _sigmoid(pa)
    i = _sigmoid(pi)
    log_a = LRU_C * r * _log_sigmoid(lam)
    a = jnp.exp(log_a)
    u = jnp.sqrt(-_expm1(2.0 * log_a)) * (i * xc)
    return a, u


def _lru_out(h, gate):
    return h * _gelu_tanh(gate)


def _blockdiag(xc, w_ref, b):
    nb = w_ref.shape[0]
    outs = [mm(xc[:, n * LRU_BLOCK:(n + 1) * LRU_BLOCK], w_ref[n]) for n in range(nb)]
    return jnp.concatenate(outs, axis=1) + b


def lru_fwd(proj, xcol, lw, cw, cb, wa, ba, wi, bi, lam):
    t = proj.shape[0]
    tb = _pick(t, (256, 128, 64))
    nb = t // tb

    def body(x_ref, xp_ref, g_ref, cw_ref, cb_ref, wa_ref, ba_ref, wi_ref, bi_ref, lam_ref, o_ref, hin_ref, hc):
        i = pl.program_id(0)

        @pl.when(i == 0)
        def _():
            hc[...] = jnp.zeros_like(hc)

        prev = jnp.where(i == 0, 0.0, xp_ref[...])
        xc = _conv(x_ref[...], prev, cw_ref[...], cb_ref[...])
        a, u = _lru_gates(xc, _blockdiag(xc, wa_ref, ba_ref[...]), _blockdiag(xc, wi_ref, bi_ref[...]), lam_ref[...])
        acum, h0 = _scan_fwd(a, u)
        h = h0 + acum * hc[...]
        hin_ref[0] = hc[...]
        hc[...] = h[tb - 1:tb, :]
        o_ref[...] = _lru_out(h, g_ref[...]).astype(o_ref.dtype)

    row = lambda col: pl.BlockSpec((tb, lw), lambda i: (i, col))
    return pl.pallas_call(
        body, name="lru_fwd", grid=(nb,),
        in_specs=[row(xcol), pl.BlockSpec((SUBLANE, lw), lambda i: (jnp.maximum(i * (tb // SUBLANE) - 1, 0), xcol)),
                  row(xcol + 1),
                  _full(cw.shape), _full(cb.shape), _full(wa.shape), _full(ba.shape), _full(wi.shape), _full(bi.shape),
                  _full(lam.shape)],
        out_specs=[pl.BlockSpec((tb, lw), lambda i: (i, 0)), pl.BlockSpec((1, 1, lw), lambda i: (i, 0, 0))],
        out_shape=[jax.ShapeDtypeStruct((t, lw), BF16), jax.ShapeDtypeStruct((nb, 1, lw), F32)],
        scratch_shapes=[pltpu.VMEM((1, lw), F32)],
        compiler_params=_cparams(("arbitrary",)),
    )(proj, proj, proj, cw, cb, wa, ba, wi, bi, lam)


def lru_bwd(proj, xcol, lw, cw, cb, wa, ba, wi, bi, lam, hin, dmix, docol, dproj, carry=None):
    t = proj.shape[0]
    tb = _pick(t, (256, 128, 64))
    nb = t // tb
    nblk = wa.shape[0]
    assert xcol % 2 == 0

    def body(x_ref, xp_ref, g_ref, cw_ref, cb_ref, wa_ref, ba_ref, wi_ref, bi_ref, lam_ref, hin_ref, do_ref, _,
             dxg_ref, dcw_ref, dcb_ref, dwa_ref, dba_ref, dwi_ref, dbi_ref, dlam_ref, gc, dxcn):
        pid = pl.program_id(0)
        i = nb - 1 - pid

        @pl.when(pid == 0)
        def _():
            gc[...] = jnp.zeros_like(gc)
            dxcn[...] = jnp.zeros_like(dxcn)
            for r in (dcw_ref, dcb_ref, dwa_ref, dba_ref, dwi_ref, dbi_ref, dlam_ref):
                r[...] = jnp.zeros_like(r)

        x = x_ref[...]
        prev = jnp.where(i == 0, 0.0, xp_ref[...])
        cw_v = cw_ref[...]
        xc = _conv(x, prev, cw_v, cb_ref[...])
        pa = _blockdiag(xc, wa_ref, ba_ref[...])
        pi = _blockdiag(xc, wi_ref, bi_ref[...])
        (a, u), vjp_g = jax.vjp(_lru_gates, xc, pa, pi, lam_ref[...])
        acum, h0 = _scan_fwd(a, u)
        hi = hin_ref[0]
        h = h0 + acum * hi
        rows = lax.broadcasted_iota(jnp.int32, h.shape, 0)
        hprev = jnp.where(rows < 1, hi, pltpu.roll(h, 1, 0))
        _, vjp_o = jax.vjp(_lru_out, h, g_ref[...])
        dh, dgate = vjp_o(do_ref[...].astype(F32))
        c = jnp.where(rows >= tb - 1, 0.0, pltpu.roll(a, tb - 1, 0))
        g = _scan_rev(c, dh + jnp.where(rows == tb - 1, gc[...], 0.0))
        gc[...] = a[0:1, :] * g[0:1, :]
        dxc, dpa, dpi, dlam = vjp_g((g * hprev, g))
        dlam_ref[...] += dlam
        dba_ref[...] += jnp.sum(dpa, axis=0, keepdims=True)
        dbi_ref[...] += jnp.sum(dpi, axis=0, keepdims=True)
        parts = []
        for n in range(nblk):
            sl = slice(n * LRU_BLOCK, (n + 1) * LRU_BLOCK)
            dwa_ref[n] += mm_tn(xc[:, sl], dpa[:, sl])
            dwi_ref[n] += mm_tn(xc[:, sl], dpi[:, sl])
            parts.append(mm_nt(dpa[:, sl], wa_ref[n]) + mm_nt(dpi[:, sl], wi_ref[n]))
        dxc = dxc + jnp.concatenate(parts, axis=1)
        dx, dcw, dcb = _conv_bwd(dxc, dxcn[...], x, prev, cw_v)
        dxcn[...] = dxc[:SUBLANE]
        dcw_ref[...] += dcw
        dcb_ref[...] += dcb
        dxg_ref[:, :lw] = dx.astype(dxg_ref.dtype)
        dxg_ref[:, lw:] = dgate.astype(dxg_ref.dtype)

    row = lambda col: pl.BlockSpec((tb, lw), lambda p: (nb - 1 - p, col))
    params = [cw, cb, wa, ba, wi, bi, lam]
    return _pcall(
        body, carry=carry, name="lru_bwd", grid=(nb,),
        in_specs=[row(xcol),
                  pl.BlockSpec((SUBLANE, lw), lambda p: (jnp.maximum((nb - 1 - p) * (tb // SUBLANE) - 1, 0), xcol)),
                  row(xcol + 1)]
        + [_full(p.shape) for p in params]
        + [pl.BlockSpec((1, 1, lw), lambda p: (nb - 1 - p, 0, 0)), row(docol), ANY],
        out_specs=[pl.BlockSpec((tb, 2 * lw), lambda p: (nb - 1 - p, xcol // 2))] + [_full(p.shape) for p in params],
        out_shape=[jax.ShapeDtypeStruct(dproj.shape, dproj.dtype)]
        + [jax.ShapeDtypeStruct(p.shape, F32) for p in params],
        input_output_aliases={12: 0},
        scratch_shapes=[pltpu.VMEM((1, lw), F32), pltpu.VMEM((SUBLANE, lw), F32)],
        compiler_params=_cparams(("arbitrary",)),
    )(proj, proj, proj, *params, hin, dmix, dproj)


def conv_silu_fwd(proj, col0, width, cw, cb, carry=None):
    t = proj.shape[0]
    tb = _pick(t, (512, 256, 128, 64))
    cbw = _pick(width, (512, 256, 128))
    off = col0 // cbw
    assert col0 % cbw == 0

    def body(x_ref, xp_ref, w_ref, b_ref, o_ref):
        prev = jnp.where(pl.program_id(1) == 0, 0.0, xp_ref[...])
        o_ref[...] = _silu(_conv(x_ref[...], prev, w_ref[...], b_ref[...]))

    return _pcall(
        body, carry=carry, name="conv_silu_fwd", grid=(width // cbw, t // tb),
        in_specs=[pl.BlockSpec((tb, cbw), lambda j, i: (i, off + j)),
                  pl.BlockSpec((SUBLANE, cbw), lambda j, i: (jnp.maximum(i * (tb // SUBLANE) - 1, 0), off + j)),
                  pl.BlockSpec((CONV_WIDTH, cbw), lambda j, i: (0, j)), pl.BlockSpec((1, cbw), lambda j, i: (0, j))],
        out_specs=pl.BlockSpec((tb, cbw), lambda j, i: (i, j)),
        out_shape=jax.ShapeDtypeStruct((t, width), F32),
        compiler_params=_cparams(("parallel", "arbitrary")),
    )(proj, proj, cw, cb)


def conv_silu_bwd(proj, col0, width, cw, cb, dact, dproj, carry=None):
    t = proj.shape[0]
    tb = _pick(t, (512, 256, 128, 64))
    nb = t // tb
    cbw = _pick(width, (512, 256, 128))
    off = col0 // cbw

    def body(x_ref, xp_ref, w_ref, b_ref, d_ref, _, dx_ref, dw_ref, db_ref, nxt):
        pid = pl.program_id(1)
        i = nb - 1 - pid

        @pl.when(pid == 0)
        def _():
            nxt[...] = jnp.zeros_like(nxt)
            dw_ref[...] = jnp.zeros_like(dw_ref)
            db_ref[...] = jnp.zeros_like(db_ref)

        x = x_ref[...]
        prev = jnp.where(i == 0, 0.0, xp_ref[...])
        w = w_ref[...]
        _, vjp = jax.vjp(_silu, _conv(x, prev, w, b_ref[...]))
        (dcv,) = vjp(d_ref[...])
        dx, dw, db = _conv_bwd(dcv, nxt[...], x, prev, w)
        nxt[...] = dcv[:SUBLANE]
        dx_ref[...] = dx.astype(dx_ref.dtype)
        dw_ref[...] += dw
        db_ref[...] += db

    return _pcall(
        body, carry=carry, name="conv_silu_bwd", grid=(width // cbw, nb),
        in_specs=[pl.BlockSpec((tb, cbw), lambda j, p: (nb - 1 - p, off + j)),
                  pl.BlockSpec((SUBLANE, cbw),
                               lambda j, p: (jnp.maximum((nb - 1 - p) * (tb // SUBLANE) - 1, 0), off + j)),
                  pl.BlockSpec((CONV_WIDTH, cbw), lambda j, p: (0, j)), pl.BlockSpec((1, cbw), lambda j, p: (0, j)),
                  pl.BlockSpec((tb, cbw), lambda j, p: (nb - 1 - p, j)), ANY],
        out_specs=[pl.BlockSpec((tb, cbw), lambda j, p: (nb - 1 - p, off + j)),
                   pl.BlockSpec((CONV_WIDTH, cbw), lambda j, p: (0, j)), pl.BlockSpec((1, cbw), lambda j, p: (0, j))],
        out_shape=[jax.ShapeDtypeStruct(dproj.shape, dproj.dtype), jax.ShapeDtypeStruct(cw.shape, F32),
                   jax.ShapeDtypeStruct(cb.shape, F32)],
        scratch_shapes=[pltpu.VMEM((SUBLANE, cbw), F32)],
        input_output_aliases={5: 0},
        compiler_params=_cparams(("parallel", "arbitrary")),
    )(proj, proj, cw, cb, dact, dproj)


def _dt_expand(raw, bias, e):
    return sel_r(_softplus(raw + bias), e)


def dt_fwd(proj, dtcol, bias, e):
    t = proj.shape[0]
    di = e.shape[1]
    tb = _pick(t, (512, 256, 128, 64))

    def body(r_ref, b_ref, e_ref, o_ref):
        o_ref[...] = _dt_expand(r_ref[...], b_ref[...], e_ref[...])

    return pl.pallas_call(
        body, name="dt_fwd", grid=(t // tb,),
        in_specs=[pl.BlockSpec((tb, LANE), lambda i: (i, dtcol)), _full(bias.shape), _full(e.shape)],
        out_specs=pl.BlockSpec((tb, di), lambda i: (i, 0)),
        out_shape=jax.ShapeDtypeStruct((t, di), F32),
        compiler_params=_cparams(("parallel",)),
    )(proj, bias, e)


def dt_bwd(proj, dtcol, bias, e, ddte, dproj):
    t = proj.shape[0]
    di = e.shape[1]
    tb = _pick(t, (512, 256, 128, 64))
    tail = dproj.shape[1] - dtcol * LANE
    assert (dtcol * LANE) % tail == 0

    def body(r_ref, b_ref, e_ref, d_ref, _, dr_ref, db_ref):
        @pl.when(pl.program_id(0) == 0)
        def _():
            db_ref[...] = jnp.zeros_like(db_ref)

        e_v = e_ref[...]
        _, vjp = jax.vjp(lambda r, b: _dt_expand(r, b, e_v), r_ref[...], b_ref[...])
        dr, db = vjp(d_ref[...])
        dr_ref[:, :LANE] = dr.astype(dr_ref.dtype)
        if tail > LANE:
            dr_ref[:, LANE:] = jnp.zeros((tb, tail - LANE), dr_ref.dtype)
        db_ref[...] += db

    return pl.pallas_call(
        body, name="dt_bwd", grid=(t // tb,),
        in_specs=[pl.BlockSpec((tb, LANE), lambda i: (i, dtcol)), _full(bias.shape), _full(e.shape),
                  pl.BlockSpec((tb, di), lambda i: (i, 0)), ANY],
        out_specs=[pl.BlockSpec((tb, tail), lambda i: (i, dtcol * LANE // tail)), _full(bias.shape)],
        out_shape=[jax.ShapeDtypeStruct(dproj.shape, dproj.dtype), jax.ShapeDtypeStruct(bias.shape, F32)],
        input_output_aliases={4: 0},
        compiler_params=_cparams(("arbitrary",)),
    )(proj, bias, e, ddte, dproj)


def head_expand(p, e, *, transpose=False, name):
    di = e.shape[1]

    def body(p_ref, e_ref, o_ref):
        if transpose:
            o_ref[...] = _sel_r_b(e_ref[...], p_ref[...])[0]
        else:
            o_ref[...] = sel_r(p_ref[...], e_ref[...])

    oshape = (SUBLANE, LANE) if transpose else (SUBLANE, di)
    return pl.pallas_call(
        body, name=name, in_specs=[_full(p.shape), _full(e.shape)], out_specs=_full(oshape),
        out_shape=jax.ShapeDtypeStruct(oshape, F32), compiler_params=_cparams(None), grid=(1,),
    )(p, e)


def _ssd_chunk(x, z, bm, cm, dte, st, alog, dskip, gn, tri, cmask, dmask, bd):
    L, gw = x.shape
    reps = gw // L
    a = dte * (-jnp.exp(alog))
    acs = sel_l(tri, a)
    acs_last = jnp.sum(a, axis=0, keepdims=True)
    arow = jnp.sum(acs * dmask, axis=0, keepdims=True)
    dtrow = jnp.sum(dte * dmask, axis=0, keepdims=True)
    cb = mm_nt(cm, jnp.concatenate([bm] * reps, axis=0))
    wts = cb * (jnp.exp(jnp.minimum(acs - arow, 0.0)) * cmask) * dtrow
    xbd = jnp.concatenate([x] * reps, axis=0) * bd
    xw = x * (jnp.exp(acs_last - acs) * dte)
    y = mm(wts, xbd) + mm(cm, st) * jnp.exp(acs) + dskip * x
    st_new = jnp.exp(acs_last) * st + mm_tn(bm, xw)
    return _rms(y * _silu(z), gn), st_new


def _ssd_dims(di):
    gw = di // SSD_GROUPS
    assert CHUNK == SSD_HEAD_DIM and gw % LANE == 0
    return gw, SSD_STATE


def _ssd_masks(gw):
    L = CHUNK
    r = jnp.arange(L)[:, None]
    c = jnp.arange(gw)[None, :]
    cmask = ((c % L) <= r).astype(F32)
    dmask = ((c % L) == r).astype(F32)
    rr = jnp.arange(gw)
    bd = ((rr[:, None] // L) == (rr[None, :] // L)).astype(F32)
    tri = (jnp.arange(L)[None, :] <= jnp.arange(L)[:, None]).astype(BF16)
    return tri, cmask, dmask, bd


def ssd_fwd(xs, proj, dte, alog_e, dskip_e, gn, carry=None):
    t, di = dte.shape
    gw, n = _ssd_dims(di)
    L, G = CHUNK, SSD_GROUPS
    nc = t // L
    masks = _ssd_masks(gw)
    cdim = xs.shape[1]

    def body(x_ref, z_ref, dt_ref, al_ref, ds_ref, gn_ref, tri_ref, cm_ref, dm_ref, bd_ref, y_ref, sp_ref, st):
        @pl.when(pl.program_id(0) == 0)
        def _():
            st[...] = jnp.zeros_like(st)

        for g in range(G):
            ch = slice(g * gw, (g + 1) * gw)
            s_prev = st[g]
            sp_ref[0, g] = s_prev
            y, s_new = _ssd_chunk(x_ref[:, ch], z_ref[:, ch], x_ref[:, di + g * n:di + (g + 1) * n],
                                  x_ref[:, di + (G + g) * n:di + (G + g + 1) * n], dt_ref[:, ch], s_prev,
                                  al_ref[0:1, ch], ds_ref[0:1, ch], gn_ref[:, ch], tri_ref[...], cm_ref[...],
                                  dm_ref[...], bd_ref[...])
            y_ref[:, ch] = y.astype(y_ref.dtype)
            st[g] = s_new

    row = lambda w: pl.BlockSpec((L, w), lambda c: (c, 0))
    return _pcall(
        body, carry=carry, name="ssd_fwd", grid=(nc,),
        in_specs=[row(cdim), row(di), row(di), _full(alog_e.shape), _full(dskip_e.shape), _full(gn.shape)]
        + [_full(m.shape) for m in masks],
        out_specs=[row(di), pl.BlockSpec((1, G, n, gw), lambda c: (c, 0, 0, 0))],
        out_shape=[jax.ShapeDtypeStruct((t, di), BF16), jax.ShapeDtypeStruct((nc, G, n, gw), F32)],
        scratch_shapes=[pltpu.VMEM((G, n, gw), F32)],
        compiler_params=_cparams(("arbitrary",)),
    )(xs, proj, dte, alog_e, dskip_e, gn, *masks)


def ssd_bwd(xs, proj, dte, alog_e, dskip_e, gn, sprev, dy, dproj_shape, carry=None):
    t, di = dte.shape
    gw, n = _ssd_dims(di)
    L, G = CHUNK, SSD_GROUPS
    nc = t // L
    masks = _ssd_masks(gw)
    cdim = xs.shape[1]

    def body(x_ref, z_ref, dt_ref, al_ref, ds_ref, gn_ref, tri_ref, cm_ref, dm_ref, bd_ref, sp_ref, dy_ref,
             dxs_ref, dz_ref, ddt_ref, dal_ref, dds_ref, dgn_ref, dst):
        @pl.when(pl.program_id(0) == 0)
        def _():
            dst[...] = jnp.zeros_like(dst)
            dal_ref[...] = jnp.zeros_like(dal_ref)
            dds_ref[...] = jnp.zeros_like(dds_ref)
            dgn_ref[...] = jnp.zeros_like(dgn_ref)

        f = functools.partial(_ssd_chunk, tri=tri_ref[...], cmask=cm_ref[...], dmask=dm_ref[...], bd=bd_ref[...])
        for g in range(G):
            ch = slice(g * gw, (g + 1) * gw)
            bs = slice(di + g * n, di + (g + 1) * n)
            cs = slice(di + (G + g) * n, di + (G + g + 1) * n)
            _, vjp = jax.vjp(f, x_ref[:, ch], z_ref[:, ch], x_ref[:, bs], x_ref[:, cs], dt_ref[:, ch], sp_ref[0, g],
                             al_ref[0:1, ch], ds_ref[0:1, ch], gn_ref[:, ch])
            dx, dz, db, dc, ddt, ds, dal, dds, dgn = vjp((dy_ref[:, ch], dst[g]))
            dxs_ref[:, ch] = dx
            dxs_ref[:, bs] = db
            dxs_ref[:, cs] = dc
            dz_ref[:, ch] = dz.astype(dz_ref.dtype)
            ddt_ref[:, ch] = ddt
            dst[g] = ds
            dal_ref[:, ch] += dal
            dds_ref[:, ch] += dds
            dgn_ref[:, ch] += dgn

    row = lambda w: pl.BlockSpec((L, w), lambda c: (nc - 1 - c, 0))
    acc = _full((1, di))
    acc_shape = jax.ShapeDtypeStruct((1, di), F32)
    return _pcall(
        body, carry=carry, name="ssd_bwd", grid=(nc,),
        in_specs=[row(cdim), row(di), row(di), _full(alog_e.shape), _full(dskip_e.shape), _full(gn.shape)]
        + [_full(m.shape) for m in masks]
        + [pl.BlockSpec((1, G, n, gw), lambda c: (nc - 1 - c, 0, 0, 0)), row(di)],
        out_specs=[row(cdim), row(di), row(di), acc, acc, acc],
        out_shape=[jax.ShapeDtypeStruct((t, cdim), F32), jax.ShapeDtypeStruct(dproj_shape, BF16),
                   jax.ShapeDtypeStruct((t, di), F32), acc_shape, acc_shape, acc_shape],
        scratch_shapes=[pltpu.VMEM((G, n, gw), F32)],
        compiler_params=_cparams(("arbitrary",)),
    )(xs, proj, dte, alog_e, dskip_e, gn, *masks, sprev, dy)


def _rows2d(a):
    return a.reshape(-1, a.shape[-1])


def _row_tile(rows, cols):
    cap = max(SUBLANE, (1 << 19) // max(cols, 1))
    step = 2 * SUBLANE
    for c in range(min(cap, rows) // step * step, 0, -step):
        if rows % c == 0:
            return c
    return rows


def chip_sum(g, r, core, *, name):
    shape = r.shape
    cols = shape[-1]
    g4 = g.reshape(4, 2, -1, cols)
    r3 = r.reshape(4, -1, cols)
    rows = r3.shape[1]
    tr = _row_tile(rows, cols)

    def body(c_ref, g_ref, r_ref, o_ref):
        o_ref[...] = (g_ref[...].astype(F32) + r_ref[...].astype(F32)).astype(o_ref.dtype)

    out = pl.pallas_call(
        body, name=name,
        grid_spec=pltpu.PrefetchScalarGridSpec(
            num_scalar_prefetch=1, grid=(4, rows // tr),
            in_specs=[pl.BlockSpec((None, None, tr, cols), lambda j, i, c: (j, c[0], i, 0)),
                      pl.BlockSpec((None, tr, cols), lambda j, i, c: (j, i, 0))],
            out_specs=pl.BlockSpec((None, tr, cols), lambda j, i, c: (j, i, 0))),
        out_shape=jax.ShapeDtypeStruct(r3.shape, BF16), compiler_params=_cparams(("parallel", "parallel")),
    )(core.reshape(1).astype(jnp.int32), g4, r3)
    return out.reshape(shape)


def mesh_sum(p, r, chip, core, *, name):
    shape = p.shape[1:]
    cols = shape[-1]
    p3 = p.reshape(4, -1, cols)
    r3 = r.reshape(3, -1, cols)
    rows = p3.shape[1]
    tr = _row_tile(rows, 2 * cols)

    def body(c_ref, p_ref, r_ref, o_ref):
        o_ref[...] = ((p_ref[...].astype(F32) + r_ref[0].astype(F32)) + r_ref[1].astype(F32)) + r_ref[2].astype(F32)

    out = pl.pallas_call(
        body, name=name,
        grid_spec=pltpu.PrefetchScalarGridSpec(
            num_scalar_prefetch=1, grid=(rows // tr,),
            in_specs=[pl.BlockSpec((None, tr, cols), lambda i, c: (c[0], i, 0)),
                      pl.BlockSpec((3, tr, cols), lambda i, c: (0, i, 0))],
            out_specs=pl.BlockSpec((None, tr, cols), lambda i, c: (c[1], i, 0))),
        out_shape=jax.ShapeDtypeStruct((2, rows, cols), F32), compiler_params=_cparams(("parallel",)),
    )(jnp.stack([chip, core]).astype(jnp.int32), p3, r3)
    return out.reshape((2,) + shape)


def sum_leading(x, *, name):
    k, r, c = x.shape
    tr = _row_tile(r, c * k)
    def body(x_ref, o_ref):
        acc = x_ref[0]
        for i in range(1, k):
            acc = acc + x_ref[i]
        o_ref[...] = acc

    return pl.pallas_call(
        body, name=name, grid=(r // tr,), in_specs=[pl.BlockSpec((k, tr, c), lambda i: (0, i, 0))],
        out_specs=pl.BlockSpec((tr, c), lambda i: (i, 0)),
        out_shape=jax.ShapeDtypeStruct((r, c), F32), compiler_params=_cparams(("parallel",)),
    )(x)


def adamw(w, gs, m, v, *, name):
    shape = w.shape
    w3, m3, v3 = (a.reshape((-1,) + a.shape[-2:]) for a in (w, m, v))
    nl, r, c = w3.shape
    assert len(gs) == nl
    tr = _row_tile(r, 2 * c)
    tc = c
    if tr == r and r * c > (1 << 19):
        tc = next(t for t in (1024, 512, 256, 128) if c % t == 0 and r * t <= (1 << 19))
    c1 = 1.0 - ADAM_B1 ** ADAM_STEP
    c2 = 1.0 - ADAM_B2 ** ADAM_STEP

    def body(w_ref, g_ref, m_ref, v_ref, *rest):
        go_ref, d_ref, mo_ref, vo_ref = rest[-4:]
        gv = g_ref[...]
        mn = ADAM_B1 * m_ref[...] + (1.0 - ADAM_B1) * gv
        vn = ADAM_B2 * v_ref[...] + (1.0 - ADAM_B2) * (gv * gv)
        go_ref[...] = gv
        d_ref[...] = -ADAM_LR * ((mn / c1) / (jnp.sqrt(vn / c2) + ADAM_EPS) + ADAM_WD * w_ref[...])
        mo_ref[...] = mn
        vo_ref[...] = vn

    outs = None
    for l, g in enumerate(gs):
        layer = pl.BlockSpec((None, tr, tc), lambda i, j, l=l: (l, i, j))
        prev = [] if outs is None else list(outs)
        outs = pl.pallas_call(
            functools.partial(body), name=f"{name}_{l}", grid=(r // tr, c // tc),
            in_specs=[layer, pl.BlockSpec((tr, tc), lambda i, j: (i, j)), layer, layer] + [ANY] * len(prev),
            out_specs=[layer] * 4, out_shape=[jax.ShapeDtypeStruct((nl, r, c), F32)] * 4,
            input_output_aliases={4 + k: k for k in range(len(prev))},
            compiler_params=_cparams(("parallel", "parallel")),
        )(w3, g.reshape(r, c), m3, v3, *prev)
    return tuple(o.reshape(shape) for o in outs)


ANY = pl.BlockSpec(memory_space=pl.ANY)


def _place():
    x, y, c = lax.axis_index("x"), lax.axis_index("y"), lax.axis_index("c")
    chips = [(1 - x, y), (x, 1 - y), (1 - x, 1 - y)]
    return x, y, c, chips


def gather8(block):
    m, n = block.shape

    def body(x_ref, out_ref, send_sems, recv_sems, local_sem):
        x, y, c, chips = _place()
        me, sibling = (x, y, c), (x, y, 1 - c)

        def rows(px, py, pc):
            return out_ref.at[4 * px + 2 * py + pc]

        def copy(k, blk, to, src=None):
            return pltpu.make_async_remote_copy(
                src_ref=rows(*blk) if src is None else src, dst_ref=rows(*blk), send_sem=send_sems.at[k],
                recv_sem=recv_sems.at[k], device_id=to, device_id_type=MESH)

        mine = pltpu.make_async_copy(x_ref, rows(*me), local_sem)
        mine.start()
        first = [copy(0, me, sibling, src=x_ref)]
        first += [copy(1 + j, me, (*chip, c), src=x_ref) for j, chip in enumerate(chips)]
        for cp in first:
            cp.start()
        passed = [copy(4 + j, (*chip, c), sibling) for j, chip in enumerate(chips)]
        for j, chip in enumerate(chips):
            copy(1 + j, (*chip, c), me).wait_recv()
            passed[j].start()
        copy(0, sibling, me).wait_recv()
        for j, chip in enumerate(chips):
            copy(4 + j, (*chip, 1 - c), me).wait_recv()
        for cp in first + passed:
            cp.wait_send()
        mine.wait()

    return pl.pallas_call(
        body, name="gather8",
        out_shape=jax.ShapeDtypeStruct((8, m, n), block.dtype),
        in_specs=[pl.BlockSpec(memory_space=pltpu.VMEM)],
        out_specs=pl.BlockSpec(memory_space=pltpu.VMEM),
        scratch_shapes=[pltpu.SemaphoreType.DMA((7,)), pltpu.SemaphoreType.DMA((7,)), pltpu.SemaphoreType.DMA],
        compiler_params=pltpu.CompilerParams(vmem_limit_bytes=VMEM_LIMIT),
    )(block)


def gather_weights(shards):
    n = len(shards)

    def copy(ins, outs, send, recv, base, a, k, chip_idx, half, to, src=None):
        dst = outs[a].at[chip_idx, half]
        return pltpu.make_async_remote_copy(
            src_ref=dst if src is None else src, dst_ref=dst, send_sem=send.at[base + 6 * a + k],
            recv_sem=recv.at[base + 6 * a + k], device_id=to, device_id_type=MESH)

    def first(ins, outs, send, recv, base):
        x, y, c, chips = _place()
        return [copy(ins, outs, send, recv, base, a, j, 2 * x + y, c, (*chip, c), src=ins[a].at[c])
                for a in range(n) for j, chip in enumerate(chips)]

    def start(ins, outs, send, recv, base):
        for cp in first(ins, outs, send, recv, base):
            cp.start()

    def finish(ins, outs, send, recv, base):
        x, y, c, chips = _place()
        sibling = (x, y, 1 - c)
        passed = []
        for a in range(n):
            for j, (cx, cy) in enumerate(chips):
                copy(ins, outs, send, recv, base, a, j, 2 * cx + cy, c, (cx, cy, c)).wait_recv()
                fw = copy(ins, outs, send, recv, base, a, 3 + j, 2 * cx + cy, c, sibling)
                fw.start()
                passed.append(fw)
        for a in range(n):
            for j, (cx, cy) in enumerate(chips):
                copy(ins, outs, send, recv, base, a, 3 + j, 2 * cx + cy, 1 - c, sibling).wait_recv()
        for cp in first(ins, outs, send, recv, base) + passed:
            cp.wait_send()

    return Carry(shards, [jax.ShapeDtypeStruct((4,) + s.shape, s.dtype) for s in shards], 6 * n, start, finish)


def exchange_halves(grads):
    n = len(grads)

    def copies(ins, outs, send, recv, base):
        x, y, c, _ = _place()
        return [pltpu.make_async_remote_copy(
            src_ref=ins[a].at[j, 1 - c], dst_ref=outs[a].at[j], send_sem=send.at[base + 4 * a + j],
            recv_sem=recv.at[base + 4 * a + j], device_id=(x, y, 1 - c), device_id_type=MESH)
            for a in range(n) for j in range(4)]

    def start(*args):
        for cp in copies(*args):
            cp.start()

    def finish(*args):
        for cp in copies(*args):
            cp.wait()

    return Carry(grads, [jax.ShapeDtypeStruct((4,) + g.shape[2:], g.dtype) for g in grads], 4 * n, start, finish)


def scatter_chips(parts, rows=None, into=None):
    n = len(parts)
    sl = (lambda r: r) if rows is None else (lambda r: r.at[pl.ds(rows[0], rows[1])])

    def copies(ins, outs, send, recv, base):
        x, y, c, chips = _place()
        return [pltpu.make_async_remote_copy(
            src_ref=sl(ins[a].at[2 * cx + cy]), dst_ref=sl(outs[a].at[j]), send_sem=send.at[base + 3 * a + j],
            recv_sem=recv.at[base + 3 * a + j], device_id=(cx, cy, c), device_id_type=MESH)
            for a in range(n) for j, (cx, cy) in enumerate(chips)]

    def start(*args):
        for cp in copies(*args):
            cp.start()

    def finish(*args):
        for cp in copies(*args):
            cp.wait()

    shapes = [jax.ShapeDtypeStruct((3,) + p.shape[1:], p.dtype) for p in parts]
    if into is None:
        return Carry(parts, shapes, 3 * n, start, finish)
    return Carry(list(parts) + list(into), shapes, 3 * n, start, finish, aliases={n + a: a for a in range(n)})


def join_halves(bufs):
    n = len(bufs)

    def copy(outs, send, recv, base, a, half):
        x, y, c, _ = _place()
        return pltpu.make_async_remote_copy(
            src_ref=outs[a].at[c], dst_ref=outs[a].at[c if half is None else half], send_sem=send.at[base + a],
            recv_sem=recv.at[base + a], device_id=(x, y, 1 - c), device_id_type=MESH)

    def start(ins, outs, send, recv, base):
        for a in range(n):
            copy(outs, send, recv, base, a, None).start()

    def finish(ins, outs, send, recv, base):
        c = lax.axis_index("c")
        for a in range(n):
            copy(outs, send, recv, base, a, 1 - c).wait_recv()
        for a in range(n):
            copy(outs, send, recv, base, a, None).wait_send()

    return Carry(bufs, [jax.ShapeDtypeStruct(h.shape, h.dtype) for h in bufs], n, start, finish,
                 aliases={a: a for a in range(n)})


def run_comm(carry, *, name):
    k_in, k_out = len(carry.arrays), len(carry.out_shape)

    def body(*refs):
        ins, outs = refs[:k_in], refs[k_in:k_in + k_out]
        send, recv = refs[-2:]
        carry.start(ins, outs, send, recv, 0)
        carry.finish(ins, outs, send, recv, 0)

    return pl.pallas_call(
        body, name=name, out_shape=carry.out_shape, in_specs=[ANY] * k_in, out_specs=[ANY] * k_out,
        scratch_shapes=[pltpu.SemaphoreType.DMA((carry.n_sems,))] * 2, input_output_aliases=carry.aliases,
    )(*carry.arrays)


INPUTS = ['x'] + WEIGHTS + ['loss_target'] + ['m_' + n for n in WEIGHTS] + ['v_' + n for n in WEIGHTS]


def _round_up(n, m):
    return -(-n // m) * m


def _pack(arrs):
    flat = jnp.concatenate([a.reshape(-1) for a in arrs])
    n = _round_up(flat.shape[0], 512 * LANE)
    return jnp.pad(flat, (0, n - flat.shape[0])).reshape(-1, LANE)


def _unpack(block, shapes):
    flat = block.reshape(-1)
    out, o = [], 0
    for s in shapes:
        n = math.prod(s)
        out.append(flat[o:o + n].reshape(s))
        o += n
    return out


def _cols(g):
    return g.transpose(1, 0, 2).reshape(g.shape[1], -1)


def _uncols(w):
    return w.reshape(w.shape[0], 4, -1).transpose(1, 0, 2)


def _pad_cols(w, total):
    return jnp.pad(w, ((0, 0), (0, total - w.shape[1])))


def kernel(*args):
    a = dict(zip(INPUTS, args))
    x, tgt = a['x'][0], a['loss_target'][0]
    t, d = x.shape
    xi, yi, ci = lax.axis_index("x"), lax.axis_index("y"), lax.axis_index("c")
    chip = 2 * xi + yi
    dk, dv, hk, hv = _gla_dims(d)
    lw = d // 2
    di = 2 * d
    nh = di // SSD_HEAD_DIM
    gn_w = SSD_GROUPS * SSD_STATE
    conv_dim = di + 2 * gn_w
    rank = GLA_GATE_RANK
    wq = 2 * dk + 2 * dv
    gla_w = _round_up(wq + LANE, 2 * lw)
    ev_tot = gla_w + 2 * lw
    od_used = di + conv_dim + nh
    od_tot = _round_up(di + conv_dim + _round_up(nh, LANE), 512)
    glr_col, xcol, dtcol = wq // LANE, gla_w // lw, (di + conv_dim) // LANE
    assert ev_tot % 512 == 0 and nh <= LANE

    def halves(w):
        w = w.astype(BF16)
        return w.reshape((2, w.shape[0] // 2) + w.shape[1:])

    own = {'ev_w_in': halves(a['ev_w_in'][0]), 'ev_w_out': halves(a['ev_w_out'][0]),
           'od_w_in_a': halves(a['od_w_in'][0][:d // 2]), 'od_w_in_b': halves(a['od_w_in'][0][d // 2:]),
           'od_w_out': halves(a['od_w_out'][0])}
    for l in range(2):
        own[f'gate{l}'], own[f'up{l}'] = halves(a['ffn_w_gate'][l]), halves(a['ffn_w_up'][l])
        own[f'down{l}'] = halves(a['ffn_w_down'][l])

    def gather(*units):
        return gather_weights([own[u] for u in units])

    def filled(unit, g):
        g = lax.dynamic_update_index_in_dim(g, own[unit], chip, 0)
        return g.reshape((4, 2 * g.shape[2]) + g.shape[3:])

    g_ev_in, g_ev_out = run_comm(gather('ev_w_in', 'ev_w_out'), name="gather_ev")
    w_ev_in = _cols(filled('ev_w_in', g_ev_in))
    cuts = [dk, 2 * dk, 2 * dk + dv, wq, wq + rank, wq + rank + lw]
    sq, sk, sv, sg, sglr, sxb, sgb = jnp.split(w_ev_in, cuts, axis=1)
    w_ev_in_p = jnp.concatenate([_pad_cols(jnp.concatenate([sq, sk, sv, sg, sglr], axis=1), gla_w), sxb, sgb], axis=1)
    w_ev_out = filled('ev_w_out', g_ev_out).reshape(-1, d)
    w_gate, w_up, w_down = [None, None], [None, None], [None, None]

    sh_names = list(SMALL_SHARDED)
    sh_shapes = [a[n].shape for n in sh_names]
    g8 = gather8(_pack([a[n] for n in sh_names]))
    per_chip = [_unpack(g8[2 * j], sh_shapes) for j in range(4)]
    full = {n: jnp.concatenate([per_chip[j][i] for j in range(4)], axis=SMALL_SHARDED[n])
            for i, n in enumerate(sh_names)}

    wg_p = jnp.zeros((LANE, dk), F32).at[:rank].set(full['ev_gla_w_gate'][0])
    bg, wn = a['ev_gla_b_gate'], a['ev_gla_w_onorm']
    lru_p = [full['ev_lru_conv_w'][0], a['ev_lru_conv_b'], a['ev_lru_w_a'][0], a['ev_lru_b_a'], a['ev_lru_w_i'][0],
             a['ev_lru_b_i'], a['ev_lru_lam']]
    od_cw, od_cb, od_gn = full['od_conv_w'][0], full['od_conv_b'], full['od_gnorm']
    heads = jnp.arange(LANE)[:, None]
    e_mat = ((jnp.arange(di)[None, :] // SSD_HEAD_DIM == heads) & (heads < nh)).astype(BF16)
    row8 = lambda p: jnp.zeros((SUBLANE, LANE), F32).at[0, :nh].set(p[0])
    dt_bias_p = jnp.zeros((1, LANE), F32).at[0, :nh].set(a['od_dt_bias'][0])
    alog_e = head_expand(row8(a['od_a_log']), e_mat, name="expand_a_log")
    dskip_e = head_expand(row8(a['od_d_skip']), e_mat, name="expand_d_skip")

    h0 = rms_fwd(x, a['ev_norm'], name="rms_ev")
    proj, (g,) = matmul(h0, w_ev_in_p, name="ev_in", carry=gather('gate0'))
    w_gate[0] = filled('gate0', g)
    (o_gla, sp_gla), (g,) = gla_fwd(proj, glr_col, wg_p, bg, wn, dv, carry=gather('up0'))
    w_up[0] = filled('up0', g)
    o_lru, hin = lru_fwd(proj, xcol, lw, *lru_p)
    x1 = matmul(o_gla, w_ev_out[:dv], add=x, name="ev_out_a")
    x1 = matmul(o_lru, w_ev_out[dv:], add=x1, name="ev_out_b")

    h1 = rms_fwd(x1, a['ffn_norm'][0:1], name="rms_ffn0")
    gate0, (g,) = matmul(h1, w_gate[0], name="ffn0_gate", carry=gather('down0'))
    w_down[0] = filled('down0', g).reshape(-1, d)
    swi = dict(epi=_swi_fwd_epi, epi_out=(F32, BF16))
    (up0, act0), (g_a,) = matmul(h1, w_up[0], name="ffn0_up", epi_in=(gate0,), carry=gather('od_w_in_a'), **swi)
    x2, (g_b,) = matmul(act0, w_down[0], add=x1, name="ffn0_down", carry=gather('od_w_in_b'))
    w_od_in = jnp.concatenate([filled('od_w_in_a', g_a), filled('od_w_in_b', g_b)], axis=1)
    w_od_in_p = _pad_cols(_cols(w_od_in), od_tot)

    h2 = rms_fwd(x2, full['od_norm'], name="rms_od")
    proj2, (g, g1) = matmul(h2, w_od_in_p, name="od_in", carry=gather('od_w_out', 'gate1'))
    w_od_out, w_gate[1] = filled('od_w_out', g).reshape(-1, d), filled('gate1', g1)
    xs, (g,) = conv_silu_fwd(proj2, di, conv_dim, od_cw, od_cb, carry=gather('up1'))
    w_up[1] = filled('up1', g)
    dte = dt_fwd(proj2, dtcol, dt_bias_p, e_mat)
    (y_ssd, sp_ssd), (g,) = ssd_fwd(xs, proj2, dte, alog_e, dskip_e, od_gn, carry=gather('down1'))
    w_down[1] = filled('down1', g).reshape(-1, d)
    x3 = matmul(y_ssd, w_od_out, add=x2, name="od_out")
    h3 = rms_fwd(x3, a['ffn_norm'][1:2], name="rms_ffn1")
    gate1 = matmul(h3, w_gate[1], name="ffn1_gate")
    up1, act1 = matmul(h3, w_up[1], name="ffn1_up", epi_in=(gate1,), **swi)
    x4 = matmul(act1, w_down[1], add=x3, name="ffn1_down")
    loss_p, dx4, dx4b, d_final = loss_head(x4, a['final_norm'][None], tgt, name="loss_head")

    grads, from_sib, part, from_chips = {}, {}, {}, {}

    def rows4(dw):
        return dw.reshape((4, 2, dw.shape[0] // 8) + dw.shape[1:])

    def cols4(dw):
        return dw.reshape((4, 2, dw.shape[1] // 2) + dw.shape[2:])

    def exchange(*units):
        return exchange_halves([grads[u] for u in units])

    def scatter(*units):
        return scatter_chips([part[u] for u in units])

    def sum_chip(u):
        part[u] = chip_sum(grads[u], from_sib[u], ci, name=f"chip_sum_{u}")

    def ffn_bwd(dxo, dxob, xin, h, gate, up, act, l, first_carry, first_units):
        dn, gt, up_ = f'down{l}', f'gate{l}', f'up{l}'
        dgu = matmul(dxob, w_down[l], tb=True, name=f"ffn{l}_d_act", carry=first_carry, epi=_swi_bwd_epi,
                     epi_in=(gate, up), epi_out=(BF16, BF16))
        (dg, du), got = dgu if first_units else (dgu, ())
        for u, r in zip(first_units, got):
            from_sib[u] = r
            sum_chip(u)
        piece = [part[u].shape[1] // 2 for u in first_units]
        d_down = matmul(act, dxob, ta=True, out_dtype=BF16, name=f"ffn{l}_dw_down",
                        carry=scatter_chips([part[u] for u in first_units], rows=(0, piece[0])) if first_units else None)
        if first_units:
            d_down, first_rb = d_down
        grads[dn] = rows4(d_down)
        dh, (from_sib[dn],) = matmul(dg, w_gate[l], tb=True, name=f"ffn{l}_dh_gate", carry=exchange(dn))
        sum_chip(dn)
        dh = matmul(du, w_up[l], tb=True, add=dh, name=f"ffn{l}_dh_up",
                    carry=scatter_chips([part[u] for u in first_units], rows=(piece[0], piece[0]), into=first_rb)
                    if first_units else None)
        if first_units:
            dh, (from_chips[first_units[0]],) = dh
        d_gate, (from_chips[dn],) = matmul(h, dg, ta=True, out_dtype=BF16, out_shards=True, name=f"ffn{l}_dw_gate",
                                           carry=scatter(dn))
        grads[gt] = cols4(d_gate)
        d_up, (from_sib[gt],) = matmul(h, du, ta=True, out_dtype=BF16, out_shards=True, name=f"ffn{l}_dw_up",
                                       carry=exchange(gt))
        grads[up_] = cols4(d_up)
        return rms_bwd(xin, a['ffn_norm'][l:l + 1], dh, dxo, name=f"rms_ffn{l}_bwd")

    dx3, dx3b, d_fn1 = ffn_bwd(dx4, dx4b, x3, h3, gate1, up1, act1, 1, None, ())
    dy, (from_sib['up1'],) = matmul(dx3b, w_od_out, tb=True, name="od_out_dy", carry=exchange('up1'))
    sum_chip('gate1')
    sum_chip('up1')
    grads['od_w_out'] = rows4(matmul(y_ssd, dx3b, ta=True, out_dtype=BF16, name="od_out_dw"))
    (dxs, dproj2, ddte, dal, dds, dgn), (from_chips['gate1'], from_chips['up1'], from_sib['od_w_out']) = ssd_bwd(
        xs, proj2, dte, alog_e, dskip_e, od_gn, sp_ssd, dy, proj2.shape,
        carry=merge_carries(scatter('gate1', 'up1'), exchange('od_w_out')))
    sum_chip('od_w_out')
    (dproj2, d_od_cw, d_od_cb), (from_chips['od_w_out'],) = conv_silu_bwd(
        proj2, di, conv_dim, od_cw, od_cb, dxs, dproj2, carry=scatter('od_w_out'))
    dproj2, d_dt_bias = dt_bwd(proj2, dtcol, dt_bias_p, e_mat, ddte, dproj2)
    dh2 = matmul(dproj2, w_od_in_p, tb=True, name="od_in_dh")
    d_od_in = matmul(h2, dproj2, ta=True, out_dtype=BF16, name="od_in_dw")[:, :od_used]
    grads['od_w_in'] = cols4(_uncols(d_od_in))
    dx2, dx2b, d_od_norm = rms_bwd(x2, full['od_norm'], dh2, dx3, name="rms_od_bwd")
    to8 = lambda acc: jnp.zeros((SUBLANE, di), F32).at[0].set(acc.reshape(-1))
    d_a_log = head_expand(to8(dal), e_mat, transpose=True, name="reduce_a_log")[0:1, :nh]
    d_d_skip = head_expand(to8(dds), e_mat, transpose=True, name="reduce_d_skip")[0:1, :nh]

    dx1, dx1b, d_fn0 = ffn_bwd(dx2, dx2b, x1, h1, gate0, up0, act0, 0, exchange('od_w_in'), ('od_w_in',))
    dmix, (from_sib['up0'],) = matmul(dx1b, w_ev_out, tb=True, name="ev_out_dmix", carry=exchange('up0'))
    sum_chip('gate0')
    sum_chip('up0')
    grads['ev_w_out'] = rows4(jnp.concatenate([matmul(o_gla, dx1b, ta=True, out_dtype=BF16, name="ev_out_dw_a"),
                                               matmul(o_lru, dx1b, ta=True, out_dtype=BF16, name="ev_out_dw_b")], axis=0))
    (dproj, d_wg, d_bg, d_wn), (from_sib['ev_w_out'],) = gla_bwd(
        proj, glr_col, wg_p, bg, wn, sp_gla, dmix, dv, gla_w, carry=exchange('ev_w_out'))
    sum_chip('ev_w_out')
    (dproj, *d_lru), (from_chips['ev_w_out'], from_chips['gate0']) = lru_bwd(
        proj, xcol, lw, *lru_p, hin, dmix, 1, dproj, carry=scatter('ev_w_out', 'gate0'))
    d_ev_in_p, (from_chips['up0'],) = matmul(h0, dproj, ta=True, out_dtype=BF16, name="ev_in_dw", carry=scatter('up0'))
    d_ev_in = jnp.concatenate([d_ev_in_p[:, :wq + rank], d_ev_in_p[:, gla_w:]], axis=1)
    grads['ev_w_in'] = cols4(_uncols(d_ev_in))
    (from_sib['ev_w_in'],) = run_comm(exchange('ev_w_in'), name="exchange_ev_in")
    sum_chip('ev_w_in')
    dh0, (from_chips['ev_w_in'],) = matmul(dproj, w_ev_in_p, tb=True, name="ev_in_dh", carry=scatter('ev_w_in'))
    dx0, _, d_ev_norm = rms_bwd(x, a['ev_norm'], dh0, dx1, name="rms_ev_bwd")

    units = list(grads)
    half = [mesh_sum(part[u], from_chips[u], chip, ci, name=f"mesh_sum_{u}") for u in units]
    done = dict(zip(units, run_comm(join_halves(half), name="join_halves")))
    layers = {n: [done[n]] for n in ('ev_w_in', 'ev_w_out', 'od_w_in', 'od_w_out')}
    layers.update({f'ffn_w_{u}': [done[f'{u}0'], done[f'{u}1']] for u in ('gate', 'up', 'down')})
    grad, delta, new_m, new_v = {}, {}, {}, {}
    for n in BIG:
        flip = a[n].shape[-1] % LANE != 0
        tr_ = (lambda t: jnp.swapaxes(t, -1, -2)) if flip else (lambda t: t)
        gs = [tr_(g.reshape(a[n].shape[-2:])) for g in layers[n]]
        outs = adamw(tr_(a[n]), gs, tr_(a['m_' + n]), tr_(a['v_' + n]), name=f"adamw_{n}")
        grad[n], delta[n], new_m[n], new_v[n] = (tr_(o) for o in outs)

    small_g = {
        'ev_norm': d_ev_norm, 'ev_gla_w_gate': d_wg[:rank][None], 'ev_gla_b_gate': d_bg, 'ev_gla_w_onorm': d_wn,
        'ev_lru_conv_w': d_lru[0][None], 'ev_lru_conv_b': d_lru[1], 'ev_lru_w_a': d_lru[2][None],
        'ev_lru_b_a': d_lru[3], 'ev_lru_w_i': d_lru[4][None], 'ev_lru_b_i': d_lru[5], 'ev_lru_lam': d_lru[6],
        'od_norm': d_od_norm, 'od_conv_w': d_od_cw[None], 'od_conv_b': d_od_cb, 'od_dt_bias': d_dt_bias[:, :nh],
        'od_a_log': d_a_log, 'od_d_skip': d_d_skip, 'od_gnorm': dgn.reshape(1, di),
        'ffn_norm': jnp.concatenate([d_fn0, d_fn1], axis=0), 'final_norm': d_final[0],
    }
    full_shapes = [small_g[n].shape for n in SMALL]
    summed = sum_leading(gather8(_pack([small_g[n] for n in SMALL])), name="sum_devices")
    for n, g in zip(SMALL, _unpack(summed, full_shapes)):
        if n in SMALL_SHARDED:
            ax = SMALL_SHARDED[n]
            sz = a[n].shape[ax]
            g = lax.dynamic_slice_in_dim(g, chip * sz, sz, axis=ax)
        grad[n] = g

    shapes = [a[n].shape for n in SMALL]
    packed = [_pack([src[n] if pre is None else a[pre + n] for n in SMALL])
              for src, pre in ((a, None), (grad, None), (None, 'm_'), (None, 'v_'))]
    small_out = adamw(packed[0], [packed[1]], packed[2], packed[3], name="adamw_small")
    for outd, blk in zip((delta, new_m, new_v), small_out[1:]):
        outd.update(zip(SMALL, _unpack(blk, shapes)))

    loss = lax.psum(loss_p[0, 0], ("x", "y", "c"))
    return (loss, dx0[None], *[grad[n] for n in WEIGHTS], *[delta[n] for n in WEIGHTS],
            *[new_m[n] for n in WEIGHTS], *[new_v[n] for n in WEIGHTS])
```

```python
import functools
import math

import jax
import jax.numpy as jnp
from jax import lax
from jax.experimental import pallas as pl
from jax.experimental.pallas import tpu as pltpu

F32 = jnp.float32
BF16 = jnp.bfloat16
MXU_DTYPE = jnp.bfloat16

NORM_EPS = 1e-6
CONV_WIDTH = 4
GLA_HEADS = 4
GLA_GATE_RANK = 16
GLA_GATE_NORM = 16.0
CHUNK = 64
LRU_BLOCK = 128
LRU_C = 8.0
SSD_HEAD_DIM = 64
SSD_GROUPS = 8
SSD_STATE = 128
ADAM_LR, ADAM_B1, ADAM_B2, ADAM_EPS, ADAM_WD, ADAM_STEP = 0.001, 0.9, 0.999, 1e-08, 0.01, 10

LANE = 128
SUBLANE = 8
VMEM_LIMIT = 48 * 1024 * 1024
MAX_TK = 2816
MATMUL_VMEM_BUDGET = 45 * 1024 * 1024
MESH = pl.DeviceIdType.MESH

WEIGHTS = ['ev_norm', 'ev_w_in', 'ev_gla_w_gate', 'ev_gla_b_gate', 'ev_gla_w_onorm', 'ev_lru_conv_w', 'ev_lru_conv_b',
           'ev_lru_w_a', 'ev_lru_b_a', 'ev_lru_w_i', 'ev_lru_b_i', 'ev_lru_lam', 'ev_w_out', 'od_norm', 'od_w_in',
           'od_conv_w', 'od_conv_b', 'od_dt_bias', 'od_a_log', 'od_d_skip', 'od_gnorm', 'od_w_out', 'ffn_norm',
           'ffn_w_gate', 'ffn_w_up', 'ffn_w_down', 'final_norm']
BIG = ['ev_w_in', 'ev_w_out', 'od_w_in', 'od_w_out', 'ffn_w_gate', 'ffn_w_up', 'ffn_w_down']
SMALL_SHARDED = {'ev_gla_w_gate': 2, 'ev_lru_conv_w': 2, 'od_norm': 1, 'od_conv_w': 2, 'od_conv_b': 1, 'od_gnorm': 1}
SMALL = [n for n in WEIGHTS if n not in BIG]


def _cparams(sem=None, **kw):
    return pltpu.CompilerParams(dimension_semantics=sem, vmem_limit_bytes=VMEM_LIMIT, **kw)


def _full(shape):
    n = len(shape)
    return pl.BlockSpec(shape, lambda *_: (0,) * n)


ANY = pl.BlockSpec(memory_space=pl.ANY)


class Carry:
    def __init__(self, arrays, out_shape, n_sems, start, finish, aliases=None):
        self.arrays, self.out_shape, self.n_sems = list(arrays), list(out_shape), n_sems
        self.start, self.finish, self.aliases = start, finish, dict(aliases or {})


def merge_carries(*cs):
    cs = [c for c in cs if c is not None]
    if not cs:
        return None
    arrays = [a for c in cs for a in c.arrays]
    out_shape = [s for c in cs for s in c.out_shape]
    offs, i0, o0, s0 = [], 0, 0, 0
    aliases = {}
    for c in cs:
        offs.append((i0, o0, s0))
        aliases.update({i0 + i: o0 + o for i, o in c.aliases.items()})
        i0, o0, s0 = i0 + len(c.arrays), o0 + len(c.out_shape), s0 + c.n_sems

    def both(which):
        def run(ins, outs, send, recv, base):
            for c, (i, o, s) in zip(cs, offs):
                getattr(c, which)(ins[i:i + len(c.arrays)], outs[o:o + len(c.out_shape)], send, recv, base + s)
        return run

    return Carry(arrays, out_shape, s0, both("start"), both("finish"), aliases)


def _pcall(body, *, name, grid, in_specs, out_specs, out_shape, scratch_shapes=(), compiler_params, carry=None,
           input_output_aliases=None):
    aliases = dict(input_output_aliases or {})
    if carry is None:
        return pl.pallas_call(body, name=name, grid=grid, in_specs=in_specs, out_specs=out_specs, out_shape=out_shape,
                              scratch_shapes=list(scratch_shapes), compiler_params=compiler_params,
                              input_output_aliases=aliases)
    single = not isinstance(out_specs, (list, tuple))
    specs_o = [out_specs] if single else list(out_specs)
    shapes_o = [out_shape] if single else list(out_shape)
    n_in, n_out, k_in, k_out, n_scr = len(in_specs), len(specs_o), len(carry.arrays), len(carry.out_shape), len(scratch_shapes)

    def wrapped(*refs):
        ins, cins = refs[:n_in], refs[n_in:n_in + k_in]
        o0 = n_in + k_in
        outs, couts = refs[o0:o0 + n_out], refs[o0 + n_out:o0 + n_out + k_out]
        scr = refs[o0 + n_out + k_out:o0 + n_out + k_out + n_scr]
        send, recv = refs[-2:]
        ids = [pl.program_id(ax) for ax in range(len(grid))]
        first = functools.reduce(jnp.logical_and, [i == 0 for i in ids])
        last = functools.reduce(jnp.logical_and, [i == g - 1 for i, g in zip(ids, grid)])

        @pl.when(first)
        def _():
            carry.start(cins, couts, send, recv, 0)

        body(*ins, *outs, *scr)

        @pl.when(last)
        def _():
            carry.finish(cins, couts, send, recv, 0)

    aliases.update({n_in + i: n_out + o for i, o in carry.aliases.items()})
    call = pl.pallas_call(
        wrapped, name=name, grid=grid, in_specs=list(in_specs) + [ANY] * k_in, out_specs=specs_o + [ANY] * k_out,
        out_shape=shapes_o + carry.out_shape,
        scratch_shapes=list(scratch_shapes) + [pltpu.SemaphoreType.DMA((carry.n_sems,))] * 2,
        compiler_params=_cparams(("arbitrary",) * len(grid)), input_output_aliases=aliases)

    def run(*args):
        res = call(*args, *carry.arrays)
        main = res[:n_out]
        return (main[0] if single else list(main)), list(res[n_out:])

    return run


def _pick(dim, cands):
    for c in cands:
        if dim % c == 0:
            return c
    return dim


def _dims(a, ca, cb):
    nb = a.ndim - 2
    return (((ca + nb,), (cb + nb,)), (tuple(range(nb)), tuple(range(nb))))


def _dot(a, b, ca, cb):
    return lax.dot_general(a.astype(MXU_DTYPE), b.astype(MXU_DTYPE), _dims(a, ca, cb), preferred_element_type=F32)


@jax.custom_vjp
def mm(a, b):
    return _dot(a, b, 1, 0)


def _mm_f(a, b):
    return mm(a, b), (a, b)


def _mm_b(res, g):
    a, b = res
    return mm_nt(g, b).astype(a.dtype), mm_tn(a, g).astype(b.dtype)


@jax.custom_vjp
def mm_nt(a, b):
    return _dot(a, b, 1, 1)


def _mm_nt_f(a, b):
    return mm_nt(a, b), (a, b)


def _mm_nt_b(res, g):
    a, b = res
    return mm(g, b).astype(a.dtype), mm_tn(g, a).astype(b.dtype)


@jax.custom_vjp
def mm_tn(a, b):
    return _dot(a, b, 0, 0)


def _mm_tn_f(a, b):
    return mm_tn(a, b), (a, b)


def _mm_tn_b(res, g):
    a, b = res
    return mm_nt(b, g).astype(a.dtype), mm(a, g).astype(b.dtype)


mm.defvjp(_mm_f, _mm_b)
mm_nt.defvjp(_mm_nt_f, _mm_nt_b)
mm_tn.defvjp(_mm_tn_f, _mm_tn_b)


def _split3(a):
    h = a.astype(BF16)
    r = a - h.astype(F32)
    m = r.astype(BF16)
    l = (r - m.astype(F32)).astype(BF16)
    return h, m, l


def _exact_dot(t, a, ca, cb):
    out = None
    for p in _split3(a):
        d = lax.dot_general(t, p, _dims(a, ca, cb), preferred_element_type=F32)
        out = d if out is None else out + d
    return out


@jax.custom_vjp
def sel_l(t, a):
    return _exact_dot(t, a, 1, 0)


def _sel_l_f(t, a):
    return sel_l(t, a), t


def _sel_l_b(t, g):
    return jnp.zeros_like(t), _exact_dot(t, g, 0, 0)


sel_l.defvjp(_sel_l_f, _sel_l_b)


@jax.custom_vjp
def sel_r(a, t):
    out = None
    for p in _split3(a):
        d = lax.dot_general(p, t, (((1,), (0,)), ((), ())), preferred_element_type=F32)
        out = d if out is None else out + d
    return out


def _sel_r_f(a, t):
    return sel_r(a, t), t


def _sel_r_b(t, g):
    out = None
    for p in _split3(g):
        d = lax.dot_general(p, t, (((1,), (1,)), ((), ())), preferred_element_type=F32)
        out = d if out is None else out + d
    return out, jnp.zeros_like(t)


sel_r.defvjp(_sel_r_f, _sel_r_b)


def _sigmoid(x):
    return 1.0 / (1.0 + jnp.exp(-x))


def _silu(x):
    return x * _sigmoid(x)


def _softplus(x):
    return jnp.maximum(x, 0.0) + jnp.log(1.0 + jnp.exp(-jnp.abs(x)))


def _log_sigmoid(x):
    return -_softplus(-x)


def _gelu_tanh(x):
    c = math.sqrt(2.0 / math.pi)
    return 0.5 * x * (1.0 + jnp.tanh(c * (x + 0.044715 * (x * x * x))))


def _rms(x, w):
    return x * lax.rsqrt(jnp.mean(x * x, axis=-1, keepdims=True) + NORM_EPS) * w


def _tri(n, dtype=BF16):
    r = lax.broadcasted_iota(jnp.int32, (n, n), 0)
    c = lax.broadcasted_iota(jnp.int32, (n, n), 1)
    return (c <= r).astype(dtype)


def matmul(a, b, *, ta=False, tb=False, add=None, out_dtype=F32, out_shards=False, carry=None, name,
           epi=None, epi_in=(), epi_out=()):
    m, k = (a.shape[1], a.shape[0]) if ta else a.shape
    b_sh = b.ndim == 3
    if b_sh:
        s, br, bc = b.shape
        k2, n = (s * bc, br) if tb else (br, s * bc)
    else:
        k2, n = (b.shape[1], b.shape[0]) if tb else b.shape
    assert k == k2, (a.shape, b.shape, ta, tb)
    tk_opts = [bc] if b_sh and tb else [k] if k <= MAX_TK else \
        [c for c in range(MAX_TK, LANE - 1, -LANE) if k % c == 0][:1]
    n_add, n_x = int(add is not None), len(epi_in)
    out_dtypes = list(epi_out) if epi is not None else [out_dtype]
    n_o = len(out_dtypes)
    tn_opts = [bc] if b_sh and not tb else [n // 4] if out_shards else \
        [c for c in range(2048, LANE - 1, -LANE) if n % c == 0] or [n]
    tm_opts = [c for c in range(2048, LANE - 1, -LANE) if m % c == 0] or [m]
    sa, sb = a.dtype.itemsize, b.dtype.itemsize
    per_elem = sum(jnp.dtype(dt).itemsize for dt in out_dtypes) + 4 * n_add + sum(x.dtype.itemsize for x in epi_in)
    best = None
    for tk_ in tk_opts:
        for tm_ in tm_opts:
            for tn_ in tn_opts:
                vmem = 2 * (tm_ * tk_ * sa + tk_ * tn_ * sb) + tm_ * tn_ * (4 + 2 * per_elem)
                vmem += tm_ * tn_ * 4
                if vmem > MATMUL_VMEM_BUDGET and (tk_, tm_, tn_) != (tk_opts[-1], tm_opts[-1], tn_opts[-1]):
                    continue
                moved = m * k * sa * (1 if tk_ == k else n // tn_) + k * n * sb * (m // tm_)
                if best is None or (moved, -tk_, -tm_ * tn_) < best[0]:
                    best = ((moved, -tk_, -tm_ * tn_), tm_, tn_, tk_)
    _, tm, tn, tk = best
    nk = k // tk

    def body(*refs):
        a_ref, b_ref = refs[:2]
        x_refs = refs[2 + n_add:2 + n_add + n_x]
        o_refs = refs[2 + n_add + n_x:2 + n_add + n_x + n_o]
        acc = refs[-1]
        kk = pl.program_id(2)

        @pl.when(kk == 0)
        def _():
            acc[...] = jnp.zeros_like(acc)

        acc[...] += _dot(a_ref[...], b_ref[...], 0 if ta else 1, 1 if tb else 0)

        @pl.when(kk == nk - 1)
        def _():
            r = acc[...]
            if add is not None:
                r = r + refs[2][...].astype(F32)
            vals = (r,) if epi is None else epi(r, *[x[...] for x in x_refs])
            for o_ref, v in zip(o_refs, vals):
                o_ref[...] = v.astype(o_ref.dtype)

    a_spec = pl.BlockSpec((tk, tm), lambda i, j, kk: (kk, i)) if ta else pl.BlockSpec((tm, tk), lambda i, j, kk: (i, kk))
    if b_sh and tb:
        b_spec = pl.BlockSpec((None, tn, tk), lambda i, j, kk: (kk, j, 0))
    elif b_sh:
        b_spec = pl.BlockSpec((None, tk, tn), lambda i, j, kk: (j, kk, 0))
    elif tb:
        b_spec = pl.BlockSpec((tn, tk), lambda i, j, kk: (j, kk))
    else:
        b_spec = pl.BlockSpec((tk, tn), lambda i, j, kk: (kk, j))
    in_specs, args = [a_spec, b_spec], [a, b]
    tile = pl.BlockSpec((tm, tn), lambda i, j, kk: (i, j))
    for extra in ([add] if add is not None else []) + list(epi_in):
        in_specs.append(tile)
        args.append(extra)
    if out_shards:
        out_spec = pl.BlockSpec((None, tm, tn), lambda i, j, kk: (j, i, 0))
        out_shape = jax.ShapeDtypeStruct((4, m, tn), out_dtype)
    elif epi is not None:
        out_spec = [tile] * n_o
        out_shape = [jax.ShapeDtypeStruct((m, n), dt) for dt in out_dtypes]
    else:
        out_spec = tile
        out_shape = jax.ShapeDtypeStruct((m, n), out_dtype)
    return _pcall(
        body, name=name, grid=(m // tm, n // tn, nk), in_specs=in_specs, out_specs=out_spec, out_shape=out_shape,
        scratch_shapes=[pltpu.VMEM((tm, tn), F32)],
        compiler_params=_cparams(("parallel", "parallel", "arbitrary")), carry=carry,
    )(*args)


def matmul_pair(a1, b1, a2, b2, *, carry=None, name):
    m, k = a1.shape
    nk, r, bc = b1.shape
    assert a2.shape == a1.shape and b2.shape == b1.shape and k == nk * bc
    tm = _pick(m, (1024, 512, 256, 128))
    tn = _pick(r, (1024, 512, 256, 128))

    def body(a1_ref, b1_ref, a2_ref, b2_ref, o_ref, acc):
        kk = pl.program_id(2)

        @pl.when(kk == 0)
        def _():
            acc[...] = jnp.zeros_like(acc)

        @pl.when(kk < nk)
        def _():
            acc[...] += _dot(a1_ref[...], b1_ref[...], 1, 1)

        @pl.when(kk >= nk)
        def _():
            acc[...] += _dot(a2_ref[...], b2_ref[...], 1, 1)

        @pl.when(kk == 2 * nk - 1)
        def _():
            o_ref[...] = acc[...]

    k1 = lambda kk: jnp.minimum(kk, nk - 1)
    k2 = lambda kk: jnp.maximum(kk - nk, 0)
    return _pcall(
        body, name=name, grid=(m // tm, r // tn, 2 * nk), carry=carry,
        in_specs=[pl.BlockSpec((tm, bc), lambda i, j, kk: (i, k1(kk))),
                  pl.BlockSpec((None, tn, bc), lambda i, j, kk: (k1(kk), j, 0)),
                  pl.BlockSpec((tm, bc), lambda i, j, kk: (i, k2(kk))),
                  pl.BlockSpec((None, tn, bc), lambda i, j, kk: (k2(kk), j, 0))],
        out_specs=pl.BlockSpec((tm, tn), lambda i, j, kk: (i, j)),
        out_shape=jax.ShapeDtypeStruct((m, r), F32), scratch_shapes=[pltpu.VMEM((tm, tn), F32)],
        compiler_params=_cparams(("parallel", "parallel", "arbitrary")),
    )(a1, b1, a2, b2)


def rms_fwd(x, w, *, name):
    t, d = x.shape
    tb = _pick(t, (256, 128, 64))

    def body(x_ref, w_ref, o_ref):
        o_ref[...] = _rms(x_ref[...], w_ref[...]).astype(o_ref.dtype)

    return pl.pallas_call(
        body, name=name, grid=(t // tb,),
        in_specs=[pl.BlockSpec((tb, d), lambda i: (i, 0)), _full((1, d))],
        out_specs=pl.BlockSpec((tb, d), lambda i: (i, 0)),
        out_shape=jax.ShapeDtypeStruct((t, d), BF16),
        compiler_params=_cparams(("parallel",)),
    )(x, w)


def rms_bwd(x, w, dh, dres, *, name, carry=None):
    t, d = x.shape
    tb = _pick(t, (256, 128, 64))

    def body(x_ref, w_ref, dh_ref, dres_ref, dx_ref, dxb_ref, dw_ref):
        @pl.when(pl.program_id(0) == 0)
        def _():
            dw_ref[...] = jnp.zeros_like(dw_ref)

        _, vjp = jax.vjp(_rms, x_ref[...], w_ref[...])
        dx, dw = vjp(dh_ref[...].astype(F32))
        dx = dx + dres_ref[...]
        dx_ref[...] = dx
        dxb_ref[...] = dx.astype(dxb_ref.dtype)
        dw_ref[...] += dw

    row = pl.BlockSpec((tb, d), lambda i: (i, 0))
    return _pcall(
        body, name=name, grid=(t // tb,), carry=carry,
        in_specs=[row, _full((1, d)), row, row],
        out_specs=[row, row, _full((1, d))],
        out_shape=[jax.ShapeDtypeStruct((t, d), F32), jax.ShapeDtypeStruct((t, d), BF16),
                   jax.ShapeDtypeStruct((1, d), F32)],
        compiler_params=_cparams(("arbitrary",)),
    )(x, w, dh, dres)


def _swi(g, u):
    return _silu(g) * u


def _swi_fwd_epi(u, g):
    return u, _swi(g, u)


def _swi_bwd_epi(d, g, u):
    return jax.vjp(_swi, g, u)[1](d)


def loss_head(x, w, target, *, name):
    t, d = x.shape
    tb = _pick(t, (256, 128, 64))

    def f(xv, wv, tv):
        y = _rms(xv, wv)
        e = y - tv
        return 0.5 * jnp.sum(jnp.mean(e * e, axis=-1, keepdims=True), axis=0, keepdims=True)

    def body(x_ref, w_ref, t_ref, l_ref, dx_ref, dxb_ref, dw_ref):
        @pl.when(pl.program_id(0) == 0)
        def _():
            l_ref[...] = jnp.zeros_like(l_ref)
            dw_ref[...] = jnp.zeros_like(dw_ref)

        val, vjp = jax.vjp(lambda a, b: f(a, b, t_ref[...]), x_ref[...], w_ref[...])
        dx, dw = vjp(jnp.ones((1, 1), F32))
        l_ref[...] += jnp.broadcast_to(val, l_ref.shape)
        dx_ref[...] = dx
        dxb_ref[...] = dx.astype(dxb_ref.dtype)
        dw_ref[...] += dw

    row = pl.BlockSpec((tb, d), lambda i: (i, 0))
    return pl.pallas_call(
        body, name=name, grid=(t // tb,),
        in_specs=[row, _full((1, d)), row],
        out_specs=[_full((SUBLANE, LANE)), row, row, _full((1, d))],
        out_shape=[jax.ShapeDtypeStruct((SUBLANE, LANE), F32), jax.ShapeDtypeStruct((t, d), F32),
                   jax.ShapeDtypeStruct((t, d), BF16),
                   jax.ShapeDtypeStruct((1, d), F32)],
        compiler_params=_cparams(("arbitrary",)),
    )(x, w, target)


def _gla_chunk(q, k, v, g, glr, st, wg, bg, wn, tri):
    L, hk = q.shape[-2:]
    la = _log_sigmoid(mm(glr, wg) + bg) / GLA_GATE_NORM
    bcum = sel_l(jnp.broadcast_to(tri, la.shape[:-2] + tri.shape), la)
    b_last = jnp.sum(la, axis=-2, keepdims=True)
    rows = lax.broadcasted_iota(jnp.int32, (L, 1), 0)
    b_mid = jnp.sum(jnp.where(rows <= L // 2, la, 0.0), axis=-2, keepdims=True)
    qs = q * (hk ** -0.5)
    q_in = qs * jnp.exp(bcum - b_mid)
    k_in = k * jnp.exp(b_mid - bcum)
    scores = mm_nt(q_in, k_in) * tri.astype(F32)
    o_intra = mm(scores, v)
    k_st = k * jnp.exp(b_last - bcum)
    d_st = mm_tn(v, k_st)
    o_inter = mm_nt(qs * jnp.exp(bcum), st)
    st_new = jnp.exp(b_last) * st + d_st
    o = _rms(o_intra + o_inter, wn) * _silu(g)
    return o, st_new


def _heads(ref, start, width, n):
    return jnp.stack([ref[:, start + h * width:start + (h + 1) * width] for h in range(n)], axis=0)


def _gla_heads(p_ref, dk, dv, n):
    hk, hv = dk // n, dv // n
    return (_heads(p_ref, 0, hk, n), _heads(p_ref, dk, hk, n), _heads(p_ref, 2 * dk, hv, n),
            _heads(p_ref, 2 * dk + dv, hv, n))


def _gla_dims(d):
    dv = d // 2
    dk = dv // 2
    return dk, dv, dk // GLA_HEADS, dv // GLA_HEADS


def gla_fwd(proj, glr_col, wg, bg, wn, dv, carry=None):
    t = proj.shape[0]
    dk, dv, hk, hv = _gla_dims(2 * dv)
    L, H = CHUNK, GLA_HEADS
    nc = t // L
    wq = 2 * dk + 2 * dv

    def body(p_ref, glr_ref, wg_ref, bg_ref, wn_ref, o_ref, sp_ref, st):
        @pl.when(pl.program_id(0) == 0)
        def _():
            st[...] = jnp.zeros_like(st)

        s_prev = st[...]
        sp_ref[0] = s_prev
        o, s_new = _gla_chunk(*_gla_heads(p_ref, dk, dv, H), jnp.broadcast_to(glr_ref[...], (H, L, LANE)), s_prev,
                              _heads(wg_ref, 0, hk, H), _heads(bg_ref, 0, hk, H), wn_ref[...], _tri(L))
        for h in range(H):
            o_ref[:, h * hv:(h + 1) * hv] = o[h].astype(o_ref.dtype)
        st[...] = s_new

    return _pcall(
        body, carry=carry, name="gla_fwd", grid=(nc,),
        in_specs=[pl.BlockSpec((L, wq), lambda c: (c, 0)), pl.BlockSpec((L, LANE), lambda c: (c, glr_col)),
                  _full(wg.shape), _full(bg.shape), _full(wn.shape)],
        out_specs=[pl.BlockSpec((L, dv), lambda c: (c, 0)), pl.BlockSpec((1, H, hv, hk), lambda c: (c, 0, 0, 0))],
        out_shape=[jax.ShapeDtypeStruct((t, dv), BF16), jax.ShapeDtypeStruct((nc, H, hv, hk), F32)],
        scratch_shapes=[pltpu.VMEM((H, hv, hk), F32)],
        compiler_params=_cparams(("arbitrary",)),
    )(proj, proj, wg, bg, wn)


def gla_bwd(proj, glr_col, wg, bg, wn, sprev, do, dv, gla_w, carry=None):
    t = proj.shape[0]
    dk, _, hk, hv = _gla_dims(2 * dv)
    L, H = CHUNK, GLA_HEADS
    nc = t // L
    wq = 2 * dk + 2 * dv

    def body(p_ref, glr_ref, wg_ref, bg_ref, wn_ref, sp_ref, do_ref, dp_ref, dwg_ref, dbg_ref, dwn_ref, dst):
        @pl.when(pl.program_id(0) == 0)
        def _():
            dst[...] = jnp.zeros_like(dst)
            dwg_ref[...] = jnp.zeros_like(dwg_ref)
            dbg_ref[...] = jnp.zeros_like(dbg_ref)
            dwn_ref[...] = jnp.zeros_like(dwn_ref)

        f = functools.partial(_gla_chunk, tri=_tri(L))
        _, vjp = jax.vjp(f, *_gla_heads(p_ref, dk, dv, H), jnp.broadcast_to(glr_ref[...], (H, L, LANE)), sp_ref[0],
                         _heads(wg_ref, 0, hk, H), _heads(bg_ref, 0, hk, H), wn_ref[...])
        dq, dkk, dvv, dg, dgl, ds, dwg, dbg, dwn = vjp((_heads(do_ref, 0, hv, H), dst[...]))
        for h in range(H):
            dp_ref[:, h * hk:(h + 1) * hk] = dq[h].astype(dp_ref.dtype)
            dp_ref[:, dk + h * hk:dk + (h + 1) * hk] = dkk[h].astype(dp_ref.dtype)
            dp_ref[:, 2 * dk + h * hv:2 * dk + (h + 1) * hv] = dvv[h].astype(dp_ref.dtype)
            dp_ref[:, 2 * dk + dv + h * hv:2 * dk + dv + (h + 1) * hv] = dg[h].astype(dp_ref.dtype)
            dwg_ref[:, h * hk:(h + 1) * hk] += dwg[h]
            dbg_ref[:, h * hk:(h + 1) * hk] += dbg[h]
        dst[...] = ds
        dwn_ref[...] += dwn
        dp_ref[:, wq:wq + LANE] = jnp.sum(dgl, axis=0).astype(dp_ref.dtype)
        if gla_w > wq + LANE:
            dp_ref[:, wq + LANE:] = jnp.zeros((L, gla_w - wq - LANE), dp_ref.dtype)

    rev = lambda c: nc - 1 - c
    return _pcall(
        body, carry=carry, name="gla_bwd", grid=(nc,),
        in_specs=[pl.BlockSpec((L, wq), lambda c: (rev(c), 0)), pl.BlockSpec((L, LANE), lambda c: (rev(c), glr_col)),
                  _full(wg.shape), _full(bg.shape), _full(wn.shape),
                  pl.BlockSpec((1, H, hv, hk), lambda c: (rev(c), 0, 0, 0)),
                  pl.BlockSpec((L, dv), lambda c: (rev(c), 0))],
        out_specs=[pl.BlockSpec((L, gla_w), lambda c: (rev(c), 0)),
                   _full(wg.shape), _full(bg.shape), _full(wn.shape)],
        out_shape=[jax.ShapeDtypeStruct(proj.shape, BF16),
                   jax.ShapeDtypeStruct(wg.shape, F32), jax.ShapeDtypeStruct(bg.shape, F32),
                   jax.ShapeDtypeStruct(wn.shape, F32)],
        scratch_shapes=[pltpu.VMEM((H, hv, hk), F32)],
        compiler_params=_cparams(("arbitrary",)),
    )(proj, proj, wg, bg, wn, sprev, do)


def _shift_down(x, tail, s):
    if s == 0:
        return x
    r = pltpu.roll(x, s, 0)
    rows = lax.broadcasted_iota(jnp.int32, tail.shape, 0)
    top = jnp.where(rows < s, pltpu.roll(tail, s, 0), r[:SUBLANE])
    return jnp.concatenate([top, r[SUBLANE:]], axis=0)


def _shift_up(x, head, s):
    if s == 0:
        return x
    n = x.shape[0]
    r = pltpu.roll(x, n - s, 0)
    rows = lax.broadcasted_iota(jnp.int32, head.shape, 0)
    bottom = jnp.where(rows >= SUBLANE - s, pltpu.roll(head, SUBLANE - s, 0), r[n - SUBLANE:])
    return jnp.concatenate([r[:n - SUBLANE], bottom], axis=0)


def _conv(x, prev, w, b):
    y = b
    for k in range(CONV_WIDTH):
        y = y + w[k:k + 1, :] * _shift_down(x, prev, CONV_WIDTH - 1 - k)
    return y


def _conv_bwd(dy, nxt, x, prev, w):
    dx = None
    dws = []
    for k in range(CONV_WIDTH):
        s = CONV_WIDTH - 1 - k
        term = w[k:k + 1, :] * _shift_up(dy, nxt, s)
        dx = term if dx is None else dx + term
        dws.append(jnp.sum(dy * _shift_down(x, prev, s), axis=0, keepdims=True))
    return dx, jnp.concatenate(dws, axis=0), jnp.sum(dy, axis=0, keepdims=True)


def _scan_fwd(a, u):
    n = a.shape[0]
    rows = lax.broadcasted_iota(jnp.int32, a.shape, 0)
    s = 1
    while s < n:
        a_sh = jnp.where(rows < s, 1.0, pltpu.roll(a, s, 0))
        u_sh = jnp.where(rows < s, 0.0, pltpu.roll(u, s, 0))
        u = a * u_sh + u
        a = a * a_sh
        s *= 2
    return a, u


def _scan_rev(c, d):
    n = c.shape[0]
    rows = lax.broadcasted_iota(jnp.int32, c.shape, 0)
    s = 1
    while s < n:
        c_sh = jnp.where(rows >= n - s, 0.0, pltpu.roll(c, n - s, 0))
        d_sh = jnp.where(rows >= n - s, 0.0, pltpu.roll(d, n - s, 0))
        d = d + c * d_sh
        c = c * c_sh
        s *= 2
    return d


def _expm1(x):
    small = x * (1.0 + x * (0.5 + x * (1.0 / 6.0 + x * (1.0 / 24.0))))
    return jnp.where(jnp.abs(x) < 1e-2, small, jnp.exp(x) - 1.0)


def _lru_gates(xc, pa, pi, lam):
    r = _sigmoid(pa)
    i = _sigmoid(pi)
    log_a = LRU_C * r * _log_sigmoid(lam)
    a = jnp.exp(log_a)
    u = jnp.sqrt(-_expm1(2.0 * log_a)) * (i * xc)
    return a, u


def _lru_out(h, gate):
    return h * _gelu_tanh(gate)


def _blockdiag(xc, w_ref, b):
    nb = w_ref.shape[0]
    outs = [mm(xc[:, n * LRU_BLOCK:(n + 1) * LRU_BLOCK], w_ref[n]) for n in range(nb)]
    return jnp.concatenate(outs, axis=1) + b


def lru_fwd(proj, xcol, lw, cw, cb, wa, ba, wi, bi, lam):
    t = proj.shape[0]
    tb = _pick(t, (256, 128, 64))
    nb = t // tb

    def body(x_ref, xp_ref, g_ref, cw_ref, cb_ref, wa_ref, ba_ref, wi_ref, bi_ref, lam_ref, o_ref, hin_ref, hc):
        i = pl.program_id(0)

        @pl.when(i == 0)
        def _():
            hc[...] = jnp.zeros_like(hc)

        prev = jnp.where(i == 0, 0.0, xp_ref[...])
        xc = _conv(x_ref[...], prev, cw_ref[...], cb_ref[...])
        a, u = _lru_gates(xc, _blockdiag(xc, wa_ref, ba_ref[...]), _blockdiag(xc, wi_ref, bi_ref[...]), lam_ref[...])
        acum, h0 = _scan_fwd(a, u)
        h = h0 + acum * hc[...]
        hin_ref[0] = hc[...]
        hc[...] = h[tb - 1:tb, :]
        o_ref[...] = _lru_out(h, g_ref[...]).astype(o_ref.dtype)

    row = lambda col: pl.BlockSpec((tb, lw), lambda i: (i, col))
    return pl.pallas_call(
        body, name="lru_fwd", grid=(nb,),
        in_specs=[row(xcol), pl.BlockSpec((SUBLANE, lw), lambda i: (jnp.maximum(i * (tb // SUBLANE) - 1, 0), xcol)),
                  row(xcol + 1),
                  _full(cw.shape), _full(cb.shape), _full(wa.shape), _full(ba.shape), _full(wi.shape), _full(bi.shape),
                  _full(lam.shape)],
        out_specs=[pl.BlockSpec((tb, lw), lambda i: (i, 0)), pl.BlockSpec((1, 1, lw), lambda i: (i, 0, 0))],
        out_shape=[jax.ShapeDtypeStruct((t, lw), BF16), jax.ShapeDtypeStruct((nb, 1, lw), F32)],
        scratch_shapes=[pltpu.VMEM((1, lw), F32)],
        compiler_params=_cparams(("arbitrary",)),
    )(proj, proj, proj, cw, cb, wa, ba, wi, bi, lam)


def lru_bwd(proj, xcol, lw, cw, cb, wa, ba, wi, bi, lam, hin, dmix, docol, dproj, carry=None):
    t = proj.shape[0]
    tb = _pick(t, (256, 128, 64))
    nb = t // tb
    nblk = wa.shape[0]
    assert xcol % 2 == 0

    def body(x_ref, xp_ref, g_ref, cw_ref, cb_ref, wa_ref, ba_ref, wi_ref, bi_ref, lam_ref, hin_ref, do_ref, _,
             dxg_ref, dcw_ref, dcb_ref, dwa_ref, dba_ref, dwi_ref, dbi_ref, dlam_ref, gc, dxcn):
        pid = pl.program_id(0)
        i = nb - 1 - pid

        @pl.when(pid == 0)
        def _():
            gc[...] = jnp.zeros_like(gc)
            dxcn[...] = jnp.zeros_like(dxcn)
            for r in (dcw_ref, dcb_ref, dwa_ref, dba_ref, dwi_ref, dbi_ref, dlam_ref):
                r[...] = jnp.zeros_like(r)

        x = x_ref[...]
        prev = jnp.where(i == 0, 0.0, xp_ref[...])
        cw_v = cw_ref[...]
        xc = _conv(x, prev, cw_v, cb_ref[...])
        pa = _blockdiag(xc, wa_ref, ba_ref[...])
        pi = _blockdiag(xc, wi_ref, bi_ref[...])
        (a, u), vjp_g = jax.vjp(_lru_gates, xc, pa, pi, lam_ref[...])
        acum, h0 = _scan_fwd(a, u)
        hi = hin_ref[0]
        h = h0 + acum * hi
        rows = lax.broadcasted_iota(jnp.int32, h.shape, 0)
        hprev = jnp.where(rows < 1, hi, pltpu.roll(h, 1, 0))
        _, vjp_o = jax.vjp(_lru_out, h, g_ref[...])
        dh, dgate = vjp_o(do_ref[...].astype(F32))
        c = jnp.where(rows >= tb - 1, 0.0, pltpu.roll(a, tb - 1, 0))
        g = _scan_rev(c, dh + jnp.where(rows == tb - 1, gc[...], 0.0))
        gc[...] = a[0:1, :] * g[0:1, :]
        dxc, dpa, dpi, dlam = vjp_g((g * hprev, g))
        dlam_ref[...] += dlam
        dba_ref[...] += jnp.sum(dpa, axis=0, keepdims=True)
        dbi_ref[...] += jnp.sum(dpi, axis=0, keepdims=True)
        parts = []
        for n in range(nblk):
            sl = slice(n * LRU_BLOCK, (n + 1) * LRU_BLOCK)
            dwa_ref[n] += mm_tn(xc[:, sl], dpa[:, sl])
            dwi_ref[n] += mm_tn(xc[:, sl], dpi[:, sl])
            parts.append(mm_nt(dpa[:, sl], wa_ref[n]) + mm_nt(dpi[:, sl], wi_ref[n]))
        dxc = dxc + jnp.concatenate(parts, axis=1)
        dx, dcw, dcb = _conv_bwd(dxc, dxcn[...], x, prev, cw_v)
        dxcn[...] = dxc[:SUBLANE]
        dcw_ref[...] += dcw
        dcb_ref[...] += dcb
        dxg_ref[:, :lw] = dx.astype(dxg_ref.dtype)
        dxg_ref[:, lw:] = dgate.astype(dxg_ref.dtype)

    row = lambda col: pl.BlockSpec((tb, lw), lambda p: (nb - 1 - p, col))
    params = [cw, cb, wa, ba, wi, bi, lam]
    return _pcall(
        body, carry=carry, name="lru_bwd", grid=(nb,),
        in_specs=[row(xcol),
                  pl.BlockSpec((SUBLANE, lw), lambda p: (jnp.maximum((nb - 1 - p) * (tb // SUBLANE) - 1, 0), xcol)),
                  row(xcol + 1)]
        + [_full(p.shape) for p in params]
        + [pl.BlockSpec((1, 1, lw), lambda p: (nb - 1 - p, 0, 0)), row(docol), ANY],
        out_specs=[pl.BlockSpec((tb, 2 * lw), lambda p: (nb - 1 - p, xcol // 2))] + [_full(p.shape) for p in params],
        out_shape=[jax.ShapeDtypeStruct(dproj.shape, dproj.dtype)]
        + [jax.ShapeDtypeStruct(p.shape, F32) for p in params],
        input_output_aliases={12: 0},
        scratch_shapes=[pltpu.VMEM((1, lw), F32), pltpu.VMEM((SUBLANE, lw), F32)],
        compiler_params=_cparams(("arbitrary",)),
    )(proj, proj, proj, *params, hin, dmix, dproj)


def conv_silu_fwd(proj, col0, width, cw, cb, carry=None):
    t = proj.shape[0]
    tb = _pick(t, (512, 256, 128, 64))
    cbw = _pick(width, (512, 256, 128))
    off = col0 // cbw
    assert col0 % cbw == 0

    def body(x_ref, xp_ref, w_ref, b_ref, o_ref):
        prev = jnp.where(pl.program_id(1) == 0, 0.0, xp_ref[...])
        o_ref[...] = _silu(_conv(x_ref[...], prev, w_ref[...], b_ref[...]))

    return _pcall(
        body, carry=carry, name="conv_silu_fwd", grid=(width // cbw, t // tb),
        in_specs=[pl.BlockSpec((tb, cbw), lambda j, i: (i, off + j)),
                  pl.BlockSpec((SUBLANE, cbw), lambda j, i: (jnp.maximum(i * (tb // SUBLANE) - 1, 0), off + j)),
                  pl.BlockSpec((CONV_WIDTH, cbw), lambda j, i: (0, j)), pl.BlockSpec((1, cbw), lambda j, i: (0, j))],
        out_specs=pl.BlockSpec((tb, cbw), lambda j, i: (i, j)),
        out_shape=jax.ShapeDtypeStruct((t, width), F32),
        compiler_params=_cparams(("parallel", "arbitrary")),
    )(proj, proj, cw, cb)


def conv_silu_bwd(proj, col0, width, cw, cb, dact, dproj, carry=None):
    t = proj.shape[0]
    tb = _pick(t, (512, 256, 128, 64))
    nb = t // tb
    cbw = _pick(width, (512, 256, 128))
    off = col0 // cbw

    def body(x_ref, xp_ref, w_ref, b_ref, d_ref, _, dx_ref, dw_ref, db_ref, nxt):
        pid = pl.program_id(1)
        i = nb - 1 - pid

        @pl.when(pid == 0)
        def _():
            nxt[...] = jnp.zeros_like(nxt)
            dw_ref[...] = jnp.zeros_like(dw_ref)
            db_ref[...] = jnp.zeros_like(db_ref)

        x = x_ref[...]
        prev = jnp.where(i == 0, 0.0, xp_ref[...])
        w = w_ref[...]
        _, vjp = jax.vjp(_silu, _conv(x, prev, w, b_ref[...]))
        (dcv,) = vjp(d_ref[...])
        dx, dw, db = _conv_bwd(dcv, nxt[...], x, prev, w)
        nxt[...] = dcv[:SUBLANE]
        dx_ref[...] = dx.astype(dx_ref.dtype)
        dw_ref[...] += dw
        db_ref[...] += db

    return _pcall(
        body, carry=carry, name="conv_silu_bwd", grid=(width // cbw, nb),
        in_specs=[pl.BlockSpec((tb, cbw), lambda j, p: (nb - 1 - p, off + j)),
                  pl.BlockSpec((SUBLANE, cbw),
                               lambda j, p: (jnp.maximum((nb - 1 - p) * (tb // SUBLANE) - 1, 0), off + j)),
                  pl.BlockSpec((CONV_WIDTH, cbw), lambda j, p: (0, j)), pl.BlockSpec((1, cbw), lambda j, p: (0, j)),
                  pl.BlockSpec((tb, cbw), lambda j, p: (nb - 1 - p, j)), ANY],
        out_specs=[pl.BlockSpec((tb, cbw), lambda j, p: (nb - 1 - p, off + j)),
                   pl.BlockSpec((CONV_WIDTH, cbw), lambda j, p: (0, j)), pl.BlockSpec((1, cbw), lambda j, p: (0, j))],
        out_shape=[jax.ShapeDtypeStruct(dproj.shape, dproj.dtype), jax.ShapeDtypeStruct(cw.shape, F32),
                   jax.ShapeDtypeStruct(cb.shape, F32)],
        scratch_shapes=[pltpu.VMEM((SUBLANE, cbw), F32)],
        input_output_aliases={5: 0},
        compiler_params=_cparams(("parallel", "arbitrary")),
    )(proj, proj, cw, cb, dact, dproj)


def _dt_expand(raw, bias, e):
    return sel_r(_softplus(raw + bias), e)


def dt_fwd(proj, dtcol, bias, e):
    t = proj.shape[0]
    di = e.shape[1]
    tb = _pick(t, (512, 256, 128, 64))

    def body(r_ref, b_ref, e_ref, o_ref):
        o_ref[...] = _dt_expand(r_ref[...], b_ref[...], e_ref[...])

    return pl.pallas_call(
        body, name="dt_fwd", grid=(t // tb,),
        in_specs=[pl.BlockSpec((tb, LANE), lambda i: (i, dtcol)), _full(bias.shape), _full(e.shape)],
        out_specs=pl.BlockSpec((tb, di), lambda i: (i, 0)),
        out_shape=jax.ShapeDtypeStruct((t, di), F32),
        compiler_params=_cparams(("parallel",)),
    )(proj, bias, e)


def dt_bwd(proj, dtcol, bias, e, ddte, dproj):
    t = proj.shape[0]
    di = e.shape[1]
    tb = _pick(t, (512, 256, 128, 64))
    tail = dproj.shape[1] - dtcol * LANE
    assert (dtcol * LANE) % tail == 0

    def body(r_ref, b_ref, e_ref, d_ref, _, dr_ref, db_ref):
        @pl.when(pl.program_id(0) == 0)
        def _():
            db_ref[...] = jnp.zeros_like(db_ref)

        e_v = e_ref[...]
        _, vjp = jax.vjp(lambda r, b: _dt_expand(r, b, e_v), r_ref[...], b_ref[...])
        dr, db = vjp(d_ref[...])
        dr_ref[:, :LANE] = dr.astype(dr_ref.dtype)
        if tail > LANE:
            dr_ref[:, LANE:] = jnp.zeros((tb, tail - LANE), dr_ref.dtype)
        db_ref[...] += db

    return pl.pallas_call(
        body, name="dt_bwd", grid=(t // tb,),
        in_specs=[pl.BlockSpec((tb, LANE), lambda i: (i, dtcol)), _full(bias.shape), _full(e.shape),
                  pl.BlockSpec((tb, di), lambda i: (i, 0)), ANY],
        out_specs=[pl.BlockSpec((tb, tail), lambda i: (i, dtcol * LANE // tail)), _full(bias.shape)],
        out_shape=[jax.ShapeDtypeStruct(dproj.shape, dproj.dtype), jax.ShapeDtypeStruct(bias.shape, F32)],
        input_output_aliases={4: 0},
        compiler_params=_cparams(("arbitrary",)),
    )(proj, bias, e, ddte, dproj)


def head_expand(p, e, *, transpose=False, name):
    di = e.shape[1]

    def body(p_ref, e_ref, o_ref):
        if transpose:
            o_ref[...] = _sel_r_b(e_ref[...], p_ref[...])[0]
        else:
            o_ref[...] = sel_r(p_ref[...], e_ref[...])

    oshape = (SUBLANE, LANE) if transpose else (SUBLANE, di)
    return pl.pallas_call(
        body, name=name, in_specs=[_full(p.shape), _full(e.shape)], out_specs=_full(oshape),
        out_shape=jax.ShapeDtypeStruct(oshape, F32), compiler_params=_cparams(None), grid=(1,),
    )(p, e)


def _ssd_chunk(x, z, bm, cm, dte, st, alog, dskip, gn, tri, cmask, dmask, bd):
    L, gw = x.shape
    reps = gw // L
    a = dte * (-jnp.exp(alog))
    acs = sel_l(tri, a)
    acs_last = jnp.sum(a, axis=0, keepdims=True)
    arow = jnp.sum(acs * dmask, axis=0, keepdims=True)
    dtrow = jnp.sum(dte * dmask, axis=0, keepdims=True)
    cb = mm_nt(cm, jnp.concatenate([bm] * reps, axis=0))
    wts = cb * (jnp.exp(jnp.minimum(acs - arow, 0.0)) * cmask) * dtrow
    xbd = jnp.concatenate([x] * reps, axis=0) * bd
    xw = x * (jnp.exp(acs_last - acs) * dte)
    y = mm(wts, xbd) + mm(cm, st) * jnp.exp(acs) + dskip * x
    st_new = jnp.exp(acs_last) * st + mm_tn(bm, xw)
    return _rms(y * _silu(z), gn), st_new


def _ssd_dims(di):
    gw = di // SSD_GROUPS
    assert CHUNK == SSD_HEAD_DIM and gw % LANE == 0
    return gw, SSD_STATE


def _ssd_masks(gw):
    L = CHUNK
    r = jnp.arange(L)[:, None]
    c = jnp.arange(gw)[None, :]
    cmask = ((c % L) <= r).astype(F32)
    dmask = ((c % L) == r).astype(F32)
    rr = jnp.arange(gw)
    bd = ((rr[:, None] // L) == (rr[None, :] // L)).astype(F32)
    tri = (jnp.arange(L)[None, :] <= jnp.arange(L)[:, None]).astype(BF16)
    return tri, cmask, dmask, bd


def ssd_fwd(xs, proj, dte, alog_e, dskip_e, gn, carry=None):
    t, di = dte.shape
    gw, n = _ssd_dims(di)
    L, G = CHUNK, SSD_GROUPS
    nc = t // L
    masks = _ssd_masks(gw)
    cdim = xs.shape[1]

    def body(x_ref, z_ref, dt_ref, al_ref, ds_ref, gn_ref, tri_ref, cm_ref, dm_ref, bd_ref, y_ref, sp_ref, st):
        @pl.when(pl.program_id(0) == 0)
        def _():
            st[...] = jnp.zeros_like(st)

        for g in range(G):
            ch = slice(g * gw, (g + 1) * gw)
            s_prev = st[g]
            sp_ref[0, g] = s_prev
            y, s_new = _ssd_chunk(x_ref[:, ch], z_ref[:, ch], x_ref[:, di + g * n:di + (g + 1) * n],
                                  x_ref[:, di + (G + g) * n:di + (G + g + 1) * n], dt_ref[:, ch], s_prev,
                                  al_ref[0:1, ch], ds_ref[0:1, ch], gn_ref[:, ch], tri_ref[...], cm_ref[...],
                                  dm_ref[...], bd_ref[...])
            y_ref[:, ch] = y.astype(y_ref.dtype)
            st[g] = s_new

    row = lambda w: pl.BlockSpec((L, w), lambda c: (c, 0))
    return _pcall(
        body, carry=carry, name="ssd_fwd", grid=(nc,),
        in_specs=[row(cdim), row(di), row(di), _full(alog_e.shape), _full(dskip_e.shape), _full(gn.shape)]
        + [_full(m.shape) for m in masks],
        out_specs=[row(di), pl.BlockSpec((1, G, n, gw), lambda c: (c, 0, 0, 0))],
        out_shape=[jax.ShapeDtypeStruct((t, di), BF16), jax.ShapeDtypeStruct((nc, G, n, gw), F32)],
        scratch_shapes=[pltpu.VMEM((G, n, gw), F32)],
        compiler_params=_cparams(("arbitrary",)),
    )(xs, proj, dte, alog_e, dskip_e, gn, *masks)


def ssd_bwd(xs, proj, dte, alog_e, dskip_e, gn, sprev, dy, dproj_shape, carry=None):
    t, di = dte.shape
    gw, n = _ssd_dims(di)
    L, G = CHUNK, SSD_GROUPS
    nc = t // L
    masks = _ssd_masks(gw)
    cdim = xs.shape[1]

    def body(x_ref, z_ref, dt_ref, al_ref, ds_ref, gn_ref, tri_ref, cm_ref, dm_ref, bd_ref, sp_ref, dy_ref,
             dxs_ref, dz_ref, ddt_ref, dal_ref, dds_ref, dgn_ref, dst):
        @pl.when(pl.program_id(0) == 0)
        def _():
            dst[...] = jnp.zeros_like(dst)
            dal_ref[...] = jnp.zeros_like(dal_ref)
            dds_ref[...] = jnp.zeros_like(dds_ref)
            dgn_ref[...] = jnp.zeros_like(dgn_ref)

        f = functools.partial(_ssd_chunk, tri=tri_ref[...], cmask=cm_ref[...], dmask=dm_ref[...], bd=bd_ref[...])
        for g in range(G):
            ch = slice(g * gw, (g + 1) * gw)
            bs = slice(di + g * n, di + (g + 1) * n)
            cs = slice(di + (G + g) * n, di + (G + g + 1) * n)
            _, vjp = jax.vjp(f, x_ref[:, ch], z_ref[:, ch], x_ref[:, bs], x_ref[:, cs], dt_ref[:, ch], sp_ref[0, g],
                             al_ref[0:1, ch], ds_ref[0:1, ch], gn_ref[:, ch])
            dx, dz, db, dc, ddt, ds, dal, dds, dgn = vjp((dy_ref[:, ch], dst[g]))
            dxs_ref[:, ch] = dx
            dxs_ref[:, bs] = db
            dxs_ref[:, cs] = dc
            dz_ref[:, ch] = dz.astype(dz_ref.dtype)
            ddt_ref[:, ch] = ddt
            dst[g] = ds
            dal_ref[:, ch] += dal
            dds_ref[:, ch] += dds
            dgn_ref[:, ch] += dgn

    row = lambda w: pl.BlockSpec((L, w), lambda c: (nc - 1 - c, 0))
    acc = _full((1, di))
    acc_shape = jax.ShapeDtypeStruct((1, di), F32)
    return _pcall(
        body, carry=carry, name="ssd_bwd", grid=(nc,),
        in_specs=[row(cdim), row(di), row(di), _full(alog_e.shape), _full(dskip_e.shape), _full(gn.shape)]
        + [_full(m.shape) for m in masks]
        + [pl.BlockSpec((1, G, n, gw), lambda c: (nc - 1 - c, 0, 0, 0)), row(di)],
        out_specs=[row(cdim), row(di), row(di), acc, acc, acc],
        out_shape=[jax.ShapeDtypeStruct((t, cdim), F32), jax.ShapeDtypeStruct(dproj_shape, BF16),
                   jax.ShapeDtypeStruct((t, di), F32), acc_shape, acc_shape, acc_shape],
        scratch_shapes=[pltpu.VMEM((G, n, gw), F32)],
        compiler_params=_cparams(("arbitrary",)),
    )(xs, proj, dte, alog_e, dskip_e, gn, *masks, sprev, dy)


def _rows2d(a):
    return a.reshape(-1, a.shape[-1])


def _row_tile(rows, cols):
    cap = max(SUBLANE, (1 << 19) // max(cols, 1))
    step = 2 * SUBLANE
    for c in range(min(cap, rows) // step * step, 0, -step):
        if rows % c == 0:
            return c
    return rows


def chip_sum(g, r, core, *, name):
    shape = r.shape
    cols = shape[-1]
    g4 = g.reshape(4, 2, -1, cols)
    r3 = r.reshape(4, -1, cols)
    rows = r3.shape[1]
    tr = _row_tile(rows, cols)

    def body(c_ref, g_ref, r_ref, o_ref):
        o_ref[...] = (g_ref[...].astype(F32) + r_ref[...].astype(F32)).astype(o_ref.dtype)

    out = pl.pallas_call(
        body, name=name,
        grid_spec=pltpu.PrefetchScalarGridSpec(
            num_scalar_prefetch=1, grid=(4, rows // tr),
            in_specs=[pl.BlockSpec((None, None, tr, cols), lambda j, i, c: (j, c[0], i, 0)),
                      pl.BlockSpec((None, tr, cols), lambda j, i, c: (j, i, 0))],
            out_specs=pl.BlockSpec((None, tr, cols), lambda j, i, c: (j, i, 0))),
        out_shape=jax.ShapeDtypeStruct(r3.shape, BF16), compiler_params=_cparams(("parallel", "parallel")),
    )(core.reshape(1).astype(jnp.int32), g4, r3)
    return out.reshape(shape)


def mesh_sum(p, r, chip, core, *, name):
    shape = p.shape[1:]
    cols = shape[-1]
    p3 = p.reshape(4, -1, cols)
    r3 = r.reshape(3, -1, cols)
    rows = p3.shape[1]
    tr = _row_tile(rows, 2 * cols)

    def body(c_ref, p_ref, r_ref, o_ref):
        o_ref[...] = ((p_ref[...].astype(F32) + r_ref[0].astype(F32)) + r_ref[1].astype(F32)) + r_ref[2].astype(F32)

    out = pl.pallas_call(
        body, name=name,
        grid_spec=pltpu.PrefetchScalarGridSpec(
            num_scalar_prefetch=1, grid=(rows // tr,),
            in_specs=[pl.BlockSpec((None, tr, cols), lambda i, c: (c[0], i, 0)),
                      pl.BlockSpec((3, tr, cols), lambda i, c: (0, i, 0))],
            out_specs=pl.BlockSpec((None, tr, cols), lambda i, c: (c[1], i, 0))),
        out_shape=jax.ShapeDtypeStruct((2, rows, cols), F32), compiler_params=_cparams(("parallel",)),
    )(jnp.stack([chip, core]).astype(jnp.int32), p3, r3)
    return out.reshape((2,) + shape)


def sum_leading(x, *, name):
    k, r, c = x.shape
    tr = _row_tile(r, c * k)
    def body(x_ref, o_ref):
        acc = x_ref[0]
        for i in range(1, k):
            acc = acc + x_ref[i]
        o_ref[...] = acc

    return pl.pallas_call(
        body, name=name, grid=(r // tr,), in_specs=[pl.BlockSpec((k, tr, c), lambda i: (0, i, 0))],
        out_specs=pl.BlockSpec((tr, c), lambda i: (i, 0)),
        out_shape=jax.ShapeDtypeStruct((r, c), F32), compiler_params=_cparams(("parallel",)),
    )(x)


def adamw(w, gs, m, v, *, name):
    shape = w.shape
    w3, m3, v3 = (a.reshape((-1,) + a.shape[-2:]) for a in (w, m, v))
    nl, r, c = w3.shape
    assert len(gs) == nl
    tr = _row_tile(r, 2 * c)
    tc = c
    if tr == r and r * c > (1 << 19):
        tc = next(t for t in (1024, 512, 256, 128) if c % t == 0 and r * t <= (1 << 19))
    c1 = 1.0 - ADAM_B1 ** ADAM_STEP
    c2 = 1.0 - ADAM_B2 ** ADAM_STEP

    def body(w_ref, g_ref, m_ref, v_ref, *rest):
        go_ref, d_ref, mo_ref, vo_ref = rest[-4:]
        gv = g_ref[...]
        mn = ADAM_B1 * m_ref[...] + (1.0 - ADAM_B1) * gv
        vn = ADAM_B2 * v_ref[...] + (1.0 - ADAM_B2) * (gv * gv)
        go_ref[...] = gv
        d_ref[...] = -ADAM_LR * ((mn / c1) / (jnp.sqrt(vn / c2) + ADAM_EPS) + ADAM_WD * w_ref[...])
        mo_ref[...] = mn
        vo_ref[...] = vn

    outs = None
    for l, g in enumerate(gs):
        layer = pl.BlockSpec((None, tr, tc), lambda i, j, l=l: (l, i, j))
        prev = [] if outs is None else list(outs)
        outs = pl.pallas_call(
            functools.partial(body), name=f"{name}_{l}", grid=(r // tr, c // tc),
            in_specs=[layer, pl.BlockSpec((tr, tc), lambda i, j: (i, j)), layer, layer] + [ANY] * len(prev),
            out_specs=[layer] * 4, out_shape=[jax.ShapeDtypeStruct((nl, r, c), F32)] * 4,
            input_output_aliases={4 + k: k for k in range(len(prev))},
            compiler_params=_cparams(("parallel", "parallel")),
        )(w3, g.reshape(r, c), m3, v3, *prev)
    return tuple(o.reshape(shape) for o in outs)


ANY = pl.BlockSpec(memory_space=pl.ANY)


def _place():
    x, y, c = lax.axis_index("x"), lax.axis_index("y"), lax.axis_index("c")
    chips = [(1 - x, y), (x, 1 - y), (1 - x, 1 - y)]
    return x, y, c, chips


def gather8(block):
    m, n = block.shape

    def body(x_ref, out_ref, send_sems, recv_sems, local_sem):
        x, y, c, chips = _place()
        me, sibling = (x, y, c), (x, y, 1 - c)

        def rows(px, py, pc):
            return out_ref.at[4 * px + 2 * py + pc]

        def copy(k, blk, to, src=None):
            return pltpu.make_async_remote_copy(
                src_ref=rows(*blk) if src is None else src, dst_ref=rows(*blk), send_sem=send_sems.at[k],
                recv_sem=recv_sems.at[k], device_id=to, device_id_type=MESH)

        mine = pltpu.make_async_copy(x_ref, rows(*me), local_sem)
        mine.start()
        first = [copy(0, me, sibling, src=x_ref)]
        first += [copy(1 + j, me, (*chip, c), src=x_ref) for j, chip in enumerate(chips)]
        for cp in first:
            cp.start()
        passed = [copy(4 + j, (*chip, c), sibling) for j, chip in enumerate(chips)]
        for j, chip in enumerate(chips):
            copy(1 + j, (*chip, c), me).wait_recv()
            passed[j].start()
        copy(0, sibling, me).wait_recv()
        for j, chip in enumerate(chips):
            copy(4 + j, (*chip, 1 - c), me).wait_recv()
        for cp in first + passed:
            cp.wait_send()
        mine.wait()

    return pl.pallas_call(
        body, name="gather8",
        out_shape=jax.ShapeDtypeStruct((8, m, n), block.dtype),
        in_specs=[pl.BlockSpec(memory_space=pltpu.VMEM)],
        out_specs=pl.BlockSpec(memory_space=pltpu.VMEM),
        scratch_shapes=[pltpu.SemaphoreType.DMA((7,)), pltpu.SemaphoreType.DMA((7,)), pltpu.SemaphoreType.DMA],
        compiler_params=pltpu.CompilerParams(vmem_limit_bytes=VMEM_LIMIT),
    )(block)


def gather_weights(shards):
    n = len(shards)

    def copy(ins, outs, send, recv, base, a, k, chip_idx, half, to, src=None):
        dst = outs[a].at[chip_idx, half]
        return pltpu.make_async_remote_copy(
            src_ref=dst if src is None else src, dst_ref=dst, send_sem=send.at[base + 6 * a + k],
            recv_sem=recv.at[base + 6 * a + k], device_id=to, device_id_type=MESH)

    def first(ins, outs, send, recv, base):
        x, y, c, chips = _place()
        return [copy(ins, outs, send, recv, base, a, j, 2 * x + y, c, (*chip, c), src=ins[a].at[c])
                for a in range(n) for j, chip in enumerate(chips)]

    def start(ins, outs, send, recv, base):
        for cp in first(ins, outs, send, recv, base):
            cp.start()

    def finish(ins, outs, send, recv, base):
        x, y, c, chips = _place()
        sibling = (x, y, 1 - c)
        passed = []
        for a in range(n):
            for j, (cx, cy) in enumerate(chips):
                copy(ins, outs, send, recv, base, a, j, 2 * cx + cy, c, (cx, cy, c)).wait_recv()
                fw = copy(ins, outs, send, recv, base, a, 3 + j, 2 * cx + cy, c, sibling)
                fw.start()
                passed.append(fw)
        for a in range(n):
            for j, (cx, cy) in enumerate(chips):
                copy(ins, outs, send, recv, base, a, 3 + j, 2 * cx + cy, 1 - c, sibling).wait_recv()
        for cp in first(ins, outs, send, recv, base) + passed:
            cp.wait_send()

    return Carry(shards, [jax.ShapeDtypeStruct((4,) + s.shape, s.dtype) for s in shards], 6 * n, start, finish)


def exchange_halves(grads):
    n = len(grads)

    def copies(ins, outs, send, recv, base):
        x, y, c, _ = _place()
        return [pltpu.make_async_remote_copy(
            src_ref=ins[a].at[j, 1 - c], dst_ref=outs[a].at[j], send_sem=send.at[base + 4 * a + j],
            recv_sem=recv.at[base + 4 * a + j], device_id=(x, y, 1 - c), device_id_type=MESH)
            for a in range(n) for j in range(4)]

    def start(*args):
        for cp in copies(*args):
            cp.start()

    def finish(*args):
        for cp in copies(*args):
            cp.wait()

    return Carry(grads, [jax.ShapeDtypeStruct((4,) + g.shape[2:], g.dtype) for g in grads], 4 * n, start, finish)


def scatter_chips(parts, rows=None, into=None):
    n = len(parts)
    sl = (lambda r: r) if rows is None else (lambda r: r.at[pl.ds(rows[0], rows[1])])

    def copies(ins, outs, send, recv, base):
        x, y, c, chips = _place()
        return [pltpu.make_async_remote_copy(
            src_ref=sl(ins[a].at[2 * cx + cy]), dst_ref=sl(outs[a].at[j]), send_sem=send.at[base + 3 * a + j],
            recv_sem=recv.at[base + 3 * a + j], device_id=(cx, cy, c), device_id_type=MESH)
            for a in range(n) for j, (cx, cy) in enumerate(chips)]

    def start(*args):
        for cp in copies(*args):
            cp.start()

    def finish(*args):
        for cp in copies(*args):
            cp.wait()

    shapes = [jax.ShapeDtypeStruct((3,) + p.shape[1:], p.dtype) for p in parts]
    if into is None:
        return Carry(parts, shapes, 3 * n, start, finish)
    return Carry(list(parts) + list(into), shapes, 3 * n, start, finish, aliases={n + a: a for a in range(n)})


def join_halves(bufs):
    n = len(bufs)

    def copy(outs, send, recv, base, a, half):
        x, y, c, _ = _place()
        return pltpu.make_async_remote_copy(
            src_ref=outs[a].at[c], dst_ref=outs[a].at[c if half is None else half], send_sem=send.at[base + a],
            recv_sem=recv.at[base + a], device_id=(x, y, 1 - c), device_id_type=MESH)

    def start(ins, outs, send, recv, base):
        for a in range(n):
            copy(outs, send, recv, base, a, None).start()

    def finish(ins, outs, send, recv, base):
        c = lax.axis_index("c")
        for a in range(n):
            copy(outs, send, recv, base, a, 1 - c).wait_recv()
        for a in range(n):
            copy(outs, send, recv, base, a, None).wait_send()

    return Carry(bufs, [jax.ShapeDtypeStruct(h.shape, h.dtype) for h in bufs], n, start, finish,
                 aliases={a: a for a in range(n)})


def run_comm(carry, *, name):
    k_in, k_out = len(carry.arrays), len(carry.out_shape)

    def body(*refs):
        ins, outs = refs[:k_in], refs[k_in:k_in + k_out]
        send, recv = refs[-2:]
        carry.start(ins, outs, send, recv, 0)
        carry.finish(ins, outs, send, recv, 0)

    return pl.pallas_call(
        body, name=name, out_shape=carry.out_shape, in_specs=[ANY] * k_in, out_specs=[ANY] * k_out,
        scratch_shapes=[pltpu.SemaphoreType.DMA((carry.n_sems,))] * 2, input_output_aliases=carry.aliases,
    )(*carry.arrays)


INPUTS = ['x'] + WEIGHTS + ['loss_target'] + ['m_' + n for n in WEIGHTS] + ['v_' + n for n in WEIGHTS]


def _round_up(n, m):
    return -(-n // m) * m


def _pack(arrs):
    flat = jnp.concatenate([a.reshape(-1) for a in arrs])
    n = _round_up(flat.shape[0], 512 * LANE)
    return jnp.pad(flat, (0, n - flat.shape[0])).reshape(-1, LANE)


def _unpack(block, shapes):
    flat = block.reshape(-1)
    out, o = [], 0
    for s in shapes:
        n = math.prod(s)
        out.append(flat[o:o + n].reshape(s))
        o += n
    return out


def _cols(g):
    return g.transpose(1, 0, 2).reshape(g.shape[1], -1)


def _uncols(w):
    return w.reshape(w.shape[0], 4, -1).transpose(1, 0, 2)


def _pad_cols(w, total):
    return jnp.pad(w, ((0, 0), (0, total - w.shape[1])))


def kernel(*args):
    a = dict(zip(INPUTS, args))
    x, tgt = a['x'][0], a['loss_target'][0]
    t, d = x.shape
    xi, yi, ci = lax.axis_index("x"), lax.axis_index("y"), lax.axis_index("c")
    chip = 2 * xi + yi
    dk, dv, hk, hv = _gla_dims(d)
    lw = d // 2
    di = 2 * d
    nh = di // SSD_HEAD_DIM
    gn_w = SSD_GROUPS * SSD_STATE
    conv_dim = di + 2 * gn_w
    rank = GLA_GATE_RANK
    wq = 2 * dk + 2 * dv
    gla_w = _round_up(wq + LANE, 2 * lw)
    ev_tot = gla_w + 2 * lw
    od_used = di + conv_dim + nh
    od_tot = _round_up(di + conv_dim + _round_up(nh, LANE), 512)
    glr_col, xcol, dtcol = wq // LANE, gla_w // lw, (di + conv_dim) // LANE
    assert ev_tot % 512 == 0 and nh <= LANE

    def halves(w):
        w = w.astype(BF16)
        return w.reshape((2, w.shape[0] // 2) + w.shape[1:])

    own = {'ev_w_in': halves(a['ev_w_in'][0]), 'ev_w_out': halves(a['ev_w_out'][0]),
           'od_w_in_a': halves(a['od_w_in'][0][:d // 2]), 'od_w_in_b': halves(a['od_w_in'][0][d // 2:]),
           'od_w_out': halves(a['od_w_out'][0])}
    for l in range(2):
        own[f'gate{l}'], own[f'up{l}'] = halves(a['ffn_w_gate'][l]), halves(a['ffn_w_up'][l])
        own[f'down{l}'] = halves(a['ffn_w_down'][l])

    def gather(*units):
        return gather_weights([own[u] for u in units])

    def filled(unit, g):
        g = lax.dynamic_update_index_in_dim(g, own[unit], chip, 0)
        return g.reshape((4, 2 * g.shape[2]) + g.shape[3:])

    g_ev_in, g_ev_out = run_comm(gather('ev_w_in', 'ev_w_out'), name="gather_ev")
    w_ev_in = _cols(filled('ev_w_in', g_ev_in))
    cuts = [dk, 2 * dk, 2 * dk + dv, wq, wq + rank, wq + rank + lw]
    sq, sk, sv, sg, sglr, sxb, sgb = jnp.split(w_ev_in, cuts, axis=1)
    w_ev_in_p = jnp.concatenate([_pad_cols(jnp.concatenate([sq, sk, sv, sg, sglr], axis=1), gla_w), sxb, sgb], axis=1)
    w_ev_out = filled('ev_w_out', g_ev_out).reshape(-1, d)
    w_gate, w_up, w_down = [None, None], [None, None], [None, None]

    sh_names = list(SMALL_SHARDED)
    sh_shapes = [a[n].shape for n in sh_names]
    g8 = gather8(_pack([a[n] for n in sh_names]))
    per_chip = [_unpack(g8[2 * j], sh_shapes) for j in range(4)]
    full = {n: jnp.concatenate([per_chip[j][i] for j in range(4)], axis=SMALL_SHARDED[n])
            for i, n in enumerate(sh_names)}

    wg_p = jnp.zeros((LANE, dk), F32).at[:rank].set(full['ev_gla_w_gate'][0])
    bg, wn = a['ev_gla_b_gate'], a['ev_gla_w_onorm']
    lru_p = [full['ev_lru_conv_w'][0], a['ev_lru_conv_b'], a['ev_lru_w_a'][0], a['ev_lru_b_a'], a['ev_lru_w_i'][0],
             a['ev_lru_b_i'], a['ev_lru_lam']]
    od_cw, od_cb, od_gn = full['od_conv_w'][0], full['od_conv_b'], full['od_gnorm']
    heads = jnp.arange(LANE)[:, None]
    e_mat = ((jnp.arange(di)[None, :] // SSD_HEAD_DIM == heads) & (heads < nh)).astype(BF16)
    row8 = lambda p: jnp.zeros((SUBLANE, LANE), F32).at[0, :nh].set(p[0])
    dt_bias_p = jnp.zeros((1, LANE), F32).at[0, :nh].set(a['od_dt_bias'][0])
    alog_e = head_expand(row8(a['od_a_log']), e_mat, name="expand_a_log")
    dskip_e = head_expand(row8(a['od_d_skip']), e_mat, name="expand_d_skip")

    h0 = rms_fwd(x, a['ev_norm'], name="rms_ev")
    proj, (g,) = matmul(h0, w_ev_in_p, name="ev_in", carry=gather('gate0'))
    w_gate[0] = filled('gate0', g)
    (o_gla, sp_gla), (g,) = gla_fwd(proj, glr_col, wg_p, bg, wn, dv, carry=gather('up0'))
    w_up[0] = filled('up0', g)
    o_lru, hin = lru_fwd(proj, xcol, lw, *lru_p)
    x1 = matmul(o_gla, w_ev_out[:dv], add=x, name="ev_out_a")
    x1 = matmul(o_lru, w_ev_out[dv:], add=x1, name="ev_out_b")

    h1 = rms_fwd(x1, a['ffn_norm'][0:1], name="rms_ffn0")
    gate0, (g,) = matmul(h1, w_gate[0], name="ffn0_gate", carry=gather('down0'))
    w_down[0] = filled('down0', g).reshape(-1, d)
    swi = dict(epi=_swi_fwd_epi, epi_out=(F32, BF16))
    (up0, act0), (g_a,) = matmul(h1, w_up[0], name="ffn0_up", epi_in=(gate0,), carry=gather('od_w_in_a'), **swi)
    x2, (g_b,) = matmul(act0, w_down[0], add=x1, name="ffn0_down", carry=gather('od_w_in_b'))
    w_od_in = jnp.concatenate([filled('od_w_in_a', g_a), filled('od_w_in_b', g_b)], axis=1)
    w_od_in_p = _pad_cols(_cols(w_od_in), od_tot)

    h2 = rms_fwd(x2, full['od_norm'], name="rms_od")
    proj2, (g, g1) = matmul(h2, w_od_in_p, name="od_in", carry=gather('od_w_out', 'gate1'))
    w_od_out, w_gate[1] = filled('od_w_out', g).reshape(-1, d), filled('gate1', g1)
    xs, (g,) = conv_silu_fwd(proj2, di, conv_dim, od_cw, od_cb, carry=gather('up1'))
    w_up[1] = filled('up1', g)
    dte = dt_fwd(proj2, dtcol, dt_bias_p, e_mat)
    (y_ssd, sp_ssd), (g,) = ssd_fwd(xs, proj2, dte, alog_e, dskip_e, od_gn, carry=gather('down1'))
    w_down[1] = filled('down1', g).reshape(-1, d)
    x3 = matmul(y_ssd, w_od_out, add=x2, name="od_out")
    h3 = rms_fwd(x3, a['ffn_norm'][1:2], name="rms_ffn1")
    gate1 = matmul(h3, w_gate[1], name="ffn1_gate")
    up1, act1 = matmul(h3, w_up[1], name="ffn1_up", epi_in=(gate1,), **swi)
    x4 = matmul(act1, w_down[1], add=x3, name="ffn1_down")
    loss_p, dx4, dx4b, d_final = loss_head(x4, a['final_norm'][None], tgt, name="loss_head")

    grads, from_sib, part, from_chips = {}, {}, {}, {}

    def rows4(dw):
        return dw.reshape((4, 2, dw.shape[0] // 8) + dw.shape[1:])

    def cols4(dw):
        return dw.reshape((4, 2, dw.shape[1] // 2) + dw.shape[2:])

    def exchange(*units):
        return exchange_halves([grads[u] for u in units])

    def scatter(*units):
        return scatter_chips([part[u] for u in units])

    def sum_chip(u):
        part[u] = chip_sum(grads[u], from_sib[u], ci, name=f"chip_sum_{u}")

    def ffn_bwd(dxo, dxob, xin, h, gate, up, act, l, first_carry, first_units):
        dn, gt, up_ = f'down{l}', f'gate{l}', f'up{l}'
        dgu = matmul(dxob, w_down[l], tb=True, name=f"ffn{l}_d_act", carry=first_carry, epi=_swi_bwd_epi,
                     epi_in=(gate, up), epi_out=(BF16, BF16))
        (dg, du), got = dgu if first_units else (dgu, ())
        for u, r in zip(first_units, got):
            from_sib[u] = r
            sum_chip(u)
        piece = [part[u].shape[1] // 2 for u in first_units]
        d_down = matmul(act, dxob, ta=True, out_dtype=BF16, name=f"ffn{l}_dw_down",
                        carry=scatter_chips([part[u] for u in first_units], rows=(0, piece[0])) if first_units else None)
        if first_units:
            d_down, first_rb = d_down
        grads[dn] = rows4(d_down)
        second = scatter_chips([part[u] for u in first_units], rows=(piece[0], piece[0]), into=first_rb) \
            if first_units else None
        dh, got = matmul_pair(dg, w_gate[l], du, w_up[l], name=f"ffn{l}_dh", carry=merge_carries(exchange(dn), second))
        from_sib[dn] = got[0]
        if first_units:
            from_chips[first_units[0]] = got[1]
        sum_chip(dn)
        d_gate, (from_chips[dn],) = matmul(h, dg, ta=True, out_dtype=BF16, out_shards=True, name=f"ffn{l}_dw_gate",
                                           carry=scatter(dn))
        grads[gt] = cols4(d_gate)
        d_up, (from_sib[gt],) = matmul(h, du, ta=True, out_dtype=BF16, out_shards=True, name=f"ffn{l}_dw_up",
                                       carry=exchange(gt))
        grads[up_] = cols4(d_up)
        return rms_bwd(xin, a['ffn_norm'][l:l + 1], dh, dxo, name=f"rms_ffn{l}_bwd")

    dx3, dx3b, d_fn1 = ffn_bwd(dx4, dx4b, x3, h3, gate1, up1, act1, 1, None, ())
    dy, (from_sib['up1'],) = matmul(dx3b, w_od_out, tb=True, name="od_out_dy", carry=exchange('up1'))
    sum_chip('gate1')
    sum_chip('up1')
    grads['od_w_out'] = rows4(matmul(y_ssd, dx3b, ta=True, out_dtype=BF16, name="od_out_dw"))
    (dxs, dproj2, ddte, dal, dds, dgn), (from_chips['gate1'], from_chips['up1'], from_sib['od_w_out']) = ssd_bwd(
        xs, proj2, dte, alog_e, dskip_e, od_gn, sp_ssd, dy, proj2.shape,
        carry=merge_carries(scatter('gate1', 'up1'), exchange('od_w_out')))
    sum_chip('od_w_out')
    (dproj2, d_od_cw, d_od_cb), (from_chips['od_w_out'],) = conv_silu_bwd(
        proj2, di, conv_dim, od_cw, od_cb, dxs, dproj2, carry=scatter('od_w_out'))
    dproj2, d_dt_bias = dt_bwd(proj2, dtcol, dt_bias_p, e_mat, ddte, dproj2)
    dh2 = matmul(dproj2, w_od_in_p, tb=True, name="od_in_dh")
    d_od_in = matmul(h2, dproj2, ta=True, out_dtype=BF16, name="od_in_dw")[:, :od_used]
    grads['od_w_in'] = cols4(_uncols(d_od_in))
    dx2, dx2b, d_od_norm = rms_bwd(x2, full['od_norm'], dh2, dx3, name="rms_od_bwd")
    to8 = lambda acc: jnp.zeros((SUBLANE, di), F32).at[0].set(acc.reshape(-1))
    d_a_log = head_expand(to8(dal), e_mat, transpose=True, name="reduce_a_log")[0:1, :nh]
    d_d_skip = head_expand(to8(dds), e_mat, transpose=True, name="reduce_d_skip")[0:1, :nh]

    dx1, dx1b, d_fn0 = ffn_bwd(dx2, dx2b, x1, h1, gate0, up0, act0, 0, exchange('od_w_in'), ('od_w_in',))
    dmix, (from_sib['up0'],) = matmul(dx1b, w_ev_out, tb=True, name="ev_out_dmix", carry=exchange('up0'))
    sum_chip('gate0')
    sum_chip('up0')
    grads['ev_w_out'] = rows4(jnp.concatenate([matmul(o_gla, dx1b, ta=True, out_dtype=BF16, name="ev_out_dw_a"),
                                               matmul(o_lru, dx1b, ta=True, out_dtype=BF16, name="ev_out_dw_b")], axis=0))
    (dproj, d_wg, d_bg, d_wn), (from_sib['ev_w_out'],) = gla_bwd(
        proj, glr_col, wg_p, bg, wn, sp_gla, dmix, dv, gla_w, carry=exchange('ev_w_out'))
    sum_chip('ev_w_out')
    (dproj, *d_lru), (from_chips['ev_w_out'], from_chips['gate0']) = lru_bwd(
        proj, xcol, lw, *lru_p, hin, dmix, 1, dproj, carry=scatter('ev_w_out', 'gate0'))
    d_ev_in_p, (from_chips['up0'],) = matmul(h0, dproj, ta=True, out_dtype=BF16, name="ev_in_dw", carry=scatter('up0'))
    d_ev_in = jnp.concatenate([d_ev_in_p[:, :wq + rank], d_ev_in_p[:, gla_w:]], axis=1)
    grads['ev_w_in'] = cols4(_uncols(d_ev_in))
    (from_sib['ev_w_in'],) = run_comm(exchange('ev_w_in'), name="exchange_ev_in")
    sum_chip('ev_w_in')
    dh0, (from_chips['ev_w_in'],) = matmul(dproj, w_ev_in_p, tb=True, name="ev_in_dh", carry=scatter('ev_w_in'))
    dx0, _, d_ev_norm = rms_bwd(x, a['ev_norm'], dh0, dx1, name="rms_ev_bwd")

    units = list(grads)
    half = [mesh_sum(part[u], from_chips[u], chip, ci, name=f"mesh_sum_{u}") for u in units]
    done = dict(zip(units, run_comm(join_halves(half), name="join_halves")))
    layers = {n: [done[n]] for n in ('ev_w_in', 'ev_w_out', 'od_w_in', 'od_w_out')}
    layers.update({f'ffn_w_{u}': [done[f'{u}0'], done[f'{u}1']] for u in ('gate', 'up', 'down')})
    grad, delta, new_m, new_v = {}, {}, {}, {}
    for n in BIG:
        flip = a[n].shape[-1] % LANE != 0
        tr_ = (lambda t: jnp.swapaxes(t, -1, -2)) if flip else (lambda t: t)
        gs = [tr_(g.reshape(a[n].shape[-2:])) for g in layers[n]]
        outs = adamw(tr_(a[n]), gs, tr_(a['m_' + n]), tr_(a['v_' + n]), name=f"adamw_{n}")
        grad[n], delta[n], new_m[n], new_v[n] = (tr_(o) for o in outs)

    small_g = {
        'ev_norm': d_ev_norm, 'ev_gla_w_gate': d_wg[:rank][None], 'ev_gla_b_gate': d_bg, 'ev_gla_w_onorm': d_wn,
        'ev_lru_conv_w': d_lru[0][None], 'ev_lru_conv_b': d_lru[1], 'ev_lru_w_a': d_lru[2][None],
        'ev_lru_b_a': d_lru[3], 'ev_lru_w_i': d_lru[4][None], 'ev_lru_b_i': d_lru[5], 'ev_lru_lam': d_lru[6],
        'od_norm': d_od_norm, 'od_conv_w': d_od_cw[None], 'od_conv_b': d_od_cb, 'od_dt_bias': d_dt_bias[:, :nh],
        'od_a_log': d_a_log, 'od_d_skip': d_d_skip, 'od_gnorm': dgn.reshape(1, di),
        'ffn_norm': jnp.concatenate([d_fn0, d_fn1], axis=0), 'final_norm': d_final[0],
    }
    full_shapes = [small_g[n].shape for n in SMALL]
    summed = sum_leading(gather8(_pack([small_g[n] for n in SMALL])), name="sum_devices")
    for n, g in zip(SMALL, _unpack(summed, full_shapes)):
        if n in SMALL_SHARDED:
            ax = SMALL_SHARDED[n]
            sz = a[n].shape[ax]
            g = lax.dynamic_slice_in_dim(g, chip * sz, sz, axis=ax)
        grad[n] = g

    shapes = [a[n].shape for n in SMALL]
    packed = [_pack([src[n] if pre is None else a[pre + n] for n in SMALL])
              for src, pre in ((a, None), (grad, None), (None, 'm_'), (None, 'v_'))]
    small_out = adamw(packed[0], [packed[1]], packed[2], packed[3], name="adamw_small")
    for outd, blk in zip((delta, new_m, new_v), small_out[1:]):
        outd.update(zip(SMALL, _unpack(blk, shapes)))

    loss = lax.psum(loss_p[0, 0], ("x", "y", "c"))
    return (loss, dx0[None], *[grad[n] for n in WEIGHTS], *[delta[n] for n in WEIGHTS],
            *[new_m[n] for n in WEIGHTS], *[new_v[n] for n in WEIGHTS])
```

```python
import functools
import math

import jax
import jax.numpy as jnp
from jax import lax
from jax.experimental import pallas as pl
from jax.experimental.pallas import tpu as pltpu

F32 = jnp.float32
BF16 = jnp.bfloat16
MXU_DTYPE = jnp.bfloat16

NORM_EPS = 1e-6
CONV_WIDTH = 4
GLA_HEADS = 4
GLA_GATE_RANK = 16
GLA_GATE_NORM = 16.0
CHUNK = 64
LRU_BLOCK = 128
LRU_C = 8.0
SSD_HEAD_DIM = 64
SSD_GROUPS = 8
SSD_STATE = 128
ADAM_LR, ADAM_B1, ADAM_B2, ADAM_EPS, ADAM_WD, ADAM_STEP = 0.001, 0.9, 0.999, 1e-08, 0.01, 10

LANE = 128
SUBLANE = 8
VMEM_LIMIT = 48 * 1024 * 1024
MAX_TK = 2816
MATMUL_VMEM_BUDGET = 45 * 1024 * 1024
MESH = pl.DeviceIdType.MESH

WEIGHTS = ['ev_norm', 'ev_w_in', 'ev_gla_w_gate', 'ev_gla_b_gate', 'ev_gla_w_onorm', 'ev_lru_conv_w', 'ev_lru_conv_b',
           'ev_lru_w_a', 'ev_lru_b_a', 'ev_lru_w_i', 'ev_lru_b_i', 'ev_lru_lam', 'ev_w_out', 'od_norm', 'od_w_in',
           'od_conv_w', 'od_conv_b', 'od_dt_bias', 'od_a_log', 'od_d_skip', 'od_gnorm', 'od_w_out', 'ffn_norm',
           'ffn_w_gate', 'ffn_w_up', 'ffn_w_down', 'final_norm']
BIG = ['ev_w_in', 'ev_w_out', 'od_w_in', 'od_w_out', 'ffn_w_gate', 'ffn_w_up', 'ffn_w_down']
SMALL_SHARDED = {'ev_gla_w_gate': 2, 'ev_lru_conv_w': 2, 'od_norm': 1, 'od_conv_w': 2, 'od_conv_b': 1, 'od_gnorm': 1}
SMALL = [n for n in WEIGHTS if n not in BIG]


def _cparams(sem=None, **kw):
    return pltpu.CompilerParams(dimension_semantics=sem, vmem_limit_bytes=VMEM_LIMIT, **kw)


def _full(shape):
    n = len(shape)
    return pl.BlockSpec(shape, lambda *_: (0,) * n)


ANY = pl.BlockSpec(memory_space=pl.ANY)


class Carry:
    def __init__(self, arrays, out_shape, n_sems, start, finish, aliases=None):
        self.arrays, self.out_shape, self.n_sems = list(arrays), list(out_shape), n_sems
        self.start, self.finish, self.aliases = start, finish, dict(aliases or {})


def merge_carries(*cs):
    cs = [c for c in cs if c is not None]
    if not cs:
        return None
    arrays = [a for c in cs for a in c.arrays]
    out_shape = [s for c in cs for s in c.out_shape]
    offs, i0, o0, s0 = [], 0, 0, 0
    aliases = {}
    for c in cs:
        offs.append((i0, o0, s0))
        aliases.update({i0 + i: o0 + o for i, o in c.aliases.items()})
        i0, o0, s0 = i0 + len(c.arrays), o0 + len(c.out_shape), s0 + c.n_sems

    def both(which):
        def run(ins, outs, send, recv, base):
            for c, (i, o, s) in zip(cs, offs):
                getattr(c, which)(ins[i:i + len(c.arrays)], outs[o:o + len(c.out_shape)], send, recv, base + s)
        return run

    return Carry(arrays, out_shape, s0, both("start"), both("finish"), aliases)


def _pcall(body, *, name, grid, in_specs, out_specs, out_shape, scratch_shapes=(), compiler_params, carry=None,
           input_output_aliases=None):
    aliases = dict(input_output_aliases or {})
    if carry is None:
        return pl.pallas_call(body, name=name, grid=grid, in_specs=in_specs, out_specs=out_specs, out_shape=out_shape,
                              scratch_shapes=list(scratch_shapes), compiler_params=compiler_params,
                              input_output_aliases=aliases)
    single = not isinstance(out_specs, (list, tuple))
    specs_o = [out_specs] if single else list(out_specs)
    shapes_o = [out_shape] if single else list(out_shape)
    n_in, n_out, k_in, k_out, n_scr = len(in_specs), len(specs_o), len(carry.arrays), len(carry.out_shape), len(scratch_shapes)

    def wrapped(*refs):
        ins, cins = refs[:n_in], refs[n_in:n_in + k_in]
        o0 = n_in + k_in
        outs, couts = refs[o0:o0 + n_out], refs[o0 + n_out:o0 + n_out + k_out]
        scr = refs[o0 + n_out + k_out:o0 + n_out + k_out + n_scr]
        send, recv = refs[-2:]
        ids = [pl.program_id(ax) for ax in range(len(grid))]
        first = functools.reduce(jnp.logical_and, [i == 0 for i in ids])
        last = functools.reduce(jnp.logical_and, [i == g - 1 for i, g in zip(ids, grid)])

        @pl.when(first)
        def _():
            carry.start(cins, couts, send, recv, 0)

        body(*ins, *outs, *scr)

        @pl.when(last)
        def _():
            carry.finish(cins, couts, send, recv, 0)

    aliases.update({n_in + i: n_out + o for i, o in carry.aliases.items()})
    call = pl.pallas_call(
        wrapped, name=name, grid=grid, in_specs=list(in_specs) + [ANY] * k_in, out_specs=specs_o + [ANY] * k_out,
        out_shape=shapes_o + carry.out_shape,
        scratch_shapes=list(scratch_shapes) + [pltpu.SemaphoreType.DMA((carry.n_sems,))] * 2,
        compiler_params=_cparams(("arbitrary",) * len(grid)), input_output_aliases=aliases)

    def run(*args):
        res = call(*args, *carry.arrays)
        main = res[:n_out]
        return (main[0] if single else list(main)), list(res[n_out:])

    return run


def _pick(dim, cands):
    for c in cands:
        if dim % c == 0:
            return c
    return dim


def _dims(a, ca, cb):
    nb = a.ndim - 2
    return (((ca + nb,), (cb + nb,)), (tuple(range(nb)), tuple(range(nb))))


def _dot(a, b, ca, cb):
    return lax.dot_general(a.astype(MXU_DTYPE), b.astype(MXU_DTYPE), _dims(a, ca, cb), preferred_element_type=F32)


@jax.custom_vjp
def mm(a, b):
    return _dot(a, b, 1, 0)


def _mm_f(a, b):
    return mm(a, b), (a, b)


def _mm_b(res, g):
    a, b = res
    return mm_nt(g, b).astype(a.dtype), mm_tn(a, g).astype(b.dtype)


@jax.custom_vjp
def mm_nt(a, b):
    return _dot(a, b, 1, 1)


def _mm_nt_f(a, b):
    return mm_nt(a, b), (a, b)


def _mm_nt_b(res, g):
    a, b = res
    return mm(g, b).astype(a.dtype), mm_tn(g, a).astype(b.dtype)


@jax.custom_vjp
def mm_tn(a, b):
    return _dot(a, b, 0, 0)


def _mm_tn_f(a, b):
    return mm_tn(a, b), (a, b)


def _mm_tn_b(res, g):
    a, b = res
    return mm_nt(b, g).astype(a.dtype), mm(a, g).astype(b.dtype)


mm.defvjp(_mm_f, _mm_b)
mm_nt.defvjp(_mm_nt_f, _mm_nt_b)
mm_tn.defvjp(_mm_tn_f, _mm_tn_b)


def _split3(a):
    h = a.astype(BF16)
    r = a - h.astype(F32)
    m = r.astype(BF16)
    l = (r - m.astype(F32)).astype(BF16)
    return h, m, l


def _exact_dot(t, a, ca, cb):
    out = None
    for p in _split3(a):
        d = lax.dot_general(t, p, _dims(a, ca, cb), preferred_element_type=F32)
        out = d if out is None else out + d
    return out


@jax.custom_vjp
def sel_l(t, a):
    return _exact_dot(t, a, 1, 0)


def _sel_l_f(t, a):
    return sel_l(t, a), t


def _sel_l_b(t, g):
    return jnp.zeros_like(t), _exact_dot(t, g, 0, 0)


sel_l.defvjp(_sel_l_f, _sel_l_b)


@jax.custom_vjp
def sel_r(a, t):
    out = None
    for p in _split3(a):
        d = lax.dot_general(p, t, (((1,), (0,)), ((), ())), preferred_element_type=F32)
        out = d if out is None else out + d
    return out


def _sel_r_f(a, t):
    return sel_r(a, t), t


def _sel_r_b(t, g):
    out = None
    for p in _split3(g):
        d = lax.dot_general(p, t, (((1,), (1,)), ((), ())), preferred_element_type=F32)
        out = d if out is None else out + d
    return out, jnp.zeros_like(t)


sel_r.defvjp(_sel_r_f, _sel_r_b)


def _sigmoid(x):
    return 1.0 / (1.0 + jnp.exp(-x))


def _silu(x):
    return x * _sigmoid(x)


def _softplus(x):
    return jnp.maximum(x, 0.0) + jnp.log(1.0 + jnp.exp(-jnp.abs(x)))


def _log_sigmoid(x):
    return -_softplus(-x)


def _gelu_tanh(x):
    c = math.sqrt(2.0 / math.pi)
    return 0.5 * x * (1.0 + jnp.tanh(c * (x + 0.044715 * (x * x * x))))


def _rms(x, w):
    return x * lax.rsqrt(jnp.mean(x * x, axis=-1, keepdims=True) + NORM_EPS) * w


def _tri(n, dtype=BF16):
    r = lax.broadcasted_iota(jnp.int32, (n, n), 0)
    c = lax.broadcasted_iota(jnp.int32, (n, n), 1)
    return (c <= r).astype(dtype)


def matmul(a, b, *, ta=False, tb=False, add=None, out_dtype=F32, out_shards=False, carry=None, name,
           epi=None, epi_in=(), epi_out=()):
    m, k = (a.shape[1], a.shape[0]) if ta else a.shape
    b_sh = b.ndim == 3
    if b_sh:
        s, br, bc = b.shape
        k2, n = (s * bc, br) if tb else (br, s * bc)
    else:
        k2, n = (b.shape[1], b.shape[0]) if tb else b.shape
    assert k == k2, (a.shape, b.shape, ta, tb)
    tk_opts = [bc] if b_sh and tb else [k] if k <= MAX_TK else \
        [c for c in range(MAX_TK, LANE - 1, -LANE) if k % c == 0][:1]
    n_add, n_x = int(add is not None), len(epi_in)
    out_dtypes = list(epi_out) if epi is not None else [out_dtype]
    n_o = len(out_dtypes)
    tn_opts = [bc] if b_sh and not tb else [n // 4] if out_shards else \
        [c for c in range(2048, LANE - 1, -LANE) if n % c == 0] or [n]
    tm_opts = [c for c in range(2048, LANE - 1, -LANE) if m % c == 0] or [m]
    sa, sb = a.dtype.itemsize, b.dtype.itemsize
    per_elem = sum(jnp.dtype(dt).itemsize for dt in out_dtypes) + 4 * n_add + sum(x.dtype.itemsize for x in epi_in)
    best = None
    for tk_ in tk_opts:
        for tm_ in tm_opts:
            for tn_ in tn_opts:
                vmem = 2 * (tm_ * tk_ * sa + tk_ * tn_ * sb) + tm_ * tn_ * (4 + 2 * per_elem)
                vmem += tm_ * tn_ * 4
                if vmem > MATMUL_VMEM_BUDGET and (tk_, tm_, tn_) != (tk_opts[-1], tm_opts[-1], tn_opts[-1]):
                    continue
                moved = m * k * sa * (1 if tk_ == k else n // tn_) + k * n * sb * (m // tm_)
                if best is None or (moved, -tk_, -tm_ * tn_) < best[0]:
                    best = ((moved, -tk_, -tm_ * tn_), tm_, tn_, tk_)
    _, tm, tn, tk = best
    nk = k // tk

    def body(*refs):
        a_ref, b_ref = refs[:2]
        x_refs = refs[2 + n_add:2 + n_add + n_x]
        o_refs = refs[2 + n_add + n_x:2 + n_add + n_x + n_o]
        acc = refs[-1]
        kk = pl.program_id(2)

        @pl.when(kk == 0)
        def _():
            acc[...] = jnp.zeros_like(acc)

        acc[...] += _dot(a_ref[...], b_ref[...], 0 if ta else 1, 1 if tb else 0)

        @pl.when(kk == nk - 1)
        def _():
            r = acc[...]
            if add is not None:
                r = r + refs[2][...].astype(F32)
            vals = (r,) if epi is None else epi(r, *[x[...] for x in x_refs])
            for o_ref, v in zip(o_refs, vals):
                o_ref[...] = v.astype(o_ref.dtype)

    a_spec = pl.BlockSpec((tk, tm), lambda i, j, kk: (kk, i)) if ta else pl.BlockSpec((tm, tk), lambda i, j, kk: (i, kk))
    if b_sh and tb:
        b_spec = pl.BlockSpec((None, tn, tk), lambda i, j, kk: (kk, j, 0))
    elif b_sh:
        b_spec = pl.BlockSpec((None, tk, tn), lambda i, j, kk: (j, kk, 0))
    elif tb:
        b_spec = pl.BlockSpec((tn, tk), lambda i, j, kk: (j, kk))
    else:
        b_spec = pl.BlockSpec((tk, tn), lambda i, j, kk: (kk, j))
    in_specs, args = [a_spec, b_spec], [a, b]
    tile = pl.BlockSpec((tm, tn), lambda i, j, kk: (i, j))
    for extra in ([add] if add is not None else []) + list(epi_in):
        in_specs.append(tile)
        args.append(extra)
    if out_shards:
        out_spec = pl.BlockSpec((None, tm, tn), lambda i, j, kk: (j, i, 0))
        out_shape = jax.ShapeDtypeStruct((4, m, tn), out_dtype)
    elif epi is not None:
        out_spec = [tile] * n_o
        out_shape = [jax.ShapeDtypeStruct((m, n), dt) for dt in out_dtypes]
    else:
        out_spec = tile
        out_shape = jax.ShapeDtypeStruct((m, n), out_dtype)
    return _pcall(
        body, name=name, grid=(m // tm, n // tn, nk), in_specs=in_specs, out_specs=out_spec, out_shape=out_shape,
        scratch_shapes=[pltpu.VMEM((tm, tn), F32)],
        compiler_params=_cparams(("parallel", "parallel", "arbitrary")), carry=carry,
    )(*args)


def matmul_pair(a1, b1, a2, b2, *, carry=None, name):
    m, k = a1.shape
    nk, r, bc = b1.shape
    assert a2.shape == a1.shape and b2.shape == b1.shape and k == nk * bc
    tm = _pick(m, (1024, 512, 256, 128))
    tn = _pick(r, (1024, 512, 256, 128))

    def body(a1_ref, b1_ref, a2_ref, b2_ref, o_ref, acc):
        kk = pl.program_id(2)

        @pl.when(kk == 0)
        def _():
            acc[...] = jnp.zeros_like(acc)

        @pl.when(kk < nk)
        def _():
            acc[...] += _dot(a1_ref[...], b1_ref[...], 1, 1)

        @pl.when(kk >= nk)
        def _():
            acc[...] += _dot(a2_ref[...], b2_ref[...], 1, 1)

        @pl.when(kk == 2 * nk - 1)
        def _():
            o_ref[...] = acc[...]

    k1 = lambda kk: jnp.minimum(kk, nk - 1)
    k2 = lambda kk: jnp.maximum(kk - nk, 0)
    return _pcall(
        body, name=name, grid=(m // tm, r // tn, 2 * nk), carry=carry,
        in_specs=[pl.BlockSpec((tm, bc), lambda i, j, kk: (i, k1(kk))),
                  pl.BlockSpec((None, tn, bc), lambda i, j, kk: (k1(kk), j, 0)),
                  pl.BlockSpec((tm, bc), lambda i, j, kk: (i, k2(kk))),
                  pl.BlockSpec((None, tn, bc), lambda i, j, kk: (k2(kk), j, 0))],
        out_specs=pl.BlockSpec((tm, tn), lambda i, j, kk: (i, j)),
        out_shape=jax.ShapeDtypeStruct((m, r), F32), scratch_shapes=[pltpu.VMEM((tm, tn), F32)],
        compiler_params=_cparams(("parallel", "parallel", "arbitrary")),
    )(a1, b1, a2, b2)


def rms_fwd(x, w, *, name):
    t, d = x.shape
    tb = _pick(t, (256, 128, 64))

    def body(x_ref, w_ref, o_ref):
        o_ref[...] = _rms(x_ref[...], w_ref[...]).astype(o_ref.dtype)

    return pl.pallas_call(
        body, name=name, grid=(t // tb,),
        in_specs=[pl.BlockSpec((tb, d), lambda i: (i, 0)), _full((1, d))],
        out_specs=pl.BlockSpec((tb, d), lambda i: (i, 0)),
        out_shape=jax.ShapeDtypeStruct((t, d), BF16),
        compiler_params=_cparams(("parallel",)),
    )(x, w)


def rms_bwd(x, w, dh, dres, *, name, carry=None):
    t, d = x.shape
    tb = _pick(t, (256, 128, 64))

    def body(x_ref, w_ref, dh_ref, dres_ref, dx_ref, dxb_ref, dw_ref):
        @pl.when(pl.program_id(0) == 0)
        def _():
            dw_ref[...] = jnp.zeros_like(dw_ref)

        _, vjp = jax.vjp(_rms, x_ref[...], w_ref[...])
        dx, dw = vjp(dh_ref[...].astype(F32))
        dx = dx + dres_ref[...]
        dx_ref[...] = dx
        dxb_ref[...] = dx.astype(dxb_ref.dtype)
        dw_ref[...] += dw

    row = pl.BlockSpec((tb, d), lambda i: (i, 0))
    return _pcall(
        body, name=name, grid=(t // tb,), carry=carry,
        in_specs=[row, _full((1, d)), row, row],
        out_specs=[row, row, _full((1, d))],
        out_shape=[jax.ShapeDtypeStruct((t, d), F32), jax.ShapeDtypeStruct((t, d), BF16),
                   jax.ShapeDtypeStruct((1, d), F32)],
        compiler_params=_cparams(("arbitrary",)),
    )(x, w, dh, dres)


def _swi(g, u):
    return _silu(g) * u


def _swi_fwd_epi(u, g):
    return u, _swi(g, u)


def _swi_bwd_epi(d, g, u):
    return jax.vjp(_swi, g, u)[1](d)


def loss_head(x, w, target, *, name):
    t, d = x.shape
    tb = _pick(t, (256, 128, 64))

    def f(xv, wv, tv):
        y = _rms(xv, wv)
        e = y - tv
        return 0.5 * jnp.sum(jnp.mean(e * e, axis=-1, keepdims=True), axis=0, keepdims=True)

    def body(x_ref, w_ref, t_ref, l_ref, dx_ref, dxb_ref, dw_ref):
        @pl.when(pl.program_id(0) == 0)
        def _():
            l_ref[...] = jnp.zeros_like(l_ref)
            dw_ref[...] = jnp.zeros_like(dw_ref)

        val, vjp = jax.vjp(lambda a, b: f(a, b, t_ref[...]), x_ref[...], w_ref[...])
        dx, dw = vjp(jnp.ones((1, 1), F32))
        l_ref[...] += jnp.broadcast_to(val, l_ref.shape)
        dx_ref[...] = dx
        dxb_ref[...] = dx.astype(dxb_ref.dtype)
        dw_ref[...] += dw

    row = pl.BlockSpec((tb, d), lambda i: (i, 0))
    return pl.pallas_call(
        body, name=name, grid=(t // tb,),
        in_specs=[row, _full((1, d)), row],
        out_specs=[_full((SUBLANE, LANE)), row, row, _full((1, d))],
        out_shape=[jax.ShapeDtypeStruct((SUBLANE, LANE), F32), jax.ShapeDtypeStruct((t, d), F32),
                   jax.ShapeDtypeStruct((t, d), BF16),
                   jax.ShapeDtypeStruct((1, d), F32)],
        compiler_params=_cparams(("arbitrary",)),
    )(x, w, target)


def _gla_chunk(q, k, v, g, glr, st, wg, bg, wn, tri):
    L, hk = q.shape[-2:]
    la = _log_sigmoid(mm(glr, wg) + bg) / GLA_GATE_NORM
    bcum = sel_l(jnp.broadcast_to(tri, la.shape[:-2] + tri.shape), la)
    b_last = jnp.sum(la, axis=-2, keepdims=True)
    rows = lax.broadcasted_iota(jnp.int32, (L, 1), 0)
    b_mid = jnp.sum(jnp.where(rows <= L // 2, la, 0.0), axis=-2, keepdims=True)
    qs = q * (hk ** -0.5)
    q_in = qs * jnp.exp(bcum - b_mid)
    k_in = k * jnp.exp(b_mid - bcum)
    scores = mm_nt(q_in, k_in) * tri.astype(F32)
    o_intra = mm(scores, v)
    k_st = k * jnp.exp(b_last - bcum)
    d_st = mm_tn(v, k_st)
    o_inter = mm_nt(qs * jnp.exp(bcum), st)
    st_new = jnp.exp(b_last) * st + d_st
    o = _rms(o_intra + o_inter, wn) * _silu(g)
    return o, st_new


def _heads(ref, start, width, n):
    return jnp.stack([ref[:, start + h * width:start + (h + 1) * width] for h in range(n)], axis=0)


def _gla_heads(p_ref, dk, dv, n):
    hk, hv = dk // n, dv // n
    return (_heads(p_ref, 0, hk, n), _heads(p_ref, dk, hk, n), _heads(p_ref, 2 * dk, hv, n),
            _heads(p_ref, 2 * dk + dv, hv, n))


def _gla_dims(d):
    dv = d // 2
    dk = dv // 2
    return dk, dv, dk // GLA_HEADS, dv // GLA_HEADS


def gla_fwd(proj, glr_col, wg, bg, wn, dv, carry=None):
    t = proj.shape[0]
    dk, dv, hk, hv = _gla_dims(2 * dv)
    L, H = CHUNK, GLA_HEADS
    nc = t // L
    wq = 2 * dk + 2 * dv

    def body(p_ref, glr_ref, wg_ref, bg_ref, wn_ref, o_ref, sp_ref, st):
        @pl.when(pl.program_id(0) == 0)
        def _():
            st[...] = jnp.zeros_like(st)

        s_prev = st[...]
        sp_ref[0] = s_prev
        o, s_new = _gla_chunk(*_gla_heads(p_ref, dk, dv, H), jnp.broadcast_to(glr_ref[...], (H, L, LANE)), s_prev,
                              _heads(wg_ref, 0, hk, H), _heads(bg_ref, 0, hk, H), wn_ref[...], _tri(L))
        for h in range(H):
            o_ref[:, h * hv:(h + 1) * hv] = o[h].astype(o_ref.dtype)
        st[...] = s_new

    return _pcall(
        body, carry=carry, name="gla_fwd", grid=(nc,),
        in_specs=[pl.BlockSpec((L, wq), lambda c: (c, 0)), pl.BlockSpec((L, LANE), lambda c: (c, glr_col)),
                  _full(wg.shape), _full(bg.shape), _full(wn.shape)],
        out_specs=[pl.BlockSpec((L, dv), lambda c: (c, 0)), pl.BlockSpec((1, H, hv, hk), lambda c: (c, 0, 0, 0))],
        out_shape=[jax.ShapeDtypeStruct((t, dv), BF16), jax.ShapeDtypeStruct((nc, H, hv, hk), F32)],
        scratch_shapes=[pltpu.VMEM((H, hv, hk), F32)],
        compiler_params=_cparams(("arbitrary",)),
    )(proj, proj, wg, bg, wn)


def gla_bwd(proj, glr_col, wg, bg, wn, sprev, do, dv, gla_w, carry=None):
    t = proj.shape[0]
    dk, _, hk, hv = _gla_dims(2 * dv)
    L, H = CHUNK, GLA_HEADS
    nc = t // L
    wq = 2 * dk + 2 * dv

    def body(p_ref, glr_ref, wg_ref, bg_ref, wn_ref, sp_ref, do_ref, dp_ref, dwg_ref, dbg_ref, dwn_ref, dst):
        @pl.when(pl.program_id(0) == 0)
        def _():
            dst[...] = jnp.zeros_like(dst)
            dwg_ref[...] = jnp.zeros_like(dwg_ref)
            dbg_ref[...] = jnp.zeros_like(dbg_ref)
            dwn_ref[...] = jnp.zeros_like(dwn_ref)

        f = functools.partial(_gla_chunk, tri=_tri(L))
        _, vjp = jax.vjp(f, *_gla_heads(p_ref, dk, dv, H), jnp.broadcast_to(glr_ref[...], (H, L, LANE)), sp_ref[0],
                         _heads(wg_ref, 0, hk, H), _heads(bg_ref, 0, hk, H), wn_ref[...])
        dq, dkk, dvv, dg, dgl, ds, dwg, dbg, dwn = vjp((_heads(do_ref, 0, hv, H), dst[...]))
        for h in range(H):
            dp_ref[:, h * hk:(h + 1) * hk] = dq[h].astype(dp_ref.dtype)
            dp_ref[:, dk + h * hk:dk + (h + 1) * hk] = dkk[h].astype(dp_ref.dtype)
            dp_ref[:, 2 * dk + h * hv:2 * dk + (h + 1) * hv] = dvv[h].astype(dp_ref.dtype)
            dp_ref[:, 2 * dk + dv + h * hv:2 * dk + dv + (h + 1) * hv] = dg[h].astype(dp_ref.dtype)
            dwg_ref[:, h * hk:(h + 1) * hk] += dwg[h]
            dbg_ref[:, h * hk:(h + 1) * hk] += dbg[h]
        dst[...] = ds
        dwn_ref[...] += dwn
        dp_ref[:, wq:wq + LANE] = jnp.sum(dgl, axis=0).astype(dp_ref.dtype)
        if gla_w > wq + LANE:
            dp_ref[:, wq + LANE:] = jnp.zeros((L, gla_w - wq - LANE), dp_ref.dtype)

    rev = lambda c: nc - 1 - c
    return _pcall(
        body, carry=carry, name="gla_bwd", grid=(nc,),
        in_specs=[pl.BlockSpec((L, wq), lambda c: (rev(c), 0)), pl.BlockSpec((L, LANE), lambda c: (rev(c), glr_col)),
                  _full(wg.shape), _full(bg.shape), _full(wn.shape),
                  pl.BlockSpec((1, H, hv, hk), lambda c: (rev(c), 0, 0, 0)),
                  pl.BlockSpec((L, dv), lambda c: (rev(c), 0))],
        out_specs=[pl.BlockSpec((L, gla_w), lambda c: (rev(c), 0)),
                   _full(wg.shape), _full(bg.shape), _full(wn.shape)],
        out_shape=[jax.ShapeDtypeStruct(proj.shape, BF16),
                   jax.ShapeDtypeStruct(wg.shape, F32), jax.ShapeDtypeStruct(bg.shape, F32),
                   jax.ShapeDtypeStruct(wn.shape, F32)],
        scratch_shapes=[pltpu.VMEM((H, hv, hk), F32)],
        compiler_params=_cparams(("arbitrary",)),
    )(proj, proj, wg, bg, wn, sprev, do)


def _shift_down(x, tail, s):
    if s == 0:
        return x
    r = pltpu.roll(x, s, 0)
    rows = lax.broadcasted_iota(jnp.int32, tail.shape, 0)
    top = jnp.where(rows < s, pltpu.roll(tail, s, 0), r[:SUBLANE])
    return jnp.concatenate([top, r[SUBLANE:]], axis=0)


def _shift_up(x, head, s):
    if s == 0:
        return x
    n = x.shape[0]
    r = pltpu.roll(x, n - s, 0)
    rows = lax.broadcasted_iota(jnp.int32, head.shape, 0)
    bottom = jnp.where(rows >= SUBLANE - s, pltpu.roll(head, SUBLANE - s, 0), r[n - SUBLANE:])
    return jnp.concatenate([r[:n - SUBLANE], bottom], axis=0)


def _conv(x, prev, w, b):
    y = b
    for k in range(CONV_WIDTH):
        y = y + w[k:k + 1, :] * _shift_down(x, prev, CONV_WIDTH - 1 - k)
    return y


def _conv_bwd(dy, nxt, x, prev, w):
    dx = None
    dws = []
    for k in range(CONV_WIDTH):
        s = CONV_WIDTH - 1 - k
        term = w[k:k + 1, :] * _shift_up(dy, nxt, s)
        dx = term if dx is None else dx + term
        dws.append(jnp.sum(dy * _shift_down(x, prev, s), axis=0, keepdims=True))
    return dx, jnp.concatenate(dws, axis=0), jnp.sum(dy, axis=0, keepdims=True)


def _scan_fwd(a, u):
    n = a.shape[0]
    rows = lax.broadcasted_iota(jnp.int32, a.shape, 0)
    s = 1
    while s < n:
        a_sh = jnp.where(rows < s, 1.0, pltpu.roll(a, s, 0))
        u_sh = jnp.where(rows < s, 0.0, pltpu.roll(u, s, 0))
        u = a * u_sh + u
        a = a * a_sh
        s *= 2
    return a, u


def _scan_rev(c, d):
    n = c.shape[0]
    rows = lax.broadcasted_iota(jnp.int32, c.shape, 0)
    s = 1
    while s < n:
        c_sh = jnp.where(rows >= n - s, 0.0, pltpu.roll(c, n - s, 0))
        d_sh = jnp.where(rows >= n - s, 0.0, pltpu.roll(d, n - s, 0))
        d = d + c * d_sh
        c = c * c_sh
        s *= 2
    return d


def _expm1(x):
    small = x * (1.0 + x * (0.5 + x * (1.0 / 6.0 + x * (1.0 / 24.0))))
    return jnp.where(jnp.abs(x) < 1e-2, small, jnp.exp(x) - 1.0)


def _lru_gates(xc, pa, pi, lam):
    r = _sigmoid(pa)
    i = _sigmoid(pi)
    log_a = LRU_C * r * _log_sigmoid(lam)
    a = jnp.exp(log_a)
    u = jnp.sqrt(-_expm1(2.0 * log_a)) * (i * xc)
    return a, u


def _lru_out(h, gate):
    return h * _gelu_tanh(gate)


def _blockdiag(xc, w_ref, b):
    nb = w_ref.shape[0]
    outs = [mm(xc[:, n * LRU_BLOCK:(n + 1) * LRU_BLOCK], w_ref[n]) for n in range(nb)]
    return jnp.concatenate(outs, axis=1) + b


def lru_fwd(proj, xcol, lw, cw, cb, wa, ba, wi, bi, lam):
    t = proj.shape[0]
    tb = _pick(t, (256, 128, 64))
    nb = t // tb

    def body(x_ref, xp_ref, g_ref, cw_ref, cb_ref, wa_ref, ba_ref, wi_ref, bi_ref, lam_ref, o_ref, hin_ref, hc):
        i = pl.program_id(0)

        @pl.when(i == 0)
        def _():
            hc[...] = jnp.zeros_like(hc)

        prev = jnp.where(i == 0, 0.0, xp_ref[...])
        xc = _conv(x_ref[...], prev, cw_ref[...], cb_ref[...])
        a, u = _lru_gates(xc, _blockdiag(xc, wa_ref, ba_ref[...]), _blockdiag(xc, wi_ref, bi_ref[...]), lam_ref[...])
        acum, h0 = _scan_fwd(a, u)
        h = h0 + acum * hc[...]
        hin_ref[0] = hc[...]
        hc[...] = h[tb - 1:tb, :]
        o_ref[...] = _lru_out(h, g_ref[...]).astype(o_ref.dtype)

    row = lambda col: pl.BlockSpec((tb, lw), lambda i: (i, col))
    return pl.pallas_call(
        body, name="lru_fwd", grid=(nb,),
        in_specs=[row(xcol), pl.BlockSpec((SUBLANE, lw), lambda i: (jnp.maximum(i * (tb // SUBLANE) - 1, 0), xcol)),
                  row(xcol + 1),
                  _full(cw.shape), _full(cb.shape), _full(wa.shape), _full(ba.shape), _full(wi.shape), _full(bi.shape),
                  _full(lam.shape)],
        out_specs=[pl.BlockSpec((tb, lw), lambda i: (i, 0)), pl.BlockSpec((1, 1, lw), lambda i: (i, 0, 0))],
        out_shape=[jax.ShapeDtypeStruct((t, lw), BF16), jax.ShapeDtypeStruct((nb, 1, lw), F32)],
        scratch_shapes=[pltpu.VMEM((1, lw), F32)],
        compiler_params=_cparams(("arbitrary",)),
    )(proj, proj, proj, cw, cb, wa, ba, wi, bi, lam)


def lru_bwd(proj, xcol, lw, cw, cb, wa, ba, wi, bi, lam, hin, dmix, docol, dproj, carry=None):
    t = proj.shape[0]
    tb = _pick(t, (256, 128, 64))
    nb = t // tb
    nblk = wa.shape[0]
    assert xcol % 2 == 0

    def body(x_ref, xp_ref, g_ref, cw_ref, cb_ref, wa_ref, ba_ref, wi_ref, bi_ref, lam_ref, hin_ref, do_ref, _,
             dxg_ref, dcw_ref, dcb_ref, dwa_ref, dba_ref, dwi_ref, dbi_ref, dlam_ref, gc, dxcn):
        pid = pl.program_id(0)
        i = nb - 1 - pid

        @pl.when(pid == 0)
        def _():
            gc[...] = jnp.zeros_like(gc)
            dxcn[...] = jnp.zeros_like(dxcn)
            for r in (dcw_ref, dcb_ref, dwa_ref, dba_ref, dwi_ref, dbi_ref, dlam_ref):
                r[...] = jnp.zeros_like(r)

        x = x_ref[...]
        prev = jnp.where(i == 0, 0.0, xp_ref[...])
        cw_v = cw_ref[...]
        xc = _conv(x, prev, cw_v, cb_ref[...])
        pa = _blockdiag(xc, wa_ref, ba_ref[...])
        pi = _blockdiag(xc, wi_ref, bi_ref[...])
        (a, u), vjp_g = jax.vjp(_lru_gates, xc, pa, pi, lam_ref[...])
        acum, h0 = _scan_fwd(a, u)
        hi = hin_ref[0]
        h = h0 + acum * hi
        rows = lax.broadcasted_iota(jnp.int32, h.shape, 0)
        hprev = jnp.where(rows < 1, hi, pltpu.roll(h, 1, 0))
        _, vjp_o = jax.vjp(_lru_out, h, g_ref[...])
        dh, dgate = vjp_o(do_ref[...].astype(F32))
        c = jnp.where(rows >= tb - 1, 0.0, pltpu.roll(a, tb - 1, 0))
        g = _scan_rev(c, dh + jnp.where(rows == tb - 1, gc[...], 0.0))
        gc[...] = a[0:1, :] * g[0:1, :]
        dxc, dpa, dpi, dlam = vjp_g((g * hprev, g))
        dlam_ref[...] += dlam
        dba_ref[...] += jnp.sum(dpa, axis=0, keepdims=True)
        dbi_ref[...] += jnp.sum(dpi, axis=0, keepdims=True)
        parts = []
        for n in range(nblk):
            sl = slice(n * LRU_BLOCK, (n + 1) * LRU_BLOCK)
            dwa_ref[n] += mm_tn(xc[:, sl], dpa[:, sl])
            dwi_ref[n] += mm_tn(xc[:, sl], dpi[:, sl])
            parts.append(mm_nt(dpa[:, sl], wa_ref[n]) + mm_nt(dpi[:, sl], wi_ref[n]))
        dxc = dxc + jnp.concatenate(parts, axis=1)
        dx, dcw, dcb = _conv_bwd(dxc, dxcn[...], x, prev, cw_v)
        dxcn[...] = dxc[:SUBLANE]
        dcw_ref[...] += dcw
        dcb_ref[...] += dcb
        dxg_ref[:, :lw] = dx.astype(dxg_ref.dtype)
        dxg_ref[:, lw:] = dgate.astype(dxg_ref.dtype)

    row = lambda col: pl.BlockSpec((tb, lw), lambda p: (nb - 1 - p, col))
    params = [cw, cb, wa, ba, wi, bi, lam]
    return _pcall(
        body, carry=carry, name="lru_bwd", grid=(nb,),
        in_specs=[row(xcol),
                  pl.BlockSpec((SUBLANE, lw), lambda p: (jnp.maximum((nb - 1 - p) * (tb // SUBLANE) - 1, 0), xcol)),
                  row(xcol + 1)]
        + [_full(p.shape) for p in params]
        + [pl.BlockSpec((1, 1, lw), lambda p: (nb - 1 - p, 0, 0)), row(docol), ANY],
        out_specs=[pl.BlockSpec((tb, 2 * lw), lambda p: (nb - 1 - p, xcol // 2))] + [_full(p.shape) for p in params],
        out_shape=[jax.ShapeDtypeStruct(dproj.shape, dproj.dtype)]
        + [jax.ShapeDtypeStruct(p.shape, F32) for p in params],
        input_output_aliases={12: 0},
        scratch_shapes=[pltpu.VMEM((1, lw), F32), pltpu.VMEM((SUBLANE, lw), F32)],
        compiler_params=_cparams(("arbitrary",)),
    )(proj, proj, proj, *params, hin, dmix, dproj)


def conv_silu_fwd(proj, col0, width, cw, cb, carry=None):
    t = proj.shape[0]
    tb = _pick(t, (512, 256, 128, 64))
    cbw = _pick(width, (512, 256, 128))
    off = col0 // cbw
    assert col0 % cbw == 0

    def body(x_ref, xp_ref, w_ref, b_ref, o_ref):
        prev = jnp.where(pl.program_id(1) == 0, 0.0, xp_ref[...])
        o_ref[...] = _silu(_conv(x_ref[...], prev, w_ref[...], b_ref[...]))

    return _pcall(
        body, carry=carry, name="conv_silu_fwd", grid=(width // cbw, t // tb),
        in_specs=[pl.BlockSpec((tb, cbw), lambda j, i: (i, off + j)),
                  pl.BlockSpec((SUBLANE, cbw), lambda j, i: (jnp.maximum(i * (tb // SUBLANE) - 1, 0), off + j)),
                  pl.BlockSpec((CONV_WIDTH, cbw), lambda j, i: (0, j)), pl.BlockSpec((1, cbw), lambda j, i: (0, j))],
        out_specs=pl.BlockSpec((tb, cbw), lambda j, i: (i, j)),
        out_shape=jax.ShapeDtypeStruct((t, width), F32),
        compiler_params=_cparams(("parallel", "arbitrary")),
    )(proj, proj, cw, cb)


def conv_silu_bwd(proj, col0, width, cw, cb, dact, dproj, carry=None):
    t = proj.shape[0]
    tb = _pick(t, (512, 256, 128, 64))
    nb = t // tb
    cbw = _pick(width, (512, 256, 128))
    off = col0 // cbw

    def body(x_ref, xp_ref, w_ref, b_ref, d_ref, _, dx_ref, dw_ref, db_ref, nxt):
        pid = pl.program_id(1)
        i = nb - 1 - pid

        @pl.when(pid == 0)
        def _():
            nxt[...] = jnp.zeros_like(nxt)
            dw_ref[...] = jnp.zeros_like(dw_ref)
            db_ref[...] = jnp.zeros_like(db_ref)

        x = x_ref[...]
        prev = jnp.where(i == 0, 0.0, xp_ref[...])
        w = w_ref[...]
        _, vjp = jax.vjp(_silu, _conv(x, prev, w, b_ref[...]))
        (dcv,) = vjp(d_ref[...])
        dx, dw, db = _conv_bwd(dcv, nxt[...], x, prev, w)
        nxt[...] = dcv[:SUBLANE]
        dx_ref[...] = dx.astype(dx_ref.dtype)
        dw_ref[...] += dw
        db_ref[...] += db

    return _pcall(
        body, carry=carry, name="conv_silu_bwd", grid=(width // cbw, nb),
        in_specs=[pl.BlockSpec((tb, cbw), lambda j, p: (nb - 1 - p, off + j)),
                  pl.BlockSpec((SUBLANE, cbw),
                               lambda j, p: (jnp.maximum((nb - 1 - p) * (tb // SUBLANE) - 1, 0), off + j)),
                  pl.BlockSpec((CONV_WIDTH, cbw), lambda j, p: (0, j)), pl.BlockSpec((1, cbw), lambda j, p: (0, j)),
                  pl.BlockSpec((tb, cbw), lambda j, p: (nb - 1 - p, j)), ANY],
        out_specs=[pl.BlockSpec((tb, cbw), lambda j, p: (nb - 1 - p, off + j)),
                   pl.BlockSpec((CONV_WIDTH, cbw), lambda j, p: (0, j)), pl.BlockSpec((1, cbw), lambda j, p: (0, j))],
        out_shape=[jax.ShapeDtypeStruct(dproj.shape, dproj.dtype), jax.ShapeDtypeStruct(cw.shape, F32),
                   jax.ShapeDtypeStruct(cb.shape, F32)],
        scratch_shapes=[pltpu.VMEM((SUBLANE, cbw), F32)],
        input_output_aliases={5: 0},
        compiler_params=_cparams(("parallel", "arbitrary")),
    )(proj, proj, cw, cb, dact, dproj)


def _dt_expand(raw, bias, e):
    return sel_r(_softplus(raw + bias), e)


def dt_fwd(proj, dtcol, bias, e):
    t = proj.shape[0]
    di = e.shape[1]
    tb = _pick(t, (512, 256, 128, 64))

    def body(r_ref, b_ref, e_ref, o_ref):
        o_ref[...] = _dt_expand(r_ref[...], b_ref[...], e_ref[...])

    return pl.pallas_call(
        body, name="dt_fwd", grid=(t // tb,),
        in_specs=[pl.BlockSpec((tb, LANE), lambda i: (i, dtcol)), _full(bias.shape), _full(e.shape)],
        out_specs=pl.BlockSpec((tb, di), lambda i: (i, 0)),
        out_shape=jax.ShapeDtypeStruct((t, di), F32),
        compiler_params=_cparams(("parallel",)),
    )(proj, bias, e)


def dt_bwd(proj, dtcol, bias, e, ddte, dproj):
    t = proj.shape[0]
    di = e.shape[1]
    tb = _pick(t, (512, 256, 128, 64))
    tail = dproj.shape[1] - dtcol * LANE
    assert (dtcol * LANE) % tail == 0

    def body(r_ref, b_ref, e_ref, d_ref, _, dr_ref, db_ref):
        @pl.when(pl.program_id(0) == 0)
        def _():
            db_ref[...] = jnp.zeros_like(db_ref)

        e_v = e_ref[...]
        _, vjp = jax.vjp(lambda r, b: _dt_expand(r, b, e_v), r_ref[...], b_ref[...])
        dr, db = vjp(d_ref[...])
        dr_ref[:, :LANE] = dr.astype(dr_ref.dtype)
        if tail > LANE:
            dr_ref[:, LANE:] = jnp.zeros((tb, tail - LANE), dr_ref.dtype)
        db_ref[...] += db

    return pl.pallas_call(
        body, name="dt_bwd", grid=(t // tb,),
        in_specs=[pl.BlockSpec((tb, LANE), lambda i: (i, dtcol)), _full(bias.shape), _full(e.shape),
                  pl.BlockSpec((tb, di), lambda i: (i, 0)), ANY],
        out_specs=[pl.BlockSpec((tb, tail), lambda i: (i, dtcol * LANE // tail)), _full(bias.shape)],
        out_shape=[jax.ShapeDtypeStruct(dproj.shape, dproj.dtype), jax.ShapeDtypeStruct(bias.shape, F32)],
        input_output_aliases={4: 0},
        compiler_params=_cparams(("arbitrary",)),
    )(proj, bias, e, ddte, dproj)


def head_expand(p, e, *, transpose=False, name):
    di = e.shape[1]

    def body(p_ref, e_ref, o_ref):
        if transpose:
            o_ref[...] = _sel_r_b(e_ref[...], p_ref[...])[0]
        else:
            o_ref[...] = sel_r(p_ref[...], e_ref[...])

    oshape = (SUBLANE, LANE) if transpose else (SUBLANE, di)
    return pl.pallas_call(
        body, name=name, in_specs=[_full(p.shape), _full(e.shape)], out_specs=_full(oshape),
        out_shape=jax.ShapeDtypeStruct(oshape, F32), compiler_params=_cparams(None), grid=(1,),
    )(p, e)


def _ssd_chunk(x, z, bm, cm, dte, st, alog, dskip, gn, tri, cmask, dmask, bd):
    L, gw = x.shape
    reps = gw // L
    a = dte * (-jnp.exp(alog))
    acs = sel_l(tri, a)
    acs_last = jnp.sum(a, axis=0, keepdims=True)
    arow = jnp.sum(acs * dmask, axis=0, keepdims=True)
    dtrow = jnp.sum(dte * dmask, axis=0, keepdims=True)
    cb = mm_nt(cm, jnp.concatenate([bm] * reps, axis=0))
    wts = cb * (jnp.exp(jnp.minimum(acs - arow, 0.0)) * cmask) * dtrow
    xbd = jnp.concatenate([x] * reps, axis=0) * bd
    xw = x * (jnp.exp(acs_last - acs) * dte)
    y = mm(wts, xbd) + mm(cm, st) * jnp.exp(acs) + dskip * x
    st_new = jnp.exp(acs_last) * st + mm_tn(bm, xw)
    return _rms(y * _silu(z), gn), st_new


def _ssd_dims(di):
    gw = di // SSD_GROUPS
    assert CHUNK == SSD_HEAD_DIM and gw % LANE == 0
    return gw, SSD_STATE


def _ssd_masks(gw):
    L = CHUNK
    r = jnp.arange(L)[:, None]
    c = jnp.arange(gw)[None, :]
    cmask = ((c % L) <= r).astype(F32)
    dmask = ((c % L) == r).astype(F32)
    rr = jnp.arange(gw)
    bd = ((rr[:, None] // L) == (rr[None, :] // L)).astype(F32)
    tri = (jnp.arange(L)[None, :] <= jnp.arange(L)[:, None]).astype(BF16)
    return tri, cmask, dmask, bd


def ssd_fwd(xs, proj, dte, alog_e, dskip_e, gn, carry=None):
    t, di = dte.shape
    gw, n = _ssd_dims(di)
    L, G = CHUNK, SSD_GROUPS
    nc = t // L
    masks = _ssd_masks(gw)
    cdim = xs.shape[1]

    def body(x_ref, z_ref, dt_ref, al_ref, ds_ref, gn_ref, tri_ref, cm_ref, dm_ref, bd_ref, y_ref, sp_ref, st):
        @pl.when(pl.program_id(0) == 0)
        def _():
            st[...] = jnp.zeros_like(st)

        for g in range(G):
            ch = slice(g * gw, (g + 1) * gw)
            s_prev = st[g]
            sp_ref[0, g] = s_prev
            y, s_new = _ssd_chunk(x_ref[:, ch], z_ref[:, ch], x_ref[:, di + g * n:di + (g + 1) * n],
                                  x_ref[:, di + (G + g) * n:di + (G + g + 1) * n], dt_ref[:, ch], s_prev,
                                  al_ref[0:1, ch], ds_ref[0:1, ch], gn_ref[:, ch], tri_ref[...], cm_ref[...],
                                  dm_ref[...], bd_ref[...])
            y_ref[:, ch] = y.astype(y_ref.dtype)
            st[g] = s_new

    row = lambda w: pl.BlockSpec((L, w), lambda c: (c, 0))
    return _pcall(
        body, carry=carry, name="ssd_fwd", grid=(nc,),
        in_specs=[row(cdim), row(di), row(di), _full(alog_e.shape), _full(dskip_e.shape), _full(gn.shape)]
        + [_full(m.shape) for m in masks],
        out_specs=[row(di), pl.BlockSpec((1, G, n, gw), lambda c: (c, 0, 0, 0))],
        out_shape=[jax.ShapeDtypeStruct((t, di), BF16), jax.ShapeDtypeStruct((nc, G, n, gw), F32)],
        scratch_shapes=[pltpu.VMEM((G, n, gw), F32)],
        compiler_params=_cparams(("arbitrary",)),
    )(xs, proj, dte, alog_e, dskip_e, gn, *masks)


def ssd_bwd(xs, proj, dte, alog_e, dskip_e, gn, sprev, dy, dproj_shape, carry=None):
    t, di = dte.shape
    gw, n = _ssd_dims(di)
    L, G = CHUNK, SSD_GROUPS
    nc = t // L
    masks = _ssd_masks(gw)
    cdim = xs.shape[1]

    def body(x_ref, z_ref, dt_ref, al_ref, ds_ref, gn_ref, tri_ref, cm_ref, dm_ref, bd_ref, sp_ref, dy_ref,
             dxs_ref, dz_ref, ddt_ref, dal_ref, dds_ref, dgn_ref, dst):
        @pl.when(pl.program_id(0) == 0)
        def _():
            dst[...] = jnp.zeros_like(dst)
            dal_ref[...] = jnp.zeros_like(dal_ref)
            dds_ref[...] = jnp.zeros_like(dds_ref)
            dgn_ref[...] = jnp.zeros_like(dgn_ref)

        f = functools.partial(_ssd_chunk, tri=tri_ref[...], cmask=cm_ref[...], dmask=dm_ref[...], bd=bd_ref[...])
        for g in range(G):
            ch = slice(g * gw, (g + 1) * gw)
            bs = slice(di + g * n, di + (g + 1) * n)
            cs = slice(di + (G + g) * n, di + (G + g + 1) * n)
            _, vjp = jax.vjp(f, x_ref[:, ch], z_ref[:, ch], x_ref[:, bs], x_ref[:, cs], dt_ref[:, ch], sp_ref[0, g],
                             al_ref[0:1, ch], ds_ref[0:1, ch], gn_ref[:, ch])
            dx, dz, db, dc, ddt, ds, dal, dds, dgn = vjp((dy_ref[:, ch], dst[g]))
            dxs_ref[:, ch] = dx
            dxs_ref[:, bs] = db
            dxs_ref[:, cs] = dc
            dz_ref[:, ch] = dz.astype(dz_ref.dtype)
            ddt_ref[:, ch] = ddt
            dst[g] = ds
            dal_ref[:, ch] += dal
            dds_ref[:, ch] += dds
            dgn_ref[:, ch] += dgn

    row = lambda w: pl.BlockSpec((L, w), lambda c: (nc - 1 - c, 0))
    acc = _full((1, di))
    acc_shape = jax.ShapeDtypeStruct((1, di), F32)
    return _pcall(
        body, carry=carry, name="ssd_bwd", grid=(nc,),
        in_specs=[row(cdim), row(di), row(di), _full(alog_e.shape), _full(dskip_e.shape), _full(gn.shape)]
        + [_full(m.shape) for m in masks]
        + [pl.BlockSpec((1, G, n, gw), lambda c: (nc - 1 - c, 0, 0, 0)), row(di)],
        out_specs=[row(cdim), row(di), row(di), acc, acc, acc],
        out_shape=[jax.ShapeDtypeStruct((t, cdim), F32), jax.ShapeDtypeStruct(dproj_shape, BF16),
                   jax.ShapeDtypeStruct((t, di), F32), acc_shape, acc_shape, acc_shape],
        scratch_shapes=[pltpu.VMEM((G, n, gw), F32)],
        compiler_params=_cparams(("arbitrary",)),
    )(xs, proj, dte, alog_e, dskip_e, gn, *masks, sprev, dy)


def _rows2d(a):
    return a.reshape(-1, a.shape[-1])


def _row_tile(rows, cols):
    cap = max(SUBLANE, (1 << 19) // max(cols, 1))
    step = 2 * SUBLANE
    for c in range(min(cap, rows) // step * step, 0, -step):
        if rows % c == 0:
            return c
    return rows


def chip_sum(g, r, core, *, name):
    shape = r.shape
    cols = shape[-1]
    g4 = g.reshape(4, 2, -1, cols)
    r3 = r.reshape(4, -1, cols)
    rows = r3.shape[1]
    tr = _row_tile(rows, cols)

    def body(c_ref, g_ref, r_ref, o_ref):
        o_ref[...] = (g_ref[...].astype(F32) + r_ref[...].astype(F32)).astype(o_ref.dtype)

    out = pl.pallas_call(
        body, name=name,
        grid_spec=pltpu.PrefetchScalarGridSpec(
            num_scalar_prefetch=1, grid=(4, rows // tr),
            in_specs=[pl.BlockSpec((None, None, tr, cols), lambda j, i, c: (j, c[0], i, 0)),
                      pl.BlockSpec((None, tr, cols), lambda j, i, c: (j, i, 0))],
            out_specs=pl.BlockSpec((None, tr, cols), lambda j, i, c: (j, i, 0))),
        out_shape=jax.ShapeDtypeStruct(r3.shape, BF16), compiler_params=_cparams(("parallel", "parallel")),
    )(core.reshape(1).astype(jnp.int32), g4, r3)
    return out.reshape(shape)


def mesh_sum(p, r, chip, core, *, name):
    shape = p.shape[1:]
    cols = shape[-1]
    p3 = p.reshape(4, -1, cols)
    r3 = r.reshape(3, -1, cols)
    rows = p3.shape[1]
    tr = _row_tile(rows, 2 * cols)

    def body(c_ref, p_ref, r_ref, o_ref):
        o_ref[...] = ((p_ref[...].astype(F32) + r_ref[0].astype(F32)) + r_ref[1].astype(F32)) + r_ref[2].astype(F32)

    out = pl.pallas_call(
        body, name=name,
        grid_spec=pltpu.PrefetchScalarGridSpec(
            num_scalar_prefetch=1, grid=(rows // tr,),
            in_specs=[pl.BlockSpec((None, tr, cols), lambda i, c: (c[0], i, 0)),
                      pl.BlockSpec((3, tr, cols), lambda i, c: (0, i, 0))],
            out_specs=pl.BlockSpec((None, tr, cols), lambda i, c: (c[1], i, 0))),
        out_shape=jax.ShapeDtypeStruct((2, rows, cols), F32), compiler_params=_cparams(("parallel",)),
    )(jnp.stack([chip, core]).astype(jnp.int32), p3, r3)
    return out.reshape((2,) + shape)


def sum_leading(x, *, name):
    k, r, c = x.shape
    tr = _row_tile(r, c * k)
    def body(x_ref, o_ref):
        acc = x_ref[0]
        for i in range(1, k):
            acc = acc + x_ref[i]
        o_ref[...] = acc

    return pl.pallas_call(
        body, name=name, grid=(r // tr,), in_specs=[pl.BlockSpec((k, tr, c), lambda i: (0, i, 0))],
        out_specs=pl.BlockSpec((tr, c), lambda i: (i, 0)),
        out_shape=jax.ShapeDtypeStruct((r, c), F32), compiler_params=_cparams(("parallel",)),
    )(x)


def adamw(w, gs, m, v, *, name):
    shape = w.shape
    w3, m3, v3 = (a.reshape((-1,) + a.shape[-2:]) for a in (w, m, v))
    nl, r, c = w3.shape
    assert len(gs) == nl
    tr = _row_tile(r, 2 * c)
    tc = c
    if tr == r and r * c > (1 << 19):
        tc = next(t for t in (1024, 512, 256, 128) if c % t == 0 and r * t <= (1 << 19))
    c1 = 1.0 - ADAM_B1 ** ADAM_STEP
    c2 = 1.0 - ADAM_B2 ** ADAM_STEP

    def body(w_ref, g_ref, m_ref, v_ref, *rest):
        go_ref, d_ref, mo_ref, vo_ref = rest[-4:]
        gv = g_ref[...]
        mn = ADAM_B1 * m_ref[...] + (1.0 - ADAM_B1) * gv
        vn = ADAM_B2 * v_ref[...] + (1.0 - ADAM_B2) * (gv * gv)
        go_ref[...] = gv
        d_ref[...] = -ADAM_LR * ((mn / c1) / (jnp.sqrt(vn / c2) + ADAM_EPS) + ADAM_WD * w_ref[...])
        mo_ref[...] = mn
        vo_ref[...] = vn

    outs = None
    for l, g in enumerate(gs):
        layer = pl.BlockSpec((None, tr, tc), lambda i, j, l=l: (l, i, j))
        prev = [] if outs is None else list(outs)
        outs = pl.pallas_call(
            functools.partial(body), name=f"{name}_{l}", grid=(r // tr, c // tc),
            in_specs=[layer, pl.BlockSpec((tr, tc), lambda i, j: (i, j)), layer, layer] + [ANY] * len(prev),
            out_specs=[layer] * 4, out_shape=[jax.ShapeDtypeStruct((nl, r, c), F32)] * 4,
            input_output_aliases={4 + k: k for k in range(len(prev))},
            compiler_params=_cparams(("parallel", "parallel")),
        )(w3, g.reshape(r, c), m3, v3, *prev)
    return tuple(o.reshape(shape) for o in outs)


ANY = pl.BlockSpec(memory_space=pl.ANY)


def _place():
    x, y, c = lax.axis_index("x"), lax.axis_index("y"), lax.axis_index("c")
    chips = [(1 - x, y), (x, 1 - y), (1 - x, 1 - y)]
    return x, y, c, chips


def gather8(block):
    m, n = block.shape

    def body(x_ref, out_ref, send_sems, recv_sems, local_sem):
        x, y, c, chips = _place()
        me, sibling = (x, y, c), (x, y, 1 - c)

        def rows(px, py, pc):
            return out_ref.at[4 * px + 2 * py + pc]

        def copy(k, blk, to, src=None):
            return pltpu.make_async_remote_copy(
                src_ref=rows(*blk) if src is None else src, dst_ref=rows(*blk), send_sem=send_sems.at[k],
                recv_sem=recv_sems.at[k], device_id=to, device_id_type=MESH)

        mine = pltpu.make_async_copy(x_ref, rows(*me), local_sem)
        mine.start()
        first = [copy(0, me, sibling, src=x_ref)]
        first += [copy(1 + j, me, (*chip, c), src=x_ref) for j, chip in enumerate(chips)]
        for cp in first:
            cp.start()
        passed = [copy(4 + j, (*chip, c), sibling) for j, chip in enumerate(chips)]
        for j, chip in enumerate(chips):
            copy(1 + j, (*chip, c), me).wait_recv()
            passed[j].start()
        copy(0, sibling, me).wait_recv()
        for j, chip in enumerate(chips):
            copy(4 + j, (*chip, 1 - c), me).wait_recv()
        for cp in first + passed:
            cp.wait_send()
        mine.wait()

    return pl.pallas_call(
        body, name="gather8",
        out_shape=jax.ShapeDtypeStruct((8, m, n), block.dtype),
        in_specs=[pl.BlockSpec(memory_space=pltpu.VMEM)],
        out_specs=pl.BlockSpec(memory_space=pltpu.VMEM),
        scratch_shapes=[pltpu.SemaphoreType.DMA((7,)), pltpu.SemaphoreType.DMA((7,)), pltpu.SemaphoreType.DMA],
        compiler_params=pltpu.CompilerParams(vmem_limit_bytes=VMEM_LIMIT),
    )(block)


def gather_weights(shards):
    n = len(shards)

    def copy(ins, outs, send, recv, base, a, k, chip_idx, half, to, src=None):
        dst = outs[a].at[chip_idx, half]
        return pltpu.make_async_remote_copy(
            src_ref=dst if src is None else src, dst_ref=dst, send_sem=send.at[base + 6 * a + k],
            recv_sem=recv.at[base + 6 * a + k], device_id=to, device_id_type=MESH)

    def first(ins, outs, send, recv, base):
        x, y, c, chips = _place()
        return [copy(ins, outs, send, recv, base, a, j, 2 * x + y, c, (*chip, c), src=ins[a].at[c])
                for a in range(n) for j, chip in enumerate(chips)]

    def start(ins, outs, send, recv, base):
        for cp in first(ins, outs, send, recv, base):
            cp.start()

    def finish(ins, outs, send, recv, base):
        x, y, c, chips = _place()
        sibling = (x, y, 1 - c)
        passed = []
        for a in range(n):
            for j, (cx, cy) in enumerate(chips):
                copy(ins, outs, send, recv, base, a, j, 2 * cx + cy, c, (cx, cy, c)).wait_recv()
                fw = copy(ins, outs, send, recv, base, a, 3 + j, 2 * cx + cy, c, sibling)
                fw.start()
                passed.append(fw)
        for a in range(n):
            for j, (cx, cy) in enumerate(chips):
                copy(ins, outs, send, recv, base, a, 3 + j, 2 * cx + cy, 1 - c, sibling).wait_recv()
        for cp in first(ins, outs, send, recv, base) + passed:
            cp.wait_send()

    return Carry(shards, [jax.ShapeDtypeStruct((4,) + s.shape, s.dtype) for s in shards], 6 * n, start, finish)


def exchange_halves(grads):
    n = len(grads)

    def copies(ins, outs, send, recv, base):
        x, y, c, _ = _place()
        return [pltpu.make_async_remote_copy(
            src_ref=ins[a].at[j, 1 - c], dst_ref=outs[a].at[j], send_sem=send.at[base + 4 * a + j],
            recv_sem=recv.at[base + 4 * a + j], device_id=(x, y, 1 - c), device_id_type=MESH)
            for a in range(n) for j in range(4)]

    def start(*args):
        for cp in copies(*args):
            cp.start()

    def finish(*args):
        for cp in copies(*args):
            cp.wait()

    return Carry(grads, [jax.ShapeDtypeStruct((4,) + g.shape[2:], g.dtype) for g in grads], 4 * n, start, finish)


def scatter_chips(parts, rows=None, into=None):
    n = len(parts)
    sl = (lambda r: r) if rows is None else (lambda r: r.at[pl.ds(rows[0], rows[1])])

    def copies(ins, outs, send, recv, base):
        x, y, c, chips = _place()
        return [pltpu.make_async_remote_copy(
            src_ref=sl(ins[a].at[2 * cx + cy]), dst_ref=sl(outs[a].at[j]), send_sem=send.at[base + 3 * a + j],
            recv_sem=recv.at[base + 3 * a + j], device_id=(cx, cy, c), device_id_type=MESH)
            for a in range(n) for j, (cx, cy) in enumerate(chips)]

    def start(*args):
        for cp in copies(*args):
            cp.start()

    def finish(*args):
        for cp in copies(*args):
            cp.wait()

    shapes = [jax.ShapeDtypeStruct((3,) + p.shape[1:], p.dtype) for p in parts]
    if into is None:
        return Carry(parts, shapes, 3 * n, start, finish)
    return Carry(list(parts) + list(into), shapes, 3 * n, start, finish, aliases={n + a: a for a in range(n)})


def join_halves(bufs):
    n = len(bufs)

    def copy(outs, send, recv, base, a, half):
        x, y, c, _ = _place()
        return pltpu.make_async_remote_copy(
            src_ref=outs[a].at[c], dst_ref=outs[a].at[c if half is None else half], send_sem=send.at[base + a],
            recv_sem=recv.at[base + a], device_id=(x, y, 1 - c), device_id_type=MESH)

    def start(ins, outs, send, recv, base):
        for a in range(n):
            copy(outs, send, recv, base, a, None).start()

    def finish(ins, outs, send, recv, base):
        c = lax.axis_index("c")
        for a in range(n):
            copy(outs, send, recv, base, a, 1 - c).wait_recv()
        for a in range(n):
            copy(outs, send, recv, base, a, None).wait_send()

    return Carry(bufs, [jax.ShapeDtypeStruct(h.shape, h.dtype) for h in bufs], n, start, finish,
                 aliases={a: a for a in range(n)})


def run_comm(carry, *, name):
    k_in, k_out = len(carry.arrays), len(carry.out_shape)

    def body(*refs):
        ins, outs = refs[:k_in], refs[k_in:k_in + k_out]
        send, recv = refs[-2:]
        carry.start(ins, outs, send, recv, 0)
        carry.finish(ins, outs, send, recv, 0)

    return pl.pallas_call(
        body, name=name, out_shape=carry.out_shape, in_specs=[ANY] * k_in, out_specs=[ANY] * k_out,
        scratch_shapes=[pltpu.SemaphoreType.DMA((carry.n_sems,))] * 2, input_output_aliases=carry.aliases,
    )(*carry.arrays)


INPUTS = ['x'] + WEIGHTS + ['loss_target'] + ['m_' + n for n in WEIGHTS] + ['v_' + n for n in WEIGHTS]


def _round_up(n, m):
    return -(-n // m) * m


def _pack(arrs):
    flat = jnp.concatenate([a.reshape(-1) for a in arrs])
    n = _round_up(flat.shape[0], 512 * LANE)
    return jnp.pad(flat, (0, n - flat.shape[0])).reshape(-1, LANE)


def _unpack(block, shapes):
    flat = block.reshape(-1)
    out, o = [], 0
    for s in shapes:
        n = math.prod(s)
        out.append(flat[o:o + n].reshape(s))
        o += n
    return out


def _cols(g):
    return g.transpose(1, 0, 2).reshape(g.shape[1], -1)


def _uncols(w):
    return w.reshape(w.shape[0], 4, -1).transpose(1, 0, 2)


def _pad_cols(w, total):
    return jnp.pad(w, ((0, 0), (0, total - w.shape[1])))


def kernel(*args):
    a = dict(zip(INPUTS, args))
    x, tgt = a['x'][0], a['loss_target'][0]
    t, d = x.shape
    xi, yi, ci = lax.axis_index("x"), lax.axis_index("y"), lax.axis_index("c")
    chip = 2 * xi + yi
    dk, dv, hk, hv = _gla_dims(d)
    lw = d // 2
    di = 2 * d
    nh = di // SSD_HEAD_DIM
    gn_w = SSD_GROUPS * SSD_STATE
    conv_dim = di + 2 * gn_w
    rank = GLA_GATE_RANK
    wq = 2 * dk + 2 * dv
    gla_w = _round_up(wq + LANE, 2 * lw)
    ev_tot = gla_w + 2 * lw
    od_used = di + conv_dim + nh
    od_tot = _round_up(di + conv_dim + _round_up(nh, LANE), 512)
    glr_col, xcol, dtcol = wq // LANE, gla_w // lw, (di + conv_dim) // LANE
    assert ev_tot % 512 == 0 and nh <= LANE

    def halves(w):
        w = w.astype(BF16)
        return w.reshape((2, w.shape[0] // 2) + w.shape[1:])

    own = {'ev_w_in': halves(a['ev_w_in'][0]), 'ev_w_out': halves(a['ev_w_out'][0]),
           'od_w_in_a': halves(a['od_w_in'][0][:d // 2]), 'od_w_in_b': halves(a['od_w_in'][0][d // 2:]),
           'od_w_out': halves(a['od_w_out'][0])}
    for l in range(2):
        own[f'gate{l}'], own[f'up{l}'] = halves(a['ffn_w_gate'][l]), halves(a['ffn_w_up'][l])
        own[f'down{l}'] = halves(a['ffn_w_down'][l])

    def gather(*units):
        return gather_weights([own[u] for u in units])

    def filled(unit, g):
        g = lax.dynamic_update_index_in_dim(g, own[unit], chip, 0)
        return g.reshape((4, 2 * g.shape[2]) + g.shape[3:])

    g_ev_in, g_ev_out = run_comm(gather('ev_w_in', 'ev_w_out'), name="gather_ev")
    w_ev_in = _cols(filled('ev_w_in', g_ev_in))
    cuts = [dk, 2 * dk, 2 * dk + dv, wq, wq + rank, wq + rank + lw]
    sq, sk, sv, sg, sglr, sxb, sgb = jnp.split(w_ev_in, cuts, axis=1)
    w_ev_in_p = jnp.concatenate([_pad_cols(jnp.concatenate([sq, sk, sv, sg, sglr], axis=1), gla_w), sxb, sgb], axis=1)
    w_ev_out = filled('ev_w_out', g_ev_out).reshape(-1, d)
    w_gate, w_up, w_down = [None, None], [None, None], [None, None]

    sh_names = list(SMALL_SHARDED)
    sh_shapes = [a[n].shape for n in sh_names]
    g8 = gather8(_pack([a[n] for n in sh_names]))
    per_chip = [_unpack(g8[2 * j], sh_shapes) for j in range(4)]
    full = {n: jnp.concatenate([per_chip[j][i] for j in range(4)], axis=SMALL_SHARDED[n])
            for i, n in enumerate(sh_names)}

    wg_p = jnp.zeros((LANE, dk), F32).at[:rank].set(full['ev_gla_w_gate'][0])
    bg, wn = a['ev_gla_b_gate'], a['ev_gla_w_onorm']
    lru_p = [full['ev_lru_conv_w'][0], a['ev_lru_conv_b'], a['ev_lru_w_a'][0], a['ev_lru_b_a'], a['ev_lru_w_i'][0],
             a['ev_lru_b_i'], a['ev_lru_lam']]
    od_cw, od_cb, od_gn = full['od_conv_w'][0], full['od_conv_b'], full['od_gnorm']
    heads = jnp.arange(LANE)[:, None]
    e_mat = ((jnp.arange(di)[None, :] // SSD_HEAD_DIM == heads) & (heads < nh)).astype(BF16)
    row8 = lambda p: jnp.zeros((SUBLANE, LANE), F32).at[0, :nh].set(p[0])
    dt_bias_p = jnp.zeros((1, LANE), F32).at[0, :nh].set(a['od_dt_bias'][0])
    alog_e = head_expand(row8(a['od_a_log']), e_mat, name="expand_a_log")
    dskip_e = head_expand(row8(a['od_d_skip']), e_mat, name="expand_d_skip")

    h0 = rms_fwd(x, a['ev_norm'], name="rms_ev")
    proj, (g,) = matmul(h0, w_ev_in_p, name="ev_in", carry=gather('gate0'))
    w_gate[0] = filled('gate0', g)
    (o_gla, sp_gla), (g,) = gla_fwd(proj, glr_col, wg_p, bg, wn, dv, carry=gather('up0'))
    w_up[0] = filled('up0', g)
    o_lru, hin = lru_fwd(proj, xcol, lw, *lru_p)
    x1 = matmul(o_gla, w_ev_out[:dv], add=x, name="ev_out_a")
    x1 = matmul(o_lru, w_ev_out[dv:], add=x1, name="ev_out_b")

    h1 = rms_fwd(x1, a['ffn_norm'][0:1], name="rms_ffn0")
    gate0, (g,) = matmul(h1, w_gate[0], name="ffn0_gate", carry=gather('down0'))
    w_down[0] = filled('down0', g).reshape(-1, d)
    swi = dict(epi=_swi_fwd_epi, epi_out=(F32, BF16))
    (up0, act0), (g_a,) = matmul(h1, w_up[0], name="ffn0_up", epi_in=(gate0,), carry=gather('od_w_in_a'), **swi)
    x2, (g_b,) = matmul(act0, w_down[0], add=x1, name="ffn0_down", carry=gather('od_w_in_b'))
    w_od_in = jnp.concatenate([filled('od_w_in_a', g_a), filled('od_w_in_b', g_b)], axis=1)
    w_od_in_p = _pad_cols(_cols(w_od_in), od_tot)

    h2 = rms_fwd(x2, full['od_norm'], name="rms_od")
    proj2, (g, g1) = matmul(h2, w_od_in_p, name="od_in", carry=gather('od_w_out', 'gate1'))
    w_od_out, w_gate[1] = filled('od_w_out', g).reshape(-1, d), filled('gate1', g1)
    xs, (g,) = conv_silu_fwd(proj2, di, conv_dim, od_cw, od_cb, carry=gather('up1'))
    w_up[1] = filled('up1', g)
    dte = dt_fwd(proj2, dtcol, dt_bias_p, e_mat)
    (y_ssd, sp_ssd), (g,) = ssd_fwd(xs, proj2, dte, alog_e, dskip_e, od_gn, carry=gather('down1'))
    w_down[1] = filled('down1', g).reshape(-1, d)
    x3 = matmul(y_ssd, w_od_out, add=x2, name="od_out")
    h3 = rms_fwd(x3, a['ffn_norm'][1:2], name="rms_ffn1")
    gate1 = matmul(h3, w_gate[1], name="ffn1_gate")
    up1, act1 = matmul(h3, w_up[1], name="ffn1_up", epi_in=(gate1,), **swi)
    x4 = matmul(act1, w_down[1], add=x3, name="ffn1_down")
    loss_p, dx4, dx4b, d_final = loss_head(x4, a['final_norm'][None], tgt, name="loss_head")

    grads, from_sib, part, from_chips = {}, {}, {}, {}

    def rows4(dw):
        return dw.reshape((4, 2, dw.shape[0] // 8) + dw.shape[1:])

    def cols4(dw):
        return dw.reshape((4, 2, dw.shape[1] // 2) + dw.shape[2:])

    def exchange(*units):
        return exchange_halves([grads[u] for u in units])

    def scatter(*units):
        return scatter_chips([part[u] for u in units])

    def sum_chip(u):
        part[u] = chip_sum(grads[u], from_sib[u], ci, name=f"chip_sum_{u}")

    def ffn_bwd(dxo, dxob, xin, h, gate, up, act, l, first_carry, first_units):
        dn, gt, up_ = f'down{l}', f'gate{l}', f'up{l}'
        dgu = matmul(dxob, w_down[l], tb=True, name=f"ffn{l}_d_act", carry=first_carry, epi=_swi_bwd_epi,
                     epi_in=(gate, up), epi_out=(BF16, BF16))
        (dg, du), got = dgu if first_units else (dgu, ())
        for u, r in zip(first_units, got):
            from_sib[u] = r
            sum_chip(u)
        piece = [part[u].shape[1] // 2 for u in first_units]
        d_down = matmul(act, dxob, ta=True, out_dtype=BF16, name=f"ffn{l}_dw_down",
                        carry=scatter_chips([part[u] for u in first_units], rows=(0, piece[0])) if first_units else None)
        if first_units:
            d_down, first_rb = d_down
        grads[dn] = rows4(d_down)
        second = scatter_chips([part[u] for u in first_units], rows=(piece[0], piece[0]), into=first_rb) \
            if first_units else None
        dh, got = matmul_pair(dg, w_gate[l], du, w_up[l], name=f"ffn{l}_dh", carry=merge_carries(exchange(dn), second))
        from_sib[dn] = got[0]
        if first_units:
            from_chips[first_units[0]] = got[1]
        sum_chip(dn)
        d_gate, (from_chips[dn],) = matmul(h, dg, ta=True, out_dtype=BF16, out_shards=True, name=f"ffn{l}_dw_gate",
                                           carry=scatter(dn))
        grads[gt] = cols4(d_gate)
        d_up, (from_sib[gt],) = matmul(h, du, ta=True, out_dtype=BF16, out_shards=True, name=f"ffn{l}_dw_up",
                                       carry=exchange(gt))
        grads[up_] = cols4(d_up)
        return rms_bwd(xin, a['ffn_norm'][l:l + 1], dh, dxo, name=f"rms_ffn{l}_bwd")

    dx3, dx3b, d_fn1 = ffn_bwd(dx4, dx4b, x3, h3, gate1, up1, act1, 1, None, ())
    dy, (from_sib['up1'],) = matmul(dx3b, w_od_out, tb=True, name="od_out_dy", carry=exchange('up1'))
    sum_chip('gate1')
    sum_chip('up1')
    grads['od_w_out'] = rows4(matmul(y_ssd, dx3b, ta=True, out_dtype=BF16, name="od_out_dw"))
    (dxs, dproj2, ddte, dal, dds, dgn), (from_chips['gate1'], from_chips['up1'], from_sib['od_w_out']) = ssd_bwd(
        xs, proj2, dte, alog_e, dskip_e, od_gn, sp_ssd, dy, proj2.shape,
        carry=merge_carries(scatter('gate1', 'up1'), exchange('od_w_out')))
    sum_chip('od_w_out')
    (dproj2, d_od_cw, d_od_cb), (from_chips['od_w_out'],) = conv_silu_bwd(
        proj2, di, conv_dim, od_cw, od_cb, dxs, dproj2, carry=scatter('od_w_out'))
    dproj2, d_dt_bias = dt_bwd(proj2, dtcol, dt_bias_p, e_mat, ddte, dproj2)
    dh2 = matmul(dproj2, w_od_in_p, tb=True, name="od_in_dh")
    d_od_in = matmul(h2, dproj2, ta=True, out_dtype=BF16, name="od_in_dw")[:, :od_used]
    grads['od_w_in'] = cols4(_uncols(d_od_in))
    dx2, dx2b, d_od_norm = rms_bwd(x2, full['od_norm'], dh2, dx3, name="rms_od_bwd")
    to8 = lambda acc: jnp.zeros((SUBLANE, di), F32).at[0].set(acc.reshape(-1))
    d_a_log = head_expand(to8(dal), e_mat, transpose=True, name="reduce_a_log")[0:1, :nh]
    d_d_skip = head_expand(to8(dds), e_mat, transpose=True, name="reduce_d_skip")[0:1, :nh]

    dx1, dx1b, d_fn0 = ffn_bwd(dx2, dx2b, x1, h1, gate0, up0, act0, 0, exchange('od_w_in'), ('od_w_in',))
    dmix, (from_sib['up0'],) = matmul(dx1b, w_ev_out, tb=True, name="ev_out_dmix", carry=exchange('up0'))
    sum_chip('gate0')
    sum_chip('up0')
    grads['ev_w_out'] = rows4(jnp.concatenate([matmul(o_gla, dx1b, ta=True, out_dtype=BF16, name="ev_out_dw_a"),
                                               matmul(o_lru, dx1b, ta=True, out_dtype=BF16, name="ev_out_dw_b")], axis=0))
    (dproj, d_wg, d_bg, d_wn), (from_sib['ev_w_out'],) = gla_bwd(
        proj, glr_col, wg_p, bg, wn, sp_gla, dmix, dv, gla_w, carry=exchange('ev_w_out'))
    sum_chip('ev_w_out')
    (dproj, *d_lru), (from_chips['ev_w_out'], from_chips['gate0']) = lru_bwd(
        proj, xcol, lw, *lru_p, hin, dmix, 1, dproj, carry=scatter('ev_w_out', 'gate0'))
    d_ev_in_p, (from_chips['up0'],) = matmul(h0, dproj, ta=True, out_dtype=BF16, name="ev_in_dw", carry=scatter('up0'))
    d_ev_in = jnp.concatenate([d_ev_in_p[:, :wq + rank], d_ev_in_p[:, gla_w:]], axis=1)
    grads['ev_w_in'] = cols4(_uncols(d_ev_in))
    (from_sib['ev_w_in'],) = run_comm(exchange('ev_w_in'), name="exchange_ev_in")
    sum_chip('ev_w_in')
    early = [u for u in grads if u != 'ev_w_in']
    half = [mesh_sum(part[u], from_chips[u], chip, ci, name=f"mesh_sum_{u}") for u in early]
    dh0, got = matmul(dproj, w_ev_in_p, tb=True, name="ev_in_dh",
                      carry=merge_carries(scatter('ev_w_in'), join_halves(half)))
    from_chips['ev_w_in'], done = got[0], dict(zip(early, got[1:]))
    dx0, _, d_ev_norm = rms_bwd(x, a['ev_norm'], dh0, dx1, name="rms_ev_bwd")
    last_half = mesh_sum(part['ev_w_in'], from_chips['ev_w_in'], chip, ci, name="mesh_sum_ev_w_in")
    (done['ev_w_in'],) = run_comm(join_halves([last_half]), name="join_halves")
    layers = {n: [done[n]] for n in ('ev_w_in', 'ev_w_out', 'od_w_in', 'od_w_out')}
    layers.update({f'ffn_w_{u}': [done[f'{u}0'], done[f'{u}1']] for u in ('gate', 'up', 'down')})
    grad, delta, new_m, new_v = {}, {}, {}, {}
    for n in BIG:
        flip = a[n].shape[-1] % LANE != 0
        tr_ = (lambda t: jnp.swapaxes(t, -1, -2)) if flip else (lambda t: t)
        gs = [tr_(g.reshape(a[n].shape[-2:])) for g in layers[n]]
        outs = adamw(tr_(a[n]), gs, tr_(a['m_' + n]), tr_(a['v_' + n]), name=f"adamw_{n}")
        grad[n], delta[n], new_m[n], new_v[n] = (tr_(o) for o in outs)

    small_g = {
        'ev_norm': d_ev_norm, 'ev_gla_w_gate': d_wg[:rank][None], 'ev_gla_b_gate': d_bg, 'ev_gla_w_onorm': d_wn,
        'ev_lru_conv_w': d_lru[0][None], 'ev_lru_conv_b': d_lru[1], 'ev_lru_w_a': d_lru[2][None],
        'ev_lru_b_a': d_lru[3], 'ev_lru_w_i': d_lru[4][None], 'ev_lru_b_i': d_lru[5], 'ev_lru_lam': d_lru[6],
        'od_norm': d_od_norm, 'od_conv_w': d_od_cw[None], 'od_conv_b': d_od_cb, 'od_dt_bias': d_dt_bias[:, :nh],
        'od_a_log': d_a_log, 'od_d_skip': d_d_skip, 'od_gnorm': dgn.reshape(1, di),
        'ffn_norm': jnp.concatenate([d_fn0, d_fn1], axis=0), 'final_norm': d_final[0],
    }
    full_shapes = [small_g[n].shape for n in SMALL]
    summed = sum_leading(gather8(_pack([small_g[n] for n in SMALL])), name="sum_devices")
    for n, g in zip(SMALL, _unpack(summed, full_shapes)):
        if n in SMALL_SHARDED:
            ax = SMALL_SHARDED[n]
            sz = a[n].shape[ax]
            g = lax.dynamic_slice_in_dim(g, chip * sz, sz, axis=ax)
        grad[n] = g

    shapes = [a[n].shape for n in SMALL]
    packed = [_pack([src[n] if pre is None else a[pre + n] for n in SMALL])
              for src, pre in ((a, None), (grad, None), (None, 'm_'), (None, 'v_'))]
    small_out = adamw(packed[0], [packed[1]], packed[2], packed[3], name="adamw_small")
    for outd, blk in zip((delta, new_m, new_v), small_out[1:]):
        outd.update(zip(SMALL, _unpack(blk, shapes)))

    loss = lax.psum(loss_p[0, 0], ("x", "y", "c"))
    return (loss, dx0[None], *[grad[n] for n in WEIGHTS], *[delta[n] for n in WEIGHTS],
            *[new_m[n] for n in WEIGHTS], *[new_v[n] for n in WEIGHTS])
```
